```python
import jax
import jax.numpy as jnp
from jax import lax
import numpy as np

D_MODEL = 1024
BATCH = 8
SEQ = 4096
DEPTH = 4

N_MIXERS = 3
HEAD_DIM = 64
BLOCK_Q = 128
GRID_W = 64
RMS_EPS = 1e-6
D_FF = 4 * D_MODEL
A_HEADS = D_MODEL // HEAD_DIM
A_KV_HEADS = A_HEADS // 4
ROPE_THETA = 10000.0
B_GROUPS = ((128, 1), (512, 4), (2048, 16))
B_HEADS_PER_GROUP = 6
B_KV_PER_GROUP = 2
C_HEADS = D_MODEL // HEAD_DIM
C_KV_HEADS = C_HEADS // 4
C_WINDOW = 128

kernel_name = "hybrid_interleaved_bidir_encoder"


def rmsnorm(x, gain):
    xf = x.astype(jnp.float32)
    y = xf * lax.rsqrt(jnp.mean(xf * xf, axis=-1, keepdims=True) + RMS_EPS)
    return (y * gain.astype(jnp.float32)).astype(x.dtype)


def alibi_slopes(n_heads):
    return 2.0 ** (-8.0 * jnp.arange(1, n_heads + 1, dtype=jnp.float32) / n_heads)


def stack_blocks(y, batch, seq):
    y = jnp.moveaxis(y, 0, 1)
    return y.reshape((batch, seq) + y.shape[3:])


def axial_rope_angles(seq):
    rows = seq // GRID_W
    row = jnp.repeat(jnp.arange(rows, dtype=jnp.float32), GRID_W)
    col = jnp.tile(jnp.arange(GRID_W, dtype=jnp.float32), rows)
    axis_dim = HEAD_DIM // 2
    inv_freq = ROPE_THETA ** (-jnp.arange(0, axis_dim, 2, dtype=jnp.float32) / axis_dim)
    return row[:, None] * inv_freq, col[:, None] * inv_freq


def rotate(x, ang):
    shape = (ang.shape[0],) + (1,) * (x.ndim - 3) + (ang.shape[1],)
    cos = jnp.cos(ang).reshape(shape).astype(x.dtype)
    sin = jnp.sin(ang).reshape(shape).astype(x.dtype)
    x1, x2 = jnp.split(x, 2, axis=-1)
    return jnp.concatenate([x1 * cos - x2 * sin, x2 * cos + x1 * sin], axis=-1)


def axial_rope(x, ang_row, ang_col):
    half = HEAD_DIM // 2
    return jnp.concatenate([rotate(x[..., :half], ang_row), rotate(x[..., half:], ang_col)], axis=-1)


def mixer_a(h, w_qkv, q_gain, k_gain, w_o):
    b, s, _ = h.shape
    rep = A_HEADS // A_KV_HEADS
    qkv = h @ w_qkv
    q, k, v = jnp.split(qkv, [A_HEADS * HEAD_DIM, (A_HEADS + A_KV_HEADS) * HEAD_DIM], axis=-1)
    q = rmsnorm(q.reshape(b, s, A_KV_HEADS, rep, HEAD_DIM), q_gain)
    k = rmsnorm(k.reshape(b, s, A_KV_HEADS, HEAD_DIM), k_gain)
    v = v.reshape(b, s, A_KV_HEADS, HEAD_DIM)
    ang_row, ang_col = axial_rope_angles(s)
    q = axial_rope(q, ang_row, ang_col) * (HEAD_DIM ** -0.5)
    k = axial_rope(k, ang_row, ang_col)

    def block(i):
        qb = lax.dynamic_slice_in_dim(q, i * BLOCK_Q, BLOCK_Q, axis=1)
        sc = jnp.einsum('bqhgd,bkhd->bhgqk', qb, k).astype(jnp.float32)
        p = jax.nn.softmax(sc, axis=-1).astype(v.dtype)
        o = jnp.einsum('bhgqk,bkhd->bqhgd', p, v)
        return o.reshape(b, BLOCK_Q, A_HEADS * HEAD_DIM)

    o = stack_blocks(lax.map(block, jnp.arange(s // BLOCK_Q)), b, s)
    return o @ w_o


def mixer_b(h, w_qkv, w_o):
    b, s, _ = h.shape
    n_g = len(B_GROUPS)
    rep = B_HEADS_PER_GROUP // B_KV_PER_GROUP
    nq = n_g * B_HEADS_PER_GROUP * HEAD_DIM
    nk = n_g * B_KV_PER_GROUP * HEAD_DIM
    qkv = h @ w_qkv
    q, k, v = jnp.split(qkv, [nq, nq + nk], axis=-1)
    q = q.reshape(b, s, n_g, B_KV_PER_GROUP, rep, HEAD_DIM) * (HEAD_DIM ** -0.5)
    k = k.reshape(b, s, n_g, B_KV_PER_GROUP, HEAD_DIM)
    v = v.reshape(b, s, n_g, B_KV_PER_GROUP, HEAD_DIM)
    slopes = alibi_slopes(n_g * B_HEADS_PER_GROUP).reshape(n_g, B_KV_PER_GROUP, rep)
    outs, lses = [], []
    for g, (window, dil) in enumerate(B_GROUPS):
        n_side = (window // 2) // dil
        offs = jnp.arange(-n_side, n_side + 1) * dil
        bias = -slopes[g][:, :, None, None] * jnp.abs(offs).astype(jnp.float32)
        qg, kg, vg = q[:, :, g], k[:, :, g], v[:, :, g]

        def block(i, qg=qg, kg=kg, vg=vg, offs=offs, bias=bias):
            start = i * BLOCK_Q
            idx = start + jnp.arange(BLOCK_Q)[:, None] + offs[None, :]
            valid = (idx >= 0) & (idx < s)
            idx = jnp.clip(idx, 0, s - 1)
            kb = jnp.take(kg, idx, axis=1)
            vb = jnp.take(vg, idx, axis=1)
            qb = lax.dynamic_slice_in_dim(qg, start, BLOCK_Q, axis=1)
            sc = jnp.einsum('bqhgd,bqnhd->bhgqn', qb, kb).astype(jnp.float32) + bias
            sc = jnp.where(valid, sc, -jnp.inf)
            m = jnp.max(sc, axis=-1, keepdims=True)
            p = jnp.exp(sc - m)
            den = jnp.sum(p, axis=-1, keepdims=True)
            o = jnp.einsum('bhgqn,bqnhd->bqhgd', (p / den).astype(vb.dtype), vb)
            lse = (m + jnp.log(den))[..., 0].transpose(0, 3, 1, 2)
            return o, lse

        o_g, lse_g = lax.map(block, jnp.arange(s // BLOCK_Q))
        outs.append(stack_blocks(o_g, b, s))
        lses.append(stack_blocks(lse_g, b, s))
    alpha = jax.nn.softmax(jnp.stack(lses, axis=2), axis=2)
    o = jnp.stack(outs, axis=2) * alpha[..., None].astype(h.dtype)
    return o.reshape(b, s, nq) @ w_o


def mixer_c(h, w_qkv, sinks, w_o):
    b, s, _ = h.shape
    rep = C_HEADS // C_KV_HEADS
    span = BLOCK_Q + 2 * C_WINDOW
    qkv = h @ w_qkv
    q, k, v = jnp.split(qkv, [C_HEADS * HEAD_DIM, (C_HEADS + C_KV_HEADS) * HEAD_DIM], axis=-1)
    q = q.reshape(b, s, C_KV_HEADS, rep, HEAD_DIM) * (HEAD_DIM ** -0.5)
    pad = ((0, 0), (C_WINDOW, C_WINDOW), (0, 0), (0, 0))
    kp = jnp.pad(k.reshape(b, s, C_KV_HEADS, HEAD_DIM), pad)
    vp = jnp.pad(v.reshape(b, s, C_KV_HEADS, HEAD_DIM), pad)
    slopes = alibi_slopes(C_HEADS).reshape(C_KV_HEADS, rep)[:, :, None, None]
    sink = sinks.astype(jnp.float32).reshape(1, C_KV_HEADS, rep, 1, 1)

    def block(i):
        start = i * BLOCK_Q
        qb = lax.dynamic_slice_in_dim(q, start, BLOCK_Q, axis=1)
        kb = lax.dynamic_slice_in_dim(kp, start, span, axis=1)
        vb = lax.dynamic_slice_in_dim(vp, start, span, axis=1)
        tq = start + jnp.arange(BLOCK_Q)
        tk = start - C_WINDOW + jnp.arange(span)
        dist = jnp.abs(tk[None, :] - tq[:, None])
        valid = (dist <= C_WINDOW) & (tk[None, :] >= 0) & (tk[None, :] < s)
        sc = jnp.einsum('bqhgd,bkhd->bhgqk', qb, kb).astype(jnp.float32)
        sc = jnp.where(valid, sc - slopes * dist.astype(jnp.float32), -jnp.inf)
        logits = jnp.concatenate([sc, jnp.broadcast_to(sink, sc.shape[:-1] + (1,))], axis=-1)
        p = jax.nn.softmax(logits, axis=-1)[..., :-1].astype(vb.dtype)
        o = jnp.einsum('bhgqk,bkhd->bqhgd', p, vb)
        return o.reshape(b, BLOCK_Q, C_HEADS * HEAD_DIM)

    o = stack_blocks(lax.map(block, jnp.arange(s // BLOCK_Q)), b, s)
    return o @ w_o


def squared_relu_mlp(h, w1, w2):
    u = jax.nn.relu(h @ w1)
    return (u * u) @ w2


def _fwd_setup_inputs(seed: int = 0) -> dict:
    key = jax.random.key(seed)
    ks = iter(jax.random.split(key, 32))
    kinds = [i % N_MIXERS for i in range(DEPTH)]
    n_a, n_b, n_c = kinds.count(0), kinds.count(1), kinds.count(2)

    def dense(k, shape):
        return jax.random.normal(k, shape, jnp.float32) * (shape[-2] ** -0.5)

    def gain(k, shape):
        return 1.0 + 0.05 * jax.random.normal(k, shape, jnp.float32)

    a_cols = (A_HEADS + 2 * A_KV_HEADS) * HEAD_DIM
    b_q = len(B_GROUPS) * B_HEADS_PER_GROUP * HEAD_DIM
    b_cols = b_q + 2 * len(B_GROUPS) * B_KV_PER_GROUP * HEAD_DIM
    c_cols = (C_HEADS + 2 * C_KV_HEADS) * HEAD_DIM
    return {
        "x": jax.random.normal(next(ks), (BATCH, SEQ, D_MODEL), jnp.float32),
        "attn_norm": gain(next(ks), (DEPTH, D_MODEL)),
        "mlp_norm": gain(next(ks), (DEPTH, D_MODEL)),
        "a_w_qkv": dense(next(ks), (n_a, D_MODEL, a_cols)),
        "a_q_gain": gain(next(ks), (n_a, HEAD_DIM)),
        "a_k_gain": gain(next(ks), (n_a, HEAD_DIM)),
        "a_w_o": dense(next(ks), (n_a, A_HEADS * HEAD_DIM, D_MODEL)),
        "b_w_qkv": dense(next(ks), (n_b, D_MODEL, b_cols)),
        "b_w_o": dense(next(ks), (n_b, b_q, D_MODEL)),
        "c_w_qkv": dense(next(ks), (n_c, D_MODEL, c_cols)),
        "c_sinks": 0.5 * jax.random.normal(next(ks), (n_c, C_HEADS), jnp.float32),
        "c_w_o": dense(next(ks), (n_c, C_HEADS * HEAD_DIM, D_MODEL)),
        "mlp_w1": dense(next(ks), (DEPTH, D_MODEL, D_FF)),
        "mlp_w2": dense(next(ks), (DEPTH, D_FF, D_MODEL)),
        "final_norm": gain(next(ks), (D_MODEL,)),
    }


def _fwd_reference(x, attn_norm, mlp_norm, a_w_qkv, a_q_gain, a_k_gain, a_w_o, b_w_qkv, b_w_o,
              c_w_qkv, c_sinks, c_w_o, mlp_w1, mlp_w2, final_norm):
    h = x
    used = [0, 0, 0]
    for layer in range(DEPTH):
        kind = layer % N_MIXERS
        j = used[kind]
        used[kind] += 1
        hn = rmsnorm(h, attn_norm[layer])
        if kind == 0:
            mix = mixer_a(hn, a_w_qkv[j], a_q_gain[j], a_k_gain[j], a_w_o[j])
        elif kind == 1:
            mix = mixer_b(hn, b_w_qkv[j], b_w_o[j])
        else:
            mix = mixer_c(hn, c_w_qkv[j], c_sinks[j], c_w_o[j])
        h = h + mix
        h = h + squared_relu_mlp(rmsnorm(h, mlp_norm[layer]), mlp_w1[layer], mlp_w2[layer])
    return rmsnorm(h, final_norm)


import jax as _jax
import jax.numpy as _jnp

TWIN_FORMAT = 'train_step'
FWD_PARAMS = ['x', 'attn_norm', 'mlp_norm', 'a_w_qkv', 'a_q_gain', 'a_k_gain', 'a_w_o', 'b_w_qkv', 'b_w_o', 'c_w_qkv', 'c_sinks', 'c_w_o', 'mlp_w1', 'mlp_w2', 'final_norm']
TWIN_WEIGHTS = ['attn_norm', 'mlp_norm', 'a_w_qkv', 'a_q_gain', 'a_k_gain', 'a_w_o', 'b_w_qkv', 'b_w_o', 'c_w_qkv', 'c_sinks', 'c_w_o', 'mlp_w1', 'mlp_w2', 'final_norm']
TWIN_DIFF_INPUT = 'x'
TWIN_INPUTS = ['x', 'attn_norm', 'mlp_norm', 'a_w_qkv', 'a_q_gain', 'a_k_gain', 'a_w_o', 'b_w_qkv', 'b_w_o', 'c_w_qkv', 'c_sinks', 'c_w_o', 'mlp_w1', 'mlp_w2', 'final_norm', 'loss_target', 'm_attn_norm', 'm_mlp_norm', 'm_a_w_qkv', 'm_a_q_gain', 'm_a_k_gain', 'm_a_w_o', 'm_b_w_qkv', 'm_b_w_o', 'm_c_w_qkv', 'm_c_sinks', 'm_c_w_o', 'm_mlp_w1', 'm_mlp_w2', 'm_final_norm', 'v_attn_norm', 'v_mlp_norm', 'v_a_w_qkv', 'v_a_q_gain', 'v_a_k_gain', 'v_a_w_o', 'v_b_w_qkv', 'v_b_w_o', 'v_c_w_qkv', 'v_c_sinks', 'v_c_w_o', 'v_mlp_w1', 'v_mlp_w2', 'v_final_norm']
TWIN_OUTPUTS = ['loss', 'grad_x', 'grad_attn_norm', 'grad_mlp_norm', 'grad_a_w_qkv', 'grad_a_q_gain', 'grad_a_k_gain', 'grad_a_w_o', 'grad_b_w_qkv', 'grad_b_w_o', 'grad_c_w_qkv', 'grad_c_sinks', 'grad_c_w_o', 'grad_mlp_w1', 'grad_mlp_w2', 'grad_final_norm', 'delta_attn_norm', 'delta_mlp_norm', 'delta_a_w_qkv', 'delta_a_q_gain', 'delta_a_k_gain', 'delta_a_w_o', 'delta_b_w_qkv', 'delta_b_w_o', 'delta_c_w_qkv', 'delta_c_sinks', 'delta_c_w_o', 'delta_mlp_w1', 'delta_mlp_w2', 'delta_final_norm', 'new_m_attn_norm', 'new_m_mlp_norm', 'new_m_a_w_qkv', 'new_m_a_q_gain', 'new_m_a_k_gain', 'new_m_a_w_o', 'new_m_b_w_qkv', 'new_m_b_w_o', 'new_m_c_w_qkv', 'new_m_c_sinks', 'new_m_c_w_o', 'new_m_mlp_w1', 'new_m_mlp_w2', 'new_m_final_norm', 'new_v_attn_norm', 'new_v_mlp_norm', 'new_v_a_w_qkv', 'new_v_a_q_gain', 'new_v_a_k_gain', 'new_v_a_w_o', 'new_v_b_w_qkv', 'new_v_b_w_o', 'new_v_c_w_qkv', 'new_v_c_sinks', 'new_v_c_w_o', 'new_v_mlp_w1', 'new_v_mlp_w2', 'new_v_final_norm']
TWIN_LEAF_KINDS = {'loss': 'loss', 'grad_x': 'grad_x', 'grad_attn_norm': 'grad_w', 'grad_mlp_norm': 'grad_w', 'grad_a_w_qkv': 'grad_w', 'grad_a_q_gain': 'grad_w', 'grad_a_k_gain': 'grad_w', 'grad_a_w_o': 'grad_w', 'grad_b_w_qkv': 'grad_w', 'grad_b_w_o': 'grad_w', 'grad_c_w_qkv': 'grad_w', 'grad_c_sinks': 'grad_w', 'grad_c_w_o': 'grad_w', 'grad_mlp_w1': 'grad_w', 'grad_mlp_w2': 'grad_w', 'grad_final_norm': 'grad_w', 'delta_attn_norm': 'delta_w', 'delta_mlp_norm': 'delta_w', 'delta_a_w_qkv': 'delta_w', 'delta_a_q_gain': 'delta_w', 'delta_a_k_gain': 'delta_w', 'delta_a_w_o': 'delta_w', 'delta_b_w_qkv': 'delta_w', 'delta_b_w_o': 'delta_w', 'delta_c_w_qkv': 'delta_w', 'delta_c_sinks': 'delta_w', 'delta_c_w_o': 'delta_w', 'delta_mlp_w1': 'delta_w', 'delta_mlp_w2': 'delta_w', 'delta_final_norm': 'delta_w', 'new_m_attn_norm': 'new_m', 'new_m_mlp_norm': 'new_m', 'new_m_a_w_qkv': 'new_m', 'new_m_a_q_gain': 'new_m', 'new_m_a_k_gain': 'new_m', 'new_m_a_w_o': 'new_m', 'new_m_b_w_qkv': 'new_m', 'new_m_b_w_o': 'new_m', 'new_m_c_w_qkv': 'new_m', 'new_m_c_sinks': 'new_m', 'new_m_c_w_o': 'new_m', 'new_m_mlp_w1': 'new_m', 'new_m_mlp_w2': 'new_m', 'new_m_final_norm': 'new_m', 'new_v_attn_norm': 'new_v', 'new_v_mlp_norm': 'new_v', 'new_v_a_w_qkv': 'new_v', 'new_v_a_q_gain': 'new_v', 'new_v_a_k_gain': 'new_v', 'new_v_a_w_o': 'new_v', 'new_v_b_w_qkv': 'new_v', 'new_v_b_w_o': 'new_v', 'new_v_c_w_qkv': 'new_v', 'new_v_c_sinks': 'new_v', 'new_v_c_w_o': 'new_v', 'new_v_mlp_w1': 'new_v', 'new_v_mlp_w2': 'new_v', 'new_v_final_norm': 'new_v'}


def _forward(args):
    return _fwd_reference(*[args[k] for k in FWD_PARAMS])


def _output_shape():
    out = _jax.eval_shape(lambda: _forward(_fwd_setup_inputs(0)))
    return out.shape, out.dtype

N_MICROBATCH = 1
ADAM_LR = 0.001
ADAM_B1 = 0.9
ADAM_B2 = 0.999
ADAM_EPS = 1e-08
ADAM_WD = 0.01
ADAM_STEP = 10
PER_EXAMPLE_BATCH_AXIS = {'x': 0, 'loss_target': 0}
SHARED_INPUTS = []
_WEIGHT_DTYPES = {'attn_norm': _jnp.float32, 'mlp_norm': _jnp.float32, 'a_w_qkv': _jnp.float32, 'a_q_gain': _jnp.float32, 'a_k_gain': _jnp.float32, 'a_w_o': _jnp.float32, 'b_w_qkv': _jnp.float32, 'b_w_o': _jnp.float32, 'c_w_qkv': _jnp.float32, 'c_sinks': _jnp.float32, 'c_w_o': _jnp.float32, 'mlp_w1': _jnp.float32, 'mlp_w2': _jnp.float32, 'final_norm': _jnp.float32}
MOMENT_SCALE = {'attn_norm': 1.140298e-01, 'mlp_norm': 1.810043e-01, 'a_w_qkv': 1.020031e-01, 'a_q_gain': 7.130780e-02, 'a_k_gain': 6.913531e-02, 'a_w_o': 1.371684e-01, 'b_w_qkv': 4.119368e-02, 'b_w_o': 6.704176e-02, 'c_w_qkv': 1.379882e-01, 'c_sinks': 8.333311e-02, 'c_w_o': 2.050021e-01, 'mlp_w1': 8.645089e-02, 'mlp_w2': 2.875819e-01, 'final_norm': 3.340232e+01}


def _to_microbatches(a, axis):
    t = _jnp.moveaxis(a, axis, 0)
    t = t.reshape((N_MICROBATCH, t.shape[0] // N_MICROBATCH) + t.shape[1:])
    return _jnp.moveaxis(t, 1, axis + 1)


def setup_inputs(seed: int = 0) -> dict:
    inp = _fwd_setup_inputs(seed)
    key = _jax.random.fold_in(_jax.random.key(seed), 7919)
    shape, _ = _output_shape()
    out = dict(inp)
    out["loss_target"] = _jax.random.normal(_jax.random.fold_in(key, 0), shape, _jnp.float32)
    for i, name in enumerate(TWIN_WEIGHTS):
        w = inp[name].astype(_jnp.float32)
        if MOMENT_SCALE is None:
            s = _jnp.sqrt(_jnp.mean(_jnp.square(w)) + 1e-30)
        else:
            s = MOMENT_SCALE[name]
        km, kv = _jax.random.split(_jax.random.fold_in(key, i + 1))
        out[name] = w
        out["m_" + name] = s * _jax.random.normal(km, w.shape, _jnp.float32)
        out["v_" + name] = (s * s) * _jax.random.uniform(kv, w.shape, _jnp.float32, 0.5, 1.5)
    if N_MICROBATCH > 1:
        for name, axis in PER_EXAMPLE_BATCH_AXIS.items():
            out[name] = _to_microbatches(out[name], axis)
    return {'x': out['x'], 'attn_norm': out['attn_norm'], 'mlp_norm': out['mlp_norm'], 'a_w_qkv': out['a_w_qkv'], 'a_q_gain': out['a_q_gain'], 'a_k_gain': out['a_k_gain'], 'a_w_o': out['a_w_o'], 'b_w_qkv': out['b_w_qkv'], 'b_w_o': out['b_w_o'], 'c_w_qkv': out['c_w_qkv'], 'c_sinks': out['c_sinks'], 'c_w_o': out['c_w_o'], 'mlp_w1': out['mlp_w1'], 'mlp_w2': out['mlp_w2'], 'final_norm': out['final_norm'], 'loss_target': out['loss_target'], 'm_attn_norm': out['m_attn_norm'], 'm_mlp_norm': out['m_mlp_norm'], 'm_a_w_qkv': out['m_a_w_qkv'], 'm_a_q_gain': out['m_a_q_gain'], 'm_a_k_gain': out['m_a_k_gain'], 'm_a_w_o': out['m_a_w_o'], 'm_b_w_qkv': out['m_b_w_qkv'], 'm_b_w_o': out['m_b_w_o'], 'm_c_w_qkv': out['m_c_w_qkv'], 'm_c_sinks': out['m_c_sinks'], 'm_c_w_o': out['m_c_w_o'], 'm_mlp_w1': out['m_mlp_w1'], 'm_mlp_w2': out['m_mlp_w2'], 'm_final_norm': out['m_final_norm'], 'v_attn_norm': out['v_attn_norm'], 'v_mlp_norm': out['v_mlp_norm'], 'v_a_w_qkv': out['v_a_w_qkv'], 'v_a_q_gain': out['v_a_q_gain'], 'v_a_k_gain': out['v_a_k_gain'], 'v_a_w_o': out['v_a_w_o'], 'v_b_w_qkv': out['v_b_w_qkv'], 'v_b_w_o': out['v_b_w_o'], 'v_c_w_qkv': out['v_c_w_qkv'], 'v_c_sinks': out['v_c_sinks'], 'v_c_w_o': out['v_c_w_o'], 'v_mlp_w1': out['v_mlp_w1'], 'v_mlp_w2': out['v_mlp_w2'], 'v_final_norm': out['v_final_norm']}


def _loss(weights, diff, rest, loss_target):
    with _jax.named_scope("forward"):
        args = {**rest, TWIN_DIFF_INPUT: diff, **{k: w.astype(_WEIGHT_DTYPES[k]) for k, w in weights.items()}}
        y = _forward(args)
    with _jax.named_scope("loss_head"):
        err = _jnp.square(y.astype(_jnp.float32) - loss_target)
        return 0.5 * _jnp.sum(_jnp.mean(err, axis=-1)) if err.ndim else 0.5 * err


def _adamw(w, g, m, v):
    m = ADAM_B1 * m + (1.0 - ADAM_B1) * g
    v = ADAM_B2 * v + (1.0 - ADAM_B2) * _jnp.square(g)
    m_hat = m / (1.0 - ADAM_B1 ** ADAM_STEP)
    v_hat = v / (1.0 - ADAM_B2 ** ADAM_STEP)
    delta = -ADAM_LR * (m_hat / (_jnp.sqrt(v_hat) + ADAM_EPS) + ADAM_WD * w)
    return delta, m, v


def reference(x, attn_norm, mlp_norm, a_w_qkv, a_q_gain, a_k_gain, a_w_o, b_w_qkv, b_w_o, c_w_qkv, c_sinks, c_w_o, mlp_w1, mlp_w2, final_norm, loss_target, m_attn_norm, m_mlp_norm, m_a_w_qkv, m_a_q_gain, m_a_k_gain, m_a_w_o, m_b_w_qkv, m_b_w_o, m_c_w_qkv, m_c_sinks, m_c_w_o, m_mlp_w1, m_mlp_w2, m_final_norm, v_attn_norm, v_mlp_norm, v_a_w_qkv, v_a_q_gain, v_a_k_gain, v_a_w_o, v_b_w_qkv, v_b_w_o, v_c_w_qkv, v_c_sinks, v_c_w_o, v_mlp_w1, v_mlp_w2, v_final_norm):
    given = dict(x=x, attn_norm=attn_norm, mlp_norm=mlp_norm, a_w_qkv=a_w_qkv, a_q_gain=a_q_gain, a_k_gain=a_k_gain, a_w_o=a_w_o, b_w_qkv=b_w_qkv, b_w_o=b_w_o, c_w_qkv=c_w_qkv, c_sinks=c_sinks, c_w_o=c_w_o, mlp_w1=mlp_w1, mlp_w2=mlp_w2, final_norm=final_norm, loss_target=loss_target, m_attn_norm=m_attn_norm, m_mlp_norm=m_mlp_norm, m_a_w_qkv=m_a_w_qkv, m_a_q_gain=m_a_q_gain, m_a_k_gain=m_a_k_gain, m_a_w_o=m_a_w_o, m_b_w_qkv=m_b_w_qkv, m_b_w_o=m_b_w_o, m_c_w_qkv=m_c_w_qkv, m_c_sinks=m_c_sinks, m_c_w_o=m_c_w_o, m_mlp_w1=m_mlp_w1, m_mlp_w2=m_mlp_w2, m_final_norm=m_final_norm, v_attn_norm=v_attn_norm, v_mlp_norm=v_mlp_norm, v_a_w_qkv=v_a_w_qkv, v_a_q_gain=v_a_q_gain, v_a_k_gain=v_a_k_gain, v_a_w_o=v_a_w_o, v_b_w_qkv=v_b_w_qkv, v_b_w_o=v_b_w_o, v_c_w_qkv=v_c_w_qkv, v_c_sinks=v_c_sinks, v_c_w_o=v_c_w_o, v_mlp_w1=v_mlp_w1, v_mlp_w2=v_mlp_w2, v_final_norm=v_final_norm)
    weights = {n: given[n] for n in TWIN_WEIGHTS}
    shared = {n: given[n] for n in SHARED_INPUTS}
    per_example = {n: given[n] for n in ['x']}
    grad_fn = _jax.value_and_grad(_loss, argnums=(0, 1))

    def one_microbatch(ex, loss_target):
        ex = dict(ex)
        diff = ex.pop(TWIN_DIFF_INPUT)
        return grad_fn(weights, diff, {**shared, **ex}, loss_target)

    if N_MICROBATCH == 1:
        loss, (grad_w, grad_x) = one_microbatch(per_example, given["loss_target"])
    else:
        def body(carry, xs):
            loss_sum, grad_sum = carry
            l_k, (gw_k, gx_k) = one_microbatch(xs[0], xs[1])
            with _jax.named_scope("update"):
                return (loss_sum + l_k, _jax.tree.map(_jnp.add, grad_sum, gw_k)), gx_k

        init = (_jnp.zeros((), _jnp.float32), _jax.tree.map(_jnp.zeros_like, weights))
        (loss, grad_w), grad_x = _jax.lax.scan(body, init, (per_example, given["loss_target"]))
    with _jax.named_scope("update"):
        delta_w, new_m, new_v = {}, {}, {}
        for n in TWIN_WEIGHTS:
            delta_w[n], new_m[n], new_v[n] = _adamw(weights[n], grad_w[n], given["m_" + n], given["v_" + n])
    return (loss, grad_x, *[grad_w[n] for n in TWIN_WEIGHTS], *[delta_w[n] for n in TWIN_WEIGHTS],
            *[new_m[n] for n in TWIN_WEIGHTS], *[new_v[n] for n in TWIN_WEIGHTS])
```

```python
import jax
import jax.numpy as jnp
from jax import lax
from jax.experimental import pallas as pl
from jax.experimental.pallas import tpu as pltpu

F32 = jnp.float32
BF16 = jnp.bfloat16
MESH = pl.DeviceIdType.MESH
ANY = pl.BlockSpec(memory_space=pl.ANY)

D_MODEL = 1024
HEAD_DIM = 64
GRID_W = 64
ROPE_THETA = 10000.0
RMS_EPS = 1e-6
QK_SCALE = HEAD_DIM ** -0.5
A_HEADS, A_KV = 16, 4
B_GROUPS = ((128, 1), (512, 4), (2048, 16))
B_HEADS_PER_GROUP, B_KV_PER_GROUP = 6, 2
C_HEADS, C_KV, C_WINDOW = 16, 4, 128
DEPTH, N_MIXERS = 4, 3
ADAM_LR, ADAM_B1, ADAM_B2, ADAM_EPS, ADAM_WD, ADAM_STEP = 0.001, 0.9, 0.999, 1e-08, 0.01, 10

WIN_PAD = 128
NEG = -1e30
V7X_VMEM_BUDGET = 48 * 1024 * 1024
LANES = 128


def _params(semantics):
    return pltpu.CompilerParams(dimension_semantics=semantics, vmem_limit_bytes=V7X_VMEM_BUDGET)


def _tile(n, cap):
    if n <= cap:
        return n
    t = (cap // LANES) * LANES
    while n % t:
        t -= LANES
    return t


def _norm_mm(h, gain, w, *, out_dtype, relu2, name):
    m, d = h.shape
    n = w.shape[1]
    tm, tn = 512, _tile(n, 768)

    def body(h_ref, g_ref, w_ref, hn_ref, y_ref):
        @pl.when(pl.program_id(1) == 0)
        def _():
            x = h_ref[...]
            r = lax.rsqrt(jnp.mean(x * x, axis=-1, keepdims=True) + RMS_EPS)
            hn_ref[...] = (x * r * g_ref[...]).astype(BF16)

        y = jnp.dot(hn_ref[...], w_ref[...], preferred_element_type=F32)
        if relu2:
            y = jnp.maximum(y, 0.0)
            y = y * y
        y_ref[...] = y.astype(y_ref.dtype)

    return pl.pallas_call(
        body, name=name, grid=(m // tm, n // tn),
        in_specs=[pl.BlockSpec((tm, d), lambda i, j: (i, 0)), pl.BlockSpec((1, d), lambda i, j: (0, 0)),
                  pl.BlockSpec((d, tn), lambda i, j: (0, j))],
        out_specs=[pl.BlockSpec((tm, d), lambda i, j: (i, 0)), pl.BlockSpec((tm, tn), lambda i, j: (i, j))],
        out_shape=[jax.ShapeDtypeStruct((m, d), BF16), jax.ShapeDtypeStruct((m, n), out_dtype)],
        compiler_params=_params(("parallel", "arbitrary")),
    )(h, gain, w)


def _mm_res(a, w, h_in, *, name):
    m, k = a.shape
    d = w.shape[1]
    tm, tk = 512, _tile(k, 512)

    def body(a_ref, w_ref, h_ref, o_ref):
        @pl.when(pl.program_id(1) == 0)
        def _():
            o_ref[...] = h_ref[...]

        o_ref[...] += jnp.dot(a_ref[...], w_ref[...], preferred_element_type=F32)

    return pl.pallas_call(
        body, name=name, grid=(m // tm, k // tk),
        in_specs=[pl.BlockSpec((tm, tk), lambda i, j: (i, j)), pl.BlockSpec((tk, d), lambda i, j: (j, 0)),
                  pl.BlockSpec((tm, d), lambda i, j: (i, 0))],
        out_specs=pl.BlockSpec((tm, d), lambda i, j: (i, 0)),
        out_shape=jax.ShapeDtypeStruct((m, d), F32),
        compiler_params=_params(("parallel", "arbitrary")),
    )(a, w, h_in)


def _mm_nt(a, w, act, *, name):
    m, d = a.shape
    n = w.shape[0]
    tm, tn = 512, _tile(n, 512)

    def body(*refs):
        a_ref, w_ref = refs[0], refs[1]
        o_ref = refs[-1]
        acc = lax.dot_general(a_ref[...].astype(BF16), w_ref[...], (((1,), (1,)), ((), ())),
                              preferred_element_type=F32)
        if act is not None:
            acc = acc * (2.0 * jnp.sqrt(refs[2][...].astype(F32)))
        o_ref[...] = acc.astype(BF16)

    in_specs = [pl.BlockSpec((tm, d), lambda i, j: (i, 0)), pl.BlockSpec((tn, d), lambda i, j: (j, 0))]
    args = [a, w]
    if act is not None:
        in_specs.append(pl.BlockSpec((tm, tn), lambda i, j: (i, j)))
        args.append(act)
    return pl.pallas_call(
        body, name=name, grid=(m // tm, n // tn), in_specs=in_specs,
        out_specs=pl.BlockSpec((tm, tn), lambda i, j: (i, j)),
        out_shape=jax.ShapeDtypeStruct((m, n), BF16),
        compiler_params=_params(("parallel", "parallel")),
    )(*args)


def _rmsnorm_bwd(dn, x, gain):
    r = lax.rsqrt(jnp.mean(x * x, axis=-1, keepdims=True) + RMS_EPS)
    xh = x * r
    dgain = jnp.sum(dn * xh, axis=0, keepdims=True)
    u = dn * gain
    dx = r * (u - xh * jnp.mean(u * xh, axis=-1, keepdims=True))
    return dx, dgain


def _mm_nt_normbwd(g, w, h, gain, dh_in, *, name):
    m, k = g.shape
    d = w.shape[0]
    tm, tk = 512, _tile(k, 768)
    nk = k // tk

    def body(g_ref, w_ref, h_ref, gain_ref, dh_ref, o_ref, dg_ref, acc_ref):
        i, j = pl.program_id(0), pl.program_id(1)

        @pl.when((i == 0) & (j == 0))
        def _():
            dg_ref[...] = jnp.zeros_like(dg_ref)

        @pl.when(j == 0)
        def _():
            acc_ref[...] = jnp.zeros_like(acc_ref)

        acc_ref[...] += lax.dot_general(g_ref[...], w_ref[...], (((1,), (1,)), ((), ())),
                                        preferred_element_type=F32)

        @pl.when(j == nk - 1)
        def _():
            dx, dgain = _rmsnorm_bwd(acc_ref[...], h_ref[...], gain_ref[...])
            dg_ref[...] += dgain
            o_ref[...] = dh_ref[...] + dx

    return pl.pallas_call(
        body, name=name, grid=(m // tm, nk),
        in_specs=[pl.BlockSpec((tm, tk), lambda i, j: (i, j)), pl.BlockSpec((d, tk), lambda i, j: (0, j)),
                  pl.BlockSpec((tm, d), lambda i, j: (i, 0)), pl.BlockSpec((1, d), lambda i, j: (0, 0)),
                  pl.BlockSpec((tm, d), lambda i, j: (i, 0))],
        out_specs=[pl.BlockSpec((tm, d), lambda i, j: (i, 0)), pl.BlockSpec((1, d), lambda i, j: (0, 0))],
        out_shape=[jax.ShapeDtypeStruct((m, d), F32), jax.ShapeDtypeStruct((1, d), F32)],
        scratch_shapes=[pltpu.VMEM((tm, d), F32)],
        compiler_params=_params(("arbitrary", "arbitrary")),
    )(g, w, h, gain, dh_in)


def _mm_tn(x, g, *, name):
    m, k = x.shape
    n = g.shape[1]
    tm, tk, tn = 512, (k if k <= 1152 else 1024), _tile(n, 512)
    nm = m // tm

    def body(x_ref, g_ref, o_ref, acc_ref):
        s = pl.program_id(2)

        @pl.when(s == 0)
        def _():
            acc_ref[...] = jnp.zeros_like(acc_ref)

        acc_ref[...] += lax.dot_general(x_ref[...], g_ref[...].astype(BF16), (((0,), (0,)), ((), ())),
                                        preferred_element_type=F32)

        @pl.when(s == nm - 1)
        def _():
            o_ref[...] = acc_ref[...].astype(BF16)

    return pl.pallas_call(
        body, name=name, grid=(k // tk, n // tn, nm),
        in_specs=[pl.BlockSpec((tm, tk), lambda a, b, s: (s, a)), pl.BlockSpec((tm, tn), lambda a, b, s: (s, b))],
        out_specs=pl.BlockSpec((tk, tn), lambda a, b, s: (a, b)),
        out_shape=jax.ShapeDtypeStruct((k, n), BF16),
        scratch_shapes=[pltpu.VMEM((tk, tn), F32)],
        compiler_params=_params(("parallel", "parallel", "arbitrary")),
    )(x, g)


def _loss_head(h, gain, target):
    m, d = h.shape
    tm = 512

    def body(h_ref, g_ref, t_ref, dh_ref, loss_ref, dg_ref):
        @pl.when(pl.program_id(0) == 0)
        def _():
            loss_ref[...] = jnp.zeros_like(loss_ref)
            dg_ref[...] = jnp.zeros_like(dg_ref)

        x = h_ref[...]
        gain_v = g_ref[...]
        r = lax.rsqrt(jnp.mean(x * x, axis=-1, keepdims=True) + RMS_EPS)
        err = x * r * gain_v - t_ref[...]
        loss_ref[...] += 0.5 * jnp.sum(jnp.mean(err * err, axis=-1, keepdims=True), axis=0, keepdims=True)
        dx, dgain = _rmsnorm_bwd(err * (1.0 / d), x, gain_v)
        dg_ref[...] += dgain
        dh_ref[...] = dx

    return pl.pallas_call(
        body, name="loss_head", grid=(m // tm,),
        in_specs=[pl.BlockSpec((tm, d), lambda i: (i, 0)), pl.BlockSpec((1, d), lambda i: (0, 0)),
                  pl.BlockSpec((tm, d), lambda i: (i, 0))],
        out_specs=[pl.BlockSpec((tm, d), lambda i: (i, 0)), pl.BlockSpec((1, LANES), lambda i: (0, 0)),
                   pl.BlockSpec((1, d), lambda i: (0, 0))],
        out_shape=[jax.ShapeDtypeStruct((m, d), F32), jax.ShapeDtypeStruct((1, LANES), F32),
                   jax.ShapeDtypeStruct((1, d), F32)],
        compiler_params=_params(("arbitrary",)),
    )(h, gain, target)


def _rope_tables(s):
    t = jnp.arange(s)
    row = (t // GRID_W).astype(F32)
    col = (t % GRID_W).astype(F32)
    axis_dim = HEAD_DIM // 2
    inv_freq = ROPE_THETA ** (-jnp.arange(0, axis_dim, 2, dtype=F32) / axis_dim)
    ar, ac = row[:, None] * inv_freq, col[:, None] * inv_freq
    cos = jnp.concatenate([jnp.cos(ar), jnp.cos(ar), jnp.cos(ac), jnp.cos(ac)], axis=-1)
    sin = jnp.concatenate([-jnp.sin(ar), jnp.sin(ar), -jnp.sin(ac), jnp.sin(ac)], axis=-1)
    return jnp.tile(cos, (1, 2)), jnp.tile(sin, (1, 2))


def _swap16(x):
    lane = lax.broadcasted_iota(jnp.int32, x.shape, 1)
    return jnp.where((lane % 32) < 16, pltpu.roll(x, LANES - 16, 1), pltpu.roll(x, 16, 1))


def _head_mean(v):
    lane = lax.broadcasted_iota(jnp.int32, v.shape, 1)
    lo = lane < HEAD_DIM
    s_all = jnp.sum(v, axis=-1, keepdims=True)
    s_lo = jnp.sum(jnp.where(lo, v, 0.0), axis=-1, keepdims=True)
    return jnp.where(lo, s_lo, s_all - s_lo) * (1.0 / HEAD_DIM)


def _norm_rope(x, gain2, cos, sin):
    r = lax.rsqrt(_head_mean(x * x) + RMS_EPS)
    nrm = x * r * gain2
    return nrm * cos + _swap16(nrm) * sin


def _norm_rope_bwd(dy, x, gain2, cos, sin):
    dn = dy * cos + _swap16(dy * sin)
    r = lax.rsqrt(_head_mean(x * x) + RMS_EPS)
    xh = x * r
    dgain = jnp.sum(dn * xh, axis=0, keepdims=True)
    u = dn * gain2
    return r * (u - xh * _head_mean(u * xh)), dgain


def _a_prep(qkv, cos, sin, gq2, gk2):
    s = qkv.shape[0]
    tr = 256
    nq, nk = A_HEADS * HEAD_DIM, A_KV * HEAD_DIM

    def body(qkv_ref, cos_ref, sin_ref, gq_ref, gk_ref, qt_ref, k_ref, v_ref):
        cos_v, sin_v = cos_ref[...], sin_ref[...]
        for c in range(nq // LANES):
            y = _norm_rope(qkv_ref[:, c * LANES:(c + 1) * LANES], gq_ref[...], cos_v, sin_v) * QK_SCALE
            yt = y.T
            qt_ref[2 * c] = yt[:HEAD_DIM].astype(BF16)
            qt_ref[2 * c + 1] = yt[HEAD_DIM:].astype(BF16)
        for c in range(nk // LANES):
            y = _norm_rope(qkv_ref[:, nq + c * LANES:nq + (c + 1) * LANES], gk_ref[...], cos_v, sin_v)
            k_ref[2 * c] = y[:, :HEAD_DIM].astype(BF16)
            k_ref[2 * c + 1] = y[:, HEAD_DIM:].astype(BF16)
            x = qkv_ref[:, nq + nk + c * LANES:nq + nk + (c + 1) * LANES]
            v_ref[2 * c] = x[:, :HEAD_DIM].astype(BF16)
            v_ref[2 * c + 1] = x[:, HEAD_DIM:].astype(BF16)

    return pl.pallas_call(
        body, name="a_prep", grid=(s // tr,),
        in_specs=[pl.BlockSpec((tr, nq + 2 * nk), lambda i: (i, 0)), pl.BlockSpec((tr, LANES), lambda i: (i, 0)),
                  pl.BlockSpec((tr, LANES), lambda i: (i, 0)), pl.BlockSpec((1, LANES), lambda i: (0, 0)),
                  pl.BlockSpec((1, LANES), lambda i: (0, 0))],
        out_specs=[pl.BlockSpec((A_HEADS, HEAD_DIM, tr), lambda i: (0, 0, i)),
                   pl.BlockSpec((A_KV, tr, HEAD_DIM), lambda i: (0, i, 0)),
                   pl.BlockSpec((A_KV, tr, HEAD_DIM), lambda i: (0, i, 0))],
        out_shape=[jax.ShapeDtypeStruct((A_HEADS, HEAD_DIM, s), BF16), jax.ShapeDtypeStruct((A_KV, s, HEAD_DIM), BF16),
                   jax.ShapeDtypeStruct((A_KV, s, HEAD_DIM), BF16)],
        compiler_params=_params(("parallel",)),
    )(qkv, cos, sin, gq2, gk2)


def _a_prep_bwd(dqt, dkt, dvt, qkv, cos, sin, gq2, gk2):
    s = qkv.shape[0]
    tr = 256
    nq, nk = A_HEADS * HEAD_DIM, A_KV * HEAD_DIM

    def body(dqt_ref, dkt_ref, dvt_ref, qkv_ref, cos_ref, sin_ref, gq_ref, gk_ref, o_ref, dgq_ref, dgk_ref):
        @pl.when(pl.program_id(0) == 0)
        def _():
            dgq_ref[...] = jnp.zeros_like(dgq_ref)
            dgk_ref[...] = jnp.zeros_like(dgk_ref)

        cos_v, sin_v = cos_ref[...], sin_ref[...]

        def pair(ref, c):
            return jnp.concatenate([ref[2 * c], ref[2 * c + 1]], axis=0).T

        for c in range(nq // LANES):
            dx, dg = _norm_rope_bwd(pair(dqt_ref, c) * QK_SCALE, qkv_ref[:, c * LANES:(c + 1) * LANES],
                                    gq_ref[...], cos_v, sin_v)
            o_ref[:, c * LANES:(c + 1) * LANES] = dx.astype(BF16)
            dgq_ref[...] += dg
        for c in range(nk // LANES):
            lo = nq + c * LANES
            dx, dg = _norm_rope_bwd(pair(dkt_ref, c), qkv_ref[:, lo:lo + LANES], gk_ref[...], cos_v, sin_v)
            o_ref[:, lo:lo + LANES] = dx.astype(BF16)
            dgk_ref[...] += dg
            o_ref[:, lo + nk:lo + nk + LANES] = pair(dvt_ref, c).astype(BF16)

    return pl.pallas_call(
        body, name="a_prep_bwd", grid=(s // tr,),
        in_specs=[pl.BlockSpec((A_HEADS, HEAD_DIM, tr), lambda i: (0, 0, i)),
                  pl.BlockSpec((A_KV, HEAD_DIM, tr), lambda i: (0, 0, i)),
                  pl.BlockSpec((A_KV, HEAD_DIM, tr), lambda i: (0, 0, i)),
                  pl.BlockSpec((tr, nq + 2 * nk), lambda i: (i, 0)), pl.BlockSpec((tr, LANES), lambda i: (i, 0)),
                  pl.BlockSpec((tr, LANES), lambda i: (i, 0)), pl.BlockSpec((1, LANES), lambda i: (0, 0)),
                  pl.BlockSpec((1, LANES), lambda i: (0, 0))],
        out_specs=[pl.BlockSpec((tr, nq + 2 * nk), lambda i: (i, 0)), pl.BlockSpec((1, LANES), lambda i: (0, 0)),
                   pl.BlockSpec((1, LANES), lambda i: (0, 0))],
        out_shape=[jax.ShapeDtypeStruct((s, nq + 2 * nk), BF16), jax.ShapeDtypeStruct((1, LANES), F32),
                   jax.ShapeDtypeStruct((1, LANES), F32)],
        compiler_params=_params(("arbitrary",)),
    )(dqt, dkt, dvt, qkv, cos, sin, gq2, gk2)


def _a_attn_fwd(qt, k, v):
    nh, _, s = qt.shape
    rep = nh // k.shape[0]
    tq = 256

    def body(qt_ref, k_ref, v_ref, o_ref):
        st = jnp.dot(k_ref[0], qt_ref[0], preferred_element_type=F32)
        p = jnp.exp(st - jnp.max(st, axis=0, keepdims=True))
        den = jnp.sum(p, axis=0, keepdims=True)
        ot = lax.dot_general(v_ref[0], p.astype(BF16), (((0,), (0,)), ((), ())), preferred_element_type=F32)
        o_ref[0] = (ot / den).astype(BF16)

    return pl.pallas_call(
        body, name="a_attn_fwd", grid=(nh, s // tq),
        in_specs=[pl.BlockSpec((1, HEAD_DIM, tq), lambda h, i: (h, 0, i)),
                  pl.BlockSpec((1, s, HEAD_DIM), lambda h, i: (h // rep, 0, 0)),
                  pl.BlockSpec((1, s, HEAD_DIM), lambda h, i: (h // rep, 0, 0))],
        out_specs=pl.BlockSpec((1, HEAD_DIM, tq), lambda h, i: (h, 0, i)),
        out_shape=jax.ShapeDtypeStruct((nh, HEAD_DIM, s), BF16),
        compiler_params=_params(("parallel", "parallel")),
    )(qt, k, v)


def _a_attn_bwd(qt, k, v, dot, ot):
    nh, _, s = qt.shape
    nkv = k.shape[0]
    rep = nh // nkv
    tq = 256

    def body(qt_ref, k_ref, v_ref, dot_ref, ot_ref, dq_ref, dk_ref, dv_ref):
        h, i = pl.program_id(0), pl.program_id(1)

        @pl.when((h % rep == 0) & (i == 0))
        def _():
            dk_ref[...] = jnp.zeros_like(dk_ref)
            dv_ref[...] = jnp.zeros_like(dv_ref)

        q_t, do_t = qt_ref[0], dot_ref[0]
        st = jnp.dot(k_ref[0], q_t, preferred_element_type=F32)
        p = jnp.exp(st - jnp.max(st, axis=0, keepdims=True))
        p = p / jnp.sum(p, axis=0, keepdims=True)
        dp = jnp.dot(v_ref[0], do_t, preferred_element_type=F32)
        delta = jnp.sum(do_t.astype(F32) * ot_ref[0].astype(F32), axis=0, keepdims=True)
        ds = (p * (dp - delta)).astype(BF16)
        nt = (((1,), (1,)), ((), ()))
        dv_ref[0] += lax.dot_general(do_t, p.astype(BF16), nt, preferred_element_type=F32)
        dk_ref[0] += lax.dot_general(q_t, ds, nt, preferred_element_type=F32)
        dq_ref[0] = lax.dot_general(k_ref[0], ds, (((0,), (0,)), ((), ())), preferred_element_type=F32)

    blk_q = pl.BlockSpec((1, HEAD_DIM, tq), lambda h, i: (h, 0, i))
    blk_kv = pl.BlockSpec((1, s, HEAD_DIM), lambda h, i: (h // rep, 0, 0))
    blk_acc = pl.BlockSpec((1, HEAD_DIM, s), lambda h, i: (h // rep, 0, 0))
    return pl.pallas_call(
        body, name="a_attn_bwd", grid=(nh, s // tq),
        in_specs=[blk_q, blk_kv, blk_kv, blk_q, blk_q],
        out_specs=[blk_q, blk_acc, blk_acc],
        out_shape=[jax.ShapeDtypeStruct((nh, HEAD_DIM, s), F32), jax.ShapeDtypeStruct((nkv, HEAD_DIM, s), F32),
                   jax.ShapeDtypeStruct((nkv, HEAD_DIM, s), F32)],
        compiler_params=_params(("arbitrary", "arbitrary")),
    )(qt, k, v, dot, ot)


def _win_scores(kw, q_t, i, tq, tk, window, dil, seg, slope):
    st = jnp.dot(kw, q_t, preferred_element_type=F32)
    qpos = i * tq + lax.broadcasted_iota(jnp.int32, (tk, tq), 1)
    kpos = i * tq - WIN_PAD + lax.broadcasted_iota(jnp.int32, (tk, tq), 0)
    dist = jnp.abs(kpos - qpos)
    seg_lo = qpos - (qpos & (seg - 1))
    valid = (dist <= window) & (kpos >= seg_lo) & (kpos < seg_lo + seg)
    return jnp.where(valid, st * QK_SCALE - slope * (dist * dil).astype(F32), NEG)


def _win_tq(s):
    return min(512, s)


def _win_fwd(qt, kp, vp, slopes, sinks, *, window, dil, seg, out_dtype, name):
    nh, _, s = qt.shape
    rep = nh // kp.shape[0]
    tq = _win_tq(s)
    tk = tq + 2 * WIN_PAD
    sp = s + 2 * WIN_PAD

    def body(*refs):
        qt_ref, k_ref, v_ref, sl_ref = refs[:4]
        o_ref, lse_ref = refs[-2:]
        i = pl.program_id(1)
        base = pl.multiple_of(i * tq, tq)
        st = _win_scores(k_ref[0, pl.ds(base, tk), :], qt_ref[0], i, tq, tk, window, dil, seg, sl_ref[0][:, :1])
        mx = jnp.max(st, axis=0, keepdims=True)
        if sinks is not None:
            sink = refs[4][0][:, :1]
            mx = jnp.maximum(mx, sink)
        p = jnp.exp(st - mx)
        den = jnp.sum(p, axis=0, keepdims=True)
        if sinks is not None:
            den = den + jnp.exp(sink - mx)
        ot = lax.dot_general(v_ref[0, pl.ds(base, tk), :], p.astype(BF16), (((0,), (0,)), ((), ())),
                             preferred_element_type=F32)
        o_ref[0] = (ot / den).astype(o_ref.dtype)
        lse_ref[0] = mx + jnp.log(den)

    blk_q = pl.BlockSpec((1, HEAD_DIM, tq), lambda h, i: (h, 0, i))
    blk_kv = pl.BlockSpec((1, sp, HEAD_DIM), lambda h, i: (h // rep, 0, 0))
    blk_h = pl.BlockSpec((1, 1, LANES), lambda h, i: (h, 0, 0))
    in_specs, args = [blk_q, blk_kv, blk_kv, blk_h], [qt, kp, vp, slopes]
    if sinks is not None:
        in_specs.append(blk_h)
        args.append(sinks)
    return pl.pallas_call(
        body, name=name, grid=(nh, s // tq), in_specs=in_specs,
        out_specs=[blk_q, pl.BlockSpec((1, 1, tq), lambda h, i: (h, 0, i))],
        out_shape=[jax.ShapeDtypeStruct((nh, HEAD_DIM, s), out_dtype), jax.ShapeDtypeStruct((nh, 1, s), F32)],
        compiler_params=_params(("parallel", "parallel")),
    )(*args)


def _win_bwd(qt, kp, vp, slopes, sinks, dot, ot, delta, *, window, dil, seg, name):
    nh, _, s = qt.shape
    nkv = kp.shape[0]
    rep = nh // nkv
    tq = _win_tq(s)
    tk = tq + 2 * WIN_PAD
    sp = s + 2 * WIN_PAD
    n_in = 6 + (sinks is not None)

    def body(*refs):
        qt_ref, k_ref, v_ref, sl_ref, dot_ref, aux_ref = refs[:6]
        outs = refs[n_in:]
        dq_ref, dk_ref, dv_ref = outs[:3]
        h, i = pl.program_id(0), pl.program_id(1)

        @pl.when((h % rep == 0) & (i == 0))
        def _():
            dk_ref[...] = jnp.zeros_like(dk_ref)
            dv_ref[...] = jnp.zeros_like(dv_ref)

        base = pl.multiple_of(i * tq, tq)
        win = pl.ds(base, tk)
        kw, q_t, do_t = k_ref[0, win, :], qt_ref[0], dot_ref[0]
        st = _win_scores(kw, q_t, i, tq, tk, window, dil, seg, sl_ref[0][:, :1])
        mx = jnp.max(st, axis=0, keepdims=True)
        if sinks is not None:
            sink = refs[6][0][:, :1]
            mx = jnp.maximum(mx, sink)
        p = jnp.exp(st - mx)
        den = jnp.sum(p, axis=0, keepdims=True)
        if sinks is not None:
            p_sink = jnp.exp(sink - mx)
            den = den + p_sink
        p = p / den
        dp = jnp.dot(v_ref[0, win, :], do_t, preferred_element_type=F32)
        if delta is None:
            row = jnp.sum(do_t.astype(F32) * aux_ref[0].astype(F32), axis=0, keepdims=True)
        else:
            row = aux_ref[0]
        ds = (p * (dp - row) * QK_SCALE).astype(BF16)
        nt = (((1,), (1,)), ((), ()))
        dv_ref[0, :, win] += lax.dot_general(do_t, p.astype(BF16), nt, preferred_element_type=F32)
        dk_ref[0, :, win] += lax.dot_general(q_t, ds, nt, preferred_element_type=F32)
        dq_ref[0] = lax.dot_general(kw, ds, (((0,), (0,)), ((), ())), preferred_element_type=F32)
        if sinks is not None:
            dsink_ref = outs[3]

            @pl.when(i == 0)
            def _():
                dsink_ref[...] = jnp.zeros_like(dsink_ref)

            dsink_ref[0] += jnp.zeros((1, LANES), F32) - jnp.sum(p_sink / den * row, axis=1, keepdims=True)

    blk_q = pl.BlockSpec((1, HEAD_DIM, tq), lambda h, i: (h, 0, i))
    blk_row = pl.BlockSpec((1, 1, tq), lambda h, i: (h, 0, i))
    blk_kv = pl.BlockSpec((1, sp, HEAD_DIM), lambda h, i: (h // rep, 0, 0))
    blk_acc = pl.BlockSpec((1, HEAD_DIM, sp), lambda h, i: (h // rep, 0, 0))
    blk_h = pl.BlockSpec((1, 1, LANES), lambda h, i: (h, 0, 0))
    in_specs = [blk_q, blk_kv, blk_kv, blk_h, blk_q, blk_q if delta is None else blk_row]
    args = [qt, kp, vp, slopes, dot, ot if delta is None else delta]
    out_specs = [blk_q, blk_acc, blk_acc]
    out_shape = [jax.ShapeDtypeStruct((nh, HEAD_DIM, s), F32), jax.ShapeDtypeStruct((nkv, HEAD_DIM, sp), F32),
                 jax.ShapeDtypeStruct((nkv, HEAD_DIM, sp), F32)]
    if sinks is not None:
        in_specs.append(blk_h)
        args.append(sinks)
        out_specs.append(blk_h)
        out_shape.append(jax.ShapeDtypeStruct((nh, 1, LANES), F32))
    res = pl.pallas_call(
        body, name=name, grid=(nh, s // tq), in_specs=in_specs, out_specs=out_specs, out_shape=out_shape,
        compiler_params=_params(("arbitrary", "arbitrary")),
    )(*args)
    return res if sinks is not None else (*res, None)


def _group_weights(lse):
    e = jnp.exp(lse - jnp.max(lse, axis=0, keepdims=True))
    return e / jnp.sum(e, axis=0, keepdims=True)


def _b_combine_fwd(ot, lse):
    nh, _, s = ot.shape
    ng, hg, _ = lse.shape
    ts = min(512, s)

    def body(ot_ref, lse_ref, o_ref):
        alpha = _group_weights(lse_ref[...])
        for g in range(ng):
            for j in range(hg):
                o_ref[g * hg + j] = (ot_ref[g * hg + j] * alpha[g, j:j + 1, :]).astype(BF16)

    return pl.pallas_call(
        body, name="b_combine_fwd", grid=(s // ts,),
        in_specs=[pl.BlockSpec((nh, HEAD_DIM, ts), lambda i: (0, 0, i)), pl.BlockSpec((ng, hg, ts), lambda i: (0, 0, i))],
        out_specs=pl.BlockSpec((nh, HEAD_DIM, ts), lambda i: (0, 0, i)),
        out_shape=jax.ShapeDtypeStruct((nh, HEAD_DIM, s), BF16),
        compiler_params=_params(("parallel",)),
    )(ot, lse)


def _b_combine_bwd(dout, ot, lse):
    nh, _, s = ot.shape
    ng, hg, _ = lse.shape
    ts = min(512, s)

    def body(dout_ref, ot_ref, lse_ref, do_ref, delta_ref):
        alpha = _group_weights(lse_ref[...])
        for j in range(hg):
            e = [jnp.sum(dout_ref[g * hg + j].astype(F32) * ot_ref[g * hg + j], axis=0, keepdims=True)
                 for g in range(ng)]
            a = [alpha[g, j:j + 1, :] for g in range(ng)]
            mix = a[0] * e[0]
            for g in range(1, ng):
                mix = mix + a[g] * e[g]
            for g in range(ng):
                do_ref[g * hg + j] = (dout_ref[g * hg + j].astype(F32) * a[g]).astype(BF16)
                delta_ref[g * hg + j] = a[g] * mix

    blk = pl.BlockSpec((nh, HEAD_DIM, ts), lambda i: (0, 0, i))
    return pl.pallas_call(
        body, name="b_combine_bwd", grid=(s // ts,),
        in_specs=[blk, blk, pl.BlockSpec((ng, hg, ts), lambda i: (0, 0, i))],
        out_specs=[blk, pl.BlockSpec((nh, 1, ts), lambda i: (0, 0, i))],
        out_shape=[jax.ShapeDtypeStruct((nh, HEAD_DIM, s), BF16), jax.ShapeDtypeStruct((nh, 1, s), F32)],
        compiler_params=_params(("parallel",)),
    )(dout, ot, lse)


def _alibi_slopes(n):
    return 2.0 ** (-8.0 * jnp.arange(1, n + 1, dtype=F32) / n)


def _per_head(v):
    return jnp.broadcast_to(v.astype(F32)[:, None, None], (v.shape[0], 1, LANES))


def _dilate(x, dil):
    if dil == 1:
        return x
    s = x.shape[-1]
    return jnp.swapaxes(x.reshape(x.shape[:-1] + (s // dil, dil)), -1, -2).reshape(x.shape)


def _undilate(x, dil):
    if dil == 1:
        return x
    s = x.shape[-1]
    return jnp.swapaxes(x.reshape(x.shape[:-1] + (dil, s // dil)), -1, -2).reshape(x.shape)


def _heads_t(x, nh):
    return jnp.transpose(x.reshape(x.shape[0], nh, HEAD_DIM), (1, 2, 0))


def _tokens(xt):
    return jnp.transpose(xt, (2, 0, 1)).reshape(xt.shape[2], -1)


def _pad_tokens(xt):
    return jnp.pad(jnp.swapaxes(xt, 1, 2), ((0, 0), (WIN_PAD, WIN_PAD), (0, 0)))


def _mixer_fwd(kind, qkv, p, tabs):
    s = qkv.shape[0]
    if kind == 0:
        qt, k, v = _a_prep(qkv, tabs[0], tabs[1], p["gq2"], p["gk2"])
        ot = _a_attn_fwd(qt, k, v)
        return _tokens(ot), dict(qt=qt, k=k, v=v, ot=ot)
    if kind == 2:
        nq, nk = C_HEADS * HEAD_DIM, C_KV * HEAD_DIM
        qt = _heads_t(qkv[:, :nq], C_HEADS)
        kp = _pad_tokens(_heads_t(qkv[:, nq:nq + nk], C_KV))
        vp = _pad_tokens(_heads_t(qkv[:, nq + nk:], C_KV))
        ot, _ = _win_fwd(qt, kp, vp, p["slopes"], p["sinks"], window=C_WINDOW, dil=1, seg=s, out_dtype=BF16,
                         name="c_attn_fwd")
        return _tokens(ot), dict(qt=qt, kp=kp, vp=vp, ot=ot)
    ng, hg, kg = len(B_GROUPS), B_HEADS_PER_GROUP, B_KV_PER_GROUP
    nq, nk = ng * hg * HEAD_DIM, ng * kg * HEAD_DIM
    qt_all = _heads_t(qkv[:, :nq], ng * hg)
    kt_all = _heads_t(qkv[:, nq:nq + nk], ng * kg)
    vt_all = _heads_t(qkv[:, nq + nk:], ng * kg)
    saved, outs, lses = [], [], []
    for g, (window, dil) in enumerate(B_GROUPS):
        qt = _dilate(qt_all[g * hg:(g + 1) * hg], dil)
        kp = _pad_tokens(_dilate(kt_all[g * kg:(g + 1) * kg], dil))
        vp = _pad_tokens(_dilate(vt_all[g * kg:(g + 1) * kg], dil))
        sl = p["slopes"][g * hg:(g + 1) * hg]
        ot, lse = _win_fwd(qt, kp, vp, sl, None, window=window // 2 // dil, dil=dil, seg=s // dil, out_dtype=F32,
                           name=f"b_attn_fwd_g{g}")
        saved.append(dict(qt=qt, kp=kp, vp=vp))
        outs.append(_undilate(ot, dil))
        lses.append(_undilate(lse[:, 0, :], dil))
    ot_all, lse_all = jnp.concatenate(outs, axis=0), jnp.stack(lses, axis=0)
    mixed = _b_combine_fwd(ot_all, lse_all)
    return _tokens(mixed), dict(groups=saved, ot=ot_all, lse=lse_all)


def _mixer_bwd(kind, do, qkv, sv, p, tabs):
    s = do.shape[0]
    small = {}
    if kind == 0:
        dqt, dkt, dvt = _a_attn_bwd(sv["qt"], sv["k"], sv["v"], _heads_t(do, A_HEADS), sv["ot"])
        dqkv, dgq, dgk = _a_prep_bwd(dqt, dkt, dvt, qkv, tabs[0], tabs[1], p["gq2"], p["gk2"])
        small["q_gain"] = dgq[0, :HEAD_DIM] + dgq[0, HEAD_DIM:]
        small["k_gain"] = dgk[0, :HEAD_DIM] + dgk[0, HEAD_DIM:]
        return dqkv, small
    if kind == 2:
        dqt, dkt, dvt, dsink = _win_bwd(sv["qt"], sv["kp"], sv["vp"], p["slopes"], p["sinks"], _heads_t(do, C_HEADS),
                                        sv["ot"], None, window=C_WINDOW, dil=1, seg=s, name="c_attn_bwd")
        small["sinks"] = dsink[:, 0, 0]
        parts = [dqt, dkt[:, :, WIN_PAD:-WIN_PAD], dvt[:, :, WIN_PAD:-WIN_PAD]]
        return jnp.concatenate([_tokens(x) for x in parts], axis=1).astype(BF16), small
    ng, hg, kg = len(B_GROUPS), B_HEADS_PER_GROUP, B_KV_PER_GROUP
    do_own, delta = _b_combine_bwd(_heads_t(do, ng * hg), sv["ot"], sv["lse"])
    dqs, dks, dvs = [], [], []
    for g, (window, dil) in enumerate(B_GROUPS):
        gs = sv["groups"][g]
        dqt, dkt, dvt, _ = _win_bwd(gs["qt"], gs["kp"], gs["vp"], p["slopes"][g * hg:(g + 1) * hg], None,
                                    _dilate(do_own[g * hg:(g + 1) * hg], dil), None,
                                    _dilate(delta[g * hg:(g + 1) * hg], dil),
                                    window=window // 2 // dil, dil=dil, seg=s // dil, name=f"b_attn_bwd_g{g}")
        dqs.append(_undilate(dqt, dil))
        dks.append(_undilate(dkt[:, :, WIN_PAD:-WIN_PAD], dil))
        dvs.append(_undilate(dvt[:, :, WIN_PAD:-WIN_PAD], dil))
    parts = [jnp.concatenate(x, axis=0) for x in (dqs, dks, dvs)]
    return jnp.concatenate([_tokens(x) for x in parts], axis=1).astype(BF16), small


def _local_step(x, target, norms, mats, mixer_params):
    s = x.shape[0]
    tabs = _rope_tables(s)
    h = x
    saved = []
    for layer in range(DEPTH):
        kind = layer % N_MIXERS
        w, p = mats[layer], mixer_params[layer]
        hn, qkv = _norm_mm(h, norms["attn"][layer][None], w["w_qkv"], out_dtype=F32 if kind == 0 else BF16,
                           relu2=False, name=f"qkv_proj_l{layer}")
        o, sv = _mixer_fwd(kind, qkv, p, tabs)
        h_mid = _mm_res(o, w["w_o"], h, name=f"o_proj_l{layer}")
        hn2, act = _norm_mm(h_mid, norms["mlp"][layer][None], w["w1"], out_dtype=BF16, relu2=True,
                            name=f"mlp_up_l{layer}")
        h_out = _mm_res(act, w["w2"], h_mid, name=f"mlp_down_l{layer}")
        saved.append(dict(h=h, hn=hn, qkv=qkv, o=o, mix=sv, h_mid=h_mid, hn2=hn2, act=act))
        h = h_out

    dh, loss, d_final = _loss_head(h, norms["final"][None], target)

    grads = [None] * DEPTH
    d_attn, d_mlp, small = [None] * DEPTH, [None] * DEPTH, [None] * DEPTH
    for layer in reversed(range(DEPTH)):
        kind = layer % N_MIXERS
        w, p, sv = mats[layer], mixer_params[layer], saved[layer]
        du = _mm_nt(dh, w["w2"], sv["act"], name=f"mlp_down_bwd_l{layer}")
        g_w2 = _mm_tn(sv["act"], dh, name=f"mlp_w2_grad_l{layer}")
        g_w1 = _mm_tn(sv["hn2"], du, name=f"mlp_w1_grad_l{layer}")
        dh_mid, d_mlp[layer] = _mm_nt_normbwd(du, w["w1"], sv["h_mid"], norms["mlp"][layer][None], dh,
                                              name=f"mlp_up_bwd_l{layer}")
        do = _mm_nt(dh_mid, w["w_o"], None, name=f"o_proj_bwd_l{layer}")
        g_wo = _mm_tn(sv["o"], dh_mid, name=f"w_o_grad_l{layer}")
        dqkv, small[layer] = _mixer_bwd(kind, do, sv["qkv"], sv["mix"], p, tabs)
        g_qkv = _mm_tn(sv["hn"], dqkv, name=f"w_qkv_grad_l{layer}")
        dh, d_attn[layer] = _mm_nt_normbwd(dqkv, w["w_qkv"], sv["h"], norms["attn"][layer][None], dh_mid,
                                           name=f"qkv_proj_bwd_l{layer}")
        grads[layer] = dict(w_qkv=g_qkv, w_o=g_wo, w1=g_w1, w2=g_w2)
    return loss, dh, grads, dict(attn=d_attn, mlp=d_mlp, final=d_final, mixer=small)


CHIP_FLIPS = ((1, 0), (0, 1), (1, 1))


def _gather_weights(shards):
    n = len(shards)

    def body(*refs):
        ins, outs = refs[:n], refs[n:2 * n]
        send_sems, recv_sems, local_sems = refs[2 * n:]
        x, y, c = lax.axis_index("x"), lax.axis_index("y"), lax.axis_index("c")
        me = 2 * x + y

        def copy(t, j, arriving):
            fx, fy = CHIP_FLIPS[j]
            px, py = (1 - x if fx else x), (1 - y if fy else y)
            return pltpu.make_async_remote_copy(
                src_ref=ins[t], dst_ref=outs[t].at[2 * px + py if arriving else me], send_sem=send_sems.at[t, j],
                recv_sem=recv_sems.at[t, j], device_id=(px, py, c), device_id_type=MESH)

        own = [pltpu.make_async_copy(ins[t], outs[t].at[me], local_sems.at[t]) for t in range(n)]
        for t in range(n):
            own[t].start()
            for j in range(len(CHIP_FLIPS)):
                copy(t, j, False).start()
        for t in range(n):
            for j in range(len(CHIP_FLIPS)):
                copy(t, j, True).wait_recv()
                copy(t, j, False).wait_send()
            own[t].wait()

    return pl.pallas_call(
        body, name="gather_weights", in_specs=[ANY] * n, out_specs=[ANY] * n,
        out_shape=[jax.ShapeDtypeStruct((4,) + a.shape, a.dtype) for a in shards],
        scratch_shapes=[pltpu.SemaphoreType.DMA((n, 3)), pltpu.SemaphoreType.DMA((n, 3)), pltpu.SemaphoreType.DMA((n,))],
    )(*shards)


def _scatter_grads(grads, small):
    n = len(grads)

    def body(*refs):
        ins, small_ref = refs[:n], refs[n]
        outs, all_ref = refs[n + 1:2 * n + 1], refs[2 * n + 1]
        send_sems, recv_sems, s_send, s_recv, local_sem = refs[2 * n + 2:]
        x, y, c = lax.axis_index("x"), lax.axis_index("y"), lax.axis_index("c")
        me = 4 * x + 2 * y + c

        def copy(t, j):
            fx, fy = CHIP_FLIPS[j]
            px, py = (1 - x if fx else x), (1 - y if fy else y)
            return pltpu.make_async_remote_copy(
                src_ref=ins[t].at[2 * px + py], dst_ref=outs[t].at[j], send_sem=send_sems.at[t, j],
                recv_sem=recv_sems.at[t, j], device_id=(px, py, c), device_id_type=MESH)

        def small_copy(r, arriving):
            fx, fy, fc = (r + 1) // 4, ((r + 1) // 2) % 2, (r + 1) % 2
            px, py, pc = (1 - x if fx else x), (1 - y if fy else y), (1 - c if fc else c)
            return pltpu.make_async_remote_copy(
                src_ref=small_ref, dst_ref=all_ref.at[4 * px + 2 * py + pc if arriving else me],
                send_sem=s_send.at[r], recv_sem=s_recv.at[r], device_id=(px, py, pc), device_id_type=MESH)

        own = pltpu.make_async_copy(small_ref, all_ref.at[me], local_sem)
        own.start()
        for r in range(7):
            small_copy(r, False).start()
        for t in range(n):
            for j in range(len(CHIP_FLIPS)):
                copy(t, j).start()
        for r in range(7):
            small_copy(r, True).wait_recv()
            small_copy(r, False).wait_send()
        own.wait()
        for t in range(n):
            for j in range(len(CHIP_FLIPS)):
                copy(t, j).wait()

    return pl.pallas_call(
        body, name="scatter_grads", in_specs=[ANY] * (n + 1), out_specs=[ANY] * (n + 1),
        out_shape=[jax.ShapeDtypeStruct((3,) + g.shape[1:], g.dtype) for g in grads]
        + [jax.ShapeDtypeStruct((8,) + small.shape, small.dtype)],
        scratch_shapes=[pltpu.SemaphoreType.DMA((n, 3)), pltpu.SemaphoreType.DMA((n, 3)),
                        pltpu.SemaphoreType.DMA((7,)), pltpu.SemaphoreType.DMA((7,)), pltpu.SemaphoreType.DMA],
    )(*grads, small)


def _swap_cores(parts):
    n = len(parts)

    def body(*refs):
        ins, outs = refs[:n], refs[n:2 * n]
        send_sems, recv_sems = refs[2 * n:]
        peer = (lax.axis_index("x"), lax.axis_index("y"), 1 - lax.axis_index("c"))
        copies = [pltpu.make_async_remote_copy(src_ref=ins[t], dst_ref=outs[t], send_sem=send_sems.at[t],
                                               recv_sem=recv_sems.at[t], device_id=peer, device_id_type=MESH)
                  for t in range(n)]
        for cp in copies:
            cp.start()
        for cp in copies:
            cp.wait()

    return pl.pallas_call(
        body, name="swap_cores", in_specs=[ANY] * n, out_specs=[ANY] * n,
        out_shape=[jax.ShapeDtypeStruct(a.shape, a.dtype) for a in parts],
        scratch_shapes=[pltpu.SemaphoreType.DMA((n,)), pltpu.SemaphoreType.DMA((n,))],
    )(*parts)


def _rows_tile(r):
    return 256 if r % 256 == 0 else r


def _sum_quarters(own, recv, *, name):
    r, c = own.shape
    tr = _rows_tile(r)

    def body(own_ref, recv_ref, o_ref):
        acc = own_ref[...].astype(F32)
        for j in range(3):
            acc = acc + recv_ref[j].astype(F32)
        o_ref[...] = acc

    return pl.pallas_call(
        body, name=name, grid=(r // tr,),
        in_specs=[pl.BlockSpec((tr, c), lambda i: (i, 0)), pl.BlockSpec((3, tr, c), lambda i: (0, i, 0))],
        out_specs=pl.BlockSpec((tr, c), lambda i: (i, 0)),
        out_shape=jax.ShapeDtypeStruct((r, c), F32),
        compiler_params=_params(("parallel",)),
    )(own, recv)


def _adamw(w, m, v, parts, *, name):
    r, c = w.shape
    tr = _rows_tile(r)
    c1, c2 = 1.0 - ADAM_B1 ** ADAM_STEP, 1.0 - ADAM_B2 ** ADAM_STEP
    n_parts = len(parts)

    def body(*refs):
        w_ref, m_ref, v_ref = refs[:3]
        g_ref, d_ref, nm_ref, nv_ref = refs[3 + n_parts:]
        terms = []
        for p_ref in refs[3:3 + n_parts]:
            terms += [p_ref[...]] if len(p_ref.shape) == 2 else [p_ref[j] for j in range(p_ref.shape[0])]
        g = terms[0]
        for term in terms[1:]:
            g = g + term
        m_new = ADAM_B1 * m_ref[...] + (1.0 - ADAM_B1) * g
        v_new = ADAM_B2 * v_ref[...] + (1.0 - ADAM_B2) * (g * g)
        step = (m_new / c1) / (jnp.sqrt(v_new / c2) + ADAM_EPS)
        g_ref[...] = g
        d_ref[...] = -ADAM_LR * (step + ADAM_WD * w_ref[...])
        nm_ref[...] = m_new
        nv_ref[...] = v_new

    blk = pl.BlockSpec((tr, c), lambda i: (i, 0))
    part_specs = [blk if p.ndim == 2 else pl.BlockSpec((p.shape[0], tr, c), lambda i: (0, i, 0)) for p in parts]
    return pl.pallas_call(
        body, name=name, grid=(r // tr,), in_specs=[blk, blk, blk] + part_specs,
        out_specs=[blk] * 4, out_shape=[jax.ShapeDtypeStruct((r, c), F32)] * 4,
        compiler_params=_params(("parallel",)),
    )(w, m, v, *parts)


MATS = ("a_w_qkv", "a_w_o", "b_w_qkv", "b_w_o", "c_w_qkv", "c_w_o", "mlp_w1", "mlp_w2")
COLUMN_SHARDED = ("a_w_qkv", "b_w_qkv", "c_w_qkv", "mlp_w1")
SMALLS = ("attn_norm", "mlp_norm", "a_q_gain", "a_k_gain", "c_sinks", "final_norm")
WEIGHTS = ("attn_norm", "mlp_norm", "a_w_qkv", "a_q_gain", "a_k_gain", "a_w_o", "b_w_qkv", "b_w_o", "c_w_qkv",
           "c_sinks", "c_w_o", "mlp_w1", "mlp_w2", "final_norm")
MIXER_OF_LAYER = tuple((layer % N_MIXERS, sum(1 for q in range(layer) if q % N_MIXERS == layer % N_MIXERS))
                       for layer in range(DEPTH))
SMALL_ROWS = 8


def _full_weight(name, gathered):
    if name in COLUMN_SHARDED:
        q, n, r, c = gathered.shape
        return jnp.transpose(gathered, (1, 2, 0, 3)).reshape(n, r, q * c)
    q, n, r, c = gathered.shape
    return jnp.transpose(gathered, (1, 0, 2, 3)).reshape(n, q * r, c)


def _quarters(name, g):
    r, c = g.shape
    if name in COLUMN_SHARDED:
        return jnp.transpose(g.reshape(r, 4, c // 4), (1, 0, 2))
    return g.reshape(4, r // 4, c)


def _pack_small(values):
    rows, spans, at = [], [], 0
    for v in values:
        flat = v.reshape(-1)
        n = -(-flat.shape[0] // (SMALL_ROWS * LANES)) * SMALL_ROWS
        rows.append(jnp.pad(flat, (0, n * LANES - flat.shape[0])).reshape(n, LANES))
        spans.append((at, n))
        at += n
    return jnp.concatenate(rows, axis=0), spans


def kernel(x, attn_norm, mlp_norm, a_w_qkv, a_q_gain, a_k_gain, a_w_o, b_w_qkv, b_w_o, c_w_qkv, c_sinks, c_w_o, mlp_w1, mlp_w2, final_norm, loss_target, m_attn_norm, m_mlp_norm, m_a_w_qkv, m_a_q_gain, m_a_k_gain, m_a_w_o, m_b_w_qkv, m_b_w_o, m_c_w_qkv, m_c_sinks, m_c_w_o, m_mlp_w1, m_mlp_w2, m_final_norm, v_attn_norm, v_mlp_norm, v_a_w_qkv, v_a_q_gain, v_a_k_gain, v_a_w_o, v_b_w_qkv, v_b_w_o, v_c_w_qkv, v_c_sinks, v_c_w_o, v_mlp_w1, v_mlp_w2, v_final_norm):
    env = dict(locals())
    w = {name: env[name] for name in WEIGHTS}
    mom = {name: (env["m_" + name], env["v_" + name]) for name in WEIGHTS}

    gathered = _gather_weights([w[name].astype(BF16) for name in MATS])
    full = {name: _full_weight(name, g) for name, g in zip(MATS, gathered)}
    prefix = ("a", "b", "c")
    mats, mixer_params = [], []
    for layer, (kind, j) in enumerate(MIXER_OF_LAYER):
        mats.append(dict(w_qkv=full[prefix[kind] + "_w_qkv"][j], w_o=full[prefix[kind] + "_w_o"][j],
                         w1=full["mlp_w1"][layer], w2=full["mlp_w2"][layer]))
        if kind == 0:
            mixer_params.append(dict(gq2=jnp.tile(a_q_gain[j], 2)[None], gk2=jnp.tile(a_k_gain[j], 2)[None]))
        elif kind == 1:
            mixer_params.append(dict(slopes=_per_head(_alibi_slopes(len(B_GROUPS) * B_HEADS_PER_GROUP))))
        else:
            mixer_params.append(dict(slopes=_per_head(_alibi_slopes(C_HEADS)), sinks=_per_head(c_sinks[j])))

    norms = dict(attn=attn_norm, mlp=mlp_norm, final=final_norm)
    loss_part, grad_x, g_layers, g_small = _local_step(x[0], loss_target[0], norms, mats, mixer_params)
    loss = lax.psum(loss_part[0, 0], ("x", "y", "c"))

    stacked = {}
    for name in MATS:
        key = name[2:] if name[0] in "abc" else name[4:]
        layers = [layer for layer, (kind, _) in enumerate(MIXER_OF_LAYER)
                  if name.startswith("mlp") or prefix[kind] == name[0]]
        stacked[name] = jnp.stack([_quarters(name, g_layers[layer][key]) for layer in layers], axis=1)
    of_kind = lambda kind, key: jnp.stack([g_small["mixer"][layer][key] for layer, (k, _) in enumerate(MIXER_OF_LAYER)
                                           if k == kind])
    small_grads = dict(
        attn_norm=jnp.concatenate(g_small["attn"], axis=0), mlp_norm=jnp.concatenate(g_small["mlp"], axis=0),
        a_q_gain=of_kind(0, "q_gain"), a_k_gain=of_kind(0, "k_gain"), c_sinks=of_kind(2, "sinks"),
        final_norm=g_small["final"][0])
    packed, spans = _pack_small([small_grads[name] for name in SMALLS])
    grads4 = [stacked[name].reshape(4, -1, stacked[name].shape[-1]) for name in MATS]
    *received, all_small = _scatter_grads(grads4, packed)

    me_chip = 2 * lax.axis_index("x") + lax.axis_index("y")
    partial = [_sum_quarters(lax.dynamic_index_in_dim(g, me_chip, axis=0, keepdims=False), r, name=f"sum_{name}")
               for name, g, r in zip(MATS, grads4, received)]
    other = _swap_cores(partial)

    out = {}
    for name, mine, theirs in zip(MATS, partial, other):
        shape = w[name].shape
        res = _adamw(*[a.reshape(-1, shape[-1]) for a in (w[name], *mom[name])], [mine, theirs], name=f"adamw_{name}")
        out[name] = [a.reshape(shape) for a in res]
    for name, (at, n) in zip(SMALLS, spans):
        shape = w[name].shape
        packed_in = [_pack_small([a])[0] for a in (w[name], *mom[name])]
        res = _adamw(*packed_in, [all_small[:, at:at + n]], name=f"adamw_{name}")
        out[name] = [a.reshape(-1)[:w[name].size].reshape(shape) for a in res]

    return (loss, grad_x[None], *[out[name][0] for name in WEIGHTS], *[out[name][1] for name in WEIGHTS],
            *[out[name][2] for name in WEIGHTS], *[out[name][3] for name in WEIGHTS])
```

```python
import jax
import jax.numpy as jnp
from jax import lax
from jax.experimental import pallas as pl
from jax.experimental.pallas import tpu as pltpu

F32 = jnp.float32
BF16 = jnp.bfloat16
MESH = pl.DeviceIdType.MESH
ANY = pl.BlockSpec(memory_space=pl.ANY)

D_MODEL = 1024
HEAD_DIM = 64
GRID_W = 64
ROPE_THETA = 10000.0
RMS_EPS = 1e-6
QK_SCALE = HEAD_DIM ** -0.5
LOG2E = 1.4426950408889634
LN2 = 0.6931471805599453
A_HEADS, A_KV = 16, 4
B_GROUPS = ((128, 1), (512, 4), (2048, 16))
B_HEADS_PER_GROUP, B_KV_PER_GROUP = 6, 2
C_HEADS, C_KV, C_WINDOW = 16, 4, 128
DEPTH, N_MIXERS = 4, 3
ADAM_LR, ADAM_B1, ADAM_B2, ADAM_EPS, ADAM_WD, ADAM_STEP = 0.001, 0.9, 0.999, 1e-08, 0.01, 10

WIN_PAD = 128
NEG = -1e30
V7X_VMEM_BUDGET = 48 * 1024 * 1024
LANES = 128
ROW_TILE = 1024


def _params(semantics):
    return pltpu.CompilerParams(dimension_semantics=semantics, vmem_limit_bytes=V7X_VMEM_BUDGET)


def _tile(n, cap):
    if n <= cap:
        return n
    t = (cap // LANES) * LANES
    while n % t:
        t -= LANES
    return t


def _norm_mm(h, gain, w, *, out_dtype, relu2, name):
    m, d = h.shape
    n = w.shape[1]
    tm, tn = min(ROW_TILE, m), _tile(n, 2048)

    def body(h_ref, g_ref, w_ref, hn_ref, y_ref):
        @pl.when(pl.program_id(1) == 0)
        def _():
            x = h_ref[...]
            r = lax.rsqrt(jnp.mean(x * x, axis=-1, keepdims=True) + RMS_EPS)
            hn_ref[...] = (x * r * g_ref[...]).astype(BF16)

        y = jnp.dot(hn_ref[...], w_ref[...], preferred_element_type=F32)
        if relu2:
            y = jnp.maximum(y, 0.0)
            y = y * y
        y_ref[...] = y.astype(y_ref.dtype)

    return pl.pallas_call(
        body, name=name, grid=(m // tm, n // tn),
        in_specs=[pl.BlockSpec((tm, d), lambda i, j: (i, 0)), pl.BlockSpec((1, d), lambda i, j: (0, 0)),
                  pl.BlockSpec((d, tn), lambda i, j: (0, j))],
        out_specs=[pl.BlockSpec((tm, d), lambda i, j: (i, 0)), pl.BlockSpec((tm, tn), lambda i, j: (i, j))],
        out_shape=[jax.ShapeDtypeStruct((m, d), BF16), jax.ShapeDtypeStruct((m, n), out_dtype)],
        compiler_params=_params(("parallel", "arbitrary")),
    )(h, gain, w)


def _mm_res(a, w, h_in, *, name):
    m, k = a.shape
    d = w.shape[1]
    tm, tk = min(ROW_TILE, m), _tile(k, 1152)

    def body(a_ref, w_ref, h_ref, o_ref):
        @pl.when(pl.program_id(1) == 0)
        def _():
            o_ref[...] = h_ref[...]

        o_ref[...] += jnp.dot(a_ref[...], w_ref[...], preferred_element_type=F32)

    return pl.pallas_call(
        body, name=name, grid=(m // tm, k // tk),
        in_specs=[pl.BlockSpec((tm, tk), lambda i, j: (i, j)), pl.BlockSpec((tk, d), lambda i, j: (j, 0)),
                  pl.BlockSpec((tm, d), lambda i, j: (i, 0))],
        out_specs=pl.BlockSpec((tm, d), lambda i, j: (i, 0)),
        out_shape=jax.ShapeDtypeStruct((m, d), F32),
        compiler_params=_params(("parallel", "arbitrary")),
    )(a, w, h_in)


def _mm_nt(a, w, act, *, name):
    m, d = a.shape
    n = w.shape[0]
    tm, tn = min(ROW_TILE, m), _tile(n, 1152)

    def body(*refs):
        a_ref, w_ref = refs[0], refs[1]
        o_ref = refs[-1]
        acc = lax.dot_general(a_ref[...].astype(BF16), w_ref[...], (((1,), (1,)), ((), ())),
                              preferred_element_type=F32)
        if act is not None:
            acc = acc * (2.0 * jnp.sqrt(refs[2][...].astype(F32)))
        o_ref[...] = acc.astype(BF16)

    in_specs = [pl.BlockSpec((tm, d), lambda i, j: (i, 0)), pl.BlockSpec((tn, d), lambda i, j: (j, 0))]
    args = [a, w]
    if act is not None:
        in_specs.append(pl.BlockSpec((tm, tn), lambda i, j: (i, j)))
        args.append(act)
    return pl.pallas_call(
        body, name=name, grid=(m // tm, n // tn), in_specs=in_specs,
        out_specs=pl.BlockSpec((tm, tn), lambda i, j: (i, j)),
        out_shape=jax.ShapeDtypeStruct((m, n), BF16),
        compiler_params=_params(("parallel", "parallel")),
    )(*args)


def _rmsnorm_bwd(dn, x, gain):
    r = lax.rsqrt(jnp.mean(x * x, axis=-1, keepdims=True) + RMS_EPS)
    xh = x * r
    dgain = jnp.sum(dn * xh, axis=0, keepdims=True)
    u = dn * gain
    dx = r * (u - xh * jnp.mean(u * xh, axis=-1, keepdims=True))
    return dx, dgain


def _mm_nt_normbwd(g, w, h, gain, dh_in, *, name):
    m, k = g.shape
    d = w.shape[0]
    tm, tk = min(ROW_TILE // 2, m), _tile(k, 1024)
    nk = k // tk

    def body(g_ref, w_ref, h_ref, gain_ref, dh_ref, o_ref, dg_ref, acc_ref):
        i, j = pl.program_id(0), pl.program_id(1)

        @pl.when((i == 0) & (j == 0))
        def _():
            dg_ref[...] = jnp.zeros_like(dg_ref)

        @pl.when(j == 0)
        def _():
            acc_ref[...] = jnp.zeros_like(acc_ref)

        acc_ref[...] += lax.dot_general(g_ref[...], w_ref[...], (((1,), (1,)), ((), ())),
                                        preferred_element_type=F32)

        @pl.when(j == nk - 1)
        def _():
            dx, dgain = _rmsnorm_bwd(acc_ref[...], h_ref[...], gain_ref[...])
            dg_ref[...] += dgain
            o_ref[...] = dh_ref[...] + dx

    return pl.pallas_call(
        body, name=name, grid=(m // tm, nk),
        in_specs=[pl.BlockSpec((tm, tk), lambda i, j: (i, j)), pl.BlockSpec((d, tk), lambda i, j: (0, j)),
                  pl.BlockSpec((tm, d), lambda i, j: (i, 0)), pl.BlockSpec((1, d), lambda i, j: (0, 0)),
                  pl.BlockSpec((tm, d), lambda i, j: (i, 0))],
        out_specs=[pl.BlockSpec((tm, d), lambda i, j: (i, 0)), pl.BlockSpec((1, d), lambda i, j: (0, 0))],
        out_shape=[jax.ShapeDtypeStruct((m, d), F32), jax.ShapeDtypeStruct((1, d), F32)],
        scratch_shapes=[pltpu.VMEM((tm, d), F32)],
        compiler_params=_params(("arbitrary", "arbitrary")),
    )(g, w, h, gain, dh_in)


def _mm_tn(x, g, *, name):
    m, k = x.shape
    n = g.shape[1]
    tm, tk, tn = min(ROW_TILE, m), _tile(k, 1152), _tile(n, 1024)
    nm = m // tm

    def body(x_ref, g_ref, o_ref, acc_ref):
        s = pl.program_id(2)

        @pl.when(s == 0)
        def _():
            acc_ref[...] = jnp.zeros_like(acc_ref)

        acc_ref[...] += lax.dot_general(x_ref[...], g_ref[...].astype(BF16), (((0,), (0,)), ((), ())),
                                        preferred_element_type=F32)

        @pl.when(s == nm - 1)
        def _():
            o_ref[...] = acc_ref[...].astype(BF16)

    return pl.pallas_call(
        body, name=name, grid=(k // tk, n // tn, nm),
        in_specs=[pl.BlockSpec((tm, tk), lambda a, b, s: (s, a)), pl.BlockSpec((tm, tn), lambda a, b, s: (s, b))],
        out_specs=pl.BlockSpec((tk, tn), lambda a, b, s: (a, b)),
        out_shape=jax.ShapeDtypeStruct((k, n), BF16),
        scratch_shapes=[pltpu.VMEM((tk, tn), F32)],
        compiler_params=_params(("parallel", "parallel", "arbitrary")),
    )(x, g)


def _loss_head(h, gain, target):
    m, d = h.shape
    tm = 512

    def body(h_ref, g_ref, t_ref, dh_ref, loss_ref, dg_ref):
        @pl.when(pl.program_id(0) == 0)
        def _():
            loss_ref[...] = jnp.zeros_like(loss_ref)
            dg_ref[...] = jnp.zeros_like(dg_ref)

        x = h_ref[...]
        gain_v = g_ref[...]
        r = lax.rsqrt(jnp.mean(x * x, axis=-1, keepdims=True) + RMS_EPS)
        err = x * r * gain_v - t_ref[...]
        loss_ref[...] += 0.5 * jnp.sum(jnp.mean(err * err, axis=-1, keepdims=True), axis=0, keepdims=True)
        dx, dgain = _rmsnorm_bwd(err * (1.0 / d), x, gain_v)
        dg_ref[...] += dgain
        dh_ref[...] = dx

    return pl.pallas_call(
        body, name="loss_head", grid=(m // tm,),
        in_specs=[pl.BlockSpec((tm, d), lambda i: (i, 0)), pl.BlockSpec((1, d), lambda i: (0, 0)),
                  pl.BlockSpec((tm, d), lambda i: (i, 0))],
        out_specs=[pl.BlockSpec((tm, d), lambda i: (i, 0)), pl.BlockSpec((1, LANES), lambda i: (0, 0)),
                   pl.BlockSpec((1, d), lambda i: (0, 0))],
        out_shape=[jax.ShapeDtypeStruct((m, d), F32), jax.ShapeDtypeStruct((1, LANES), F32),
                   jax.ShapeDtypeStruct((1, d), F32)],
        compiler_params=_params(("arbitrary",)),
    )(h, gain, target)


def _rope_tables(s):
    t = jnp.arange(s)
    row = (t // GRID_W).astype(F32)
    col = (t % GRID_W).astype(F32)
    axis_dim = HEAD_DIM // 2
    inv_freq = ROPE_THETA ** (-jnp.arange(0, axis_dim, 2, dtype=F32) / axis_dim)
    ar, ac = row[:, None] * inv_freq, col[:, None] * inv_freq
    cos = jnp.concatenate([jnp.cos(ar), jnp.cos(ar), jnp.cos(ac), jnp.cos(ac)], axis=-1)
    sin = jnp.concatenate([-jnp.sin(ar), jnp.sin(ar), -jnp.sin(ac), jnp.sin(ac)], axis=-1)
    return jnp.tile(cos, (1, 2)), jnp.tile(sin, (1, 2))


def _swap16(x):
    lane = lax.broadcasted_iota(jnp.int32, x.shape, 1)
    return jnp.where((lane % 32) < 16, pltpu.roll(x, LANES - 16, 1), pltpu.roll(x, 16, 1))


def _head_mean(v):
    lane = lax.broadcasted_iota(jnp.int32, v.shape, 1)
    lo = lane < HEAD_DIM
    s_all = jnp.sum(v, axis=-1, keepdims=True)
    s_lo = jnp.sum(jnp.where(lo, v, 0.0), axis=-1, keepdims=True)
    return jnp.where(lo, s_lo, s_all - s_lo) * (1.0 / HEAD_DIM)


def _norm_rope(x, gain2, cos, sin):
    r = lax.rsqrt(_head_mean(x * x) + RMS_EPS)
    nrm = x * r * gain2
    return nrm * cos + _swap16(nrm) * sin


def _norm_rope_bwd(dy, x, gain2, cos, sin):
    dn = dy * cos + _swap16(dy * sin)
    r = lax.rsqrt(_head_mean(x * x) + RMS_EPS)
    xh = x * r
    dgain = jnp.sum(dn * xh, axis=0, keepdims=True)
    u = dn * gain2
    return r * (u - xh * _head_mean(u * xh)), dgain


def _a_prep(qkv, cos, sin, gq2, gk2):
    s = qkv.shape[0]
    tr = 256
    nq, nk = A_HEADS * HEAD_DIM, A_KV * HEAD_DIM

    def body(qkv_ref, cos_ref, sin_ref, gq_ref, gk_ref, qt_ref, k_ref, v_ref):
        cos_v, sin_v = cos_ref[...], sin_ref[...]
        for c in range(nq // LANES):
            y = _norm_rope(qkv_ref[:, c * LANES:(c + 1) * LANES], gq_ref[...], cos_v, sin_v) * (QK_SCALE * LOG2E)
            yt = y.T
            qt_ref[2 * c] = yt[:HEAD_DIM].astype(BF16)
            qt_ref[2 * c + 1] = yt[HEAD_DIM:].astype(BF16)
        for c in range(nk // LANES):
            y = _norm_rope(qkv_ref[:, nq + c * LANES:nq + (c + 1) * LANES], gk_ref[...], cos_v, sin_v)
            k_ref[2 * c] = y[:, :HEAD_DIM].astype(BF16)
            k_ref[2 * c + 1] = y[:, HEAD_DIM:].astype(BF16)
            x = qkv_ref[:, nq + nk + c * LANES:nq + nk + (c + 1) * LANES]
            v_ref[2 * c] = x[:, :HEAD_DIM].astype(BF16)
            v_ref[2 * c + 1] = x[:, HEAD_DIM:].astype(BF16)

    return pl.pallas_call(
        body, name="a_prep", grid=(s // tr,),
        in_specs=[pl.BlockSpec((tr, nq + 2 * nk), lambda i: (i, 0)), pl.BlockSpec((tr, LANES), lambda i: (i, 0)),
                  pl.BlockSpec((tr, LANES), lambda i: (i, 0)), pl.BlockSpec((1, LANES), lambda i: (0, 0)),
                  pl.BlockSpec((1, LANES), lambda i: (0, 0))],
        out_specs=[pl.BlockSpec((A_HEADS, HEAD_DIM, tr), lambda i: (0, 0, i)),
                   pl.BlockSpec((A_KV, tr, HEAD_DIM), lambda i: (0, i, 0)),
                   pl.BlockSpec((A_KV, tr, HEAD_DIM), lambda i: (0, i, 0))],
        out_shape=[jax.ShapeDtypeStruct((A_HEADS, HEAD_DIM, s), BF16), jax.ShapeDtypeStruct((A_KV, s, HEAD_DIM), BF16),
                   jax.ShapeDtypeStruct((A_KV, s, HEAD_DIM), BF16)],
        compiler_params=_params(("parallel",)),
    )(qkv, cos, sin, gq2, gk2)


def _a_prep_bwd(dqt, dkt, dvt, qkv, cos, sin, gq2, gk2):
    s = qkv.shape[0]
    tr = 256
    nq, nk = A_HEADS * HEAD_DIM, A_KV * HEAD_DIM

    def body(dqt_ref, dkt_ref, dvt_ref, qkv_ref, cos_ref, sin_ref, gq_ref, gk_ref, o_ref, dgq_ref, dgk_ref):
        @pl.when(pl.program_id(0) == 0)
        def _():
            dgq_ref[...] = jnp.zeros_like(dgq_ref)
            dgk_ref[...] = jnp.zeros_like(dgk_ref)

        cos_v, sin_v = cos_ref[...], sin_ref[...]

        def pair(ref, c):
            return jnp.concatenate([ref[2 * c], ref[2 * c + 1]], axis=0).T

        for c in range(nq // LANES):
            dx, dg = _norm_rope_bwd(pair(dqt_ref, c) * QK_SCALE, qkv_ref[:, c * LANES:(c + 1) * LANES],
                                    gq_ref[...], cos_v, sin_v)
            o_ref[:, c * LANES:(c + 1) * LANES] = dx.astype(BF16)
            dgq_ref[...] += dg
        for c in range(nk // LANES):
            lo = nq + c * LANES
            dx, dg = _norm_rope_bwd(pair(dkt_ref, c) * LN2, qkv_ref[:, lo:lo + LANES], gk_ref[...], cos_v, sin_v)
            o_ref[:, lo:lo + LANES] = dx.astype(BF16)
            dgk_ref[...] += dg
            o_ref[:, lo + nk:lo + nk + LANES] = pair(dvt_ref, c).astype(BF16)

    return pl.pallas_call(
        body, name="a_prep_bwd", grid=(s // tr,),
        in_specs=[pl.BlockSpec((A_HEADS, HEAD_DIM, tr), lambda i: (0, 0, i)),
                  pl.BlockSpec((A_KV, HEAD_DIM, tr), lambda i: (0, 0, i)),
                  pl.BlockSpec((A_KV, HEAD_DIM, tr), lambda i: (0, 0, i)),
                  pl.BlockSpec((tr, nq + 2 * nk), lambda i: (i, 0)), pl.BlockSpec((tr, LANES), lambda i: (i, 0)),
                  pl.BlockSpec((tr, LANES), lambda i: (i, 0)), pl.BlockSpec((1, LANES), lambda i: (0, 0)),
                  pl.BlockSpec((1, LANES), lambda i: (0, 0))],
        out_specs=[pl.BlockSpec((tr, nq + 2 * nk), lambda i: (i, 0)), pl.BlockSpec((1, LANES), lambda i: (0, 0)),
                   pl.BlockSpec((1, LANES), lambda i: (0, 0))],
        out_shape=[jax.ShapeDtypeStruct((s, nq + 2 * nk), BF16), jax.ShapeDtypeStruct((1, LANES), F32),
                   jax.ShapeDtypeStruct((1, LANES), F32)],
        compiler_params=_params(("arbitrary",)),
    )(dqt, dkt, dvt, qkv, cos, sin, gq2, gk2)


A_TQ = 256
A_KEY_CHUNK = 512


def _a_attn_fwd(qt, k, v):
    nh, _, s = qt.shape
    rep = nh // k.shape[0]
    tq = min(A_TQ, s)

    def body(qt_ref, k_ref, v_ref, o_ref, lse_ref):
        st = jnp.dot(k_ref[0], qt_ref[0], preferred_element_type=F32)
        mx = jnp.max(st, axis=0, keepdims=True)
        p = jnp.exp2(st - mx)
        den = jnp.sum(p, axis=0, keepdims=True)
        ot = lax.dot_general(v_ref[0], p.astype(BF16), (((0,), (0,)), ((), ())), preferred_element_type=F32)
        o_ref[0] = (ot / den).astype(BF16)
        lse_ref[0] = mx + jnp.log(den) * LOG2E

    return pl.pallas_call(
        body, name="a_attn_fwd", grid=(nh, s // tq),
        in_specs=[pl.BlockSpec((1, HEAD_DIM, tq), lambda h, i: (h, 0, i)),
                  pl.BlockSpec((1, s, HEAD_DIM), lambda h, i: (h // rep, 0, 0)),
                  pl.BlockSpec((1, s, HEAD_DIM), lambda h, i: (h // rep, 0, 0))],
        out_specs=[pl.BlockSpec((1, HEAD_DIM, tq), lambda h, i: (h, 0, i)),
                   pl.BlockSpec((1, 1, tq), lambda h, i: (h, 0, i))],
        out_shape=[jax.ShapeDtypeStruct((nh, HEAD_DIM, s), BF16), jax.ShapeDtypeStruct((nh, 1, s), F32)],
        compiler_params=_params(("parallel", "parallel")),
    )(qt, k, v)


def _a_attn_bwd(qt, k, v, dot, ot, lse):
    nh, _, s = qt.shape
    nkv = k.shape[0]
    rep = nh // nkv
    tq, ck = min(A_TQ, s), min(A_KEY_CHUNK, s)

    def body(qt_ref, k_ref, v_ref, dot_ref, ot_ref, lse_ref, dq_ref, dk_ref, dv_ref):
        h, i = pl.program_id(0), pl.program_id(1)

        @pl.when((h % rep == 0) & (i == 0))
        def _():
            dk_ref[...] = jnp.zeros_like(dk_ref)
            dv_ref[...] = jnp.zeros_like(dv_ref)

        q_t, do_t, lse_v = qt_ref[0], dot_ref[0], lse_ref[0]
        delta = jnp.sum(do_t.astype(F32) * ot_ref[0].astype(F32), axis=0, keepdims=True)
        nt = (((1,), (1,)), ((), ()))
        dq = jnp.zeros((HEAD_DIM, tq), F32)
        for c in range(s // ck):
            keys = slice(c * ck, (c + 1) * ck)
            kc = k_ref[0, keys, :]
            p = jnp.exp2(jnp.dot(kc, q_t, preferred_element_type=F32) - lse_v)
            dp = jnp.dot(v_ref[0, keys, :], do_t, preferred_element_type=F32)
            ds = (p * (dp - delta)).astype(BF16)
            dv_ref[0, :, keys] += lax.dot_general(do_t, p.astype(BF16), nt, preferred_element_type=F32)
            dk_ref[0, :, keys] += lax.dot_general(q_t, ds, nt, preferred_element_type=F32)
            dq = dq + lax.dot_general(kc, ds, (((0,), (0,)), ((), ())), preferred_element_type=F32)
        dq_ref[0] = dq

    blk_q = pl.BlockSpec((1, HEAD_DIM, tq), lambda h, i: (h, 0, i))
    blk_row = pl.BlockSpec((1, 1, tq), lambda h, i: (h, 0, i))
    blk_kv = pl.BlockSpec((1, s, HEAD_DIM), lambda h, i: (h // rep, 0, 0))
    blk_acc = pl.BlockSpec((1, HEAD_DIM, s), lambda h, i: (h // rep, 0, 0))
    return pl.pallas_call(
        body, name="a_attn_bwd", grid=(nh, s // tq),
        in_specs=[blk_q, blk_kv, blk_kv, blk_q, blk_q, blk_row],
        out_specs=[blk_q, blk_acc, blk_acc],
        out_shape=[jax.ShapeDtypeStruct((nh, HEAD_DIM, s), F32), jax.ShapeDtypeStruct((nkv, HEAD_DIM, s), F32),
                   jax.ShapeDtypeStruct((nkv, HEAD_DIM, s), F32)],
        compiler_params=_params(("arbitrary", "arbitrary")),
    )(qt, k, v, dot, ot, lse)


def _win_scores(kw, q_t, i, tq, tk, window, dil, seg, slope):
    st = jnp.dot(kw, q_t, preferred_element_type=F32)
    qpos = i * tq + lax.broadcasted_iota(jnp.int32, (tk, tq), 1)
    kpos = i * tq - WIN_PAD + lax.broadcasted_iota(jnp.int32, (tk, tq), 0)
    dist = jnp.abs(kpos - qpos)
    seg_lo = qpos - (qpos & (seg - 1))
    valid = (dist <= window) & (kpos >= seg_lo) & (kpos < seg_lo + seg)
    return jnp.where(valid, st * QK_SCALE - slope * (dist * dil).astype(F32), NEG)


def _win_tq(s):
    return min(512, s)


def _win_fwd(qt, kp, vp, slopes, sinks, *, window, dil, seg, out_dtype, name):
    nh, _, s = qt.shape
    rep = nh // kp.shape[0]
    tq = _win_tq(s)
    tk = tq + 2 * WIN_PAD
    sp = s + 2 * WIN_PAD

    def body(*refs):
        qt_ref, k_ref, v_ref, sl_ref = refs[:4]
        o_ref, lse_ref = refs[-2:]
        i = pl.program_id(1)
        base = pl.multiple_of(i * tq, tq)
        st = _win_scores(k_ref[0, pl.ds(base, tk), :], qt_ref[0], i, tq, tk, window, dil, seg, sl_ref[0][:, :1])
        mx = jnp.max(st, axis=0, keepdims=True)
        if sinks is not None:
            sink = refs[4][0][:, :1]
            mx = jnp.maximum(mx, sink)
        p = jnp.exp(st - mx)
        den = jnp.sum(p, axis=0, keepdims=True)
        if sinks is not None:
            den = den + jnp.exp(sink - mx)
        ot = lax.dot_general(v_ref[0, pl.ds(base, tk), :], p.astype(BF16), (((0,), (0,)), ((), ())),
                             preferred_element_type=F32)
        o_ref[0] = (ot / den).astype(o_ref.dtype)
        lse_ref[0] = mx + jnp.log(den)

    blk_q = pl.BlockSpec((1, HEAD_DIM, tq), lambda h, i: (h, 0, i))
    blk_kv = pl.BlockSpec((1, sp, HEAD_DIM), lambda h, i: (h // rep, 0, 0))
    blk_h = pl.BlockSpec((1, 1, LANES), lambda h, i: (h, 0, 0))
    in_specs, args = [blk_q, blk_kv, blk_kv, blk_h], [qt, kp, vp, slopes]
    if sinks is not None:
        in_specs.append(blk_h)
        args.append(sinks)
    return pl.pallas_call(
        body, name=name, grid=(nh, s // tq), in_specs=in_specs,
        out_specs=[blk_q, pl.BlockSpec((1, 1, tq), lambda h, i: (h, 0, i))],
        out_shape=[jax.ShapeDtypeStruct((nh, HEAD_DIM, s), out_dtype), jax.ShapeDtypeStruct((nh, 1, s), F32)],
        compiler_params=_params(("parallel", "parallel")),
    )(*args)


def _win_bwd(qt, kp, vp, slopes, sinks, dot, ot, delta, *, window, dil, seg, name):
    nh, _, s = qt.shape
    nkv = kp.shape[0]
    rep = nh // nkv
    tq = _win_tq(s)
    tk = tq + 2 * WIN_PAD
    sp = s + 2 * WIN_PAD
    n_in = 6 + (sinks is not None)

    def body(*refs):
        qt_ref, k_ref, v_ref, sl_ref, dot_ref, aux_ref = refs[:6]
        outs = refs[n_in:]
        dq_ref, dk_ref, dv_ref = outs[:3]
        h, i = pl.program_id(0), pl.program_id(1)

        @pl.when((h % rep == 0) & (i == 0))
        def _():
            dk_ref[...] = jnp.zeros_like(dk_ref)
            dv_ref[...] = jnp.zeros_like(dv_ref)

        base = pl.multiple_of(i * tq, tq)
        win = pl.ds(base, tk)
        kw, q_t, do_t = k_ref[0, win, :], qt_ref[0], dot_ref[0]
        st = _win_scores(kw, q_t, i, tq, tk, window, dil, seg, sl_ref[0][:, :1])
        mx = jnp.max(st, axis=0, keepdims=True)
        if sinks is not None:
            sink = refs[6][0][:, :1]
            mx = jnp.maximum(mx, sink)
        p = jnp.exp(st - mx)
        den = jnp.sum(p, axis=0, keepdims=True)
        if sinks is not None:
            p_sink = jnp.exp(sink - mx)
            den = den + p_sink
        p = p / den
        dp = jnp.dot(v_ref[0, win, :], do_t, preferred_element_type=F32)
        if delta is None:
            row = jnp.sum(do_t.astype(F32) * aux_ref[0].astype(F32), axis=0, keepdims=True)
        else:
            row = aux_ref[0]
        ds = (p * (dp - row) * QK_SCALE).astype(BF16)
        nt = (((1,), (1,)), ((), ()))
        dv_ref[0, :, win] += lax.dot_general(do_t, p.astype(BF16), nt, preferred_element_type=F32)
        dk_ref[0, :, win] += lax.dot_general(q_t, ds, nt, preferred_element_type=F32)
        dq_ref[0] = lax.dot_general(kw, ds, (((0,), (0,)), ((), ())), preferred_element_type=F32)
        if sinks is not None:
            dsink_ref = outs[3]

            @pl.when(i == 0)
            def _():
                dsink_ref[...] = jnp.zeros_like(dsink_ref)

            dsink_ref[0] += jnp.zeros((1, LANES), F32) - jnp.sum(p_sink / den * row, axis=1, keepdims=True)

    blk_q = pl.BlockSpec((1, HEAD_DIM, tq), lambda h, i: (h, 0, i))
    blk_row = pl.BlockSpec((1, 1, tq), lambda h, i: (h, 0, i))
    blk_kv = pl.BlockSpec((1, sp, HEAD_DIM), lambda h, i: (h // rep, 0, 0))
    blk_acc = pl.BlockSpec((1, HEAD_DIM, sp), lambda h, i: (h // rep, 0, 0))
    blk_h = pl.BlockSpec((1, 1, LANES), lambda h, i: (h, 0, 0))
    in_specs = [blk_q, blk_kv, blk_kv, blk_h, blk_q, blk_q if delta is None else blk_row]
    args = [qt, kp, vp, slopes, dot, ot if delta is None else delta]
    out_specs = [blk_q, blk_acc, blk_acc]
    out_shape = [jax.ShapeDtypeStruct((nh, HEAD_DIM, s), F32), jax.ShapeDtypeStruct((nkv, HEAD_DIM, sp), F32),
                 jax.ShapeDtypeStruct((nkv, HEAD_DIM, sp), F32)]
    if sinks is not None:
        in_specs.append(blk_h)
        args.append(sinks)
        out_specs.append(blk_h)
        out_shape.append(jax.ShapeDtypeStruct((nh, 1, LANES), F32))
    res = pl.pallas_call(
        body, name=name, grid=(nh, s // tq), in_specs=in_specs, out_specs=out_specs, out_shape=out_shape,
        compiler_params=_params(("arbitrary", "arbitrary")),
    )(*args)
    return res if sinks is not None else (*res, None)


def _group_weights(lse):
    e = jnp.exp(lse - jnp.max(lse, axis=0, keepdims=True))
    return e / jnp.sum(e, axis=0, keepdims=True)


def _b_combine_fwd(ot, lse):
    nh, _, s = ot.shape
    ng, hg, _ = lse.shape
    ts = min(512, s)

    def body(ot_ref, lse_ref, o_ref):
        alpha = _group_weights(lse_ref[...])
        for g in range(ng):
            for j in range(hg):
                o_ref[g * hg + j] = (ot_ref[g * hg + j] * alpha[g, j:j + 1, :]).astype(BF16)

    return pl.pallas_call(
        body, name="b_combine_fwd", grid=(s // ts,),
        in_specs=[pl.BlockSpec((nh, HEAD_DIM, ts), lambda i: (0, 0, i)), pl.BlockSpec((ng, hg, ts), lambda i: (0, 0, i))],
        out_specs=pl.BlockSpec((nh, HEAD_DIM, ts), lambda i: (0, 0, i)),
        out_shape=jax.ShapeDtypeStruct((nh, HEAD_DIM, s), BF16),
        compiler_params=_params(("parallel",)),
    )(ot, lse)


def _b_combine_bwd(dout, ot, lse):
    nh, _, s = ot.shape
    ng, hg, _ = lse.shape
    ts = min(512, s)

    def body(dout_ref, ot_ref, lse_ref, do_ref, delta_ref):
        alpha = _group_weights(lse_ref[...])
        for j in range(hg):
            e = [jnp.sum(dout_ref[g * hg + j].astype(F32) * ot_ref[g * hg + j], axis=0, keepdims=True)
                 for g in range(ng)]
            a = [alpha[g, j:j + 1, :] for g in range(ng)]
            mix = a[0] * e[0]
            for g in range(1, ng):
                mix = mix + a[g] * e[g]
            for g in range(ng):
                do_ref[g * hg + j] = (dout_ref[g * hg + j].astype(F32) * a[g]).astype(BF16)
                delta_ref[g * hg + j] = a[g] * mix

    blk = pl.BlockSpec((nh, HEAD_DIM, ts), lambda i: (0, 0, i))
    return pl.pallas_call(
        body, name="b_combine_bwd", grid=(s // ts,),
        in_specs=[blk, blk, pl.BlockSpec((ng, hg, ts), lambda i: (0, 0, i))],
        out_specs=[blk, pl.BlockSpec((nh, 1, ts), lambda i: (0, 0, i))],
        out_shape=[jax.ShapeDtypeStruct((nh, HEAD_DIM, s), BF16), jax.ShapeDtypeStruct((nh, 1, s), F32)],
        compiler_params=_params(("parallel",)),
    )(dout, ot, lse)


def _alibi_slopes(n):
    return 2.0 ** (-8.0 * jnp.arange(1, n + 1, dtype=F32) / n)


def _per_head(v):
    return jnp.broadcast_to(v.astype(F32)[:, None, None], (v.shape[0], 1, LANES))


def _dilate(x, dil):
    if dil == 1:
        return x
    s = x.shape[-1]
    return jnp.swapaxes(x.reshape(x.shape[:-1] + (s // dil, dil)), -1, -2).reshape(x.shape)


def _undilate(x, dil):
    if dil == 1:
        return x
    s = x.shape[-1]
    return jnp.swapaxes(x.reshape(x.shape[:-1] + (dil, s // dil)), -1, -2).reshape(x.shape)


def _heads_t(x, nh):
    return jnp.transpose(x.reshape(x.shape[0], nh, HEAD_DIM), (1, 2, 0))


def _tokens(xt):
    return jnp.transpose(xt, (2, 0, 1)).reshape(xt.shape[2], -1)


def _pad_tokens(xt):
    return jnp.pad(jnp.swapaxes(xt, 1, 2), ((0, 0), (WIN_PAD, WIN_PAD), (0, 0)))


def _mixer_fwd(kind, qkv, p, tabs):
    s = qkv.shape[0]
    if kind == 0:
        qt, k, v = _a_prep(qkv, tabs[0], tabs[1], p["gq2"], p["gk2"])
        ot, lse = _a_attn_fwd(qt, k, v)
        return _tokens(ot), dict(qt=qt, k=k, v=v, ot=ot, lse=lse)
    if kind == 2:
        nq, nk = C_HEADS * HEAD_DIM, C_KV * HEAD_DIM
        qt = _heads_t(qkv[:, :nq], C_HEADS)
        kp = _pad_tokens(_heads_t(qkv[:, nq:nq + nk], C_KV))
        vp = _pad_tokens(_heads_t(qkv[:, nq + nk:], C_KV))
        ot, _ = _win_fwd(qt, kp, vp, p["slopes"], p["sinks"], window=C_WINDOW, dil=1, seg=s, out_dtype=BF16,
                         name="c_attn_fwd")
        return _tokens(ot), dict(qt=qt, kp=kp, vp=vp, ot=ot)
    ng, hg, kg = len(B_GROUPS), B_HEADS_PER_GROUP, B_KV_PER_GROUP
    nq, nk = ng * hg * HEAD_DIM, ng * kg * HEAD_DIM
    qt_all = _heads_t(qkv[:, :nq], ng * hg)
    kt_all = _heads_t(qkv[:, nq:nq + nk], ng * kg)
    vt_all = _heads_t(qkv[:, nq + nk:], ng * kg)
    saved, outs, lses = [], [], []
    for g, (window, dil) in enumerate(B_GROUPS):
        qt = _dilate(qt_all[g * hg:(g + 1) * hg], dil)
        kp = _pad_tokens(_dilate(kt_all[g * kg:(g + 1) * kg], dil))
        vp = _pad_tokens(_dilate(vt_all[g * kg:(g + 1) * kg], dil))
        sl = p["slopes"][g * hg:(g + 1) * hg]
        ot, lse = _win_fwd(qt, kp, vp, sl, None, window=window // 2 // dil, dil=dil, seg=s // dil, out_dtype=F32,
                           name=f"b_attn_fwd_g{g}")
        saved.append(dict(qt=qt, kp=kp, vp=vp))
        outs.append(_undilate(ot, dil))
        lses.append(_undilate(lse[:, 0, :], dil))
    ot_all, lse_all = jnp.concatenate(outs, axis=0), jnp.stack(lses, axis=0)
    mixed = _b_combine_fwd(ot_all, lse_all)
    return _tokens(mixed), dict(groups=saved, ot=ot_all, lse=lse_all)


def _mixer_bwd(kind, do, qkv, sv, p, tabs):
    s = do.shape[0]
    small = {}
    if kind == 0:
        dqt, dkt, dvt = _a_attn_bwd(sv["qt"], sv["k"], sv["v"], _heads_t(do, A_HEADS), sv["ot"], sv["lse"])
        dqkv, dgq, dgk = _a_prep_bwd(dqt, dkt, dvt, qkv, tabs[0], tabs[1], p["gq2"], p["gk2"])
        small["q_gain"] = dgq[0, :HEAD_DIM] + dgq[0, HEAD_DIM:]
        small["k_gain"] = dgk[0, :HEAD_DIM] + dgk[0, HEAD_DIM:]
        return dqkv, small
    if kind == 2:
        dqt, dkt, dvt, dsink = _win_bwd(sv["qt"], sv["kp"], sv["vp"], p["slopes"], p["sinks"], _heads_t(do, C_HEADS),
                                        sv["ot"], None, window=C_WINDOW, dil=1, seg=s, name="c_attn_bwd")
        small["sinks"] = dsink[:, 0, 0]
        parts = [dqt, dkt[:, :, WIN_PAD:-WIN_PAD], dvt[:, :, WIN_PAD:-WIN_PAD]]
        return jnp.concatenate([_tokens(x) for x in parts], axis=1).astype(BF16), small
    ng, hg, kg = len(B_GROUPS), B_HEADS_PER_GROUP, B_KV_PER_GROUP
    do_own, delta = _b_combine_bwd(_heads_t(do, ng * hg), sv["ot"], sv["lse"])
    dqs, dks, dvs = [], [], []
    for g, (window, dil) in enumerate(B_GROUPS):
        gs = sv["groups"][g]
        dqt, dkt, dvt, _ = _win_bwd(gs["qt"], gs["kp"], gs["vp"], p["slopes"][g * hg:(g + 1) * hg], None,
                                    _dilate(do_own[g * hg:(g + 1) * hg], dil), None,
                                    _dilate(delta[g * hg:(g + 1) * hg], dil),
                                    window=window // 2 // dil, dil=dil, seg=s // dil, name=f"b_attn_bwd_g{g}")
        dqs.append(_undilate(dqt, dil))
        dks.append(_undilate(dkt[:, :, WIN_PAD:-WIN_PAD], dil))
        dvs.append(_undilate(dvt[:, :, WIN_PAD:-WIN_PAD], dil))
    parts = [jnp.concatenate(x, axis=0) for x in (dqs, dks, dvs)]
    return jnp.concatenate([_tokens(x) for x in parts], axis=1).astype(BF16), small


def _local_step(x, target, norms, mats, mixer_params):
    s = x.shape[0]
    tabs = _rope_tables(s)
    h = x
    saved = []
    for layer in range(DEPTH):
        kind = layer % N_MIXERS
        w, p = mats[layer], mixer_params[layer]
        hn, qkv = _norm_mm(h, norms["attn"][layer][None], w["w_qkv"], out_dtype=F32 if kind == 0 else BF16,
                           relu2=False, name=f"qkv_proj_l{layer}")
        o, sv = _mixer_fwd(kind, qkv, p, tabs)
        h_mid = _mm_res(o, w["w_o"], h, name=f"o_proj_l{layer}")
        hn2, act = _norm_mm(h_mid, norms["mlp"][layer][None], w["w1"], out_dtype=BF16, relu2=True,
                            name=f"mlp_up_l{layer}")
        h_out = _mm_res(act, w["w2"], h_mid, name=f"mlp_down_l{layer}")
        saved.append(dict(h=h, hn=hn, qkv=qkv, o=o, mix=sv, h_mid=h_mid, hn2=hn2, act=act))
        h = h_out

    dh, loss, d_final = _loss_head(h, norms["final"][None], target)

    grads = [None] * DEPTH
    d_attn, d_mlp, small = [None] * DEPTH, [None] * DEPTH, [None] * DEPTH
    for layer in reversed(range(DEPTH)):
        kind = layer % N_MIXERS
        w, p, sv = mats[layer], mixer_params[layer], saved[layer]
        du = _mm_nt(dh, w["w2"], sv["act"], name=f"mlp_down_bwd_l{layer}")
        g_w2 = _mm_tn(sv["act"], dh, name=f"mlp_w2_grad_l{layer}")
        g_w1 = _mm_tn(sv["hn2"], du, name=f"mlp_w1_grad_l{layer}")
        dh_mid, d_mlp[layer] = _mm_nt_normbwd(du, w["w1"], sv["h_mid"], norms["mlp"][layer][None], dh,
                                              name=f"mlp_up_bwd_l{layer}")
        do = _mm_nt(dh_mid, w["w_o"], None, name=f"o_proj_bwd_l{layer}")
        g_wo = _mm_tn(sv["o"], dh_mid, name=f"w_o_grad_l{layer}")
        dqkv, small[layer] = _mixer_bwd(kind, do, sv["qkv"], sv["mix"], p, tabs)
        g_qkv = _mm_tn(sv["hn"], dqkv, name=f"w_qkv_grad_l{layer}")
        dh, d_attn[layer] = _mm_nt_normbwd(dqkv, w["w_qkv"], sv["h"], norms["attn"][layer][None], dh_mid,
                                           name=f"qkv_proj_bwd_l{layer}")
        grads[layer] = dict(w_qkv=g_qkv, w_o=g_wo, w1=g_w1, w2=g_w2)
    return loss, dh, grads, dict(attn=d_attn, mlp=d_mlp, final=d_final, mixer=small)


CHIP_FLIPS = ((1, 0), (0, 1), (1, 1))


def _gather_weights(shards):
    n = len(shards)

    def body(*refs):
        ins, outs = refs[:n], refs[n:2 * n]
        send_sems, recv_sems, local_sems = refs[2 * n:]
        x, y, c = lax.axis_index("x"), lax.axis_index("y"), lax.axis_index("c")
        me = 2 * x + y

        def copy(t, j, arriving):
            fx, fy = CHIP_FLIPS[j]
            px, py = (1 - x if fx else x), (1 - y if fy else y)
            return pltpu.make_async_remote_copy(
                src_ref=ins[t], dst_ref=outs[t].at[2 * px + py if arriving else me], send_sem=send_sems.at[t, j],
                recv_sem=recv_sems.at[t, j], device_id=(px, py, c), device_id_type=MESH)

        own = [pltpu.make_async_copy(ins[t], outs[t].at[me], local_sems.at[t]) for t in range(n)]
        for t in range(n):
            own[t].start()
            for j in range(len(CHIP_FLIPS)):
                copy(t, j, False).start()
        for t in range(n):
            for j in range(len(CHIP_FLIPS)):
                copy(t, j, True).wait_recv()
                copy(t, j, False).wait_send()
            own[t].wait()

    return pl.pallas_call(
        body, name="gather_weights", in_specs=[ANY] * n, out_specs=[ANY] * n,
        out_shape=[jax.ShapeDtypeStruct((4,) + a.shape, a.dtype) for a in shards],
        scratch_shapes=[pltpu.SemaphoreType.DMA((n, 3)), pltpu.SemaphoreType.DMA((n, 3)), pltpu.SemaphoreType.DMA((n,))],
    )(*shards)


def _scatter_grads(grads, small):
    n = len(grads)

    def body(*refs):
        ins, small_ref = refs[:n], refs[n]
        outs, all_ref = refs[n + 1:2 * n + 1], refs[2 * n + 1]
        send_sems, recv_sems, s_send, s_recv, local_sem = refs[2 * n + 2:]
        x, y, c = lax.axis_index("x"), lax.axis_index("y"), lax.axis_index("c")
        me = 4 * x + 2 * y + c

        def copy(t, j):
            fx, fy = CHIP_FLIPS[j]
            px, py = (1 - x if fx else x), (1 - y if fy else y)
            return pltpu.make_async_remote_copy(
                src_ref=ins[t].at[2 * px + py], dst_ref=outs[t].at[j], send_sem=send_sems.at[t, j],
                recv_sem=recv_sems.at[t, j], device_id=(px, py, c), device_id_type=MESH)

        def small_copy(r, arriving):
            fx, fy, fc = (r + 1) // 4, ((r + 1) // 2) % 2, (r + 1) % 2
            px, py, pc = (1 - x if fx else x), (1 - y if fy else y), (1 - c if fc else c)
            return pltpu.make_async_remote_copy(
                src_ref=small_ref, dst_ref=all_ref.at[4 * px + 2 * py + pc if arriving else me],
                send_sem=s_send.at[r], recv_sem=s_recv.at[r], device_id=(px, py, pc), device_id_type=MESH)

        own = pltpu.make_async_copy(small_ref, all_ref.at[me], local_sem)
        own.start()
        for r in range(7):
            small_copy(r, False).start()
        for t in range(n):
            for j in range(len(CHIP_FLIPS)):
                copy(t, j).start()
        for r in range(7):
            small_copy(r, True).wait_recv()
            small_copy(r, False).wait_send()
        own.wait()
        for t in range(n):
            for j in range(len(CHIP_FLIPS)):
                copy(t, j).wait()

    return pl.pallas_call(
        body, name="scatter_grads", in_specs=[ANY] * (n + 1), out_specs=[ANY] * (n + 1),
        out_shape=[jax.ShapeDtypeStruct((3,) + g.shape[1:], g.dtype) for g in grads]
        + [jax.ShapeDtypeStruct((8,) + small.shape, small.dtype)],
        scratch_shapes=[pltpu.SemaphoreType.DMA((n, 3)), pltpu.SemaphoreType.DMA((n, 3)),
                        pltpu.SemaphoreType.DMA((7,)), pltpu.SemaphoreType.DMA((7,)), pltpu.SemaphoreType.DMA],
    )(*grads, small)


def _swap_cores(parts):
    n = len(parts)

    def body(*refs):
        ins, outs = refs[:n], refs[n:2 * n]
        send_sems, recv_sems = refs[2 * n:]
        peer = (lax.axis_index("x"), lax.axis_index("y"), 1 - lax.axis_index("c"))
        copies = [pltpu.make_async_remote_copy(src_ref=ins[t], dst_ref=outs[t], send_sem=send_sems.at[t],
                                               recv_sem=recv_sems.at[t], device_id=peer, device_id_type=MESH)
                  for t in range(n)]
        for cp in copies:
            cp.start()
        for cp in copies:
            cp.wait()

    return pl.pallas_call(
        body, name="swap_cores", in_specs=[ANY] * n, out_specs=[ANY] * n,
        out_shape=[jax.ShapeDtypeStruct(a.shape, a.dtype) for a in parts],
        scratch_shapes=[pltpu.SemaphoreType.DMA((n,)), pltpu.SemaphoreType.DMA((n,))],
    )(*parts)


def _rows_tile(r):
    return 256 if r % 256 == 0 else r


def _sum_quarters(own, recv, *, name):
    r, c = own.shape
    tr = _rows_tile(r)

    def body(own_ref, recv_ref, o_ref):
        acc = own_ref[...].astype(F32)
        for j in range(3):
            acc = acc + recv_ref[j].astype(F32)
        o_ref[...] = acc

    return pl.pallas_call(
        body, name=name, grid=(r // tr,),
        in_specs=[pl.BlockSpec((tr, c), lambda i: (i, 0)), pl.BlockSpec((3, tr, c), lambda i: (0, i, 0))],
        out_specs=pl.BlockSpec((tr, c), lambda i: (i, 0)),
        out_shape=jax.ShapeDtypeStruct((r, c), F32),
        compiler_params=_params(("parallel",)),
    )(own, recv)


def _adamw(w, m, v, parts, *, name):
    r, c = w.shape
    tr = _rows_tile(r)
    c1, c2 = 1.0 - ADAM_B1 ** ADAM_STEP, 1.0 - ADAM_B2 ** ADAM_STEP
    n_parts = len(parts)

    def body(*refs):
        w_ref, m_ref, v_ref = refs[:3]
        g_ref, d_ref, nm_ref, nv_ref = refs[3 + n_parts:]
        terms = []
        for p_ref in refs[3:3 + n_parts]:
            terms += [p_ref[...]] if len(p_ref.shape) == 2 else [p_ref[j] for j in range(p_ref.shape[0])]
        g = terms[0]
        for term in terms[1:]:
            g = g + term
        m_new = ADAM_B1 * m_ref[...] + (1.0 - ADAM_B1) * g
        v_new = ADAM_B2 * v_ref[...] + (1.0 - ADAM_B2) * (g * g)
        step = (m_new / c1) / (jnp.sqrt(v_new / c2) + ADAM_EPS)
        g_ref[...] = g
        d_ref[...] = -ADAM_LR * (step + ADAM_WD * w_ref[...])
        nm_ref[...] = m_new
        nv_ref[...] = v_new

    blk = pl.BlockSpec((tr, c), lambda i: (i, 0))
    part_specs = [blk if p.ndim == 2 else pl.BlockSpec((p.shape[0], tr, c), lambda i: (0, i, 0)) for p in parts]
    return pl.pallas_call(
        body, name=name, grid=(r // tr,), in_specs=[blk, blk, blk] + part_specs,
        out_specs=[blk] * 4, out_shape=[jax.ShapeDtypeStruct((r, c), F32)] * 4,
        compiler_params=_params(("parallel",)),
    )(w, m, v, *parts)


MATS = ("a_w_qkv", "a_w_o", "b_w_qkv", "b_w_o", "c_w_qkv", "c_w_o", "mlp_w1", "mlp_w2")
COLUMN_SHARDED = ("a_w_qkv", "b_w_qkv", "c_w_qkv", "mlp_w1")
SMALLS = ("attn_norm", "mlp_norm", "a_q_gain", "a_k_gain", "c_sinks", "final_norm")
WEIGHTS = ("attn_norm", "mlp_norm", "a_w_qkv", "a_q_gain", "a_k_gain", "a_w_o", "b_w_qkv", "b_w_o", "c_w_qkv",
           "c_sinks", "c_w_o", "mlp_w1", "mlp_w2", "final_norm")
MIXER_OF_LAYER = tuple((layer % N_MIXERS, sum(1 for q in range(layer) if q % N_MIXERS == layer % N_MIXERS))
                       for layer in range(DEPTH))
SMALL_ROWS = 8


def _full_weight(name, gathered):
    if name in COLUMN_SHARDED:
        q, n, r, c = gathered.shape
        return jnp.transpose(gathered, (1, 2, 0, 3)).reshape(n, r, q * c)
    q, n, r, c = gathered.shape
    return jnp.transpose(gathered, (1, 0, 2, 3)).reshape(n, q * r, c)


def _quarters(name, g):
    r, c = g.shape
    if name in COLUMN_SHARDED:
        return jnp.transpose(g.reshape(r, 4, c // 4), (1, 0, 2))
    return g.reshape(4, r // 4, c)


def _pack_small(values):
    rows, spans, at = [], [], 0
    for v in values:
        flat = v.reshape(-1)
        n = -(-flat.shape[0] // (SMALL_ROWS * LANES)) * SMALL_ROWS
        rows.append(jnp.pad(flat, (0, n * LANES - flat.shape[0])).reshape(n, LANES))
        spans.append((at, n))
        at += n
    return jnp.concatenate(rows, axis=0), spans


def kernel(x, attn_norm, mlp_norm, a_w_qkv, a_q_gain, a_k_gain, a_w_o, b_w_qkv, b_w_o, c_w_qkv, c_sinks, c_w_o, mlp_w1, mlp_w2, final_norm, loss_target, m_attn_norm, m_mlp_norm, m_a_w_qkv, m_a_q_gain, m_a_k_gain, m_a_w_o, m_b_w_qkv, m_b_w_o, m_c_w_qkv, m_c_sinks, m_c_w_o, m_mlp_w1, m_mlp_w2, m_final_norm, v_attn_norm, v_mlp_norm, v_a_w_qkv, v_a_q_gain, v_a_k_gain, v_a_w_o, v_b_w_qkv, v_b_w_o, v_c_w_qkv, v_c_sinks, v_c_w_o, v_mlp_w1, v_mlp_w2, v_final_norm):
    env = dict(locals())
    w = {name: env[name] for name in WEIGHTS}
    mom = {name: (env["m_" + name], env["v_" + name]) for name in WEIGHTS}

    gathered = _gather_weights([w[name].astype(BF16) for name in MATS])
    full = {name: _full_weight(name, g) for name, g in zip(MATS, gathered)}
    prefix = ("a", "b", "c")
    mats, mixer_params = [], []
    for layer, (kind, j) in enumerate(MIXER_OF_LAYER):
        mats.append(dict(w_qkv=full[prefix[kind] + "_w_qkv"][j], w_o=full[prefix[kind] + "_w_o"][j],
                         w1=full["mlp_w1"][layer], w2=full["mlp_w2"][layer]))
        if kind == 0:
            mixer_params.append(dict(gq2=jnp.tile(a_q_gain[j], 2)[None], gk2=jnp.tile(a_k_gain[j], 2)[None]))
        elif kind == 1:
            mixer_params.append(dict(slopes=_per_head(_alibi_slopes(len(B_GROUPS) * B_HEADS_PER_GROUP))))
        else:
            mixer_params.append(dict(slopes=_per_head(_alibi_slopes(C_HEADS)), sinks=_per_head(c_sinks[j])))

    norms = dict(attn=attn_norm, mlp=mlp_norm, final=final_norm)
    loss_part, grad_x, g_layers, g_small = _local_step(x[0], loss_target[0], norms, mats, mixer_params)
    loss = lax.psum(loss_part[0, 0], ("x", "y", "c"))

    stacked = {}
    for name in MATS:
        key = name[2:] if name[0] in "abc" else name[4:]
        layers = [layer for layer, (kind, _) in enumerate(MIXER_OF_LAYER)
                  if name.startswith("mlp") or prefix[kind] == name[0]]
        stacked[name] = jnp.stack([_quarters(name, g_layers[layer][key]) for layer in layers], axis=1)
    of_kind = lambda kind, key: jnp.stack([g_small["mixer"][layer][key] for layer, (k, _) in enumerate(MIXER_OF_LAYER)
                                           if k == kind])
    small_grads = dict(
        attn_norm=jnp.concatenate(g_small["attn"], axis=0), mlp_norm=jnp.concatenate(g_small["mlp"], axis=0),
        a_q_gain=of_kind(0, "q_gain"), a_k_gain=of_kind(0, "k_gain"), c_sinks=of_kind(2, "sinks"),
        final_norm=g_small["final"][0])
    packed, spans = _pack_small([small_grads[name] for name in SMALLS])
    grads4 = [stacked[name].reshape(4, -1, stacked[name].shape[-1]) for name in MATS]
    *received, all_small = _scatter_grads(grads4, packed)

    me_chip = 2 * lax.axis_index("x") + lax.axis_index("y")
    partial = [_sum_quarters(lax.dynamic_index_in_dim(g, me_chip, axis=0, keepdims=False), r, name=f"sum_{name}")
               for name, g, r in zip(MATS, grads4, received)]
    other = _swap_cores(partial)

    out = {}
    for name, mine, theirs in zip(MATS, partial, other):
        shape = w[name].shape
        res = _adamw(*[a.reshape(-1, shape[-1]) for a in (w[name], *mom[name])], [mine, theirs], name=f"adamw_{name}")
        out[name] = [a.reshape(shape) for a in res]
    for name, (at, n) in zip(SMALLS, spans):
        shape = w[name].shape
        packed_in = [_pack_small([a])[0] for a in (w[name], *mom[name])]
        res = _adamw(*packed_in, [all_small[:, at:at + n]], name=f"adamw_{name}")
        out[name] = [a.reshape(-1)[:w[name].size].reshape(shape) for a in res]

    return (loss, grad_x[None], *[out[name][0] for name in WEIGHTS], *[out[name][1] for name in WEIGHTS],
            *[out[name][2] for name in WEIGHTS], *[out[name][3] for name in WEIGHTS])
```

```python
from typing import Callable, NamedTuple

import jax
import jax.numpy as jnp
from jax import lax
from jax.experimental import pallas as pl
from jax.experimental.pallas import tpu as pltpu

F32 = jnp.float32
BF16 = jnp.bfloat16
MESH = pl.DeviceIdType.MESH
ANY = pl.BlockSpec(memory_space=pl.ANY)

D_MODEL = 1024
HEAD_DIM = 64
GRID_W = 64
ROPE_THETA = 10000.0
RMS_EPS = 1e-6
QK_SCALE = HEAD_DIM ** -0.5
LOG2E = 1.4426950408889634
LN2 = 0.6931471805599453
A_HEADS, A_KV = 16, 4
B_GROUPS = ((128, 1), (512, 4), (2048, 16))
B_HEADS_PER_GROUP, B_KV_PER_GROUP = 6, 2
C_HEADS, C_KV, C_WINDOW = 16, 4, 128
DEPTH, N_MIXERS = 4, 3
ADAM_LR, ADAM_B1, ADAM_B2, ADAM_EPS, ADAM_WD, ADAM_STEP = 0.001, 0.9, 0.999, 1e-08, 0.01, 10

WIN_PAD = 128
NEG = -1e30
V7X_VMEM_BUDGET = 48 * 1024 * 1024
LANES = 128
ROW_TILE = 1024


def _params(semantics):
    return pltpu.CompilerParams(dimension_semantics=semantics, vmem_limit_bytes=V7X_VMEM_BUDGET)


def _tile(n, cap):
    if n <= cap:
        return n
    t = (cap // LANES) * LANES
    while n % t:
        t -= LANES
    return t


def _norm_mm(h, gain, w, *, out_dtype, relu2, name):
    m, d = h.shape
    n = w.shape[1]
    tm, tn = min(ROW_TILE, m), _tile(n, 2048)

    def body(h_ref, g_ref, w_ref, hn_ref, y_ref):
        @pl.when(pl.program_id(1) == 0)
        def _():
            x = h_ref[...]
            r = lax.rsqrt(jnp.mean(x * x, axis=-1, keepdims=True) + RMS_EPS)
            hn_ref[...] = (x * r * g_ref[...]).astype(BF16)

        y = jnp.dot(hn_ref[...], w_ref[...], preferred_element_type=F32)
        if relu2:
            y = jnp.maximum(y, 0.0)
            y = y * y
        y_ref[...] = y.astype(y_ref.dtype)

    return pl.pallas_call(
        body, name=name, grid=(m // tm, n // tn),
        in_specs=[pl.BlockSpec((tm, d), lambda i, j: (i, 0)), pl.BlockSpec((1, d), lambda i, j: (0, 0)),
                  pl.BlockSpec((d, tn), lambda i, j: (0, j))],
        out_specs=[pl.BlockSpec((tm, d), lambda i, j: (i, 0)), pl.BlockSpec((tm, tn), lambda i, j: (i, j))],
        out_shape=[jax.ShapeDtypeStruct((m, d), BF16), jax.ShapeDtypeStruct((m, n), out_dtype)],
        compiler_params=_params(("parallel", "arbitrary")),
    )(h, gain, w)


def _mm_res(a, w, h_in, *, name):
    m, k = a.shape
    d = w.shape[1]
    tm, tk = min(ROW_TILE, m), _tile(k, 1152)

    def body(a_ref, w_ref, h_ref, o_ref):
        @pl.when(pl.program_id(1) == 0)
        def _():
            o_ref[...] = h_ref[...]

        o_ref[...] += jnp.dot(a_ref[...], w_ref[...], preferred_element_type=F32)

    return pl.pallas_call(
        body, name=name, grid=(m // tm, k // tk),
        in_specs=[pl.BlockSpec((tm, tk), lambda i, j: (i, j)), pl.BlockSpec((tk, d), lambda i, j: (j, 0)),
                  pl.BlockSpec((tm, d), lambda i, j: (i, 0))],
        out_specs=pl.BlockSpec((tm, d), lambda i, j: (i, 0)),
        out_shape=jax.ShapeDtypeStruct((m, d), F32),
        compiler_params=_params(("parallel", "arbitrary")),
    )(a, w, h_in)


def _mm_nt(a, w, act, *, name):
    m, d = a.shape
    n = w.shape[0]
    tm, tn = min(ROW_TILE, m), _tile(n, 1152)

    def body(*refs):
        a_ref, w_ref = refs[0], refs[1]
        o_ref = refs[-1]
        acc = lax.dot_general(a_ref[...].astype(BF16), w_ref[...], (((1,), (1,)), ((), ())),
                              preferred_element_type=F32)
        if act is not None:
            acc = acc * (2.0 * jnp.sqrt(refs[2][...].astype(F32)))
        o_ref[...] = acc.astype(BF16)

    in_specs = [pl.BlockSpec((tm, d), lambda i, j: (i, 0)), pl.BlockSpec((tn, d), lambda i, j: (j, 0))]
    args = [a, w]
    if act is not None:
        in_specs.append(pl.BlockSpec((tm, tn), lambda i, j: (i, j)))
        args.append(act)
    return pl.pallas_call(
        body, name=name, grid=(m // tm, n // tn), in_specs=in_specs,
        out_specs=pl.BlockSpec((tm, tn), lambda i, j: (i, j)),
        out_shape=jax.ShapeDtypeStruct((m, n), BF16),
        compiler_params=_params(("parallel", "parallel")),
    )(*args)


def _rmsnorm_bwd(dn, x, gain):
    r = lax.rsqrt(jnp.mean(x * x, axis=-1, keepdims=True) + RMS_EPS)
    xh = x * r
    dgain = jnp.sum(dn * xh, axis=0, keepdims=True)
    u = dn * gain
    dx = r * (u - xh * jnp.mean(u * xh, axis=-1, keepdims=True))
    return dx, dgain


def _mm_nt_normbwd(g, w, h, gain, dh_in, *, name):
    m, k = g.shape
    d = w.shape[0]
    tm, tk = min(ROW_TILE // 2, m), _tile(k, 1024)
    nk = k // tk

    def body(g_ref, w_ref, h_ref, gain_ref, dh_ref, o_ref, dg_ref, acc_ref):
        i, j = pl.program_id(0), pl.program_id(1)

        @pl.when((i == 0) & (j == 0))
        def _():
            dg_ref[...] = jnp.zeros_like(dg_ref)

        @pl.when(j == 0)
        def _():
            acc_ref[...] = jnp.zeros_like(acc_ref)

        acc_ref[...] += lax.dot_general(g_ref[...], w_ref[...], (((1,), (1,)), ((), ())),
                                        preferred_element_type=F32)

        @pl.when(j == nk - 1)
        def _():
            dx, dgain = _rmsnorm_bwd(acc_ref[...], h_ref[...], gain_ref[...])
            dg_ref[...] += dgain
            o_ref[...] = dh_ref[...] + dx

    return pl.pallas_call(
        body, name=name, grid=(m // tm, nk),
        in_specs=[pl.BlockSpec((tm, tk), lambda i, j: (i, j)), pl.BlockSpec((d, tk), lambda i, j: (0, j)),
                  pl.BlockSpec((tm, d), lambda i, j: (i, 0)), pl.BlockSpec((1, d), lambda i, j: (0, 0)),
                  pl.BlockSpec((tm, d), lambda i, j: (i, 0))],
        out_specs=[pl.BlockSpec((tm, d), lambda i, j: (i, 0)), pl.BlockSpec((1, d), lambda i, j: (0, 0))],
        out_shape=[jax.ShapeDtypeStruct((m, d), F32), jax.ShapeDtypeStruct((1, d), F32)],
        scratch_shapes=[pltpu.VMEM((tm, d), F32)],
        compiler_params=_params(("arbitrary", "arbitrary")),
    )(g, w, h, gain, dh_in)


def _mm_tn(x, g, *, name):
    m, k = x.shape
    n = g.shape[1]
    tm, tk, tn = min(ROW_TILE, m), _tile(k, 1152), _tile(n, 1024)
    nm = m // tm

    def body(x_ref, g_ref, o_ref, acc_ref):
        s = pl.program_id(2)

        @pl.when(s == 0)
        def _():
            acc_ref[...] = jnp.zeros_like(acc_ref)

        acc_ref[...] += lax.dot_general(x_ref[...], g_ref[...].astype(BF16), (((0,), (0,)), ((), ())),
                                        preferred_element_type=F32)

        @pl.when(s == nm - 1)
        def _():
            o_ref[...] = acc_ref[...].astype(BF16)

    return pl.pallas_call(
        body, name=name, grid=(k // tk, n // tn, nm),
        in_specs=[pl.BlockSpec((tm, tk), lambda a, b, s: (s, a)), pl.BlockSpec((tm, tn), lambda a, b, s: (s, b))],
        out_specs=pl.BlockSpec((tk, tn), lambda a, b, s: (a, b)),
        out_shape=jax.ShapeDtypeStruct((k, n), BF16),
        scratch_shapes=[pltpu.VMEM((tk, tn), F32)],
        compiler_params=_params(("parallel", "parallel", "arbitrary")),
    )(x, g)


def _loss_head(h, gain, target):
    m, d = h.shape
    tm = 512

    def body(h_ref, g_ref, t_ref, dh_ref, loss_ref, dg_ref):
        @pl.when(pl.program_id(0) == 0)
        def _():
            loss_ref[...] = jnp.zeros_like(loss_ref)
            dg_ref[...] = jnp.zeros_like(dg_ref)

        x = h_ref[...]
        gain_v = g_ref[...]
        r = lax.rsqrt(jnp.mean(x * x, axis=-1, keepdims=True) + RMS_EPS)
        err = x * r * gain_v - t_ref[...]
        loss_ref[...] += 0.5 * jnp.sum(jnp.mean(err * err, axis=-1, keepdims=True), axis=0, keepdims=True)
        dx, dgain = _rmsnorm_bwd(err * (1.0 / d), x, gain_v)
        dg_ref[...] += dgain
        dh_ref[...] = dx

    return pl.pallas_call(
        body, name="loss_head", grid=(m // tm,),
        in_specs=[pl.BlockSpec((tm, d), lambda i: (i, 0)), pl.BlockSpec((1, d), lambda i: (0, 0)),
                  pl.BlockSpec((tm, d), lambda i: (i, 0))],
        out_specs=[pl.BlockSpec((tm, d), lambda i: (i, 0)), pl.BlockSpec((1, LANES), lambda i: (0, 0)),
                   pl.BlockSpec((1, d), lambda i: (0, 0))],
        out_shape=[jax.ShapeDtypeStruct((m, d), F32), jax.ShapeDtypeStruct((1, LANES), F32),
                   jax.ShapeDtypeStruct((1, d), F32)],
        compiler_params=_params(("arbitrary",)),
    )(h, gain, target)


def _rope_tables(s):
    t = jnp.arange(s)
    row = (t // GRID_W).astype(F32)
    col = (t % GRID_W).astype(F32)
    axis_dim = HEAD_DIM // 2
    inv_freq = ROPE_THETA ** (-jnp.arange(0, axis_dim, 2, dtype=F32) / axis_dim)
    ar, ac = row[:, None] * inv_freq, col[:, None] * inv_freq
    cos = jnp.concatenate([jnp.cos(ar), jnp.cos(ar), jnp.cos(ac), jnp.cos(ac)], axis=-1)
    sin = jnp.concatenate([-jnp.sin(ar), jnp.sin(ar), -jnp.sin(ac), jnp.sin(ac)], axis=-1)
    return jnp.tile(cos, (1, 2)), jnp.tile(sin, (1, 2))


def _swap16(x):
    lane = lax.broadcasted_iota(jnp.int32, x.shape, 1)
    return jnp.where((lane % 32) < 16, pltpu.roll(x, LANES - 16, 1), pltpu.roll(x, 16, 1))


def _head_mean(v):
    lane = lax.broadcasted_iota(jnp.int32, v.shape, 1)
    lo = lane < HEAD_DIM
    s_all = jnp.sum(v, axis=-1, keepdims=True)
    s_lo = jnp.sum(jnp.where(lo, v, 0.0), axis=-1, keepdims=True)
    return jnp.where(lo, s_lo, s_all - s_lo) * (1.0 / HEAD_DIM)


def _norm_rope(x, gain2, cos, sin):
    r = lax.rsqrt(_head_mean(x * x) + RMS_EPS)
    nrm = x * r * gain2
    return nrm * cos + _swap16(nrm) * sin


def _norm_rope_bwd(dy, x, gain2, cos, sin):
    dn = dy * cos + _swap16(dy * sin)
    r = lax.rsqrt(_head_mean(x * x) + RMS_EPS)
    xh = x * r
    dgain = jnp.sum(dn * xh, axis=0, keepdims=True)
    u = dn * gain2
    return r * (u - xh * _head_mean(u * xh)), dgain


def _a_prep(qkv, cos, sin, gq2, gk2):
    s = qkv.shape[0]
    tr = 256
    nq, nk = A_HEADS * HEAD_DIM, A_KV * HEAD_DIM

    def body(qkv_ref, cos_ref, sin_ref, gq_ref, gk_ref, qt_ref, k_ref, v_ref):
        cos_v, sin_v = cos_ref[...], sin_ref[...]
        for c in range(nq // LANES):
            y = _norm_rope(qkv_ref[:, c * LANES:(c + 1) * LANES], gq_ref[...], cos_v, sin_v) * (QK_SCALE * LOG2E)
            yt = y.T
            qt_ref[2 * c] = yt[:HEAD_DIM].astype(BF16)
            qt_ref[2 * c + 1] = yt[HEAD_DIM:].astype(BF16)
        for c in range(nk // LANES):
            y = _norm_rope(qkv_ref[:, nq + c * LANES:nq + (c + 1) * LANES], gk_ref[...], cos_v, sin_v)
            k_ref[2 * c] = y[:, :HEAD_DIM].astype(BF16)
            k_ref[2 * c + 1] = y[:, HEAD_DIM:].astype(BF16)
            x = qkv_ref[:, nq + nk + c * LANES:nq + nk + (c + 1) * LANES]
            v_ref[2 * c] = x[:, :HEAD_DIM].astype(BF16)
            v_ref[2 * c + 1] = x[:, HEAD_DIM:].astype(BF16)

    return pl.pallas_call(
        body, name="a_prep", grid=(s // tr,),
        in_specs=[pl.BlockSpec((tr, nq + 2 * nk), lambda i: (i, 0)), pl.BlockSpec((tr, LANES), lambda i: (i, 0)),
                  pl.BlockSpec((tr, LANES), lambda i: (i, 0)), pl.BlockSpec((1, LANES), lambda i: (0, 0)),
                  pl.BlockSpec((1, LANES), lambda i: (0, 0))],
        out_specs=[pl.BlockSpec((A_HEADS, HEAD_DIM, tr), lambda i: (0, 0, i)),
                   pl.BlockSpec((A_KV, tr, HEAD_DIM), lambda i: (0, i, 0)),
                   pl.BlockSpec((A_KV, tr, HEAD_DIM), lambda i: (0, i, 0))],
        out_shape=[jax.ShapeDtypeStruct((A_HEADS, HEAD_DIM, s), BF16), jax.ShapeDtypeStruct((A_KV, s, HEAD_DIM), BF16),
                   jax.ShapeDtypeStruct((A_KV, s, HEAD_DIM), BF16)],
        compiler_params=_params(("parallel",)),
    )(qkv, cos, sin, gq2, gk2)


def _a_prep_bwd(dqt, dkt, dvt, qkv, cos, sin, gq2, gk2):
    s = qkv.shape[0]
    tr = 256
    nq, nk = A_HEADS * HEAD_DIM, A_KV * HEAD_DIM

    def body(dqt_ref, dkt_ref, dvt_ref, qkv_ref, cos_ref, sin_ref, gq_ref, gk_ref, o_ref, dgq_ref, dgk_ref):
        @pl.when(pl.program_id(0) == 0)
        def _():
            dgq_ref[...] = jnp.zeros_like(dgq_ref)
            dgk_ref[...] = jnp.zeros_like(dgk_ref)

        cos_v, sin_v = cos_ref[...], sin_ref[...]

        def pair(ref, c):
            return jnp.concatenate([ref[2 * c], ref[2 * c + 1]], axis=0).T

        for c in range(nq // LANES):
            dx, dg = _norm_rope_bwd(pair(dqt_ref, c) * QK_SCALE, qkv_ref[:, c * LANES:(c + 1) * LANES],
                                    gq_ref[...], cos_v, sin_v)
            o_ref[:, c * LANES:(c + 1) * LANES] = dx.astype(BF16)
            dgq_ref[...] += dg
        for c in range(nk // LANES):
            lo = nq + c * LANES
            dx, dg = _norm_rope_bwd(pair(dkt_ref, c) * LN2, qkv_ref[:, lo:lo + LANES], gk_ref[...], cos_v, sin_v)
            o_ref[:, lo:lo + LANES] = dx.astype(BF16)
            dgk_ref[...] += dg
            o_ref[:, lo + nk:lo + nk + LANES] = pair(dvt_ref, c).astype(BF16)

    return pl.pallas_call(
        body, name="a_prep_bwd", grid=(s // tr,),
        in_specs=[pl.BlockSpec((A_HEADS, HEAD_DIM, tr), lambda i: (0, 0, i)),
                  pl.BlockSpec((A_KV, HEAD_DIM, tr), lambda i: (0, 0, i)),
                  pl.BlockSpec((A_KV, HEAD_DIM, tr), lambda i: (0, 0, i)),
                  pl.BlockSpec((tr, nq + 2 * nk), lambda i: (i, 0)), pl.BlockSpec((tr, LANES), lambda i: (i, 0)),
                  pl.BlockSpec((tr, LANES), lambda i: (i, 0)), pl.BlockSpec((1, LANES), lambda i: (0, 0)),
                  pl.BlockSpec((1, LANES), lambda i: (0, 0))],
        out_specs=[pl.BlockSpec((tr, nq + 2 * nk), lambda i: (i, 0)), pl.BlockSpec((1, LANES), lambda i: (0, 0)),
                   pl.BlockSpec((1, LANES), lambda i: (0, 0))],
        out_shape=[jax.ShapeDtypeStruct((s, nq + 2 * nk), BF16), jax.ShapeDtypeStruct((1, LANES), F32),
                   jax.ShapeDtypeStruct((1, LANES), F32)],
        compiler_params=_params(("arbitrary",)),
    )(dqt, dkt, dvt, qkv, cos, sin, gq2, gk2)


A_TQ = 256
A_KEY_CHUNK = 512


def _a_attn_fwd(qt, k, v, cargo, *, name):
    nh, _, s = qt.shape
    rep = nh // k.shape[0]
    tq = min(A_TQ, s)
    grid = (nh, s // tq)

    def body(qt_ref, k_ref, v_ref, o_ref, lse_ref):
        st = jnp.dot(k_ref[0], qt_ref[0], preferred_element_type=F32)
        mx = jnp.max(st, axis=0, keepdims=True)
        p = jnp.exp2(st - mx)
        den = jnp.sum(p, axis=0, keepdims=True)
        ot = lax.dot_general(v_ref[0], p.astype(BF16), (((0,), (0,)), ((), ())), preferred_element_type=F32)
        o_ref[0] = (ot / den).astype(BF16)
        lse_ref[0] = mx + jnp.log(den) * LOG2E

    carried = _carry(cargo, grid, 3, 2, body)
    res = pl.pallas_call(
        carried.body, name=name, grid=grid,
        in_specs=[pl.BlockSpec((1, HEAD_DIM, tq), lambda h, i: (h, 0, i)),
                  pl.BlockSpec((1, s, HEAD_DIM), lambda h, i: (h // rep, 0, 0)),
                  pl.BlockSpec((1, s, HEAD_DIM), lambda h, i: (h // rep, 0, 0))] + carried.in_specs,
        out_specs=[pl.BlockSpec((1, HEAD_DIM, tq), lambda h, i: (h, 0, i)),
                   pl.BlockSpec((1, 1, tq), lambda h, i: (h, 0, i))] + carried.out_specs,
        out_shape=[jax.ShapeDtypeStruct((nh, HEAD_DIM, s), BF16), jax.ShapeDtypeStruct((nh, 1, s), F32)]
        + carried.out_shape,
        scratch_shapes=carried.scratch,
        compiler_params=_params(("arbitrary", "arbitrary")),
    )(qt, k, v, *carried.args)
    return res[0], res[1], res[2:]


def _a_attn_bwd(qt, k, v, dot, ot, lse, cargo, *, name):
    nh, _, s = qt.shape
    nkv = k.shape[0]
    rep = nh // nkv
    tq, ck = min(A_TQ, s), min(A_KEY_CHUNK, s)
    grid = (nh, s // tq)

    def body(qt_ref, k_ref, v_ref, dot_ref, ot_ref, lse_ref, dq_ref, dk_ref, dv_ref):
        h, i = pl.program_id(0), pl.program_id(1)

        @pl.when((h % rep == 0) & (i == 0))
        def _():
            dk_ref[...] = jnp.zeros_like(dk_ref)
            dv_ref[...] = jnp.zeros_like(dv_ref)

        q_t, do_t, lse_v = qt_ref[0], dot_ref[0], lse_ref[0]
        delta = jnp.sum(do_t.astype(F32) * ot_ref[0].astype(F32), axis=0, keepdims=True)
        nt = (((1,), (1,)), ((), ()))
        dq = jnp.zeros((HEAD_DIM, tq), F32)
        for c in range(s // ck):
            keys = slice(c * ck, (c + 1) * ck)
            kc = k_ref[0, keys, :]
            p = jnp.exp2(jnp.dot(kc, q_t, preferred_element_type=F32) - lse_v)
            dp = jnp.dot(v_ref[0, keys, :], do_t, preferred_element_type=F32)
            ds = (p * (dp - delta)).astype(BF16)
            dv_ref[0, :, keys] += lax.dot_general(do_t, p.astype(BF16), nt, preferred_element_type=F32)
            dk_ref[0, :, keys] += lax.dot_general(q_t, ds, nt, preferred_element_type=F32)
            dq = dq + lax.dot_general(kc, ds, (((0,), (0,)), ((), ())), preferred_element_type=F32)
        dq_ref[0] = dq

    blk_q = pl.BlockSpec((1, HEAD_DIM, tq), lambda h, i: (h, 0, i))
    blk_row = pl.BlockSpec((1, 1, tq), lambda h, i: (h, 0, i))
    blk_kv = pl.BlockSpec((1, s, HEAD_DIM), lambda h, i: (h // rep, 0, 0))
    blk_acc = pl.BlockSpec((1, HEAD_DIM, s), lambda h, i: (h // rep, 0, 0))
    carried = _carry(cargo, grid, 6, 3, body)
    res = pl.pallas_call(
        carried.body, name=name, grid=grid,
        in_specs=[blk_q, blk_kv, blk_kv, blk_q, blk_q, blk_row] + carried.in_specs,
        out_specs=[blk_q, blk_acc, blk_acc] + carried.out_specs,
        out_shape=[jax.ShapeDtypeStruct((nh, HEAD_DIM, s), F32), jax.ShapeDtypeStruct((nkv, HEAD_DIM, s), F32),
                   jax.ShapeDtypeStruct((nkv, HEAD_DIM, s), F32)] + carried.out_shape,
        scratch_shapes=carried.scratch,
        compiler_params=_params(("arbitrary", "arbitrary")),
    )(qt, k, v, dot, ot, lse, *carried.args)
    return res[0], res[1], res[2], res[3:]


def _win_scores(kw, q_t, i, tq, tk, window, dil, seg, slope):
    st = jnp.dot(kw, q_t, preferred_element_type=F32)
    qpos = i * tq + lax.broadcasted_iota(jnp.int32, (tk, tq), 1)
    kpos = i * tq - WIN_PAD + lax.broadcasted_iota(jnp.int32, (tk, tq), 0)
    dist = jnp.abs(kpos - qpos)
    seg_lo = qpos - (qpos & (seg - 1))
    valid = (dist <= window) & (kpos >= seg_lo) & (kpos < seg_lo + seg)
    return jnp.where(valid, st * QK_SCALE - slope * (dist * dil).astype(F32), NEG)


def _win_tq(s):
    return min(512, s)


def _win_fwd(qt, kp, vp, slopes, sinks, *, window, dil, seg, out_dtype, name):
    nh, _, s = qt.shape
    rep = nh // kp.shape[0]
    tq = _win_tq(s)
    tk = tq + 2 * WIN_PAD
    sp = s + 2 * WIN_PAD

    def body(*refs):
        qt_ref, k_ref, v_ref, sl_ref = refs[:4]
        o_ref, lse_ref = refs[-2:]
        i = pl.program_id(1)
        base = pl.multiple_of(i * tq, tq)
        st = _win_scores(k_ref[0, pl.ds(base, tk), :], qt_ref[0], i, tq, tk, window, dil, seg, sl_ref[0][:, :1])
        mx = jnp.max(st, axis=0, keepdims=True)
        if sinks is not None:
            sink = refs[4][0][:, :1]
            mx = jnp.maximum(mx, sink)
        p = jnp.exp(st - mx)
        den = jnp.sum(p, axis=0, keepdims=True)
        if sinks is not None:
            den = den + jnp.exp(sink - mx)
        ot = lax.dot_general(v_ref[0, pl.ds(base, tk), :], p.astype(BF16), (((0,), (0,)), ((), ())),
                             preferred_element_type=F32)
        o_ref[0] = (ot / den).astype(o_ref.dtype)
        lse_ref[0] = mx + jnp.log(den)

    blk_q = pl.BlockSpec((1, HEAD_DIM, tq), lambda h, i: (h, 0, i))
    blk_kv = pl.BlockSpec((1, sp, HEAD_DIM), lambda h, i: (h // rep, 0, 0))
    blk_h = pl.BlockSpec((1, 1, LANES), lambda h, i: (h, 0, 0))
    in_specs, args = [blk_q, blk_kv, blk_kv, blk_h], [qt, kp, vp, slopes]
    if sinks is not None:
        in_specs.append(blk_h)
        args.append(sinks)
    return pl.pallas_call(
        body, name=name, grid=(nh, s // tq), in_specs=in_specs,
        out_specs=[blk_q, pl.BlockSpec((1, 1, tq), lambda h, i: (h, 0, i))],
        out_shape=[jax.ShapeDtypeStruct((nh, HEAD_DIM, s), out_dtype), jax.ShapeDtypeStruct((nh, 1, s), F32)],
        compiler_params=_params(("parallel", "parallel")),
    )(*args)


def _win_bwd(qt, kp, vp, slopes, sinks, dot, ot, delta, *, window, dil, seg, name):
    nh, _, s = qt.shape
    nkv = kp.shape[0]
    rep = nh // nkv
    tq = _win_tq(s)
    tk = tq + 2 * WIN_PAD
    sp = s + 2 * WIN_PAD
    n_in = 6 + (sinks is not None)

    def body(*refs):
        qt_ref, k_ref, v_ref, sl_ref, dot_ref, aux_ref = refs[:6]
        outs = refs[n_in:]
        dq_ref, dk_ref, dv_ref = outs[:3]
        h, i = pl.program_id(0), pl.program_id(1)

        @pl.when((h % rep == 0) & (i == 0))
        def _():
            dk_ref[...] = jnp.zeros_like(dk_ref)
            dv_ref[...] = jnp.zeros_like(dv_ref)

        base = pl.multiple_of(i * tq, tq)
        win = pl.ds(base, tk)
        kw, q_t, do_t = k_ref[0, win, :], qt_ref[0], dot_ref[0]
        st = _win_scores(kw, q_t, i, tq, tk, window, dil, seg, sl_ref[0][:, :1])
        mx = jnp.max(st, axis=0, keepdims=True)
        if sinks is not None:
            sink = refs[6][0][:, :1]
            mx = jnp.maximum(mx, sink)
        p = jnp.exp(st - mx)
        den = jnp.sum(p, axis=0, keepdims=True)
        if sinks is not None:
            p_sink = jnp.exp(sink - mx)
            den = den + p_sink
        p = p / den
        dp = jnp.dot(v_ref[0, win, :], do_t, preferred_element_type=F32)
        if delta is None:
            row = jnp.sum(do_t.astype(F32) * aux_ref[0].astype(F32), axis=0, keepdims=True)
        else:
            row = aux_ref[0]
        ds = (p * (dp - row) * QK_SCALE).astype(BF16)
        nt = (((1,), (1,)), ((), ()))
        dv_ref[0, :, win] += lax.dot_general(do_t, p.astype(BF16), nt, preferred_element_type=F32)
        dk_ref[0, :, win] += lax.dot_general(q_t, ds, nt, preferred_element_type=F32)
        dq_ref[0] = lax.dot_general(kw, ds, (((0,), (0,)), ((), ())), preferred_element_type=F32)
        if sinks is not None:
            dsink_ref = outs[3]

            @pl.when(i == 0)
            def _():
                dsink_ref[...] = jnp.zeros_like(dsink_ref)

            dsink_ref[0] += jnp.zeros((1, LANES), F32) - jnp.sum(p_sink / den * row, axis=1, keepdims=True)

    blk_q = pl.BlockSpec((1, HEAD_DIM, tq), lambda h, i: (h, 0, i))
    blk_row = pl.BlockSpec((1, 1, tq), lambda h, i: (h, 0, i))
    blk_kv = pl.BlockSpec((1, sp, HEAD_DIM), lambda h, i: (h // rep, 0, 0))
    blk_acc = pl.BlockSpec((1, HEAD_DIM, sp), lambda h, i: (h // rep, 0, 0))
    blk_h = pl.BlockSpec((1, 1, LANES), lambda h, i: (h, 0, 0))
    in_specs = [blk_q, blk_kv, blk_kv, blk_h, blk_q, blk_q if delta is None else blk_row]
    args = [qt, kp, vp, slopes, dot, ot if delta is None else delta]
    out_specs = [blk_q, blk_acc, blk_acc]
    out_shape = [jax.ShapeDtypeStruct((nh, HEAD_DIM, s), F32), jax.ShapeDtypeStruct((nkv, HEAD_DIM, sp), F32),
                 jax.ShapeDtypeStruct((nkv, HEAD_DIM, sp), F32)]
    if sinks is not None:
        in_specs.append(blk_h)
        args.append(sinks)
        out_specs.append(blk_h)
        out_shape.append(jax.ShapeDtypeStruct((nh, 1, LANES), F32))
    res = pl.pallas_call(
        body, name=name, grid=(nh, s // tq), in_specs=in_specs, out_specs=out_specs, out_shape=out_shape,
        compiler_params=_params(("arbitrary", "arbitrary")),
    )(*args)
    return res if sinks is not None else (*res, None)


def _group_weights(lse):
    e = jnp.exp(lse - jnp.max(lse, axis=0, keepdims=True))
    return e / jnp.sum(e, axis=0, keepdims=True)


def _b_combine_fwd(ot, lse):
    nh, _, s = ot.shape
    ng, hg, _ = lse.shape
    ts = min(512, s)

    def body(ot_ref, lse_ref, o_ref):
        alpha = _group_weights(lse_ref[...])
        for g in range(ng):
            for j in range(hg):
                o_ref[g * hg + j] = (ot_ref[g * hg + j] * alpha[g, j:j + 1, :]).astype(BF16)

    return pl.pallas_call(
        body, name="b_combine_fwd", grid=(s // ts,),
        in_specs=[pl.BlockSpec((nh, HEAD_DIM, ts), lambda i: (0, 0, i)), pl.BlockSpec((ng, hg, ts), lambda i: (0, 0, i))],
        out_specs=pl.BlockSpec((nh, HEAD_DIM, ts), lambda i: (0, 0, i)),
        out_shape=jax.ShapeDtypeStruct((nh, HEAD_DIM, s), BF16),
        compiler_params=_params(("parallel",)),
    )(ot, lse)


def _b_combine_bwd(dout, ot, lse):
    nh, _, s = ot.shape
    ng, hg, _ = lse.shape
    ts = min(512, s)

    def body(dout_ref, ot_ref, lse_ref, do_ref, delta_ref):
        alpha = _group_weights(lse_ref[...])
        for j in range(hg):
            e = [jnp.sum(dout_ref[g * hg + j].astype(F32) * ot_ref[g * hg + j], axis=0, keepdims=True)
                 for g in range(ng)]
            a = [alpha[g, j:j + 1, :] for g in range(ng)]
            mix = a[0] * e[0]
            for g in range(1, ng):
                mix = mix + a[g] * e[g]
            for g in range(ng):
                do_ref[g * hg + j] = (dout_ref[g * hg + j].astype(F32) * a[g]).astype(BF16)
                delta_ref[g * hg + j] = a[g] * mix

    blk = pl.BlockSpec((nh, HEAD_DIM, ts), lambda i: (0, 0, i))
    return pl.pallas_call(
        body, name="b_combine_bwd", grid=(s // ts,),
        in_specs=[blk, blk, pl.BlockSpec((ng, hg, ts), lambda i: (0, 0, i))],
        out_specs=[blk, pl.BlockSpec((nh, 1, ts), lambda i: (0, 0, i))],
        out_shape=[jax.ShapeDtypeStruct((nh, HEAD_DIM, s), BF16), jax.ShapeDtypeStruct((nh, 1, s), F32)],
        compiler_params=_params(("parallel",)),
    )(dout, ot, lse)


def _alibi_slopes(n):
    return 2.0 ** (-8.0 * jnp.arange(1, n + 1, dtype=F32) / n)


def _per_head(v):
    return jnp.broadcast_to(v.astype(F32)[:, None, None], (v.shape[0], 1, LANES))


def _dilate(x, dil):
    if dil == 1:
        return x
    s = x.shape[-1]
    return jnp.swapaxes(x.reshape(x.shape[:-1] + (s // dil, dil)), -1, -2).reshape(x.shape)


def _undilate(x, dil):
    if dil == 1:
        return x
    s = x.shape[-1]
    return jnp.swapaxes(x.reshape(x.shape[:-1] + (dil, s // dil)), -1, -2).reshape(x.shape)


def _heads_t(x, nh):
    return jnp.transpose(x.reshape(x.shape[0], nh, HEAD_DIM), (1, 2, 0))


def _tokens(xt):
    return jnp.transpose(xt, (2, 0, 1)).reshape(xt.shape[2], -1)


def _pad_tokens(xt):
    return jnp.pad(jnp.swapaxes(xt, 1, 2), ((0, 0), (WIN_PAD, WIN_PAD), (0, 0)))


def _mixer_fwd(kind, qkv, p, tabs, cargo, layer):
    s = qkv.shape[0]
    if kind == 0:
        qt, k, v = _a_prep(qkv, tabs[0], tabs[1], p["gq2"], p["gk2"])
        ot, lse, brought = _a_attn_fwd(qt, k, v, cargo, name=f"a_attn_fwd_l{layer}")
        return _tokens(ot), dict(qt=qt, k=k, v=v, ot=ot, lse=lse), brought
    assert cargo is None
    if kind == 2:
        nq, nk = C_HEADS * HEAD_DIM, C_KV * HEAD_DIM
        qt = _heads_t(qkv[:, :nq], C_HEADS)
        kp = _pad_tokens(_heads_t(qkv[:, nq:nq + nk], C_KV))
        vp = _pad_tokens(_heads_t(qkv[:, nq + nk:], C_KV))
        ot, _ = _win_fwd(qt, kp, vp, p["slopes"], p["sinks"], window=C_WINDOW, dil=1, seg=s, out_dtype=BF16,
                         name="c_attn_fwd")
        return _tokens(ot), dict(qt=qt, kp=kp, vp=vp, ot=ot), ()
    ng, hg, kg = len(B_GROUPS), B_HEADS_PER_GROUP, B_KV_PER_GROUP
    nq, nk = ng * hg * HEAD_DIM, ng * kg * HEAD_DIM
    qt_all = _heads_t(qkv[:, :nq], ng * hg)
    kt_all = _heads_t(qkv[:, nq:nq + nk], ng * kg)
    vt_all = _heads_t(qkv[:, nq + nk:], ng * kg)
    saved, outs, lses = [], [], []
    for g, (window, dil) in enumerate(B_GROUPS):
        qt = _dilate(qt_all[g * hg:(g + 1) * hg], dil)
        kp = _pad_tokens(_dilate(kt_all[g * kg:(g + 1) * kg], dil))
        vp = _pad_tokens(_dilate(vt_all[g * kg:(g + 1) * kg], dil))
        sl = p["slopes"][g * hg:(g + 1) * hg]
        ot, lse = _win_fwd(qt, kp, vp, sl, None, window=window // 2 // dil, dil=dil, seg=s // dil, out_dtype=F32,
                           name=f"b_attn_fwd_g{g}")
        saved.append(dict(qt=qt, kp=kp, vp=vp))
        outs.append(_undilate(ot, dil))
        lses.append(_undilate(lse[:, 0, :], dil))
    ot_all, lse_all = jnp.concatenate(outs, axis=0), jnp.stack(lses, axis=0)
    mixed = _b_combine_fwd(ot_all, lse_all)
    return _tokens(mixed), dict(groups=saved, ot=ot_all, lse=lse_all), ()


def _mixer_bwd(kind, do, qkv, sv, p, tabs, cargo, layer):
    s = do.shape[0]
    small = {}
    if kind == 0:
        dqt, dkt, dvt, brought = _a_attn_bwd(sv["qt"], sv["k"], sv["v"], _heads_t(do, A_HEADS), sv["ot"], sv["lse"],
                                             cargo, name=f"a_attn_bwd_l{layer}")
        dqkv, dgq, dgk = _a_prep_bwd(dqt, dkt, dvt, qkv, tabs[0], tabs[1], p["gq2"], p["gk2"])
        small["q_gain"] = dgq[0, :HEAD_DIM] + dgq[0, HEAD_DIM:]
        small["k_gain"] = dgk[0, :HEAD_DIM] + dgk[0, HEAD_DIM:]
        return dqkv, small, brought
    assert cargo is None
    if kind == 2:
        dqt, dkt, dvt, dsink = _win_bwd(sv["qt"], sv["kp"], sv["vp"], p["slopes"], p["sinks"], _heads_t(do, C_HEADS),
                                        sv["ot"], None, window=C_WINDOW, dil=1, seg=s, name="c_attn_bwd")
        small["sinks"] = dsink[:, 0, 0]
        parts = [dqt, dkt[:, :, WIN_PAD:-WIN_PAD], dvt[:, :, WIN_PAD:-WIN_PAD]]
        return jnp.concatenate([_tokens(x) for x in parts], axis=1).astype(BF16), small, ()
    ng, hg, kg = len(B_GROUPS), B_HEADS_PER_GROUP, B_KV_PER_GROUP
    do_own, delta = _b_combine_bwd(_heads_t(do, ng * hg), sv["ot"], sv["lse"])
    dqs, dks, dvs = [], [], []
    for g, (window, dil) in enumerate(B_GROUPS):
        gs = sv["groups"][g]
        dqt, dkt, dvt, _ = _win_bwd(gs["qt"], gs["kp"], gs["vp"], p["slopes"][g * hg:(g + 1) * hg], None,
                                    _dilate(do_own[g * hg:(g + 1) * hg], dil), None,
                                    _dilate(delta[g * hg:(g + 1) * hg], dil),
                                    window=window // 2 // dil, dil=dil, seg=s // dil, name=f"b_attn_bwd_g{g}")
        dqs.append(_undilate(dqt, dil))
        dks.append(_undilate(dkt[:, :, WIN_PAD:-WIN_PAD], dil))
        dvs.append(_undilate(dvt[:, :, WIN_PAD:-WIN_PAD], dil))
    parts = [jnp.concatenate(x, axis=0) for x in (dqs, dks, dvs)]
    return jnp.concatenate([_tokens(x) for x in parts], axis=1).astype(BF16), small, ()


LAYER_MATS = ("w_qkv", "w_o", "w1", "w2")
COLUMN_QUARTERS = ("w_qkv", "w1")


def _whole(key, gathered):
    q, r, c = gathered.shape
    if key in COLUMN_QUARTERS:
        return jnp.transpose(gathered, (1, 0, 2)).reshape(r, q * c)
    return gathered.reshape(q * r, c)


def _quarters(key, g):
    r, c = g.shape
    if key in COLUMN_QUARTERS:
        return jnp.transpose(g.reshape(r, 4, c // 4), (1, 0, 2))
    return g.reshape(4, r // 4, c)


def _local_step(x, target, norms, mixer_params, shards, whole=None):
    s = x.shape[0]
    tabs = _rope_tables(s)
    if whole is None:
        assert MIXER_OF_LAYER[0][0] == 0
        first = _run_cargo(_gather_cargo([shards[0][key] for key in LAYER_MATS]), name="gather_l0")
        mats = {0: {key: _whole(key, g) for key, g in zip(LAYER_MATS, first)}}
        later = _gather_cargo([shards[layer][key] for layer in range(1, DEPTH) for key in LAYER_MATS])
    else:
        mats, later = dict(enumerate(whole)), None
    h = x
    saved = []
    for layer in range(DEPTH):
        kind = layer % N_MIXERS
        w, p = mats[layer], mixer_params[layer]
        hn, qkv = _norm_mm(h, norms["attn"][layer][None], w["w_qkv"], out_dtype=F32 if kind == 0 else BF16,
                           relu2=False, name=f"qkv_proj_l{layer}")
        o, sv, brought = _mixer_fwd(kind, qkv, p, tabs, later if layer == 0 else None, layer)
        for n, g in enumerate(brought):
            mats.setdefault(1 + n // len(LAYER_MATS), {})[LAYER_MATS[n % len(LAYER_MATS)]] = _whole(
                LAYER_MATS[n % len(LAYER_MATS)], g)
        h_mid = _mm_res(o, w["w_o"], h, name=f"o_proj_l{layer}")
        hn2, act = _norm_mm(h_mid, norms["mlp"][layer][None], w["w1"], out_dtype=BF16, relu2=True,
                            name=f"mlp_up_l{layer}")
        h_out = _mm_res(act, w["w2"], h_mid, name=f"mlp_down_l{layer}")
        saved.append(dict(h=h, hn=hn, qkv=qkv, o=o, mix=sv, h_mid=h_mid, hn2=hn2, act=act))
        h = h_out

    dh, loss, d_final = _loss_head(h, norms["final"][None], target)

    own, received, pending = {}, {}, []
    d_attn, d_mlp, small = [None] * DEPTH, [None] * DEPTH, [None] * DEPTH
    for layer in reversed(range(DEPTH)):
        kind = layer % N_MIXERS
        w, p, sv = mats[layer], mixer_params[layer], saved[layer]
        du = _mm_nt(dh, w["w2"], sv["act"], name=f"mlp_down_bwd_l{layer}")
        own[layer, "w2"] = _quarters("w2", _mm_tn(sv["act"], dh, name=f"mlp_w2_grad_l{layer}"))
        own[layer, "w1"] = _quarters("w1", _mm_tn(sv["hn2"], du, name=f"mlp_w1_grad_l{layer}"))
        dh_mid, d_mlp[layer] = _mm_nt_normbwd(du, w["w1"], sv["h_mid"], norms["mlp"][layer][None], dh,
                                              name=f"mlp_up_bwd_l{layer}")
        do = _mm_nt(dh_mid, w["w_o"], None, name=f"o_proj_bwd_l{layer}")
        own[layer, "w_o"] = _quarters("w_o", _mm_tn(sv["o"], dh_mid, name=f"w_o_grad_l{layer}"))
        pending += [(layer, "w2"), (layer, "w1"), (layer, "w_o")]
        cargo = None
        if kind == 0 and whole is None:
            cargo, sent, pending = _scatter_cargo([own[item] for item in pending], None), pending, []
        dqkv, small[layer], brought = _mixer_bwd(kind, do, sv["qkv"], sv["mix"], p, tabs, cargo, layer)
        if cargo is not None:
            received.update(zip(sent, brought))
        own[layer, "w_qkv"] = _quarters("w_qkv", _mm_tn(sv["hn"], dqkv, name=f"w_qkv_grad_l{layer}"))
        pending.append((layer, "w_qkv"))
        dh, d_attn[layer] = _mm_nt_normbwd(dqkv, w["w_qkv"], sv["h"], norms["attn"][layer][None], dh_mid,
                                           name=f"qkv_proj_bwd_l{layer}")
    return loss, dh, own, received, pending, dict(attn=d_attn, mlp=d_mlp, final=d_final, mixer=small)


CHIP_FLIPS = ((1, 0), (0, 1), (1, 1))


class _Cargo(NamedTuple):
    ins: tuple
    out_shape: tuple
    sem_shapes: tuple
    start: Callable
    wait: Callable


class _Carried(NamedTuple):
    body: Callable
    in_specs: list
    out_specs: list
    out_shape: list
    scratch: list
    args: tuple


def _carry(cargo, grid, n_in, n_out, body):
    if cargo is None:
        return _Carried(body, [], [], [], [], ())
    ci, co = len(cargo.ins), len(cargo.out_shape)

    def wrapped(*refs):
        ins, c_ins = refs[:n_in], refs[n_in:n_in + ci]
        outs, c_outs = refs[n_in + ci:n_in + ci + n_out], refs[n_in + ci + n_out:n_in + ci + n_out + co]
        sems = refs[n_in + ci + n_out + co:]
        first = last = None
        for axis, extent in enumerate(grid):
            at = pl.program_id(axis)
            first = (at == 0) if first is None else first & (at == 0)
            last = (at == extent - 1) if last is None else last & (at == extent - 1)

        @pl.when(first)
        def _():
            cargo.start(c_ins, c_outs, sems)

        body(*ins, *outs)

        @pl.when(last)
        def _():
            cargo.wait(c_ins, c_outs, sems)

    return _Carried(wrapped, [ANY] * ci, [ANY] * co, list(cargo.out_shape), list(cargo.sem_shapes), tuple(cargo.ins))


def _run_cargo(cargo, *, name):
    ci, co = len(cargo.ins), len(cargo.out_shape)

    def body(*refs):
        cargo.start(refs[:ci], refs[ci:ci + co], refs[ci + co:])
        cargo.wait(refs[:ci], refs[ci:ci + co], refs[ci + co:])

    return pl.pallas_call(body, name=name, in_specs=[ANY] * ci, out_specs=[ANY] * co, out_shape=list(cargo.out_shape),
                          scratch_shapes=list(cargo.sem_shapes))(*cargo.ins)


def _other_chip(x, y, j):
    fx, fy = CHIP_FLIPS[j]
    return (1 - x if fx else x), (1 - y if fy else y)


def _gather_cargo(shards):
    n = len(shards)

    def copies(ins, outs, sems):
        send_sems, recv_sems, local_sems = sems
        x, y, c = lax.axis_index("x"), lax.axis_index("y"), lax.axis_index("c")
        me = 2 * x + y

        def remote(t, j, arriving):
            px, py = _other_chip(x, y, j)
            return pltpu.make_async_remote_copy(
                src_ref=ins[t], dst_ref=outs[t].at[2 * px + py if arriving else me], send_sem=send_sems.at[t, j],
                recv_sem=recv_sems.at[t, j], device_id=(px, py, c), device_id_type=MESH)

        return remote, lambda t: pltpu.make_async_copy(ins[t], outs[t].at[me], local_sems.at[t])

    def start(ins, outs, sems):
        remote, own = copies(ins, outs, sems)
        for t in range(n):
            own(t).start()
            for j in range(len(CHIP_FLIPS)):
                remote(t, j, False).start()

    def wait(ins, outs, sems):
        remote, own = copies(ins, outs, sems)
        for t in range(n):
            for j in range(len(CHIP_FLIPS)):
                remote(t, j, True).wait_recv()
                remote(t, j, False).wait_send()
            own(t).wait()

    dma = pltpu.SemaphoreType.DMA
    return _Cargo(tuple(shards), tuple(jax.ShapeDtypeStruct((4,) + a.shape, a.dtype) for a in shards),
                  (dma((n, 3)), dma((n, 3)), dma((n,))), start, wait)


def _scatter_cargo(grads, small):
    n = len(grads)

    def copies(ins, outs, sems):
        x, y, c = lax.axis_index("x"), lax.axis_index("y"), lax.axis_index("c")
        me = 4 * x + 2 * y + c

        def remote(t, j):
            px, py = _other_chip(x, y, j)
            return pltpu.make_async_remote_copy(
                src_ref=ins[t].at[2 * px + py], dst_ref=outs[t].at[j], send_sem=sems[0].at[t, j],
                recv_sem=sems[1].at[t, j], device_id=(px, py, c), device_id_type=MESH)

        def small_remote(r, arriving):
            fx, fy, fc = (r + 1) // 4, ((r + 1) // 2) % 2, (r + 1) % 2
            px, py, pc = (1 - x if fx else x), (1 - y if fy else y), (1 - c if fc else c)
            return pltpu.make_async_remote_copy(
                src_ref=ins[n], dst_ref=outs[n].at[4 * px + 2 * py + pc if arriving else me],
                send_sem=sems[2].at[r], recv_sem=sems[3].at[r], device_id=(px, py, pc), device_id_type=MESH)

        return remote, small_remote, lambda: pltpu.make_async_copy(ins[n], outs[n].at[me], sems[4])

    def start(ins, outs, sems):
        remote, small_remote, small_own = copies(ins, outs, sems)
        if small is not None:
            small_own().start()
            for r in range(7):
                small_remote(r, False).start()
        for t in range(n):
            for j in range(len(CHIP_FLIPS)):
                remote(t, j).start()

    def wait(ins, outs, sems):
        remote, small_remote, small_own = copies(ins, outs, sems)
        if small is not None:
            for r in range(7):
                small_remote(r, True).wait_recv()
                small_remote(r, False).wait_send()
            small_own().wait()
        for t in range(n):
            for j in range(len(CHIP_FLIPS)):
                remote(t, j).wait()

    dma = pltpu.SemaphoreType.DMA
    ins = tuple(grads) + (() if small is None else (small,))
    out_shape = tuple(jax.ShapeDtypeStruct((3,) + g.shape[1:], g.dtype) for g in grads)
    sem_shapes = (dma((n, 3)), dma((n, 3)))
    if small is not None:
        out_shape += (jax.ShapeDtypeStruct((8,) + small.shape, small.dtype),)
        sem_shapes += (dma((7,)), dma((7,)), dma(()))
    return _Cargo(ins, out_shape, sem_shapes, start, wait)


def _swap_cores(parts):
    n = len(parts)

    def body(*refs):
        ins, outs = refs[:n], refs[n:2 * n]
        send_sems, recv_sems = refs[2 * n:]
        peer = (lax.axis_index("x"), lax.axis_index("y"), 1 - lax.axis_index("c"))
        copies = [pltpu.make_async_remote_copy(src_ref=ins[t], dst_ref=outs[t], send_sem=send_sems.at[t],
                                               recv_sem=recv_sems.at[t], device_id=peer, device_id_type=MESH)
                  for t in range(n)]
        for cp in copies:
            cp.start()
        for cp in copies:
            cp.wait()

    return pl.pallas_call(
        body, name="swap_cores", in_specs=[ANY] * n, out_specs=[ANY] * n,
        out_shape=[jax.ShapeDtypeStruct(a.shape, a.dtype) for a in parts],
        scratch_shapes=[pltpu.SemaphoreType.DMA((n,)), pltpu.SemaphoreType.DMA((n,))],
    )(*parts)


def _rows_tile(r):
    return 256 if r % 256 == 0 else r


def _sum_quarters(own, recv, *, name):
    r, c = own.shape
    tr = _rows_tile(r)

    def body(own_ref, recv_ref, o_ref):
        acc = own_ref[...].astype(F32)
        for j in range(3):
            acc = acc + recv_ref[j].astype(F32)
        o_ref[...] = acc

    return pl.pallas_call(
        body, name=name, grid=(r // tr,),
        in_specs=[pl.BlockSpec((tr, c), lambda i: (i, 0)), pl.BlockSpec((3, tr, c), lambda i: (0, i, 0))],
        out_specs=pl.BlockSpec((tr, c), lambda i: (i, 0)),
        out_shape=jax.ShapeDtypeStruct((r, c), F32),
        compiler_params=_params(("parallel",)),
    )(own, recv)


def _adamw(w, m, v, parts, *, name):
    r, c = w.shape
    tr = _rows_tile(r)
    c1, c2 = 1.0 - ADAM_B1 ** ADAM_STEP, 1.0 - ADAM_B2 ** ADAM_STEP
    n_parts = len(parts)

    def body(*refs):
        w_ref, m_ref, v_ref = refs[:3]
        g_ref, d_ref, nm_ref, nv_ref = refs[3 + n_parts:]
        terms = []
        for p_ref in refs[3:3 + n_parts]:
            terms += [p_ref[...]] if len(p_ref.shape) == 2 else [p_ref[j] for j in range(p_ref.shape[0])]
        g = terms[0]
        for term in terms[1:]:
            g = g + term
        m_new = ADAM_B1 * m_ref[...] + (1.0 - ADAM_B1) * g
        v_new = ADAM_B2 * v_ref[...] + (1.0 - ADAM_B2) * (g * g)
        step = (m_new / c1) / (jnp.sqrt(v_new / c2) + ADAM_EPS)
        g_ref[...] = g
        d_ref[...] = -ADAM_LR * (step + ADAM_WD * w_ref[...])
        nm_ref[...] = m_new
        nv_ref[...] = v_new

    blk = pl.BlockSpec((tr, c), lambda i: (i, 0))
    part_specs = [blk if p.ndim == 2 else pl.BlockSpec((p.shape[0], tr, c), lambda i: (0, i, 0)) for p in parts]
    return pl.pallas_call(
        body, name=name, grid=(r // tr,), in_specs=[blk, blk, blk] + part_specs,
        out_specs=[blk] * 4, out_shape=[jax.ShapeDtypeStruct((r, c), F32)] * 4,
        compiler_params=_params(("parallel",)),
    )(w, m, v, *parts)


MATS = ("a_w_qkv", "a_w_o", "b_w_qkv", "b_w_o", "c_w_qkv", "c_w_o", "mlp_w1", "mlp_w2")
SMALLS = ("attn_norm", "mlp_norm", "a_q_gain", "a_k_gain", "c_sinks", "final_norm")
WEIGHTS = ("attn_norm", "mlp_norm", "a_w_qkv", "a_q_gain", "a_k_gain", "a_w_o", "b_w_qkv", "b_w_o", "c_w_qkv",
           "c_sinks", "c_w_o", "mlp_w1", "mlp_w2", "final_norm")
MIXER_OF_LAYER = tuple((layer % N_MIXERS, sum(1 for q in range(layer) if q % N_MIXERS == layer % N_MIXERS))
                       for layer in range(DEPTH))
SMALL_ROWS = 8


def _pack_small(values):
    rows, spans, at = [], [], 0
    for v in values:
        flat = v.reshape(-1)
        n = -(-flat.shape[0] // (SMALL_ROWS * LANES)) * SMALL_ROWS
        rows.append(jnp.pad(flat, (0, n * LANES - flat.shape[0])).reshape(n, LANES))
        spans.append((at, n))
        at += n
    return jnp.concatenate(rows, axis=0), spans


def kernel(x, attn_norm, mlp_norm, a_w_qkv, a_q_gain, a_k_gain, a_w_o, b_w_qkv, b_w_o, c_w_qkv, c_sinks, c_w_o, mlp_w1, mlp_w2, final_norm, loss_target, m_attn_norm, m_mlp_norm, m_a_w_qkv, m_a_q_gain, m_a_k_gain, m_a_w_o, m_b_w_qkv, m_b_w_o, m_c_w_qkv, m_c_sinks, m_c_w_o, m_mlp_w1, m_mlp_w2, m_final_norm, v_attn_norm, v_mlp_norm, v_a_w_qkv, v_a_q_gain, v_a_k_gain, v_a_w_o, v_b_w_qkv, v_b_w_o, v_c_w_qkv, v_c_sinks, v_c_w_o, v_mlp_w1, v_mlp_w2, v_final_norm):
    env = dict(locals())
    w = {name: env[name] for name in WEIGHTS}
    mom = {name: (env["m_" + name], env["v_" + name]) for name in WEIGHTS}

    prefix = ("a", "b", "c")
    shards, mixer_params = [], []
    for layer, (kind, j) in enumerate(MIXER_OF_LAYER):
        shards.append(dict(w_qkv=w[prefix[kind] + "_w_qkv"][j].astype(BF16), w_o=w[prefix[kind] + "_w_o"][j].astype(BF16),
                           w1=mlp_w1[layer].astype(BF16), w2=mlp_w2[layer].astype(BF16)))
        if kind == 0:
            mixer_params.append(dict(gq2=jnp.tile(a_q_gain[j], 2)[None], gk2=jnp.tile(a_k_gain[j], 2)[None]))
        elif kind == 1:
            mixer_params.append(dict(slopes=_per_head(_alibi_slopes(len(B_GROUPS) * B_HEADS_PER_GROUP))))
        else:
            mixer_params.append(dict(slopes=_per_head(_alibi_slopes(C_HEADS)), sinks=_per_head(c_sinks[j])))

    norms = dict(attn=attn_norm, mlp=mlp_norm, final=final_norm)
    loss_part, grad_x, own, received, pending, g_small = _local_step(x[0], loss_target[0], norms, mixer_params, shards)
    loss = lax.psum(loss_part[0, 0], ("x", "y", "c"))

    of_kind = lambda kind, key: jnp.stack([g_small["mixer"][layer][key] for layer, (k, _) in enumerate(MIXER_OF_LAYER)
                                           if k == kind])
    small_grads = dict(
        attn_norm=jnp.concatenate(g_small["attn"], axis=0), mlp_norm=jnp.concatenate(g_small["mlp"], axis=0),
        a_q_gain=of_kind(0, "q_gain"), a_k_gain=of_kind(0, "k_gain"), c_sinks=of_kind(2, "sinks"),
        final_norm=g_small["final"][0])
    packed, spans = _pack_small([small_grads[name] for name in SMALLS])
    *last, all_small = _run_cargo(_scatter_cargo([own[item] for item in pending], packed), name="scatter_last")
    received.update(zip(pending, last))

    me_chip = 2 * lax.axis_index("x") + lax.axis_index("y")
    partial = []
    for name in MATS:
        key = name[2:] if name[0] in "abc" else name[4:]
        layers = [layer for layer, (kind, _) in enumerate(MIXER_OF_LAYER)
                  if name.startswith("mlp") or prefix[kind] == name[0]]
        sums = [_sum_quarters(lax.dynamic_index_in_dim(own[layer, key], me_chip, axis=0, keepdims=False),
                              received[layer, key], name=f"sum_{name}_l{layer}") for layer in layers]
        partial.append(jnp.concatenate(sums, axis=0))
    other = _swap_cores(partial)

    out = {}
    for name, mine, theirs in zip(MATS, partial, other):
        shape = w[name].shape
        res = _adamw(*[a.reshape(-1, shape[-1]) for a in (w[name], *mom[name])], [mine, theirs], name=f"adamw_{name}")
        out[name] = [a.reshape(shape) for a in res]
    for name, (at, n) in zip(SMALLS, spans):
        shape = w[name].shape
        packed_in = [_pack_small([a])[0] for a in (w[name], *mom[name])]
        res = _adamw(*packed_in, [all_small[:, at:at + n]], name=f"adamw_{name}")
        out[name] = [a.reshape(-1)[:w[name].size].reshape(shape) for a in res]

    return (loss, grad_x[None], *[out[name][0] for name in WEIGHTS], *[out[name][1] for name in WEIGHTS],
            *[out[name][2] for name in WEIGHTS], *[out[name][3] for name in WEIGHTS])
```

```python
from typing import Callable, NamedTuple

import jax
import jax.numpy as jnp
from jax import lax
from jax.experimental import pallas as pl
from jax.experimental.pallas import tpu as pltpu

F32 = jnp.float32
BF16 = jnp.bfloat16
MESH = pl.DeviceIdType.MESH
ANY = pl.BlockSpec(memory_space=pl.ANY)

D_MODEL = 1024
HEAD_DIM = 64
GRID_W = 64
ROPE_THETA = 10000.0
RMS_EPS = 1e-6
QK_SCALE = HEAD_DIM ** -0.5
LOG2E = 1.4426950408889634
LN2 = 0.6931471805599453
A_HEADS, A_KV = 16, 4
B_GROUPS = ((128, 1), (512, 4), (2048, 16))
B_HEADS_PER_GROUP, B_KV_PER_GROUP = 6, 2
C_HEADS, C_KV, C_WINDOW = 16, 4, 128
DEPTH, N_MIXERS = 4, 3
ADAM_LR, ADAM_B1, ADAM_B2, ADAM_EPS, ADAM_WD, ADAM_STEP = 0.001, 0.9, 0.999, 1e-08, 0.01, 10

WIN_PAD = 128
V7X_VMEM_BUDGET = 48 * 1024 * 1024
LANES = 128
ROW_TILE = 1024


def _params(semantics):
    return pltpu.CompilerParams(dimension_semantics=semantics, vmem_limit_bytes=V7X_VMEM_BUDGET)


def _tile(n, cap):
    if n <= cap:
        return n
    t = (cap // LANES) * LANES
    while n % t:
        t -= LANES
    return t


def _norm_mm(h, gain, w, *, out_dtype, relu2, name):
    m, d = h.shape
    n = w.shape[1]
    tm, tn = min(ROW_TILE, m), _tile(n, 2048)

    def body(h_ref, g_ref, w_ref, hn_ref, y_ref):
        @pl.when(pl.program_id(1) == 0)
        def _():
            x = h_ref[...]
            r = lax.rsqrt(jnp.mean(x * x, axis=-1, keepdims=True) + RMS_EPS)
            hn_ref[...] = (x * r * g_ref[...]).astype(BF16)

        y = jnp.dot(hn_ref[...], w_ref[...], preferred_element_type=F32)
        if relu2:
            y = jnp.maximum(y, 0.0)
            y = y * y
        y_ref[...] = y.astype(y_ref.dtype)

    return pl.pallas_call(
        body, name=name, grid=(m // tm, n // tn),
        in_specs=[pl.BlockSpec((tm, d), lambda i, j: (i, 0)), pl.BlockSpec((1, d), lambda i, j: (0, 0)),
                  pl.BlockSpec((d, tn), lambda i, j: (0, j))],
        out_specs=[pl.BlockSpec((tm, d), lambda i, j: (i, 0)), pl.BlockSpec((tm, tn), lambda i, j: (i, j))],
        out_shape=[jax.ShapeDtypeStruct((m, d), BF16), jax.ShapeDtypeStruct((m, n), out_dtype)],
        compiler_params=_params(("parallel", "arbitrary")),
    )(h, gain, w)


def _mm_res(a, w, h_in, *, name):
    m, k = a.shape
    d = w.shape[1]
    tm, tk = min(ROW_TILE, m), _tile(k, 1152)

    def body(a_ref, w_ref, h_ref, o_ref):
        @pl.when(pl.program_id(1) == 0)
        def _():
            o_ref[...] = h_ref[...]

        o_ref[...] += jnp.dot(a_ref[...], w_ref[...], preferred_element_type=F32)

    return pl.pallas_call(
        body, name=name, grid=(m // tm, k // tk),
        in_specs=[pl.BlockSpec((tm, tk), lambda i, j: (i, j)), pl.BlockSpec((tk, d), lambda i, j: (j, 0)),
                  pl.BlockSpec((tm, d), lambda i, j: (i, 0))],
        out_specs=pl.BlockSpec((tm, d), lambda i, j: (i, 0)),
        out_shape=jax.ShapeDtypeStruct((m, d), F32),
        compiler_params=_params(("parallel", "arbitrary")),
    )(a, w, h_in)


def _mm_nt(a, w, act, *, name):
    m, d = a.shape
    n = w.shape[0]
    tm, tn = min(ROW_TILE, m), _tile(n, 1152)

    def body(*refs):
        a_ref, w_ref = refs[0], refs[1]
        o_ref = refs[-1]
        acc = lax.dot_general(a_ref[...].astype(BF16), w_ref[...], (((1,), (1,)), ((), ())),
                              preferred_element_type=F32)
        if act is not None:
            acc = acc * (2.0 * jnp.sqrt(refs[2][...].astype(F32)))
        o_ref[...] = acc.astype(BF16)

    in_specs = [pl.BlockSpec((tm, d), lambda i, j: (i, 0)), pl.BlockSpec((tn, d), lambda i, j: (j, 0))]
    args = [a, w]
    if act is not None:
        in_specs.append(pl.BlockSpec((tm, tn), lambda i, j: (i, j)))
        args.append(act)
    return pl.pallas_call(
        body, name=name, grid=(m // tm, n // tn), in_specs=in_specs,
        out_specs=pl.BlockSpec((tm, tn), lambda i, j: (i, j)),
        out_shape=jax.ShapeDtypeStruct((m, n), BF16),
        compiler_params=_params(("parallel", "parallel")),
    )(*args)


def _rmsnorm_bwd(dn, x, gain):
    r = lax.rsqrt(jnp.mean(x * x, axis=-1, keepdims=True) + RMS_EPS)
    xh = x * r
    dgain = jnp.sum(dn * xh, axis=0, keepdims=True)
    u = dn * gain
    dx = r * (u - xh * jnp.mean(u * xh, axis=-1, keepdims=True))
    return dx, dgain


def _mm_nt_normbwd(g, w, h, gain, dh_in, *, name):
    m, k = g.shape
    d = w.shape[0]
    tm, tk = min(ROW_TILE // 2, m), _tile(k, 1024)
    nk = k // tk

    def body(g_ref, w_ref, h_ref, gain_ref, dh_ref, o_ref, dg_ref, acc_ref):
        i, j = pl.program_id(0), pl.program_id(1)

        @pl.when((i == 0) & (j == 0))
        def _():
            dg_ref[...] = jnp.zeros_like(dg_ref)

        @pl.when(j == 0)
        def _():
            acc_ref[...] = jnp.zeros_like(acc_ref)

        acc_ref[...] += lax.dot_general(g_ref[...], w_ref[...], (((1,), (1,)), ((), ())),
                                        preferred_element_type=F32)

        @pl.when(j == nk - 1)
        def _():
            dx, dgain = _rmsnorm_bwd(acc_ref[...], h_ref[...], gain_ref[...])
            dg_ref[...] += dgain
            o_ref[...] = dh_ref[...] + dx

    return pl.pallas_call(
        body, name=name, grid=(m // tm, nk),
        in_specs=[pl.BlockSpec((tm, tk), lambda i, j: (i, j)), pl.BlockSpec((d, tk), lambda i, j: (0, j)),
                  pl.BlockSpec((tm, d), lambda i, j: (i, 0)), pl.BlockSpec((1, d), lambda i, j: (0, 0)),
                  pl.BlockSpec((tm, d), lambda i, j: (i, 0))],
        out_specs=[pl.BlockSpec((tm, d), lambda i, j: (i, 0)), pl.BlockSpec((1, d), lambda i, j: (0, 0))],
        out_shape=[jax.ShapeDtypeStruct((m, d), F32), jax.ShapeDtypeStruct((1, d), F32)],
        scratch_shapes=[pltpu.VMEM((tm, d), F32)],
        compiler_params=_params(("arbitrary", "arbitrary")),
    )(g, w, h, gain, dh_in)


def _mm_tn(x, g, *, name):
    m, k = x.shape
    n = g.shape[1]
    tm, tk, tn = min(ROW_TILE, m), _tile(k, 1152), _tile(n, 1024)
    nm = m // tm

    def body(x_ref, g_ref, o_ref, acc_ref):
        s = pl.program_id(2)

        @pl.when(s == 0)
        def _():
            acc_ref[...] = jnp.zeros_like(acc_ref)

        acc_ref[...] += lax.dot_general(x_ref[...], g_ref[...].astype(BF16), (((0,), (0,)), ((), ())),
                                        preferred_element_type=F32)

        @pl.when(s == nm - 1)
        def _():
            o_ref[...] = acc_ref[...].astype(BF16)

    return pl.pallas_call(
        body, name=name, grid=(k // tk, n // tn, nm),
        in_specs=[pl.BlockSpec((tm, tk), lambda a, b, s: (s, a)), pl.BlockSpec((tm, tn), lambda a, b, s: (s, b))],
        out_specs=pl.BlockSpec((tk, tn), lambda a, b, s: (a, b)),
        out_shape=jax.ShapeDtypeStruct((k, n), BF16),
        scratch_shapes=[pltpu.VMEM((tk, tn), F32)],
        compiler_params=_params(("parallel", "parallel", "arbitrary")),
    )(x, g)


def _loss_head(h, gain, target):
    m, d = h.shape
    tm = 512

    def body(h_ref, g_ref, t_ref, dh_ref, loss_ref, dg_ref):
        @pl.when(pl.program_id(0) == 0)
        def _():
            loss_ref[...] = jnp.zeros_like(loss_ref)
            dg_ref[...] = jnp.zeros_like(dg_ref)

        x = h_ref[...]
        gain_v = g_ref[...]
        r = lax.rsqrt(jnp.mean(x * x, axis=-1, keepdims=True) + RMS_EPS)
        err = x * r * gain_v - t_ref[...]
        loss_ref[...] += 0.5 * jnp.sum(jnp.mean(err * err, axis=-1, keepdims=True), axis=0, keepdims=True)
        dx, dgain = _rmsnorm_bwd(err * (1.0 / d), x, gain_v)
        dg_ref[...] += dgain
        dh_ref[...] = dx

    return pl.pallas_call(
        body, name="loss_head", grid=(m // tm,),
        in_specs=[pl.BlockSpec((tm, d), lambda i: (i, 0)), pl.BlockSpec((1, d), lambda i: (0, 0)),
                  pl.BlockSpec((tm, d), lambda i: (i, 0))],
        out_specs=[pl.BlockSpec((tm, d), lambda i: (i, 0)), pl.BlockSpec((1, LANES), lambda i: (0, 0)),
                   pl.BlockSpec((1, d), lambda i: (0, 0))],
        out_shape=[jax.ShapeDtypeStruct((m, d), F32), jax.ShapeDtypeStruct((1, LANES), F32),
                   jax.ShapeDtypeStruct((1, d), F32)],
        compiler_params=_params(("arbitrary",)),
    )(h, gain, target)


def _rope_tables(s):
    t = jnp.arange(s)
    row = (t // GRID_W).astype(F32)
    col = (t % GRID_W).astype(F32)
    axis_dim = HEAD_DIM // 2
    inv_freq = ROPE_THETA ** (-jnp.arange(0, axis_dim, 2, dtype=F32) / axis_dim)
    ar, ac = row[:, None] * inv_freq, col[:, None] * inv_freq
    cos = jnp.concatenate([jnp.cos(ar), jnp.cos(ar), jnp.cos(ac), jnp.cos(ac)], axis=-1)
    sin = jnp.concatenate([-jnp.sin(ar), jnp.sin(ar), -jnp.sin(ac), jnp.sin(ac)], axis=-1)
    return jnp.tile(cos, (1, 2)), jnp.tile(sin, (1, 2))


def _swap16(x):
    lane = lax.broadcasted_iota(jnp.int32, x.shape, 1)
    return jnp.where((lane % 32) < 16, pltpu.roll(x, LANES - 16, 1), pltpu.roll(x, 16, 1))


def _head_mean(v):
    lane = lax.broadcasted_iota(jnp.int32, v.shape, 1)
    lo = lane < HEAD_DIM
    s_all = jnp.sum(v, axis=-1, keepdims=True)
    s_lo = jnp.sum(jnp.where(lo, v, 0.0), axis=-1, keepdims=True)
    return jnp.where(lo, s_lo, s_all - s_lo) * (1.0 / HEAD_DIM)


def _norm_rope(x, gain2, cos, sin):
    r = lax.rsqrt(_head_mean(x * x) + RMS_EPS)
    nrm = x * r * gain2
    return nrm * cos + _swap16(nrm) * sin


def _norm_rope_bwd(dy, x, gain2, cos, sin):
    dn = dy * cos + _swap16(dy * sin)
    r = lax.rsqrt(_head_mean(x * x) + RMS_EPS)
    xh = x * r
    dgain = jnp.sum(dn * xh, axis=0, keepdims=True)
    u = dn * gain2
    return r * (u - xh * _head_mean(u * xh)), dgain


def _a_prep(qkv, cos, sin, gq2, gk2):
    s = qkv.shape[0]
    tr = 256
    nq, nk = A_HEADS * HEAD_DIM, A_KV * HEAD_DIM

    def body(qkv_ref, cos_ref, sin_ref, gq_ref, gk_ref, qt_ref, k_ref, v_ref):
        cos_v, sin_v = cos_ref[...], sin_ref[...]
        for c in range(nq // LANES):
            y = _norm_rope(qkv_ref[:, c * LANES:(c + 1) * LANES], gq_ref[...], cos_v, sin_v) * (QK_SCALE * LOG2E)
            yt = y.T
            qt_ref[2 * c] = yt[:HEAD_DIM].astype(BF16)
            qt_ref[2 * c + 1] = yt[HEAD_DIM:].astype(BF16)
        for c in range(nk // LANES):
            y = _norm_rope(qkv_ref[:, nq + c * LANES:nq + (c + 1) * LANES], gk_ref[...], cos_v, sin_v)
            k_ref[2 * c] = y[:, :HEAD_DIM].astype(BF16)
            k_ref[2 * c + 1] = y[:, HEAD_DIM:].astype(BF16)
            x = qkv_ref[:, nq + nk + c * LANES:nq + nk + (c + 1) * LANES]
            v_ref[2 * c] = x[:, :HEAD_DIM].astype(BF16)
            v_ref[2 * c + 1] = x[:, HEAD_DIM:].astype(BF16)

    return pl.pallas_call(
        body, name="a_prep", grid=(s // tr,),
        in_specs=[pl.BlockSpec((tr, nq + 2 * nk), lambda i: (i, 0)), pl.BlockSpec((tr, LANES), lambda i: (i, 0)),
                  pl.BlockSpec((tr, LANES), lambda i: (i, 0)), pl.BlockSpec((1, LANES), lambda i: (0, 0)),
                  pl.BlockSpec((1, LANES), lambda i: (0, 0))],
        out_specs=[pl.BlockSpec((A_HEADS, HEAD_DIM, tr), lambda i: (0, 0, i)),
                   pl.BlockSpec((A_KV, tr, HEAD_DIM), lambda i: (0, i, 0)),
                   pl.BlockSpec((A_KV, tr, HEAD_DIM), lambda i: (0, i, 0))],
        out_shape=[jax.ShapeDtypeStruct((A_HEADS, HEAD_DIM, s), BF16), jax.ShapeDtypeStruct((A_KV, s, HEAD_DIM), BF16),
                   jax.ShapeDtypeStruct((A_KV, s, HEAD_DIM), BF16)],
        compiler_params=_params(("parallel",)),
    )(qkv, cos, sin, gq2, gk2)


def _a_prep_bwd(dqt, dkt, dvt, qkv, cos, sin, gq2, gk2):
    s = qkv.shape[0]
    tr = 256
    nq, nk = A_HEADS * HEAD_DIM, A_KV * HEAD_DIM

    def body(dqt_ref, dkt_ref, dvt_ref, qkv_ref, cos_ref, sin_ref, gq_ref, gk_ref, o_ref, dgq_ref, dgk_ref):
        @pl.when(pl.program_id(0) == 0)
        def _():
            dgq_ref[...] = jnp.zeros_like(dgq_ref)
            dgk_ref[...] = jnp.zeros_like(dgk_ref)

        cos_v, sin_v = cos_ref[...], sin_ref[...]

        def pair(ref, c):
            return jnp.concatenate([ref[2 * c], ref[2 * c + 1]], axis=0).T

        for c in range(nq // LANES):
            dx, dg = _norm_rope_bwd(pair(dqt_ref, c) * QK_SCALE, qkv_ref[:, c * LANES:(c + 1) * LANES],
                                    gq_ref[...], cos_v, sin_v)
            o_ref[:, c * LANES:(c + 1) * LANES] = dx.astype(BF16)
            dgq_ref[...] += dg
        for c in range(nk // LANES):
            lo = nq + c * LANES
            dx, dg = _norm_rope_bwd(pair(dkt_ref, c) * LN2, qkv_ref[:, lo:lo + LANES], gk_ref[...], cos_v, sin_v)
            o_ref[:, lo:lo + LANES] = dx.astype(BF16)
            dgk_ref[...] += dg
            o_ref[:, lo + nk:lo + nk + LANES] = pair(dvt_ref, c).astype(BF16)

    return pl.pallas_call(
        body, name="a_prep_bwd", grid=(s // tr,),
        in_specs=[pl.BlockSpec((A_HEADS, HEAD_DIM, tr), lambda i: (0, 0, i)),
                  pl.BlockSpec((A_KV, HEAD_DIM, tr), lambda i: (0, 0, i)),
                  pl.BlockSpec((A_KV, HEAD_DIM, tr), lambda i: (0, 0, i)),
                  pl.BlockSpec((tr, nq + 2 * nk), lambda i: (i, 0)), pl.BlockSpec((tr, LANES), lambda i: (i, 0)),
                  pl.BlockSpec((tr, LANES), lambda i: (i, 0)), pl.BlockSpec((1, LANES), lambda i: (0, 0)),
                  pl.BlockSpec((1, LANES), lambda i: (0, 0))],
        out_specs=[pl.BlockSpec((tr, nq + 2 * nk), lambda i: (i, 0)), pl.BlockSpec((1, LANES), lambda i: (0, 0)),
                   pl.BlockSpec((1, LANES), lambda i: (0, 0))],
        out_shape=[jax.ShapeDtypeStruct((s, nq + 2 * nk), BF16), jax.ShapeDtypeStruct((1, LANES), F32),
                   jax.ShapeDtypeStruct((1, LANES), F32)],
        compiler_params=_params(("arbitrary",)),
    )(dqt, dkt, dvt, qkv, cos, sin, gq2, gk2)


A_TQ = 1024
A_TQ_SUB = 256
A_TQ_BWD = 512
A_KEY_CHUNK = 512


def _a_attn_fwd(qt, k, v, cargo, *, name):
    nh, _, s = qt.shape
    rep = nh // k.shape[0]
    tq = min(A_TQ, s)
    sub = min(A_TQ_SUB, tq)
    grid = (nh, s // tq)

    def body(qt_ref, k_ref, v_ref, o_ref, lse_ref):
        scores = [jnp.dot(k_ref[0], qt_ref[0, :, a:a + sub], preferred_element_type=F32)
                  for a in range(0, tq, sub)]
        for a, st in zip(range(0, tq, sub), scores):
            mx = jnp.max(st, axis=0, keepdims=True)
            p = jnp.exp2(st - mx)
            den = jnp.sum(p, axis=0, keepdims=True)
            ot = lax.dot_general(v_ref[0], p.astype(BF16), (((0,), (0,)), ((), ())), preferred_element_type=F32)
            o_ref[0, :, a:a + sub] = (ot / den).astype(BF16)
            lse_ref[0, :, a:a + sub] = mx + jnp.log(den) * LOG2E

    carried = _carry(cargo, grid, 3, 2, body)
    res = pl.pallas_call(
        carried.body, name=name, grid=grid,
        in_specs=[pl.BlockSpec((1, HEAD_DIM, tq), lambda h, i: (h, 0, i)),
                  pl.BlockSpec((1, s, HEAD_DIM), lambda h, i: (h // rep, 0, 0)),
                  pl.BlockSpec((1, s, HEAD_DIM), lambda h, i: (h // rep, 0, 0))] + carried.in_specs,
        out_specs=[pl.BlockSpec((1, HEAD_DIM, tq), lambda h, i: (h, 0, i)),
                   pl.BlockSpec((1, 1, tq), lambda h, i: (h, 0, i))] + carried.out_specs,
        out_shape=[jax.ShapeDtypeStruct((nh, HEAD_DIM, s), BF16), jax.ShapeDtypeStruct((nh, 1, s), F32)]
        + carried.out_shape,
        scratch_shapes=carried.scratch,
        compiler_params=_params(("arbitrary", "arbitrary")),
    )(qt, k, v, *carried.args)
    return res[0], res[1], res[2:]


def _a_attn_bwd(qt, k, v, dot, ot, lse, cargo, *, name):
    nh, _, s = qt.shape
    nkv = k.shape[0]
    rep = nh // nkv
    tq, ck = min(A_TQ_BWD, s), min(A_KEY_CHUNK, s)
    grid = (nh, s // tq)

    def body(qt_ref, k_ref, v_ref, dot_ref, ot_ref, lse_ref, dq_ref, dk_ref, dv_ref):
        h, i = pl.program_id(0), pl.program_id(1)

        @pl.when((h % rep == 0) & (i == 0))
        def _():
            dk_ref[...] = jnp.zeros_like(dk_ref)
            dv_ref[...] = jnp.zeros_like(dv_ref)

        q_t, do_t, lse_v = qt_ref[0], dot_ref[0], lse_ref[0]
        delta = jnp.sum(do_t.astype(F32) * ot_ref[0].astype(F32), axis=0, keepdims=True)
        nt = (((1,), (1,)), ((), ()))
        dq = jnp.zeros((HEAD_DIM, tq), F32)
        for c in range(s // ck):
            keys = slice(c * ck, (c + 1) * ck)
            kc = k_ref[0, keys, :]
            p = jnp.exp2(jnp.dot(kc, q_t, preferred_element_type=F32) - lse_v)
            dp = jnp.dot(v_ref[0, keys, :], do_t, preferred_element_type=F32)
            ds = (p * (dp - delta)).astype(BF16)
            dv_ref[0, :, keys] += lax.dot_general(do_t, p.astype(BF16), nt, preferred_element_type=F32)
            dk_ref[0, :, keys] += lax.dot_general(q_t, ds, nt, preferred_element_type=F32)
            dq = dq + lax.dot_general(kc, ds, (((0,), (0,)), ((), ())), preferred_element_type=F32)
        dq_ref[0] = dq

    blk_q = pl.BlockSpec((1, HEAD_DIM, tq), lambda h, i: (h, 0, i))
    blk_row = pl.BlockSpec((1, 1, tq), lambda h, i: (h, 0, i))
    blk_kv = pl.BlockSpec((1, s, HEAD_DIM), lambda h, i: (h // rep, 0, 0))
    blk_acc = pl.BlockSpec((1, HEAD_DIM, s), lambda h, i: (h // rep, 0, 0))
    carried = _carry(cargo, grid, 6, 3, body)
    res = pl.pallas_call(
        carried.body, name=name, grid=grid,
        in_specs=[blk_q, blk_kv, blk_kv, blk_q, blk_q, blk_row] + carried.in_specs,
        out_specs=[blk_q, blk_acc, blk_acc] + carried.out_specs,
        out_shape=[jax.ShapeDtypeStruct((nh, HEAD_DIM, s), F32), jax.ShapeDtypeStruct((nkv, HEAD_DIM, s), F32),
                   jax.ShapeDtypeStruct((nkv, HEAD_DIM, s), F32)] + carried.out_shape,
        scratch_shapes=carried.scratch,
        compiler_params=_params(("arbitrary", "arbitrary")),
    )(qt, k, v, dot, ot, lse, *carried.args)
    return res[0], res[1], res[2], res[3:]


WIN_FAR = 1e30


def _win_penalty(i, tq, tk, window, dil, seg):
    qpos = i * tq + lax.broadcasted_iota(jnp.int32, (tk, tq), 1)
    kpos = i * tq - WIN_PAD + lax.broadcasted_iota(jnp.int32, (tk, tq), 0)
    dist = jnp.abs(kpos - qpos)
    seg_lo = qpos - (qpos & (seg - 1))
    valid = (dist <= window) & (kpos >= seg_lo) & (kpos < seg_lo + seg)
    return jnp.where(valid, (dist * dil).astype(F32), WIN_FAR)


def _win_scores(kw, q_t, slope, pen):
    return jnp.dot(kw, q_t, preferred_element_type=F32) * (QK_SCALE * LOG2E) - (slope * LOG2E) * pen


def _win_tq(s):
    return min(512, s)


def _win_fwd(qt, kp, vp, slopes, sinks, *, window, dil, seg, out_dtype, name):
    nh, _, s = qt.shape
    rep = nh // kp.shape[0]
    tq = _win_tq(s)
    tk = tq + 2 * WIN_PAD
    sp = s + 2 * WIN_PAD

    def body(*refs):
        qt_ref, k_ref, v_ref, sl_ref = refs[:4]
        o_ref, lse_ref, pen_ref = refs[-3:]
        i, h = pl.program_id(0), pl.program_id(1)

        @pl.when(h == 0)
        def _():
            pen_ref[...] = _win_penalty(i, tq, tk, window, dil, seg)

        base = pl.multiple_of(i * tq, tq)
        st = _win_scores(k_ref[0, pl.ds(base, tk), :], qt_ref[0], sl_ref[0][:, :1], pen_ref[...])
        mx = jnp.max(st, axis=0, keepdims=True)
        if sinks is not None:
            sink = refs[4][0][:, :1] * LOG2E
            mx = jnp.maximum(mx, sink)
        p = jnp.exp2(st - mx)
        den = jnp.sum(p, axis=0, keepdims=True)
        if sinks is not None:
            den = den + jnp.exp2(sink - mx)
        ot = lax.dot_general(v_ref[0, pl.ds(base, tk), :], p.astype(BF16), (((0,), (0,)), ((), ())),
                             preferred_element_type=F32)
        o_ref[0] = (ot / den).astype(o_ref.dtype)
        lse_ref[0] = mx * LN2 + jnp.log(den)

    blk_q = pl.BlockSpec((1, HEAD_DIM, tq), lambda i, h: (h, 0, i))
    blk_kv = pl.BlockSpec((1, sp, HEAD_DIM), lambda i, h: (h // rep, 0, 0))
    blk_h = pl.BlockSpec((1, 1, LANES), lambda i, h: (h, 0, 0))
    in_specs, args = [blk_q, blk_kv, blk_kv, blk_h], [qt, kp, vp, slopes]
    if sinks is not None:
        in_specs.append(blk_h)
        args.append(sinks)
    return pl.pallas_call(
        body, name=name, grid=(s // tq, nh), in_specs=in_specs,
        out_specs=[blk_q, pl.BlockSpec((1, 1, tq), lambda i, h: (h, 0, i))],
        out_shape=[jax.ShapeDtypeStruct((nh, HEAD_DIM, s), out_dtype), jax.ShapeDtypeStruct((nh, 1, s), F32)],
        scratch_shapes=[pltpu.VMEM((tk, tq), F32)],
        compiler_params=_params(("arbitrary", "arbitrary")),
    )(*args)


def _win_bwd(qt, kp, vp, slopes, sinks, dot, ot, delta, *, window, dil, seg, name):
    nh, _, s = qt.shape
    nkv = kp.shape[0]
    rep = nh // nkv
    tq = _win_tq(s)
    tk = tq + 2 * WIN_PAD
    sp = s + 2 * WIN_PAD
    n_in = 6 + (sinks is not None)

    def body(*refs):
        qt_ref, k_ref, v_ref, sl_ref, dot_ref, aux_ref = refs[:6]
        outs, pen_ref = refs[n_in:-1], refs[-1]
        dq_ref, dk_ref, dv_ref = outs[:3]
        i, h = pl.program_id(0), pl.program_id(1)

        @pl.when((i == 0) & (h == 0))
        def _():
            dk_ref[...] = jnp.zeros_like(dk_ref)
            dv_ref[...] = jnp.zeros_like(dv_ref)
            if sinks is not None:
                outs[3][...] = jnp.zeros_like(outs[3])

        @pl.when(h == 0)
        def _():
            pen_ref[...] = _win_penalty(i, tq, tk, window, dil, seg)

        base = pl.multiple_of(i * tq, tq)
        win = pl.ds(base, tk)
        kv = h // rep
        kw, q_t, do_t = k_ref[0, win, :], qt_ref[0], dot_ref[0]
        st = _win_scores(kw, q_t, sl_ref[0][:, :1], pen_ref[...])
        mx = jnp.max(st, axis=0, keepdims=True)
        if sinks is not None:
            sink = refs[6][0][:, :1] * LOG2E
            mx = jnp.maximum(mx, sink)
        p = jnp.exp2(st - mx)
        den = jnp.sum(p, axis=0, keepdims=True)
        if sinks is not None:
            p_sink = jnp.exp2(sink - mx)
            den = den + p_sink
        p = p / den
        dp = jnp.dot(v_ref[0, win, :], do_t, preferred_element_type=F32)
        if delta is None:
            row = jnp.sum(do_t.astype(F32) * aux_ref[0].astype(F32), axis=0, keepdims=True)
        else:
            row = aux_ref[0]
        ds = (p * (dp - row) * QK_SCALE).astype(BF16)
        nt = (((1,), (1,)), ((), ()))
        dv_ref[kv, :, win] += lax.dot_general(do_t, p.astype(BF16), nt, preferred_element_type=F32)
        dk_ref[kv, :, win] += lax.dot_general(q_t, ds, nt, preferred_element_type=F32)
        dq_ref[0] = lax.dot_general(kw, ds, (((0,), (0,)), ((), ())), preferred_element_type=F32)
        if sinks is not None:
            outs[3][h] += jnp.zeros((1, LANES), F32) - jnp.sum(p_sink / den * row, axis=1, keepdims=True)

    blk_q = pl.BlockSpec((1, HEAD_DIM, tq), lambda i, h: (h, 0, i))
    blk_row = pl.BlockSpec((1, 1, tq), lambda i, h: (h, 0, i))
    blk_kv = pl.BlockSpec((1, sp, HEAD_DIM), lambda i, h: (h // rep, 0, 0))
    blk_acc = pl.BlockSpec((nkv, HEAD_DIM, sp), lambda i, h: (0, 0, 0))
    blk_h = pl.BlockSpec((1, 1, LANES), lambda i, h: (h, 0, 0))
    in_specs = [blk_q, blk_kv, blk_kv, blk_h, blk_q, blk_q if delta is None else blk_row]
    args = [qt, kp, vp, slopes, dot, ot if delta is None else delta]
    out_specs = [blk_q, blk_acc, blk_acc]
    out_shape = [jax.ShapeDtypeStruct((nh, HEAD_DIM, s), F32), jax.ShapeDtypeStruct((nkv, HEAD_DIM, sp), F32),
                 jax.ShapeDtypeStruct((nkv, HEAD_DIM, sp), F32)]
    if sinks is not None:
        in_specs.append(blk_h)
        args.append(sinks)
        out_specs.append(pl.BlockSpec((nh, 1, LANES), lambda i, h: (0, 0, 0)))
        out_shape.append(jax.ShapeDtypeStruct((nh, 1, LANES), F32))
    res = pl.pallas_call(
        body, name=name, grid=(s // tq, nh), in_specs=in_specs, out_specs=out_specs, out_shape=out_shape,
        scratch_shapes=[pltpu.VMEM((tk, tq), F32)],
        compiler_params=_params(("arbitrary", "arbitrary")),
    )(*args)
    return res if sinks is not None else (*res, None)


def _group_weights(lse):
    e = jnp.exp(lse - jnp.max(lse, axis=0, keepdims=True))
    return e / jnp.sum(e, axis=0, keepdims=True)


def _b_combine_fwd(ot, lse):
    nh, _, s = ot.shape
    ng, hg, _ = lse.shape
    ts = min(512, s)

    def body(ot_ref, lse_ref, o_ref):
        alpha = _group_weights(lse_ref[...])
        for g in range(ng):
            for j in range(hg):
                o_ref[g * hg + j] = (ot_ref[g * hg + j] * alpha[g, j:j + 1, :]).astype(BF16)

    return pl.pallas_call(
        body, name="b_combine_fwd", grid=(s // ts,),
        in_specs=[pl.BlockSpec((nh, HEAD_DIM, ts), lambda i: (0, 0, i)), pl.BlockSpec((ng, hg, ts), lambda i: (0, 0, i))],
        out_specs=pl.BlockSpec((nh, HEAD_DIM, ts), lambda i: (0, 0, i)),
        out_shape=jax.ShapeDtypeStruct((nh, HEAD_DIM, s), BF16),
        compiler_params=_params(("parallel",)),
    )(ot, lse)


def _b_combine_bwd(dout, ot, lse):
    nh, _, s = ot.shape
    ng, hg, _ = lse.shape
    ts = min(512, s)

    def body(dout_ref, ot_ref, lse_ref, do_ref, delta_ref):
        alpha = _group_weights(lse_ref[...])
        for j in range(hg):
            e = [jnp.sum(dout_ref[g * hg + j].astype(F32) * ot_ref[g * hg + j], axis=0, keepdims=True)
                 for g in range(ng)]
            a = [alpha[g, j:j + 1, :] for g in range(ng)]
            mix = a[0] * e[0]
            for g in range(1, ng):
                mix = mix + a[g] * e[g]
            for g in range(ng):
                do_ref[g * hg + j] = (dout_ref[g * hg + j].astype(F32) * a[g]).astype(BF16)
                delta_ref[g * hg + j] = a[g] * mix

    blk = pl.BlockSpec((nh, HEAD_DIM, ts), lambda i: (0, 0, i))
    return pl.pallas_call(
        body, name="b_combine_bwd", grid=(s // ts,),
        in_specs=[blk, blk, pl.BlockSpec((ng, hg, ts), lambda i: (0, 0, i))],
        out_specs=[blk, pl.BlockSpec((nh, 1, ts), lambda i: (0, 0, i))],
        out_shape=[jax.ShapeDtypeStruct((nh, HEAD_DIM, s), BF16), jax.ShapeDtypeStruct((nh, 1, s), F32)],
        compiler_params=_params(("parallel",)),
    )(dout, ot, lse)


def _alibi_slopes(n):
    return 2.0 ** (-8.0 * jnp.arange(1, n + 1, dtype=F32) / n)


def _per_head(v):
    return jnp.broadcast_to(v.astype(F32)[:, None, None], (v.shape[0], 1, LANES))


def _dilate(x, dil):
    if dil == 1:
        return x
    s = x.shape[-1]
    return jnp.swapaxes(x.reshape(x.shape[:-1] + (s // dil, dil)), -1, -2).reshape(x.shape)


def _undilate(x, dil):
    if dil == 1:
        return x
    s = x.shape[-1]
    return jnp.swapaxes(x.reshape(x.shape[:-1] + (dil, s // dil)), -1, -2).reshape(x.shape)


def _heads_t(x, nh):
    return jnp.transpose(x.reshape(x.shape[0], nh, HEAD_DIM), (1, 2, 0))


def _tokens(xt):
    return jnp.transpose(xt, (2, 0, 1)).reshape(xt.shape[2], -1)


def _pad_tokens(xt):
    return jnp.pad(jnp.swapaxes(xt, 1, 2), ((0, 0), (WIN_PAD, WIN_PAD), (0, 0)))


def _mixer_fwd(kind, qkv, p, tabs, cargo, layer):
    s = qkv.shape[0]
    if kind == 0:
        qt, k, v = _a_prep(qkv, tabs[0], tabs[1], p["gq2"], p["gk2"])
        ot, lse, brought = _a_attn_fwd(qt, k, v, cargo, name=f"a_attn_fwd_l{layer}")
        return _tokens(ot), dict(qt=qt, k=k, v=v, ot=ot, lse=lse), brought
    assert cargo is None
    if kind == 2:
        nq, nk = C_HEADS * HEAD_DIM, C_KV * HEAD_DIM
        qt = _heads_t(qkv[:, :nq], C_HEADS)
        kp = _pad_tokens(_heads_t(qkv[:, nq:nq + nk], C_KV))
        vp = _pad_tokens(_heads_t(qkv[:, nq + nk:], C_KV))
        ot, _ = _win_fwd(qt, kp, vp, p["slopes"], p["sinks"], window=C_WINDOW, dil=1, seg=s, out_dtype=BF16,
                         name="c_attn_fwd")
        return _tokens(ot), dict(qt=qt, kp=kp, vp=vp, ot=ot), ()
    ng, hg, kg = len(B_GROUPS), B_HEADS_PER_GROUP, B_KV_PER_GROUP
    nq, nk = ng * hg * HEAD_DIM, ng * kg * HEAD_DIM
    qt_all = _heads_t(qkv[:, :nq], ng * hg)
    kt_all = _heads_t(qkv[:, nq:nq + nk], ng * kg)
    vt_all = _heads_t(qkv[:, nq + nk:], ng * kg)
    saved, outs, lses = [], [], []
    for g, (window, dil) in enumerate(B_GROUPS):
        qt = _dilate(qt_all[g * hg:(g + 1) * hg], dil)
        kp = _pad_tokens(_dilate(kt_all[g * kg:(g + 1) * kg], dil))
        vp = _pad_tokens(_dilate(vt_all[g * kg:(g + 1) * kg], dil))
        sl = p["slopes"][g * hg:(g + 1) * hg]
        ot, lse = _win_fwd(qt, kp, vp, sl, None, window=window // 2 // dil, dil=dil, seg=s // dil, out_dtype=F32,
                           name=f"b_attn_fwd_g{g}")
        saved.append(dict(qt=qt, kp=kp, vp=vp))
        outs.append(_undilate(ot, dil))
        lses.append(_undilate(lse[:, 0, :], dil))
    ot_all, lse_all = jnp.concatenate(outs, axis=0), jnp.stack(lses, axis=0)
    mixed = _b_combine_fwd(ot_all, lse_all)
    return _tokens(mixed), dict(groups=saved, ot=ot_all, lse=lse_all), ()


def _mixer_bwd(kind, do, qkv, sv, p, tabs, cargo, layer):
    s = do.shape[0]
    small = {}
    if kind == 0:
        dqt, dkt, dvt, brought = _a_attn_bwd(sv["qt"], sv["k"], sv["v"], _heads_t(do, A_HEADS), sv["ot"], sv["lse"],
                                             cargo, name=f"a_attn_bwd_l{layer}")
        dqkv, dgq, dgk = _a_prep_bwd(dqt, dkt, dvt, qkv, tabs[0], tabs[1], p["gq2"], p["gk2"])
        small["q_gain"] = dgq[0, :HEAD_DIM] + dgq[0, HEAD_DIM:]
        small["k_gain"] = dgk[0, :HEAD_DIM] + dgk[0, HEAD_DIM:]
        return dqkv, small, brought
    assert cargo is None
    if kind == 2:
        dqt, dkt, dvt, dsink = _win_bwd(sv["qt"], sv["kp"], sv["vp"], p["slopes"], p["sinks"], _heads_t(do, C_HEADS),
                                        sv["ot"], None, window=C_WINDOW, dil=1, seg=s, name="c_attn_bwd")
        small["sinks"] = dsink[:, 0, 0]
        parts = [dqt, dkt[:, :, WIN_PAD:-WIN_PAD], dvt[:, :, WIN_PAD:-WIN_PAD]]
        return jnp.concatenate([_tokens(x) for x in parts], axis=1).astype(BF16), small, ()
    ng, hg, kg = len(B_GROUPS), B_HEADS_PER_GROUP, B_KV_PER_GROUP
    do_own, delta = _b_combine_bwd(_heads_t(do, ng * hg), sv["ot"], sv["lse"])
    dqs, dks, dvs = [], [], []
    for g, (window, dil) in enumerate(B_GROUPS):
        gs = sv["groups"][g]
        dqt, dkt, dvt, _ = _win_bwd(gs["qt"], gs["kp"], gs["vp"], p["slopes"][g * hg:(g + 1) * hg], None,
                                    _dilate(do_own[g * hg:(g + 1) * hg], dil), None,
                                    _dilate(delta[g * hg:(g + 1) * hg], dil),
                                    window=window // 2 // dil, dil=dil, seg=s // dil, name=f"b_attn_bwd_g{g}")
        dqs.append(_undilate(dqt, dil))
        dks.append(_undilate(dkt[:, :, WIN_PAD:-WIN_PAD], dil))
        dvs.append(_undilate(dvt[:, :, WIN_PAD:-WIN_PAD], dil))
    parts = [jnp.concatenate(x, axis=0) for x in (dqs, dks, dvs)]
    return jnp.concatenate([_tokens(x) for x in parts], axis=1).astype(BF16), small, ()


LAYER_MATS = ("w_qkv", "w_o", "w1", "w2")
COLUMN_QUARTERS = ("w_qkv", "w1")


def _whole(key, gathered):
    q, r, c = gathered.shape
    if key in COLUMN_QUARTERS:
        return jnp.transpose(gathered, (1, 0, 2)).reshape(r, q * c)
    return gathered.reshape(q * r, c)


def _quarters(key, g):
    r, c = g.shape
    if key in COLUMN_QUARTERS:
        return jnp.transpose(g.reshape(r, 4, c // 4), (1, 0, 2))
    return g.reshape(4, r // 4, c)


def _local_step(x, target, norms, mixer_params, shards, whole=None):
    s = x.shape[0]
    tabs = _rope_tables(s)
    if whole is None:
        assert MIXER_OF_LAYER[0][0] == 0
        first = _run_cargo(_gather_cargo([shards[0][key] for key in LAYER_MATS]), name="gather_l0")
        mats = {0: {key: _whole(key, g) for key, g in zip(LAYER_MATS, first)}}
        later = _gather_cargo([shards[layer][key] for layer in range(1, DEPTH) for key in LAYER_MATS])
    else:
        mats, later = dict(enumerate(whole)), None
    h = x
    saved = []
    for layer in range(DEPTH):
        kind = layer % N_MIXERS
        w, p = mats[layer], mixer_params[layer]
        hn, qkv = _norm_mm(h, norms["attn"][layer][None], w["w_qkv"], out_dtype=F32 if kind == 0 else BF16,
                           relu2=False, name=f"qkv_proj_l{layer}")
        o, sv, brought = _mixer_fwd(kind, qkv, p, tabs, later if layer == 0 else None, layer)
        for n, g in enumerate(brought):
            mats.setdefault(1 + n // len(LAYER_MATS), {})[LAYER_MATS[n % len(LAYER_MATS)]] = _whole(
                LAYER_MATS[n % len(LAYER_MATS)], g)
        h_mid = _mm_res(o, w["w_o"], h, name=f"o_proj_l{layer}")
        hn2, act = _norm_mm(h_mid, norms["mlp"][layer][None], w["w1"], out_dtype=BF16, relu2=True,
                            name=f"mlp_up_l{layer}")
        h_out = _mm_res(act, w["w2"], h_mid, name=f"mlp_down_l{layer}")
        saved.append(dict(h=h, hn=hn, qkv=qkv, o=o, mix=sv, h_mid=h_mid, hn2=hn2, act=act))
        h = h_out

    dh, loss, d_final = _loss_head(h, norms["final"][None], target)

    own, received, pending = {}, {}, []
    d_attn, d_mlp, small = [None] * DEPTH, [None] * DEPTH, [None] * DEPTH
    for layer in reversed(range(DEPTH)):
        kind = layer % N_MIXERS
        w, p, sv = mats[layer], mixer_params[layer], saved[layer]
        du = _mm_nt(dh, w["w2"], sv["act"], name=f"mlp_down_bwd_l{layer}")
        own[layer, "w2"] = _quarters("w2", _mm_tn(sv["act"], dh, name=f"mlp_w2_grad_l{layer}"))
        own[layer, "w1"] = _quarters("w1", _mm_tn(sv["hn2"], du, name=f"mlp_w1_grad_l{layer}"))
        dh_mid, d_mlp[layer] = _mm_nt_normbwd(du, w["w1"], sv["h_mid"], norms["mlp"][layer][None], dh,
                                              name=f"mlp_up_bwd_l{layer}")
        do = _mm_nt(dh_mid, w["w_o"], None, name=f"o_proj_bwd_l{layer}")
        own[layer, "w_o"] = _quarters("w_o", _mm_tn(sv["o"], dh_mid, name=f"w_o_grad_l{layer}"))
        pending += [(layer, "w2"), (layer, "w1"), (layer, "w_o")]
        cargo = None
        if kind == 0 and whole is None:
            cargo, sent, pending = _scatter_cargo([own[item] for item in pending], None), pending, []
        dqkv, small[layer], brought = _mixer_bwd(kind, do, sv["qkv"], sv["mix"], p, tabs, cargo, layer)
        if cargo is not None:
            received.update(zip(sent, brought))
        own[layer, "w_qkv"] = _quarters("w_qkv", _mm_tn(sv["hn"], dqkv, name=f"w_qkv_grad_l{layer}"))
        pending.append((layer, "w_qkv"))
        dh, d_attn[layer] = _mm_nt_normbwd(dqkv, w["w_qkv"], sv["h"], norms["attn"][layer][None], dh_mid,
                                           name=f"qkv_proj_bwd_l{layer}")
    return loss, dh, own, received, pending, dict(attn=d_attn, mlp=d_mlp, final=d_final, mixer=small)


CHIP_FLIPS = ((1, 0), (0, 1), (1, 1))


class _Cargo(NamedTuple):
    ins: tuple
    out_shape: tuple
    sem_shapes: tuple
    start: Callable
    wait: Callable


class _Carried(NamedTuple):
    body: Callable
    in_specs: list
    out_specs: list
    out_shape: list
    scratch: list
    args: tuple


def _carry(cargo, grid, n_in, n_out, body):
    if cargo is None:
        return _Carried(body, [], [], [], [], ())
    ci, co = len(cargo.ins), len(cargo.out_shape)

    def wrapped(*refs):
        ins, c_ins = refs[:n_in], refs[n_in:n_in + ci]
        outs, c_outs = refs[n_in + ci:n_in + ci + n_out], refs[n_in + ci + n_out:n_in + ci + n_out + co]
        sems = refs[n_in + ci + n_out + co:]
        first = last = None
        for axis, extent in enumerate(grid):
            at = pl.program_id(axis)
            first = (at == 0) if first is None else first & (at == 0)
            last = (at == extent - 1) if last is None else last & (at == extent - 1)

        @pl.when(first)
        def _():
            cargo.start(c_ins, c_outs, sems)

        body(*ins, *outs)

        @pl.when(last)
        def _():
            cargo.wait(c_ins, c_outs, sems)

    return _Carried(wrapped, [ANY] * ci, [ANY] * co, list(cargo.out_shape), list(cargo.sem_shapes), tuple(cargo.ins))


def _run_cargo(cargo, *, name):
    ci, co = len(cargo.ins), len(cargo.out_shape)

    def body(*refs):
        cargo.start(refs[:ci], refs[ci:ci + co], refs[ci + co:])
        cargo.wait(refs[:ci], refs[ci:ci + co], refs[ci + co:])

    return pl.pallas_call(body, name=name, in_specs=[ANY] * ci, out_specs=[ANY] * co, out_shape=list(cargo.out_shape),
                          scratch_shapes=list(cargo.sem_shapes))(*cargo.ins)


def _other_chip(x, y, j):
    fx, fy = CHIP_FLIPS[j]
    return (1 - x if fx else x), (1 - y if fy else y)


def _gather_cargo(shards):
    n = len(shards)

    def copies(ins, outs, sems):
        send_sems, recv_sems, local_sems = sems
        x, y, c = lax.axis_index("x"), lax.axis_index("y"), lax.axis_index("c")
        me = 2 * x + y

        def remote(t, j, arriving):
            px, py = _other_chip(x, y, j)
            return pltpu.make_async_remote_copy(
                src_ref=ins[t], dst_ref=outs[t].at[2 * px + py if arriving else me], send_sem=send_sems.at[t, j],
                recv_sem=recv_sems.at[t, j], device_id=(px, py, c), device_id_type=MESH)

        return remote, lambda t: pltpu.make_async_copy(ins[t], outs[t].at[me], local_sems.at[t])

    def start(ins, outs, sems):
        remote, own = copies(ins, outs, sems)
        for t in range(n):
            own(t).start()
            for j in range(len(CHIP_FLIPS)):
                remote(t, j, False).start()

    def wait(ins, outs, sems):
        remote, own = copies(ins, outs, sems)
        for t in range(n):
            for j in range(len(CHIP_FLIPS)):
                remote(t, j, True).wait_recv()
                remote(t, j, False).wait_send()
            own(t).wait()

    dma = pltpu.SemaphoreType.DMA
    return _Cargo(tuple(shards), tuple(jax.ShapeDtypeStruct((4,) + a.shape, a.dtype) for a in shards),
                  (dma((n, 3)), dma((n, 3)), dma((n,))), start, wait)


def _scatter_cargo(grads, small):
    n = len(grads)

    def copies(ins, outs, sems):
        x, y, c = lax.axis_index("x"), lax.axis_index("y"), lax.axis_index("c")
        me = 4 * x + 2 * y + c

        def remote(t, j):
            px, py = _other_chip(x, y, j)
            return pltpu.make_async_remote_copy(
                src_ref=ins[t].at[2 * px + py], dst_ref=outs[t].at[j], send_sem=sems[0].at[t, j],
                recv_sem=sems[1].at[t, j], device_id=(px, py, c), device_id_type=MESH)

        def small_remote(r, arriving):
            fx, fy, fc = (r + 1) // 4, ((r + 1) // 2) % 2, (r + 1) % 2
            px, py, pc = (1 - x if fx else x), (1 - y if fy else y), (1 - c if fc else c)
            return pltpu.make_async_remote_copy(
                src_ref=ins[n], dst_ref=outs[n].at[4 * px + 2 * py + pc if arriving else me],
                send_sem=sems[2].at[r], recv_sem=sems[3].at[r], device_id=(px, py, pc), device_id_type=MESH)

        return remote, small_remote, lambda: pltpu.make_async_copy(ins[n], outs[n].at[me], sems[4])

    def start(ins, outs, sems):
        remote, small_remote, small_own = copies(ins, outs, sems)
        if small is not None:
            small_own().start()
            for r in range(7):
                small_remote(r, False).start()
        for t in range(n):
            for j in range(len(CHIP_FLIPS)):
                remote(t, j).start()

    def wait(ins, outs, sems):
        remote, small_remote, small_own = copies(ins, outs, sems)
        if small is not None:
            for r in range(7):
                small_remote(r, True).wait_recv()
                small_remote(r, False).wait_send()
            small_own().wait()
        for t in range(n):
            for j in range(len(CHIP_FLIPS)):
                remote(t, j).wait()

    dma = pltpu.SemaphoreType.DMA
    ins = tuple(grads) + (() if small is None else (small,))
    out_shape = tuple(jax.ShapeDtypeStruct((3,) + g.shape[1:], g.dtype) for g in grads)
    sem_shapes = (dma((n, 3)), dma((n, 3)))
    if small is not None:
        out_shape += (jax.ShapeDtypeStruct((8,) + small.shape, small.dtype),)
        sem_shapes += (dma((7,)), dma((7,)), dma(()))
    return _Cargo(ins, out_shape, sem_shapes, start, wait)


def _swap_cores(parts):
    n = len(parts)

    def body(*refs):
        ins, outs = refs[:n], refs[n:2 * n]
        send_sems, recv_sems = refs[2 * n:]
        peer = (lax.axis_index("x"), lax.axis_index("y"), 1 - lax.axis_index("c"))
        copies = [pltpu.make_async_remote_copy(src_ref=ins[t], dst_ref=outs[t], send_sem=send_sems.at[t],
                                               recv_sem=recv_sems.at[t], device_id=peer, device_id_type=MESH)
                  for t in range(n)]
        for cp in copies:
            cp.start()
        for cp in copies:
            cp.wait()

    return pl.pallas_call(
        body, name="swap_cores", in_specs=[ANY] * n, out_specs=[ANY] * n,
        out_shape=[jax.ShapeDtypeStruct(a.shape, a.dtype) for a in parts],
        scratch_shapes=[pltpu.SemaphoreType.DMA((n,)), pltpu.SemaphoreType.DMA((n,))],
    )(*parts)


def _rows_tile(r):
    return 256 if r % 256 == 0 else r


def _sum_quarters(own, recv, *, name):
    r, c = own.shape
    tr = _rows_tile(r)

    def body(own_ref, recv_ref, o_ref):
        acc = own_ref[...].astype(F32)
        for j in range(3):
            acc = acc + recv_ref[j].astype(F32)
        o_ref[...] = acc

    return pl.pallas_call(
        body, name=name, grid=(r // tr,),
        in_specs=[pl.BlockSpec((tr, c), lambda i: (i, 0)), pl.BlockSpec((3, tr, c), lambda i: (0, i, 0))],
        out_specs=pl.BlockSpec((tr, c), lambda i: (i, 0)),
        out_shape=jax.ShapeDtypeStruct((r, c), F32),
        compiler_params=_params(("parallel",)),
    )(own, recv)


def _adamw(w, m, v, parts, *, name):
    r, c = w.shape
    tr = _rows_tile(r)
    c1, c2 = 1.0 - ADAM_B1 ** ADAM_STEP, 1.0 - ADAM_B2 ** ADAM_STEP
    n_parts = len(parts)

    def body(*refs):
        w_ref, m_ref, v_ref = refs[:3]
        g_ref, d_ref, nm_ref, nv_ref = refs[3 + n_parts:]
        terms = []
        for p_ref in refs[3:3 + n_parts]:
            terms += [p_ref[...]] if len(p_ref.shape) == 2 else [p_ref[j] for j in range(p_ref.shape[0])]
        g = terms[0]
        for term in terms[1:]:
            g = g + term
        m_new = ADAM_B1 * m_ref[...] + (1.0 - ADAM_B1) * g
        v_new = ADAM_B2 * v_ref[...] + (1.0 - ADAM_B2) * (g * g)
        step = (m_new / c1) / (jnp.sqrt(v_new / c2) + ADAM_EPS)
        g_ref[...] = g
        d_ref[...] = -ADAM_LR * (step + ADAM_WD * w_ref[...])
        nm_ref[...] = m_new
        nv_ref[...] = v_new

    blk = pl.BlockSpec((tr, c), lambda i: (i, 0))
    part_specs = [blk if p.ndim == 2 else pl.BlockSpec((p.shape[0], tr, c), lambda i: (0, i, 0)) for p in parts]
    return pl.pallas_call(
        body, name=name, grid=(r // tr,), in_specs=[blk, blk, blk] + part_specs,
        out_specs=[blk] * 4, out_shape=[jax.ShapeDtypeStruct((r, c), F32)] * 4,
        compiler_params=_params(("parallel",)),
    )(w, m, v, *parts)


MATS = ("a_w_qkv", "a_w_o", "b_w_qkv", "b_w_o", "c_w_qkv", "c_w_o", "mlp_w1", "mlp_w2")
SMALLS = ("attn_norm", "mlp_norm", "a_q_gain", "a_k_gain", "c_sinks", "final_norm")
WEIGHTS = ("attn_norm", "mlp_norm", "a_w_qkv", "a_q_gain", "a_k_gain", "a_w_o", "b_w_qkv", "b_w_o", "c_w_qkv",
           "c_sinks", "c_w_o", "mlp_w1", "mlp_w2", "final_norm")
MIXER_OF_LAYER = tuple((layer % N_MIXERS, sum(1 for q in range(layer) if q % N_MIXERS == layer % N_MIXERS))
                       for layer in range(DEPTH))
SMALL_ROWS = 8


def _pack_small(values):
    rows, spans, at = [], [], 0
    for v in values:
        flat = v.reshape(-1)
        n = -(-flat.shape[0] // (SMALL_ROWS * LANES)) * SMALL_ROWS
        rows.append(jnp.pad(flat, (0, n * LANES - flat.shape[0])).reshape(n, LANES))
        spans.append((at, n))
        at += n
    return jnp.concatenate(rows, axis=0), spans


def kernel(x, attn_norm, mlp_norm, a_w_qkv, a_q_gain, a_k_gain, a_w_o, b_w_qkv, b_w_o, c_w_qkv, c_sinks, c_w_o, mlp_w1, mlp_w2, final_norm, loss_target, m_attn_norm, m_mlp_norm, m_a_w_qkv, m_a_q_gain, m_a_k_gain, m_a_w_o, m_b_w_qkv, m_b_w_o, m_c_w_qkv, m_c_sinks, m_c_w_o, m_mlp_w1, m_mlp_w2, m_final_norm, v_attn_norm, v_mlp_norm, v_a_w_qkv, v_a_q_gain, v_a_k_gain, v_a_w_o, v_b_w_qkv, v_b_w_o, v_c_w_qkv, v_c_sinks, v_c_w_o, v_mlp_w1, v_mlp_w2, v_final_norm):
    env = dict(locals())
    w = {name: env[name] for name in WEIGHTS}
    mom = {name: (env["m_" + name], env["v_" + name]) for name in WEIGHTS}

    prefix = ("a", "b", "c")
    shards, mixer_params = [], []
    for layer, (kind, j) in enumerate(MIXER_OF_LAYER):
        shards.append(dict(w_qkv=w[prefix[kind] + "_w_qkv"][j].astype(BF16), w_o=w[prefix[kind] + "_w_o"][j].astype(BF16),
                           w1=mlp_w1[layer].astype(BF16), w2=mlp_w2[layer].astype(BF16)))
        if kind == 0:
            mixer_params.append(dict(gq2=jnp.tile(a_q_gain[j], 2)[None], gk2=jnp.tile(a_k_gain[j], 2)[None]))
        elif kind == 1:
            mixer_params.append(dict(slopes=_per_head(_alibi_slopes(len(B_GROUPS) * B_HEADS_PER_GROUP))))
        else:
            mixer_params.append(dict(slopes=_per_head(_alibi_slopes(C_HEADS)), sinks=_per_head(c_sinks[j])))

    norms = dict(attn=attn_norm, mlp=mlp_norm, final=final_norm)
    loss_part, grad_x, own, received, pending, g_small = _local_step(x[0], loss_target[0], norms, mixer_params, shards)
    loss = lax.psum(loss_part[0, 0], ("x", "y", "c"))

    of_kind = lambda kind, key: jnp.stack([g_small["mixer"][layer][key] for layer, (k, _) in enumerate(MIXER_OF_LAYER)
                                           if k == kind])
    small_grads = dict(
        attn_norm=jnp.concatenate(g_small["attn"], axis=0), mlp_norm=jnp.concatenate(g_small["mlp"], axis=0),
        a_q_gain=of_kind(0, "q_gain"), a_k_gain=of_kind(0, "k_gain"), c_sinks=of_kind(2, "sinks"),
        final_norm=g_small["final"][0])
    packed, spans = _pack_small([small_grads[name] for name in SMALLS])
    *last, all_small = _run_cargo(_scatter_cargo([own[item] for item in pending], packed), name="scatter_last")
    received.update(zip(pending, last))

    me_chip = 2 * lax.axis_index("x") + lax.axis_index("y")
    partial = []
    for name in MATS:
        key = name[2:] if name[0] in "abc" else name[4:]
        layers = [layer for layer, (kind, _) in enumerate(MIXER_OF_LAYER)
                  if name.startswith("mlp") or prefix[kind] == name[0]]
        sums = [_sum_quarters(lax.dynamic_index_in_dim(own[layer, key], me_chip, axis=0, keepdims=False),
                              received[layer, key], name=f"sum_{name}_l{layer}") for layer in layers]
        partial.append(jnp.concatenate(sums, axis=0))
    other = _swap_cores(partial)

    out = {}
    for name, mine, theirs in zip(MATS, partial, other):
        shape = w[name].shape
        res = _adamw(*[a.reshape(-1, shape[-1]) for a in (w[name], *mom[name])], [mine, theirs], name=f"adamw_{name}")
        out[name] = [a.reshape(shape) for a in res]
    for name, (at, n) in zip(SMALLS, spans):
        shape = w[name].shape
        packed_in = [_pack_small([a])[0] for a in (w[name], *mom[name])]
        res = _adamw(*packed_in, [all_small[:, at:at + n]], name=f"adamw_{name}")
        out[name] = [a.reshape(-1)[:w[name].size].reshape(shape) for a in res]

    return (loss, grad_x[None], *[out[name][0] for name in WEIGHTS], *[out[name][1] for name in WEIGHTS],
            *[out[name][2] for name in WEIGHTS], *[out[name][3] for name in WEIGHTS])
```

```python
from typing import Callable, NamedTuple

import jax
import jax.numpy as jnp
from jax import lax
from jax.experimental import pallas as pl
from jax.experimental.pallas import tpu as pltpu

F32 = jnp.float32
BF16 = jnp.bfloat16
MESH = pl.DeviceIdType.MESH
ANY = pl.BlockSpec(memory_space=pl.ANY)

D_MODEL = 1024
HEAD_DIM = 64
GRID_W = 64
ROPE_THETA = 10000.0
RMS_EPS = 1e-6
QK_SCALE = HEAD_DIM ** -0.5
LOG2E = 1.4426950408889634
LN2 = 0.6931471805599453
A_HEADS, A_KV = 16, 4
B_GROUPS = ((128, 1), (512, 4), (2048, 16))
B_HEADS_PER_GROUP, B_KV_PER_GROUP = 6, 2
C_HEADS, C_KV, C_WINDOW = 16, 4, 128
DEPTH, N_MIXERS = 4, 3
ADAM_LR, ADAM_B1, ADAM_B2, ADAM_EPS, ADAM_WD, ADAM_STEP = 0.001, 0.9, 0.999, 1e-08, 0.01, 10

WIN_PAD = 128
V7X_VMEM_BUDGET = 48 * 1024 * 1024
LANES = 128
ROW_TILE = 1024


def _params(semantics):
    return pltpu.CompilerParams(dimension_semantics=semantics, vmem_limit_bytes=V7X_VMEM_BUDGET)


def _tile(n, cap):
    if n <= cap:
        return n
    t = (cap // LANES) * LANES
    while n % t:
        t -= LANES
    return t


def _norm_mm(h, gain, w, *, out_dtype, relu2, name):
    m, d = h.shape
    n = w.shape[1]
    tm, tn = min(ROW_TILE, m), _tile(n, 2048)

    def body(h_ref, g_ref, w_ref, hn_ref, y_ref):
        @pl.when(pl.program_id(1) == 0)
        def _():
            x = h_ref[...]
            r = lax.rsqrt(jnp.mean(x * x, axis=-1, keepdims=True) + RMS_EPS)
            hn_ref[...] = (x * r * g_ref[...]).astype(BF16)

        y = jnp.dot(hn_ref[...], w_ref[...], preferred_element_type=F32)
        if relu2:
            y = jnp.maximum(y, 0.0)
            y = y * y
        y_ref[...] = y.astype(y_ref.dtype)

    return pl.pallas_call(
        body, name=name, grid=(m // tm, n // tn),
        in_specs=[pl.BlockSpec((tm, d), lambda i, j: (i, 0)), pl.BlockSpec((1, d), lambda i, j: (0, 0)),
                  pl.BlockSpec((d, tn), lambda i, j: (0, j))],
        out_specs=[pl.BlockSpec((tm, d), lambda i, j: (i, 0)), pl.BlockSpec((tm, tn), lambda i, j: (i, j))],
        out_shape=[jax.ShapeDtypeStruct((m, d), BF16), jax.ShapeDtypeStruct((m, n), out_dtype)],
        compiler_params=_params(("parallel", "arbitrary")),
    )(h, gain, w)


def _mm_res(a, w, h_in, *, name):
    m, k = a.shape
    d = w.shape[1]
    tm, tk = min(ROW_TILE, m), _tile(k, 1152)

    def body(a_ref, w_ref, h_ref, o_ref):
        @pl.when(pl.program_id(1) == 0)
        def _():
            o_ref[...] = h_ref[...]

        o_ref[...] += jnp.dot(a_ref[...], w_ref[...], preferred_element_type=F32)

    return pl.pallas_call(
        body, name=name, grid=(m // tm, k // tk),
        in_specs=[pl.BlockSpec((tm, tk), lambda i, j: (i, j)), pl.BlockSpec((tk, d), lambda i, j: (j, 0)),
                  pl.BlockSpec((tm, d), lambda i, j: (i, 0))],
        out_specs=pl.BlockSpec((tm, d), lambda i, j: (i, 0)),
        out_shape=jax.ShapeDtypeStruct((m, d), F32),
        compiler_params=_params(("parallel", "arbitrary")),
    )(a, w, h_in)


def _mm_nt(a, w, act, *, name):
    m, d = a.shape
    n = w.shape[0]
    tm, tn = min(ROW_TILE, m), _tile(n, 1152)

    def body(*refs):
        a_ref, w_ref = refs[0], refs[1]
        o_ref = refs[-1]
        acc = lax.dot_general(a_ref[...].astype(BF16), w_ref[...], (((1,), (1,)), ((), ())),
                              preferred_element_type=F32)
        if act is not None:
            acc = acc * (2.0 * jnp.sqrt(refs[2][...].astype(F32)))
        o_ref[...] = acc.astype(BF16)

    in_specs = [pl.BlockSpec((tm, d), lambda i, j: (i, 0)), pl.BlockSpec((tn, d), lambda i, j: (j, 0))]
    args = [a, w]
    if act is not None:
        in_specs.append(pl.BlockSpec((tm, tn), lambda i, j: (i, j)))
        args.append(act)
    return pl.pallas_call(
        body, name=name, grid=(m // tm, n // tn), in_specs=in_specs,
        out_specs=pl.BlockSpec((tm, tn), lambda i, j: (i, j)),
        out_shape=jax.ShapeDtypeStruct((m, n), BF16),
        compiler_params=_params(("parallel", "parallel")),
    )(*args)


def _rmsnorm_bwd(dn, x, gain):
    r = lax.rsqrt(jnp.mean(x * x, axis=-1, keepdims=True) + RMS_EPS)
    xh = x * r
    dgain = jnp.sum(dn * xh, axis=0, keepdims=True)
    u = dn * gain
    dx = r * (u - xh * jnp.mean(u * xh, axis=-1, keepdims=True))
    return dx, dgain


def _mm_nt_normbwd(g, w, h, gain, dh_in, *, name):
    m, k = g.shape
    d = w.shape[0]
    tm, tk = min(ROW_TILE, m), _tile(k, 1024)
    sub = min(256, tm)
    nk = k // tk

    def body(g_ref, w_ref, h_ref, gain_ref, dh_ref, o_ref, dg_ref, acc_ref):
        i, j = pl.program_id(0), pl.program_id(1)

        @pl.when((i == 0) & (j == 0))
        def _():
            dg_ref[...] = jnp.zeros_like(dg_ref)

        @pl.when(j == 0)
        def _():
            acc_ref[...] = jnp.zeros_like(acc_ref)

        acc_ref[...] += lax.dot_general(g_ref[...], w_ref[...], (((1,), (1,)), ((), ())),
                                        preferred_element_type=F32)

        @pl.when(j == nk - 1)
        def _():
            for r in range(0, tm, sub):
                rows = slice(r, r + sub)
                dx, dgain = _rmsnorm_bwd(acc_ref[rows, :], h_ref[rows, :], gain_ref[...])
                dg_ref[...] += dgain
                o_ref[rows, :] = dh_ref[rows, :] + dx

    return pl.pallas_call(
        body, name=name, grid=(m // tm, nk),
        in_specs=[pl.BlockSpec((tm, tk), lambda i, j: (i, j)), pl.BlockSpec((d, tk), lambda i, j: (0, j)),
                  pl.BlockSpec((tm, d), lambda i, j: (i, 0)), pl.BlockSpec((1, d), lambda i, j: (0, 0)),
                  pl.BlockSpec((tm, d), lambda i, j: (i, 0))],
        out_specs=[pl.BlockSpec((tm, d), lambda i, j: (i, 0)), pl.BlockSpec((1, d), lambda i, j: (0, 0))],
        out_shape=[jax.ShapeDtypeStruct((m, d), F32), jax.ShapeDtypeStruct((1, d), F32)],
        scratch_shapes=[pltpu.VMEM((tm, d), F32)],
        compiler_params=_params(("arbitrary", "arbitrary")),
    )(g, w, h, gain, dh_in)


def _mm_tn(x, g, *, name):
    m, k = x.shape
    n = g.shape[1]
    tm, tk, tn = min(ROW_TILE, m), _tile(k, 1152), _tile(n, 1024)
    nm = m // tm

    def body(x_ref, g_ref, o_ref, acc_ref):
        s = pl.program_id(2)

        @pl.when(s == 0)
        def _():
            acc_ref[...] = jnp.zeros_like(acc_ref)

        acc_ref[...] += lax.dot_general(x_ref[...], g_ref[...].astype(BF16), (((0,), (0,)), ((), ())),
                                        preferred_element_type=F32)

        @pl.when(s == nm - 1)
        def _():
            o_ref[...] = acc_ref[...].astype(BF16)

    return pl.pallas_call(
        body, name=name, grid=(k // tk, n // tn, nm),
        in_specs=[pl.BlockSpec((tm, tk), lambda a, b, s: (s, a)), pl.BlockSpec((tm, tn), lambda a, b, s: (s, b))],
        out_specs=pl.BlockSpec((tk, tn), lambda a, b, s: (a, b)),
        out_shape=jax.ShapeDtypeStruct((k, n), BF16),
        scratch_shapes=[pltpu.VMEM((tk, tn), F32)],
        compiler_params=_params(("parallel", "parallel", "arbitrary")),
    )(x, g)


def _loss_head(h, gain, target):
    m, d = h.shape
    tm = 512

    def body(h_ref, g_ref, t_ref, dh_ref, loss_ref, dg_ref):
        @pl.when(pl.program_id(0) == 0)
        def _():
            loss_ref[...] = jnp.zeros_like(loss_ref)
            dg_ref[...] = jnp.zeros_like(dg_ref)

        x = h_ref[...]
        gain_v = g_ref[...]
        r = lax.rsqrt(jnp.mean(x * x, axis=-1, keepdims=True) + RMS_EPS)
        err = x * r * gain_v - t_ref[...]
        loss_ref[...] += 0.5 * jnp.sum(jnp.mean(err * err, axis=-1, keepdims=True), axis=0, keepdims=True)
        dx, dgain = _rmsnorm_bwd(err * (1.0 / d), x, gain_v)
        dg_ref[...] += dgain
        dh_ref[...] = dx

    return pl.pallas_call(
        body, name="loss_head", grid=(m // tm,),
        in_specs=[pl.BlockSpec((tm, d), lambda i: (i, 0)), pl.BlockSpec((1, d), lambda i: (0, 0)),
                  pl.BlockSpec((tm, d), lambda i: (i, 0))],
        out_specs=[pl.BlockSpec((tm, d), lambda i: (i, 0)), pl.BlockSpec((1, LANES), lambda i: (0, 0)),
                   pl.BlockSpec((1, d), lambda i: (0, 0))],
        out_shape=[jax.ShapeDtypeStruct((m, d), F32), jax.ShapeDtypeStruct((1, LANES), F32),
                   jax.ShapeDtypeStruct((1, d), F32)],
        compiler_params=_params(("arbitrary",)),
    )(h, gain, target)


def _rope_tables(s):
    t = jnp.arange(s)
    row = (t // GRID_W).astype(F32)
    col = (t % GRID_W).astype(F32)
    axis_dim = HEAD_DIM // 2
    inv_freq = ROPE_THETA ** (-jnp.arange(0, axis_dim, 2, dtype=F32) / axis_dim)
    ar, ac = row[:, None] * inv_freq, col[:, None] * inv_freq
    cos = jnp.concatenate([jnp.cos(ar), jnp.cos(ar), jnp.cos(ac), jnp.cos(ac)], axis=-1)
    sin = jnp.concatenate([-jnp.sin(ar), jnp.sin(ar), -jnp.sin(ac), jnp.sin(ac)], axis=-1)
    return jnp.tile(cos, (1, 2)), jnp.tile(sin, (1, 2))


def _swap16(x):
    lane = lax.broadcasted_iota(jnp.int32, x.shape, 1)
    return jnp.where((lane % 32) < 16, pltpu.roll(x, LANES - 16, 1), pltpu.roll(x, 16, 1))


def _head_mean(v):
    lane = lax.broadcasted_iota(jnp.int32, v.shape, 1)
    lo = lane < HEAD_DIM
    s_all = jnp.sum(v, axis=-1, keepdims=True)
    s_lo = jnp.sum(jnp.where(lo, v, 0.0), axis=-1, keepdims=True)
    return jnp.where(lo, s_lo, s_all - s_lo) * (1.0 / HEAD_DIM)


def _norm_rope(x, gain2, cos, sin):
    r = lax.rsqrt(_head_mean(x * x) + RMS_EPS)
    nrm = x * r * gain2
    return nrm * cos + _swap16(nrm) * sin


def _norm_rope_bwd(dy, x, gain2, cos, sin):
    dn = dy * cos + _swap16(dy * sin)
    r = lax.rsqrt(_head_mean(x * x) + RMS_EPS)
    xh = x * r
    dgain = jnp.sum(dn * xh, axis=0, keepdims=True)
    u = dn * gain2
    return r * (u - xh * _head_mean(u * xh)), dgain


def _a_prep(qkv, cos, sin, gq2, gk2):
    s = qkv.shape[0]
    tr = 256
    nq, nk = A_HEADS * HEAD_DIM, A_KV * HEAD_DIM

    def body(qkv_ref, cos_ref, sin_ref, gq_ref, gk_ref, qt_ref, k_ref, v_ref):
        cos_v, sin_v = cos_ref[...], sin_ref[...]
        for c in range(nq // LANES):
            y = _norm_rope(qkv_ref[:, c * LANES:(c + 1) * LANES], gq_ref[...], cos_v, sin_v) * (QK_SCALE * LOG2E)
            yt = y.T
            qt_ref[2 * c] = yt[:HEAD_DIM].astype(BF16)
            qt_ref[2 * c + 1] = yt[HEAD_DIM:].astype(BF16)
        for c in range(nk // LANES):
            y = _norm_rope(qkv_ref[:, nq + c * LANES:nq + (c + 1) * LANES], gk_ref[...], cos_v, sin_v)
            k_ref[2 * c] = y[:, :HEAD_DIM].astype(BF16)
            k_ref[2 * c + 1] = y[:, HEAD_DIM:].astype(BF16)
            x = qkv_ref[:, nq + nk + c * LANES:nq + nk + (c + 1) * LANES]
            v_ref[2 * c] = x[:, :HEAD_DIM].astype(BF16)
            v_ref[2 * c + 1] = x[:, HEAD_DIM:].astype(BF16)

    return pl.pallas_call(
        body, name="a_prep", grid=(s // tr,),
        in_specs=[pl.BlockSpec((tr, nq + 2 * nk), lambda i: (i, 0)), pl.BlockSpec((tr, LANES), lambda i: (i, 0)),
                  pl.BlockSpec((tr, LANES), lambda i: (i, 0)), pl.BlockSpec((1, LANES), lambda i: (0, 0)),
                  pl.BlockSpec((1, LANES), lambda i: (0, 0))],
        out_specs=[pl.BlockSpec((A_HEADS, HEAD_DIM, tr), lambda i: (0, 0, i)),
                   pl.BlockSpec((A_KV, tr, HEAD_DIM), lambda i: (0, i, 0)),
                   pl.BlockSpec((A_KV, tr, HEAD_DIM), lambda i: (0, i, 0))],
        out_shape=[jax.ShapeDtypeStruct((A_HEADS, HEAD_DIM, s), BF16), jax.ShapeDtypeStruct((A_KV, s, HEAD_DIM), BF16),
                   jax.ShapeDtypeStruct((A_KV, s, HEAD_DIM), BF16)],
        compiler_params=_params(("parallel",)),
    )(qkv, cos, sin, gq2, gk2)


def _a_prep_bwd(dqt, dkt, dvt, qkv, cos, sin, gq2, gk2):
    s = qkv.shape[0]
    tr = 256
    nq, nk = A_HEADS * HEAD_DIM, A_KV * HEAD_DIM

    def body(dqt_ref, dkt_ref, dvt_ref, qkv_ref, cos_ref, sin_ref, gq_ref, gk_ref, o_ref, dgq_ref, dgk_ref):
        @pl.when(pl.program_id(0) == 0)
        def _():
            dgq_ref[...] = jnp.zeros_like(dgq_ref)
            dgk_ref[...] = jnp.zeros_like(dgk_ref)

        cos_v, sin_v = cos_ref[...], sin_ref[...]

        def pair(ref, c):
            return jnp.concatenate([ref[2 * c], ref[2 * c + 1]], axis=0).T

        for c in range(nq // LANES):
            dx, dg = _norm_rope_bwd(pair(dqt_ref, c) * QK_SCALE, qkv_ref[:, c * LANES:(c + 1) * LANES],
                                    gq_ref[...], cos_v, sin_v)
            o_ref[:, c * LANES:(c + 1) * LANES] = dx.astype(BF16)
            dgq_ref[...] += dg
        for c in range(nk // LANES):
            lo = nq + c * LANES
            dx, dg = _norm_rope_bwd(pair(dkt_ref, c) * LN2, qkv_ref[:, lo:lo + LANES], gk_ref[...], cos_v, sin_v)
            o_ref[:, lo:lo + LANES] = dx.astype(BF16)
            dgk_ref[...] += dg
            o_ref[:, lo + nk:lo + nk + LANES] = pair(dvt_ref, c).astype(BF16)

    return pl.pallas_call(
        body, name="a_prep_bwd", grid=(s // tr,),
        in_specs=[pl.BlockSpec((A_HEADS, HEAD_DIM, tr), lambda i: (0, 0, i)),
                  pl.BlockSpec((A_KV, HEAD_DIM, tr), lambda i: (0, 0, i)),
                  pl.BlockSpec((A_KV, HEAD_DIM, tr), lambda i: (0, 0, i)),
                  pl.BlockSpec((tr, nq + 2 * nk), lambda i: (i, 0)), pl.BlockSpec((tr, LANES), lambda i: (i, 0)),
                  pl.BlockSpec((tr, LANES), lambda i: (i, 0)), pl.BlockSpec((1, LANES), lambda i: (0, 0)),
                  pl.BlockSpec((1, LANES), lambda i: (0, 0))],
        out_specs=[pl.BlockSpec((tr, nq + 2 * nk), lambda i: (i, 0)), pl.BlockSpec((1, LANES), lambda i: (0, 0)),
                   pl.BlockSpec((1, LANES), lambda i: (0, 0))],
        out_shape=[jax.ShapeDtypeStruct((s, nq + 2 * nk), BF16), jax.ShapeDtypeStruct((1, LANES), F32),
                   jax.ShapeDtypeStruct((1, LANES), F32)],
        compiler_params=_params(("arbitrary",)),
    )(dqt, dkt, dvt, qkv, cos, sin, gq2, gk2)


A_TQ = 1024
A_TQ_SUB = 256
A_TQ_BWD = 512
A_KEY_CHUNK = 512


def _a_attn_fwd(qt, k, v, cargo, *, name):
    nh, _, s = qt.shape
    rep = nh // k.shape[0]
    tq = min(A_TQ, s)
    sub = min(A_TQ_SUB, tq)
    grid = (nh, s // tq)

    def body(qt_ref, k_ref, v_ref, o_ref, lse_ref):
        scores = [jnp.dot(k_ref[0], qt_ref[0, :, a:a + sub], preferred_element_type=F32)
                  for a in range(0, tq, sub)]
        for a, st in zip(range(0, tq, sub), scores):
            mx = jnp.max(st, axis=0, keepdims=True)
            p = jnp.exp2(st - mx)
            den = jnp.sum(p, axis=0, keepdims=True)
            ot = lax.dot_general(v_ref[0], p.astype(BF16), (((0,), (0,)), ((), ())), preferred_element_type=F32)
            o_ref[0, :, a:a + sub] = (ot / den).astype(BF16)
            lse_ref[0, :, a:a + sub] = mx + jnp.log(den) * LOG2E

    carried = _carry(cargo, grid, 3, 2, body)
    res = pl.pallas_call(
        carried.body, name=name, grid=grid,
        in_specs=[pl.BlockSpec((1, HEAD_DIM, tq), lambda h, i: (h, 0, i)),
                  pl.BlockSpec((1, s, HEAD_DIM), lambda h, i: (h // rep, 0, 0)),
                  pl.BlockSpec((1, s, HEAD_DIM), lambda h, i: (h // rep, 0, 0))] + carried.in_specs,
        out_specs=[pl.BlockSpec((1, HEAD_DIM, tq), lambda h, i: (h, 0, i)),
                   pl.BlockSpec((1, 1, tq), lambda h, i: (h, 0, i))] + carried.out_specs,
        out_shape=[jax.ShapeDtypeStruct((nh, HEAD_DIM, s), BF16), jax.ShapeDtypeStruct((nh, 1, s), F32)]
        + carried.out_shape,
        scratch_shapes=carried.scratch,
        compiler_params=_params(("arbitrary", "arbitrary")),
    )(qt, k, v, *carried.args)
    return res[0], res[1], res[2:]


def _a_attn_bwd(qt, k, v, dot, ot, lse, cargo, *, name):
    nh, _, s = qt.shape
    nkv = k.shape[0]
    rep = nh // nkv
    tq, ck = min(A_TQ_BWD, s), min(A_KEY_CHUNK, s)
    grid = (nh, s // tq)

    def body(qt_ref, k_ref, v_ref, dot_ref, ot_ref, lse_ref, dq_ref, dk_ref, dv_ref):
        h, i = pl.program_id(0), pl.program_id(1)

        @pl.when((h % rep == 0) & (i == 0))
        def _():
            dk_ref[...] = jnp.zeros_like(dk_ref)
            dv_ref[...] = jnp.zeros_like(dv_ref)

        q_t, do_t, lse_v = qt_ref[0], dot_ref[0], lse_ref[0]
        delta = jnp.sum(do_t.astype(F32) * ot_ref[0].astype(F32), axis=0, keepdims=True)
        nt = (((1,), (1,)), ((), ()))
        dq = jnp.zeros((HEAD_DIM, tq), F32)
        for c in range(s // ck):
            keys = slice(c * ck, (c + 1) * ck)
            kc = k_ref[0, keys, :]
            p = jnp.exp2(jnp.dot(kc, q_t, preferred_element_type=F32) - lse_v)
            dp = jnp.dot(v_ref[0, keys, :], do_t, preferred_element_type=F32)
            ds = (p * (dp - delta)).astype(BF16)
            dv_ref[0, :, keys] += lax.dot_general(do_t, p.astype(BF16), nt, preferred_element_type=F32)
            dk_ref[0, :, keys] += lax.dot_general(q_t, ds, nt, preferred_element_type=F32)
            dq = dq + lax.dot_general(kc, ds, (((0,), (0,)), ((), ())), preferred_element_type=F32)
        dq_ref[0] = dq

    blk_q = pl.BlockSpec((1, HEAD_DIM, tq), lambda h, i: (h, 0, i))
    blk_row = pl.BlockSpec((1, 1, tq), lambda h, i: (h, 0, i))
    blk_kv = pl.BlockSpec((1, s, HEAD_DIM), lambda h, i: (h // rep, 0, 0))
    blk_acc = pl.BlockSpec((1, HEAD_DIM, s), lambda h, i: (h // rep, 0, 0))
    carried = _carry(cargo, grid, 6, 3, body)
    res = pl.pallas_call(
        carried.body, name=name, grid=grid,
        in_specs=[blk_q, blk_kv, blk_kv, blk_q, blk_q, blk_row] + carried.in_specs,
        out_specs=[blk_q, blk_acc, blk_acc] + carried.out_specs,
        out_shape=[jax.ShapeDtypeStruct((nh, HEAD_DIM, s), F32), jax.ShapeDtypeStruct((nkv, HEAD_DIM, s), F32),
                   jax.ShapeDtypeStruct((nkv, HEAD_DIM, s), F32)] + carried.out_shape,
        scratch_shapes=carried.scratch,
        compiler_params=_params(("arbitrary", "arbitrary")),
    )(qt, k, v, dot, ot, lse, *carried.args)
    return res[0], res[1], res[2], res[3:]


WIN_FAR = 1e30


def _win_penalty(i, tq, tk, window, dil, seg):
    qpos = i * tq + lax.broadcasted_iota(jnp.int32, (tk, tq), 1)
    kpos = i * tq - WIN_PAD + lax.broadcasted_iota(jnp.int32, (tk, tq), 0)
    dist = jnp.abs(kpos - qpos)
    seg_lo = qpos - (qpos & (seg - 1))
    valid = (dist <= window) & (kpos >= seg_lo) & (kpos < seg_lo + seg)
    return jnp.where(valid, (dist * dil).astype(F32), WIN_FAR)


def _win_scores(kw, q_t, slope, pen):
    return jnp.dot(kw, q_t, preferred_element_type=F32) * (QK_SCALE * LOG2E) - (slope * LOG2E) * pen


def _win_tq(s):
    return min(512, s)


def _win_fwd(qt, kp, vp, slopes, sinks, *, window, dil, seg, out_dtype, name):
    nh, _, s = qt.shape
    rep = nh // kp.shape[0]
    tq = _win_tq(s)
    tk = tq + 2 * WIN_PAD
    sp = s + 2 * WIN_PAD

    def body(*refs):
        qt_ref, k_ref, v_ref, sl_ref = refs[:4]
        o_ref, lse_ref, pen_ref = refs[-3:]
        i, h = pl.program_id(0), pl.program_id(1)

        @pl.when(h == 0)
        def _():
            pen_ref[...] = _win_penalty(i, tq, tk, window, dil, seg)

        base = pl.multiple_of(i * tq, tq)
        st = _win_scores(k_ref[0, pl.ds(base, tk), :], qt_ref[0], sl_ref[0][:, :1], pen_ref[...])
        mx = jnp.max(st, axis=0, keepdims=True)
        if sinks is not None:
            sink = refs[4][0][:, :1] * LOG2E
            mx = jnp.maximum(mx, sink)
        p = jnp.exp2(st - mx)
        den = jnp.sum(p, axis=0, keepdims=True)
        if sinks is not None:
            den = den + jnp.exp2(sink - mx)
        ot = lax.dot_general(v_ref[0, pl.ds(base, tk), :], p.astype(BF16), (((0,), (0,)), ((), ())),
                             preferred_element_type=F32)
        o_ref[0] = (ot / den).astype(o_ref.dtype)
        lse_ref[0] = mx * LN2 + jnp.log(den)

    blk_q = pl.BlockSpec((1, HEAD_DIM, tq), lambda i, h: (h, 0, i))
    blk_kv = pl.BlockSpec((1, sp, HEAD_DIM), lambda i, h: (h // rep, 0, 0))
    blk_h = pl.BlockSpec((1, 1, LANES), lambda i, h: (h, 0, 0))
    in_specs, args = [blk_q, blk_kv, blk_kv, blk_h], [qt, kp, vp, slopes]
    if sinks is not None:
        in_specs.append(blk_h)
        args.append(sinks)
    return pl.pallas_call(
        body, name=name, grid=(s // tq, nh), in_specs=in_specs,
        out_specs=[blk_q, pl.BlockSpec((1, 1, tq), lambda i, h: (h, 0, i))],
        out_shape=[jax.ShapeDtypeStruct((nh, HEAD_DIM, s), out_dtype), jax.ShapeDtypeStruct((nh, 1, s), F32)],
        scratch_shapes=[pltpu.VMEM((tk, tq), F32)],
        compiler_params=_params(("arbitrary", "arbitrary")),
    )(*args)


def _win_bwd(qt, kp, vp, slopes, sinks, dot, ot, delta, *, window, dil, seg, name):
    nh, _, s = qt.shape
    nkv = kp.shape[0]
    rep = nh // nkv
    tq = _win_tq(s)
    tk = tq + 2 * WIN_PAD
    sp = s + 2 * WIN_PAD
    n_in = 6 + (sinks is not None)

    def body(*refs):
        qt_ref, k_ref, v_ref, sl_ref, dot_ref, aux_ref = refs[:6]
        outs, pen_ref = refs[n_in:-1], refs[-1]
        dq_ref, dk_ref, dv_ref = outs[:3]
        i, h = pl.program_id(0), pl.program_id(1)

        @pl.when((i == 0) & (h == 0))
        def _():
            dk_ref[...] = jnp.zeros_like(dk_ref)
            dv_ref[...] = jnp.zeros_like(dv_ref)
            if sinks is not None:
                outs[3][...] = jnp.zeros_like(outs[3])

        @pl.when(h == 0)
        def _():
            pen_ref[...] = _win_penalty(i, tq, tk, window, dil, seg)

        base = pl.multiple_of(i * tq, tq)
        win = pl.ds(base, tk)
        kv = h // rep
        kw, q_t, do_t = k_ref[0, win, :], qt_ref[0], dot_ref[0]
        st = _win_scores(kw, q_t, sl_ref[0][:, :1], pen_ref[...])
        mx = jnp.max(st, axis=0, keepdims=True)
        if sinks is not None:
            sink = refs[6][0][:, :1] * LOG2E
            mx = jnp.maximum(mx, sink)
        p = jnp.exp2(st - mx)
        den = jnp.sum(p, axis=0, keepdims=True)
        if sinks is not None:
            p_sink = jnp.exp2(sink - mx)
            den = den + p_sink
        p = p / den
        dp = jnp.dot(v_ref[0, win, :], do_t, preferred_element_type=F32)
        if delta is None:
            row = jnp.sum(do_t.astype(F32) * aux_ref[0].astype(F32), axis=0, keepdims=True)
        else:
            row = aux_ref[0]
        ds = (p * (dp - row) * QK_SCALE).astype(BF16)
        nt = (((1,), (1,)), ((), ()))
        dv_ref[kv, :, win] += lax.dot_general(do_t, p.astype(BF16), nt, preferred_element_type=F32)
        dk_ref[kv, :, win] += lax.dot_general(q_t, ds, nt, preferred_element_type=F32)
        dq_ref[0] = lax.dot_general(kw, ds, (((0,), (0,)), ((), ())), preferred_element_type=F32)
        if sinks is not None:
            outs[3][h] += jnp.zeros((1, LANES), F32) - jnp.sum(p_sink / den * row, axis=1, keepdims=True)

    blk_q = pl.BlockSpec((1, HEAD_DIM, tq), lambda i, h: (h, 0, i))
    blk_row = pl.BlockSpec((1, 1, tq), lambda i, h: (h, 0, i))
    blk_kv = pl.BlockSpec((1, sp, HEAD_DIM), lambda i, h: (h // rep, 0, 0))
    blk_acc = pl.BlockSpec((nkv, HEAD_DIM, sp), lambda i, h: (0, 0, 0))
    blk_h = pl.BlockSpec((1, 1, LANES), lambda i, h: (h, 0, 0))
    in_specs = [blk_q, blk_kv, blk_kv, blk_h, blk_q, blk_q if delta is None else blk_row]
    args = [qt, kp, vp, slopes, dot, ot if delta is None else delta]
    out_specs = [blk_q, blk_acc, blk_acc]
    out_shape = [jax.ShapeDtypeStruct((nh, HEAD_DIM, s), F32), jax.ShapeDtypeStruct((nkv, HEAD_DIM, sp), F32),
                 jax.ShapeDtypeStruct((nkv, HEAD_DIM, sp), F32)]
    if sinks is not None:
        in_specs.append(blk_h)
        args.append(sinks)
        out_specs.append(pl.BlockSpec((nh, 1, LANES), lambda i, h: (0, 0, 0)))
        out_shape.append(jax.ShapeDtypeStruct((nh, 1, LANES), F32))
    res = pl.pallas_call(
        body, name=name, grid=(s // tq, nh), in_specs=in_specs, out_specs=out_specs, out_shape=out_shape,
        scratch_shapes=[pltpu.VMEM((tk, tq), F32)],
        compiler_params=_params(("arbitrary", "arbitrary")),
    )(*args)
    return res if sinks is not None else (*res, None)


def _group_weights(lse):
    e = jnp.exp(lse - jnp.max(lse, axis=0, keepdims=True))
    return e / jnp.sum(e, axis=0, keepdims=True)


def _b_combine_fwd(ot, lse):
    nh, _, s = ot.shape
    ng, hg, _ = lse.shape
    ts = min(512, s)

    def body(ot_ref, lse_ref, o_ref):
        alpha = _group_weights(lse_ref[...])
        for g in range(ng):
            for j in range(hg):
                o_ref[g * hg + j] = (ot_ref[g * hg + j] * alpha[g, j:j + 1, :]).astype(BF16)

    return pl.pallas_call(
        body, name="b_combine_fwd", grid=(s // ts,),
        in_specs=[pl.BlockSpec((nh, HEAD_DIM, ts), lambda i: (0, 0, i)), pl.BlockSpec((ng, hg, ts), lambda i: (0, 0, i))],
        out_specs=pl.BlockSpec((nh, HEAD_DIM, ts), lambda i: (0, 0, i)),
        out_shape=jax.ShapeDtypeStruct((nh, HEAD_DIM, s), BF16),
        compiler_params=_params(("parallel",)),
    )(ot, lse)


def _b_combine_bwd(dout, ot, lse):
    nh, _, s = ot.shape
    ng, hg, _ = lse.shape
    ts = min(512, s)

    def body(dout_ref, ot_ref, lse_ref, do_ref, delta_ref):
        alpha = _group_weights(lse_ref[...])
        for j in range(hg):
            e = [jnp.sum(dout_ref[g * hg + j].astype(F32) * ot_ref[g * hg + j], axis=0, keepdims=True)
                 for g in range(ng)]
            a = [alpha[g, j:j + 1, :] for g in range(ng)]
            mix = a[0] * e[0]
            for g in range(1, ng):
                mix = mix + a[g] * e[g]
            for g in range(ng):
                do_ref[g * hg + j] = (dout_ref[g * hg + j].astype(F32) * a[g]).astype(BF16)
                delta_ref[g * hg + j] = a[g] * mix

    blk = pl.BlockSpec((nh, HEAD_DIM, ts), lambda i: (0, 0, i))
    return pl.pallas_call(
        body, name="b_combine_bwd", grid=(s // ts,),
        in_specs=[blk, blk, pl.BlockSpec((ng, hg, ts), lambda i: (0, 0, i))],
        out_specs=[blk, pl.BlockSpec((nh, 1, ts), lambda i: (0, 0, i))],
        out_shape=[jax.ShapeDtypeStruct((nh, HEAD_DIM, s), BF16), jax.ShapeDtypeStruct((nh, 1, s), F32)],
        compiler_params=_params(("parallel",)),
    )(dout, ot, lse)


def _alibi_slopes(n):
    return 2.0 ** (-8.0 * jnp.arange(1, n + 1, dtype=F32) / n)


def _per_head(v):
    return jnp.broadcast_to(v.astype(F32)[:, None, None], (v.shape[0], 1, LANES))


def _dilate(x, dil):
    if dil == 1:
        return x
    s = x.shape[-1]
    return jnp.swapaxes(x.reshape(x.shape[:-1] + (s // dil, dil)), -1, -2).reshape(x.shape)


def _undilate(x, dil):
    if dil == 1:
        return x
    s = x.shape[-1]
    return jnp.swapaxes(x.reshape(x.shape[:-1] + (dil, s // dil)), -1, -2).reshape(x.shape)


def _heads_t(x, nh):
    return jnp.transpose(x.reshape(x.shape[0], nh, HEAD_DIM), (1, 2, 0))


def _tokens(xt):
    return jnp.transpose(xt, (2, 0, 1)).reshape(xt.shape[2], -1)


def _pad_tokens(xt):
    return jnp.pad(jnp.swapaxes(xt, 1, 2), ((0, 0), (WIN_PAD, WIN_PAD), (0, 0)))


def _mixer_fwd(kind, qkv, p, tabs, cargo, layer):
    s = qkv.shape[0]
    if kind == 0:
        qt, k, v = _a_prep(qkv, tabs[0], tabs[1], p["gq2"], p["gk2"])
        ot, lse, brought = _a_attn_fwd(qt, k, v, cargo, name=f"a_attn_fwd_l{layer}")
        return _tokens(ot), dict(qt=qt, k=k, v=v, ot=ot, lse=lse), brought
    assert cargo is None
    if kind == 2:
        nq, nk = C_HEADS * HEAD_DIM, C_KV * HEAD_DIM
        qt = _heads_t(qkv[:, :nq], C_HEADS)
        kp = _pad_tokens(_heads_t(qkv[:, nq:nq + nk], C_KV))
        vp = _pad_tokens(_heads_t(qkv[:, nq + nk:], C_KV))
        ot, _ = _win_fwd(qt, kp, vp, p["slopes"], p["sinks"], window=C_WINDOW, dil=1, seg=s, out_dtype=BF16,
                         name="c_attn_fwd")
        return _tokens(ot), dict(qt=qt, kp=kp, vp=vp, ot=ot), ()
    ng, hg, kg = len(B_GROUPS), B_HEADS_PER_GROUP, B_KV_PER_GROUP
    nq, nk = ng * hg * HEAD_DIM, ng * kg * HEAD_DIM
    qt_all = _heads_t(qkv[:, :nq], ng * hg)
    kt_all = _heads_t(qkv[:, nq:nq + nk], ng * kg)
    vt_all = _heads_t(qkv[:, nq + nk:], ng * kg)
    saved, outs, lses = [], [], []
    for g, (window, dil) in enumerate(B_GROUPS):
        qt = _dilate(qt_all[g * hg:(g + 1) * hg], dil)
        kp = _pad_tokens(_dilate(kt_all[g * kg:(g + 1) * kg], dil))
        vp = _pad_tokens(_dilate(vt_all[g * kg:(g + 1) * kg], dil))
        sl = p["slopes"][g * hg:(g + 1) * hg]
        ot, lse = _win_fwd(qt, kp, vp, sl, None, window=window // 2 // dil, dil=dil, seg=s // dil, out_dtype=F32,
                           name=f"b_attn_fwd_g{g}")
        saved.append(dict(qt=qt, kp=kp, vp=vp))
        outs.append(_undilate(ot, dil))
        lses.append(_undilate(lse[:, 0, :], dil))
    ot_all, lse_all = jnp.concatenate(outs, axis=0), jnp.stack(lses, axis=0)
    mixed = _b_combine_fwd(ot_all, lse_all)
    return _tokens(mixed), dict(groups=saved, ot=ot_all, lse=lse_all), ()


def _mixer_bwd(kind, do, qkv, sv, p, tabs, cargo, layer):
    s = do.shape[0]
    small = {}
    if kind == 0:
        dqt, dkt, dvt, brought = _a_attn_bwd(sv["qt"], sv["k"], sv["v"], _heads_t(do, A_HEADS), sv["ot"], sv["lse"],
                                             cargo, name=f"a_attn_bwd_l{layer}")
        dqkv, dgq, dgk = _a_prep_bwd(dqt, dkt, dvt, qkv, tabs[0], tabs[1], p["gq2"], p["gk2"])
        small["q_gain"] = dgq[0, :HEAD_DIM] + dgq[0, HEAD_DIM:]
        small["k_gain"] = dgk[0, :HEAD_DIM] + dgk[0, HEAD_DIM:]
        return dqkv, small, brought
    assert cargo is None
    if kind == 2:
        dqt, dkt, dvt, dsink = _win_bwd(sv["qt"], sv["kp"], sv["vp"], p["slopes"], p["sinks"], _heads_t(do, C_HEADS),
                                        sv["ot"], None, window=C_WINDOW, dil=1, seg=s, name="c_attn_bwd")
        small["sinks"] = dsink[:, 0, 0]
        parts = [dqt, dkt[:, :, WIN_PAD:-WIN_PAD], dvt[:, :, WIN_PAD:-WIN_PAD]]
        return jnp.concatenate([_tokens(x) for x in parts], axis=1).astype(BF16), small, ()
    ng, hg, kg = len(B_GROUPS), B_HEADS_PER_GROUP, B_KV_PER_GROUP
    do_own, delta = _b_combine_bwd(_heads_t(do, ng * hg), sv["ot"], sv["lse"])
    dqs, dks, dvs = [], [], []
    for g, (window, dil) in enumerate(B_GROUPS):
        gs = sv["groups"][g]
        dqt, dkt, dvt, _ = _win_bwd(gs["qt"], gs["kp"], gs["vp"], p["slopes"][g * hg:(g + 1) * hg], None,
                                    _dilate(do_own[g * hg:(g + 1) * hg], dil), None,
                                    _dilate(delta[g * hg:(g + 1) * hg], dil),
                                    window=window // 2 // dil, dil=dil, seg=s // dil, name=f"b_attn_bwd_g{g}")
        dqs.append(_undilate(dqt, dil))
        dks.append(_undilate(dkt[:, :, WIN_PAD:-WIN_PAD], dil))
        dvs.append(_undilate(dvt[:, :, WIN_PAD:-WIN_PAD], dil))
    parts = [jnp.concatenate(x, axis=0) for x in (dqs, dks, dvs)]
    return jnp.concatenate([_tokens(x) for x in parts], axis=1).astype(BF16), small, ()


LAYER_MATS = ("w_qkv", "w_o", "w1", "w2")
COLUMN_QUARTERS = ("w_qkv", "w1")


def _whole(key, gathered):
    q, r, c = gathered.shape
    if key in COLUMN_QUARTERS:
        return jnp.transpose(gathered, (1, 0, 2)).reshape(r, q * c)
    return gathered.reshape(q * r, c)


def _quarters(key, g):
    r, c = g.shape
    if key in COLUMN_QUARTERS:
        return jnp.transpose(g.reshape(r, 4, c // 4), (1, 0, 2))
    return g.reshape(4, r // 4, c)


def _local_step(x, target, norms, mixer_params, shards, whole=None):
    s = x.shape[0]
    tabs = _rope_tables(s)
    if whole is None:
        assert MIXER_OF_LAYER[0][0] == 0
        first = _run_cargo(_gather_cargo([shards[0][key] for key in LAYER_MATS]), name="gather_l0")
        mats = {0: {key: _whole(key, g) for key, g in zip(LAYER_MATS, first)}}
        later = _gather_cargo([shards[layer][key] for layer in range(1, DEPTH) for key in LAYER_MATS])
    else:
        mats, later = dict(enumerate(whole)), None
    h = x
    saved = []
    for layer in range(DEPTH):
        kind = layer % N_MIXERS
        w, p = mats[layer], mixer_params[layer]
        hn, qkv = _norm_mm(h, norms["attn"][layer][None], w["w_qkv"], out_dtype=F32 if kind == 0 else BF16,
                           relu2=False, name=f"qkv_proj_l{layer}")
        o, sv, brought = _mixer_fwd(kind, qkv, p, tabs, later if layer == 0 else None, layer)
        for n, g in enumerate(brought):
            mats.setdefault(1 + n // len(LAYER_MATS), {})[LAYER_MATS[n % len(LAYER_MATS)]] = _whole(
                LAYER_MATS[n % len(LAYER_MATS)], g)
        h_mid = _mm_res(o, w["w_o"], h, name=f"o_proj_l{layer}")
        hn2, act = _norm_mm(h_mid, norms["mlp"][layer][None], w["w1"], out_dtype=BF16, relu2=True,
                            name=f"mlp_up_l{layer}")
        h_out = _mm_res(act, w["w2"], h_mid, name=f"mlp_down_l{layer}")
        saved.append(dict(h=h, hn=hn, qkv=qkv, o=o, mix=sv, h_mid=h_mid, hn2=hn2, act=act))
        h = h_out

    dh, loss, d_final = _loss_head(h, norms["final"][None], target)

    own, received, pending = {}, {}, []
    d_attn, d_mlp, small = [None] * DEPTH, [None] * DEPTH, [None] * DEPTH
    for layer in reversed(range(DEPTH)):
        kind = layer % N_MIXERS
        w, p, sv = mats[layer], mixer_params[layer], saved[layer]
        du = _mm_nt(dh, w["w2"], sv["act"], name=f"mlp_down_bwd_l{layer}")
        own[layer, "w2"] = _quarters("w2", _mm_tn(sv["act"], dh, name=f"mlp_w2_grad_l{layer}"))
        own[layer, "w1"] = _quarters("w1", _mm_tn(sv["hn2"], du, name=f"mlp_w1_grad_l{layer}"))
        dh_mid, d_mlp[layer] = _mm_nt_normbwd(du, w["w1"], sv["h_mid"], norms["mlp"][layer][None], dh,
                                              name=f"mlp_up_bwd_l{layer}")
        do = _mm_nt(dh_mid, w["w_o"], None, name=f"o_proj_bwd_l{layer}")
        own[layer, "w_o"] = _quarters("w_o", _mm_tn(sv["o"], dh_mid, name=f"w_o_grad_l{layer}"))
        pending += [(layer, "w2"), (layer, "w1"), (layer, "w_o")]
        cargo = None
        if kind == 0 and whole is None:
            cargo, sent, pending = _scatter_cargo([own[item] for item in pending], None), pending, []
        dqkv, small[layer], brought = _mixer_bwd(kind, do, sv["qkv"], sv["mix"], p, tabs, cargo, layer)
        if cargo is not None:
            received.update(zip(sent, brought))
        own[layer, "w_qkv"] = _quarters("w_qkv", _mm_tn(sv["hn"], dqkv, name=f"w_qkv_grad_l{layer}"))
        pending.append((layer, "w_qkv"))
        dh, d_attn[layer] = _mm_nt_normbwd(dqkv, w["w_qkv"], sv["h"], norms["attn"][layer][None], dh_mid,
                                           name=f"qkv_proj_bwd_l{layer}")
    return loss, dh, own, received, pending, dict(attn=d_attn, mlp=d_mlp, final=d_final, mixer=small)


CHIP_FLIPS = ((1, 0), (0, 1), (1, 1))


class _Cargo(NamedTuple):
    ins: tuple
    out_shape: tuple
    sem_shapes: tuple
    start: Callable
    wait: Callable


class _Carried(NamedTuple):
    body: Callable
    in_specs: list
    out_specs: list
    out_shape: list
    scratch: list
    args: tuple


def _carry(cargo, grid, n_in, n_out, body):
    if cargo is None:
        return _Carried(body, [], [], [], [], ())
    ci, co = len(cargo.ins), len(cargo.out_shape)

    def wrapped(*refs):
        ins, c_ins = refs[:n_in], refs[n_in:n_in + ci]
        outs, c_outs = refs[n_in + ci:n_in + ci + n_out], refs[n_in + ci + n_out:n_in + ci + n_out + co]
        sems = refs[n_in + ci + n_out + co:]
        first = last = None
        for axis, extent in enumerate(grid):
            at = pl.program_id(axis)
            first = (at == 0) if first is None else first & (at == 0)
            last = (at == extent - 1) if last is None else last & (at == extent - 1)

        @pl.when(first)
        def _():
            cargo.start(c_ins, c_outs, sems)

        body(*ins, *outs)

        @pl.when(last)
        def _():
            cargo.wait(c_ins, c_outs, sems)

    return _Carried(wrapped, [ANY] * ci, [ANY] * co, list(cargo.out_shape), list(cargo.sem_shapes), tuple(cargo.ins))


def _run_cargo(cargo, *, name):
    ci, co = len(cargo.ins), len(cargo.out_shape)

    def body(*refs):
        cargo.start(refs[:ci], refs[ci:ci + co], refs[ci + co:])
        cargo.wait(refs[:ci], refs[ci:ci + co], refs[ci + co:])

    return pl.pallas_call(body, name=name, in_specs=[ANY] * ci, out_specs=[ANY] * co, out_shape=list(cargo.out_shape),
                          scratch_shapes=list(cargo.sem_shapes))(*cargo.ins)


def _other_chip(x, y, j):
    fx, fy = CHIP_FLIPS[j]
    return (1 - x if fx else x), (1 - y if fy else y)


def _gather_cargo(shards):
    n = len(shards)
    halves = [a.shape[0] // 2 for a in shards]

    def copies(ins, outs, sems):
        ici_send, ici_recv, d2d_send, d2d_recv, local_sems = sems
        x, y, c = lax.axis_index("x"), lax.axis_index("y"), lax.axis_index("c")
        me = 2 * x + y

        def half(t, which):
            return pl.ds(pl.multiple_of(which * halves[t], 16), halves[t])

        def over_ici(t, j, arriving):
            px, py = _other_chip(x, y, j)
            return pltpu.make_async_remote_copy(
                src_ref=ins[t].at[half(t, c)], dst_ref=outs[t].at[2 * px + py if arriving else me, half(t, c)],
                send_sem=ici_send.at[t, j], recv_sem=ici_recv.at[t, j], device_id=(px, py, c), device_id_type=MESH)

        def over_d2d(t, j, arriving):
            px, py = _other_chip(x, y, j)
            mine = outs[t].at[2 * px + py, half(t, c)]
            return pltpu.make_async_remote_copy(
                src_ref=mine, dst_ref=outs[t].at[2 * px + py, half(t, 1 - c)] if arriving else mine,
                send_sem=d2d_send.at[t, j], recv_sem=d2d_recv.at[t, j], device_id=(x, y, 1 - c), device_id_type=MESH)

        return over_ici, over_d2d, lambda t: pltpu.make_async_copy(ins[t], outs[t].at[me], local_sems.at[t])

    def start(ins, outs, sems):
        over_ici, _, own = copies(ins, outs, sems)
        for t in range(n):
            own(t).start()
            for j in range(len(CHIP_FLIPS)):
                over_ici(t, j, False).start()

    def wait(ins, outs, sems):
        over_ici, over_d2d, own = copies(ins, outs, sems)
        for t in range(n):
            for j in range(len(CHIP_FLIPS)):
                over_ici(t, j, True).wait_recv()
                over_d2d(t, j, False).start()
        for t in range(n):
            for j in range(len(CHIP_FLIPS)):
                over_d2d(t, j, True).wait_recv()
                over_d2d(t, j, False).wait_send()
                over_ici(t, j, False).wait_send()
            own(t).wait()

    dma = pltpu.SemaphoreType.DMA
    return _Cargo(tuple(shards), tuple(jax.ShapeDtypeStruct((4,) + a.shape, a.dtype) for a in shards),
                  (dma((n, 3)), dma((n, 3)), dma((n, 3)), dma((n, 3)), dma((n,))), start, wait)


def _scatter_cargo(grads, small):
    n = len(grads)

    def copies(ins, outs, sems):
        x, y, c = lax.axis_index("x"), lax.axis_index("y"), lax.axis_index("c")
        me = 4 * x + 2 * y + c

        def remote(t, j):
            px, py = _other_chip(x, y, j)
            return pltpu.make_async_remote_copy(
                src_ref=ins[t].at[2 * px + py], dst_ref=outs[t].at[j], send_sem=sems[0].at[t, j],
                recv_sem=sems[1].at[t, j], device_id=(px, py, c), device_id_type=MESH)

        def small_remote(r, arriving):
            fx, fy, fc = (r + 1) // 4, ((r + 1) // 2) % 2, (r + 1) % 2
            px, py, pc = (1 - x if fx else x), (1 - y if fy else y), (1 - c if fc else c)
            return pltpu.make_async_remote_copy(
                src_ref=ins[n], dst_ref=outs[n].at[4 * px + 2 * py + pc if arriving else me],
                send_sem=sems[2].at[r], recv_sem=sems[3].at[r], device_id=(px, py, pc), device_id_type=MESH)

        return remote, small_remote, lambda: pltpu.make_async_copy(ins[n], outs[n].at[me], sems[4])

    def start(ins, outs, sems):
        remote, small_remote, small_own = copies(ins, outs, sems)
        if small is not None:
            small_own().start()
            for r in range(7):
                small_remote(r, False).start()
        for t in range(n):
            for j in range(len(CHIP_FLIPS)):
                remote(t, j).start()

    def wait(ins, outs, sems):
        remote, small_remote, small_own = copies(ins, outs, sems)
        if small is not None:
            for r in range(7):
                small_remote(r, True).wait_recv()
                small_remote(r, False).wait_send()
            small_own().wait()
        for t in range(n):
            for j in range(len(CHIP_FLIPS)):
                remote(t, j).wait()

    dma = pltpu.SemaphoreType.DMA
    ins = tuple(grads) + (() if small is None else (small,))
    out_shape = tuple(jax.ShapeDtypeStruct((3,) + g.shape[1:], g.dtype) for g in grads)
    sem_shapes = (dma((n, 3)), dma((n, 3)))
    if small is not None:
        out_shape += (jax.ShapeDtypeStruct((8,) + small.shape, small.dtype),)
        sem_shapes += (dma((7,)), dma((7,)), dma(()))
    return _Cargo(ins, out_shape, sem_shapes, start, wait)


def _swap_cores(parts):
    n = len(parts)

    def body(*refs):
        ins, outs = refs[:n], refs[n:2 * n]
        send_sems, recv_sems = refs[2 * n:]
        peer = (lax.axis_index("x"), lax.axis_index("y"), 1 - lax.axis_index("c"))
        copies = [pltpu.make_async_remote_copy(src_ref=ins[t], dst_ref=outs[t], send_sem=send_sems.at[t],
                                               recv_sem=recv_sems.at[t], device_id=peer, device_id_type=MESH)
                  for t in range(n)]
        for cp in copies:
            cp.start()
        for cp in copies:
            cp.wait()

    return pl.pallas_call(
        body, name="swap_cores", in_specs=[ANY] * n, out_specs=[ANY] * n,
        out_shape=[jax.ShapeDtypeStruct(a.shape, a.dtype) for a in parts],
        scratch_shapes=[pltpu.SemaphoreType.DMA((n,)), pltpu.SemaphoreType.DMA((n,))],
    )(*parts)


def _rows_tile(r):
    return 256 if r % 256 == 0 else r


def _sum_quarters(own, recv, *, name):
    r, c = own.shape
    tr = _rows_tile(r)

    def body(own_ref, recv_ref, o_ref):
        acc = own_ref[...].astype(F32)
        for j in range(3):
            acc = acc + recv_ref[j].astype(F32)
        o_ref[...] = acc

    return pl.pallas_call(
        body, name=name, grid=(r // tr,),
        in_specs=[pl.BlockSpec((tr, c), lambda i: (i, 0)), pl.BlockSpec((3, tr, c), lambda i: (0, i, 0))],
        out_specs=pl.BlockSpec((tr, c), lambda i: (i, 0)),
        out_shape=jax.ShapeDtypeStruct((r, c), F32),
        compiler_params=_params(("parallel",)),
    )(own, recv)


def _adamw(w, m, v, parts, *, name):
    r, c = w.shape
    tr = _rows_tile(r)
    c1, c2 = 1.0 - ADAM_B1 ** ADAM_STEP, 1.0 - ADAM_B2 ** ADAM_STEP
    n_parts = len(parts)

    def body(*refs):
        w_ref, m_ref, v_ref = refs[:3]
        g_ref, d_ref, nm_ref, nv_ref = refs[3 + n_parts:]
        terms = []
        for p_ref in refs[3:3 + n_parts]:
            terms += [p_ref[...]] if len(p_ref.shape) == 2 else [p_ref[j] for j in range(p_ref.shape[0])]
        g = terms[0]
        for term in terms[1:]:
            g = g + term
        m_new = ADAM_B1 * m_ref[...] + (1.0 - ADAM_B1) * g
        v_new = ADAM_B2 * v_ref[...] + (1.0 - ADAM_B2) * (g * g)
        step = (m_new / c1) / (jnp.sqrt(v_new / c2) + ADAM_EPS)
        g_ref[...] = g
        d_ref[...] = -ADAM_LR * (step + ADAM_WD * w_ref[...])
        nm_ref[...] = m_new
        nv_ref[...] = v_new

    blk = pl.BlockSpec((tr, c), lambda i: (i, 0))
    part_specs = [blk if p.ndim == 2 else pl.BlockSpec((p.shape[0], tr, c), lambda i: (0, i, 0)) for p in parts]
    return pl.pallas_call(
        body, name=name, grid=(r // tr,), in_specs=[blk, blk, blk] + part_specs,
        out_specs=[blk] * 4, out_shape=[jax.ShapeDtypeStruct((r, c), F32)] * 4,
        compiler_params=_params(("parallel",)),
    )(w, m, v, *parts)


MATS = ("a_w_qkv", "a_w_o", "b_w_qkv", "b_w_o", "c_w_qkv", "c_w_o", "mlp_w1", "mlp_w2")
SMALLS = ("attn_norm", "mlp_norm", "a_q_gain", "a_k_gain", "c_sinks", "final_norm")
WEIGHTS = ("attn_norm", "mlp_norm", "a_w_qkv", "a_q_gain", "a_k_gain", "a_w_o", "b_w_qkv", "b_w_o", "c_w_qkv",
           "c_sinks", "c_w_o", "mlp_w1", "mlp_w2", "final_norm")
MIXER_OF_LAYER = tuple((layer % N_MIXERS, sum(1 for q in range(layer) if q % N_MIXERS == layer % N_MIXERS))
                       for layer in range(DEPTH))
SMALL_ROWS = 8


def _pack_small(values):
    rows, spans, at = [], [], 0
    for v in values:
        flat = v.reshape(-1)
        n = -(-flat.shape[0] // (SMALL_ROWS * LANES)) * SMALL_ROWS
        rows.append(jnp.pad(flat, (0, n * LANES - flat.shape[0])).reshape(n, LANES))
        spans.append((at, n))
        at += n
    return jnp.concatenate(rows, axis=0), spans


def kernel(x, attn_norm, mlp_norm, a_w_qkv, a_q_gain, a_k_gain, a_w_o, b_w_qkv, b_w_o, c_w_qkv, c_sinks, c_w_o, mlp_w1, mlp_w2, final_norm, loss_target, m_attn_norm, m_mlp_norm, m_a_w_qkv, m_a_q_gain, m_a_k_gain, m_a_w_o, m_b_w_qkv, m_b_w_o, m_c_w_qkv, m_c_sinks, m_c_w_o, m_mlp_w1, m_mlp_w2, m_final_norm, v_attn_norm, v_mlp_norm, v_a_w_qkv, v_a_q_gain, v_a_k_gain, v_a_w_o, v_b_w_qkv, v_b_w_o, v_c_w_qkv, v_c_sinks, v_c_w_o, v_mlp_w1, v_mlp_w2, v_final_norm):
    env = dict(locals())
    w = {name: env[name] for name in WEIGHTS}
    mom = {name: (env["m_" + name], env["v_" + name]) for name in WEIGHTS}

    prefix = ("a", "b", "c")
    shards, mixer_params = [], []
    for layer, (kind, j) in enumerate(MIXER_OF_LAYER):
        shards.append(dict(w_qkv=w[prefix[kind] + "_w_qkv"][j].astype(BF16), w_o=w[prefix[kind] + "_w_o"][j].astype(BF16),
                           w1=mlp_w1[layer].astype(BF16), w2=mlp_w2[layer].astype(BF16)))
        if kind == 0:
            mixer_params.append(dict(gq2=jnp.tile(a_q_gain[j], 2)[None], gk2=jnp.tile(a_k_gain[j], 2)[None]))
        elif kind == 1:
            mixer_params.append(dict(slopes=_per_head(_alibi_slopes(len(B_GROUPS) * B_HEADS_PER_GROUP))))
        else:
            mixer_params.append(dict(slopes=_per_head(_alibi_slopes(C_HEADS)), sinks=_per_head(c_sinks[j])))

    norms = dict(attn=attn_norm, mlp=mlp_norm, final=final_norm)
    loss_part, grad_x, own, received, pending, g_small = _local_step(x[0], loss_target[0], norms, mixer_params, shards)
    loss = lax.psum(loss_part[0, 0], ("x", "y", "c"))

    of_kind = lambda kind, key: jnp.stack([g_small["mixer"][layer][key] for layer, (k, _) in enumerate(MIXER_OF_LAYER)
                                           if k == kind])
    small_grads = dict(
        attn_norm=jnp.concatenate(g_small["attn"], axis=0), mlp_norm=jnp.concatenate(g_small["mlp"], axis=0),
        a_q_gain=of_kind(0, "q_gain"), a_k_gain=of_kind(0, "k_gain"), c_sinks=of_kind(2, "sinks"),
        final_norm=g_small["final"][0])
    packed, spans = _pack_small([small_grads[name] for name in SMALLS])
    *last, all_small = _run_cargo(_scatter_cargo([own[item] for item in pending], packed), name="scatter_last")
    received.update(zip(pending, last))

    me_chip = 2 * lax.axis_index("x") + lax.axis_index("y")
    partial = []
    for name in MATS:
        key = name[2:] if name[0] in "abc" else name[4:]
        layers = [layer for layer, (kind, _) in enumerate(MIXER_OF_LAYER)
                  if name.startswith("mlp") or prefix[kind] == name[0]]
        sums = [_sum_quarters(lax.dynamic_index_in_dim(own[layer, key], me_chip, axis=0, keepdims=False),
                              received[layer, key], name=f"sum_{name}_l{layer}") for layer in layers]
        partial.append(jnp.concatenate(sums, axis=0))
    other = _swap_cores(partial)

    out = {}
    for name, mine, theirs in zip(MATS, partial, other):
        shape = w[name].shape
        res = _adamw(*[a.reshape(-1, shape[-1]) for a in (w[name], *mom[name])], [mine, theirs], name=f"adamw_{name}")
        out[name] = [a.reshape(shape) for a in res]
    for name, (at, n) in zip(SMALLS, spans):
        shape = w[name].shape
        packed_in = [_pack_small([a])[0] for a in (w[name], *mom[name])]
        res = _adamw(*packed_in, [all_small[:, at:at + n]], name=f"adamw_{name}")
        out[name] = [a.reshape(-1)[:w[name].size].reshape(shape) for a in res]

    return (loss, grad_x[None], *[out[name][0] for name in WEIGHTS], *[out[name][1] for name in WEIGHTS],
            *[out[name][2] for name in WEIGHTS], *[out[name][3] for name in WEIGHTS])
```

```python
from typing import Callable, NamedTuple

import jax
import jax.numpy as jnp
from jax import lax
from jax.experimental import pallas as pl
from jax.experimental.pallas import tpu as pltpu

F32 = jnp.float32
BF16 = jnp.bfloat16
MESH = pl.DeviceIdType.MESH
ANY = pl.BlockSpec(memory_space=pl.ANY)

D_MODEL = 1024
HEAD_DIM = 64
GRID_W = 64
ROPE_THETA = 10000.0
RMS_EPS = 1e-6
QK_SCALE = HEAD_DIM ** -0.5
LOG2E = 1.4426950408889634
LN2 = 0.6931471805599453
A_HEADS, A_KV = 16, 4
B_GROUPS = ((128, 1), (512, 4), (2048, 16))
B_HEADS_PER_GROUP, B_KV_PER_GROUP = 6, 2
C_HEADS, C_KV, C_WINDOW = 16, 4, 128
DEPTH, N_MIXERS = 4, 3
ADAM_LR, ADAM_B1, ADAM_B2, ADAM_EPS, ADAM_WD, ADAM_STEP = 0.001, 0.9, 0.999, 1e-08, 0.01, 10

WIN_PAD = 128
V7X_VMEM_BUDGET = 48 * 1024 * 1024
LANES = 128
ROW_TILE = 1024


def _params(semantics):
    return pltpu.CompilerParams(dimension_semantics=semantics, vmem_limit_bytes=V7X_VMEM_BUDGET)


def _tile(n, cap):
    if n <= cap:
        return n
    t = (cap // LANES) * LANES
    while n % t:
        t -= LANES
    return t


def _norm_mm(h, gain, w, *, out_dtype, relu2, name):
    m, d = h.shape
    by_quarter = w.ndim == 3
    n = w.shape[-1] * (4 if by_quarter else 1)
    tm, tn = min(ROW_TILE, m), (w.shape[-1] if by_quarter else _tile(n, 2048))
    w_spec = (pl.BlockSpec((None, d, tn), lambda i, j: (j, 0, 0)) if by_quarter
              else pl.BlockSpec((d, tn), lambda i, j: (0, j)))

    def body(h_ref, g_ref, w_ref, hn_ref, y_ref):
        @pl.when(pl.program_id(1) == 0)
        def _():
            x = h_ref[...]
            r = lax.rsqrt(jnp.mean(x * x, axis=-1, keepdims=True) + RMS_EPS)
            hn_ref[...] = (x * r * g_ref[...]).astype(BF16)

        y = jnp.dot(hn_ref[...], w_ref[...], preferred_element_type=F32)
        if relu2:
            y = jnp.maximum(y, 0.0)
            y = y * y
        y_ref[...] = y.astype(y_ref.dtype)

    return pl.pallas_call(
        body, name=name, grid=(m // tm, n // tn),
        in_specs=[pl.BlockSpec((tm, d), lambda i, j: (i, 0)), pl.BlockSpec((1, d), lambda i, j: (0, 0)), w_spec],
        out_specs=[pl.BlockSpec((tm, d), lambda i, j: (i, 0)), pl.BlockSpec((tm, tn), lambda i, j: (i, j))],
        out_shape=[jax.ShapeDtypeStruct((m, d), BF16), jax.ShapeDtypeStruct((m, n), out_dtype)],
        compiler_params=_params(("parallel", "arbitrary")),
    )(h, gain, w)


def _mm_res(a, w, h_in, *, a_transposed, name):
    k, d = w.shape
    m = h_in.shape[0]
    tm, tk = min(ROW_TILE, m), _tile(k, 1152)
    lhs_contracts = 0 if a_transposed else 1

    def body(a_ref, w_ref, h_ref, o_ref):
        @pl.when(pl.program_id(1) == 0)
        def _():
            o_ref[...] = h_ref[...]

        o_ref[...] += lax.dot_general(a_ref[...], w_ref[...], (((lhs_contracts,), (0,)), ((), ())),
                                      preferred_element_type=F32)

    a_spec = (pl.BlockSpec((tk, tm), lambda i, j: (j, i)) if a_transposed
              else pl.BlockSpec((tm, tk), lambda i, j: (i, j)))
    return pl.pallas_call(
        body, name=name, grid=(m // tm, k // tk),
        in_specs=[a_spec, pl.BlockSpec((tk, d), lambda i, j: (j, 0)), pl.BlockSpec((tm, d), lambda i, j: (i, 0))],
        out_specs=pl.BlockSpec((tm, d), lambda i, j: (i, 0)),
        out_shape=jax.ShapeDtypeStruct((m, d), F32),
        compiler_params=_params(("parallel", "arbitrary")),
    )(a, w, h_in)


def _mm_nt(a, w, act, *, transpose_out, name):
    m, d = a.shape
    n = w.shape[0]
    tm, tn = min(ROW_TILE, m), _tile(n, 1152)
    assert act is None or not transpose_out
    nt = (((1,), (1,)), ((), ()))

    def body(*refs):
        a_ref, w_ref = refs[0], refs[1]
        o_ref = refs[-1]
        if transpose_out:
            acc = lax.dot_general(w_ref[...], a_ref[...].astype(BF16), nt, preferred_element_type=F32)
        else:
            acc = lax.dot_general(a_ref[...].astype(BF16), w_ref[...], nt, preferred_element_type=F32)
        if act is not None:
            acc = acc * (2.0 * jnp.sqrt(refs[2][...].astype(F32)))
        o_ref[...] = acc.astype(BF16)

    in_specs = [pl.BlockSpec((tm, d), lambda i, j: (i, 0)), pl.BlockSpec((tn, d), lambda i, j: (j, 0))]
    args = [a, w]
    if act is not None:
        in_specs.append(pl.BlockSpec((tm, tn), lambda i, j: (i, j)))
        args.append(act)
    out_spec = (pl.BlockSpec((tn, tm), lambda i, j: (j, i)) if transpose_out
                else pl.BlockSpec((tm, tn), lambda i, j: (i, j)))
    return pl.pallas_call(
        body, name=name, grid=(m // tm, n // tn), in_specs=in_specs, out_specs=out_spec,
        out_shape=jax.ShapeDtypeStruct((n, m) if transpose_out else (m, n), BF16),
        compiler_params=_params(("parallel", "parallel")),
    )(*args)


def _rmsnorm_bwd(dn, x, gain):
    r = lax.rsqrt(jnp.mean(x * x, axis=-1, keepdims=True) + RMS_EPS)
    xh = x * r
    dgain = jnp.sum(dn * xh, axis=0, keepdims=True)
    u = dn * gain
    dx = r * (u - xh * jnp.mean(u * xh, axis=-1, keepdims=True))
    return dx, dgain


def _mm_nt_normbwd(g, w, h, gain, dh_in, *, name):
    m, k = g.shape
    by_quarter = w.ndim == 3
    d = w.shape[-2]
    tm, tk = min(ROW_TILE, m), (w.shape[-1] if by_quarter else _tile(k, 1024))
    sub = min(256, tm)
    nk = k // tk
    w_spec = (pl.BlockSpec((None, d, tk), lambda i, j: (j, 0, 0)) if by_quarter
              else pl.BlockSpec((d, tk), lambda i, j: (0, j)))

    def body(g_ref, w_ref, h_ref, gain_ref, dh_ref, o_ref, dg_ref, acc_ref):
        i, j = pl.program_id(0), pl.program_id(1)

        @pl.when((i == 0) & (j == 0))
        def _():
            dg_ref[...] = jnp.zeros_like(dg_ref)

        @pl.when(j == 0)
        def _():
            acc_ref[...] = jnp.zeros_like(acc_ref)

        acc_ref[...] += lax.dot_general(g_ref[...], w_ref[...], (((1,), (1,)), ((), ())),
                                        preferred_element_type=F32)

        @pl.when(j == nk - 1)
        def _():
            for r in range(0, tm, sub):
                rows = slice(r, r + sub)
                dx, dgain = _rmsnorm_bwd(acc_ref[rows, :], h_ref[rows, :], gain_ref[...])
                dg_ref[...] += dgain
                o_ref[rows, :] = dh_ref[rows, :] + dx

    return pl.pallas_call(
        body, name=name, grid=(m // tm, nk),
        in_specs=[pl.BlockSpec((tm, tk), lambda i, j: (i, j)), w_spec,
                  pl.BlockSpec((tm, d), lambda i, j: (i, 0)), pl.BlockSpec((1, d), lambda i, j: (0, 0)),
                  pl.BlockSpec((tm, d), lambda i, j: (i, 0))],
        out_specs=[pl.BlockSpec((tm, d), lambda i, j: (i, 0)), pl.BlockSpec((1, d), lambda i, j: (0, 0))],
        out_shape=[jax.ShapeDtypeStruct((m, d), F32), jax.ShapeDtypeStruct((1, d), F32)],
        scratch_shapes=[pltpu.VMEM((tm, d), F32)],
        compiler_params=_params(("arbitrary", "arbitrary")),
    )(g, w, h, gain, dh_in)


def _mm_tn(x, g, *, x_transposed, column_quarters, name):
    k, m = x.shape if x_transposed else x.shape[::-1]
    n = g.shape[1]
    tm, tk, tn = min(ROW_TILE, m), _tile(k, 1152), (n // 4 if column_quarters else _tile(n, 1024))
    nm = m // tm
    lhs_contracts = 1 if x_transposed else 0

    def body(x_ref, g_ref, o_ref, acc_ref):
        s = pl.program_id(2)

        @pl.when(s == 0)
        def _():
            acc_ref[...] = jnp.zeros_like(acc_ref)

        acc_ref[...] += lax.dot_general(x_ref[...], g_ref[...].astype(BF16), (((lhs_contracts,), (0,)), ((), ())),
                                        preferred_element_type=F32)

        @pl.when(s == nm - 1)
        def _():
            o_ref[...] = acc_ref[...].astype(BF16)

    x_spec = (pl.BlockSpec((tk, tm), lambda a, b, s: (a, s)) if x_transposed
              else pl.BlockSpec((tm, tk), lambda a, b, s: (s, a)))
    out_spec = (pl.BlockSpec((None, tk, tn), lambda a, b, s: (b, a, 0)) if column_quarters
                else pl.BlockSpec((tk, tn), lambda a, b, s: (a, b)))
    return pl.pallas_call(
        body, name=name, grid=(k // tk, n // tn, nm),
        in_specs=[x_spec, pl.BlockSpec((tm, tn), lambda a, b, s: (s, b))], out_specs=out_spec,
        out_shape=jax.ShapeDtypeStruct((4, k, n // 4) if column_quarters else (k, n), BF16),
        scratch_shapes=[pltpu.VMEM((tk, tn), F32)],
        compiler_params=_params(("parallel", "parallel", "arbitrary")),
    )(x, g)


def _loss_head(h, gain, target):
    m, d = h.shape
    tm = 512

    def body(h_ref, g_ref, t_ref, dh_ref, loss_ref, dg_ref):
        @pl.when(pl.program_id(0) == 0)
        def _():
            loss_ref[...] = jnp.zeros_like(loss_ref)
            dg_ref[...] = jnp.zeros_like(dg_ref)

        x = h_ref[...]
        gain_v = g_ref[...]
        r = lax.rsqrt(jnp.mean(x * x, axis=-1, keepdims=True) + RMS_EPS)
        err = x * r * gain_v - t_ref[...]
        loss_ref[...] += 0.5 * jnp.sum(jnp.mean(err * err, axis=-1, keepdims=True), axis=0, keepdims=True)
        dx, dgain = _rmsnorm_bwd(err * (1.0 / d), x, gain_v)
        dg_ref[...] += dgain
        dh_ref[...] = dx

    return pl.pallas_call(
        body, name="loss_head", grid=(m // tm,),
        in_specs=[pl.BlockSpec((tm, d), lambda i: (i, 0)), pl.BlockSpec((1, d), lambda i: (0, 0)),
                  pl.BlockSpec((tm, d), lambda i: (i, 0))],
        out_specs=[pl.BlockSpec((tm, d), lambda i: (i, 0)), pl.BlockSpec((1, LANES), lambda i: (0, 0)),
                   pl.BlockSpec((1, d), lambda i: (0, 0))],
        out_shape=[jax.ShapeDtypeStruct((m, d), F32), jax.ShapeDtypeStruct((1, LANES), F32),
                   jax.ShapeDtypeStruct((1, d), F32)],
        compiler_params=_params(("arbitrary",)),
    )(h, gain, target)


def _rope_tables(s):
    t = jnp.arange(s)
    row = (t // GRID_W).astype(F32)
    col = (t % GRID_W).astype(F32)
    axis_dim = HEAD_DIM // 2
    inv_freq = ROPE_THETA ** (-jnp.arange(0, axis_dim, 2, dtype=F32) / axis_dim)
    ar, ac = row[:, None] * inv_freq, col[:, None] * inv_freq
    cos = jnp.concatenate([jnp.cos(ar), jnp.cos(ar), jnp.cos(ac), jnp.cos(ac)], axis=-1)
    sin = jnp.concatenate([-jnp.sin(ar), jnp.sin(ar), -jnp.sin(ac), jnp.sin(ac)], axis=-1)
    return jnp.tile(cos, (1, 2)), jnp.tile(sin, (1, 2))


def _swap16(x):
    lane = lax.broadcasted_iota(jnp.int32, x.shape, 1)
    return jnp.where((lane % 32) < 16, pltpu.roll(x, LANES - 16, 1), pltpu.roll(x, 16, 1))


def _head_mean(v):
    lane = lax.broadcasted_iota(jnp.int32, v.shape, 1)
    lo = lane < HEAD_DIM
    s_all = jnp.sum(v, axis=-1, keepdims=True)
    s_lo = jnp.sum(jnp.where(lo, v, 0.0), axis=-1, keepdims=True)
    return jnp.where(lo, s_lo, s_all - s_lo) * (1.0 / HEAD_DIM)


def _norm_rope(x, gain2, cos, sin):
    r = lax.rsqrt(_head_mean(x * x) + RMS_EPS)
    nrm = x * r * gain2
    return nrm * cos + _swap16(nrm) * sin


def _norm_rope_bwd(dy, x, gain2, cos, sin):
    dn = dy * cos + _swap16(dy * sin)
    r = lax.rsqrt(_head_mean(x * x) + RMS_EPS)
    xh = x * r
    dgain = jnp.sum(dn * xh, axis=0, keepdims=True)
    u = dn * gain2
    return r * (u - xh * _head_mean(u * xh)), dgain


def _a_prep(qkv, cos, sin, gq2, gk2):
    s = qkv.shape[0]
    tr = 256
    nq, nk = A_HEADS * HEAD_DIM, A_KV * HEAD_DIM

    def body(qkv_ref, cos_ref, sin_ref, gq_ref, gk_ref, qt_ref, k_ref, v_ref):
        cos_v, sin_v = cos_ref[...], sin_ref[...]
        for c in range(nq // LANES):
            y = _norm_rope(qkv_ref[:, c * LANES:(c + 1) * LANES], gq_ref[...], cos_v, sin_v) * (QK_SCALE * LOG2E)
            yt = y.T
            qt_ref[2 * c] = yt[:HEAD_DIM].astype(BF16)
            qt_ref[2 * c + 1] = yt[HEAD_DIM:].astype(BF16)
        for c in range(nk // LANES):
            y = _norm_rope(qkv_ref[:, nq + c * LANES:nq + (c + 1) * LANES], gk_ref[...], cos_v, sin_v)
            k_ref[2 * c] = y[:, :HEAD_DIM].astype(BF16)
            k_ref[2 * c + 1] = y[:, HEAD_DIM:].astype(BF16)
            x = qkv_ref[:, nq + nk + c * LANES:nq + nk + (c + 1) * LANES]
            v_ref[2 * c] = x[:, :HEAD_DIM].astype(BF16)
            v_ref[2 * c + 1] = x[:, HEAD_DIM:].astype(BF16)

    return pl.pallas_call(
        body, name="a_prep", grid=(s // tr,),
        in_specs=[pl.BlockSpec((tr, nq + 2 * nk), lambda i: (i, 0)), pl.BlockSpec((tr, LANES), lambda i: (i, 0)),
                  pl.BlockSpec((tr, LANES), lambda i: (i, 0)), pl.BlockSpec((1, LANES), lambda i: (0, 0)),
                  pl.BlockSpec((1, LANES), lambda i: (0, 0))],
        out_specs=[pl.BlockSpec((A_HEADS, HEAD_DIM, tr), lambda i: (0, 0, i)),
                   pl.BlockSpec((A_KV, tr, HEAD_DIM), lambda i: (0, i, 0)),
                   pl.BlockSpec((A_KV, tr, HEAD_DIM), lambda i: (0, i, 0))],
        out_shape=[jax.ShapeDtypeStruct((A_HEADS, HEAD_DIM, s), BF16), jax.ShapeDtypeStruct((A_KV, s, HEAD_DIM), BF16),
                   jax.ShapeDtypeStruct((A_KV, s, HEAD_DIM), BF16)],
        compiler_params=_params(("parallel",)),
    )(qkv, cos, sin, gq2, gk2)


def _a_prep_bwd(dqt, dkt, dvt, qkv, cos, sin, gq2, gk2):
    s = qkv.shape[0]
    tr = 256
    nq, nk = A_HEADS * HEAD_DIM, A_KV * HEAD_DIM

    def body(dqt_ref, dkt_ref, dvt_ref, qkv_ref, cos_ref, sin_ref, gq_ref, gk_ref, o_ref, dgq_ref, dgk_ref):
        @pl.when(pl.program_id(0) == 0)
        def _():
            dgq_ref[...] = jnp.zeros_like(dgq_ref)
            dgk_ref[...] = jnp.zeros_like(dgk_ref)

        cos_v, sin_v = cos_ref[...], sin_ref[...]

        def pair(ref, c):
            return jnp.concatenate([ref[2 * c], ref[2 * c + 1]], axis=0).T

        for c in range(nq // LANES):
            dx, dg = _norm_rope_bwd(pair(dqt_ref, c) * QK_SCALE, qkv_ref[:, c * LANES:(c + 1) * LANES],
                                    gq_ref[...], cos_v, sin_v)
            o_ref[:, c * LANES:(c + 1) * LANES] = dx.astype(BF16)
            dgq_ref[...] += dg
        for c in range(nk // LANES):
            lo = nq + c * LANES
            dx, dg = _norm_rope_bwd(pair(dkt_ref, c) * LN2, qkv_ref[:, lo:lo + LANES], gk_ref[...], cos_v, sin_v)
            o_ref[:, lo:lo + LANES] = dx.astype(BF16)
            dgk_ref[...] += dg
            o_ref[:, lo + nk:lo + nk + LANES] = pair(dvt_ref, c).astype(BF16)

    return pl.pallas_call(
        body, name="a_prep_bwd", grid=(s // tr,),
        in_specs=[pl.BlockSpec((A_HEADS, HEAD_DIM, tr), lambda i: (0, 0, i)),
                  pl.BlockSpec((A_KV, HEAD_DIM, tr), lambda i: (0, 0, i)),
                  pl.BlockSpec((A_KV, HEAD_DIM, tr), lambda i: (0, 0, i)),
                  pl.BlockSpec((tr, nq + 2 * nk), lambda i: (i, 0)), pl.BlockSpec((tr, LANES), lambda i: (i, 0)),
                  pl.BlockSpec((tr, LANES), lambda i: (i, 0)), pl.BlockSpec((1, LANES), lambda i: (0, 0)),
                  pl.BlockSpec((1, LANES), lambda i: (0, 0))],
        out_specs=[pl.BlockSpec((tr, nq + 2 * nk), lambda i: (i, 0)), pl.BlockSpec((1, LANES), lambda i: (0, 0)),
                   pl.BlockSpec((1, LANES), lambda i: (0, 0))],
        out_shape=[jax.ShapeDtypeStruct((s, nq + 2 * nk), BF16), jax.ShapeDtypeStruct((1, LANES), F32),
                   jax.ShapeDtypeStruct((1, LANES), F32)],
        compiler_params=_params(("arbitrary",)),
    )(dqt, dkt, dvt, qkv, cos, sin, gq2, gk2)


A_TQ = 1024
A_TQ_SUB = 256
A_TQ_BWD = 512
A_KEY_CHUNK = 512


def _a_attn_fwd(qt, k, v, cargo, *, name):
    nh, _, s = qt.shape
    rep = nh // k.shape[0]
    tq = min(A_TQ, s)
    sub = min(A_TQ_SUB, tq)
    grid = (nh, s // tq)

    def body(qt_ref, k_ref, v_ref, o_ref, lse_ref):
        scores = [jnp.dot(k_ref[0], qt_ref[0, :, a:a + sub], preferred_element_type=F32)
                  for a in range(0, tq, sub)]
        for a, st in zip(range(0, tq, sub), scores):
            mx = jnp.max(st, axis=0, keepdims=True)
            p = jnp.exp2(st - mx)
            den = jnp.sum(p, axis=0, keepdims=True)
            ot = lax.dot_general(v_ref[0], p.astype(BF16), (((0,), (0,)), ((), ())), preferred_element_type=F32)
            o_ref[0, :, a:a + sub] = (ot / den).astype(BF16)
            lse_ref[0, :, a:a + sub] = mx + jnp.log(den) * LOG2E

    carried = _carry(cargo, grid, 3, 2, body)
    res = pl.pallas_call(
        carried.body, name=name, grid=grid,
        in_specs=[pl.BlockSpec((1, HEAD_DIM, tq), lambda h, i: (h, 0, i)),
                  pl.BlockSpec((1, s, HEAD_DIM), lambda h, i: (h // rep, 0, 0)),
                  pl.BlockSpec((1, s, HEAD_DIM), lambda h, i: (h // rep, 0, 0))] + carried.in_specs,
        out_specs=[pl.BlockSpec((1, HEAD_DIM, tq), lambda h, i: (h, 0, i)),
                   pl.BlockSpec((1, 1, tq), lambda h, i: (h, 0, i))] + carried.out_specs,
        out_shape=[jax.ShapeDtypeStruct((nh, HEAD_DIM, s), BF16), jax.ShapeDtypeStruct((nh, 1, s), F32)]
        + carried.out_shape,
        scratch_shapes=carried.scratch,
        compiler_params=_params(("arbitrary", "arbitrary")),
    )(qt, k, v, *carried.args)
    return res[0], res[1], res[2:]


def _a_attn_bwd(qt, k, v, dot, ot, lse, cargo, *, name):
    nh, _, s = qt.shape
    nkv = k.shape[0]
    rep = nh // nkv
    tq, ck = min(A_TQ_BWD, s), min(A_KEY_CHUNK, s)
    grid = (nh, s // tq)

    def body(qt_ref, k_ref, v_ref, dot_ref, ot_ref, lse_ref, dq_ref, dk_ref, dv_ref):
        h, i = pl.program_id(0), pl.program_id(1)

        @pl.when((h % rep == 0) & (i == 0))
        def _():
            dk_ref[...] = jnp.zeros_like(dk_ref)
            dv_ref[...] = jnp.zeros_like(dv_ref)

        q_t, do_t, lse_v = qt_ref[0], dot_ref[0], lse_ref[0]
        delta = jnp.sum(do_t.astype(F32) * ot_ref[0].astype(F32), axis=0, keepdims=True)
        nt = (((1,), (1,)), ((), ()))
        dq = jnp.zeros((HEAD_DIM, tq), F32)
        for c in range(s // ck):
            keys = slice(c * ck, (c + 1) * ck)
            kc = k_ref[0, keys, :]
            p = jnp.exp2(jnp.dot(kc, q_t, preferred_element_type=F32) - lse_v)
            dp = jnp.dot(v_ref[0, keys, :], do_t, preferred_element_type=F32)
            ds = (p * (dp - delta)).astype(BF16)
            dv_ref[0, :, keys] += lax.dot_general(do_t, p.astype(BF16), nt, preferred_element_type=F32)
            dk_ref[0, :, keys] += lax.dot_general(q_t, ds, nt, preferred_element_type=F32)
            dq = dq + lax.dot_general(kc, ds, (((0,), (0,)), ((), ())), preferred_element_type=F32)
        dq_ref[0] = dq

    blk_q = pl.BlockSpec((1, HEAD_DIM, tq), lambda h, i: (h, 0, i))
    blk_row = pl.BlockSpec((1, 1, tq), lambda h, i: (h, 0, i))
    blk_kv = pl.BlockSpec((1, s, HEAD_DIM), lambda h, i: (h // rep, 0, 0))
    blk_acc = pl.BlockSpec((1, HEAD_DIM, s), lambda h, i: (h // rep, 0, 0))
    carried = _carry(cargo, grid, 6, 3, body)
    res = pl.pallas_call(
        carried.body, name=name, grid=grid,
        in_specs=[blk_q, blk_kv, blk_kv, blk_q, blk_q, blk_row] + carried.in_specs,
        out_specs=[blk_q, blk_acc, blk_acc] + carried.out_specs,
        out_shape=[jax.ShapeDtypeStruct((nh, HEAD_DIM, s), F32), jax.ShapeDtypeStruct((nkv, HEAD_DIM, s), F32),
                   jax.ShapeDtypeStruct((nkv, HEAD_DIM, s), F32)] + carried.out_shape,
        scratch_shapes=carried.scratch,
        compiler_params=_params(("arbitrary", "arbitrary")),
    )(qt, k, v, dot, ot, lse, *carried.args)
    return res[0], res[1], res[2], res[3:]


WIN_FAR = 1e30


def _win_penalty(i, tq, tk, window, dil, seg):
    qpos = i * tq + lax.broadcasted_iota(jnp.int32, (tk, tq), 1)
    kpos = i * tq - WIN_PAD + lax.broadcasted_iota(jnp.int32, (tk, tq), 0)
    dist = jnp.abs(kpos - qpos)
    seg_lo = qpos - (qpos & (seg - 1))
    valid = (dist <= window) & (kpos >= seg_lo) & (kpos < seg_lo + seg)
    return jnp.where(valid, (dist * dil).astype(F32), WIN_FAR)


def _win_scores(kw, q_t, slope, pen):
    return jnp.dot(kw, q_t, preferred_element_type=F32) * (QK_SCALE * LOG2E) - (slope * LOG2E) * pen


def _win_tq(s):
    return min(512, s)


def _win_fwd(qt, kp, vp, slopes, sinks, *, window, dil, seg, out_dtype, name):
    nh, _, s = qt.shape
    rep = nh // kp.shape[0]
    tq = _win_tq(s)
    tk = tq + 2 * WIN_PAD
    sp = s + 2 * WIN_PAD

    def body(*refs):
        qt_ref, k_ref, v_ref, sl_ref = refs[:4]
        o_ref, lse_ref, pen_ref = refs[-3:]
        i, h = pl.program_id(0), pl.program_id(1)

        @pl.when(h == 0)
        def _():
            pen_ref[...] = _win_penalty(i, tq, tk, window, dil, seg)

        base = pl.multiple_of(i * tq, tq)
        st = _win_scores(k_ref[0, pl.ds(base, tk), :], qt_ref[0], sl_ref[0][:, :1], pen_ref[...])
        mx = jnp.max(st, axis=0, keepdims=True)
        if sinks is not None:
            sink = refs[4][0][:, :1] * LOG2E
            mx = jnp.maximum(mx, sink)
        p = jnp.exp2(st - mx)
        den = jnp.sum(p, axis=0, keepdims=True)
        if sinks is not None:
            den = den + jnp.exp2(sink - mx)
        ot = lax.dot_general(v_ref[0, pl.ds(base, tk), :], p.astype(BF16), (((0,), (0,)), ((), ())),
                             preferred_element_type=F32)
        o_ref[0] = (ot / den).astype(o_ref.dtype)
        lse_ref[0] = mx * LN2 + jnp.log(den)

    blk_q = pl.BlockSpec((1, HEAD_DIM, tq), lambda i, h: (h, 0, i))
    blk_kv = pl.BlockSpec((1, sp, HEAD_DIM), lambda i, h: (h // rep, 0, 0))
    blk_h = pl.BlockSpec((1, 1, LANES), lambda i, h: (h, 0, 0))
    in_specs, args = [blk_q, blk_kv, blk_kv, blk_h], [qt, kp, vp, slopes]
    if sinks is not None:
        in_specs.append(blk_h)
        args.append(sinks)
    return pl.pallas_call(
        body, name=name, grid=(s // tq, nh), in_specs=in_specs,
        out_specs=[blk_q, pl.BlockSpec((1, 1, tq), lambda i, h: (h, 0, i))],
        out_shape=[jax.ShapeDtypeStruct((nh, HEAD_DIM, s), out_dtype), jax.ShapeDtypeStruct((nh, 1, s), F32)],
        scratch_shapes=[pltpu.VMEM((tk, tq), F32)],
        compiler_params=_params(("arbitrary", "arbitrary")),
    )(*args)


def _win_bwd(qt, kp, vp, slopes, sinks, dot, ot, delta, *, window, dil, seg, name):
    nh, _, s = qt.shape
    nkv = kp.shape[0]
    rep = nh // nkv
    tq = _win_tq(s)
    tk = tq + 2 * WIN_PAD
    sp = s + 2 * WIN_PAD
    n_in = 6 + (sinks is not None)

    def body(*refs):
        qt_ref, k_ref, v_ref, sl_ref, dot_ref, aux_ref = refs[:6]
        outs, pen_ref = refs[n_in:-1], refs[-1]
        dq_ref, dk_ref, dv_ref = outs[:3]
        i, h = pl.program_id(0), pl.program_id(1)

        @pl.when((i == 0) & (h == 0))
        def _():
            dk_ref[...] = jnp.zeros_like(dk_ref)
            dv_ref[...] = jnp.zeros_like(dv_ref)
            if sinks is not None:
                outs[3][...] = jnp.zeros_like(outs[3])

        @pl.when(h == 0)
        def _():
            pen_ref[...] = _win_penalty(i, tq, tk, window, dil, seg)

        base = pl.multiple_of(i * tq, tq)
        win = pl.ds(base, tk)
        kv = h // rep
        kw, q_t, do_t = k_ref[0, win, :], qt_ref[0], dot_ref[0]
        st = _win_scores(kw, q_t, sl_ref[0][:, :1], pen_ref[...])
        mx = jnp.max(st, axis=0, keepdims=True)
        if sinks is not None:
            sink = refs[6][0][:, :1] * LOG2E
            mx = jnp.maximum(mx, sink)
        p = jnp.exp2(st - mx)
        den = jnp.sum(p, axis=0, keepdims=True)
        if sinks is not None:
            p_sink = jnp.exp2(sink - mx)
            den = den + p_sink
        p = p / den
        dp = jnp.dot(v_ref[0, win, :], do_t, preferred_element_type=F32)
        if delta is None:
            row = jnp.sum(do_t.astype(F32) * aux_ref[0].astype(F32), axis=0, keepdims=True)
        else:
            row = aux_ref[0]
        ds = (p * (dp - row) * QK_SCALE).astype(BF16)
        nt = (((1,), (1,)), ((), ()))
        dv_ref[kv, :, win] += lax.dot_general(do_t, p.astype(BF16), nt, preferred_element_type=F32)
        dk_ref[kv, :, win] += lax.dot_general(q_t, ds, nt, preferred_element_type=F32)
        dq_ref[0] = lax.dot_general(kw, ds, (((0,), (0,)), ((), ())), preferred_element_type=F32)
        if sinks is not None:
            outs[3][h] += jnp.zeros((1, LANES), F32) - jnp.sum(p_sink / den * row, axis=1, keepdims=True)

    blk_q = pl.BlockSpec((1, HEAD_DIM, tq), lambda i, h: (h, 0, i))
    blk_row = pl.BlockSpec((1, 1, tq), lambda i, h: (h, 0, i))
    blk_kv = pl.BlockSpec((1, sp, HEAD_DIM), lambda i, h: (h // rep, 0, 0))
    blk_acc = pl.BlockSpec((nkv, HEAD_DIM, sp), lambda i, h: (0, 0, 0))
    blk_h = pl.BlockSpec((1, 1, LANES), lambda i, h: (h, 0, 0))
    in_specs = [blk_q, blk_kv, blk_kv, blk_h, blk_q, blk_q if delta is None else blk_row]
    args = [qt, kp, vp, slopes, dot, ot if delta is None else delta]
    out_specs = [blk_q, blk_acc, blk_acc]
    out_shape = [jax.ShapeDtypeStruct((nh, HEAD_DIM, s), F32), jax.ShapeDtypeStruct((nkv, HEAD_DIM, sp), F32),
                 jax.ShapeDtypeStruct((nkv, HEAD_DIM, sp), F32)]
    if sinks is not None:
        in_specs.append(blk_h)
        args.append(sinks)
        out_specs.append(pl.BlockSpec((nh, 1, LANES), lambda i, h: (0, 0, 0)))
        out_shape.append(jax.ShapeDtypeStruct((nh, 1, LANES), F32))
    res = pl.pallas_call(
        body, name=name, grid=(s // tq, nh), in_specs=in_specs, out_specs=out_specs, out_shape=out_shape,
        scratch_shapes=[pltpu.VMEM((tk, tq), F32)],
        compiler_params=_params(("arbitrary", "arbitrary")),
    )(*args)
    return res if sinks is not None else (*res, None)


def _group_weights(lse):
    e = jnp.exp(lse - jnp.max(lse, axis=0, keepdims=True))
    return e / jnp.sum(e, axis=0, keepdims=True)


def _b_combine_fwd(ot, lse):
    nh, _, s = ot.shape
    ng, hg, _ = lse.shape
    ts = min(512, s)

    def body(ot_ref, lse_ref, o_ref):
        alpha = _group_weights(lse_ref[...])
        for g in range(ng):
            for j in range(hg):
                o_ref[g * hg + j] = (ot_ref[g * hg + j] * alpha[g, j:j + 1, :]).astype(BF16)

    return pl.pallas_call(
        body, name="b_combine_fwd", grid=(s // ts,),
        in_specs=[pl.BlockSpec((nh, HEAD_DIM, ts), lambda i: (0, 0, i)), pl.BlockSpec((ng, hg, ts), lambda i: (0, 0, i))],
        out_specs=pl.BlockSpec((nh, HEAD_DIM, ts), lambda i: (0, 0, i)),
        out_shape=jax.ShapeDtypeStruct((nh, HEAD_DIM, s), BF16),
        compiler_params=_params(("parallel",)),
    )(ot, lse)


def _b_combine_bwd(dout, ot, lse):
    nh, _, s = ot.shape
    ng, hg, _ = lse.shape
    ts = min(512, s)

    def body(dout_ref, ot_ref, lse_ref, do_ref, delta_ref):
        alpha = _group_weights(lse_ref[...])
        for j in range(hg):
            e = [jnp.sum(dout_ref[g * hg + j].astype(F32) * ot_ref[g * hg + j], axis=0, keepdims=True)
                 for g in range(ng)]
            a = [alpha[g, j:j + 1, :] for g in range(ng)]
            mix = a[0] * e[0]
            for g in range(1, ng):
                mix = mix + a[g] * e[g]
            for g in range(ng):
                do_ref[g * hg + j] = (dout_ref[g * hg + j].astype(F32) * a[g]).astype(BF16)
                delta_ref[g * hg + j] = a[g] * mix

    blk = pl.BlockSpec((nh, HEAD_DIM, ts), lambda i: (0, 0, i))
    return pl.pallas_call(
        body, name="b_combine_bwd", grid=(s // ts,),
        in_specs=[blk, blk, pl.BlockSpec((ng, hg, ts), lambda i: (0, 0, i))],
        out_specs=[blk, pl.BlockSpec((nh, 1, ts), lambda i: (0, 0, i))],
        out_shape=[jax.ShapeDtypeStruct((nh, HEAD_DIM, s), BF16), jax.ShapeDtypeStruct((nh, 1, s), F32)],
        compiler_params=_params(("parallel",)),
    )(dout, ot, lse)


def _alibi_slopes(n):
    return 2.0 ** (-8.0 * jnp.arange(1, n + 1, dtype=F32) / n)


def _per_head(v):
    return jnp.broadcast_to(v.astype(F32)[:, None, None], (v.shape[0], 1, LANES))


def _dilate(x, dil):
    if dil == 1:
        return x
    s = x.shape[-1]
    return jnp.swapaxes(x.reshape(x.shape[:-1] + (s // dil, dil)), -1, -2).reshape(x.shape)


def _undilate(x, dil):
    if dil == 1:
        return x
    s = x.shape[-1]
    return jnp.swapaxes(x.reshape(x.shape[:-1] + (dil, s // dil)), -1, -2).reshape(x.shape)


def _heads_t(x, nh):
    return jnp.transpose(x.reshape(x.shape[0], nh, HEAD_DIM), (1, 2, 0))


def _tokens(xt):
    return jnp.transpose(xt, (2, 0, 1)).reshape(xt.shape[2], -1)


def _pad_tokens(xt):
    return jnp.pad(jnp.swapaxes(xt, 1, 2), ((0, 0), (WIN_PAD, WIN_PAD), (0, 0)))


def _mixer_fwd(kind, qkv, p, tabs, cargo, layer):
    s = qkv.shape[0]
    if kind == 0:
        qt, k, v = _a_prep(qkv, tabs[0], tabs[1], p["gq2"], p["gk2"])
        ot, lse, brought = _a_attn_fwd(qt, k, v, cargo, name=f"a_attn_fwd_l{layer}")
        return ot.reshape(-1, s), dict(qt=qt, k=k, v=v, ot=ot, lse=lse), brought
    assert cargo is None
    if kind == 2:
        nq, nk = C_HEADS * HEAD_DIM, C_KV * HEAD_DIM
        qt = _heads_t(qkv[:, :nq], C_HEADS)
        kp = _pad_tokens(_heads_t(qkv[:, nq:nq + nk], C_KV))
        vp = _pad_tokens(_heads_t(qkv[:, nq + nk:], C_KV))
        ot, _ = _win_fwd(qt, kp, vp, p["slopes"], p["sinks"], window=C_WINDOW, dil=1, seg=s, out_dtype=BF16,
                         name="c_attn_fwd")
        return ot.reshape(-1, s), dict(qt=qt, kp=kp, vp=vp, ot=ot), ()
    ng, hg, kg = len(B_GROUPS), B_HEADS_PER_GROUP, B_KV_PER_GROUP
    nq, nk = ng * hg * HEAD_DIM, ng * kg * HEAD_DIM
    qt_all = _heads_t(qkv[:, :nq], ng * hg)
    kt_all = _heads_t(qkv[:, nq:nq + nk], ng * kg)
    vt_all = _heads_t(qkv[:, nq + nk:], ng * kg)
    saved, outs, lses = [], [], []
    for g, (window, dil) in enumerate(B_GROUPS):
        qt = _dilate(qt_all[g * hg:(g + 1) * hg], dil)
        kp = _pad_tokens(_dilate(kt_all[g * kg:(g + 1) * kg], dil))
        vp = _pad_tokens(_dilate(vt_all[g * kg:(g + 1) * kg], dil))
        sl = p["slopes"][g * hg:(g + 1) * hg]
        ot, lse = _win_fwd(qt, kp, vp, sl, None, window=window // 2 // dil, dil=dil, seg=s // dil, out_dtype=F32,
                           name=f"b_attn_fwd_g{g}")
        saved.append(dict(qt=qt, kp=kp, vp=vp))
        outs.append(_undilate(ot, dil))
        lses.append(_undilate(lse[:, 0, :], dil))
    ot_all, lse_all = jnp.concatenate(outs, axis=0), jnp.stack(lses, axis=0)
    mixed = _b_combine_fwd(ot_all, lse_all)
    return mixed.reshape(-1, s), dict(groups=saved, ot=ot_all, lse=lse_all), ()


def _mixer_bwd(kind, do_t, qkv, sv, p, tabs, cargo, layer):
    s = do_t.shape[1]
    do_heads = do_t.reshape(-1, HEAD_DIM, s)
    small = {}
    if kind == 0:
        dqt, dkt, dvt, brought = _a_attn_bwd(sv["qt"], sv["k"], sv["v"], do_heads, sv["ot"], sv["lse"],
                                             cargo, name=f"a_attn_bwd_l{layer}")
        dqkv, dgq, dgk = _a_prep_bwd(dqt, dkt, dvt, qkv, tabs[0], tabs[1], p["gq2"], p["gk2"])
        small["q_gain"] = dgq[0, :HEAD_DIM] + dgq[0, HEAD_DIM:]
        small["k_gain"] = dgk[0, :HEAD_DIM] + dgk[0, HEAD_DIM:]
        return dqkv, small, brought
    assert cargo is None
    if kind == 2:
        dqt, dkt, dvt, dsink = _win_bwd(sv["qt"], sv["kp"], sv["vp"], p["slopes"], p["sinks"], do_heads,
                                        sv["ot"], None, window=C_WINDOW, dil=1, seg=s, name="c_attn_bwd")
        small["sinks"] = dsink[:, 0, 0]
        parts = [dqt, dkt[:, :, WIN_PAD:-WIN_PAD], dvt[:, :, WIN_PAD:-WIN_PAD]]
        return jnp.concatenate([_tokens(x) for x in parts], axis=1).astype(BF16), small, ()
    ng, hg, kg = len(B_GROUPS), B_HEADS_PER_GROUP, B_KV_PER_GROUP
    do_own, delta = _b_combine_bwd(do_heads, sv["ot"], sv["lse"])
    dqs, dks, dvs = [], [], []
    for g, (window, dil) in enumerate(B_GROUPS):
        gs = sv["groups"][g]
        dqt, dkt, dvt, _ = _win_bwd(gs["qt"], gs["kp"], gs["vp"], p["slopes"][g * hg:(g + 1) * hg], None,
                                    _dilate(do_own[g * hg:(g + 1) * hg], dil), None,
                                    _dilate(delta[g * hg:(g + 1) * hg], dil),
                                    window=window // 2 // dil, dil=dil, seg=s // dil, name=f"b_attn_bwd_g{g}")
        dqs.append(_undilate(dqt, dil))
        dks.append(_undilate(dkt[:, :, WIN_PAD:-WIN_PAD], dil))
        dvs.append(_undilate(dvt[:, :, WIN_PAD:-WIN_PAD], dil))
    parts = [jnp.concatenate(x, axis=0) for x in (dqs, dks, dvs)]
    return jnp.concatenate([_tokens(x) for x in parts], axis=1).astype(BF16), small, ()


LAYER_MATS = ("w_qkv", "w_o", "w1", "w2")
COLUMN_QUARTERS = ("w_qkv", "w1")


def _whole(key, gathered):
    q, r, c = gathered.shape
    if key == "w1":
        return gathered
    if key in COLUMN_QUARTERS:
        return jnp.transpose(gathered, (1, 0, 2)).reshape(r, q * c)
    return gathered.reshape(q * r, c)


def _quarters(key, g):
    r, c = g.shape
    if key in COLUMN_QUARTERS:
        return jnp.transpose(g.reshape(r, 4, c // 4), (1, 0, 2))
    return g.reshape(4, r // 4, c)


def _local_step(x, target, norms, mixer_params, shards, whole=None):
    s = x.shape[0]
    tabs = _rope_tables(s)
    if whole is None:
        assert MIXER_OF_LAYER[0][0] == 0
        first = _run_cargo(_gather_cargo([shards[0][key] for key in LAYER_MATS]), name="gather_l0")
        mats = {0: {key: _whole(key, g) for key, g in zip(LAYER_MATS, first)}}
        later = _gather_cargo([shards[layer][key] for layer in range(1, DEPTH) for key in LAYER_MATS])
    else:
        mats, later = dict(enumerate(whole)), None
    h = x
    saved = []
    for layer in range(DEPTH):
        kind = layer % N_MIXERS
        w, p = mats[layer], mixer_params[layer]
        hn, qkv = _norm_mm(h, norms["attn"][layer][None], w["w_qkv"], out_dtype=F32 if kind == 0 else BF16,
                           relu2=False, name=f"qkv_proj_l{layer}")
        o_t, sv, brought = _mixer_fwd(kind, qkv, p, tabs, later if layer == 0 else None, layer)
        for n, g in enumerate(brought):
            mats.setdefault(1 + n // len(LAYER_MATS), {})[LAYER_MATS[n % len(LAYER_MATS)]] = _whole(
                LAYER_MATS[n % len(LAYER_MATS)], g)
        h_mid = _mm_res(o_t, w["w_o"], h, a_transposed=True, name=f"o_proj_l{layer}")
        hn2, act = _norm_mm(h_mid, norms["mlp"][layer][None], w["w1"], out_dtype=BF16, relu2=True,
                            name=f"mlp_up_l{layer}")
        h_out = _mm_res(act, w["w2"], h_mid, a_transposed=False, name=f"mlp_down_l{layer}")
        saved.append(dict(h=h, hn=hn, qkv=qkv, o_t=o_t, mix=sv, h_mid=h_mid, hn2=hn2, act=act))
        h = h_out

    dh, loss, d_final = _loss_head(h, norms["final"][None], target)

    own, received, pending = {}, {}, []
    d_attn, d_mlp, small = [None] * DEPTH, [None] * DEPTH, [None] * DEPTH
    for layer in reversed(range(DEPTH)):
        kind = layer % N_MIXERS
        w, p, sv = mats[layer], mixer_params[layer], saved[layer]
        du = _mm_nt(dh, w["w2"], sv["act"], transpose_out=False, name=f"mlp_down_bwd_l{layer}")
        own[layer, "w2"] = _quarters("w2", _mm_tn(sv["act"], dh, x_transposed=False, column_quarters=False,
                                                  name=f"mlp_w2_grad_l{layer}"))
        own[layer, "w1"] = _mm_tn(sv["hn2"], du, x_transposed=False, column_quarters=True,
                                  name=f"mlp_w1_grad_l{layer}")
        dh_mid, d_mlp[layer] = _mm_nt_normbwd(du, w["w1"], sv["h_mid"], norms["mlp"][layer][None], dh,
                                              name=f"mlp_up_bwd_l{layer}")
        do_t = _mm_nt(dh_mid, w["w_o"], None, transpose_out=True, name=f"o_proj_bwd_l{layer}")
        own[layer, "w_o"] = _quarters("w_o", _mm_tn(sv["o_t"], dh_mid, x_transposed=True, column_quarters=False,
                                                    name=f"w_o_grad_l{layer}"))
        pending += [(layer, "w2"), (layer, "w1"), (layer, "w_o")]
        cargo = None
        if kind == 0 and whole is None:
            cargo, sent, pending = _scatter_cargo([own[item] for item in pending], None), pending, []
        dqkv, small[layer], brought = _mixer_bwd(kind, do_t, sv["qkv"], sv["mix"], p, tabs, cargo, layer)
        if cargo is not None:
            received.update(zip(sent, brought))
        own[layer, "w_qkv"] = _quarters("w_qkv", _mm_tn(sv["hn"], dqkv, x_transposed=False, column_quarters=False,
                                                        name=f"w_qkv_grad_l{layer}"))
        pending.append((layer, "w_qkv"))
        dh, d_attn[layer] = _mm_nt_normbwd(dqkv, w["w_qkv"], sv["h"], norms["attn"][layer][None], dh_mid,
                                           name=f"qkv_proj_bwd_l{layer}")
    return loss, dh, own, received, pending, dict(attn=d_attn, mlp=d_mlp, final=d_final, mixer=small)


CHIP_FLIPS = ((1, 0), (0, 1), (1, 1))


class _Cargo(NamedTuple):
    ins: tuple
    out_shape: tuple
    sem_shapes: tuple
    start: Callable
    wait: Callable


class _Carried(NamedTuple):
    body: Callable
    in_specs: list
    out_specs: list
    out_shape: list
    scratch: list
    args: tuple


def _carry(cargo, grid, n_in, n_out, body):
    if cargo is None:
        return _Carried(body, [], [], [], [], ())
    ci, co = len(cargo.ins), len(cargo.out_shape)

    def wrapped(*refs):
        ins, c_ins = refs[:n_in], refs[n_in:n_in + ci]
        outs, c_outs = refs[n_in + ci:n_in + ci + n_out], refs[n_in + ci + n_out:n_in + ci + n_out + co]
        sems = refs[n_in + ci + n_out + co:]
        first = last = None
        for axis, extent in enumerate(grid):
            at = pl.program_id(axis)
            first = (at == 0) if first is None else first & (at == 0)
            last = (at == extent - 1) if last is None else last & (at == extent - 1)

        @pl.when(first)
        def _():
            cargo.start(c_ins, c_outs, sems)

        body(*ins, *outs)

        @pl.when(last)
        def _():
            cargo.wait(c_ins, c_outs, sems)

    return _Carried(wrapped, [ANY] * ci, [ANY] * co, list(cargo.out_shape), list(cargo.sem_shapes), tuple(cargo.ins))


def _run_cargo(cargo, *, name):
    ci, co = len(cargo.ins), len(cargo.out_shape)

    def body(*refs):
        cargo.start(refs[:ci], refs[ci:ci + co], refs[ci + co:])
        cargo.wait(refs[:ci], refs[ci:ci + co], refs[ci + co:])

    return pl.pallas_call(body, name=name, in_specs=[ANY] * ci, out_specs=[ANY] * co, out_shape=list(cargo.out_shape),
                          scratch_shapes=list(cargo.sem_shapes))(*cargo.ins)


def _other_chip(x, y, j):
    fx, fy = CHIP_FLIPS[j]
    return (1 - x if fx else x), (1 - y if fy else y)


def _gather_cargo(shards):
    n = len(shards)
    halves = [a.shape[0] // 2 for a in shards]

    def copies(ins, outs, sems):
        ici_send, ici_recv, d2d_send, d2d_recv, local_sems = sems
        x, y, c = lax.axis_index("x"), lax.axis_index("y"), lax.axis_index("c")
        me = 2 * x + y

        def half(t, which):
            return pl.ds(pl.multiple_of(which * halves[t], 16), halves[t])

        def over_ici(t, j, arriving):
            px, py = _other_chip(x, y, j)
            return pltpu.make_async_remote_copy(
                src_ref=ins[t].at[half(t, c)], dst_ref=outs[t].at[2 * px + py if arriving else me, half(t, c)],
                send_sem=ici_send.at[t, j], recv_sem=ici_recv.at[t, j], device_id=(px, py, c), device_id_type=MESH)

        def over_d2d(t, j, arriving):
            px, py = _other_chip(x, y, j)
            mine = outs[t].at[2 * px + py, half(t, c)]
            return pltpu.make_async_remote_copy(
                src_ref=mine, dst_ref=outs[t].at[2 * px + py, half(t, 1 - c)] if arriving else mine,
                send_sem=d2d_send.at[t, j], recv_sem=d2d_recv.at[t, j], device_id=(x, y, 1 - c), device_id_type=MESH)

        return over_ici, over_d2d, lambda t: pltpu.make_async_copy(ins[t], outs[t].at[me], local_sems.at[t])

    def start(ins, outs, sems):
        over_ici, _, own = copies(ins, outs, sems)
        for t in range(n):
            own(t).start()
            for j in range(len(CHIP_FLIPS)):
                over_ici(t, j, False).start()

    def wait(ins, outs, sems):
        over_ici, over_d2d, own = copies(ins, outs, sems)
        for t in range(n):
            for j in range(len(CHIP_FLIPS)):
                over_ici(t, j, True).wait_recv()
                over_d2d(t, j, False).start()
        for t in range(n):
            for j in range(len(CHIP_FLIPS)):
                over_d2d(t, j, True).wait_recv()
                over_d2d(t, j, False).wait_send()
                over_ici(t, j, False).wait_send()
            own(t).wait()

    dma = pltpu.SemaphoreType.DMA
    return _Cargo(tuple(shards), tuple(jax.ShapeDtypeStruct((4,) + a.shape, a.dtype) for a in shards),
                  (dma((n, 3)), dma((n, 3)), dma((n, 3)), dma((n, 3)), dma((n,))), start, wait)


def _scatter_cargo(grads, small):
    n = len(grads)

    def copies(ins, outs, sems):
        x, y, c = lax.axis_index("x"), lax.axis_index("y"), lax.axis_index("c")
        me = 4 * x + 2 * y + c

        def remote(t, j):
            px, py = _other_chip(x, y, j)
            return pltpu.make_async_remote_copy(
                src_ref=ins[t].at[2 * px + py], dst_ref=outs[t].at[j], send_sem=sems[0].at[t, j],
                recv_sem=sems[1].at[t, j], device_id=(px, py, c), device_id_type=MESH)

        def small_remote(r, arriving):
            fx, fy, fc = (r + 1) // 4, ((r + 1) // 2) % 2, (r + 1) % 2
            px, py, pc = (1 - x if fx else x), (1 - y if fy else y), (1 - c if fc else c)
            return pltpu.make_async_remote_copy(
                src_ref=ins[n], dst_ref=outs[n].at[4 * px + 2 * py + pc if arriving else me],
                send_sem=sems[2].at[r], recv_sem=sems[3].at[r], device_id=(px, py, pc), device_id_type=MESH)

        return remote, small_remote, lambda: pltpu.make_async_copy(ins[n], outs[n].at[me], sems[4])

    def start(ins, outs, sems):
        remote, small_remote, small_own = copies(ins, outs, sems)
        if small is not None:
            small_own().start()
            for r in range(7):
                small_remote(r, False).start()
        for t in range(n):
            for j in range(len(CHIP_FLIPS)):
                remote(t, j).start()

    def wait(ins, outs, sems):
        remote, small_remote, small_own = copies(ins, outs, sems)
        if small is not None:
            for r in range(7):
                small_remote(r, True).wait_recv()
                small_remote(r, False).wait_send()
            small_own().wait()
        for t in range(n):
            for j in range(len(CHIP_FLIPS)):
                remote(t, j).wait()

    dma = pltpu.SemaphoreType.DMA
    ins = tuple(grads) + (() if small is None else (small,))
    out_shape = tuple(jax.ShapeDtypeStruct((3,) + g.shape[1:], g.dtype) for g in grads)
    sem_shapes = (dma((n, 3)), dma((n, 3)))
    if small is not None:
        out_shape += (jax.ShapeDtypeStruct((8,) + small.shape, small.dtype),)
        sem_shapes += (dma((7,)), dma((7,)), dma(()))
    return _Cargo(ins, out_shape, sem_shapes, start, wait)


def _swap_cores(parts):
    n = len(parts)

    def body(*refs):
        ins, outs = refs[:n], refs[n:2 * n]
        send_sems, recv_sems = refs[2 * n:]
        peer = (lax.axis_index("x"), lax.axis_index("y"), 1 - lax.axis_index("c"))
        copies = [pltpu.make_async_remote_copy(src_ref=ins[t], dst_ref=outs[t], send_sem=send_sems.at[t],
                                               recv_sem=recv_sems.at[t], device_id=peer, device_id_type=MESH)
                  for t in range(n)]
        for cp in copies:
            cp.start()
        for cp in copies:
            cp.wait()

    return pl.pallas_call(
        body, name="swap_cores", in_specs=[ANY] * n, out_specs=[ANY] * n,
        out_shape=[jax.ShapeDtypeStruct(a.shape, a.dtype) for a in parts],
        scratch_shapes=[pltpu.SemaphoreType.DMA((n,)), pltpu.SemaphoreType.DMA((n,))],
    )(*parts)


def _rows_tile(r):
    return 256 if r % 256 == 0 else r


def _sum_quarters(own, recv, *, name):
    r, c = own.shape
    tr = _rows_tile(r)

    def body(own_ref, recv_ref, o_ref):
        acc = own_ref[...].astype(F32)
        for j in range(3):
            acc = acc + recv_ref[j].astype(F32)
        o_ref[...] = acc

    return pl.pallas_call(
        body, name=name, grid=(r // tr,),
        in_specs=[pl.BlockSpec((tr, c), lambda i: (i, 0)), pl.BlockSpec((3, tr, c), lambda i: (0, i, 0))],
        out_specs=pl.BlockSpec((tr, c), lambda i: (i, 0)),
        out_shape=jax.ShapeDtypeStruct((r, c), F32),
        compiler_params=_params(("parallel",)),
    )(own, recv)


def _adamw(w, m, v, parts, *, name):
    r, c = w.shape
    tr = _rows_tile(r)
    c1, c2 = 1.0 - ADAM_B1 ** ADAM_STEP, 1.0 - ADAM_B2 ** ADAM_STEP
    n_parts = len(parts)

    def body(*refs):
        w_ref, m_ref, v_ref = refs[:3]
        g_ref, d_ref, nm_ref, nv_ref = refs[3 + n_parts:]
        terms = []
        for p_ref in refs[3:3 + n_parts]:
            terms += [p_ref[...]] if len(p_ref.shape) == 2 else [p_ref[j] for j in range(p_ref.shape[0])]
        g = terms[0]
        for term in terms[1:]:
            g = g + term
        m_new = ADAM_B1 * m_ref[...] + (1.0 - ADAM_B1) * g
        v_new = ADAM_B2 * v_ref[...] + (1.0 - ADAM_B2) * (g * g)
        step = (m_new / c1) / (jnp.sqrt(v_new / c2) + ADAM_EPS)
        g_ref[...] = g
        d_ref[...] = -ADAM_LR * (step + ADAM_WD * w_ref[...])
        nm_ref[...] = m_new
        nv_ref[...] = v_new

    blk = pl.BlockSpec((tr, c), lambda i: (i, 0))
    part_specs = [blk if p.ndim == 2 else pl.BlockSpec((p.shape[0], tr, c), lambda i: (0, i, 0)) for p in parts]
    return pl.pallas_call(
        body, name=name, grid=(r // tr,), in_specs=[blk, blk, blk] + part_specs,
        out_specs=[blk] * 4, out_shape=[jax.ShapeDtypeStruct((r, c), F32)] * 4,
        compiler_params=_params(("parallel",)),
    )(w, m, v, *parts)


MATS = ("a_w_qkv", "a_w_o", "b_w_qkv", "b_w_o", "c_w_qkv", "c_w_o", "mlp_w1", "mlp_w2")
SMALLS = ("attn_norm", "mlp_norm", "a_q_gain", "a_k_gain", "c_sinks", "final_norm")
WEIGHTS = ("attn_norm", "mlp_norm", "a_w_qkv", "a_q_gain", "a_k_gain", "a_w_o", "b_w_qkv", "b_w_o", "c_w_qkv",
           "c_sinks", "c_w_o", "mlp_w1", "mlp_w2", "final_norm")
MIXER_OF_LAYER = tuple((layer % N_MIXERS, sum(1 for q in range(layer) if q % N_MIXERS == layer % N_MIXERS))
                       for layer in range(DEPTH))
SMALL_ROWS = 8


def _pack_small(values):
    rows, spans, at = [], [], 0
    for v in values:
        flat = v.reshape(-1)
        n = -(-flat.shape[0] // (SMALL_ROWS * LANES)) * SMALL_ROWS
        rows.append(jnp.pad(flat, (0, n * LANES - flat.shape[0])).reshape(n, LANES))
        spans.append((at, n))
        at += n
    return jnp.concatenate(rows, axis=0), spans


def kernel(x, attn_norm, mlp_norm, a_w_qkv, a_q_gain, a_k_gain, a_w_o, b_w_qkv, b_w_o, c_w_qkv, c_sinks, c_w_o, mlp_w1, mlp_w2, final_norm, loss_target, m_attn_norm, m_mlp_norm, m_a_w_qkv, m_a_q_gain, m_a_k_gain, m_a_w_o, m_b_w_qkv, m_b_w_o, m_c_w_qkv, m_c_sinks, m_c_w_o, m_mlp_w1, m_mlp_w2, m_final_norm, v_attn_norm, v_mlp_norm, v_a_w_qkv, v_a_q_gain, v_a_k_gain, v_a_w_o, v_b_w_qkv, v_b_w_o, v_c_w_qkv, v_c_sinks, v_c_w_o, v_mlp_w1, v_mlp_w2, v_final_norm):
    env = dict(locals())
    w = {name: env[name] for name in WEIGHTS}
    mom = {name: (env["m_" + name], env["v_" + name]) for name in WEIGHTS}

    prefix = ("a", "b", "c")
    shards, mixer_params = [], []
    for layer, (kind, j) in enumerate(MIXER_OF_LAYER):
        shards.append(dict(w_qkv=w[prefix[kind] + "_w_qkv"][j].astype(BF16), w_o=w[prefix[kind] + "_w_o"][j].astype(BF16),
                           w1=mlp_w1[layer].astype(BF16), w2=mlp_w2[layer].astype(BF16)))
        if kind == 0:
            mixer_params.append(dict(gq2=jnp.tile(a_q_gain[j], 2)[None], gk2=jnp.tile(a_k_gain[j], 2)[None]))
        elif kind == 1:
            mixer_params.append(dict(slopes=_per_head(_alibi_slopes(len(B_GROUPS) * B_HEADS_PER_GROUP))))
        else:
            mixer_params.append(dict(slopes=_per_head(_alibi_slopes(C_HEADS)), sinks=_per_head(c_sinks[j])))

    norms = dict(attn=attn_norm, mlp=mlp_norm, final=final_norm)
    loss_part, grad_x, own, received, pending, g_small = _local_step(x[0], loss_target[0], norms, mixer_params, shards)
    loss = lax.psum(loss_part[0, 0], ("x", "y", "c"))

    of_kind = lambda kind, key: jnp.stack([g_small["mixer"][layer][key] for layer, (k, _) in enumerate(MIXER_OF_LAYER)
                                           if k == kind])
    small_grads = dict(
        attn_norm=jnp.concatenate(g_small["attn"], axis=0), mlp_norm=jnp.concatenate(g_small["mlp"], axis=0),
        a_q_gain=of_kind(0, "q_gain"), a_k_gain=of_kind(0, "k_gain"), c_sinks=of_kind(2, "sinks"),
        final_norm=g_small["final"][0])
    packed, spans = _pack_small([small_grads[name] for name in SMALLS])
    *last, all_small = _run_cargo(_scatter_cargo([own[item] for item in pending], packed), name="scatter_last")
    received.update(zip(pending, last))

    me_chip = 2 * lax.axis_index("x") + lax.axis_index("y")
    partial = []
    for name in MATS:
        key = name[2:] if name[0] in "abc" else name[4:]
        layers = [layer for layer, (kind, _) in enumerate(MIXER_OF_LAYER)
                  if name.startswith("mlp") or prefix[kind] == name[0]]
        sums = [_sum_quarters(lax.dynamic_index_in_dim(own[layer, key], me_chip, axis=0, keepdims=False),
                              received[layer, key], name=f"sum_{name}_l{layer}") for layer in layers]
        partial.append(jnp.concatenate(sums, axis=0))
    other = _swap_cores(partial)

    out = {}
    for name, mine, theirs in zip(MATS, partial, other):
        shape = w[name].shape
        res = _adamw(*[a.reshape(-1, shape[-1]) for a in (w[name], *mom[name])], [mine, theirs], name=f"adamw_{name}")
        out[name] = [a.reshape(shape) for a in res]
    for name, (at, n) in zip(SMALLS, spans):
        shape = w[name].shape
        packed_in = [_pack_small([a])[0] for a in (w[name], *mom[name])]
        res = _adamw(*packed_in, [all_small[:, at:at + n]], name=f"adamw_{name}")
        out[name] = [a.reshape(-1)[:w[name].size].reshape(shape) for a in res]

    return (loss, grad_x[None], *[out[name][0] for name in WEIGHTS], *[out[name][1] for name in WEIGHTS],
            *[out[name][2] for name in WEIGHTS], *[out[name][3] for name in WEIGHTS])
```

```python
from typing import Callable, NamedTuple

import jax
import jax.numpy as jnp
from jax import lax
from jax.experimental import pallas as pl
from jax.experimental.pallas import tpu as pltpu

F32 = jnp.float32
BF16 = jnp.bfloat16
MESH = pl.DeviceIdType.MESH
ANY = pl.BlockSpec(memory_space=pl.ANY)

D_MODEL = 1024
HEAD_DIM = 64
GRID_W = 64
ROPE_THETA = 10000.0
RMS_EPS = 1e-6
QK_SCALE = HEAD_DIM ** -0.5
LOG2E = 1.4426950408889634
LN2 = 0.6931471805599453
A_HEADS, A_KV = 16, 4
B_GROUPS = ((128, 1), (512, 4), (2048, 16))
B_HEADS_PER_GROUP, B_KV_PER_GROUP = 6, 2
C_HEADS, C_KV, C_WINDOW = 16, 4, 128
DEPTH, N_MIXERS = 4, 3
ADAM_LR, ADAM_B1, ADAM_B2, ADAM_EPS, ADAM_WD, ADAM_STEP = 0.001, 0.9, 0.999, 1e-08, 0.01, 10

WIN_PAD = 128
V7X_VMEM_BUDGET = 48 * 1024 * 1024
LANES = 128
ROW_TILE = 1024


def _params(semantics):
    return pltpu.CompilerParams(dimension_semantics=semantics, vmem_limit_bytes=V7X_VMEM_BUDGET)


def _tile(n, cap):
    if n <= cap:
        return n
    t = (cap // LANES) * LANES
    while n % t:
        t -= LANES
    return t


def _norm_mm(h, gain, w, *, out_dtype, relu2, transpose_out, name):
    m, d = h.shape
    by_quarter = w.ndim == 3
    n = w.shape[-1] * (4 if by_quarter else 1)
    tm, tn = min(ROW_TILE, m), (w.shape[-1] if by_quarter else _tile(n, 2048))
    w_spec = (pl.BlockSpec((None, d, tn), lambda i, j: (j, 0, 0)) if by_quarter
              else pl.BlockSpec((d, tn), lambda i, j: (0, j)))
    y_spec = (pl.BlockSpec((tn, tm), lambda i, j: (j, i)) if transpose_out
              else pl.BlockSpec((tm, tn), lambda i, j: (i, j)))

    def body(h_ref, g_ref, w_ref, hn_ref, y_ref):
        @pl.when(pl.program_id(1) == 0)
        def _():
            x = h_ref[...]
            r = lax.rsqrt(jnp.mean(x * x, axis=-1, keepdims=True) + RMS_EPS)
            hn_ref[...] = (x * r * g_ref[...]).astype(BF16)

        if transpose_out:
            y = lax.dot_general(w_ref[...], hn_ref[...], (((0,), (1,)), ((), ())), preferred_element_type=F32)
        else:
            y = jnp.dot(hn_ref[...], w_ref[...], preferred_element_type=F32)
        if relu2:
            y = jnp.maximum(y, 0.0)
            y = y * y
        y_ref[...] = y.astype(y_ref.dtype)

    return pl.pallas_call(
        body, name=name, grid=(m // tm, n // tn),
        in_specs=[pl.BlockSpec((tm, d), lambda i, j: (i, 0)), pl.BlockSpec((1, d), lambda i, j: (0, 0)), w_spec],
        out_specs=[pl.BlockSpec((tm, d), lambda i, j: (i, 0)), y_spec],
        out_shape=[jax.ShapeDtypeStruct((m, d), BF16), jax.ShapeDtypeStruct((n, m) if transpose_out else (m, n), out_dtype)],
        compiler_params=_params(("parallel", "arbitrary")),
    )(h, gain, w)


def _mm_res(a, w, h_in, *, a_transposed, name):
    k, d = w.shape
    m = h_in.shape[0]
    tm, tk = min(ROW_TILE, m), _tile(k, 1152)
    lhs_contracts = 0 if a_transposed else 1

    def body(a_ref, w_ref, h_ref, o_ref):
        @pl.when(pl.program_id(1) == 0)
        def _():
            o_ref[...] = h_ref[...]

        o_ref[...] += lax.dot_general(a_ref[...], w_ref[...], (((lhs_contracts,), (0,)), ((), ())),
                                      preferred_element_type=F32)

    a_spec = (pl.BlockSpec((tk, tm), lambda i, j: (j, i)) if a_transposed
              else pl.BlockSpec((tm, tk), lambda i, j: (i, j)))
    return pl.pallas_call(
        body, name=name, grid=(m // tm, k // tk),
        in_specs=[a_spec, pl.BlockSpec((tk, d), lambda i, j: (j, 0)), pl.BlockSpec((tm, d), lambda i, j: (i, 0))],
        out_specs=pl.BlockSpec((tm, d), lambda i, j: (i, 0)),
        out_shape=jax.ShapeDtypeStruct((m, d), F32),
        compiler_params=_params(("parallel", "arbitrary")),
    )(a, w, h_in)


def _mm_nt(a, w, act, *, transpose_out, name):
    m, d = a.shape
    n = w.shape[0]
    tm, tn = min(ROW_TILE, m), _tile(n, 1152)
    assert act is None or not transpose_out
    nt = (((1,), (1,)), ((), ()))

    def body(*refs):
        a_ref, w_ref = refs[0], refs[1]
        o_ref = refs[-1]
        if transpose_out:
            acc = lax.dot_general(w_ref[...], a_ref[...].astype(BF16), nt, preferred_element_type=F32)
        else:
            acc = lax.dot_general(a_ref[...].astype(BF16), w_ref[...], nt, preferred_element_type=F32)
        if act is not None:
            acc = acc * (2.0 * jnp.sqrt(refs[2][...].astype(F32)))
        o_ref[...] = acc.astype(BF16)

    in_specs = [pl.BlockSpec((tm, d), lambda i, j: (i, 0)), pl.BlockSpec((tn, d), lambda i, j: (j, 0))]
    args = [a, w]
    if act is not None:
        in_specs.append(pl.BlockSpec((tm, tn), lambda i, j: (i, j)))
        args.append(act)
    out_spec = (pl.BlockSpec((tn, tm), lambda i, j: (j, i)) if transpose_out
                else pl.BlockSpec((tm, tn), lambda i, j: (i, j)))
    return pl.pallas_call(
        body, name=name, grid=(m // tm, n // tn), in_specs=in_specs, out_specs=out_spec,
        out_shape=jax.ShapeDtypeStruct((n, m) if transpose_out else (m, n), BF16),
        compiler_params=_params(("parallel", "parallel")),
    )(*args)


def _rmsnorm_bwd(dn, x, gain):
    r = lax.rsqrt(jnp.mean(x * x, axis=-1, keepdims=True) + RMS_EPS)
    xh = x * r
    dgain = jnp.sum(dn * xh, axis=0, keepdims=True)
    u = dn * gain
    dx = r * (u - xh * jnp.mean(u * xh, axis=-1, keepdims=True))
    return dx, dgain


def _mm_nt_normbwd(g, w, h, gain, dh_in, *, g_transposed, name):
    m, k = g.shape[::-1] if g_transposed else g.shape
    by_quarter = w.ndim == 3
    d = w.shape[-2]
    tm, tk = min(ROW_TILE, m), (w.shape[-1] if by_quarter else _tile(k, 1024))
    sub = min(256, tm)
    nk = k // tk
    w_spec = (pl.BlockSpec((None, d, tk), lambda i, j: (j, 0, 0)) if by_quarter
              else pl.BlockSpec((d, tk), lambda i, j: (0, j)))

    def body(g_ref, w_ref, h_ref, gain_ref, dh_ref, o_ref, dg_ref, acc_ref):
        i, j = pl.program_id(0), pl.program_id(1)

        @pl.when((i == 0) & (j == 0))
        def _():
            dg_ref[...] = jnp.zeros_like(dg_ref)

        @pl.when(j == 0)
        def _():
            acc_ref[...] = jnp.zeros_like(acc_ref)

        acc_ref[...] += lax.dot_general(g_ref[...], w_ref[...], (((0 if g_transposed else 1,), (1,)), ((), ())),
                                        preferred_element_type=F32)

        @pl.when(j == nk - 1)
        def _():
            for r in range(0, tm, sub):
                rows = slice(r, r + sub)
                dx, dgain = _rmsnorm_bwd(acc_ref[rows, :], h_ref[rows, :], gain_ref[...])
                dg_ref[...] += dgain
                o_ref[rows, :] = dh_ref[rows, :] + dx

    return pl.pallas_call(
        body, name=name, grid=(m // tm, nk),
        in_specs=[pl.BlockSpec((tk, tm), lambda i, j: (j, i)) if g_transposed
                  else pl.BlockSpec((tm, tk), lambda i, j: (i, j)), w_spec,
                  pl.BlockSpec((tm, d), lambda i, j: (i, 0)), pl.BlockSpec((1, d), lambda i, j: (0, 0)),
                  pl.BlockSpec((tm, d), lambda i, j: (i, 0))],
        out_specs=[pl.BlockSpec((tm, d), lambda i, j: (i, 0)), pl.BlockSpec((1, d), lambda i, j: (0, 0))],
        out_shape=[jax.ShapeDtypeStruct((m, d), F32), jax.ShapeDtypeStruct((1, d), F32)],
        scratch_shapes=[pltpu.VMEM((tm, d), F32)],
        compiler_params=_params(("arbitrary", "arbitrary")),
    )(g, w, h, gain, dh_in)


def _mm_tn(x, g, *, x_transposed, g_transposed, column_quarters, name):
    k, m = x.shape if x_transposed else x.shape[::-1]
    n = g.shape[0] if g_transposed else g.shape[1]
    tm, tk, tn = min(ROW_TILE, m), _tile(k, 1152), (n // 4 if column_quarters else _tile(n, 1024))
    nm = m // tm
    lhs_contracts, rhs_contracts = (1 if x_transposed else 0), (1 if g_transposed else 0)
    g_spec = (pl.BlockSpec((tn, tm), lambda a, b, s: (b, s)) if g_transposed
              else pl.BlockSpec((tm, tn), lambda a, b, s: (s, b)))

    def body(x_ref, g_ref, o_ref, acc_ref):
        s = pl.program_id(2)

        @pl.when(s == 0)
        def _():
            acc_ref[...] = jnp.zeros_like(acc_ref)

        acc_ref[...] += lax.dot_general(x_ref[...], g_ref[...].astype(BF16),
                                        (((lhs_contracts,), (rhs_contracts,)), ((), ())), preferred_element_type=F32)

        @pl.when(s == nm - 1)
        def _():
            o_ref[...] = acc_ref[...].astype(BF16)

    x_spec = (pl.BlockSpec((tk, tm), lambda a, b, s: (a, s)) if x_transposed
              else pl.BlockSpec((tm, tk), lambda a, b, s: (s, a)))
    out_spec = (pl.BlockSpec((None, tk, tn), lambda a, b, s: (b, a, 0)) if column_quarters
                else pl.BlockSpec((tk, tn), lambda a, b, s: (a, b)))
    return pl.pallas_call(
        body, name=name, grid=(k // tk, n // tn, nm),
        in_specs=[x_spec, g_spec], out_specs=out_spec,
        out_shape=jax.ShapeDtypeStruct((4, k, n // 4) if column_quarters else (k, n), BF16),
        scratch_shapes=[pltpu.VMEM((tk, tn), F32)],
        compiler_params=_params(("parallel", "parallel", "arbitrary")),
    )(x, g)


def _loss_head(h, gain, target):
    m, d = h.shape
    tm = 512

    def body(h_ref, g_ref, t_ref, dh_ref, loss_ref, dg_ref):
        @pl.when(pl.program_id(0) == 0)
        def _():
            loss_ref[...] = jnp.zeros_like(loss_ref)
            dg_ref[...] = jnp.zeros_like(dg_ref)

        x = h_ref[...]
        gain_v = g_ref[...]
        r = lax.rsqrt(jnp.mean(x * x, axis=-1, keepdims=True) + RMS_EPS)
        err = x * r * gain_v - t_ref[...]
        loss_ref[...] += 0.5 * jnp.sum(jnp.mean(err * err, axis=-1, keepdims=True), axis=0, keepdims=True)
        dx, dgain = _rmsnorm_bwd(err * (1.0 / d), x, gain_v)
        dg_ref[...] += dgain
        dh_ref[...] = dx

    return pl.pallas_call(
        body, name="loss_head", grid=(m // tm,),
        in_specs=[pl.BlockSpec((tm, d), lambda i: (i, 0)), pl.BlockSpec((1, d), lambda i: (0, 0)),
                  pl.BlockSpec((tm, d), lambda i: (i, 0))],
        out_specs=[pl.BlockSpec((tm, d), lambda i: (i, 0)), pl.BlockSpec((1, LANES), lambda i: (0, 0)),
                   pl.BlockSpec((1, d), lambda i: (0, 0))],
        out_shape=[jax.ShapeDtypeStruct((m, d), F32), jax.ShapeDtypeStruct((1, LANES), F32),
                   jax.ShapeDtypeStruct((1, d), F32)],
        compiler_params=_params(("arbitrary",)),
    )(h, gain, target)


def _rope_tables(s):
    t = jnp.arange(s)
    row = (t // GRID_W).astype(F32)
    col = (t % GRID_W).astype(F32)
    axis_dim = HEAD_DIM // 2
    inv_freq = ROPE_THETA ** (-jnp.arange(0, axis_dim, 2, dtype=F32) / axis_dim)
    ar, ac = row[:, None] * inv_freq, col[:, None] * inv_freq
    cos = jnp.concatenate([jnp.cos(ar), jnp.cos(ar), jnp.cos(ac), jnp.cos(ac)], axis=-1)
    sin = jnp.concatenate([-jnp.sin(ar), jnp.sin(ar), -jnp.sin(ac), jnp.sin(ac)], axis=-1)
    return jnp.tile(cos, (1, 2)), jnp.tile(sin, (1, 2))


def _swap16(x):
    lane = lax.broadcasted_iota(jnp.int32, x.shape, 1)
    return jnp.where((lane % 32) < 16, pltpu.roll(x, LANES - 16, 1), pltpu.roll(x, 16, 1))


def _head_mean(v):
    lane = lax.broadcasted_iota(jnp.int32, v.shape, 1)
    lo = lane < HEAD_DIM
    s_all = jnp.sum(v, axis=-1, keepdims=True)
    s_lo = jnp.sum(jnp.where(lo, v, 0.0), axis=-1, keepdims=True)
    return jnp.where(lo, s_lo, s_all - s_lo) * (1.0 / HEAD_DIM)


def _norm_rope(x, gain2, cos, sin):
    r = lax.rsqrt(_head_mean(x * x) + RMS_EPS)
    nrm = x * r * gain2
    return nrm * cos + _swap16(nrm) * sin


def _norm_rope_bwd(dy, x, gain2, cos, sin):
    dn = dy * cos + _swap16(dy * sin)
    r = lax.rsqrt(_head_mean(x * x) + RMS_EPS)
    xh = x * r
    dgain = jnp.sum(dn * xh, axis=0, keepdims=True)
    u = dn * gain2
    return r * (u - xh * _head_mean(u * xh)), dgain


def _a_prep(qkv, cos, sin, gq2, gk2):
    s = qkv.shape[0]
    tr = 256
    nq, nk = A_HEADS * HEAD_DIM, A_KV * HEAD_DIM

    def body(qkv_ref, cos_ref, sin_ref, gq_ref, gk_ref, qt_ref, k_ref, v_ref):
        cos_v, sin_v = cos_ref[...], sin_ref[...]
        for c in range(nq // LANES):
            y = _norm_rope(qkv_ref[:, c * LANES:(c + 1) * LANES], gq_ref[...], cos_v, sin_v) * (QK_SCALE * LOG2E)
            yt = y.T
            qt_ref[2 * c] = yt[:HEAD_DIM].astype(BF16)
            qt_ref[2 * c + 1] = yt[HEAD_DIM:].astype(BF16)
        for c in range(nk // LANES):
            y = _norm_rope(qkv_ref[:, nq + c * LANES:nq + (c + 1) * LANES], gk_ref[...], cos_v, sin_v)
            k_ref[2 * c] = y[:, :HEAD_DIM].astype(BF16)
            k_ref[2 * c + 1] = y[:, HEAD_DIM:].astype(BF16)
            x = qkv_ref[:, nq + nk + c * LANES:nq + nk + (c + 1) * LANES]
            v_ref[2 * c] = x[:, :HEAD_DIM].astype(BF16)
            v_ref[2 * c + 1] = x[:, HEAD_DIM:].astype(BF16)

    return pl.pallas_call(
        body, name="a_prep", grid=(s // tr,),
        in_specs=[pl.BlockSpec((tr, nq + 2 * nk), lambda i: (i, 0)), pl.BlockSpec((tr, LANES), lambda i: (i, 0)),
                  pl.BlockSpec((tr, LANES), lambda i: (i, 0)), pl.BlockSpec((1, LANES), lambda i: (0, 0)),
                  pl.BlockSpec((1, LANES), lambda i: (0, 0))],
        out_specs=[pl.BlockSpec((A_HEADS, HEAD_DIM, tr), lambda i: (0, 0, i)),
                   pl.BlockSpec((A_KV, tr, HEAD_DIM), lambda i: (0, i, 0)),
                   pl.BlockSpec((A_KV, tr, HEAD_DIM), lambda i: (0, i, 0))],
        out_shape=[jax.ShapeDtypeStruct((A_HEADS, HEAD_DIM, s), BF16), jax.ShapeDtypeStruct((A_KV, s, HEAD_DIM), BF16),
                   jax.ShapeDtypeStruct((A_KV, s, HEAD_DIM), BF16)],
        compiler_params=_params(("parallel",)),
    )(qkv, cos, sin, gq2, gk2)


def _a_prep_bwd(dqt, dkt, dvt, qkv, cos, sin, gq2, gk2):
    s = qkv.shape[0]
    tr = 256
    nq, nk = A_HEADS * HEAD_DIM, A_KV * HEAD_DIM

    def body(dqt_ref, dkt_ref, dvt_ref, qkv_ref, cos_ref, sin_ref, gq_ref, gk_ref, o_ref, dgq_ref, dgk_ref):
        @pl.when(pl.program_id(0) == 0)
        def _():
            dgq_ref[...] = jnp.zeros_like(dgq_ref)
            dgk_ref[...] = jnp.zeros_like(dgk_ref)

        cos_v, sin_v = cos_ref[...], sin_ref[...]

        def pair(ref, c):
            return jnp.concatenate([ref[2 * c], ref[2 * c + 1]], axis=0).T

        for c in range(nq // LANES):
            dx, dg = _norm_rope_bwd(pair(dqt_ref, c) * QK_SCALE, qkv_ref[:, c * LANES:(c + 1) * LANES],
                                    gq_ref[...], cos_v, sin_v)
            o_ref[:, c * LANES:(c + 1) * LANES] = dx.astype(BF16)
            dgq_ref[...] += dg
        for c in range(nk // LANES):
            lo = nq + c * LANES
            dx, dg = _norm_rope_bwd(pair(dkt_ref, c) * LN2, qkv_ref[:, lo:lo + LANES], gk_ref[...], cos_v, sin_v)
            o_ref[:, lo:lo + LANES] = dx.astype(BF16)
            dgk_ref[...] += dg
            o_ref[:, lo + nk:lo + nk + LANES] = pair(dvt_ref, c).astype(BF16)

    return pl.pallas_call(
        body, name="a_prep_bwd", grid=(s // tr,),
        in_specs=[pl.BlockSpec((A_HEADS, HEAD_DIM, tr), lambda i: (0, 0, i)),
                  pl.BlockSpec((A_KV, HEAD_DIM, tr), lambda i: (0, 0, i)),
                  pl.BlockSpec((A_KV, HEAD_DIM, tr), lambda i: (0, 0, i)),
                  pl.BlockSpec((tr, nq + 2 * nk), lambda i: (i, 0)), pl.BlockSpec((tr, LANES), lambda i: (i, 0)),
                  pl.BlockSpec((tr, LANES), lambda i: (i, 0)), pl.BlockSpec((1, LANES), lambda i: (0, 0)),
                  pl.BlockSpec((1, LANES), lambda i: (0, 0))],
        out_specs=[pl.BlockSpec((tr, nq + 2 * nk), lambda i: (i, 0)), pl.BlockSpec((1, LANES), lambda i: (0, 0)),
                   pl.BlockSpec((1, LANES), lambda i: (0, 0))],
        out_shape=[jax.ShapeDtypeStruct((s, nq + 2 * nk), BF16), jax.ShapeDtypeStruct((1, LANES), F32),
                   jax.ShapeDtypeStruct((1, LANES), F32)],
        compiler_params=_params(("arbitrary",)),
    )(dqt, dkt, dvt, qkv, cos, sin, gq2, gk2)


A_TQ = 1024
A_TQ_SUB = 256
A_TQ_BWD = 512
A_KEY_CHUNK = 512


def _a_attn_fwd(qt, k, v, cargo, *, name):
    nh, _, s = qt.shape
    rep = nh // k.shape[0]
    tq = min(A_TQ, s)
    sub = min(A_TQ_SUB, tq)
    grid = (nh, s // tq)

    def body(qt_ref, k_ref, v_ref, o_ref, lse_ref):
        scores = [jnp.dot(k_ref[0], qt_ref[0, :, a:a + sub], preferred_element_type=F32)
                  for a in range(0, tq, sub)]
        for a, st in zip(range(0, tq, sub), scores):
            mx = jnp.max(st, axis=0, keepdims=True)
            p = jnp.exp2(st - mx)
            den = jnp.sum(p, axis=0, keepdims=True)
            ot = lax.dot_general(v_ref[0], p.astype(BF16), (((0,), (0,)), ((), ())), preferred_element_type=F32)
            o_ref[0, :, a:a + sub] = (ot / den).astype(BF16)
            lse_ref[0, :, a:a + sub] = mx + jnp.log(den) * LOG2E

    carried = _carry(cargo, grid, 3, 2, body)
    res = pl.pallas_call(
        carried.body, name=name, grid=grid,
        in_specs=[pl.BlockSpec((1, HEAD_DIM, tq), lambda h, i: (h, 0, i)),
                  pl.BlockSpec((1, s, HEAD_DIM), lambda h, i: (h // rep, 0, 0)),
                  pl.BlockSpec((1, s, HEAD_DIM), lambda h, i: (h // rep, 0, 0))] + carried.in_specs,
        out_specs=[pl.BlockSpec((1, HEAD_DIM, tq), lambda h, i: (h, 0, i)),
                   pl.BlockSpec((1, 1, tq), lambda h, i: (h, 0, i))] + carried.out_specs,
        out_shape=[jax.ShapeDtypeStruct((nh, HEAD_DIM, s), BF16), jax.ShapeDtypeStruct((nh, 1, s), F32)]
        + carried.out_shape,
        scratch_shapes=carried.scratch,
        compiler_params=_params(("arbitrary", "arbitrary")),
    )(qt, k, v, *carried.args)
    return res[0], res[1], res[2:]


def _a_attn_bwd(qt, k, v, dot, ot, lse, cargo, *, name):
    nh, _, s = qt.shape
    nkv = k.shape[0]
    rep = nh // nkv
    tq, ck = min(A_TQ_BWD, s), min(A_KEY_CHUNK, s)
    grid = (nh, s // tq)

    def body(qt_ref, k_ref, v_ref, dot_ref, ot_ref, lse_ref, dq_ref, dk_ref, dv_ref):
        h, i = pl.program_id(0), pl.program_id(1)

        @pl.when((h % rep == 0) & (i == 0))
        def _():
            dk_ref[...] = jnp.zeros_like(dk_ref)
            dv_ref[...] = jnp.zeros_like(dv_ref)

        q_t, do_t, lse_v = qt_ref[0], dot_ref[0], lse_ref[0]
        delta = jnp.sum(do_t.astype(F32) * ot_ref[0].astype(F32), axis=0, keepdims=True)
        nt = (((1,), (1,)), ((), ()))
        dq = jnp.zeros((HEAD_DIM, tq), F32)
        for c in range(s // ck):
            keys = slice(c * ck, (c + 1) * ck)
            kc = k_ref[0, keys, :]
            p = jnp.exp2(jnp.dot(kc, q_t, preferred_element_type=F32) - lse_v)
            dp = jnp.dot(v_ref[0, keys, :], do_t, preferred_element_type=F32)
            ds = (p * (dp - delta)).astype(BF16)
            dv_ref[0, :, keys] += lax.dot_general(do_t, p.astype(BF16), nt, preferred_element_type=F32)
            dk_ref[0, :, keys] += lax.dot_general(q_t, ds, nt, preferred_element_type=F32)
            dq = dq + lax.dot_general(kc, ds, (((0,), (0,)), ((), ())), preferred_element_type=F32)
        dq_ref[0] = dq

    blk_q = pl.BlockSpec((1, HEAD_DIM, tq), lambda h, i: (h, 0, i))
    blk_row = pl.BlockSpec((1, 1, tq), lambda h, i: (h, 0, i))
    blk_kv = pl.BlockSpec((1, s, HEAD_DIM), lambda h, i: (h // rep, 0, 0))
    blk_acc = pl.BlockSpec((1, HEAD_DIM, s), lambda h, i: (h // rep, 0, 0))
    carried = _carry(cargo, grid, 6, 3, body)
    res = pl.pallas_call(
        carried.body, name=name, grid=grid,
        in_specs=[blk_q, blk_kv, blk_kv, blk_q, blk_q, blk_row] + carried.in_specs,
        out_specs=[blk_q, blk_acc, blk_acc] + carried.out_specs,
        out_shape=[jax.ShapeDtypeStruct((nh, HEAD_DIM, s), F32), jax.ShapeDtypeStruct((nkv, HEAD_DIM, s), F32),
                   jax.ShapeDtypeStruct((nkv, HEAD_DIM, s), F32)] + carried.out_shape,
        scratch_shapes=carried.scratch,
        compiler_params=_params(("arbitrary", "arbitrary")),
    )(qt, k, v, dot, ot, lse, *carried.args)
    return res[0], res[1], res[2], res[3:]


WIN_FAR = 1e30


def _win_penalty(i, tq, tk, window, dil, seg):
    qpos = i * tq + lax.broadcasted_iota(jnp.int32, (tk, tq), 1)
    kpos = i * tq - WIN_PAD + lax.broadcasted_iota(jnp.int32, (tk, tq), 0)
    dist = jnp.abs(kpos - qpos)
    seg_lo = qpos - (qpos & (seg - 1))
    valid = (dist <= window) & (kpos >= seg_lo) & (kpos < seg_lo + seg)
    return jnp.where(valid, (dist * dil).astype(F32), WIN_FAR)


def _win_scores(kw_t, q_t, slope, pen):
    st = lax.dot_general(kw_t, q_t, (((0,), (0,)), ((), ())), preferred_element_type=F32)
    return st * (QK_SCALE * LOG2E) - (slope * LOG2E) * pen


def _win_tq(s):
    return min(512, s)


def _win_fwd(qt, ktp, vtp, slopes, sinks, *, window, dil, seg, out_dtype, name):
    nh, _, s = qt.shape
    nkv = ktp.shape[0]
    rep = nh // nkv
    tq = _win_tq(s)
    tk = tq + 2 * WIN_PAD
    sp = s + 2 * WIN_PAD

    def body(*refs):
        qt_ref, kt_ref, vt_ref, sl_ref = refs[:4]
        o_ref, lse_ref, pen_ref = refs[-3:]
        i, kv = pl.program_id(0), pl.program_id(1)

        @pl.when(kv == 0)
        def _():
            pen_ref[...] = _win_penalty(i, tq, tk, window, dil, seg)

        win = pl.ds(pl.multiple_of(i * tq, tq), tk)
        kw_t, vw_t, pen = kt_ref[0, :, win], vt_ref[0, :, win], pen_ref[...]
        for g in range(rep):
            st = _win_scores(kw_t, qt_ref[g], sl_ref[g][:, :1], pen)
            mx = jnp.max(st, axis=0, keepdims=True)
            if sinks is not None:
                sink = refs[4][g][:, :1] * LOG2E
                mx = jnp.maximum(mx, sink)
            p = jnp.exp2(st - mx)
            den = jnp.sum(p, axis=0, keepdims=True)
            if sinks is not None:
                den = den + jnp.exp2(sink - mx)
            ot = jnp.dot(vw_t, p.astype(BF16), preferred_element_type=F32)
            o_ref[g] = (ot / den).astype(o_ref.dtype)
            lse_ref[g] = mx * LN2 + jnp.log(den)

    blk_q = pl.BlockSpec((rep, HEAD_DIM, tq), lambda i, kv: (kv, 0, i))
    blk_kv = pl.BlockSpec((1, HEAD_DIM, sp), lambda i, kv: (kv, 0, 0))
    blk_h = pl.BlockSpec((rep, 1, LANES), lambda i, kv: (kv, 0, 0))
    in_specs, args = [blk_q, blk_kv, blk_kv, blk_h], [qt, ktp, vtp, slopes]
    if sinks is not None:
        in_specs.append(blk_h)
        args.append(sinks)
    return pl.pallas_call(
        body, name=name, grid=(s // tq, nkv), in_specs=in_specs,
        out_specs=[blk_q, pl.BlockSpec((rep, 1, tq), lambda i, kv: (kv, 0, i))],
        out_shape=[jax.ShapeDtypeStruct((nh, HEAD_DIM, s), out_dtype), jax.ShapeDtypeStruct((nh, 1, s), F32)],
        scratch_shapes=[pltpu.VMEM((tk, tq), F32)],
        compiler_params=_params(("arbitrary", "arbitrary")),
    )(*args)


def _win_bwd(qt, ktp, vtp, slopes, sinks, dot, ot, delta, *, window, dil, seg, name):
    nh, _, s = qt.shape
    nkv = ktp.shape[0]
    rep = nh // nkv
    tq = _win_tq(s)
    tk = tq + 2 * WIN_PAD
    sp = s + 2 * WIN_PAD
    n_in = 6 + (sinks is not None)

    def body(*refs):
        qt_ref, kt_ref, vt_ref, sl_ref, dot_ref, aux_ref = refs[:6]
        outs, pen_ref = refs[n_in:-1], refs[-1]
        dq_ref, dk_ref, dv_ref = outs[:3]
        i, kv = pl.program_id(0), pl.program_id(1)

        @pl.when((i == 0) & (kv == 0))
        def _():
            dk_ref[...] = jnp.zeros_like(dk_ref)
            dv_ref[...] = jnp.zeros_like(dv_ref)
            if sinks is not None:
                outs[3][...] = jnp.zeros_like(outs[3])

        @pl.when(kv == 0)
        def _():
            pen_ref[...] = _win_penalty(i, tq, tk, window, dil, seg)

        win = pl.ds(pl.multiple_of(i * tq, tq), tk)
        kw_t, vw_t, pen = kt_ref[0, :, win], vt_ref[0, :, win], pen_ref[...]
        nt = (((1,), (1,)), ((), ()))
        dk_acc = jnp.zeros((HEAD_DIM, tk), F32)
        dv_acc = jnp.zeros((HEAD_DIM, tk), F32)
        for g in range(rep):
            q_t, do_t = qt_ref[g], dot_ref[g]
            st = _win_scores(kw_t, q_t, sl_ref[g][:, :1], pen)
            mx = jnp.max(st, axis=0, keepdims=True)
            if sinks is not None:
                sink = refs[6][g][:, :1] * LOG2E
                mx = jnp.maximum(mx, sink)
            p = jnp.exp2(st - mx)
            den = jnp.sum(p, axis=0, keepdims=True)
            if sinks is not None:
                p_sink = jnp.exp2(sink - mx)
                den = den + p_sink
            p = p / den
            dp = lax.dot_general(vw_t, do_t, (((0,), (0,)), ((), ())), preferred_element_type=F32)
            if delta is None:
                row = jnp.sum(do_t.astype(F32) * aux_ref[g].astype(F32), axis=0, keepdims=True)
            else:
                row = aux_ref[g]
            ds = (p * (dp - row) * QK_SCALE).astype(BF16)
            dv_acc = dv_acc + lax.dot_general(do_t, p.astype(BF16), nt, preferred_element_type=F32)
            dk_acc = dk_acc + lax.dot_general(q_t, ds, nt, preferred_element_type=F32)
            dq_ref[g] = jnp.dot(kw_t, ds, preferred_element_type=F32)
            if sinks is not None:
                outs[3][kv * rep + g] += (jnp.zeros((1, LANES), F32)
                                          - jnp.sum(p_sink / den * row, axis=1, keepdims=True))
        dv_ref[kv, :, win] += dv_acc
        dk_ref[kv, :, win] += dk_acc

    blk_q = pl.BlockSpec((rep, HEAD_DIM, tq), lambda i, kv: (kv, 0, i))
    blk_row = pl.BlockSpec((rep, 1, tq), lambda i, kv: (kv, 0, i))
    blk_kv = pl.BlockSpec((1, HEAD_DIM, sp), lambda i, kv: (kv, 0, 0))
    blk_acc = pl.BlockSpec((nkv, HEAD_DIM, sp), lambda i, kv: (0, 0, 0))
    blk_h = pl.BlockSpec((rep, 1, LANES), lambda i, kv: (kv, 0, 0))
    in_specs = [blk_q, blk_kv, blk_kv, blk_h, blk_q, blk_q if delta is None else blk_row]
    args = [qt, ktp, vtp, slopes, dot, ot if delta is None else delta]
    out_specs = [blk_q, blk_acc, blk_acc]
    out_shape = [jax.ShapeDtypeStruct((nh, HEAD_DIM, s), F32), jax.ShapeDtypeStruct((nkv, HEAD_DIM, sp), F32),
                 jax.ShapeDtypeStruct((nkv, HEAD_DIM, sp), F32)]
    if sinks is not None:
        in_specs.append(blk_h)
        args.append(sinks)
        out_specs.append(pl.BlockSpec((nh, 1, LANES), lambda i, h: (0, 0, 0)))
        out_shape.append(jax.ShapeDtypeStruct((nh, 1, LANES), F32))
    res = pl.pallas_call(
        body, name=name, grid=(s // tq, nkv), in_specs=in_specs, out_specs=out_specs, out_shape=out_shape,
        scratch_shapes=[pltpu.VMEM((tk, tq), F32)],
        compiler_params=_params(("arbitrary", "arbitrary")),
    )(*args)
    return res if sinks is not None else (*res, None)


def _group_weights(lse):
    e = jnp.exp(lse - jnp.max(lse, axis=0, keepdims=True))
    return e / jnp.sum(e, axis=0, keepdims=True)


def _b_combine_fwd(ot, lse):
    nh, _, s = ot.shape
    ng, hg, _ = lse.shape
    ts = min(512, s)

    def body(ot_ref, lse_ref, o_ref):
        alpha = _group_weights(lse_ref[...])
        for g in range(ng):
            for j in range(hg):
                o_ref[g * hg + j] = (ot_ref[g * hg + j] * alpha[g, j:j + 1, :]).astype(BF16)

    return pl.pallas_call(
        body, name="b_combine_fwd", grid=(s // ts,),
        in_specs=[pl.BlockSpec((nh, HEAD_DIM, ts), lambda i: (0, 0, i)), pl.BlockSpec((ng, hg, ts), lambda i: (0, 0, i))],
        out_specs=pl.BlockSpec((nh, HEAD_DIM, ts), lambda i: (0, 0, i)),
        out_shape=jax.ShapeDtypeStruct((nh, HEAD_DIM, s), BF16),
        compiler_params=_params(("parallel",)),
    )(ot, lse)


def _b_combine_bwd(dout, ot, lse):
    nh, _, s = ot.shape
    ng, hg, _ = lse.shape
    ts = min(512, s)

    def body(dout_ref, ot_ref, lse_ref, do_ref, delta_ref):
        alpha = _group_weights(lse_ref[...])
        for j in range(hg):
            e = [jnp.sum(dout_ref[g * hg + j].astype(F32) * ot_ref[g * hg + j], axis=0, keepdims=True)
                 for g in range(ng)]
            a = [alpha[g, j:j + 1, :] for g in range(ng)]
            mix = a[0] * e[0]
            for g in range(1, ng):
                mix = mix + a[g] * e[g]
            for g in range(ng):
                do_ref[g * hg + j] = (dout_ref[g * hg + j].astype(F32) * a[g]).astype(BF16)
                delta_ref[g * hg + j] = a[g] * mix

    blk = pl.BlockSpec((nh, HEAD_DIM, ts), lambda i: (0, 0, i))
    return pl.pallas_call(
        body, name="b_combine_bwd", grid=(s // ts,),
        in_specs=[blk, blk, pl.BlockSpec((ng, hg, ts), lambda i: (0, 0, i))],
        out_specs=[blk, pl.BlockSpec((nh, 1, ts), lambda i: (0, 0, i))],
        out_shape=[jax.ShapeDtypeStruct((nh, HEAD_DIM, s), BF16), jax.ShapeDtypeStruct((nh, 1, s), F32)],
        compiler_params=_params(("parallel",)),
    )(dout, ot, lse)


def _alibi_slopes(n):
    return 2.0 ** (-8.0 * jnp.arange(1, n + 1, dtype=F32) / n)


def _per_head(v):
    return jnp.broadcast_to(v.astype(F32)[:, None, None], (v.shape[0], 1, LANES))


def _dilate(x, dil):
    if dil == 1:
        return x
    s = x.shape[-1]
    return jnp.swapaxes(x.reshape(x.shape[:-1] + (s // dil, dil)), -1, -2).reshape(x.shape)


def _undilate(x, dil):
    if dil == 1:
        return x
    s = x.shape[-1]
    return jnp.swapaxes(x.reshape(x.shape[:-1] + (dil, s // dil)), -1, -2).reshape(x.shape)


def _heads(x_t):
    return x_t.reshape(-1, HEAD_DIM, x_t.shape[-1])


def _pad_tokens(xt):
    return jnp.pad(xt, ((0, 0), (0, 0), (WIN_PAD, WIN_PAD)))


def _unpad_rows(xt):
    return xt[:, :, WIN_PAD:-WIN_PAD].reshape(-1, xt.shape[-1] - 2 * WIN_PAD)


def _mixer_fwd(kind, qkv, p, tabs, cargo, layer):
    s = qkv.shape[0 if kind == 0 else 1]
    if kind == 0:
        qt, k, v = _a_prep(qkv, tabs[0], tabs[1], p["gq2"], p["gk2"])
        ot, lse, brought = _a_attn_fwd(qt, k, v, cargo, name=f"a_attn_fwd_l{layer}")
        return ot.reshape(-1, s), dict(qt=qt, k=k, v=v, ot=ot, lse=lse), brought
    assert cargo is None
    if kind == 2:
        nq, nk = C_HEADS * HEAD_DIM, C_KV * HEAD_DIM
        qt = _heads(qkv[:nq])
        kp = _pad_tokens(_heads(qkv[nq:nq + nk]))
        vp = _pad_tokens(_heads(qkv[nq + nk:]))
        ot, _ = _win_fwd(qt, kp, vp, p["slopes"], p["sinks"], window=C_WINDOW, dil=1, seg=s, out_dtype=BF16,
                         name="c_attn_fwd")
        return ot.reshape(-1, s), dict(qt=qt, kp=kp, vp=vp, ot=ot), ()
    ng, hg, kg = len(B_GROUPS), B_HEADS_PER_GROUP, B_KV_PER_GROUP
    nq, nk = ng * hg * HEAD_DIM, ng * kg * HEAD_DIM
    qt_all, kt_all, vt_all = _heads(qkv[:nq]), _heads(qkv[nq:nq + nk]), _heads(qkv[nq + nk:])
    saved, outs, lses = [], [], []
    for g, (window, dil) in enumerate(B_GROUPS):
        qt = _dilate(qt_all[g * hg:(g + 1) * hg], dil)
        kp = _pad_tokens(_dilate(kt_all[g * kg:(g + 1) * kg], dil))
        vp = _pad_tokens(_dilate(vt_all[g * kg:(g + 1) * kg], dil))
        sl = p["slopes"][g * hg:(g + 1) * hg]
        ot, lse = _win_fwd(qt, kp, vp, sl, None, window=window // 2 // dil, dil=dil, seg=s // dil, out_dtype=F32,
                           name=f"b_attn_fwd_g{g}")
        saved.append(dict(qt=qt, kp=kp, vp=vp))
        outs.append(_undilate(ot, dil))
        lses.append(_undilate(lse[:, 0, :], dil))
    ot_all, lse_all = jnp.concatenate(outs, axis=0), jnp.stack(lses, axis=0)
    mixed = _b_combine_fwd(ot_all, lse_all)
    return mixed.reshape(-1, s), dict(groups=saved, ot=ot_all, lse=lse_all), ()


def _mixer_bwd(kind, do_t, qkv, sv, p, tabs, cargo, layer):
    s = do_t.shape[1]
    do_heads = _heads(do_t)
    small = {}
    if kind == 0:
        dqt, dkt, dvt, brought = _a_attn_bwd(sv["qt"], sv["k"], sv["v"], do_heads, sv["ot"], sv["lse"],
                                             cargo, name=f"a_attn_bwd_l{layer}")
        dqkv, dgq, dgk = _a_prep_bwd(dqt, dkt, dvt, qkv, tabs[0], tabs[1], p["gq2"], p["gk2"])
        small["q_gain"] = dgq[0, :HEAD_DIM] + dgq[0, HEAD_DIM:]
        small["k_gain"] = dgk[0, :HEAD_DIM] + dgk[0, HEAD_DIM:]
        return dqkv, small, brought
    assert cargo is None
    if kind == 2:
        dqt, dkt, dvt, dsink = _win_bwd(sv["qt"], sv["kp"], sv["vp"], p["slopes"], p["sinks"], do_heads,
                                        sv["ot"], None, window=C_WINDOW, dil=1, seg=s, name="c_attn_bwd")
        small["sinks"] = dsink[:, 0, 0]
        parts = [dqt.reshape(-1, s), _unpad_rows(dkt), _unpad_rows(dvt)]
        return jnp.concatenate(parts, axis=0).astype(BF16), small, ()
    ng, hg, kg = len(B_GROUPS), B_HEADS_PER_GROUP, B_KV_PER_GROUP
    do_own, delta = _b_combine_bwd(do_heads, sv["ot"], sv["lse"])
    dqs, dks, dvs = [], [], []
    for g, (window, dil) in enumerate(B_GROUPS):
        gs = sv["groups"][g]
        dqt, dkt, dvt, _ = _win_bwd(gs["qt"], gs["kp"], gs["vp"], p["slopes"][g * hg:(g + 1) * hg], None,
                                    _dilate(do_own[g * hg:(g + 1) * hg], dil), None,
                                    _dilate(delta[g * hg:(g + 1) * hg], dil),
                                    window=window // 2 // dil, dil=dil, seg=s // dil, name=f"b_attn_bwd_g{g}")
        dqs.append(_undilate(dqt, dil))
        dks.append(_undilate(dkt[:, :, WIN_PAD:-WIN_PAD], dil))
        dvs.append(_undilate(dvt[:, :, WIN_PAD:-WIN_PAD], dil))
    parts = [x.reshape(-1, s) for x in dqs + dks + dvs]
    return jnp.concatenate(parts, axis=0).astype(BF16), small, ()


LAYER_MATS = ("w_qkv", "w_o", "w1", "w2")
COLUMN_QUARTERS = ("w_qkv", "w1")


def _whole(key, gathered):
    q, r, c = gathered.shape
    if key == "w1":
        return gathered
    if key in COLUMN_QUARTERS:
        return jnp.transpose(gathered, (1, 0, 2)).reshape(r, q * c)
    return gathered.reshape(q * r, c)


def _quarters(key, g):
    r, c = g.shape
    if key in COLUMN_QUARTERS:
        return jnp.transpose(g.reshape(r, 4, c // 4), (1, 0, 2))
    return g.reshape(4, r // 4, c)


def _local_step(x, target, norms, mixer_params, shards, whole=None):
    s = x.shape[0]
    tabs = _rope_tables(s)
    if whole is None:
        assert MIXER_OF_LAYER[0][0] == 0
        first = _run_cargo(_gather_cargo([shards[0][key] for key in LAYER_MATS]), name="gather_l0")
        mats = {0: {key: _whole(key, g) for key, g in zip(LAYER_MATS, first)}}
        later = _gather_cargo([shards[layer][key] for layer in range(1, DEPTH) for key in LAYER_MATS])
    else:
        mats, later = dict(enumerate(whole)), None
    h = x
    saved = []
    for layer in range(DEPTH):
        kind = layer % N_MIXERS
        w, p = mats[layer], mixer_params[layer]
        hn, qkv = _norm_mm(h, norms["attn"][layer][None], w["w_qkv"], out_dtype=F32 if kind == 0 else BF16,
                           relu2=False, transpose_out=kind != 0, name=f"qkv_proj_l{layer}")
        o_t, sv, brought = _mixer_fwd(kind, qkv, p, tabs, later if layer == 0 else None, layer)
        for n, g in enumerate(brought):
            mats.setdefault(1 + n // len(LAYER_MATS), {})[LAYER_MATS[n % len(LAYER_MATS)]] = _whole(
                LAYER_MATS[n % len(LAYER_MATS)], g)
        h_mid = _mm_res(o_t, w["w_o"], h, a_transposed=True, name=f"o_proj_l{layer}")
        hn2, act = _norm_mm(h_mid, norms["mlp"][layer][None], w["w1"], out_dtype=BF16, relu2=True,
                            transpose_out=False, name=f"mlp_up_l{layer}")
        h_out = _mm_res(act, w["w2"], h_mid, a_transposed=False, name=f"mlp_down_l{layer}")
        saved.append(dict(h=h, hn=hn, qkv=qkv, o_t=o_t, mix=sv, h_mid=h_mid, hn2=hn2, act=act))
        h = h_out

    dh, loss, d_final = _loss_head(h, norms["final"][None], target)

    own, received, pending = {}, {}, []
    d_attn, d_mlp, small = [None] * DEPTH, [None] * DEPTH, [None] * DEPTH
    for layer in reversed(range(DEPTH)):
        kind = layer % N_MIXERS
        w, p, sv = mats[layer], mixer_params[layer], saved[layer]
        du = _mm_nt(dh, w["w2"], sv["act"], transpose_out=False, name=f"mlp_down_bwd_l{layer}")
        own[layer, "w2"] = _quarters("w2", _mm_tn(sv["act"], dh, x_transposed=False, g_transposed=False,
                                                  column_quarters=False, name=f"mlp_w2_grad_l{layer}"))
        own[layer, "w1"] = _mm_tn(sv["hn2"], du, x_transposed=False, g_transposed=False, column_quarters=True,
                                  name=f"mlp_w1_grad_l{layer}")
        dh_mid, d_mlp[layer] = _mm_nt_normbwd(du, w["w1"], sv["h_mid"], norms["mlp"][layer][None], dh,
                                              g_transposed=False, name=f"mlp_up_bwd_l{layer}")
        do_t = _mm_nt(dh_mid, w["w_o"], None, transpose_out=True, name=f"o_proj_bwd_l{layer}")
        own[layer, "w_o"] = _quarters("w_o", _mm_tn(sv["o_t"], dh_mid, x_transposed=True, g_transposed=False,
                                                    column_quarters=False, name=f"w_o_grad_l{layer}"))
        pending += [(layer, "w2"), (layer, "w1"), (layer, "w_o")]
        cargo = None
        if kind == 0 and whole is None:
            cargo, sent, pending = _scatter_cargo([own[item] for item in pending], None), pending, []
        dqkv, small[layer], brought = _mixer_bwd(kind, do_t, sv["qkv"], sv["mix"], p, tabs, cargo, layer)
        if cargo is not None:
            received.update(zip(sent, brought))
        own[layer, "w_qkv"] = _quarters("w_qkv", _mm_tn(sv["hn"], dqkv, x_transposed=False, g_transposed=kind != 0,
                                                        column_quarters=False, name=f"w_qkv_grad_l{layer}"))
        pending.append((layer, "w_qkv"))
        dh, d_attn[layer] = _mm_nt_normbwd(dqkv, w["w_qkv"], sv["h"], norms["attn"][layer][None], dh_mid,
                                           g_transposed=kind != 0, name=f"qkv_proj_bwd_l{layer}")
    return loss, dh, own, received, pending, dict(attn=d_attn, mlp=d_mlp, final=d_final, mixer=small)


CHIP_FLIPS = ((1, 0), (0, 1), (1, 1))


class _Cargo(NamedTuple):
    ins: tuple
    out_shape: tuple
    sem_shapes: tuple
    start: Callable
    wait: Callable


class _Carried(NamedTuple):
    body: Callable
    in_specs: list
    out_specs: list
    out_shape: list
    scratch: list
    args: tuple


def _carry(cargo, grid, n_in, n_out, body):
    if cargo is None:
        return _Carried(body, [], [], [], [], ())
    ci, co = len(cargo.ins), len(cargo.out_shape)

    def wrapped(*refs):
        ins, c_ins = refs[:n_in], refs[n_in:n_in + ci]
        outs, c_outs = refs[n_in + ci:n_in + ci + n_out], refs[n_in + ci + n_out:n_in + ci + n_out + co]
        sems = refs[n_in + ci + n_out + co:]
        first = last = None
        for axis, extent in enumerate(grid):
            at = pl.program_id(axis)
            first = (at == 0) if first is None else first & (at == 0)
            last = (at == extent - 1) if last is None else last & (at == extent - 1)

        @pl.when(first)
        def _():
            cargo.start(c_ins, c_outs, sems)

        body(*ins, *outs)

        @pl.when(last)
        def _():
            cargo.wait(c_ins, c_outs, sems)

    return _Carried(wrapped, [ANY] * ci, [ANY] * co, list(cargo.out_shape), list(cargo.sem_shapes), tuple(cargo.ins))


def _run_cargo(cargo, *, name):
    ci, co = len(cargo.ins), len(cargo.out_shape)

    def body(*refs):
        cargo.start(refs[:ci], refs[ci:ci + co], refs[ci + co:])
        cargo.wait(refs[:ci], refs[ci:ci + co], refs[ci + co:])

    return pl.pallas_call(body, name=name, in_specs=[ANY] * ci, out_specs=[ANY] * co, out_shape=list(cargo.out_shape),
                          scratch_shapes=list(cargo.sem_shapes))(*cargo.ins)


def _other_chip(x, y, j):
    fx, fy = CHIP_FLIPS[j]
    return (1 - x if fx else x), (1 - y if fy else y)


def _gather_cargo(shards):
    n = len(shards)
    halves = [a.shape[0] // 2 for a in shards]

    def copies(ins, outs, sems):
        ici_send, ici_recv, d2d_send, d2d_recv, local_sems = sems
        x, y, c = lax.axis_index("x"), lax.axis_index("y"), lax.axis_index("c")
        me = 2 * x + y

        def half(t, which):
            return pl.ds(pl.multiple_of(which * halves[t], 16), halves[t])

        def over_ici(t, j, arriving):
            px, py = _other_chip(x, y, j)
            return pltpu.make_async_remote_copy(
                src_ref=ins[t].at[half(t, c)], dst_ref=outs[t].at[2 * px + py if arriving else me, half(t, c)],
                send_sem=ici_send.at[t, j], recv_sem=ici_recv.at[t, j], device_id=(px, py, c), device_id_type=MESH)

        def over_d2d(t, j, arriving):
            px, py = _other_chip(x, y, j)
            mine = outs[t].at[2 * px + py, half(t, c)]
            return pltpu.make_async_remote_copy(
                src_ref=mine, dst_ref=outs[t].at[2 * px + py, half(t, 1 - c)] if arriving else mine,
                send_sem=d2d_send.at[t, j], recv_sem=d2d_recv.at[t, j], device_id=(x, y, 1 - c), device_id_type=MESH)

        return over_ici, over_d2d, lambda t: pltpu.make_async_copy(ins[t], outs[t].at[me], local_sems.at[t])

    def start(ins, outs, sems):
        over_ici, _, own = copies(ins, outs, sems)
        for t in range(n):
            own(t).start()
            for j in range(len(CHIP_FLIPS)):
                over_ici(t, j, False).start()

    def wait(ins, outs, sems):
        over_ici, over_d2d, own = copies(ins, outs, sems)
        for t in range(n):
            for j in range(len(CHIP_FLIPS)):
                over_ici(t, j, True).wait_recv()
                over_d2d(t, j, False).start()
        for t in range(n):
            for j in range(len(CHIP_FLIPS)):
                over_d2d(t, j, True).wait_recv()
                over_d2d(t, j, False).wait_send()
                over_ici(t, j, False).wait_send()
            own(t).wait()

    dma = pltpu.SemaphoreType.DMA
    return _Cargo(tuple(shards), tuple(jax.ShapeDtypeStruct((4,) + a.shape, a.dtype) for a in shards),
                  (dma((n, 3)), dma((n, 3)), dma((n, 3)), dma((n, 3)), dma((n,))), start, wait)


def _scatter_cargo(grads, small):
    n = len(grads)

    def copies(ins, outs, sems):
        x, y, c = lax.axis_index("x"), lax.axis_index("y"), lax.axis_index("c")
        me = 4 * x + 2 * y + c

        def remote(t, j):
            px, py = _other_chip(x, y, j)
            return pltpu.make_async_remote_copy(
                src_ref=ins[t].at[2 * px + py], dst_ref=outs[t].at[j], send_sem=sems[0].at[t, j],
                recv_sem=sems[1].at[t, j], device_id=(px, py, c), device_id_type=MESH)

        def small_remote(r, arriving):
            fx, fy, fc = (r + 1) // 4, ((r + 1) // 2) % 2, (r + 1) % 2
            px, py, pc = (1 - x if fx else x), (1 - y if fy else y), (1 - c if fc else c)
            return pltpu.make_async_remote_copy(
                src_ref=ins[n], dst_ref=outs[n].at[4 * px + 2 * py + pc if arriving else me],
                send_sem=sems[2].at[r], recv_sem=sems[3].at[r], device_id=(px, py, pc), device_id_type=MESH)

        return remote, small_remote, lambda: pltpu.make_async_copy(ins[n], outs[n].at[me], sems[4])

    def start(ins, outs, sems):
        remote, small_remote, small_own = copies(ins, outs, sems)
        if small is not None:
            small_own().start()
            for r in range(7):
                small_remote(r, False).start()
        for t in range(n):
            for j in range(len(CHIP_FLIPS)):
                remote(t, j).start()

    def wait(ins, outs, sems):
        remote, small_remote, small_own = copies(ins, outs, sems)
        if small is not None:
            for r in range(7):
                small_remote(r, True).wait_recv()
                small_remote(r, False).wait_send()
            small_own().wait()
        for t in range(n):
            for j in range(len(CHIP_FLIPS)):
                remote(t, j).wait()

    dma = pltpu.SemaphoreType.DMA
    ins = tuple(grads) + (() if small is None else (small,))
    out_shape = tuple(jax.ShapeDtypeStruct((3,) + g.shape[1:], g.dtype) for g in grads)
    sem_shapes = (dma((n, 3)), dma((n, 3)))
    if small is not None:
        out_shape += (jax.ShapeDtypeStruct((8,) + small.shape, small.dtype),)
        sem_shapes += (dma((7,)), dma((7,)), dma(()))
    return _Cargo(ins, out_shape, sem_shapes, start, wait)


def _swap_cores(parts):
    n = len(parts)

    def body(*refs):
        ins, outs = refs[:n], refs[n:2 * n]
        send_sems, recv_sems = refs[2 * n:]
        peer = (lax.axis_index("x"), lax.axis_index("y"), 1 - lax.axis_index("c"))
        copies = [pltpu.make_async_remote_copy(src_ref=ins[t], dst_ref=outs[t], send_sem=send_sems.at[t],
                                               recv_sem=recv_sems.at[t], device_id=peer, device_id_type=MESH)
                  for t in range(n)]
        for cp in copies:
            cp.start()
        for cp in copies:
            cp.wait()

    return pl.pallas_call(
        body, name="swap_cores", in_specs=[ANY] * n, out_specs=[ANY] * n,
        out_shape=[jax.ShapeDtypeStruct(a.shape, a.dtype) for a in parts],
        scratch_shapes=[pltpu.SemaphoreType.DMA((n,)), pltpu.SemaphoreType.DMA((n,))],
    )(*parts)


def _rows_tile(r):
    return 256 if r % 256 == 0 else r


def _sum_quarters(own, recv, *, name):
    r, c = own.shape
    tr = _rows_tile(r)

    def body(own_ref, recv_ref, o_ref):
        acc = own_ref[...].astype(F32)
        for j in range(3):
            acc = acc + recv_ref[j].astype(F32)
        o_ref[...] = acc

    return pl.pallas_call(
        body, name=name, grid=(r // tr,),
        in_specs=[pl.BlockSpec((tr, c), lambda i: (i, 0)), pl.BlockSpec((3, tr, c), lambda i: (0, i, 0))],
        out_specs=pl.BlockSpec((tr, c), lambda i: (i, 0)),
        out_shape=jax.ShapeDtypeStruct((r, c), F32),
        compiler_params=_params(("parallel",)),
    )(own, recv)


def _adamw(w, m, v, parts, *, name):
    r, c = w.shape
    tr = _rows_tile(r)
    c1, c2 = 1.0 - ADAM_B1 ** ADAM_STEP, 1.0 - ADAM_B2 ** ADAM_STEP
    n_parts = len(parts)

    def body(*refs):
        w_ref, m_ref, v_ref = refs[:3]
        g_ref, d_ref, nm_ref, nv_ref = refs[3 + n_parts:]
        terms = []
        for p_ref in refs[3:3 + n_parts]:
            terms += [p_ref[...]] if len(p_ref.shape) == 2 else [p_ref[j] for j in range(p_ref.shape[0])]
        g = terms[0]
        for term in terms[1:]:
            g = g + term
        m_new = ADAM_B1 * m_ref[...] + (1.0 - ADAM_B1) * g
        v_new = ADAM_B2 * v_ref[...] + (1.0 - ADAM_B2) * (g * g)
        step = (m_new / c1) / (jnp.sqrt(v_new / c2) + ADAM_EPS)
        g_ref[...] = g
        d_ref[...] = -ADAM_LR * (step + ADAM_WD * w_ref[...])
        nm_ref[...] = m_new
        nv_ref[...] = v_new

    blk = pl.BlockSpec((tr, c), lambda i: (i, 0))
    part_specs = [blk if p.ndim == 2 else pl.BlockSpec((p.shape[0], tr, c), lambda i: (0, i, 0)) for p in parts]
    return pl.pallas_call(
        body, name=name, grid=(r // tr,), in_specs=[blk, blk, blk] + part_specs,
        out_specs=[blk] * 4, out_shape=[jax.ShapeDtypeStruct((r, c), F32)] * 4,
        compiler_params=_params(("parallel",)),
    )(w, m, v, *parts)


MATS = ("a_w_qkv", "a_w_o", "b_w_qkv", "b_w_o", "c_w_qkv", "c_w_o", "mlp_w1", "mlp_w2")
SMALLS = ("attn_norm", "mlp_norm", "a_q_gain", "a_k_gain", "c_sinks", "final_norm")
WEIGHTS = ("attn_norm", "mlp_norm", "a_w_qkv", "a_q_gain", "a_k_gain", "a_w_o", "b_w_qkv", "b_w_o", "c_w_qkv",
           "c_sinks", "c_w_o", "mlp_w1", "mlp_w2", "final_norm")
MIXER_OF_LAYER = tuple((layer % N_MIXERS, sum(1 for q in range(layer) if q % N_MIXERS == layer % N_MIXERS))
                       for layer in range(DEPTH))
SMALL_ROWS = 8


def _pack_small(values):
    rows, spans, at = [], [], 0
    for v in values:
        flat = v.reshape(-1)
        n = -(-flat.shape[0] // (SMALL_ROWS * LANES)) * SMALL_ROWS
        rows.append(jnp.pad(flat, (0, n * LANES - flat.shape[0])).reshape(n, LANES))
        spans.append((at, n))
        at += n
    return jnp.concatenate(rows, axis=0), spans


def kernel(x, attn_norm, mlp_norm, a_w_qkv, a_q_gain, a_k_gain, a_w_o, b_w_qkv, b_w_o, c_w_qkv, c_sinks, c_w_o, mlp_w1, mlp_w2, final_norm, loss_target, m_attn_norm, m_mlp_norm, m_a_w_qkv, m_a_q_gain, m_a_k_gain, m_a_w_o, m_b_w_qkv, m_b_w_o, m_c_w_qkv, m_c_sinks, m_c_w_o, m_mlp_w1, m_mlp_w2, m_final_norm, v_attn_norm, v_mlp_norm, v_a_w_qkv, v_a_q_gain, v_a_k_gain, v_a_w_o, v_b_w_qkv, v_b_w_o, v_c_w_qkv, v_c_sinks, v_c_w_o, v_mlp_w1, v_mlp_w2, v_final_norm):
    env = dict(locals())
    w = {name: env[name] for name in WEIGHTS}
    mom = {name: (env["m_" + name], env["v_" + name]) for name in WEIGHTS}

    prefix = ("a", "b", "c")
    shards, mixer_params = [], []
    for layer, (kind, j) in enumerate(MIXER_OF_LAYER):
        shards.append(dict(w_qkv=w[prefix[kind] + "_w_qkv"][j].astype(BF16), w_o=w[prefix[kind] + "_w_o"][j].astype(BF16),
                           w1=mlp_w1[layer].astype(BF16), w2=mlp_w2[layer].astype(BF16)))
        if kind == 0:
            mixer_params.append(dict(gq2=jnp.tile(a_q_gain[j], 2)[None], gk2=jnp.tile(a_k_gain[j], 2)[None]))
        elif kind == 1:
            mixer_params.append(dict(slopes=_per_head(_alibi_slopes(len(B_GROUPS) * B_HEADS_PER_GROUP))))
        else:
            mixer_params.append(dict(slopes=_per_head(_alibi_slopes(C_HEADS)), sinks=_per_head(c_sinks[j])))

    norms = dict(attn=attn_norm, mlp=mlp_norm, final=final_norm)
    loss_part, grad_x, own, received, pending, g_small = _local_step(x[0], loss_target[0], norms, mixer_params, shards)
    loss = lax.psum(loss_part[0, 0], ("x", "y", "c"))

    of_kind = lambda kind, key: jnp.stack([g_small["mixer"][layer][key] for layer, (k, _) in enumerate(MIXER_OF_LAYER)
                                           if k == kind])
    small_grads = dict(
        attn_norm=jnp.concatenate(g_small["attn"], axis=0), mlp_norm=jnp.concatenate(g_small["mlp"], axis=0),
        a_q_gain=of_kind(0, "q_gain"), a_k_gain=of_kind(0, "k_gain"), c_sinks=of_kind(2, "sinks"),
        final_norm=g_small["final"][0])
    packed, spans = _pack_small([small_grads[name] for name in SMALLS])
    *last, all_small = _run_cargo(_scatter_cargo([own[item] for item in pending], packed), name="scatter_last")
    received.update(zip(pending, last))

    me_chip = 2 * lax.axis_index("x") + lax.axis_index("y")
    partial = []
    for name in MATS:
        key = name[2:] if name[0] in "abc" else name[4:]
        layers = [layer for layer, (kind, _) in enumerate(MIXER_OF_LAYER)
                  if name.startswith("mlp") or prefix[kind] == name[0]]
        sums = [_sum_quarters(lax.dynamic_index_in_dim(own[layer, key], me_chip, axis=0, keepdims=False),
                              received[layer, key], name=f"sum_{name}_l{layer}") for layer in layers]
        partial.append(jnp.concatenate(sums, axis=0))
    other = _swap_cores(partial)

    out = {}
    for name, mine, theirs in zip(MATS, partial, other):
        shape = w[name].shape
        res = _adamw(*[a.reshape(-1, shape[-1]) for a in (w[name], *mom[name])], [mine, theirs], name=f"adamw_{name}")
        out[name] = [a.reshape(shape) for a in res]
    for name, (at, n) in zip(SMALLS, spans):
        shape = w[name].shape
        packed_in = [_pack_small([a])[0] for a in (w[name], *mom[name])]
        res = _adamw(*packed_in, [all_small[:, at:at + n]], name=f"adamw_{name}")
        out[name] = [a.reshape(-1)[:w[name].size].reshape(shape) for a in res]

    return (loss, grad_x[None], *[out[name][0] for name in WEIGHTS], *[out[name][1] for name in WEIGHTS],
            *[out[name][2] for name in WEIGHTS], *[out[name][3] for name in WEIGHTS])
```

```python
from typing import Callable, NamedTuple

import jax
import jax.numpy as jnp
from jax import lax
from jax.experimental import pallas as pl
from jax.experimental.pallas import tpu as pltpu

F32 = jnp.float32
BF16 = jnp.bfloat16
MESH = pl.DeviceIdType.MESH
ANY = pl.BlockSpec(memory_space=pl.ANY)

D_MODEL = 1024
HEAD_DIM = 64
GRID_W = 64
ROPE_THETA = 10000.0
RMS_EPS = 1e-6
QK_SCALE = HEAD_DIM ** -0.5
LOG2E = 1.4426950408889634
LN2 = 0.6931471805599453
A_HEADS, A_KV = 16, 4
B_GROUPS = ((128, 1), (512, 4), (2048, 16))
B_HEADS_PER_GROUP, B_KV_PER_GROUP = 6, 2
C_HEADS, C_KV, C_WINDOW = 16, 4, 128
DEPTH, N_MIXERS = 4, 3
ADAM_LR, ADAM_B1, ADAM_B2, ADAM_EPS, ADAM_WD, ADAM_STEP = 0.001, 0.9, 0.999, 1e-08, 0.01, 10

WIN_REACH = 128
V7X_VMEM_BUDGET = 48 * 1024 * 1024
LANES = 128
ROW_TILE = 1024


def _params(semantics):
    return pltpu.CompilerParams(dimension_semantics=semantics, vmem_limit_bytes=V7X_VMEM_BUDGET)


def _tile(n, cap):
    if n <= cap:
        return n
    t = (cap // LANES) * LANES
    while n % t:
        t -= LANES
    return t


def _norm_mm(h, gain, w, *, out_dtype, relu2, transpose_out, name):
    m, d = h.shape
    by_quarter = w.ndim == 3
    assert not (by_quarter and transpose_out)
    n = w.shape[-1] * (4 if by_quarter else 1)
    per_step = 2 if by_quarter and 2 * w.shape[-1] <= 2048 else 1
    tm, tn = min(ROW_TILE, m), (per_step * w.shape[-1] if by_quarter else _tile(n, 2048))
    w_spec = (pl.BlockSpec((per_step, d, tn // per_step), lambda i, j: (j, 0, 0)) if by_quarter
              else pl.BlockSpec((d, tn), lambda i, j: (0, j)))
    y_spec = (pl.BlockSpec((tn, tm), lambda i, j: (j, i)) if transpose_out
              else pl.BlockSpec((tm, tn), lambda i, j: (i, j)))

    def body(h_ref, g_ref, w_ref, hn_ref, y_ref):
        @pl.when(pl.program_id(1) == 0)
        def _():
            x = h_ref[...]
            r = lax.rsqrt(jnp.mean(x * x, axis=-1, keepdims=True) + RMS_EPS)
            hn_ref[...] = (x * r * g_ref[...]).astype(BF16)

        def finish(y):
            if relu2:
                y = jnp.maximum(y, 0.0)
                y = y * y
            return y.astype(y_ref.dtype)

        if transpose_out:
            y_ref[...] = finish(lax.dot_general(w_ref[...], hn_ref[...], (((0,), (1,)), ((), ())),
                                                preferred_element_type=F32))
        elif by_quarter:
            cols = tn // per_step
            for q in range(per_step):
                y_ref[:, q * cols:(q + 1) * cols] = finish(jnp.dot(hn_ref[...], w_ref[q], preferred_element_type=F32))
        else:
            y_ref[...] = finish(jnp.dot(hn_ref[...], w_ref[...], preferred_element_type=F32))

    return pl.pallas_call(
        body, name=name, grid=(m // tm, n // tn),
        in_specs=[pl.BlockSpec((tm, d), lambda i, j: (i, 0)), pl.BlockSpec((1, d), lambda i, j: (0, 0)), w_spec],
        out_specs=[pl.BlockSpec((tm, d), lambda i, j: (i, 0)), y_spec],
        out_shape=[jax.ShapeDtypeStruct((m, d), BF16), jax.ShapeDtypeStruct((n, m) if transpose_out else (m, n), out_dtype)],
        compiler_params=_params(("parallel", "arbitrary")),
    )(h, gain, w)


def _mm_res(a, w, h_in, *, a_transposed, name):
    k, d = w.shape
    m = h_in.shape[0]
    tm, tk = min(ROW_TILE, m), _tile(k, 1152)
    lhs_contracts = 0 if a_transposed else 1

    def body(a_ref, w_ref, h_ref, o_ref):
        @pl.when(pl.program_id(1) == 0)
        def _():
            o_ref[...] = h_ref[...]

        o_ref[...] += lax.dot_general(a_ref[...], w_ref[...], (((lhs_contracts,), (0,)), ((), ())),
                                      preferred_element_type=F32)

    a_spec = (pl.BlockSpec((tk, tm), lambda i, j: (j, i)) if a_transposed
              else pl.BlockSpec((tm, tk), lambda i, j: (i, j)))
    return pl.pallas_call(
        body, name=name, grid=(m // tm, k // tk),
        in_specs=[a_spec, pl.BlockSpec((tk, d), lambda i, j: (j, 0)), pl.BlockSpec((tm, d), lambda i, j: (i, 0))],
        out_specs=pl.BlockSpec((tm, d), lambda i, j: (i, 0)),
        out_shape=jax.ShapeDtypeStruct((m, d), F32),
        compiler_params=_params(("parallel", "arbitrary")),
    )(a, w, h_in)


def _mm_nt(a, w, act, *, transpose_out, name):
    m, d = a.shape
    n = w.shape[0]
    tm, tn = min(ROW_TILE, m), _tile(n, 1152)
    assert act is None or not transpose_out
    nt = (((1,), (1,)), ((), ()))

    def body(*refs):
        a_ref, w_ref = refs[0], refs[1]
        o_ref = refs[-1]
        if transpose_out:
            acc = lax.dot_general(w_ref[...], a_ref[...].astype(BF16), nt, preferred_element_type=F32)
        else:
            acc = lax.dot_general(a_ref[...].astype(BF16), w_ref[...], nt, preferred_element_type=F32)
        if act is not None:
            acc = acc * (2.0 * jnp.sqrt(refs[2][...].astype(F32)))
        o_ref[...] = acc.astype(BF16)

    in_specs = [pl.BlockSpec((tm, d), lambda i, j: (i, 0)), pl.BlockSpec((tn, d), lambda i, j: (j, 0))]
    args = [a, w]
    if act is not None:
        in_specs.append(pl.BlockSpec((tm, tn), lambda i, j: (i, j)))
        args.append(act)
    out_spec = (pl.BlockSpec((tn, tm), lambda i, j: (j, i)) if transpose_out
                else pl.BlockSpec((tm, tn), lambda i, j: (i, j)))
    return pl.pallas_call(
        body, name=name, grid=(m // tm, n // tn), in_specs=in_specs, out_specs=out_spec,
        out_shape=jax.ShapeDtypeStruct((n, m) if transpose_out else (m, n), BF16),
        compiler_params=_params(("parallel", "parallel")),
    )(*args)


def _rmsnorm_bwd(dn, x, gain):
    r = lax.rsqrt(jnp.mean(x * x, axis=-1, keepdims=True) + RMS_EPS)
    xh = x * r
    dgain = jnp.sum(dn * xh, axis=0, keepdims=True)
    u = dn * gain
    dx = r * (u - xh * jnp.mean(u * xh, axis=-1, keepdims=True))
    return dx, dgain


def _mm_nt_normbwd(g, w, h, gain, dh_in, *, g_transposed, name):
    m, k = g.shape[::-1] if g_transposed else g.shape
    by_quarter = w.ndim == 3
    d = w.shape[-2]
    tm, tk = min(ROW_TILE, m), (w.shape[-1] if by_quarter else _tile(k, 1024))
    sub = min(256, tm)
    nk = k // tk
    w_spec = (pl.BlockSpec((None, d, tk), lambda i, j: (j, 0, 0)) if by_quarter
              else pl.BlockSpec((d, tk), lambda i, j: (0, j)))

    def body(g_ref, w_ref, h_ref, gain_ref, dh_ref, o_ref, dg_ref, acc_ref):
        i, j = pl.program_id(0), pl.program_id(1)

        @pl.when((i == 0) & (j == 0))
        def _():
            dg_ref[...] = jnp.zeros_like(dg_ref)

        @pl.when(j == 0)
        def _():
            acc_ref[...] = jnp.zeros_like(acc_ref)

        acc_ref[...] += lax.dot_general(g_ref[...], w_ref[...], (((0 if g_transposed else 1,), (1,)), ((), ())),
                                        preferred_element_type=F32)

        @pl.when(j == nk - 1)
        def _():
            for r in range(0, tm, sub):
                rows = slice(r, r + sub)
                dx, dgain = _rmsnorm_bwd(acc_ref[rows, :], h_ref[rows, :], gain_ref[...])
                dg_ref[...] += dgain
                o_ref[rows, :] = dh_ref[rows, :] + dx

    return pl.pallas_call(
        body, name=name, grid=(m // tm, nk),
        in_specs=[pl.BlockSpec((tk, tm), lambda i, j: (j, i)) if g_transposed
                  else pl.BlockSpec((tm, tk), lambda i, j: (i, j)), w_spec,
                  pl.BlockSpec((tm, d), lambda i, j: (i, 0)), pl.BlockSpec((1, d), lambda i, j: (0, 0)),
                  pl.BlockSpec((tm, d), lambda i, j: (i, 0))],
        out_specs=[pl.BlockSpec((tm, d), lambda i, j: (i, 0)), pl.BlockSpec((1, d), lambda i, j: (0, 0))],
        out_shape=[jax.ShapeDtypeStruct((m, d), F32), jax.ShapeDtypeStruct((1, d), F32)],
        scratch_shapes=[pltpu.VMEM((tm, d), F32)],
        compiler_params=_params(("arbitrary", "arbitrary")),
    )(g, w, h, gain, dh_in)


def _mm_tn(x, g, *, x_transposed, g_transposed, column_quarters, name):
    k, m = x.shape if x_transposed else x.shape[::-1]
    n = g.shape[0] if g_transposed else g.shape[1]
    tm, tk, tn = min(ROW_TILE, m), _tile(k, 1152), (n // 4 if column_quarters else _tile(n, 1024))
    nm = m // tm
    lhs_contracts, rhs_contracts = (1 if x_transposed else 0), (1 if g_transposed else 0)
    g_spec = (pl.BlockSpec((tn, tm), lambda a, b, s: (b, s)) if g_transposed
              else pl.BlockSpec((tm, tn), lambda a, b, s: (s, b)))

    def body(x_ref, g_ref, o_ref, acc_ref):
        s = pl.program_id(2)

        @pl.when(s == 0)
        def _():
            acc_ref[...] = jnp.zeros_like(acc_ref)

        acc_ref[...] += lax.dot_general(x_ref[...], g_ref[...].astype(BF16),
                                        (((lhs_contracts,), (rhs_contracts,)), ((), ())), preferred_element_type=F32)

        @pl.when(s == nm - 1)
        def _():
            o_ref[...] = acc_ref[...].astype(BF16)

    x_spec = (pl.BlockSpec((tk, tm), lambda a, b, s: (a, s)) if x_transposed
              else pl.BlockSpec((tm, tk), lambda a, b, s: (s, a)))
    out_spec = (pl.BlockSpec((None, tk, tn), lambda a, b, s: (b, a, 0)) if column_quarters
                else pl.BlockSpec((tk, tn), lambda a, b, s: (a, b)))
    return pl.pallas_call(
        body, name=name, grid=(k // tk, n // tn, nm),
        in_specs=[x_spec, g_spec], out_specs=out_spec,
        out_shape=jax.ShapeDtypeStruct((4, k, n // 4) if column_quarters else (k, n), BF16),
        scratch_shapes=[pltpu.VMEM((tk, tn), F32)],
        compiler_params=_params(("parallel", "parallel", "arbitrary")),
    )(x, g)


def _loss_head(h, gain, target):
    m, d = h.shape
    tm = 512

    def body(h_ref, g_ref, t_ref, dh_ref, loss_ref, dg_ref):
        @pl.when(pl.program_id(0) == 0)
        def _():
            loss_ref[...] = jnp.zeros_like(loss_ref)
            dg_ref[...] = jnp.zeros_like(dg_ref)

        x = h_ref[...]
        gain_v = g_ref[...]
        r = lax.rsqrt(jnp.mean(x * x, axis=-1, keepdims=True) + RMS_EPS)
        err = x * r * gain_v - t_ref[...]
        loss_ref[...] += 0.5 * jnp.sum(jnp.mean(err * err, axis=-1, keepdims=True), axis=0, keepdims=True)
        dx, dgain = _rmsnorm_bwd(err * (1.0 / d), x, gain_v)
        dg_ref[...] += dgain
        dh_ref[...] = dx

    return pl.pallas_call(
        body, name="loss_head", grid=(m // tm,),
        in_specs=[pl.BlockSpec((tm, d), lambda i: (i, 0)), pl.BlockSpec((1, d), lambda i: (0, 0)),
                  pl.BlockSpec((tm, d), lambda i: (i, 0))],
        out_specs=[pl.BlockSpec((tm, d), lambda i: (i, 0)), pl.BlockSpec((1, LANES), lambda i: (0, 0)),
                   pl.BlockSpec((1, d), lambda i: (0, 0))],
        out_shape=[jax.ShapeDtypeStruct((m, d), F32), jax.ShapeDtypeStruct((1, LANES), F32),
                   jax.ShapeDtypeStruct((1, d), F32)],
        compiler_params=_params(("arbitrary",)),
    )(h, gain, target)


def _rope_tables(s):
    t = jnp.arange(s)
    row = (t // GRID_W).astype(F32)
    col = (t % GRID_W).astype(F32)
    axis_dim = HEAD_DIM // 2
    inv_freq = ROPE_THETA ** (-jnp.arange(0, axis_dim, 2, dtype=F32) / axis_dim)
    ar, ac = row[:, None] * inv_freq, col[:, None] * inv_freq
    cos = jnp.concatenate([jnp.cos(ar), jnp.cos(ar), jnp.cos(ac), jnp.cos(ac)], axis=-1)
    sin = jnp.concatenate([-jnp.sin(ar), jnp.sin(ar), -jnp.sin(ac), jnp.sin(ac)], axis=-1)
    return jnp.tile(cos, (1, 2)), jnp.tile(sin, (1, 2))


def _swap16(x):
    lane = lax.broadcasted_iota(jnp.int32, x.shape, 1)
    return jnp.where((lane % 32) < 16, pltpu.roll(x, LANES - 16, 1), pltpu.roll(x, 16, 1))


def _head_mean(v):
    lane = lax.broadcasted_iota(jnp.int32, v.shape, 1)
    lo = lane < HEAD_DIM
    s_all = jnp.sum(v, axis=-1, keepdims=True)
    s_lo = jnp.sum(jnp.where(lo, v, 0.0), axis=-1, keepdims=True)
    return jnp.where(lo, s_lo, s_all - s_lo) * (1.0 / HEAD_DIM)


def _norm_rope(x, gain2, cos, sin):
    r = lax.rsqrt(_head_mean(x * x) + RMS_EPS)
    nrm = x * r * gain2
    return nrm * cos + _swap16(nrm) * sin


def _norm_rope_bwd(dy, x, gain2, cos, sin):
    dn = dy * cos + _swap16(dy * sin)
    r = lax.rsqrt(_head_mean(x * x) + RMS_EPS)
    xh = x * r
    dgain = jnp.sum(dn * xh, axis=0, keepdims=True)
    u = dn * gain2
    return r * (u - xh * _head_mean(u * xh)), dgain


def _a_prep(qkv, cos, sin, gq2, gk2):
    s = qkv.shape[0]
    tr = 256
    nq, nk = A_HEADS * HEAD_DIM, A_KV * HEAD_DIM

    def body(qkv_ref, cos_ref, sin_ref, gq_ref, gk_ref, qt_ref, k_ref, v_ref):
        cos_v, sin_v = cos_ref[...], sin_ref[...]
        for c in range(nq // LANES):
            y = _norm_rope(qkv_ref[:, c * LANES:(c + 1) * LANES], gq_ref[...], cos_v, sin_v) * (QK_SCALE * LOG2E)
            yt = y.T
            qt_ref[2 * c] = yt[:HEAD_DIM].astype(BF16)
            qt_ref[2 * c + 1] = yt[HEAD_DIM:].astype(BF16)
        for c in range(nk // LANES):
            y = _norm_rope(qkv_ref[:, nq + c * LANES:nq + (c + 1) * LANES], gk_ref[...], cos_v, sin_v)
            k_ref[2 * c] = y[:, :HEAD_DIM].astype(BF16)
            k_ref[2 * c + 1] = y[:, HEAD_DIM:].astype(BF16)
            x = qkv_ref[:, nq + nk + c * LANES:nq + nk + (c + 1) * LANES]
            v_ref[2 * c] = x[:, :HEAD_DIM].astype(BF16)
            v_ref[2 * c + 1] = x[:, HEAD_DIM:].astype(BF16)

    return pl.pallas_call(
        body, name="a_prep", grid=(s // tr,),
        in_specs=[pl.BlockSpec((tr, nq + 2 * nk), lambda i: (i, 0)), pl.BlockSpec((tr, LANES), lambda i: (i, 0)),
                  pl.BlockSpec((tr, LANES), lambda i: (i, 0)), pl.BlockSpec((1, LANES), lambda i: (0, 0)),
                  pl.BlockSpec((1, LANES), lambda i: (0, 0))],
        out_specs=[pl.BlockSpec((A_HEADS, HEAD_DIM, tr), lambda i: (0, 0, i)),
                   pl.BlockSpec((A_KV, tr, HEAD_DIM), lambda i: (0, i, 0)),
                   pl.BlockSpec((A_KV, tr, HEAD_DIM), lambda i: (0, i, 0))],
        out_shape=[jax.ShapeDtypeStruct((A_HEADS, HEAD_DIM, s), BF16), jax.ShapeDtypeStruct((A_KV, s, HEAD_DIM), BF16),
                   jax.ShapeDtypeStruct((A_KV, s, HEAD_DIM), BF16)],
        compiler_params=_params(("parallel",)),
    )(qkv, cos, sin, gq2, gk2)


def _a_prep_bwd(dqt, dkt, dvt, qkv, cos, sin, gq2, gk2):
    s = qkv.shape[0]
    tr = 256
    nq, nk = A_HEADS * HEAD_DIM, A_KV * HEAD_DIM

    def body(dqt_ref, dkt_ref, dvt_ref, qkv_ref, cos_ref, sin_ref, gq_ref, gk_ref, o_ref, dgq_ref, dgk_ref):
        @pl.when(pl.program_id(0) == 0)
        def _():
            dgq_ref[...] = jnp.zeros_like(dgq_ref)
            dgk_ref[...] = jnp.zeros_like(dgk_ref)

        cos_v, sin_v = cos_ref[...], sin_ref[...]

        def pair(ref, c):
            return jnp.concatenate([ref[2 * c], ref[2 * c + 1]], axis=0).T

        for c in range(nq // LANES):
            dx, dg = _norm_rope_bwd(pair(dqt_ref, c) * QK_SCALE, qkv_ref[:, c * LANES:(c + 1) * LANES],
                                    gq_ref[...], cos_v, sin_v)
            o_ref[:, c * LANES:(c + 1) * LANES] = dx.astype(BF16)
            dgq_ref[...] += dg
        for c in range(nk // LANES):
            lo = nq + c * LANES
            dx, dg = _norm_rope_bwd(pair(dkt_ref, c) * LN2, qkv_ref[:, lo:lo + LANES], gk_ref[...], cos_v, sin_v)
            o_ref[:, lo:lo + LANES] = dx.astype(BF16)
            dgk_ref[...] += dg
            o_ref[:, lo + nk:lo + nk + LANES] = pair(dvt_ref, c).astype(BF16)

    return pl.pallas_call(
        body, name="a_prep_bwd", grid=(s // tr,),
        in_specs=[pl.BlockSpec((A_HEADS, HEAD_DIM, tr), lambda i: (0, 0, i)),
                  pl.BlockSpec((A_KV, HEAD_DIM, tr), lambda i: (0, 0, i)),
                  pl.BlockSpec((A_KV, HEAD_DIM, tr), lambda i: (0, 0, i)),
                  pl.BlockSpec((tr, nq + 2 * nk), lambda i: (i, 0)), pl.BlockSpec((tr, LANES), lambda i: (i, 0)),
                  pl.BlockSpec((tr, LANES), lambda i: (i, 0)), pl.BlockSpec((1, LANES), lambda i: (0, 0)),
                  pl.BlockSpec((1, LANES), lambda i: (0, 0))],
        out_specs=[pl.BlockSpec((tr, nq + 2 * nk), lambda i: (i, 0)), pl.BlockSpec((1, LANES), lambda i: (0, 0)),
                   pl.BlockSpec((1, LANES), lambda i: (0, 0))],
        out_shape=[jax.ShapeDtypeStruct((s, nq + 2 * nk), BF16), jax.ShapeDtypeStruct((1, LANES), F32),
                   jax.ShapeDtypeStruct((1, LANES), F32)],
        compiler_params=_params(("arbitrary",)),
    )(dqt, dkt, dvt, qkv, cos, sin, gq2, gk2)


A_TQ = 1024
A_TQ_SUB = 256
A_TQ_BWD = 1024
A_KEY_CHUNK = 512


def _a_attn_fwd(qt, k, v, cargo, *, name):
    nh, _, s = qt.shape
    rep = nh // k.shape[0]
    tq = min(A_TQ, s)
    sub = min(A_TQ_SUB, tq)
    grid = (nh, s // tq)

    def body(qt_ref, k_ref, v_ref, o_ref, lse_ref):
        scores = [jnp.dot(k_ref[0], qt_ref[0, :, a:a + sub], preferred_element_type=F32)
                  for a in range(0, tq, sub)]
        for a, st in zip(range(0, tq, sub), scores):
            mx = jnp.max(st, axis=0, keepdims=True)
            p = jnp.exp2(st - mx)
            den = jnp.sum(p, axis=0, keepdims=True)
            ot = lax.dot_general(v_ref[0], p.astype(BF16), (((0,), (0,)), ((), ())), preferred_element_type=F32)
            o_ref[0, :, a:a + sub] = (ot / den).astype(BF16)
            lse_ref[0, :, a:a + sub] = mx + jnp.log(den) * LOG2E

    carried = _carry(cargo, grid, 3, 2, body)
    res = pl.pallas_call(
        carried.body, name=name, grid=grid,
        in_specs=[pl.BlockSpec((1, HEAD_DIM, tq), lambda h, i: (h, 0, i)),
                  pl.BlockSpec((1, s, HEAD_DIM), lambda h, i: (h // rep, 0, 0)),
                  pl.BlockSpec((1, s, HEAD_DIM), lambda h, i: (h // rep, 0, 0))] + carried.in_specs,
        out_specs=[pl.BlockSpec((1, HEAD_DIM, tq), lambda h, i: (h, 0, i)),
                   pl.BlockSpec((1, 1, tq), lambda h, i: (h, 0, i))] + carried.out_specs,
        out_shape=[jax.ShapeDtypeStruct((nh, HEAD_DIM, s), BF16), jax.ShapeDtypeStruct((nh, 1, s), F32)]
        + carried.out_shape,
        scratch_shapes=carried.scratch,
        compiler_params=_params(("arbitrary", "arbitrary")),
    )(qt, k, v, *carried.args)
    return res[0], res[1], res[2:]


def _a_attn_bwd(qt, k, v, dot, ot, lse, cargo, *, name):
    nh, _, s = qt.shape
    nkv = k.shape[0]
    rep = nh // nkv
    tq, ck = min(A_TQ_BWD, s), min(A_KEY_CHUNK, s)
    grid = (nh, s // tq)

    def body(qt_ref, k_ref, v_ref, dot_ref, ot_ref, lse_ref, dq_ref, dk_ref, dv_ref):
        h, i = pl.program_id(0), pl.program_id(1)

        @pl.when((h % rep == 0) & (i == 0))
        def _():
            dk_ref[...] = jnp.zeros_like(dk_ref)
            dv_ref[...] = jnp.zeros_like(dv_ref)

        q_t, do_t, lse_v = qt_ref[0], dot_ref[0], lse_ref[0]
        delta = jnp.sum(do_t.astype(F32) * ot_ref[0].astype(F32), axis=0, keepdims=True)
        nt = (((1,), (1,)), ((), ()))
        dq = jnp.zeros((HEAD_DIM, tq), F32)
        for c in range(s // ck):
            keys = slice(c * ck, (c + 1) * ck)
            kc = k_ref[0, keys, :]
            p = jnp.exp2(jnp.dot(kc, q_t, preferred_element_type=F32) - lse_v)
            dp = jnp.dot(v_ref[0, keys, :], do_t, preferred_element_type=F32)
            ds = (p * (dp - delta)).astype(BF16)
            dv_ref[0, :, keys] += lax.dot_general(do_t, p.astype(BF16), nt, preferred_element_type=F32)
            dk_ref[0, :, keys] += lax.dot_general(q_t, ds, nt, preferred_element_type=F32)
            dq = dq + lax.dot_general(kc, ds, (((0,), (0,)), ((), ())), preferred_element_type=F32)
        dq_ref[0] = dq

    blk_q = pl.BlockSpec((1, HEAD_DIM, tq), lambda h, i: (h, 0, i))
    blk_row = pl.BlockSpec((1, 1, tq), lambda h, i: (h, 0, i))
    blk_kv = pl.BlockSpec((1, s, HEAD_DIM), lambda h, i: (h // rep, 0, 0))
    blk_acc = pl.BlockSpec((1, HEAD_DIM, s), lambda h, i: (h // rep, 0, 0))
    carried = _carry(cargo, grid, 6, 3, body)
    res = pl.pallas_call(
        carried.body, name=name, grid=grid,
        in_specs=[blk_q, blk_kv, blk_kv, blk_q, blk_q, blk_row] + carried.in_specs,
        out_specs=[blk_q, blk_acc, blk_acc] + carried.out_specs,
        out_shape=[jax.ShapeDtypeStruct((nh, HEAD_DIM, s), F32), jax.ShapeDtypeStruct((nkv, HEAD_DIM, s), F32),
                   jax.ShapeDtypeStruct((nkv, HEAD_DIM, s), F32)] + carried.out_shape,
        scratch_shapes=carried.scratch,
        compiler_params=_params(("arbitrary", "arbitrary")),
    )(qt, k, v, dot, ot, lse, *carried.args)
    return res[0], res[1], res[2], res[3:]


WIN_FAR = 1e30


def _win_start(i, tq, tk, s):
    return pl.multiple_of(jnp.clip(i * tq - WIN_REACH, 0, s - tk), LANES)


def _win_penalty(i, start, tq, tk, window, dil, seg):
    qpos = i * tq + lax.broadcasted_iota(jnp.int32, (tk, tq), 1)
    kpos = start + lax.broadcasted_iota(jnp.int32, (tk, tq), 0)
    dist = jnp.abs(kpos - qpos)
    seg_lo = qpos - (qpos & (seg - 1))
    valid = (dist <= window) & (kpos >= seg_lo) & (kpos < seg_lo + seg)
    return jnp.where(valid, (dist * dil).astype(F32), WIN_FAR)


def _win_scores(kw_t, q_t, slope, pen):
    st = lax.dot_general(kw_t, q_t, (((0,), (0,)), ((), ())), preferred_element_type=F32)
    return st * (QK_SCALE * LOG2E) - (slope * LOG2E) * pen


def _win_tq(s):
    return min(512, s)


def _win_fwd(qt, ktp, vtp, slopes, sinks, *, window, dil, seg, out_dtype, name):
    nh, _, s = qt.shape
    nkv = ktp.shape[0]
    rep = nh // nkv
    tq = _win_tq(s)
    tk = min(tq + 2 * WIN_REACH, s)
    assert window <= WIN_REACH

    def body(*refs):
        qt_ref, kt_ref, vt_ref, sl_ref = refs[:4]
        o_ref, lse_ref, pen_ref = refs[-3:]
        i, kv = pl.program_id(0), pl.program_id(1)
        start = _win_start(i, tq, tk, s)

        @pl.when(kv == 0)
        def _():
            pen_ref[...] = _win_penalty(i, start, tq, tk, window, dil, seg)

        win = pl.ds(start, tk)
        kw_t, vw_t, pen = kt_ref[0, :, win], vt_ref[0, :, win], pen_ref[...]
        for g in range(rep):
            st = _win_scores(kw_t, qt_ref[g], sl_ref[g][:, :1], pen)
            mx = jnp.max(st, axis=0, keepdims=True)
            if sinks is not None:
                sink = refs[4][g][:, :1] * LOG2E
                mx = jnp.maximum(mx, sink)
            p = jnp.exp2(st - mx)
            den = jnp.sum(p, axis=0, keepdims=True)
            if sinks is not None:
                den = den + jnp.exp2(sink - mx)
            ot = jnp.dot(vw_t, p.astype(BF16), preferred_element_type=F32)
            o_ref[g] = (ot / den).astype(o_ref.dtype)
            lse_ref[g] = mx * LN2 + jnp.log(den)

    blk_q = pl.BlockSpec((rep, HEAD_DIM, tq), lambda i, kv: (kv, 0, i))
    blk_kv = pl.BlockSpec((1, HEAD_DIM, s), lambda i, kv: (kv, 0, 0))
    blk_h = pl.BlockSpec((rep, 1, LANES), lambda i, kv: (kv, 0, 0))
    in_specs, args = [blk_q, blk_kv, blk_kv, blk_h], [qt, ktp, vtp, slopes]
    if sinks is not None:
        in_specs.append(blk_h)
        args.append(sinks)
    return pl.pallas_call(
        body, name=name, grid=(s // tq, nkv), in_specs=in_specs,
        out_specs=[blk_q, pl.BlockSpec((rep, 1, tq), lambda i, kv: (kv, 0, i))],
        out_shape=[jax.ShapeDtypeStruct((nh, HEAD_DIM, s), out_dtype), jax.ShapeDtypeStruct((nh, 1, s), F32)],
        scratch_shapes=[pltpu.VMEM((tk, tq), F32)],
        compiler_params=_params(("arbitrary", "arbitrary")),
    )(*args)


def _win_bwd(qt, ktp, vtp, slopes, sinks, dot, ot, delta, *, window, dil, seg, name):
    nh, _, s = qt.shape
    nkv = ktp.shape[0]
    rep = nh // nkv
    tq = _win_tq(s)
    tk = min(tq + 2 * WIN_REACH, s)
    assert window <= WIN_REACH
    n_in = 6 + (sinks is not None)

    def body(*refs):
        qt_ref, kt_ref, vt_ref, sl_ref, dot_ref, aux_ref = refs[:6]
        outs, pen_ref = refs[n_in:-1], refs[-1]
        dq_ref, dk_ref, dv_ref = outs[:3]
        i, kv = pl.program_id(0), pl.program_id(1)

        @pl.when((i == 0) & (kv == 0))
        def _():
            dk_ref[...] = jnp.zeros_like(dk_ref)
            dv_ref[...] = jnp.zeros_like(dv_ref)
            if sinks is not None:
                outs[3][...] = jnp.zeros_like(outs[3])

        start = _win_start(i, tq, tk, s)

        @pl.when(kv == 0)
        def _():
            pen_ref[...] = _win_penalty(i, start, tq, tk, window, dil, seg)

        win = pl.ds(start, tk)
        kw_t, vw_t, pen = kt_ref[0, :, win], vt_ref[0, :, win], pen_ref[...]
        nt = (((1,), (1,)), ((), ()))
        dk_acc = jnp.zeros((HEAD_DIM, tk), F32)
        dv_acc = jnp.zeros((HEAD_DIM, tk), F32)
        for g in range(rep):
            q_t, do_t = qt_ref[g], dot_ref[g]
            st = _win_scores(kw_t, q_t, sl_ref[g][:, :1], pen)
            mx = jnp.max(st, axis=0, keepdims=True)
            if sinks is not None:
                sink = refs[6][g][:, :1] * LOG2E
                mx = jnp.maximum(mx, sink)
            p = jnp.exp2(st - mx)
            den = jnp.sum(p, axis=0, keepdims=True)
            if sinks is not None:
                p_sink = jnp.exp2(sink - mx)
                den = den + p_sink
            p = p / den
            dp = lax.dot_general(vw_t, do_t, (((0,), (0,)), ((), ())), preferred_element_type=F32)
            if delta is None:
                row = jnp.sum(do_t.astype(F32) * aux_ref[g].astype(F32), axis=0, keepdims=True)
            else:
                row = aux_ref[g]
            ds = (p * (dp - row) * QK_SCALE).astype(BF16)
            dv_acc = dv_acc + lax.dot_general(do_t, p.astype(BF16), nt, preferred_element_type=F32)
            dk_acc = dk_acc + lax.dot_general(q_t, ds, nt, preferred_element_type=F32)
            dq_ref[g] = jnp.dot(kw_t, ds, preferred_element_type=F32)
            if sinks is not None:
                outs[3][kv * rep + g] += (jnp.zeros((1, LANES), F32)
                                          - jnp.sum(p_sink / den * row, axis=1, keepdims=True))
        dv_ref[kv, :, win] += dv_acc
        dk_ref[kv, :, win] += dk_acc

    blk_q = pl.BlockSpec((rep, HEAD_DIM, tq), lambda i, kv: (kv, 0, i))
    blk_row = pl.BlockSpec((rep, 1, tq), lambda i, kv: (kv, 0, i))
    blk_kv = pl.BlockSpec((1, HEAD_DIM, s), lambda i, kv: (kv, 0, 0))
    blk_acc = pl.BlockSpec((nkv, HEAD_DIM, s), lambda i, kv: (0, 0, 0))
    blk_h = pl.BlockSpec((rep, 1, LANES), lambda i, kv: (kv, 0, 0))
    in_specs = [blk_q, blk_kv, blk_kv, blk_h, blk_q, blk_q if delta is None else blk_row]
    args = [qt, ktp, vtp, slopes, dot, ot if delta is None else delta]
    out_specs = [blk_q, blk_acc, blk_acc]
    out_shape = [jax.ShapeDtypeStruct((nh, HEAD_DIM, s), F32), jax.ShapeDtypeStruct((nkv, HEAD_DIM, s), F32),
                 jax.ShapeDtypeStruct((nkv, HEAD_DIM, s), F32)]
    if sinks is not None:
        in_specs.append(blk_h)
        args.append(sinks)
        out_specs.append(pl.BlockSpec((nh, 1, LANES), lambda i, h: (0, 0, 0)))
        out_shape.append(jax.ShapeDtypeStruct((nh, 1, LANES), F32))
    res = pl.pallas_call(
        body, name=name, grid=(s // tq, nkv), in_specs=in_specs, out_specs=out_specs, out_shape=out_shape,
        scratch_shapes=[pltpu.VMEM((tk, tq), F32)],
        compiler_params=_params(("arbitrary", "arbitrary")),
    )(*args)
    return res if sinks is not None else (*res, None)


def _group_weights(lse):
    e = jnp.exp(lse - jnp.max(lse, axis=0, keepdims=True))
    return e / jnp.sum(e, axis=0, keepdims=True)


def _b_combine_fwd(ot, lse):
    nh, _, s = ot.shape
    ng, hg, _ = lse.shape
    ts = min(512, s)

    def body(ot_ref, lse_ref, o_ref):
        alpha = _group_weights(lse_ref[...])
        for g in range(ng):
            for j in range(hg):
                o_ref[g * hg + j] = (ot_ref[g * hg + j] * alpha[g, j:j + 1, :]).astype(BF16)

    return pl.pallas_call(
        body, name="b_combine_fwd", grid=(s // ts,),
        in_specs=[pl.BlockSpec((nh, HEAD_DIM, ts), lambda i: (0, 0, i)), pl.BlockSpec((ng, hg, ts), lambda i: (0, 0, i))],
        out_specs=pl.BlockSpec((nh, HEAD_DIM, ts), lambda i: (0, 0, i)),
        out_shape=jax.ShapeDtypeStruct((nh, HEAD_DIM, s), BF16),
        compiler_params=_params(("parallel",)),
    )(ot, lse)


def _b_combine_bwd(dout, ot, lse):
    nh, _, s = ot.shape
    ng, hg, _ = lse.shape
    ts = min(512, s)

    def body(dout_ref, ot_ref, lse_ref, do_ref, delta_ref):
        alpha = _group_weights(lse_ref[...])
        for j in range(hg):
            e = [jnp.sum(dout_ref[g * hg + j].astype(F32) * ot_ref[g * hg + j], axis=0, keepdims=True)
                 for g in range(ng)]
            a = [alpha[g, j:j + 1, :] for g in range(ng)]
            mix = a[0] * e[0]
            for g in range(1, ng):
                mix = mix + a[g] * e[g]
            for g in range(ng):
                do_ref[g * hg + j] = (dout_ref[g * hg + j].astype(F32) * a[g]).astype(BF16)
                delta_ref[g * hg + j] = a[g] * mix

    blk = pl.BlockSpec((nh, HEAD_DIM, ts), lambda i: (0, 0, i))
    return pl.pallas_call(
        body, name="b_combine_bwd", grid=(s // ts,),
        in_specs=[blk, blk, pl.BlockSpec((ng, hg, ts), lambda i: (0, 0, i))],
        out_specs=[blk, pl.BlockSpec((nh, 1, ts), lambda i: (0, 0, i))],
        out_shape=[jax.ShapeDtypeStruct((nh, HEAD_DIM, s), BF16), jax.ShapeDtypeStruct((nh, 1, s), F32)],
        compiler_params=_params(("parallel",)),
    )(dout, ot, lse)


def _alibi_slopes(n):
    return 2.0 ** (-8.0 * jnp.arange(1, n + 1, dtype=F32) / n)


def _per_head(v):
    return jnp.broadcast_to(v.astype(F32)[:, None, None], (v.shape[0], 1, LANES))


def _dilate(x, dil):
    if dil == 1:
        return x
    s = x.shape[-1]
    return jnp.swapaxes(x.reshape(x.shape[:-1] + (s // dil, dil)), -1, -2).reshape(x.shape)


def _undilate(x, dil):
    if dil == 1:
        return x
    s = x.shape[-1]
    return jnp.swapaxes(x.reshape(x.shape[:-1] + (dil, s // dil)), -1, -2).reshape(x.shape)


def _heads(x_t):
    return x_t.reshape(-1, HEAD_DIM, x_t.shape[-1])


def _mixer_fwd(kind, qkv, p, tabs, cargo, layer):
    s = qkv.shape[0 if kind == 0 else 1]
    if kind == 0:
        qt, k, v = _a_prep(qkv, tabs[0], tabs[1], p["gq2"], p["gk2"])
        ot, lse, brought = _a_attn_fwd(qt, k, v, cargo, name=f"a_attn_fwd_l{layer}")
        return ot.reshape(-1, s), dict(qt=qt, k=k, v=v, ot=ot, lse=lse), brought
    assert cargo is None
    if kind == 2:
        nq, nk = C_HEADS * HEAD_DIM, C_KV * HEAD_DIM
        qt = _heads(qkv[:nq])
        kp, vp = _heads(qkv[nq:nq + nk]), _heads(qkv[nq + nk:])
        ot, _ = _win_fwd(qt, kp, vp, p["slopes"], p["sinks"], window=C_WINDOW, dil=1, seg=s, out_dtype=BF16,
                         name="c_attn_fwd")
        return ot.reshape(-1, s), dict(qt=qt, kp=kp, vp=vp, ot=ot), ()
    ng, hg, kg = len(B_GROUPS), B_HEADS_PER_GROUP, B_KV_PER_GROUP
    nq, nk = ng * hg * HEAD_DIM, ng * kg * HEAD_DIM
    qt_all, kt_all, vt_all = _heads(qkv[:nq]), _heads(qkv[nq:nq + nk]), _heads(qkv[nq + nk:])
    saved, outs, lses = [], [], []
    for g, (window, dil) in enumerate(B_GROUPS):
        qt = _dilate(qt_all[g * hg:(g + 1) * hg], dil)
        kp = _dilate(kt_all[g * kg:(g + 1) * kg], dil)
        vp = _dilate(vt_all[g * kg:(g + 1) * kg], dil)
        sl = p["slopes"][g * hg:(g + 1) * hg]
        ot, lse = _win_fwd(qt, kp, vp, sl, None, window=window // 2 // dil, dil=dil, seg=s // dil, out_dtype=F32,
                           name=f"b_attn_fwd_g{g}")
        saved.append(dict(qt=qt, kp=kp, vp=vp))
        outs.append(_undilate(ot, dil))
        lses.append(_undilate(lse[:, 0, :], dil))
    ot_all, lse_all = jnp.concatenate(outs, axis=0), jnp.stack(lses, axis=0)
    mixed = _b_combine_fwd(ot_all, lse_all)
    return mixed.reshape(-1, s), dict(groups=saved, ot=ot_all, lse=lse_all), ()


def _mixer_bwd(kind, do_t, qkv, sv, p, tabs, cargo, layer):
    s = do_t.shape[1]
    do_heads = _heads(do_t)
    small = {}
    if kind == 0:
        dqt, dkt, dvt, brought = _a_attn_bwd(sv["qt"], sv["k"], sv["v"], do_heads, sv["ot"], sv["lse"],
                                             cargo, name=f"a_attn_bwd_l{layer}")
        dqkv, dgq, dgk = _a_prep_bwd(dqt, dkt, dvt, qkv, tabs[0], tabs[1], p["gq2"], p["gk2"])
        small["q_gain"] = dgq[0, :HEAD_DIM] + dgq[0, HEAD_DIM:]
        small["k_gain"] = dgk[0, :HEAD_DIM] + dgk[0, HEAD_DIM:]
        return dqkv, small, brought
    assert cargo is None
    if kind == 2:
        dqt, dkt, dvt, dsink = _win_bwd(sv["qt"], sv["kp"], sv["vp"], p["slopes"], p["sinks"], do_heads,
                                        sv["ot"], None, window=C_WINDOW, dil=1, seg=s, name="c_attn_bwd")
        small["sinks"] = dsink[:, 0, 0]
        parts = [dqt.reshape(-1, s), dkt.reshape(-1, s), dvt.reshape(-1, s)]
        return jnp.concatenate(parts, axis=0).astype(BF16), small, ()
    ng, hg, kg = len(B_GROUPS), B_HEADS_PER_GROUP, B_KV_PER_GROUP
    do_own, delta = _b_combine_bwd(do_heads, sv["ot"], sv["lse"])
    dqs, dks, dvs = [], [], []
    for g, (window, dil) in enumerate(B_GROUPS):
        gs = sv["groups"][g]
        dqt, dkt, dvt, _ = _win_bwd(gs["qt"], gs["kp"], gs["vp"], p["slopes"][g * hg:(g + 1) * hg], None,
                                    _dilate(do_own[g * hg:(g + 1) * hg], dil), None,
                                    _dilate(delta[g * hg:(g + 1) * hg], dil),
                                    window=window // 2 // dil, dil=dil, seg=s // dil, name=f"b_attn_bwd_g{g}")
        dqs.append(_undilate(dqt, dil))
        dks.append(_undilate(dkt, dil))
        dvs.append(_undilate(dvt, dil))
    parts = [x.reshape(-1, s) for x in dqs + dks + dvs]
    return jnp.concatenate(parts, axis=0).astype(BF16), small, ()


LAYER_MATS = ("w_qkv", "w_o", "w1", "w2")
COLUMN_QUARTERS = ("w_qkv", "w1")


def _whole(key, gathered):
    q, r, c = gathered.shape
    if key == "w1":
        return gathered
    if key in COLUMN_QUARTERS:
        return jnp.transpose(gathered, (1, 0, 2)).reshape(r, q * c)
    return gathered.reshape(q * r, c)


def _quarters(key, g):
    r, c = g.shape
    if key in COLUMN_QUARTERS:
        return jnp.transpose(g.reshape(r, 4, c // 4), (1, 0, 2))
    return g.reshape(4, r // 4, c)


def _local_step(x, target, norms, mixer_params, shards, whole=None):
    s = x.shape[0]
    tabs = _rope_tables(s)
    if whole is None:
        assert MIXER_OF_LAYER[0][0] == 0
        first = _run_cargo(_gather_cargo([shards[0][key] for key in LAYER_MATS]), name="gather_l0")
        mats = {0: {key: _whole(key, g) for key, g in zip(LAYER_MATS, first)}}
        later = _gather_cargo([shards[layer][key] for layer in range(1, DEPTH) for key in LAYER_MATS])
    else:
        mats, later = dict(enumerate(whole)), None
    h = x
    saved = []
    for layer in range(DEPTH):
        kind = layer % N_MIXERS
        w, p = mats[layer], mixer_params[layer]
        hn, qkv = _norm_mm(h, norms["attn"][layer][None], w["w_qkv"], out_dtype=F32 if kind == 0 else BF16,
                           relu2=False, transpose_out=kind != 0, name=f"qkv_proj_l{layer}")
        o_t, sv, brought = _mixer_fwd(kind, qkv, p, tabs, later if layer == 0 else None, layer)
        for n, g in enumerate(brought):
            mats.setdefault(1 + n // len(LAYER_MATS), {})[LAYER_MATS[n % len(LAYER_MATS)]] = _whole(
                LAYER_MATS[n % len(LAYER_MATS)], g)
        h_mid = _mm_res(o_t, w["w_o"], h, a_transposed=True, name=f"o_proj_l{layer}")
        hn2, act = _norm_mm(h_mid, norms["mlp"][layer][None], w["w1"], out_dtype=BF16, relu2=True,
                            transpose_out=False, name=f"mlp_up_l{layer}")
        h_out = _mm_res(act, w["w2"], h_mid, a_transposed=False, name=f"mlp_down_l{layer}")
        saved.append(dict(h=h, hn=hn, qkv=qkv, o_t=o_t, mix=sv, h_mid=h_mid, hn2=hn2, act=act))
        h = h_out

    dh, loss, d_final = _loss_head(h, norms["final"][None], target)

    own, received, pending = {}, {}, []
    d_attn, d_mlp, small = [None] * DEPTH, [None] * DEPTH, [None] * DEPTH
    for layer in reversed(range(DEPTH)):
        kind = layer % N_MIXERS
        w, p, sv = mats[layer], mixer_params[layer], saved[layer]
        du = _mm_nt(dh, w["w2"], sv["act"], transpose_out=False, name=f"mlp_down_bwd_l{layer}")
        own[layer, "w2"] = _quarters("w2", _mm_tn(sv["act"], dh, x_transposed=False, g_transposed=False,
                                                  column_quarters=False, name=f"mlp_w2_grad_l{layer}"))
        own[layer, "w1"] = _mm_tn(sv["hn2"], du, x_transposed=False, g_transposed=False, column_quarters=True,
                                  name=f"mlp_w1_grad_l{layer}")
        dh_mid, d_mlp[layer] = _mm_nt_normbwd(du, w["w1"], sv["h_mid"], norms["mlp"][layer][None], dh,
                                              g_transposed=False, name=f"mlp_up_bwd_l{layer}")
        do_t = _mm_nt(dh_mid, w["w_o"], None, transpose_out=True, name=f"o_proj_bwd_l{layer}")
        own[layer, "w_o"] = _quarters("w_o", _mm_tn(sv["o_t"], dh_mid, x_transposed=True, g_transposed=False,
                                                    column_quarters=False, name=f"w_o_grad_l{layer}"))
        pending += [(layer, "w2"), (layer, "w1"), (layer, "w_o")]
        cargo = None
        if kind == 0 and whole is None:
            cargo, sent, pending = _scatter_cargo([own[item] for item in pending], None), pending, []
        dqkv, small[layer], brought = _mixer_bwd(kind, do_t, sv["qkv"], sv["mix"], p, tabs, cargo, layer)
        if cargo is not None:
            received.update(zip(sent, brought))
        own[layer, "w_qkv"] = _quarters("w_qkv", _mm_tn(sv["hn"], dqkv, x_transposed=False, g_transposed=kind != 0,
                                                        column_quarters=False, name=f"w_qkv_grad_l{layer}"))
        pending.append((layer, "w_qkv"))
        dh, d_attn[layer] = _mm_nt_normbwd(dqkv, w["w_qkv"], sv["h"], norms["attn"][layer][None], dh_mid,
                                           g_transposed=kind != 0, name=f"qkv_proj_bwd_l{layer}")
    return loss, dh, own, received, pending, dict(attn=d_attn, mlp=d_mlp, final=d_final, mixer=small)


CHIP_FLIPS = ((1, 0), (0, 1), (1, 1))


class _Cargo(NamedTuple):
    ins: tuple
    out_shape: tuple
    sem_shapes: tuple
    start: Callable
    wait: Callable


class _Carried(NamedTuple):
    body: Callable
    in_specs: list
    out_specs: list
    out_shape: list
    scratch: list
    args: tuple


def _carry(cargo, grid, n_in, n_out, body):
    if cargo is None:
        return _Carried(body, [], [], [], [], ())
    ci, co = len(cargo.ins), len(cargo.out_shape)

    def wrapped(*refs):
        ins, c_ins = refs[:n_in], refs[n_in:n_in + ci]
        outs, c_outs = refs[n_in + ci:n_in + ci + n_out], refs[n_in + ci + n_out:n_in + ci + n_out + co]
        sems = refs[n_in + ci + n_out + co:]
        first = last = None
        for axis, extent in enumerate(grid):
            at = pl.program_id(axis)
            first = (at == 0) if first is None else first & (at == 0)
            last = (at == extent - 1) if last is None else last & (at == extent - 1)

        @pl.when(first)
        def _():
            cargo.start(c_ins, c_outs, sems)

        body(*ins, *outs)

        @pl.when(last)
        def _():
            cargo.wait(c_ins, c_outs, sems)

    return _Carried(wrapped, [ANY] * ci, [ANY] * co, list(cargo.out_shape), list(cargo.sem_shapes), tuple(cargo.ins))


def _run_cargo(cargo, *, name):
    ci, co = len(cargo.ins), len(cargo.out_shape)

    def body(*refs):
        cargo.start(refs[:ci], refs[ci:ci + co], refs[ci + co:])
        cargo.wait(refs[:ci], refs[ci:ci + co], refs[ci + co:])

    return pl.pallas_call(body, name=name, in_specs=[ANY] * ci, out_specs=[ANY] * co, out_shape=list(cargo.out_shape),
                          scratch_shapes=list(cargo.sem_shapes))(*cargo.ins)


def _other_chip(x, y, j):
    fx, fy = CHIP_FLIPS[j]
    return (1 - x if fx else x), (1 - y if fy else y)


def _gather_cargo(shards):
    n = len(shards)
    halves = [a.shape[0] // 2 for a in shards]

    def copies(ins, outs, sems):
        ici_send, ici_recv, d2d_send, d2d_recv, local_sems = sems
        x, y, c = lax.axis_index("x"), lax.axis_index("y"), lax.axis_index("c")
        me = 2 * x + y

        def half(t, which):
            return pl.ds(pl.multiple_of(which * halves[t], 16), halves[t])

        def over_ici(t, j, arriving):
            px, py = _other_chip(x, y, j)
            return pltpu.make_async_remote_copy(
                src_ref=ins[t].at[half(t, c)], dst_ref=outs[t].at[2 * px + py if arriving else me, half(t, c)],
                send_sem=ici_send.at[t, j], recv_sem=ici_recv.at[t, j], device_id=(px, py, c), device_id_type=MESH)

        def over_d2d(t, j, arriving):
            px, py = _other_chip(x, y, j)
            mine = outs[t].at[2 * px + py, half(t, c)]
            return pltpu.make_async_remote_copy(
                src_ref=mine, dst_ref=outs[t].at[2 * px + py, half(t, 1 - c)] if arriving else mine,
                send_sem=d2d_send.at[t, j], recv_sem=d2d_recv.at[t, j], device_id=(x, y, 1 - c), device_id_type=MESH)

        return over_ici, over_d2d, lambda t: pltpu.make_async_copy(ins[t], outs[t].at[me], local_sems.at[t])

    def start(ins, outs, sems):
        over_ici, _, own = copies(ins, outs, sems)
        for t in range(n):
            own(t).start()
            for j in range(len(CHIP_FLIPS)):
                over_ici(t, j, False).start()

    def wait(ins, outs, sems):
        over_ici, over_d2d, own = copies(ins, outs, sems)
        for t in range(n):
            for j in range(len(CHIP_FLIPS)):
                over_ici(t, j, True).wait_recv()
                over_d2d(t, j, False).start()
        for t in range(n):
            for j in range(len(CHIP_FLIPS)):
                over_d2d(t, j, True).wait_recv()
                over_d2d(t, j, False).wait_send()
                over_ici(t, j, False).wait_send()
            own(t).wait()

    dma = pltpu.SemaphoreType.DMA
    return _Cargo(tuple(shards), tuple(jax.ShapeDtypeStruct((4,) + a.shape, a.dtype) for a in shards),
                  (dma((n, 3)), dma((n, 3)), dma((n, 3)), dma((n, 3)), dma((n,))), start, wait)


def _scatter_cargo(grads, small):
    n = len(grads)

    def copies(ins, outs, sems):
        x, y, c = lax.axis_index("x"), lax.axis_index("y"), lax.axis_index("c")
        me = 4 * x + 2 * y + c

        def remote(t, j):
            px, py = _other_chip(x, y, j)
            return pltpu.make_async_remote_copy(
                src_ref=ins[t].at[2 * px + py], dst_ref=outs[t].at[j], send_sem=sems[0].at[t, j],
                recv_sem=sems[1].at[t, j], device_id=(px, py, c), device_id_type=MESH)

        def small_remote(r, arriving):
            fx, fy, fc = (r + 1) // 4, ((r + 1) // 2) % 2, (r + 1) % 2
            px, py, pc = (1 - x if fx else x), (1 - y if fy else y), (1 - c if fc else c)
            return pltpu.make_async_remote_copy(
                src_ref=ins[n], dst_ref=outs[n].at[4 * px + 2 * py + pc if arriving else me],
                send_sem=sems[2].at[r], recv_sem=sems[3].at[r], device_id=(px, py, pc), device_id_type=MESH)

        return remote, small_remote, lambda: pltpu.make_async_copy(ins[n], outs[n].at[me], sems[4])

    def start(ins, outs, sems):
        remote, small_remote, small_own = copies(ins, outs, sems)
        if small is not None:
            small_own().start()
            for r in range(7):
                small_remote(r, False).start()
        for t in range(n):
            for j in range(len(CHIP_FLIPS)):
                remote(t, j).start()

    def wait(ins, outs, sems):
        remote, small_remote, small_own = copies(ins, outs, sems)
        if small is not None:
            for r in range(7):
                small_remote(r, True).wait_recv()
                small_remote(r, False).wait_send()
            small_own().wait()
        for t in range(n):
            for j in range(len(CHIP_FLIPS)):
                remote(t, j).wait()

    dma = pltpu.SemaphoreType.DMA
    ins = tuple(grads) + (() if small is None else (small,))
    out_shape = tuple(jax.ShapeDtypeStruct((3,) + g.shape[1:], g.dtype) for g in grads)
    sem_shapes = (dma((n, 3)), dma((n, 3)))
    if small is not None:
        out_shape += (jax.ShapeDtypeStruct((8,) + small.shape, small.dtype),)
        sem_shapes += (dma((7,)), dma((7,)), dma(()))
    return _Cargo(ins, out_shape, sem_shapes, start, wait)


def _swap_cores(parts):
    n = len(parts)

    def body(*refs):
        ins, outs = refs[:n], refs[n:2 * n]
        send_sems, recv_sems = refs[2 * n:]
        peer = (lax.axis_index("x"), lax.axis_index("y"), 1 - lax.axis_index("c"))
        copies = [pltpu.make_async_remote_copy(src_ref=ins[t], dst_ref=outs[t], send_sem=send_sems.at[t],
                                               recv_sem=recv_sems.at[t], device_id=peer, device_id_type=MESH)
                  for t in range(n)]
        for cp in copies:
            cp.start()
        for cp in copies:
            cp.wait()

    return pl.pallas_call(
        body, name="swap_cores", in_specs=[ANY] * n, out_specs=[ANY] * n,
        out_shape=[jax.ShapeDtypeStruct(a.shape, a.dtype) for a in parts],
        scratch_shapes=[pltpu.SemaphoreType.DMA((n,)), pltpu.SemaphoreType.DMA((n,))],
    )(*parts)


def _rows_tile(r):
    return 256 if r % 256 == 0 else r


def _sum_quarters(own, recv, *, name):
    r, c = own.shape
    tr = _rows_tile(r)

    def body(own_ref, recv_ref, o_ref):
        acc = own_ref[...].astype(F32)
        for j in range(3):
            acc = acc + recv_ref[j].astype(F32)
        o_ref[...] = acc

    return pl.pallas_call(
        body, name=name, grid=(r // tr,),
        in_specs=[pl.BlockSpec((tr, c), lambda i: (i, 0)), pl.BlockSpec((3, tr, c), lambda i: (0, i, 0))],
        out_specs=pl.BlockSpec((tr, c), lambda i: (i, 0)),
        out_shape=jax.ShapeDtypeStruct((r, c), F32),
        compiler_params=_params(("parallel",)),
    )(own, recv)


def _adamw(w, m, v, parts, *, name):
    r, c = w.shape
    tr = _rows_tile(r)
    c1, c2 = 1.0 - ADAM_B1 ** ADAM_STEP, 1.0 - ADAM_B2 ** ADAM_STEP
    n_parts = len(parts)

    def body(*refs):
        w_ref, m_ref, v_ref = refs[:3]
        g_ref, d_ref, nm_ref, nv_ref = refs[3 + n_parts:]
        terms = []
        for p_ref in refs[3:3 + n_parts]:
            terms += [p_ref[...]] if len(p_ref.shape) == 2 else [p_ref[j] for j in range(p_ref.shape[0])]
        g = terms[0]
        for term in terms[1:]:
            g = g + term
        m_new = ADAM_B1 * m_ref[...] + (1.0 - ADAM_B1) * g
        v_new = ADAM_B2 * v_ref[...] + (1.0 - ADAM_B2) * (g * g)
        step = (m_new / c1) / (jnp.sqrt(v_new / c2) + ADAM_EPS)
        g_ref[...] = g
        d_ref[...] = -ADAM_LR * (step + ADAM_WD * w_ref[...])
        nm_ref[...] = m_new
        nv_ref[...] = v_new

    blk = pl.BlockSpec((tr, c), lambda i: (i, 0))
    part_specs = [blk if p.ndim == 2 else pl.BlockSpec((p.shape[0], tr, c), lambda i: (0, i, 0)) for p in parts]
    return pl.pallas_call(
        body, name=name, grid=(r // tr,), in_specs=[blk, blk, blk] + part_specs,
        out_specs=[blk] * 4, out_shape=[jax.ShapeDtypeStruct((r, c), F32)] * 4,
        compiler_params=_params(("parallel",)),
    )(w, m, v, *parts)


MATS = ("a_w_qkv", "a_w_o", "b_w_qkv", "b_w_o", "c_w_qkv", "c_w_o", "mlp_w1", "mlp_w2")
SMALLS = ("attn_norm", "mlp_norm", "a_q_gain", "a_k_gain", "c_sinks", "final_norm")
WEIGHTS = ("attn_norm", "mlp_norm", "a_w_qkv", "a_q_gain", "a_k_gain", "a_w_o", "b_w_qkv", "b_w_o", "c_w_qkv",
           "c_sinks", "c_w_o", "mlp_w1", "mlp_w2", "final_norm")
MIXER_OF_LAYER = tuple((layer % N_MIXERS, sum(1 for q in range(layer) if q % N_MIXERS == layer % N_MIXERS))
                       for layer in range(DEPTH))
SMALL_ROWS = 8


def _pack_small(values):
    rows, spans, at = [], [], 0
    for v in values:
        flat = v.reshape(-1)
        n = -(-flat.shape[0] // (SMALL_ROWS * LANES)) * SMALL_ROWS
        rows.append(jnp.pad(flat, (0, n * LANES - flat.shape[0])).reshape(n, LANES))
        spans.append((at, n))
        at += n
    return jnp.concatenate(rows, axis=0), spans


def kernel(x, attn_norm, mlp_norm, a_w_qkv, a_q_gain, a_k_gain, a_w_o, b_w_qkv, b_w_o, c_w_qkv, c_sinks, c_w_o, mlp_w1, mlp_w2, final_norm, loss_target, m_attn_norm, m_mlp_norm, m_a_w_qkv, m_a_q_gain, m_a_k_gain, m_a_w_o, m_b_w_qkv, m_b_w_o, m_c_w_qkv, m_c_sinks, m_c_w_o, m_mlp_w1, m_mlp_w2, m_final_norm, v_attn_norm, v_mlp_norm, v_a_w_qkv, v_a_q_gain, v_a_k_gain, v_a_w_o, v_b_w_qkv, v_b_w_o, v_c_w_qkv, v_c_sinks, v_c_w_o, v_mlp_w1, v_mlp_w2, v_final_norm):
    env = dict(locals())
    w = {name: env[name] for name in WEIGHTS}
    mom = {name: (env["m_" + name], env["v_" + name]) for name in WEIGHTS}

    prefix = ("a", "b", "c")
    shards, mixer_params = [], []
    for layer, (kind, j) in enumerate(MIXER_OF_LAYER):
        shards.append(dict(w_qkv=w[prefix[kind] + "_w_qkv"][j].astype(BF16), w_o=w[prefix[kind] + "_w_o"][j].astype(BF16),
                           w1=mlp_w1[layer].astype(BF16), w2=mlp_w2[layer].astype(BF16)))
        if kind == 0:
            mixer_params.append(dict(gq2=jnp.tile(a_q_gain[j], 2)[None], gk2=jnp.tile(a_k_gain[j], 2)[None]))
        elif kind == 1:
            mixer_params.append(dict(slopes=_per_head(_alibi_slopes(len(B_GROUPS) * B_HEADS_PER_GROUP))))
        else:
            mixer_params.append(dict(slopes=_per_head(_alibi_slopes(C_HEADS)), sinks=_per_head(c_sinks[j])))

    norms = dict(attn=attn_norm, mlp=mlp_norm, final=final_norm)
    loss_part, grad_x, own, received, pending, g_small = _local_step(x[0], loss_target[0], norms, mixer_params, shards)
    loss = lax.psum(loss_part[0, 0], ("x", "y", "c"))

    of_kind = lambda kind, key: jnp.stack([g_small["mixer"][layer][key] for layer, (k, _) in enumerate(MIXER_OF_LAYER)
                                           if k == kind])
    small_grads = dict(
        attn_norm=jnp.concatenate(g_small["attn"], axis=0), mlp_norm=jnp.concatenate(g_small["mlp"], axis=0),
        a_q_gain=of_kind(0, "q_gain"), a_k_gain=of_kind(0, "k_gain"), c_sinks=of_kind(2, "sinks"),
        final_norm=g_small["final"][0])
    packed, spans = _pack_small([small_grads[name] for name in SMALLS])
    *last, all_small = _run_cargo(_scatter_cargo([own[item] for item in pending], packed), name="scatter_last")
    received.update(zip(pending, last))

    me_chip = 2 * lax.axis_index("x") + lax.axis_index("y")
    partial = []
    for name in MATS:
        key = name[2:] if name[0] in "abc" else name[4:]
        layers = [layer for layer, (kind, _) in enumerate(MIXER_OF_LAYER)
                  if name.startswith("mlp") or prefix[kind] == name[0]]
        sums = [_sum_quarters(lax.dynamic_index_in_dim(own[layer, key], me_chip, axis=0, keepdims=False),
                              received[layer, key], name=f"sum_{name}_l{layer}") for layer in layers]
        partial.append(jnp.concatenate(sums, axis=0))
    other = _swap_cores(partial)

    out = {}
    for name, mine, theirs in zip(MATS, partial, other):
        shape = w[name].shape
        res = _adamw(*[a.reshape(-1, shape[-1]) for a in (w[name], *mom[name])], [mine, theirs], name=f"adamw_{name}")
        out[name] = [a.reshape(shape) for a in res]
    for name, (at, n) in zip(SMALLS, spans):
        shape = w[name].shape
        packed_in = [_pack_small([a])[0] for a in (w[name], *mom[name])]
        res = _adamw(*packed_in, [all_small[:, at:at + n]], name=f"adamw_{name}")
        out[name] = [a.reshape(-1)[:w[name].size].reshape(shape) for a in res]

    return (loss, grad_x[None], *[out[name][0] for name in WEIGHTS], *[out[name][1] for name in WEIGHTS],
            *[out[name][2] for name in WEIGHTS], *[out[name][3] for name in WEIGHTS])
```

```python
from typing import Callable, NamedTuple

import jax
import jax.numpy as jnp
from jax import lax
from jax.experimental import pallas as pl
from jax.experimental.pallas import tpu as pltpu

F32 = jnp.float32
BF16 = jnp.bfloat16
MESH = pl.DeviceIdType.MESH
ANY = pl.BlockSpec(memory_space=pl.ANY)

D_MODEL = 1024
HEAD_DIM = 64
GRID_W = 64
ROPE_THETA = 10000.0
RMS_EPS = 1e-6
QK_SCALE = HEAD_DIM ** -0.5
LOG2E = 1.4426950408889634
LN2 = 0.6931471805599453
A_HEADS, A_KV = 16, 4
B_GROUPS = ((128, 1), (512, 4), (2048, 16))
B_HEADS_PER_GROUP, B_KV_PER_GROUP = 6, 2
C_HEADS, C_KV, C_WINDOW = 16, 4, 128
DEPTH, N_MIXERS = 4, 3
ADAM_LR, ADAM_B1, ADAM_B2, ADAM_EPS, ADAM_WD, ADAM_STEP = 0.001, 0.9, 0.999, 1e-08, 0.01, 10

WIN_REACH = 128
V7X_VMEM_BUDGET = 48 * 1024 * 1024
LANES = 128
ROW_TILE = 1024


def _params(semantics):
    return pltpu.CompilerParams(dimension_semantics=semantics, vmem_limit_bytes=V7X_VMEM_BUDGET)


def _tile(n, cap):
    if n <= cap:
        return n
    t = (cap // LANES) * LANES
    while n % t:
        t -= LANES
    return t


def _norm_mm(h, gain, w, *, out_dtype, relu2, transpose_out, name):
    m, d = h.shape
    by_quarter = w.ndim == 3
    assert not (by_quarter and transpose_out)
    n = w.shape[-1] * (4 if by_quarter else 1)
    per_step = 2 if by_quarter and 2 * w.shape[-1] <= 2048 else 1
    tm, tn = min(ROW_TILE, m), (per_step * w.shape[-1] if by_quarter else _tile(n, 2048))
    w_spec = (pl.BlockSpec((per_step, d, tn // per_step), lambda i, j: (j, 0, 0)) if by_quarter
              else pl.BlockSpec((d, tn), lambda i, j: (0, j)))
    y_spec = (pl.BlockSpec((tn, tm), lambda i, j: (j, i)) if transpose_out
              else pl.BlockSpec((tm, tn), lambda i, j: (i, j)))

    def body(h_ref, g_ref, w_ref, hn_ref, y_ref):
        @pl.when(pl.program_id(1) == 0)
        def _():
            x = h_ref[...]
            r = lax.rsqrt(jnp.mean(x * x, axis=-1, keepdims=True) + RMS_EPS)
            hn_ref[...] = (x * r * g_ref[...]).astype(BF16)

        def finish(y):
            if relu2:
                y = jnp.maximum(y, 0.0)
                y = y * y
            return y.astype(y_ref.dtype)

        if transpose_out:
            y_ref[...] = finish(lax.dot_general(w_ref[...], hn_ref[...], (((0,), (1,)), ((), ())),
                                                preferred_element_type=F32))
        elif by_quarter:
            cols = tn // per_step
            for q in range(per_step):
                y_ref[:, q * cols:(q + 1) * cols] = finish(jnp.dot(hn_ref[...], w_ref[q], preferred_element_type=F32))
        else:
            y_ref[...] = finish(jnp.dot(hn_ref[...], w_ref[...], preferred_element_type=F32))

    return pl.pallas_call(
        body, name=name, grid=(m // tm, n // tn),
        in_specs=[pl.BlockSpec((tm, d), lambda i, j: (i, 0)), pl.BlockSpec((1, d), lambda i, j: (0, 0)), w_spec],
        out_specs=[pl.BlockSpec((tm, d), lambda i, j: (i, 0)), y_spec],
        out_shape=[jax.ShapeDtypeStruct((m, d), BF16), jax.ShapeDtypeStruct((n, m) if transpose_out else (m, n), out_dtype)],
        compiler_params=_params(("parallel", "arbitrary")),
    )(h, gain, w)


def _mm_res(a, w, h_in, *, a_transposed, name):
    k, d = w.shape
    m = h_in.shape[0]
    tm, tk = min(ROW_TILE, m), _tile(k, 1152)
    lhs_contracts = 0 if a_transposed else 1

    def body(a_ref, w_ref, h_ref, o_ref):
        @pl.when(pl.program_id(1) == 0)
        def _():
            o_ref[...] = h_ref[...]

        o_ref[...] += lax.dot_general(a_ref[...], w_ref[...], (((lhs_contracts,), (0,)), ((), ())),
                                      preferred_element_type=F32)

    a_spec = (pl.BlockSpec((tk, tm), lambda i, j: (j, i)) if a_transposed
              else pl.BlockSpec((tm, tk), lambda i, j: (i, j)))
    return pl.pallas_call(
        body, name=name, grid=(m // tm, k // tk),
        in_specs=[a_spec, pl.BlockSpec((tk, d), lambda i, j: (j, 0)), pl.BlockSpec((tm, d), lambda i, j: (i, 0))],
        out_specs=pl.BlockSpec((tm, d), lambda i, j: (i, 0)),
        out_shape=jax.ShapeDtypeStruct((m, d), F32),
        compiler_params=_params(("parallel", "arbitrary")),
    )(a, w, h_in)


def _mm_nt(a, w, act, *, transpose_out, name):
    m, d = a.shape
    n = w.shape[0]
    tm, tn = min(ROW_TILE, m), _tile(n, 1152)
    assert act is None or not transpose_out
    nt = (((1,), (1,)), ((), ()))

    def body(*refs):
        a_ref, w_ref = refs[0], refs[1]
        o_ref = refs[-1]
        if transpose_out:
            acc = lax.dot_general(w_ref[...], a_ref[...].astype(BF16), nt, preferred_element_type=F32)
        else:
            acc = lax.dot_general(a_ref[...].astype(BF16), w_ref[...], nt, preferred_element_type=F32)
        if act is not None:
            acc = acc * (2.0 * jnp.sqrt(refs[2][...].astype(F32)))
        o_ref[...] = acc.astype(BF16)

    in_specs = [pl.BlockSpec((tm, d), lambda i, j: (i, 0)), pl.BlockSpec((tn, d), lambda i, j: (j, 0))]
    args = [a, w]
    if act is not None:
        in_specs.append(pl.BlockSpec((tm, tn), lambda i, j: (i, j)))
        args.append(act)
    out_spec = (pl.BlockSpec((tn, tm), lambda i, j: (j, i)) if transpose_out
                else pl.BlockSpec((tm, tn), lambda i, j: (i, j)))
    return pl.pallas_call(
        body, name=name, grid=(m // tm, n // tn), in_specs=in_specs, out_specs=out_spec,
        out_shape=jax.ShapeDtypeStruct((n, m) if transpose_out else (m, n), BF16),
        compiler_params=_params(("parallel", "parallel")),
    )(*args)


def _rmsnorm_bwd(dn, x, gain):
    r = lax.rsqrt(jnp.mean(x * x, axis=-1, keepdims=True) + RMS_EPS)
    xh = x * r
    dgain = jnp.sum(dn * xh, axis=0, keepdims=True)
    u = dn * gain
    dx = r * (u - xh * jnp.mean(u * xh, axis=-1, keepdims=True))
    return dx, dgain


def _mm_nt_normbwd(g, w, h, gain, dh_in, *, g_transposed, name):
    m, k = g.shape[::-1] if g_transposed else g.shape
    by_quarter = w.ndim == 3
    d = w.shape[-2]
    tm, tk = min(ROW_TILE, m), (w.shape[-1] if by_quarter else _tile(k, 1024))
    sub = min(256, tm)
    nk = k // tk
    w_spec = (pl.BlockSpec((None, d, tk), lambda i, j: (j, 0, 0)) if by_quarter
              else pl.BlockSpec((d, tk), lambda i, j: (0, j)))

    def body(g_ref, w_ref, h_ref, gain_ref, dh_ref, o_ref, dg_ref, acc_ref):
        i, j = pl.program_id(0), pl.program_id(1)

        @pl.when((i == 0) & (j == 0))
        def _():
            dg_ref[...] = jnp.zeros_like(dg_ref)

        @pl.when(j == 0)
        def _():
            acc_ref[...] = jnp.zeros_like(acc_ref)

        acc_ref[...] += lax.dot_general(g_ref[...], w_ref[...], (((0 if g_transposed else 1,), (1,)), ((), ())),
                                        preferred_element_type=F32)

        @pl.when(j == nk - 1)
        def _():
            for r in range(0, tm, sub):
                rows = slice(r, r + sub)
                dx, dgain = _rmsnorm_bwd(acc_ref[rows, :], h_ref[rows, :], gain_ref[...])
                dg_ref[...] += dgain
                o_ref[rows, :] = dh_ref[rows, :] + dx

    return pl.pallas_call(
        body, name=name, grid=(m // tm, nk),
        in_specs=[pl.BlockSpec((tk, tm), lambda i, j: (j, i)) if g_transposed
                  else pl.BlockSpec((tm, tk), lambda i, j: (i, j)), w_spec,
                  pl.BlockSpec((tm, d), lambda i, j: (i, 0)), pl.BlockSpec((1, d), lambda i, j: (0, 0)),
                  pl.BlockSpec((tm, d), lambda i, j: (i, 0))],
        out_specs=[pl.BlockSpec((tm, d), lambda i, j: (i, 0)), pl.BlockSpec((1, d), lambda i, j: (0, 0))],
        out_shape=[jax.ShapeDtypeStruct((m, d), F32), jax.ShapeDtypeStruct((1, d), F32)],
        scratch_shapes=[pltpu.VMEM((tm, d), F32)],
        compiler_params=_params(("arbitrary", "arbitrary")),
    )(g, w, h, gain, dh_in)


def _mm_tn(x, g, *, x_transposed, g_transposed, column_quarters, name):
    k, m = x.shape if x_transposed else x.shape[::-1]
    n = g.shape[0] if g_transposed else g.shape[1]
    tm, tk, tn = min(ROW_TILE, m), _tile(k, 1152), (n // 4 if column_quarters else _tile(n, 1024))
    nm = m // tm
    lhs_contracts, rhs_contracts = (1 if x_transposed else 0), (1 if g_transposed else 0)
    g_spec = (pl.BlockSpec((tn, tm), lambda a, b, s: (b, s)) if g_transposed
              else pl.BlockSpec((tm, tn), lambda a, b, s: (s, b)))

    def body(x_ref, g_ref, o_ref, acc_ref):
        s = pl.program_id(2)

        @pl.when(s == 0)
        def _():
            acc_ref[...] = jnp.zeros_like(acc_ref)

        acc_ref[...] += lax.dot_general(x_ref[...], g_ref[...].astype(BF16),
                                        (((lhs_contracts,), (rhs_contracts,)), ((), ())), preferred_element_type=F32)

        @pl.when(s == nm - 1)
        def _():
            o_ref[...] = acc_ref[...].astype(BF16)

    x_spec = (pl.BlockSpec((tk, tm), lambda a, b, s: (a, s)) if x_transposed
              else pl.BlockSpec((tm, tk), lambda a, b, s: (s, a)))
    out_spec = (pl.BlockSpec((None, tk, tn), lambda a, b, s: (b, a, 0)) if column_quarters
                else pl.BlockSpec((tk, tn), lambda a, b, s: (a, b)))
    return pl.pallas_call(
        body, name=name, grid=(k // tk, n // tn, nm),
        in_specs=[x_spec, g_spec], out_specs=out_spec,
        out_shape=jax.ShapeDtypeStruct((4, k, n // 4) if column_quarters else (k, n), BF16),
        scratch_shapes=[pltpu.VMEM((tk, tn), F32)],
        compiler_params=_params(("parallel", "parallel", "arbitrary")),
    )(x, g)


def _loss_head(h, gain, target):
    m, d = h.shape
    tm = 512

    def body(h_ref, g_ref, t_ref, dh_ref, loss_ref, dg_ref):
        @pl.when(pl.program_id(0) == 0)
        def _():
            loss_ref[...] = jnp.zeros_like(loss_ref)
            dg_ref[...] = jnp.zeros_like(dg_ref)

        x = h_ref[...]
        gain_v = g_ref[...]
        r = lax.rsqrt(jnp.mean(x * x, axis=-1, keepdims=True) + RMS_EPS)
        err = x * r * gain_v - t_ref[...]
        loss_ref[...] += 0.5 * jnp.sum(jnp.mean(err * err, axis=-1, keepdims=True), axis=0, keepdims=True)
        dx, dgain = _rmsnorm_bwd(err * (1.0 / d), x, gain_v)
        dg_ref[...] += dgain
        dh_ref[...] = dx

    return pl.pallas_call(
        body, name="loss_head", grid=(m // tm,),
        in_specs=[pl.BlockSpec((tm, d), lambda i: (i, 0)), pl.BlockSpec((1, d), lambda i: (0, 0)),
                  pl.BlockSpec((tm, d), lambda i: (i, 0))],
        out_specs=[pl.BlockSpec((tm, d), lambda i: (i, 0)), pl.BlockSpec((1, LANES), lambda i: (0, 0)),
                   pl.BlockSpec((1, d), lambda i: (0, 0))],
        out_shape=[jax.ShapeDtypeStruct((m, d), F32), jax.ShapeDtypeStruct((1, LANES), F32),
                   jax.ShapeDtypeStruct((1, d), F32)],
        compiler_params=_params(("arbitrary",)),
    )(h, gain, target)


def _rope_tables(s):
    t = jnp.arange(s)
    row = (t // GRID_W).astype(F32)
    col = (t % GRID_W).astype(F32)
    axis_dim = HEAD_DIM // 2
    inv_freq = ROPE_THETA ** (-jnp.arange(0, axis_dim, 2, dtype=F32) / axis_dim)
    ar, ac = row[:, None] * inv_freq, col[:, None] * inv_freq
    cos = jnp.concatenate([jnp.cos(ar), jnp.cos(ar), jnp.cos(ac), jnp.cos(ac)], axis=-1)
    sin = jnp.concatenate([-jnp.sin(ar), jnp.sin(ar), -jnp.sin(ac), jnp.sin(ac)], axis=-1)
    return jnp.tile(cos, (1, 2)), jnp.tile(sin, (1, 2))


def _swap16(x):
    lane = lax.broadcasted_iota(jnp.int32, x.shape, 1)
    return jnp.where((lane % 32) < 16, pltpu.roll(x, LANES - 16, 1), pltpu.roll(x, 16, 1))


def _head_mean(v):
    lane = lax.broadcasted_iota(jnp.int32, v.shape, 1)
    lo = lane < HEAD_DIM
    s_all = jnp.sum(v, axis=-1, keepdims=True)
    s_lo = jnp.sum(jnp.where(lo, v, 0.0), axis=-1, keepdims=True)
    return jnp.where(lo, s_lo, s_all - s_lo) * (1.0 / HEAD_DIM)


def _norm_rope(x, gain2, cos, sin):
    r = lax.rsqrt(_head_mean(x * x) + RMS_EPS)
    nrm = x * r * gain2
    return nrm * cos + _swap16(nrm) * sin


def _norm_rope_bwd(dy, x, gain2, cos, sin):
    dn = dy * cos + _swap16(dy * sin)
    r = lax.rsqrt(_head_mean(x * x) + RMS_EPS)
    xh = x * r
    dgain = jnp.sum(dn * xh, axis=0, keepdims=True)
    u = dn * gain2
    return r * (u - xh * _head_mean(u * xh)), dgain


def _a_prep(qkv, cos, sin, gq2, gk2):
    s = qkv.shape[0]
    tr = 256
    nq, nk = A_HEADS * HEAD_DIM, A_KV * HEAD_DIM

    def body(qkv_ref, cos_ref, sin_ref, gq_ref, gk_ref, qt_ref, k_ref, v_ref):
        cos_v, sin_v = cos_ref[...], sin_ref[...]
        for c in range(nq // LANES):
            y = _norm_rope(qkv_ref[:, c * LANES:(c + 1) * LANES], gq_ref[...], cos_v, sin_v) * (QK_SCALE * LOG2E)
            yt = y.T
            qt_ref[2 * c] = yt[:HEAD_DIM].astype(BF16)
            qt_ref[2 * c + 1] = yt[HEAD_DIM:].astype(BF16)
        for c in range(nk // LANES):
            y = _norm_rope(qkv_ref[:, nq + c * LANES:nq + (c + 1) * LANES], gk_ref[...], cos_v, sin_v)
            k_ref[2 * c] = y[:, :HEAD_DIM].astype(BF16)
            k_ref[2 * c + 1] = y[:, HEAD_DIM:].astype(BF16)
            x = qkv_ref[:, nq + nk + c * LANES:nq + nk + (c + 1) * LANES]
            v_ref[2 * c] = x[:, :HEAD_DIM].astype(BF16)
            v_ref[2 * c + 1] = x[:, HEAD_DIM:].astype(BF16)

    return pl.pallas_call(
        body, name="a_prep", grid=(s // tr,),
        in_specs=[pl.BlockSpec((tr, nq + 2 * nk), lambda i: (i, 0)), pl.BlockSpec((tr, LANES), lambda i: (i, 0)),
                  pl.BlockSpec((tr, LANES), lambda i: (i, 0)), pl.BlockSpec((1, LANES), lambda i: (0, 0)),
                  pl.BlockSpec((1, LANES), lambda i: (0, 0))],
        out_specs=[pl.BlockSpec((A_HEADS, HEAD_DIM, tr), lambda i: (0, 0, i)),
                   pl.BlockSpec((A_KV, tr, HEAD_DIM), lambda i: (0, i, 0)),
                   pl.BlockSpec((A_KV, tr, HEAD_DIM), lambda i: (0, i, 0))],
        out_shape=[jax.ShapeDtypeStruct((A_HEADS, HEAD_DIM, s), BF16), jax.ShapeDtypeStruct((A_KV, s, HEAD_DIM), BF16),
                   jax.ShapeDtypeStruct((A_KV, s, HEAD_DIM), BF16)],
        compiler_params=_params(("parallel",)),
    )(qkv, cos, sin, gq2, gk2)


def _a_prep_bwd(dqt, dkt, dvt, qkv, cos, sin, gq2, gk2):
    s = qkv.shape[0]
    tr = 256
    nq, nk = A_HEADS * HEAD_DIM, A_KV * HEAD_DIM

    def body(dqt_ref, dkt_ref, dvt_ref, qkv_ref, cos_ref, sin_ref, gq_ref, gk_ref, o_ref, dgq_ref, dgk_ref):
        @pl.when(pl.program_id(0) == 0)
        def _():
            dgq_ref[...] = jnp.zeros_like(dgq_ref)
            dgk_ref[...] = jnp.zeros_like(dgk_ref)

        cos_v, sin_v = cos_ref[...], sin_ref[...]

        def pair(ref, c):
            return jnp.concatenate([ref[2 * c], ref[2 * c + 1]], axis=0).T

        for c in range(nq // LANES):
            dx, dg = _norm_rope_bwd(pair(dqt_ref, c) * QK_SCALE, qkv_ref[:, c * LANES:(c + 1) * LANES],
                                    gq_ref[...], cos_v, sin_v)
            o_ref[:, c * LANES:(c + 1) * LANES] = dx.astype(BF16)
            dgq_ref[...] += dg
        for c in range(nk // LANES):
            lo = nq + c * LANES
            dx, dg = _norm_rope_bwd(pair(dkt_ref, c) * LN2, qkv_ref[:, lo:lo + LANES], gk_ref[...], cos_v, sin_v)
            o_ref[:, lo:lo + LANES] = dx.astype(BF16)
            dgk_ref[...] += dg
            o_ref[:, lo + nk:lo + nk + LANES] = pair(dvt_ref, c).astype(BF16)

    return pl.pallas_call(
        body, name="a_prep_bwd", grid=(s // tr,),
        in_specs=[pl.BlockSpec((A_HEADS, HEAD_DIM, tr), lambda i: (0, 0, i)),
                  pl.BlockSpec((A_KV, HEAD_DIM, tr), lambda i: (0, 0, i)),
                  pl.BlockSpec((A_KV, HEAD_DIM, tr), lambda i: (0, 0, i)),
                  pl.BlockSpec((tr, nq + 2 * nk), lambda i: (i, 0)), pl.BlockSpec((tr, LANES), lambda i: (i, 0)),
                  pl.BlockSpec((tr, LANES), lambda i: (i, 0)), pl.BlockSpec((1, LANES), lambda i: (0, 0)),
                  pl.BlockSpec((1, LANES), lambda i: (0, 0))],
        out_specs=[pl.BlockSpec((tr, nq + 2 * nk), lambda i: (i, 0)), pl.BlockSpec((1, LANES), lambda i: (0, 0)),
                   pl.BlockSpec((1, LANES), lambda i: (0, 0))],
        out_shape=[jax.ShapeDtypeStruct((s, nq + 2 * nk), BF16), jax.ShapeDtypeStruct((1, LANES), F32),
                   jax.ShapeDtypeStruct((1, LANES), F32)],
        compiler_params=_params(("arbitrary",)),
    )(dqt, dkt, dvt, qkv, cos, sin, gq2, gk2)


A_TQ = 1024
A_TQ_SUB = 256
A_TQ_BWD = 1024
A_KEY_CHUNK = 512


def _a_attn_fwd(qt, k, v, cargo, *, name):
    nh, _, s = qt.shape
    rep = nh // k.shape[0]
    tq = min(A_TQ, s)
    sub = min(A_TQ_SUB, tq)
    grid = (nh, s // tq)

    def body(qt_ref, k_ref, v_ref, o_ref, lse_ref):
        scores = [jnp.dot(k_ref[0], qt_ref[0, :, a:a + sub], preferred_element_type=F32)
                  for a in range(0, tq, sub)]
        for a, st in zip(range(0, tq, sub), scores):
            mx = jnp.max(st, axis=0, keepdims=True)
            p = jnp.exp2(st - mx)
            den = jnp.sum(p, axis=0, keepdims=True)
            ot = lax.dot_general(v_ref[0], p.astype(BF16), (((0,), (0,)), ((), ())), preferred_element_type=F32)
            o_ref[0, :, a:a + sub] = (ot / den).astype(BF16)
            lse_ref[0, :, a:a + sub] = mx + jnp.log(den) * LOG2E

    carried = _carry(cargo, grid, 3, 2, body)
    res = pl.pallas_call(
        carried.body, name=name, grid=grid,
        in_specs=[pl.BlockSpec((1, HEAD_DIM, tq), lambda h, i: (h, 0, i)),
                  pl.BlockSpec((1, s, HEAD_DIM), lambda h, i: (h // rep, 0, 0)),
                  pl.BlockSpec((1, s, HEAD_DIM), lambda h, i: (h // rep, 0, 0))] + carried.in_specs,
        out_specs=[pl.BlockSpec((1, HEAD_DIM, tq), lambda h, i: (h, 0, i)),
                   pl.BlockSpec((1, 1, tq), lambda h, i: (h, 0, i))] + carried.out_specs,
        out_shape=[jax.ShapeDtypeStruct((nh, HEAD_DIM, s), BF16), jax.ShapeDtypeStruct((nh, 1, s), F32)]
        + carried.out_shape,
        scratch_shapes=carried.scratch,
        compiler_params=_params(("arbitrary", "arbitrary")),
    )(qt, k, v, *carried.args)
    return res[0], res[1], res[2:]


def _a_attn_bwd(qt, k, v, dot, ot, lse, cargo, *, name):
    nh, _, s = qt.shape
    nkv = k.shape[0]
    rep = nh // nkv
    tq, ck = min(A_TQ_BWD, s), min(A_KEY_CHUNK, s)
    grid = (nh, s // tq)

    def body(qt_ref, k_ref, v_ref, dot_ref, ot_ref, lse_ref, dq_ref, dk_ref, dv_ref):
        h, i = pl.program_id(0), pl.program_id(1)

        @pl.when((h % rep == 0) & (i == 0))
        def _():
            dk_ref[...] = jnp.zeros_like(dk_ref)
            dv_ref[...] = jnp.zeros_like(dv_ref)

        q_t, do_t, lse_v = qt_ref[0], dot_ref[0], lse_ref[0]
        delta = jnp.sum(do_t.astype(F32) * ot_ref[0].astype(F32), axis=0, keepdims=True)
        nt = (((1,), (1,)), ((), ()))
        dq = jnp.zeros((HEAD_DIM, tq), F32)
        for c in range(s // ck):
            keys = slice(c * ck, (c + 1) * ck)
            kc = k_ref[0, keys, :]
            p = jnp.exp2(jnp.dot(kc, q_t, preferred_element_type=F32) - lse_v)
            dp = jnp.dot(v_ref[0, keys, :], do_t, preferred_element_type=F32)
            ds = (p * (dp - delta)).astype(BF16)
            dv_ref[0, :, keys] += lax.dot_general(do_t, p.astype(BF16), nt, preferred_element_type=F32)
            dk_ref[0, :, keys] += lax.dot_general(q_t, ds, nt, preferred_element_type=F32)
            dq = dq + lax.dot_general(kc, ds, (((0,), (0,)), ((), ())), preferred_element_type=F32)
        dq_ref[0] = dq

    blk_q = pl.BlockSpec((1, HEAD_DIM, tq), lambda h, i: (h, 0, i))
    blk_row = pl.BlockSpec((1, 1, tq), lambda h, i: (h, 0, i))
    blk_kv = pl.BlockSpec((1, s, HEAD_DIM), lambda h, i: (h // rep, 0, 0))
    blk_acc = pl.BlockSpec((1, HEAD_DIM, s), lambda h, i: (h // rep, 0, 0))
    carried = _carry(cargo, grid, 6, 3, body)
    res = pl.pallas_call(
        carried.body, name=name, grid=grid,
        in_specs=[blk_q, blk_kv, blk_kv, blk_q, blk_q, blk_row] + carried.in_specs,
        out_specs=[blk_q, blk_acc, blk_acc] + carried.out_specs,
        out_shape=[jax.ShapeDtypeStruct((nh, HEAD_DIM, s), F32), jax.ShapeDtypeStruct((nkv, HEAD_DIM, s), F32),
                   jax.ShapeDtypeStruct((nkv, HEAD_DIM, s), F32)] + carried.out_shape,
        scratch_shapes=carried.scratch,
        compiler_params=_params(("arbitrary", "arbitrary")),
    )(qt, k, v, dot, ot, lse, *carried.args)
    return res[0], res[1], res[2], res[3:]


WIN_FAR = 1e30


class _Band(NamedTuple):
    window: int
    dil: int
    seg: int
    stride: int

    @property
    def reach(self):
        return -(-self.window // WIN_REACH) * WIN_REACH


def _win_start(i, tq, tk, s, reach):
    return pl.multiple_of(jnp.clip(i * tq - reach, 0, s - tk), LANES)


def _win_penalty(i, start, tq, tk, band):
    qpos = i * tq + lax.broadcasted_iota(jnp.int32, (tk, tq), 1)
    kpos = start + lax.broadcasted_iota(jnp.int32, (tk, tq), 0)
    dist = jnp.abs(kpos - qpos)
    seg_lo = qpos - (qpos & (band.seg - 1))
    valid = (dist <= band.window) & (kpos >= seg_lo) & (kpos < seg_lo + band.seg)
    if band.stride > 1:
        valid &= (dist & (band.stride - 1)) == 0
    return jnp.where(valid, (dist * band.dil).astype(F32), WIN_FAR)


def _win_scores(kw_t, q_t, slope, pen):
    st = lax.dot_general(kw_t, q_t, (((0,), (0,)), ((), ())), preferred_element_type=F32)
    return st * (QK_SCALE * LOG2E) - (slope * LOG2E) * pen


def _win_tq(s):
    return min(512, s)


def _win_fwd(qt, ktp, vtp, slopes, sinks, *, band, out_dtype, name):
    nh, _, s = qt.shape
    nkv = ktp.shape[0]
    rep = nh // nkv
    tq = _win_tq(s)
    tk = min(tq + 2 * band.reach, s)

    def body(*refs):
        qt_ref, kt_ref, vt_ref, sl_ref = refs[:4]
        o_ref, lse_ref, pen_ref = refs[-3:]
        i, kv = pl.program_id(0), pl.program_id(1)
        start = _win_start(i, tq, tk, s, band.reach)

        @pl.when(kv == 0)
        def _():
            pen_ref[...] = _win_penalty(i, start, tq, tk, band)

        win = pl.ds(start, tk)
        kw_t, vw_t, pen = kt_ref[0, :, win], vt_ref[0, :, win], pen_ref[...]
        for g in range(rep):
            st = _win_scores(kw_t, qt_ref[g], sl_ref[g][:, :1], pen)
            mx = jnp.max(st, axis=0, keepdims=True)
            if sinks is not None:
                sink = refs[4][g][:, :1] * LOG2E
                mx = jnp.maximum(mx, sink)
            p = jnp.exp2(st - mx)
            den = jnp.sum(p, axis=0, keepdims=True)
            if sinks is not None:
                den = den + jnp.exp2(sink - mx)
            ot = jnp.dot(vw_t, p.astype(BF16), preferred_element_type=F32)
            o_ref[g] = (ot / den).astype(o_ref.dtype)
            lse_ref[g] = mx * LN2 + jnp.log(den)

    blk_q = pl.BlockSpec((rep, HEAD_DIM, tq), lambda i, kv: (kv, 0, i))
    blk_kv = pl.BlockSpec((1, HEAD_DIM, s), lambda i, kv: (kv, 0, 0))
    blk_h = pl.BlockSpec((rep, 1, LANES), lambda i, kv: (kv, 0, 0))
    in_specs, args = [blk_q, blk_kv, blk_kv, blk_h], [qt, ktp, vtp, slopes]
    if sinks is not None:
        in_specs.append(blk_h)
        args.append(sinks)
    return pl.pallas_call(
        body, name=name, grid=(s // tq, nkv), in_specs=in_specs,
        out_specs=[blk_q, pl.BlockSpec((rep, 1, tq), lambda i, kv: (kv, 0, i))],
        out_shape=[jax.ShapeDtypeStruct((nh, HEAD_DIM, s), out_dtype), jax.ShapeDtypeStruct((nh, 1, s), F32)],
        scratch_shapes=[pltpu.VMEM((tk, tq), F32)],
        compiler_params=_params(("arbitrary", "arbitrary")),
    )(*args)


def _win_bwd(qt, ktp, vtp, slopes, sinks, dot, ot, delta, *, band, name):
    nh, _, s = qt.shape
    nkv = ktp.shape[0]
    rep = nh // nkv
    tq = _win_tq(s)
    tk = min(tq + 2 * band.reach, s)
    n_in = 6 + (sinks is not None)

    def body(*refs):
        qt_ref, kt_ref, vt_ref, sl_ref, dot_ref, aux_ref = refs[:6]
        outs, pen_ref = refs[n_in:-1], refs[-1]
        dq_ref, dk_ref, dv_ref = outs[:3]
        i, kv = pl.program_id(0), pl.program_id(1)

        @pl.when((i == 0) & (kv == 0))
        def _():
            dk_ref[...] = jnp.zeros_like(dk_ref)
            dv_ref[...] = jnp.zeros_like(dv_ref)
            if sinks is not None:
                outs[3][...] = jnp.zeros_like(outs[3])

        start = _win_start(i, tq, tk, s, band.reach)

        @pl.when(kv == 0)
        def _():
            pen_ref[...] = _win_penalty(i, start, tq, tk, band)

        win = pl.ds(start, tk)
        kw_t, vw_t, pen = kt_ref[0, :, win], vt_ref[0, :, win], pen_ref[...]
        nt = (((1,), (1,)), ((), ()))
        dk_acc = jnp.zeros((HEAD_DIM, tk), F32)
        dv_acc = jnp.zeros((HEAD_DIM, tk), F32)
        for g in range(rep):
            q_t, do_t = qt_ref[g], dot_ref[g]
            st = _win_scores(kw_t, q_t, sl_ref[g][:, :1], pen)
            mx = jnp.max(st, axis=0, keepdims=True)
            if sinks is not None:
                sink = refs[6][g][:, :1] * LOG2E
                mx = jnp.maximum(mx, sink)
            p = jnp.exp2(st - mx)
            den = jnp.sum(p, axis=0, keepdims=True)
            if sinks is not None:
                p_sink = jnp.exp2(sink - mx)
                den = den + p_sink
            p = p / den
            dp = lax.dot_general(vw_t, do_t, (((0,), (0,)), ((), ())), preferred_element_type=F32)
            if delta is None:
                row = jnp.sum(do_t.astype(F32) * aux_ref[g].astype(F32), axis=0, keepdims=True)
            else:
                row = aux_ref[g]
            ds = (p * (dp - row) * QK_SCALE).astype(BF16)
            dv_acc = dv_acc + lax.dot_general(do_t, p.astype(BF16), nt, preferred_element_type=F32)
            dk_acc = dk_acc + lax.dot_general(q_t, ds, nt, preferred_element_type=F32)
            dq_ref[g] = jnp.dot(kw_t, ds, preferred_element_type=F32)
            if sinks is not None:
                outs[3][kv * rep + g] += (jnp.zeros((1, LANES), F32)
                                          - jnp.sum(p_sink / den * row, axis=1, keepdims=True))
        dv_ref[kv, :, win] += dv_acc
        dk_ref[kv, :, win] += dk_acc

    blk_q = pl.BlockSpec((rep, HEAD_DIM, tq), lambda i, kv: (kv, 0, i))
    blk_row = pl.BlockSpec((rep, 1, tq), lambda i, kv: (kv, 0, i))
    blk_kv = pl.BlockSpec((1, HEAD_DIM, s), lambda i, kv: (kv, 0, 0))
    blk_acc = pl.BlockSpec((nkv, HEAD_DIM, s), lambda i, kv: (0, 0, 0))
    blk_h = pl.BlockSpec((rep, 1, LANES), lambda i, kv: (kv, 0, 0))
    in_specs = [blk_q, blk_kv, blk_kv, blk_h, blk_q, blk_q if delta is None else blk_row]
    args = [qt, ktp, vtp, slopes, dot, ot if delta is None else delta]
    out_specs = [blk_q, blk_acc, blk_acc]
    out_shape = [jax.ShapeDtypeStruct((nh, HEAD_DIM, s), F32), jax.ShapeDtypeStruct((nkv, HEAD_DIM, s), F32),
                 jax.ShapeDtypeStruct((nkv, HEAD_DIM, s), F32)]
    if sinks is not None:
        in_specs.append(blk_h)
        args.append(sinks)
        out_specs.append(pl.BlockSpec((nh, 1, LANES), lambda i, h: (0, 0, 0)))
        out_shape.append(jax.ShapeDtypeStruct((nh, 1, LANES), F32))
    res = pl.pallas_call(
        body, name=name, grid=(s // tq, nkv), in_specs=in_specs, out_specs=out_specs, out_shape=out_shape,
        scratch_shapes=[pltpu.VMEM((tk, tq), F32)],
        compiler_params=_params(("arbitrary", "arbitrary")),
    )(*args)
    return res if sinks is not None else (*res, None)


def _group_weights(lse):
    e = jnp.exp(lse - jnp.max(lse, axis=0, keepdims=True))
    return e / jnp.sum(e, axis=0, keepdims=True)


def _b_combine_fwd(ot, lse):
    nh, _, s = ot.shape
    ng, hg, _ = lse.shape
    ts = min(512, s)

    def body(ot_ref, lse_ref, o_ref):
        alpha = _group_weights(lse_ref[...])
        for g in range(ng):
            for j in range(hg):
                o_ref[g * hg + j] = (ot_ref[g * hg + j] * alpha[g, j:j + 1, :]).astype(BF16)

    return pl.pallas_call(
        body, name="b_combine_fwd", grid=(s // ts,),
        in_specs=[pl.BlockSpec((nh, HEAD_DIM, ts), lambda i: (0, 0, i)), pl.BlockSpec((ng, hg, ts), lambda i: (0, 0, i))],
        out_specs=pl.BlockSpec((nh, HEAD_DIM, ts), lambda i: (0, 0, i)),
        out_shape=jax.ShapeDtypeStruct((nh, HEAD_DIM, s), BF16),
        compiler_params=_params(("parallel",)),
    )(ot, lse)


def _b_combine_bwd(dout, ot, lse):
    nh, _, s = ot.shape
    ng, hg, _ = lse.shape
    ts = min(512, s)

    def body(dout_ref, ot_ref, lse_ref, do_ref, delta_ref):
        alpha = _group_weights(lse_ref[...])
        for j in range(hg):
            e = [jnp.sum(dout_ref[g * hg + j].astype(F32) * ot_ref[g * hg + j], axis=0, keepdims=True)
                 for g in range(ng)]
            a = [alpha[g, j:j + 1, :] for g in range(ng)]
            mix = a[0] * e[0]
            for g in range(1, ng):
                mix = mix + a[g] * e[g]
            for g in range(ng):
                do_ref[g * hg + j] = (dout_ref[g * hg + j].astype(F32) * a[g]).astype(BF16)
                delta_ref[g * hg + j] = a[g] * mix

    blk = pl.BlockSpec((nh, HEAD_DIM, ts), lambda i: (0, 0, i))
    return pl.pallas_call(
        body, name="b_combine_bwd", grid=(s // ts,),
        in_specs=[blk, blk, pl.BlockSpec((ng, hg, ts), lambda i: (0, 0, i))],
        out_specs=[blk, pl.BlockSpec((nh, 1, ts), lambda i: (0, 0, i))],
        out_shape=[jax.ShapeDtypeStruct((nh, HEAD_DIM, s), BF16), jax.ShapeDtypeStruct((nh, 1, s), F32)],
        compiler_params=_params(("parallel",)),
    )(dout, ot, lse)


def _alibi_slopes(n):
    return 2.0 ** (-8.0 * jnp.arange(1, n + 1, dtype=F32) / n)


def _per_head(v):
    return jnp.broadcast_to(v.astype(F32)[:, None, None], (v.shape[0], 1, LANES))


def _dilate(x, dil):
    if dil == 1:
        return x
    s = x.shape[-1]
    return jnp.swapaxes(x.reshape(x.shape[:-1] + (s // dil, dil)), -1, -2).reshape(x.shape)


def _undilate(x, dil):
    if dil == 1:
        return x
    s = x.shape[-1]
    return jnp.swapaxes(x.reshape(x.shape[:-1] + (dil, s // dil)), -1, -2).reshape(x.shape)


def _heads(x_t):
    return x_t.reshape(-1, HEAD_DIM, x_t.shape[-1])


B_MAX_STRIDE = 4


def _b_band(window, dilation, s):
    if dilation <= B_MAX_STRIDE:
        return _Band(window // 2, 1, s, dilation), 1
    return _Band(window // 2 // dilation, dilation, s // dilation, 1), dilation


def _mixer_fwd(kind, qkv, p, tabs, cargo, layer):
    s = qkv.shape[0 if kind == 0 else 1]
    if kind == 0:
        qt, k, v = _a_prep(qkv, tabs[0], tabs[1], p["gq2"], p["gk2"])
        ot, lse, brought = _a_attn_fwd(qt, k, v, cargo, name=f"a_attn_fwd_l{layer}")
        return ot.reshape(-1, s), dict(qt=qt, k=k, v=v, ot=ot, lse=lse), brought
    assert cargo is None
    if kind == 2:
        nq, nk = C_HEADS * HEAD_DIM, C_KV * HEAD_DIM
        qt = _heads(qkv[:nq])
        kp, vp = _heads(qkv[nq:nq + nk]), _heads(qkv[nq + nk:])
        ot, _ = _win_fwd(qt, kp, vp, p["slopes"], p["sinks"], band=_Band(C_WINDOW, 1, s, 1), out_dtype=BF16,
                         name="c_attn_fwd")
        return ot.reshape(-1, s), dict(qt=qt, kp=kp, vp=vp, ot=ot), ()
    ng, hg, kg = len(B_GROUPS), B_HEADS_PER_GROUP, B_KV_PER_GROUP
    nq, nk = ng * hg * HEAD_DIM, ng * kg * HEAD_DIM
    qt_all, kt_all, vt_all = _heads(qkv[:nq]), _heads(qkv[nq:nq + nk]), _heads(qkv[nq + nk:])
    saved, outs, lses = [], [], []
    for g, (window, dilation) in enumerate(B_GROUPS):
        band, dil = _b_band(window, dilation, s)
        qt = _dilate(qt_all[g * hg:(g + 1) * hg], dil)
        kp = _dilate(kt_all[g * kg:(g + 1) * kg], dil)
        vp = _dilate(vt_all[g * kg:(g + 1) * kg], dil)
        sl = p["slopes"][g * hg:(g + 1) * hg]
        ot, lse = _win_fwd(qt, kp, vp, sl, None, band=band, out_dtype=F32, name=f"b_attn_fwd_g{g}")
        saved.append(dict(qt=qt, kp=kp, vp=vp))
        outs.append(_undilate(ot, dil))
        lses.append(_undilate(lse[:, 0, :], dil))
    ot_all, lse_all = jnp.concatenate(outs, axis=0), jnp.stack(lses, axis=0)
    mixed = _b_combine_fwd(ot_all, lse_all)
    return mixed.reshape(-1, s), dict(groups=saved, ot=ot_all, lse=lse_all), ()


def _mixer_bwd(kind, do_t, qkv, sv, p, tabs, cargo, layer):
    s = do_t.shape[1]
    do_heads = _heads(do_t)
    small = {}
    if kind == 0:
        dqt, dkt, dvt, brought = _a_attn_bwd(sv["qt"], sv["k"], sv["v"], do_heads, sv["ot"], sv["lse"],
                                             cargo, name=f"a_attn_bwd_l{layer}")
        dqkv, dgq, dgk = _a_prep_bwd(dqt, dkt, dvt, qkv, tabs[0], tabs[1], p["gq2"], p["gk2"])
        small["q_gain"] = dgq[0, :HEAD_DIM] + dgq[0, HEAD_DIM:]
        small["k_gain"] = dgk[0, :HEAD_DIM] + dgk[0, HEAD_DIM:]
        return dqkv, small, brought
    assert cargo is None
    if kind == 2:
        dqt, dkt, dvt, dsink = _win_bwd(sv["qt"], sv["kp"], sv["vp"], p["slopes"], p["sinks"], do_heads,
                                        sv["ot"], None, band=_Band(C_WINDOW, 1, s, 1), name="c_attn_bwd")
        small["sinks"] = dsink[:, 0, 0]
        parts = [dqt.reshape(-1, s), dkt.reshape(-1, s), dvt.reshape(-1, s)]
        return jnp.concatenate(parts, axis=0).astype(BF16), small, ()
    ng, hg, kg = len(B_GROUPS), B_HEADS_PER_GROUP, B_KV_PER_GROUP
    do_own, delta = _b_combine_bwd(do_heads, sv["ot"], sv["lse"])
    dqs, dks, dvs = [], [], []
    for g, (window, dilation) in enumerate(B_GROUPS):
        band, dil = _b_band(window, dilation, s)
        gs = sv["groups"][g]
        dqt, dkt, dvt, _ = _win_bwd(gs["qt"], gs["kp"], gs["vp"], p["slopes"][g * hg:(g + 1) * hg], None,
                                    _dilate(do_own[g * hg:(g + 1) * hg], dil), None,
                                    _dilate(delta[g * hg:(g + 1) * hg], dil), band=band, name=f"b_attn_bwd_g{g}")
        dqs.append(_undilate(dqt, dil))
        dks.append(_undilate(dkt, dil))
        dvs.append(_undilate(dvt, dil))
    parts = [x.reshape(-1, s) for x in dqs + dks + dvs]
    return jnp.concatenate(parts, axis=0).astype(BF16), small, ()


LAYER_MATS = ("w_qkv", "w_o", "w1", "w2")
COLUMN_QUARTERS = ("w_qkv", "w1")


def _whole(key, gathered):
    q, r, c = gathered.shape
    if key == "w1":
        return gathered
    if key in COLUMN_QUARTERS:
        return jnp.transpose(gathered, (1, 0, 2)).reshape(r, q * c)
    return gathered.reshape(q * r, c)


def _quarters(key, g):
    r, c = g.shape
    if key in COLUMN_QUARTERS:
        return jnp.transpose(g.reshape(r, 4, c // 4), (1, 0, 2))
    return g.reshape(4, r // 4, c)


def _local_step(x, target, norms, mixer_params, shards, whole=None):
    s = x.shape[0]
    tabs = _rope_tables(s)
    if whole is None:
        assert MIXER_OF_LAYER[0][0] == 0
        first = _run_cargo(_gather_cargo([shards[0][key] for key in LAYER_MATS]), name="gather_l0")
        mats = {0: {key: _whole(key, g) for key, g in zip(LAYER_MATS, first)}}
        later = _gather_cargo([shards[layer][key] for layer in range(1, DEPTH) for key in LAYER_MATS])
    else:
        mats, later = dict(enumerate(whole)), None
    h = x
    saved = []
    for layer in range(DEPTH):
        kind = layer % N_MIXERS
        w, p = mats[layer], mixer_params[layer]
        hn, qkv = _norm_mm(h, norms["attn"][layer][None], w["w_qkv"], out_dtype=F32 if kind == 0 else BF16,
                           relu2=False, transpose_out=kind != 0, name=f"qkv_proj_l{layer}")
        o_t, sv, brought = _mixer_fwd(kind, qkv, p, tabs, later if layer == 0 else None, layer)
        for n, g in enumerate(brought):
            mats.setdefault(1 + n // len(LAYER_MATS), {})[LAYER_MATS[n % len(LAYER_MATS)]] = _whole(
                LAYER_MATS[n % len(LAYER_MATS)], g)
        h_mid = _mm_res(o_t, w["w_o"], h, a_transposed=True, name=f"o_proj_l{layer}")
        hn2, act = _norm_mm(h_mid, norms["mlp"][layer][None], w["w1"], out_dtype=BF16, relu2=True,
                            transpose_out=False, name=f"mlp_up_l{layer}")
        h_out = _mm_res(act, w["w2"], h_mid, a_transposed=False, name=f"mlp_down_l{layer}")
        saved.append(dict(h=h, hn=hn, qkv=qkv, o_t=o_t, mix=sv, h_mid=h_mid, hn2=hn2, act=act))
        h = h_out

    dh, loss, d_final = _loss_head(h, norms["final"][None], target)

    own, received, pending = {}, {}, []
    d_attn, d_mlp, small = [None] * DEPTH, [None] * DEPTH, [None] * DEPTH
    for layer in reversed(range(DEPTH)):
        kind = layer % N_MIXERS
        w, p, sv = mats[layer], mixer_params[layer], saved[layer]
        du = _mm_nt(dh, w["w2"], sv["act"], transpose_out=False, name=f"mlp_down_bwd_l{layer}")
        own[layer, "w2"] = _quarters("w2", _mm_tn(sv["act"], dh, x_transposed=False, g_transposed=False,
                                                  column_quarters=False, name=f"mlp_w2_grad_l{layer}"))
        own[layer, "w1"] = _mm_tn(sv["hn2"], du, x_transposed=False, g_transposed=False, column_quarters=True,
                                  name=f"mlp_w1_grad_l{layer}")
        dh_mid, d_mlp[layer] = _mm_nt_normbwd(du, w["w1"], sv["h_mid"], norms["mlp"][layer][None], dh,
                                              g_transposed=False, name=f"mlp_up_bwd_l{layer}")
        do_t = _mm_nt(dh_mid, w["w_o"], None, transpose_out=True, name=f"o_proj_bwd_l{layer}")
        own[layer, "w_o"] = _quarters("w_o", _mm_tn(sv["o_t"], dh_mid, x_transposed=True, g_transposed=False,
                                                    column_quarters=False, name=f"w_o_grad_l{layer}"))
        pending += [(layer, "w2"), (layer, "w1"), (layer, "w_o")]
        cargo = None
        if kind == 0 and whole is None:
            cargo, sent, pending = _scatter_cargo([own[item] for item in pending], None), pending, []
        dqkv, small[layer], brought = _mixer_bwd(kind, do_t, sv["qkv"], sv["mix"], p, tabs, cargo, layer)
        if cargo is not None:
            received.update(zip(sent, brought))
        own[layer, "w_qkv"] = _quarters("w_qkv", _mm_tn(sv["hn"], dqkv, x_transposed=False, g_transposed=kind != 0,
                                                        column_quarters=False, name=f"w_qkv_grad_l{layer}"))
        pending.append((layer, "w_qkv"))
        dh, d_attn[layer] = _mm_nt_normbwd(dqkv, w["w_qkv"], sv["h"], norms["attn"][layer][None], dh_mid,
                                           g_transposed=kind != 0, name=f"qkv_proj_bwd_l{layer}")
    return loss, dh, own, received, pending, dict(attn=d_attn, mlp=d_mlp, final=d_final, mixer=small)


CHIP_FLIPS = ((1, 0), (0, 1), (1, 1))


class _Cargo(NamedTuple):
    ins: tuple
    out_shape: tuple
    sem_shapes: tuple
    start: Callable
    wait: Callable


class _Carried(NamedTuple):
    body: Callable
    in_specs: list
    out_specs: list
    out_shape: list
    scratch: list
    args: tuple


def _carry(cargo, grid, n_in, n_out, body):
    if cargo is None:
        return _Carried(body, [], [], [], [], ())
    ci, co = len(cargo.ins), len(cargo.out_shape)

    def wrapped(*refs):
        ins, c_ins = refs[:n_in], refs[n_in:n_in + ci]
        outs, c_outs = refs[n_in + ci:n_in + ci + n_out], refs[n_in + ci + n_out:n_in + ci + n_out + co]
        sems = refs[n_in + ci + n_out + co:]
        first = last = None
        for axis, extent in enumerate(grid):
            at = pl.program_id(axis)
            first = (at == 0) if first is None else first & (at == 0)
            last = (at == extent - 1) if last is None else last & (at == extent - 1)

        @pl.when(first)
        def _():
            cargo.start(c_ins, c_outs, sems)

        body(*ins, *outs)

        @pl.when(last)
        def _():
            cargo.wait(c_ins, c_outs, sems)

    return _Carried(wrapped, [ANY] * ci, [ANY] * co, list(cargo.out_shape), list(cargo.sem_shapes), tuple(cargo.ins))


def _run_cargo(cargo, *, name):
    ci, co = len(cargo.ins), len(cargo.out_shape)

    def body(*refs):
        cargo.start(refs[:ci], refs[ci:ci + co], refs[ci + co:])
        cargo.wait(refs[:ci], refs[ci:ci + co], refs[ci + co:])

    return pl.pallas_call(body, name=name, in_specs=[ANY] * ci, out_specs=[ANY] * co, out_shape=list(cargo.out_shape),
                          scratch_shapes=list(cargo.sem_shapes))(*cargo.ins)


def _other_chip(x, y, j):
    fx, fy = CHIP_FLIPS[j]
    return (1 - x if fx else x), (1 - y if fy else y)


def _gather_cargo(shards):
    n = len(shards)
    halves = [a.shape[0] // 2 for a in shards]

    def copies(ins, outs, sems):
        ici_send, ici_recv, d2d_send, d2d_recv, local_sems = sems
        x, y, c = lax.axis_index("x"), lax.axis_index("y"), lax.axis_index("c")
        me = 2 * x + y

        def half(t, which):
            return pl.ds(pl.multiple_of(which * halves[t], 16), halves[t])

        def over_ici(t, j, arriving):
            px, py = _other_chip(x, y, j)
            return pltpu.make_async_remote_copy(
                src_ref=ins[t].at[half(t, c)], dst_ref=outs[t].at[2 * px + py if arriving else me, half(t, c)],
                send_sem=ici_send.at[t, j], recv_sem=ici_recv.at[t, j], device_id=(px, py, c), device_id_type=MESH)

        def over_d2d(t, j, arriving):
            px, py = _other_chip(x, y, j)
            mine = outs[t].at[2 * px + py, half(t, c)]
            return pltpu.make_async_remote_copy(
                src_ref=mine, dst_ref=outs[t].at[2 * px + py, half(t, 1 - c)] if arriving else mine,
                send_sem=d2d_send.at[t, j], recv_sem=d2d_recv.at[t, j], device_id=(x, y, 1 - c), device_id_type=MESH)

        return over_ici, over_d2d, lambda t: pltpu.make_async_copy(ins[t], outs[t].at[me], local_sems.at[t])

    def start(ins, outs, sems):
        over_ici, _, own = copies(ins, outs, sems)
        for t in range(n):
            own(t).start()
            for j in range(len(CHIP_FLIPS)):
                over_ici(t, j, False).start()

    def wait(ins, outs, sems):
        over_ici, over_d2d, own = copies(ins, outs, sems)
        for t in range(n):
            for j in range(len(CHIP_FLIPS)):
                over_ici(t, j, True).wait_recv()
                over_d2d(t, j, False).start()
        for t in range(n):
            for j in range(len(CHIP_FLIPS)):
                over_d2d(t, j, True).wait_recv()
                over_d2d(t, j, False).wait_send()
                over_ici(t, j, False).wait_send()
            own(t).wait()

    dma = pltpu.SemaphoreType.DMA
    return _Cargo(tuple(shards), tuple(jax.ShapeDtypeStruct((4,) + a.shape, a.dtype) for a in shards),
                  (dma((n, 3)), dma((n, 3)), dma((n, 3)), dma((n, 3)), dma((n,))), start, wait)


def _scatter_cargo(grads, small):
    n = len(grads)

    def copies(ins, outs, sems):
        x, y, c = lax.axis_index("x"), lax.axis_index("y"), lax.axis_index("c")
        me = 4 * x + 2 * y + c

        def remote(t, j):
            px, py = _other_chip(x, y, j)
            return pltpu.make_async_remote_copy(
                src_ref=ins[t].at[2 * px + py], dst_ref=outs[t].at[j], send_sem=sems[0].at[t, j],
                recv_sem=sems[1].at[t, j], device_id=(px, py, c), device_id_type=MESH)

        def small_remote(r, arriving):
            fx, fy, fc = (r + 1) // 4, ((r + 1) // 2) % 2, (r + 1) % 2
            px, py, pc = (1 - x if fx else x), (1 - y if fy else y), (1 - c if fc else c)
            return pltpu.make_async_remote_copy(
                src_ref=ins[n], dst_ref=outs[n].at[4 * px + 2 * py + pc if arriving else me],
                send_sem=sems[2].at[r], recv_sem=sems[3].at[r], device_id=(px, py, pc), device_id_type=MESH)

        return remote, small_remote, lambda: pltpu.make_async_copy(ins[n], outs[n].at[me], sems[4])

    def start(ins, outs, sems):
        remote, small_remote, small_own = copies(ins, outs, sems)
        if small is not None:
            small_own().start()
            for r in range(7):
                small_remote(r, False).start()
        for t in range(n):
            for j in range(len(CHIP_FLIPS)):
                remote(t, j).start()

    def wait(ins, outs, sems):
        remote, small_remote, small_own = copies(ins, outs, sems)
        if small is not None:
            for r in range(7):
                small_remote(r, True).wait_recv()
                small_remote(r, False).wait_send()
            small_own().wait()
        for t in range(n):
            for j in range(len(CHIP_FLIPS)):
                remote(t, j).wait()

    dma = pltpu.SemaphoreType.DMA
    ins = tuple(grads) + (() if small is None else (small,))
    out_shape = tuple(jax.ShapeDtypeStruct((3,) + g.shape[1:], g.dtype) for g in grads)
    sem_shapes = (dma((n, 3)), dma((n, 3)))
    if small is not None:
        out_shape += (jax.ShapeDtypeStruct((8,) + small.shape, small.dtype),)
        sem_shapes += (dma((7,)), dma((7,)), dma(()))
    return _Cargo(ins, out_shape, sem_shapes, start, wait)


def _swap_cores(parts):
    n = len(parts)

    def body(*refs):
        ins, outs = refs[:n], refs[n:2 * n]
        send_sems, recv_sems = refs[2 * n:]
        peer = (lax.axis_index("x"), lax.axis_index("y"), 1 - lax.axis_index("c"))
        copies = [pltpu.make_async_remote_copy(src_ref=ins[t], dst_ref=outs[t], send_sem=send_sems.at[t],
                                               recv_sem=recv_sems.at[t], device_id=peer, device_id_type=MESH)
                  for t in range(n)]
        for cp in copies:
            cp.start()
        for cp in copies:
            cp.wait()

    return pl.pallas_call(
        body, name="swap_cores", in_specs=[ANY] * n, out_specs=[ANY] * n,
        out_shape=[jax.ShapeDtypeStruct(a.shape, a.dtype) for a in parts],
        scratch_shapes=[pltpu.SemaphoreType.DMA((n,)), pltpu.SemaphoreType.DMA((n,))],
    )(*parts)


def _rows_tile(r):
    return 256 if r % 256 == 0 else r


def _sum_quarters(own, recv, *, name):
    r, c = own.shape
    tr = _rows_tile(r)

    def body(own_ref, recv_ref, o_ref):
        acc = own_ref[...].astype(F32)
        for j in range(3):
            acc = acc + recv_ref[j].astype(F32)
        o_ref[...] = acc

    return pl.pallas_call(
        body, name=name, grid=(r // tr,),
        in_specs=[pl.BlockSpec((tr, c), lambda i: (i, 0)), pl.BlockSpec((3, tr, c), lambda i: (0, i, 0))],
        out_specs=pl.BlockSpec((tr, c), lambda i: (i, 0)),
        out_shape=jax.ShapeDtypeStruct((r, c), F32),
        compiler_params=_params(("parallel",)),
    )(own, recv)


def _adamw(w, m, v, parts, *, name):
    r, c = w.shape
    tr = _rows_tile(r)
    c1, c2 = 1.0 - ADAM_B1 ** ADAM_STEP, 1.0 - ADAM_B2 ** ADAM_STEP
    n_parts = len(parts)

    def body(*refs):
        w_ref, m_ref, v_ref = refs[:3]
        g_ref, d_ref, nm_ref, nv_ref = refs[3 + n_parts:]
        terms = []
        for p_ref in refs[3:3 + n_parts]:
            terms += [p_ref[...]] if len(p_ref.shape) == 2 else [p_ref[j] for j in range(p_ref.shape[0])]
        g = terms[0]
        for term in terms[1:]:
            g = g + term
        m_new = ADAM_B1 * m_ref[...] + (1.0 - ADAM_B1) * g
        v_new = ADAM_B2 * v_ref[...] + (1.0 - ADAM_B2) * (g * g)
        step = (m_new / c1) / (jnp.sqrt(v_new / c2) + ADAM_EPS)
        g_ref[...] = g
        d_ref[...] = -ADAM_LR * (step + ADAM_WD * w_ref[...])
        nm_ref[...] = m_new
        nv_ref[...] = v_new

    blk = pl.BlockSpec((tr, c), lambda i: (i, 0))
    part_specs = [blk if p.ndim == 2 else pl.BlockSpec((p.shape[0], tr, c), lambda i: (0, i, 0)) for p in parts]
    return pl.pallas_call(
        body, name=name, grid=(r // tr,), in_specs=[blk, blk, blk] + part_specs,
        out_specs=[blk] * 4, out_shape=[jax.ShapeDtypeStruct((r, c), F32)] * 4,
        compiler_params=_params(("parallel",)),
    )(w, m, v, *parts)


MATS = ("a_w_qkv", "a_w_o", "b_w_qkv", "b_w_o", "c_w_qkv", "c_w_o", "mlp_w1", "mlp_w2")
SMALLS = ("attn_norm", "mlp_norm", "a_q_gain", "a_k_gain", "c_sinks", "final_norm")
WEIGHTS = ("attn_norm", "mlp_norm", "a_w_qkv", "a_q_gain", "a_k_gain", "a_w_o", "b_w_qkv", "b_w_o", "c_w_qkv",
           "c_sinks", "c_w_o", "mlp_w1", "mlp_w2", "final_norm")
MIXER_OF_LAYER = tuple((layer % N_MIXERS, sum(1 for q in range(layer) if q % N_MIXERS == layer % N_MIXERS))
                       for layer in range(DEPTH))
SMALL_ROWS = 8


def _pack_small(values):
    rows, spans, at = [], [], 0
    for v in values:
        flat = v.reshape(-1)
        n = -(-flat.shape[0] // (SMALL_ROWS * LANES)) * SMALL_ROWS
        rows.append(jnp.pad(flat, (0, n * LANES - flat.shape[0])).reshape(n, LANES))
        spans.append((at, n))
        at += n
    return jnp.concatenate(rows, axis=0), spans


def kernel(x, attn_norm, mlp_norm, a_w_qkv, a_q_gain, a_k_gain, a_w_o, b_w_qkv, b_w_o, c_w_qkv, c_sinks, c_w_o, mlp_w1, mlp_w2, final_norm, loss_target, m_attn_norm, m_mlp_norm, m_a_w_qkv, m_a_q_gain, m_a_k_gain, m_a_w_o, m_b_w_qkv, m_b_w_o, m_c_w_qkv, m_c_sinks, m_c_w_o, m_mlp_w1, m_mlp_w2, m_final_norm, v_attn_norm, v_mlp_norm, v_a_w_qkv, v_a_q_gain, v_a_k_gain, v_a_w_o, v_b_w_qkv, v_b_w_o, v_c_w_qkv, v_c_sinks, v_c_w_o, v_mlp_w1, v_mlp_w2, v_final_norm):
    env = dict(locals())
    w = {name: env[name] for name in WEIGHTS}
    mom = {name: (env["m_" + name], env["v_" + name]) for name in WEIGHTS}

    prefix = ("a", "b", "c")
    shards, mixer_params = [], []
    for layer, (kind, j) in enumerate(MIXER_OF_LAYER):
        shards.append(dict(w_qkv=w[prefix[kind] + "_w_qkv"][j].astype(BF16), w_o=w[prefix[kind] + "_w_o"][j].astype(BF16),
                           w1=mlp_w1[layer].astype(BF16), w2=mlp_w2[layer].astype(BF16)))
        if kind == 0:
            mixer_params.append(dict(gq2=jnp.tile(a_q_gain[j], 2)[None], gk2=jnp.tile(a_k_gain[j], 2)[None]))
        elif kind == 1:
            mixer_params.append(dict(slopes=_per_head(_alibi_slopes(len(B_GROUPS) * B_HEADS_PER_GROUP))))
        else:
            mixer_params.append(dict(slopes=_per_head(_alibi_slopes(C_HEADS)), sinks=_per_head(c_sinks[j])))

    norms = dict(attn=attn_norm, mlp=mlp_norm, final=final_norm)
    loss_part, grad_x, own, received, pending, g_small = _local_step(x[0], loss_target[0], norms, mixer_params, shards)
    loss = lax.psum(loss_part[0, 0], ("x", "y", "c"))

    of_kind = lambda kind, key: jnp.stack([g_small["mixer"][layer][key] for layer, (k, _) in enumerate(MIXER_OF_LAYER)
                                           if k == kind])
    small_grads = dict(
        attn_norm=jnp.concatenate(g_small["attn"], axis=0), mlp_norm=jnp.concatenate(g_small["mlp"], axis=0),
        a_q_gain=of_kind(0, "q_gain"), a_k_gain=of_kind(0, "k_gain"), c_sinks=of_kind(2, "sinks"),
        final_norm=g_small["final"][0])
    packed, spans = _pack_small([small_grads[name] for name in SMALLS])
    *last, all_small = _run_cargo(_scatter_cargo([own[item] for item in pending], packed), name="scatter_last")
    received.update(zip(pending, last))

    me_chip = 2 * lax.axis_index("x") + lax.axis_index("y")
    partial = []
    for name in MATS:
        key = name[2:] if name[0] in "abc" else name[4:]
        layers = [layer for layer, (kind, _) in enumerate(MIXER_OF_LAYER)
                  if name.startswith("mlp") or prefix[kind] == name[0]]
        sums = [_sum_quarters(lax.dynamic_index_in_dim(own[layer, key], me_chip, axis=0, keepdims=False),
                              received[layer, key], name=f"sum_{name}_l{layer}") for layer in layers]
        partial.append(jnp.concatenate(sums, axis=0))
    other = _swap_cores(partial)

    out = {}
    for name, mine, theirs in zip(MATS, partial, other):
        shape = w[name].shape
        res = _adamw(*[a.reshape(-1, shape[-1]) for a in (w[name], *mom[name])], [mine, theirs], name=f"adamw_{name}")
        out[name] = [a.reshape(shape) for a in res]
    for name, (at, n) in zip(SMALLS, spans):
        shape = w[name].shape
        packed_in = [_pack_small([a])[0] for a in (w[name], *mom[name])]
        res = _adamw(*packed_in, [all_small[:, at:at + n]], name=f"adamw_{name}")
        out[name] = [a.reshape(-1)[:w[name].size].reshape(shape) for a in res]

    return (loss, grad_x[None], *[out[name][0] for name in WEIGHTS], *[out[name][1] for name in WEIGHTS],
            *[out[name][2] for name in WEIGHTS], *[out[name][3] for name in WEIGHTS])
```

```python
from typing import Callable, NamedTuple

import jax
import jax.numpy as jnp
from jax import lax
from jax.experimental import pallas as pl
from jax.experimental.pallas import tpu as pltpu

F32 = jnp.float32
BF16 = jnp.bfloat16
MESH = pl.DeviceIdType.MESH
ANY = pl.BlockSpec(memory_space=pl.ANY)

D_MODEL = 1024
HEAD_DIM = 64
GRID_W = 64
ROPE_THETA = 10000.0
RMS_EPS = 1e-6
QK_SCALE = HEAD_DIM ** -0.5
LOG2E = 1.4426950408889634
LN2 = 0.6931471805599453
A_HEADS, A_KV = 16, 4
B_GROUPS = ((128, 1), (512, 4), (2048, 16))
B_HEADS_PER_GROUP, B_KV_PER_GROUP = 6, 2
C_HEADS, C_KV, C_WINDOW = 16, 4, 128
DEPTH, N_MIXERS = 4, 3
ADAM_LR, ADAM_B1, ADAM_B2, ADAM_EPS, ADAM_WD, ADAM_STEP = 0.001, 0.9, 0.999, 1e-08, 0.01, 10

WIN_REACH = 128
V7X_VMEM_BUDGET = 48 * 1024 * 1024
LANES = 128
ROW_TILE = 1024


def _params(semantics):
    return pltpu.CompilerParams(dimension_semantics=semantics, vmem_limit_bytes=V7X_VMEM_BUDGET)


def _tile(n, cap):
    if n <= cap:
        return n
    t = (cap // LANES) * LANES
    while n % t:
        t -= LANES
    return t


def _norm_mm(h, gain, w, *, out_dtype, relu2, transpose_out, name):
    m, d = h.shape
    by_quarter = w.ndim == 3
    assert not (by_quarter and transpose_out)
    n = w.shape[-1] * (4 if by_quarter else 1)
    per_step = 2 if by_quarter and 2 * w.shape[-1] <= 2048 else 1
    tm, tn = min(ROW_TILE, m), (per_step * w.shape[-1] if by_quarter else _tile(n, 2048))
    w_spec = (pl.BlockSpec((per_step, d, tn // per_step), lambda i, j: (j, 0, 0)) if by_quarter
              else pl.BlockSpec((d, tn), lambda i, j: (0, j)))
    y_spec = (pl.BlockSpec((tn, tm), lambda i, j: (j, i)) if transpose_out
              else pl.BlockSpec((tm, tn), lambda i, j: (i, j)))

    def body(h_ref, g_ref, w_ref, hn_ref, y_ref):
        @pl.when(pl.program_id(1) == 0)
        def _():
            x = h_ref[...]
            r = lax.rsqrt(jnp.mean(x * x, axis=-1, keepdims=True) + RMS_EPS)
            hn_ref[...] = (x * r * g_ref[...]).astype(BF16)

        def finish(y):
            if relu2:
                y = jnp.maximum(y, 0.0)
                y = y * y
            return y.astype(y_ref.dtype)

        if transpose_out:
            y_ref[...] = finish(lax.dot_general(w_ref[...], hn_ref[...], (((0,), (1,)), ((), ())),
                                                preferred_element_type=F32))
        elif by_quarter:
            cols = tn // per_step
            for q in range(per_step):
                y_ref[:, q * cols:(q + 1) * cols] = finish(jnp.dot(hn_ref[...], w_ref[q], preferred_element_type=F32))
        else:
            y_ref[...] = finish(jnp.dot(hn_ref[...], w_ref[...], preferred_element_type=F32))

    return pl.pallas_call(
        body, name=name, grid=(m // tm, n // tn),
        in_specs=[pl.BlockSpec((tm, d), lambda i, j: (i, 0)), pl.BlockSpec((1, d), lambda i, j: (0, 0)), w_spec],
        out_specs=[pl.BlockSpec((tm, d), lambda i, j: (i, 0)), y_spec],
        out_shape=[jax.ShapeDtypeStruct((m, d), BF16), jax.ShapeDtypeStruct((n, m) if transpose_out else (m, n), out_dtype)],
        compiler_params=_params(("parallel", "arbitrary")),
    )(h, gain, w)


def _mm_res(a, w, h_in, *, a_transposed, name):
    k, d = w.shape
    m = h_in.shape[0]
    tm, tk = min(ROW_TILE, m), _tile(k, 2048)
    lhs_contracts = 0 if a_transposed else 1

    def body(a_ref, w_ref, h_ref, o_ref):
        @pl.when(pl.program_id(1) == 0)
        def _():
            o_ref[...] = h_ref[...]

        o_ref[...] += lax.dot_general(a_ref[...], w_ref[...], (((lhs_contracts,), (0,)), ((), ())),
                                      preferred_element_type=F32)

    a_spec = (pl.BlockSpec((tk, tm), lambda i, j: (j, i)) if a_transposed
              else pl.BlockSpec((tm, tk), lambda i, j: (i, j)))
    return pl.pallas_call(
        body, name=name, grid=(m // tm, k // tk),
        in_specs=[a_spec, pl.BlockSpec((tk, d), lambda i, j: (j, 0)), pl.BlockSpec((tm, d), lambda i, j: (i, 0))],
        out_specs=pl.BlockSpec((tm, d), lambda i, j: (i, 0)),
        out_shape=jax.ShapeDtypeStruct((m, d), F32),
        compiler_params=_params(("parallel", "arbitrary")),
    )(a, w, h_in)


def _mm_nt(a, w, act, *, transpose_out, name):
    m, d = a.shape
    n = w.shape[0]
    tm, tn = min(ROW_TILE, m), _tile(n, 1152)
    assert act is None or not transpose_out
    nt = (((1,), (1,)), ((), ()))

    def body(*refs):
        a_ref, w_ref = refs[0], refs[1]
        o_ref = refs[-1]
        if transpose_out:
            acc = lax.dot_general(w_ref[...], a_ref[...].astype(BF16), nt, preferred_element_type=F32)
        else:
            acc = lax.dot_general(a_ref[...].astype(BF16), w_ref[...], nt, preferred_element_type=F32)
        if act is not None:
            acc = acc * (2.0 * jnp.sqrt(refs[2][...].astype(F32)))
        o_ref[...] = acc.astype(BF16)

    in_specs = [pl.BlockSpec((tm, d), lambda i, j: (i, 0)), pl.BlockSpec((tn, d), lambda i, j: (j, 0))]
    args = [a, w]
    if act is not None:
        in_specs.append(pl.BlockSpec((tm, tn), lambda i, j: (i, j)))
        args.append(act)
    out_spec = (pl.BlockSpec((tn, tm), lambda i, j: (j, i)) if transpose_out
                else pl.BlockSpec((tm, tn), lambda i, j: (i, j)))
    return pl.pallas_call(
        body, name=name, grid=(m // tm, n // tn), in_specs=in_specs, out_specs=out_spec,
        out_shape=jax.ShapeDtypeStruct((n, m) if transpose_out else (m, n), BF16),
        compiler_params=_params(("parallel", "parallel")),
    )(*args)


def _rmsnorm_bwd(dn, x, gain):
    r = lax.rsqrt(jnp.mean(x * x, axis=-1, keepdims=True) + RMS_EPS)
    xh = x * r
    dgain = jnp.sum(dn * xh, axis=0, keepdims=True)
    u = dn * gain
    dx = r * (u - xh * jnp.mean(u * xh, axis=-1, keepdims=True))
    return dx, dgain


def _mm_nt_normbwd(g, w, h, gain, dh_in, *, g_transposed, name):
    m, k = g.shape[::-1] if g_transposed else g.shape
    by_quarter = w.ndim == 3
    d = w.shape[-2]
    tm, tk = min(ROW_TILE, m), (w.shape[-1] if by_quarter else _tile(k, 1024))
    sub = min(256, tm)
    nk = k // tk
    w_spec = (pl.BlockSpec((None, d, tk), lambda i, j: (j, 0, 0)) if by_quarter
              else pl.BlockSpec((d, tk), lambda i, j: (0, j)))

    def body(g_ref, w_ref, h_ref, gain_ref, dh_ref, o_ref, dg_ref, acc_ref):
        i, j = pl.program_id(0), pl.program_id(1)

        @pl.when((i == 0) & (j == 0))
        def _():
            dg_ref[...] = jnp.zeros_like(dg_ref)

        @pl.when(j == 0)
        def _():
            acc_ref[...] = jnp.zeros_like(acc_ref)

        acc_ref[...] += lax.dot_general(g_ref[...], w_ref[...], (((0 if g_transposed else 1,), (1,)), ((), ())),
                                        preferred_element_type=F32)

        @pl.when(j == nk - 1)
        def _():
            for r in range(0, tm, sub):
                rows = slice(r, r + sub)
                dx, dgain = _rmsnorm_bwd(acc_ref[rows, :], h_ref[rows, :], gain_ref[...])
                dg_ref[...] += dgain
                o_ref[rows, :] = dh_ref[rows, :] + dx

    return pl.pallas_call(
        body, name=name, grid=(m // tm, nk),
        in_specs=[pl.BlockSpec((tk, tm), lambda i, j: (j, i)) if g_transposed
                  else pl.BlockSpec((tm, tk), lambda i, j: (i, j)), w_spec,
                  pl.BlockSpec((tm, d), lambda i, j: (i, 0)), pl.BlockSpec((1, d), lambda i, j: (0, 0)),
                  pl.BlockSpec((tm, d), lambda i, j: (i, 0))],
        out_specs=[pl.BlockSpec((tm, d), lambda i, j: (i, 0)), pl.BlockSpec((1, d), lambda i, j: (0, 0))],
        out_shape=[jax.ShapeDtypeStruct((m, d), F32), jax.ShapeDtypeStruct((1, d), F32)],
        scratch_shapes=[pltpu.VMEM((tm, d), F32)],
        compiler_params=_params(("arbitrary", "arbitrary")),
    )(g, w, h, gain, dh_in)


def _mm_tn(x, g, *, x_transposed, g_transposed, column_quarters, name):
    k, m = x.shape if x_transposed else x.shape[::-1]
    n = g.shape[0] if g_transposed else g.shape[1]
    tm, tk, tn = min(2 * ROW_TILE, m), _tile(k, 1152), (n // 4 if column_quarters else _tile(n, 1024))
    nm = m // tm
    lhs_contracts, rhs_contracts = (1 if x_transposed else 0), (1 if g_transposed else 0)
    g_spec = (pl.BlockSpec((tn, tm), lambda a, b, s: (b, s)) if g_transposed
              else pl.BlockSpec((tm, tn), lambda a, b, s: (s, b)))

    def body(x_ref, g_ref, o_ref, acc_ref):
        s = pl.program_id(2)

        @pl.when(s == 0)
        def _():
            acc_ref[...] = jnp.zeros_like(acc_ref)

        acc_ref[...] += lax.dot_general(x_ref[...], g_ref[...].astype(BF16),
                                        (((lhs_contracts,), (rhs_contracts,)), ((), ())), preferred_element_type=F32)

        @pl.when(s == nm - 1)
        def _():
            o_ref[...] = acc_ref[...].astype(BF16)

    x_spec = (pl.BlockSpec((tk, tm), lambda a, b, s: (a, s)) if x_transposed
              else pl.BlockSpec((tm, tk), lambda a, b, s: (s, a)))
    out_spec = (pl.BlockSpec((None, tk, tn), lambda a, b, s: (b, a, 0)) if column_quarters
                else pl.BlockSpec((tk, tn), lambda a, b, s: (a, b)))
    return pl.pallas_call(
        body, name=name, grid=(k // tk, n // tn, nm),
        in_specs=[x_spec, g_spec], out_specs=out_spec,
        out_shape=jax.ShapeDtypeStruct((4, k, n // 4) if column_quarters else (k, n), BF16),
        scratch_shapes=[pltpu.VMEM((tk, tn), F32)],
        compiler_params=_params(("parallel", "parallel", "arbitrary")),
    )(x, g)


def _loss_head(h, gain, target):
    m, d = h.shape
    tm = 512

    def body(h_ref, g_ref, t_ref, dh_ref, loss_ref, dg_ref):
        @pl.when(pl.program_id(0) == 0)
        def _():
            loss_ref[...] = jnp.zeros_like(loss_ref)
            dg_ref[...] = jnp.zeros_like(dg_ref)

        x = h_ref[...]
        gain_v = g_ref[...]
        r = lax.rsqrt(jnp.mean(x * x, axis=-1, keepdims=True) + RMS_EPS)
        err = x * r * gain_v - t_ref[...]
        loss_ref[...] += 0.5 * jnp.sum(jnp.mean(err * err, axis=-1, keepdims=True), axis=0, keepdims=True)
        dx, dgain = _rmsnorm_bwd(err * (1.0 / d), x, gain_v)
        dg_ref[...] += dgain
        dh_ref[...] = dx

    return pl.pallas_call(
        body, name="loss_head", grid=(m // tm,),
        in_specs=[pl.BlockSpec((tm, d), lambda i: (i, 0)), pl.BlockSpec((1, d), lambda i: (0, 0)),
                  pl.BlockSpec((tm, d), lambda i: (i, 0))],
        out_specs=[pl.BlockSpec((tm, d), lambda i: (i, 0)), pl.BlockSpec((1, LANES), lambda i: (0, 0)),
                   pl.BlockSpec((1, d), lambda i: (0, 0))],
        out_shape=[jax.ShapeDtypeStruct((m, d), F32), jax.ShapeDtypeStruct((1, LANES), F32),
                   jax.ShapeDtypeStruct((1, d), F32)],
        compiler_params=_params(("arbitrary",)),
    )(h, gain, target)


def _rope_tables(s):
    t = jnp.arange(s)
    row = (t // GRID_W).astype(F32)
    col = (t % GRID_W).astype(F32)
    axis_dim = HEAD_DIM // 2
    inv_freq = ROPE_THETA ** (-jnp.arange(0, axis_dim, 2, dtype=F32) / axis_dim)
    ar, ac = row[:, None] * inv_freq, col[:, None] * inv_freq
    cos = jnp.concatenate([jnp.cos(ar), jnp.cos(ar), jnp.cos(ac), jnp.cos(ac)], axis=-1)
    sin = jnp.concatenate([-jnp.sin(ar), jnp.sin(ar), -jnp.sin(ac), jnp.sin(ac)], axis=-1)
    return jnp.tile(cos, (1, 2)), jnp.tile(sin, (1, 2))


def _swap16(x):
    lane = lax.broadcasted_iota(jnp.int32, x.shape, 1)
    return jnp.where((lane % 32) < 16, pltpu.roll(x, LANES - 16, 1), pltpu.roll(x, 16, 1))


def _head_mean(v):
    lane = lax.broadcasted_iota(jnp.int32, v.shape, 1)
    lo = lane < HEAD_DIM
    s_all = jnp.sum(v, axis=-1, keepdims=True)
    s_lo = jnp.sum(jnp.where(lo, v, 0.0), axis=-1, keepdims=True)
    return jnp.where(lo, s_lo, s_all - s_lo) * (1.0 / HEAD_DIM)


def _norm_rope(x, gain2, cos, sin):
    r = lax.rsqrt(_head_mean(x * x) + RMS_EPS)
    nrm = x * r * gain2
    return nrm * cos + _swap16(nrm) * sin


def _norm_rope_bwd(dy, x, gain2, cos, sin):
    dn = dy * cos + _swap16(dy * sin)
    r = lax.rsqrt(_head_mean(x * x) + RMS_EPS)
    xh = x * r
    dgain = jnp.sum(dn * xh, axis=0, keepdims=True)
    u = dn * gain2
    return r * (u - xh * _head_mean(u * xh)), dgain


def _a_prep(qkv, cos, sin, gq2, gk2):
    s = qkv.shape[0]
    tr = 256
    nq, nk = A_HEADS * HEAD_DIM, A_KV * HEAD_DIM

    def body(qkv_ref, cos_ref, sin_ref, gq_ref, gk_ref, qt_ref, k_ref, v_ref):
        cos_v, sin_v = cos_ref[...], sin_ref[...]
        for c in range(nq // LANES):
            y = _norm_rope(qkv_ref[:, c * LANES:(c + 1) * LANES], gq_ref[...], cos_v, sin_v) * (QK_SCALE * LOG2E)
            yt = y.T
            qt_ref[2 * c] = yt[:HEAD_DIM].astype(BF16)
            qt_ref[2 * c + 1] = yt[HEAD_DIM:].astype(BF16)
        for c in range(nk // LANES):
            y = _norm_rope(qkv_ref[:, nq + c * LANES:nq + (c + 1) * LANES], gk_ref[...], cos_v, sin_v)
            k_ref[2 * c] = y[:, :HEAD_DIM].astype(BF16)
            k_ref[2 * c + 1] = y[:, HEAD_DIM:].astype(BF16)
            x = qkv_ref[:, nq + nk + c * LANES:nq + nk + (c + 1) * LANES]
            v_ref[2 * c] = x[:, :HEAD_DIM].astype(BF16)
            v_ref[2 * c + 1] = x[:, HEAD_DIM:].astype(BF16)

    return pl.pallas_call(
        body, name="a_prep", grid=(s // tr,),
        in_specs=[pl.BlockSpec((tr, nq + 2 * nk), lambda i: (i, 0)), pl.BlockSpec((tr, LANES), lambda i: (i, 0)),
                  pl.BlockSpec((tr, LANES), lambda i: (i, 0)), pl.BlockSpec((1, LANES), lambda i: (0, 0)),
                  pl.BlockSpec((1, LANES), lambda i: (0, 0))],
        out_specs=[pl.BlockSpec((A_HEADS, HEAD_DIM, tr), lambda i: (0, 0, i)),
                   pl.BlockSpec((A_KV, tr, HEAD_DIM), lambda i: (0, i, 0)),
                   pl.BlockSpec((A_KV, tr, HEAD_DIM), lambda i: (0, i, 0))],
        out_shape=[jax.ShapeDtypeStruct((A_HEADS, HEAD_DIM, s), BF16), jax.ShapeDtypeStruct((A_KV, s, HEAD_DIM), BF16),
                   jax.ShapeDtypeStruct((A_KV, s, HEAD_DIM), BF16)],
        compiler_params=_params(("parallel",)),
    )(qkv, cos, sin, gq2, gk2)


def _a_prep_bwd(dqt, dkt, dvt, qkv, cos, sin, gq2, gk2):
    s = qkv.shape[0]
    tr = 256
    nq, nk = A_HEADS * HEAD_DIM, A_KV * HEAD_DIM

    def body(dqt_ref, dkt_ref, dvt_ref, qkv_ref, cos_ref, sin_ref, gq_ref, gk_ref, o_ref, dgq_ref, dgk_ref):
        @pl.when(pl.program_id(0) == 0)
        def _():
            dgq_ref[...] = jnp.zeros_like(dgq_ref)
            dgk_ref[...] = jnp.zeros_like(dgk_ref)

        cos_v, sin_v = cos_ref[...], sin_ref[...]

        def pair(ref, c):
            return jnp.concatenate([ref[2 * c], ref[2 * c + 1]], axis=0).T

        for c in range(nq // LANES):
            dx, dg = _norm_rope_bwd(pair(dqt_ref, c) * QK_SCALE, qkv_ref[:, c * LANES:(c + 1) * LANES],
                                    gq_ref[...], cos_v, sin_v)
            o_ref[:, c * LANES:(c + 1) * LANES] = dx.astype(BF16)
            dgq_ref[...] += dg
        for c in range(nk // LANES):
            lo = nq + c * LANES
            dx, dg = _norm_rope_bwd(pair(dkt_ref, c) * LN2, qkv_ref[:, lo:lo + LANES], gk_ref[...], cos_v, sin_v)
            o_ref[:, lo:lo + LANES] = dx.astype(BF16)
            dgk_ref[...] += dg
            o_ref[:, lo + nk:lo + nk + LANES] = pair(dvt_ref, c).astype(BF16)

    return pl.pallas_call(
        body, name="a_prep_bwd", grid=(s // tr,),
        in_specs=[pl.BlockSpec((A_HEADS, HEAD_DIM, tr), lambda i: (0, 0, i)),
                  pl.BlockSpec((A_KV, HEAD_DIM, tr), lambda i: (0, 0, i)),
                  pl.BlockSpec((A_KV, HEAD_DIM, tr), lambda i: (0, 0, i)),
                  pl.BlockSpec((tr, nq + 2 * nk), lambda i: (i, 0)), pl.BlockSpec((tr, LANES), lambda i: (i, 0)),
                  pl.BlockSpec((tr, LANES), lambda i: (i, 0)), pl.BlockSpec((1, LANES), lambda i: (0, 0)),
                  pl.BlockSpec((1, LANES), lambda i: (0, 0))],
        out_specs=[pl.BlockSpec((tr, nq + 2 * nk), lambda i: (i, 0)), pl.BlockSpec((1, LANES), lambda i: (0, 0)),
                   pl.BlockSpec((1, LANES), lambda i: (0, 0))],
        out_shape=[jax.ShapeDtypeStruct((s, nq + 2 * nk), BF16), jax.ShapeDtypeStruct((1, LANES), F32),
                   jax.ShapeDtypeStruct((1, LANES), F32)],
        compiler_params=_params(("arbitrary",)),
    )(dqt, dkt, dvt, qkv, cos, sin, gq2, gk2)


A_TQ = 1024
A_TQ_SUB = 256
A_TQ_BWD = 1024
A_KEY_CHUNK = 512


def _a_attn_fwd(qt, k, v, cargo, *, name):
    nh, _, s = qt.shape
    rep = nh // k.shape[0]
    tq = min(A_TQ, s)
    sub = min(A_TQ_SUB, tq)
    grid = (nh, s // tq)

    def body(qt_ref, k_ref, v_ref, o_ref, lse_ref):
        scores = [jnp.dot(k_ref[0], qt_ref[0, :, a:a + sub], preferred_element_type=F32)
                  for a in range(0, tq, sub)]
        for a, st in zip(range(0, tq, sub), scores):
            mx = jnp.max(st, axis=0, keepdims=True)
            p = jnp.exp2(st - mx)
            den = jnp.sum(p, axis=0, keepdims=True)
            ot = lax.dot_general(v_ref[0], p.astype(BF16), (((0,), (0,)), ((), ())), preferred_element_type=F32)
            o_ref[0, :, a:a + sub] = (ot / den).astype(BF16)
            lse_ref[0, :, a:a + sub] = mx + jnp.log(den) * LOG2E

    carried = _carry(cargo, grid, 3, 2, body)
    res = pl.pallas_call(
        carried.body, name=name, grid=grid,
        in_specs=[pl.BlockSpec((1, HEAD_DIM, tq), lambda h, i: (h, 0, i)),
                  pl.BlockSpec((1, s, HEAD_DIM), lambda h, i: (h // rep, 0, 0)),
                  pl.BlockSpec((1, s, HEAD_DIM), lambda h, i: (h // rep, 0, 0))] + carried.in_specs,
        out_specs=[pl.BlockSpec((1, HEAD_DIM, tq), lambda h, i: (h, 0, i)),
                   pl.BlockSpec((1, 1, tq), lambda h, i: (h, 0, i))] + carried.out_specs,
        out_shape=[jax.ShapeDtypeStruct((nh, HEAD_DIM, s), BF16), jax.ShapeDtypeStruct((nh, 1, s), F32)]
        + carried.out_shape,
        scratch_shapes=carried.scratch,
        compiler_params=_params(("arbitrary", "arbitrary")),
    )(qt, k, v, *carried.args)
    return res[0], res[1], res[2:]


def _a_attn_bwd(qt, k, v, dot, ot, lse, cargo, *, name):
    nh, _, s = qt.shape
    nkv = k.shape[0]
    rep = nh // nkv
    tq, ck = min(A_TQ_BWD, s), min(A_KEY_CHUNK, s)
    grid = (nh, s // tq)

    def body(qt_ref, k_ref, v_ref, dot_ref, ot_ref, lse_ref, dq_ref, dk_ref, dv_ref):
        h, i = pl.program_id(0), pl.program_id(1)

        @pl.when((h % rep == 0) & (i == 0))
        def _():
            dk_ref[...] = jnp.zeros_like(dk_ref)
            dv_ref[...] = jnp.zeros_like(dv_ref)

        q_t, do_t, lse_v = qt_ref[0], dot_ref[0], lse_ref[0]
        delta = jnp.sum(do_t.astype(F32) * ot_ref[0].astype(F32), axis=0, keepdims=True)
        nt = (((1,), (1,)), ((), ()))
        dq = jnp.zeros((HEAD_DIM, tq), F32)
        for c in range(s // ck):
            keys = slice(c * ck, (c + 1) * ck)
            kc = k_ref[0, keys, :]
            p = jnp.exp2(jnp.dot(kc, q_t, preferred_element_type=F32) - lse_v)
            dp = jnp.dot(v_ref[0, keys, :], do_t, preferred_element_type=F32)
            ds = (p * (dp - delta)).astype(BF16)
            dv_ref[0, :, keys] += lax.dot_general(do_t, p.astype(BF16), nt, preferred_element_type=F32)
            dk_ref[0, :, keys] += lax.dot_general(q_t, ds, nt, preferred_element_type=F32)
            dq = dq + lax.dot_general(kc, ds, (((0,), (0,)), ((), ())), preferred_element_type=F32)
        dq_ref[0] = dq

    blk_q = pl.BlockSpec((1, HEAD_DIM, tq), lambda h, i: (h, 0, i))
    blk_row = pl.BlockSpec((1, 1, tq), lambda h, i: (h, 0, i))
    blk_kv = pl.BlockSpec((1, s, HEAD_DIM), lambda h, i: (h // rep, 0, 0))
    blk_acc = pl.BlockSpec((1, HEAD_DIM, s), lambda h, i: (h // rep, 0, 0))
    carried = _carry(cargo, grid, 6, 3, body)
    res = pl.pallas_call(
        carried.body, name=name, grid=grid,
        in_specs=[blk_q, blk_kv, blk_kv, blk_q, blk_q, blk_row] + carried.in_specs,
        out_specs=[blk_q, blk_acc, blk_acc] + carried.out_specs,
        out_shape=[jax.ShapeDtypeStruct((nh, HEAD_DIM, s), F32), jax.ShapeDtypeStruct((nkv, HEAD_DIM, s), F32),
                   jax.ShapeDtypeStruct((nkv, HEAD_DIM, s), F32)] + carried.out_shape,
        scratch_shapes=carried.scratch,
        compiler_params=_params(("arbitrary", "arbitrary")),
    )(qt, k, v, dot, ot, lse, *carried.args)
    return res[0], res[1], res[2], res[3:]


WIN_FAR = 1e30


class _Band(NamedTuple):
    window: int
    dil: int
    seg: int
    stride: int

    @property
    def reach(self):
        return -(-self.window // WIN_REACH) * WIN_REACH


def _win_start(i, tq, tk, s, reach):
    return pl.multiple_of(jnp.clip(i * tq - reach, 0, s - tk), LANES)


def _win_penalty(i, start, tq, tk, band):
    qpos = i * tq + lax.broadcasted_iota(jnp.int32, (tk, tq), 1)
    kpos = start + lax.broadcasted_iota(jnp.int32, (tk, tq), 0)
    dist = jnp.abs(kpos - qpos)
    seg_lo = qpos - (qpos & (band.seg - 1))
    valid = (dist <= band.window) & (kpos >= seg_lo) & (kpos < seg_lo + band.seg)
    if band.stride > 1:
        valid &= (dist & (band.stride - 1)) == 0
    return jnp.where(valid, (dist * band.dil).astype(F32), WIN_FAR)


def _win_scores(kw_t, q_t, slope, pen):
    st = lax.dot_general(kw_t, q_t, (((0,), (0,)), ((), ())), preferred_element_type=F32)
    return st * (QK_SCALE * LOG2E) - (slope * LOG2E) * pen


def _win_tq(s):
    return min(512, s)


def _win_fwd(qt, ktp, vtp, slopes, sinks, *, band, out_dtype, name):
    nh, _, s = qt.shape
    nkv = ktp.shape[0]
    rep = nh // nkv
    tq = _win_tq(s)
    tk = min(tq + 2 * band.reach, s)

    def body(*refs):
        qt_ref, kt_ref, vt_ref, sl_ref = refs[:4]
        o_ref, lse_ref, pen_ref = refs[-3:]
        i, kv = pl.program_id(0), pl.program_id(1)
        start = _win_start(i, tq, tk, s, band.reach)

        @pl.when(kv == 0)
        def _():
            pen_ref[...] = _win_penalty(i, start, tq, tk, band)

        win = pl.ds(start, tk)
        kw_t, vw_t, pen = kt_ref[0, :, win], vt_ref[0, :, win], pen_ref[...]
        for g in range(rep):
            st = _win_scores(kw_t, qt_ref[g], sl_ref[g][:, :1], pen)
            mx = jnp.max(st, axis=0, keepdims=True)
            if sinks is not None:
                sink = refs[4][g][:, :1] * LOG2E
                mx = jnp.maximum(mx, sink)
            p = jnp.exp2(st - mx)
            den = jnp.sum(p, axis=0, keepdims=True)
            if sinks is not None:
                den = den + jnp.exp2(sink - mx)
            ot = jnp.dot(vw_t, p.astype(BF16), preferred_element_type=F32)
            o_ref[g] = (ot / den).astype(o_ref.dtype)
            lse_ref[g] = mx * LN2 + jnp.log(den)

    blk_q = pl.BlockSpec((rep, HEAD_DIM, tq), lambda i, kv: (kv, 0, i))
    blk_kv = pl.BlockSpec((1, HEAD_DIM, s), lambda i, kv: (kv, 0, 0))
    blk_h = pl.BlockSpec((rep, 1, LANES), lambda i, kv: (kv, 0, 0))
    in_specs, args = [blk_q, blk_kv, blk_kv, blk_h], [qt, ktp, vtp, slopes]
    if sinks is not None:
        in_specs.append(blk_h)
        args.append(sinks)
    return pl.pallas_call(
        body, name=name, grid=(s // tq, nkv), in_specs=in_specs,
        out_specs=[blk_q, pl.BlockSpec((rep, 1, tq), lambda i, kv: (kv, 0, i))],
        out_shape=[jax.ShapeDtypeStruct((nh, HEAD_DIM, s), out_dtype), jax.ShapeDtypeStruct((nh, 1, s), F32)],
        scratch_shapes=[pltpu.VMEM((tk, tq), F32)],
        compiler_params=_params(("arbitrary", "arbitrary")),
    )(*args)


def _win_bwd(qt, ktp, vtp, slopes, sinks, dot, ot, delta, *, band, name):
    nh, _, s = qt.shape
    nkv = ktp.shape[0]
    rep = nh // nkv
    tq = _win_tq(s)
    tk = min(tq + 2 * band.reach, s)
    n_in = 6 + (sinks is not None)

    def body(*refs):
        qt_ref, kt_ref, vt_ref, sl_ref, dot_ref, aux_ref = refs[:6]
        outs, pen_ref = refs[n_in:-1], refs[-1]
        dq_ref, dk_ref, dv_ref = outs[:3]
        i, kv = pl.program_id(0), pl.program_id(1)

        @pl.when((i == 0) & (kv == 0))
        def _():
            dk_ref[...] = jnp.zeros_like(dk_ref)
            dv_ref[...] = jnp.zeros_like(dv_ref)
            if sinks is not None:
                outs[3][...] = jnp.zeros_like(outs[3])

        start = _win_start(i, tq, tk, s, band.reach)

        @pl.when(kv == 0)
        def _():
            pen_ref[...] = _win_penalty(i, start, tq, tk, band)

        win = pl.ds(start, tk)
        kw_t, vw_t, pen = kt_ref[0, :, win], vt_ref[0, :, win], pen_ref[...]
        nt = (((1,), (1,)), ((), ()))
        dk_acc = jnp.zeros((HEAD_DIM, tk), F32)
        dv_acc = jnp.zeros((HEAD_DIM, tk), F32)
        for g in range(rep):
            q_t, do_t = qt_ref[g], dot_ref[g]
            st = _win_scores(kw_t, q_t, sl_ref[g][:, :1], pen)
            mx = jnp.max(st, axis=0, keepdims=True)
            if sinks is not None:
                sink = refs[6][g][:, :1] * LOG2E
                mx = jnp.maximum(mx, sink)
            p = jnp.exp2(st - mx)
            den = jnp.sum(p, axis=0, keepdims=True)
            if sinks is not None:
                p_sink = jnp.exp2(sink - mx)
                den = den + p_sink
            p = p / den
            dp = lax.dot_general(vw_t, do_t, (((0,), (0,)), ((), ())), preferred_element_type=F32)
            if delta is None:
                row = jnp.sum(do_t.astype(F32) * aux_ref[g].astype(F32), axis=0, keepdims=True)
            else:
                row = aux_ref[g]
            ds = (p * (dp - row) * QK_SCALE).astype(BF16)
            dv_acc = dv_acc + lax.dot_general(do_t, p.astype(BF16), nt, preferred_element_type=F32)
            dk_acc = dk_acc + lax.dot_general(q_t, ds, nt, preferred_element_type=F32)
            dq_ref[g] = jnp.dot(kw_t, ds, preferred_element_type=F32)
            if sinks is not None:
                outs[3][kv * rep + g] += (jnp.zeros((1, LANES), F32)
                                          - jnp.sum(p_sink / den * row, axis=1, keepdims=True))
        dv_ref[kv, :, win] += dv_acc
        dk_ref[kv, :, win] += dk_acc

    blk_q = pl.BlockSpec((rep, HEAD_DIM, tq), lambda i, kv: (kv, 0, i))
    blk_row = pl.BlockSpec((rep, 1, tq), lambda i, kv: (kv, 0, i))
    blk_kv = pl.BlockSpec((1, HEAD_DIM, s), lambda i, kv: (kv, 0, 0))
    blk_acc = pl.BlockSpec((nkv, HEAD_DIM, s), lambda i, kv: (0, 0, 0))
    blk_h = pl.BlockSpec((rep, 1, LANES), lambda i, kv: (kv, 0, 0))
    in_specs = [blk_q, blk_kv, blk_kv, blk_h, blk_q, blk_q if delta is None else blk_row]
    args = [qt, ktp, vtp, slopes, dot, ot if delta is None else delta]
    out_specs = [blk_q, blk_acc, blk_acc]
    out_shape = [jax.ShapeDtypeStruct((nh, HEAD_DIM, s), F32), jax.ShapeDtypeStruct((nkv, HEAD_DIM, s), F32),
                 jax.ShapeDtypeStruct((nkv, HEAD_DIM, s), F32)]
    if sinks is not None:
        in_specs.append(blk_h)
        args.append(sinks)
        out_specs.append(pl.BlockSpec((nh, 1, LANES), lambda i, h: (0, 0, 0)))
        out_shape.append(jax.ShapeDtypeStruct((nh, 1, LANES), F32))
    res = pl.pallas_call(
        body, name=name, grid=(s // tq, nkv), in_specs=in_specs, out_specs=out_specs, out_shape=out_shape,
        scratch_shapes=[pltpu.VMEM((tk, tq), F32)],
        compiler_params=_params(("arbitrary", "arbitrary")),
    )(*args)
    return res if sinks is not None else (*res, None)


def _group_weights(lse):
    e = jnp.exp(lse - jnp.max(lse, axis=0, keepdims=True))
    return e / jnp.sum(e, axis=0, keepdims=True)


def _b_combine_fwd(ot, lse):
    nh, _, s = ot.shape
    ng, hg, _ = lse.shape
    ts = min(512, s)

    def body(ot_ref, lse_ref, o_ref):
        alpha = _group_weights(lse_ref[...])
        for g in range(ng):
            for j in range(hg):
                o_ref[g * hg + j] = (ot_ref[g * hg + j] * alpha[g, j:j + 1, :]).astype(BF16)

    return pl.pallas_call(
        body, name="b_combine_fwd", grid=(s // ts,),
        in_specs=[pl.BlockSpec((nh, HEAD_DIM, ts), lambda i: (0, 0, i)), pl.BlockSpec((ng, hg, ts), lambda i: (0, 0, i))],
        out_specs=pl.BlockSpec((nh, HEAD_DIM, ts), lambda i: (0, 0, i)),
        out_shape=jax.ShapeDtypeStruct((nh, HEAD_DIM, s), BF16),
        compiler_params=_params(("parallel",)),
    )(ot, lse)


def _b_combine_bwd(dout, ot, lse):
    nh, _, s = ot.shape
    ng, hg, _ = lse.shape
    ts = min(512, s)

    def body(dout_ref, ot_ref, lse_ref, do_ref, delta_ref):
        alpha = _group_weights(lse_ref[...])
        for j in range(hg):
            e = [jnp.sum(dout_ref[g * hg + j].astype(F32) * ot_ref[g * hg + j], axis=0, keepdims=True)
                 for g in range(ng)]
            a = [alpha[g, j:j + 1, :] for g in range(ng)]
            mix = a[0] * e[0]
            for g in range(1, ng):
                mix = mix + a[g] * e[g]
            for g in range(ng):
                do_ref[g * hg + j] = (dout_ref[g * hg + j].astype(F32) * a[g]).astype(BF16)
                delta_ref[g * hg + j] = a[g] * mix

    blk = pl.BlockSpec((nh, HEAD_DIM, ts), lambda i: (0, 0, i))
    return pl.pallas_call(
        body, name="b_combine_bwd", grid=(s // ts,),
        in_specs=[blk, blk, pl.BlockSpec((ng, hg, ts), lambda i: (0, 0, i))],
        out_specs=[blk, pl.BlockSpec((nh, 1, ts), lambda i: (0, 0, i))],
        out_shape=[jax.ShapeDtypeStruct((nh, HEAD_DIM, s), BF16), jax.ShapeDtypeStruct((nh, 1, s), F32)],
        compiler_params=_params(("parallel",)),
    )(dout, ot, lse)


def _alibi_slopes(n):
    return 2.0 ** (-8.0 * jnp.arange(1, n + 1, dtype=F32) / n)


def _per_head(v):
    return jnp.broadcast_to(v.astype(F32)[:, None, None], (v.shape[0], 1, LANES))


def _dilate(x, dil):
    if dil == 1:
        return x
    s = x.shape[-1]
    return jnp.swapaxes(x.reshape(x.shape[:-1] + (s // dil, dil)), -1, -2).reshape(x.shape)


def _undilate(x, dil):
    if dil == 1:
        return x
    s = x.shape[-1]
    return jnp.swapaxes(x.reshape(x.shape[:-1] + (dil, s // dil)), -1, -2).reshape(x.shape)


def _heads(x_t):
    return x_t.reshape(-1, HEAD_DIM, x_t.shape[-1])


B_MAX_STRIDE = 4


def _b_band(window, dilation, s):
    if dilation <= B_MAX_STRIDE:
        return _Band(window // 2, 1, s, dilation), 1
    return _Band(window // 2 // dilation, dilation, s // dilation, 1), dilation


def _mixer_fwd(kind, qkv, p, tabs, cargo, layer):
    s = qkv.shape[0 if kind == 0 else 1]
    if kind == 0:
        qt, k, v = _a_prep(qkv, tabs[0], tabs[1], p["gq2"], p["gk2"])
        ot, lse, brought = _a_attn_fwd(qt, k, v, cargo, name=f"a_attn_fwd_l{layer}")
        return ot.reshape(-1, s), dict(qt=qt, k=k, v=v, ot=ot, lse=lse), brought
    assert cargo is None
    if kind == 2:
        nq, nk = C_HEADS * HEAD_DIM, C_KV * HEAD_DIM
        qt = _heads(qkv[:nq])
        kp, vp = _heads(qkv[nq:nq + nk]), _heads(qkv[nq + nk:])
        ot, _ = _win_fwd(qt, kp, vp, p["slopes"], p["sinks"], band=_Band(C_WINDOW, 1, s, 1), out_dtype=BF16,
                         name="c_attn_fwd")
        return ot.reshape(-1, s), dict(qt=qt, kp=kp, vp=vp, ot=ot), ()
    ng, hg, kg = len(B_GROUPS), B_HEADS_PER_GROUP, B_KV_PER_GROUP
    nq, nk = ng * hg * HEAD_DIM, ng * kg * HEAD_DIM
    qt_all, kt_all, vt_all = _heads(qkv[:nq]), _heads(qkv[nq:nq + nk]), _heads(qkv[nq + nk:])
    saved, outs, lses = [], [], []
    for g, (window, dilation) in enumerate(B_GROUPS):
        band, dil = _b_band(window, dilation, s)
        qt = _dilate(qt_all[g * hg:(g + 1) * hg], dil)
        kp = _dilate(kt_all[g * kg:(g + 1) * kg], dil)
        vp = _dilate(vt_all[g * kg:(g + 1) * kg], dil)
        sl = p["slopes"][g * hg:(g + 1) * hg]
        ot, lse = _win_fwd(qt, kp, vp, sl, None, band=band, out_dtype=F32, name=f"b_attn_fwd_g{g}")
        saved.append(dict(qt=qt, kp=kp, vp=vp))
        outs.append(_undilate(ot, dil))
        lses.append(_undilate(lse[:, 0, :], dil))
    ot_all, lse_all = jnp.concatenate(outs, axis=0), jnp.stack(lses, axis=0)
    mixed = _b_combine_fwd(ot_all, lse_all)
    return mixed.reshape(-1, s), dict(groups=saved, ot=ot_all, lse=lse_all), ()


def _mixer_bwd(kind, do_t, qkv, sv, p, tabs, cargo, layer):
    s = do_t.shape[1]
    do_heads = _heads(do_t)
    small = {}
    if kind == 0:
        dqt, dkt, dvt, brought = _a_attn_bwd(sv["qt"], sv["k"], sv["v"], do_heads, sv["ot"], sv["lse"],
                                             cargo, name=f"a_attn_bwd_l{layer}")
        dqkv, dgq, dgk = _a_prep_bwd(dqt, dkt, dvt, qkv, tabs[0], tabs[1], p["gq2"], p["gk2"])
        small["q_gain"] = dgq[0, :HEAD_DIM] + dgq[0, HEAD_DIM:]
        small["k_gain"] = dgk[0, :HEAD_DIM] + dgk[0, HEAD_DIM:]
        return dqkv, small, brought
    assert cargo is None
    if kind == 2:
        dqt, dkt, dvt, dsink = _win_bwd(sv["qt"], sv["kp"], sv["vp"], p["slopes"], p["sinks"], do_heads,
                                        sv["ot"], None, band=_Band(C_WINDOW, 1, s, 1), name="c_attn_bwd")
        small["sinks"] = dsink[:, 0, 0]
        parts = [dqt.reshape(-1, s), dkt.reshape(-1, s), dvt.reshape(-1, s)]
        return jnp.concatenate(parts, axis=0).astype(BF16), small, ()
    ng, hg, kg = len(B_GROUPS), B_HEADS_PER_GROUP, B_KV_PER_GROUP
    do_own, delta = _b_combine_bwd(do_heads, sv["ot"], sv["lse"])
    dqs, dks, dvs = [], [], []
    for g, (window, dilation) in enumerate(B_GROUPS):
        band, dil = _b_band(window, dilation, s)
        gs = sv["groups"][g]
        dqt, dkt, dvt, _ = _win_bwd(gs["qt"], gs["kp"], gs["vp"], p["slopes"][g * hg:(g + 1) * hg], None,
                                    _dilate(do_own[g * hg:(g + 1) * hg], dil), None,
                                    _dilate(delta[g * hg:(g + 1) * hg], dil), band=band, name=f"b_attn_bwd_g{g}")
        dqs.append(_undilate(dqt, dil))
        dks.append(_undilate(dkt, dil))
        dvs.append(_undilate(dvt, dil))
    parts = [x.reshape(-1, s) for x in dqs + dks + dvs]
    return jnp.concatenate(parts, axis=0).astype(BF16), small, ()


LAYER_MATS = ("w_qkv", "w_o", "w1", "w2")
COLUMN_QUARTERS = ("w_qkv", "w1")


def _whole(key, gathered):
    q, r, c = gathered.shape
    if key == "w1":
        return gathered
    if key in COLUMN_QUARTERS:
        return jnp.transpose(gathered, (1, 0, 2)).reshape(r, q * c)
    return gathered.reshape(q * r, c)


def _quarters(key, g):
    r, c = g.shape
    if key in COLUMN_QUARTERS:
        return jnp.transpose(g.reshape(r, 4, c // 4), (1, 0, 2))
    return g.reshape(4, r // 4, c)


def _local_step(x, target, norms, mixer_params, shards, whole=None):
    s = x.shape[0]
    tabs = _rope_tables(s)
    if whole is None:
        assert MIXER_OF_LAYER[0][0] == 0
        first = _run_cargo(_gather_cargo([shards[0][key] for key in LAYER_MATS]), name="gather_l0")
        mats = {0: {key: _whole(key, g) for key, g in zip(LAYER_MATS, first)}}
        later = _gather_cargo([shards[layer][key] for layer in range(1, DEPTH) for key in LAYER_MATS])
    else:
        mats, later = dict(enumerate(whole)), None
    h = x
    saved = []
    for layer in range(DEPTH):
        kind = layer % N_MIXERS
        w, p = mats[layer], mixer_params[layer]
        hn, qkv = _norm_mm(h, norms["attn"][layer][None], w["w_qkv"], out_dtype=F32 if kind == 0 else BF16,
                           relu2=False, transpose_out=kind != 0, name=f"qkv_proj_l{layer}")
        o_t, sv, brought = _mixer_fwd(kind, qkv, p, tabs, later if layer == 0 else None, layer)
        for n, g in enumerate(brought):
            mats.setdefault(1 + n // len(LAYER_MATS), {})[LAYER_MATS[n % len(LAYER_MATS)]] = _whole(
                LAYER_MATS[n % len(LAYER_MATS)], g)
        h_mid = _mm_res(o_t, w["w_o"], h, a_transposed=True, name=f"o_proj_l{layer}")
        hn2, act = _norm_mm(h_mid, norms["mlp"][layer][None], w["w1"], out_dtype=BF16, relu2=True,
                            transpose_out=False, name=f"mlp_up_l{layer}")
        h_out = _mm_res(act, w["w2"], h_mid, a_transposed=False, name=f"mlp_down_l{layer}")
        saved.append(dict(h=h, hn=hn, qkv=qkv, o_t=o_t, mix=sv, h_mid=h_mid, hn2=hn2, act=act))
        h = h_out

    dh, loss, d_final = _loss_head(h, norms["final"][None], target)

    own, received, pending = {}, {}, []
    d_attn, d_mlp, small = [None] * DEPTH, [None] * DEPTH, [None] * DEPTH
    for layer in reversed(range(DEPTH)):
        kind = layer % N_MIXERS
        w, p, sv = mats[layer], mixer_params[layer], saved[layer]
        du = _mm_nt(dh, w["w2"], sv["act"], transpose_out=False, name=f"mlp_down_bwd_l{layer}")
        own[layer, "w2"] = _quarters("w2", _mm_tn(sv["act"], dh, x_transposed=False, g_transposed=False,
                                                  column_quarters=False, name=f"mlp_w2_grad_l{layer}"))
        own[layer, "w1"] = _mm_tn(sv["hn2"], du, x_transposed=False, g_transposed=False, column_quarters=True,
                                  name=f"mlp_w1_grad_l{layer}")
        dh_mid, d_mlp[layer] = _mm_nt_normbwd(du, w["w1"], sv["h_mid"], norms["mlp"][layer][None], dh,
                                              g_transposed=False, name=f"mlp_up_bwd_l{layer}")
        do_t = _mm_nt(dh_mid, w["w_o"], None, transpose_out=True, name=f"o_proj_bwd_l{layer}")
        own[layer, "w_o"] = _quarters("w_o", _mm_tn(sv["o_t"], dh_mid, x_transposed=True, g_transposed=False,
                                                    column_quarters=False, name=f"w_o_grad_l{layer}"))
        pending += [(layer, "w2"), (layer, "w1"), (layer, "w_o")]
        cargo = None
        if kind == 0 and whole is None:
            cargo, sent, pending = _scatter_cargo([own[item] for item in pending], None), pending, []
        dqkv, small[layer], brought = _mixer_bwd(kind, do_t, sv["qkv"], sv["mix"], p, tabs, cargo, layer)
        if cargo is not None:
            received.update(zip(sent, brought))
        own[layer, "w_qkv"] = _quarters("w_qkv", _mm_tn(sv["hn"], dqkv, x_transposed=False, g_transposed=kind != 0,
                                                        column_quarters=False, name=f"w_qkv_grad_l{layer}"))
        pending.append((layer, "w_qkv"))
        dh, d_attn[layer] = _mm_nt_normbwd(dqkv, w["w_qkv"], sv["h"], norms["attn"][layer][None], dh_mid,
                                           g_transposed=kind != 0, name=f"qkv_proj_bwd_l{layer}")
    return loss, dh, own, received, pending, dict(attn=d_attn, mlp=d_mlp, final=d_final, mixer=small)


CHIP_FLIPS = ((1, 0), (0, 1), (1, 1))


class _Cargo(NamedTuple):
    ins: tuple
    out_shape: tuple
    sem_shapes: tuple
    start: Callable
    wait: Callable


class _Carried(NamedTuple):
    body: Callable
    in_specs: list
    out_specs: list
    out_shape: list
    scratch: list
    args: tuple


def _carry(cargo, grid, n_in, n_out, body):
    if cargo is None:
        return _Carried(body, [], [], [], [], ())
    ci, co = len(cargo.ins), len(cargo.out_shape)

    def wrapped(*refs):
        ins, c_ins = refs[:n_in], refs[n_in:n_in + ci]
        outs, c_outs = refs[n_in + ci:n_in + ci + n_out], refs[n_in + ci + n_out:n_in + ci + n_out + co]
        sems = refs[n_in + ci + n_out + co:]
        first = last = None
        for axis, extent in enumerate(grid):
            at = pl.program_id(axis)
            first = (at == 0) if first is None else first & (at == 0)
            last = (at == extent - 1) if last is None else last & (at == extent - 1)

        @pl.when(first)
        def _():
            cargo.start(c_ins, c_outs, sems)

        body(*ins, *outs)

        @pl.when(last)
        def _():
            cargo.wait(c_ins, c_outs, sems)

    return _Carried(wrapped, [ANY] * ci, [ANY] * co, list(cargo.out_shape), list(cargo.sem_shapes), tuple(cargo.ins))


def _run_cargo(cargo, *, name):
    ci, co = len(cargo.ins), len(cargo.out_shape)

    def body(*refs):
        cargo.start(refs[:ci], refs[ci:ci + co], refs[ci + co:])
        cargo.wait(refs[:ci], refs[ci:ci + co], refs[ci + co:])

    return pl.pallas_call(body, name=name, in_specs=[ANY] * ci, out_specs=[ANY] * co, out_shape=list(cargo.out_shape),
                          scratch_shapes=list(cargo.sem_shapes))(*cargo.ins)


def _other_chip(x, y, j):
    fx, fy = CHIP_FLIPS[j]
    return (1 - x if fx else x), (1 - y if fy else y)


def _gather_cargo(shards):
    n = len(shards)
    halves = [a.shape[0] // 2 for a in shards]

    def copies(ins, outs, sems):
        ici_send, ici_recv, d2d_send, d2d_recv, local_sems = sems
        x, y, c = lax.axis_index("x"), lax.axis_index("y"), lax.axis_index("c")
        me = 2 * x + y

        def half(t, which):
            return pl.ds(pl.multiple_of(which * halves[t], 16), halves[t])

        def over_ici(t, j, arriving):
            px, py = _other_chip(x, y, j)
            return pltpu.make_async_remote_copy(
                src_ref=ins[t].at[half(t, c)], dst_ref=outs[t].at[2 * px + py if arriving else me, half(t, c)],
                send_sem=ici_send.at[t, j], recv_sem=ici_recv.at[t, j], device_id=(px, py, c), device_id_type=MESH)

        def over_d2d(t, j, arriving):
            px, py = _other_chip(x, y, j)
            mine = outs[t].at[2 * px + py, half(t, c)]
            return pltpu.make_async_remote_copy(
                src_ref=mine, dst_ref=outs[t].at[2 * px + py, half(t, 1 - c)] if arriving else mine,
                send_sem=d2d_send.at[t, j], recv_sem=d2d_recv.at[t, j], device_id=(x, y, 1 - c), device_id_type=MESH)

        return over_ici, over_d2d, lambda t: pltpu.make_async_copy(ins[t], outs[t].at[me], local_sems.at[t])

    def start(ins, outs, sems):
        over_ici, _, own = copies(ins, outs, sems)
        for t in range(n):
            own(t).start()
            for j in range(len(CHIP_FLIPS)):
                over_ici(t, j, False).start()

    def wait(ins, outs, sems):
        over_ici, over_d2d, own = copies(ins, outs, sems)
        for t in range(n):
            for j in range(len(CHIP_FLIPS)):
                over_ici(t, j, True).wait_recv()
                over_d2d(t, j, False).start()
        for t in range(n):
            for j in range(len(CHIP_FLIPS)):
                over_d2d(t, j, True).wait_recv()
                over_d2d(t, j, False).wait_send()
                over_ici(t, j, False).wait_send()
            own(t).wait()

    dma = pltpu.SemaphoreType.DMA
    return _Cargo(tuple(shards), tuple(jax.ShapeDtypeStruct((4,) + a.shape, a.dtype) for a in shards),
                  (dma((n, 3)), dma((n, 3)), dma((n, 3)), dma((n, 3)), dma((n,))), start, wait)


def _scatter_cargo(grads, small):
    n = len(grads)

    def copies(ins, outs, sems):
        x, y, c = lax.axis_index("x"), lax.axis_index("y"), lax.axis_index("c")
        me = 4 * x + 2 * y + c

        def remote(t, j):
            px, py = _other_chip(x, y, j)
            return pltpu.make_async_remote_copy(
                src_ref=ins[t].at[2 * px + py], dst_ref=outs[t].at[j], send_sem=sems[0].at[t, j],
                recv_sem=sems[1].at[t, j], device_id=(px, py, c), device_id_type=MESH)

        def small_remote(r, arriving):
            fx, fy, fc = (r + 1) // 4, ((r + 1) // 2) % 2, (r + 1) % 2
            px, py, pc = (1 - x if fx else x), (1 - y if fy else y), (1 - c if fc else c)
            return pltpu.make_async_remote_copy(
                src_ref=ins[n], dst_ref=outs[n].at[4 * px + 2 * py + pc if arriving else me],
                send_sem=sems[2].at[r], recv_sem=sems[3].at[r], device_id=(px, py, pc), device_id_type=MESH)

        return remote, small_remote, lambda: pltpu.make_async_copy(ins[n], outs[n].at[me], sems[4])

    def start(ins, outs, sems):
        remote, small_remote, small_own = copies(ins, outs, sems)
        if small is not None:
            small_own().start()
            for r in range(7):
                small_remote(r, False).start()
        for t in range(n):
            for j in range(len(CHIP_FLIPS)):
                remote(t, j).start()

    def wait(ins, outs, sems):
        remote, small_remote, small_own = copies(ins, outs, sems)
        if small is not None:
            for r in range(7):
                small_remote(r, True).wait_recv()
                small_remote(r, False).wait_send()
            small_own().wait()
        for t in range(n):
            for j in range(len(CHIP_FLIPS)):
                remote(t, j).wait()

    dma = pltpu.SemaphoreType.DMA
    ins = tuple(grads) + (() if small is None else (small,))
    out_shape = tuple(jax.ShapeDtypeStruct((3,) + g.shape[1:], g.dtype) for g in grads)
    sem_shapes = (dma((n, 3)), dma((n, 3)))
    if small is not None:
        out_shape += (jax.ShapeDtypeStruct((8,) + small.shape, small.dtype),)
        sem_shapes += (dma((7,)), dma((7,)), dma(()))
    return _Cargo(ins, out_shape, sem_shapes, start, wait)


def _swap_cores(parts):
    n = len(parts)

    def body(*refs):
        ins, outs = refs[:n], refs[n:2 * n]
        send_sems, recv_sems = refs[2 * n:]
        peer = (lax.axis_index("x"), lax.axis_index("y"), 1 - lax.axis_index("c"))
        copies = [pltpu.make_async_remote_copy(src_ref=ins[t], dst_ref=outs[t], send_sem=send_sems.at[t],
                                               recv_sem=recv_sems.at[t], device_id=peer, device_id_type=MESH)
                  for t in range(n)]
        for cp in copies:
            cp.start()
        for cp in copies:
            cp.wait()

    return pl.pallas_call(
        body, name="swap_cores", in_specs=[ANY] * n, out_specs=[ANY] * n,
        out_shape=[jax.ShapeDtypeStruct(a.shape, a.dtype) for a in parts],
        scratch_shapes=[pltpu.SemaphoreType.DMA((n,)), pltpu.SemaphoreType.DMA((n,))],
    )(*parts)


def _rows_tile(r):
    return 256 if r % 256 == 0 else r


def _sum_quarters(own, recv, *, name):
    r, c = own.shape
    tr = _rows_tile(r)

    def body(own_ref, recv_ref, o_ref):
        acc = own_ref[...].astype(F32)
        for j in range(3):
            acc = acc + recv_ref[j].astype(F32)
        o_ref[...] = acc

    return pl.pallas_call(
        body, name=name, grid=(r // tr,),
        in_specs=[pl.BlockSpec((tr, c), lambda i: (i, 0)), pl.BlockSpec((3, tr, c), lambda i: (0, i, 0))],
        out_specs=pl.BlockSpec((tr, c), lambda i: (i, 0)),
        out_shape=jax.ShapeDtypeStruct((r, c), F32),
        compiler_params=_params(("parallel",)),
    )(own, recv)


def _adamw(w, m, v, parts, *, name):
    r, c = w.shape
    tr = _rows_tile(r)
    c1, c2 = 1.0 - ADAM_B1 ** ADAM_STEP, 1.0 - ADAM_B2 ** ADAM_STEP
    n_parts = len(parts)

    def body(*refs):
        w_ref, m_ref, v_ref = refs[:3]
        g_ref, d_ref, nm_ref, nv_ref = refs[3 + n_parts:]
        terms = []
        for p_ref in refs[3:3 + n_parts]:
            terms += [p_ref[...]] if len(p_ref.shape) == 2 else [p_ref[j] for j in range(p_ref.shape[0])]
        g = terms[0]
        for term in terms[1:]:
            g = g + term
        m_new = ADAM_B1 * m_ref[...] + (1.0 - ADAM_B1) * g
        v_new = ADAM_B2 * v_ref[...] + (1.0 - ADAM_B2) * (g * g)
        step = (m_new / c1) / (jnp.sqrt(v_new / c2) + ADAM_EPS)
        g_ref[...] = g
        d_ref[...] = -ADAM_LR * (step + ADAM_WD * w_ref[...])
        nm_ref[...] = m_new
        nv_ref[...] = v_new

    blk = pl.BlockSpec((tr, c), lambda i: (i, 0))
    part_specs = [blk if p.ndim == 2 else pl.BlockSpec((p.shape[0], tr, c), lambda i: (0, i, 0)) for p in parts]
    return pl.pallas_call(
        body, name=name, grid=(r // tr,), in_specs=[blk, blk, blk] + part_specs,
        out_specs=[blk] * 4, out_shape=[jax.ShapeDtypeStruct((r, c), F32)] * 4,
        compiler_params=_params(("parallel",)),
    )(w, m, v, *parts)


MATS = ("a_w_qkv", "a_w_o", "b_w_qkv", "b_w_o", "c_w_qkv", "c_w_o", "mlp_w1", "mlp_w2")
SMALLS = ("attn_norm", "mlp_norm", "a_q_gain", "a_k_gain", "c_sinks", "final_norm")
WEIGHTS = ("attn_norm", "mlp_norm", "a_w_qkv", "a_q_gain", "a_k_gain", "a_w_o", "b_w_qkv", "b_w_o", "c_w_qkv",
           "c_sinks", "c_w_o", "mlp_w1", "mlp_w2", "final_norm")
MIXER_OF_LAYER = tuple((layer % N_MIXERS, sum(1 for q in range(layer) if q % N_MIXERS == layer % N_MIXERS))
                       for layer in range(DEPTH))
SMALL_ROWS = 8


def _pack_small(values):
    rows, spans, at = [], [], 0
    for v in values:
        flat = v.reshape(-1)
        n = -(-flat.shape[0] // (SMALL_ROWS * LANES)) * SMALL_ROWS
        rows.append(jnp.pad(flat, (0, n * LANES - flat.shape[0])).reshape(n, LANES))
        spans.append((at, n))
        at += n
    return jnp.concatenate(rows, axis=0), spans


def kernel(x, attn_norm, mlp_norm, a_w_qkv, a_q_gain, a_k_gain, a_w_o, b_w_qkv, b_w_o, c_w_qkv, c_sinks, c_w_o, mlp_w1, mlp_w2, final_norm, loss_target, m_attn_norm, m_mlp_norm, m_a_w_qkv, m_a_q_gain, m_a_k_gain, m_a_w_o, m_b_w_qkv, m_b_w_o, m_c_w_qkv, m_c_sinks, m_c_w_o, m_mlp_w1, m_mlp_w2, m_final_norm, v_attn_norm, v_mlp_norm, v_a_w_qkv, v_a_q_gain, v_a_k_gain, v_a_w_o, v_b_w_qkv, v_b_w_o, v_c_w_qkv, v_c_sinks, v_c_w_o, v_mlp_w1, v_mlp_w2, v_final_norm):
    env = dict(locals())
    w = {name: env[name] for name in WEIGHTS}
    mom = {name: (env["m_" + name], env["v_" + name]) for name in WEIGHTS}

    prefix = ("a", "b", "c")
    shards, mixer_params = [], []
    for layer, (kind, j) in enumerate(MIXER_OF_LAYER):
        shards.append(dict(w_qkv=w[prefix[kind] + "_w_qkv"][j].astype(BF16), w_o=w[prefix[kind] + "_w_o"][j].astype(BF16),
                           w1=mlp_w1[layer].astype(BF16), w2=mlp_w2[layer].astype(BF16)))
        if kind == 0:
            mixer_params.append(dict(gq2=jnp.tile(a_q_gain[j], 2)[None], gk2=jnp.tile(a_k_gain[j], 2)[None]))
        elif kind == 1:
            mixer_params.append(dict(slopes=_per_head(_alibi_slopes(len(B_GROUPS) * B_HEADS_PER_GROUP))))
        else:
            mixer_params.append(dict(slopes=_per_head(_alibi_slopes(C_HEADS)), sinks=_per_head(c_sinks[j])))

    norms = dict(attn=attn_norm, mlp=mlp_norm, final=final_norm)
    loss_part, grad_x, own, received, pending, g_small = _local_step(x[0], loss_target[0], norms, mixer_params, shards)
    loss = lax.psum(loss_part[0, 0], ("x", "y", "c"))

    of_kind = lambda kind, key: jnp.stack([g_small["mixer"][layer][key] for layer, (k, _) in enumerate(MIXER_OF_LAYER)
                                           if k == kind])
    small_grads = dict(
        attn_norm=jnp.concatenate(g_small["attn"], axis=0), mlp_norm=jnp.concatenate(g_small["mlp"], axis=0),
        a_q_gain=of_kind(0, "q_gain"), a_k_gain=of_kind(0, "k_gain"), c_sinks=of_kind(2, "sinks"),
        final_norm=g_small["final"][0])
    packed, spans = _pack_small([small_grads[name] for name in SMALLS])
    *last, all_small = _run_cargo(_scatter_cargo([own[item] for item in pending], packed), name="scatter_last")
    received.update(zip(pending, last))

    me_chip = 2 * lax.axis_index("x") + lax.axis_index("y")
    partial = []
    for name in MATS:
        key = name[2:] if name[0] in "abc" else name[4:]
        layers = [layer for layer, (kind, _) in enumerate(MIXER_OF_LAYER)
                  if name.startswith("mlp") or prefix[kind] == name[0]]
        sums = [_sum_quarters(lax.dynamic_index_in_dim(own[layer, key], me_chip, axis=0, keepdims=False),
                              received[layer, key], name=f"sum_{name}_l{layer}") for layer in layers]
        partial.append(jnp.concatenate(sums, axis=0))
    other = _swap_cores(partial)

    out = {}
    for name, mine, theirs in zip(MATS, partial, other):
        shape = w[name].shape
        res = _adamw(*[a.reshape(-1, shape[-1]) for a in (w[name], *mom[name])], [mine, theirs], name=f"adamw_{name}")
        out[name] = [a.reshape(shape) for a in res]
    for name, (at, n) in zip(SMALLS, spans):
        shape = w[name].shape
        packed_in = [_pack_small([a])[0] for a in (w[name], *mom[name])]
        res = _adamw(*packed_in, [all_small[:, at:at + n]], name=f"adamw_{name}")
        out[name] = [a.reshape(-1)[:w[name].size].reshape(shape) for a in res]

    return (loss, grad_x[None], *[out[name][0] for name in WEIGHTS], *[out[name][1] for name in WEIGHTS],
            *[out[name][2] for name in WEIGHTS], *[out[name][3] for name in WEIGHTS])
```

```python
from typing import Callable, NamedTuple

import jax
import jax.numpy as jnp
from jax import lax
from jax.experimental import pallas as pl
from jax.experimental.pallas import tpu as pltpu

F32 = jnp.float32
BF16 = jnp.bfloat16
MESH = pl.DeviceIdType.MESH
ANY = pl.BlockSpec(memory_space=pl.ANY)

D_MODEL = 1024
HEAD_DIM = 64
GRID_W = 64
ROPE_THETA = 10000.0
RMS_EPS = 1e-6
QK_SCALE = HEAD_DIM ** -0.5
LOG2E = 1.4426950408889634
LN2 = 0.6931471805599453
A_HEADS, A_KV = 16, 4
B_GROUPS = ((128, 1), (512, 4), (2048, 16))
B_HEADS_PER_GROUP, B_KV_PER_GROUP = 6, 2
C_HEADS, C_KV, C_WINDOW = 16, 4, 128
DEPTH, N_MIXERS = 4, 3
ADAM_LR, ADAM_B1, ADAM_B2, ADAM_EPS, ADAM_WD, ADAM_STEP = 0.001, 0.9, 0.999, 1e-08, 0.01, 10

WIN_REACH = 128
V7X_VMEM_BUDGET = 48 * 1024 * 1024
LANES = 128
ROW_TILE = 1024


def _params(semantics):
    return pltpu.CompilerParams(dimension_semantics=semantics, vmem_limit_bytes=V7X_VMEM_BUDGET)


def _tile(n, cap):
    if n <= cap:
        return n
    t = (cap // LANES) * LANES
    while n % t:
        t -= LANES
    return t


def _norm_mm(h, gain, w, *, out_dtype, relu2, transpose_out, name):
    m, d = h.shape
    by_quarter = w.ndim == 3
    assert not (by_quarter and transpose_out)
    n = w.shape[-1] * (4 if by_quarter else 1)
    per_step = 2 if by_quarter and 2 * w.shape[-1] <= 2048 else 1
    tm, tn = min(ROW_TILE, m), (per_step * w.shape[-1] if by_quarter else _tile(n, 2048))
    w_spec = (pl.BlockSpec((per_step, d, tn // per_step), lambda i, j: (j, 0, 0)) if by_quarter
              else pl.BlockSpec((d, tn), lambda i, j: (0, j)))
    y_spec = (pl.BlockSpec((tn, tm), lambda i, j: (j, i)) if transpose_out
              else pl.BlockSpec((tm, tn), lambda i, j: (i, j)))

    def body(h_ref, g_ref, w_ref, hn_ref, y_ref):
        @pl.when(pl.program_id(1) == 0)
        def _():
            x = h_ref[...]
            r = lax.rsqrt(jnp.mean(x * x, axis=-1, keepdims=True) + RMS_EPS)
            hn_ref[...] = (x * r * g_ref[...]).astype(BF16)

        def finish(y):
            if relu2:
                y = jnp.maximum(y, 0.0)
                y = y * y
            return y.astype(y_ref.dtype)

        if transpose_out:
            y_ref[...] = finish(lax.dot_general(w_ref[...], hn_ref[...], (((0,), (1,)), ((), ())),
                                                preferred_element_type=F32))
        elif by_quarter:
            cols = tn // per_step
            for q in range(per_step):
                y_ref[:, q * cols:(q + 1) * cols] = finish(jnp.dot(hn_ref[...], w_ref[q], preferred_element_type=F32))
        else:
            y_ref[...] = finish(jnp.dot(hn_ref[...], w_ref[...], preferred_element_type=F32))

    return pl.pallas_call(
        body, name=name, grid=(m // tm, n // tn),
        in_specs=[pl.BlockSpec((tm, d), lambda i, j: (i, 0)), pl.BlockSpec((1, d), lambda i, j: (0, 0)), w_spec],
        out_specs=[pl.BlockSpec((tm, d), lambda i, j: (i, 0)), y_spec],
        out_shape=[jax.ShapeDtypeStruct((m, d), BF16), jax.ShapeDtypeStruct((n, m) if transpose_out else (m, n), out_dtype)],
        compiler_params=_params(("parallel", "arbitrary")),
    )(h, gain, w)


def _mm_res(a, w, h_in, *, a_transposed, name):
    k, d = w.shape
    m = h_in.shape[0]
    tm, tk = min(ROW_TILE, m), _tile(k, 2048)
    lhs_contracts = 0 if a_transposed else 1

    def body(a_ref, w_ref, h_ref, o_ref):
        @pl.when(pl.program_id(1) == 0)
        def _():
            o_ref[...] = h_ref[...]

        o_ref[...] += lax.dot_general(a_ref[...], w_ref[...], (((lhs_contracts,), (0,)), ((), ())),
                                      preferred_element_type=F32)

    a_spec = (pl.BlockSpec((tk, tm), lambda i, j: (j, i)) if a_transposed
              else pl.BlockSpec((tm, tk), lambda i, j: (i, j)))
    return pl.pallas_call(
        body, name=name, grid=(m // tm, k // tk),
        in_specs=[a_spec, pl.BlockSpec((tk, d), lambda i, j: (j, 0)), pl.BlockSpec((tm, d), lambda i, j: (i, 0))],
        out_specs=pl.BlockSpec((tm, d), lambda i, j: (i, 0)),
        out_shape=jax.ShapeDtypeStruct((m, d), F32),
        compiler_params=_params(("parallel", "arbitrary")),
    )(a, w, h_in)


def _mm_nt(a, w, act, *, transpose_out, name):
    m, d = a.shape
    n = w.shape[0]
    tm, tn = min(ROW_TILE, m), _tile(n, 1152)
    assert act is None or not transpose_out
    nt = (((1,), (1,)), ((), ()))

    def body(*refs):
        a_ref, w_ref = refs[0], refs[1]
        o_ref = refs[-1]
        if transpose_out:
            acc = lax.dot_general(w_ref[...], a_ref[...].astype(BF16), nt, preferred_element_type=F32)
        else:
            acc = lax.dot_general(a_ref[...].astype(BF16), w_ref[...], nt, preferred_element_type=F32)
        if act is not None:
            acc = acc * (2.0 * jnp.sqrt(refs[2][...].astype(F32)))
        o_ref[...] = acc.astype(BF16)

    in_specs = [pl.BlockSpec((tm, d), lambda i, j: (i, 0)), pl.BlockSpec((tn, d), lambda i, j: (j, 0))]
    args = [a, w]
    if act is not None:
        in_specs.append(pl.BlockSpec((tm, tn), lambda i, j: (i, j)))
        args.append(act)
    out_spec = (pl.BlockSpec((tn, tm), lambda i, j: (j, i)) if transpose_out
                else pl.BlockSpec((tm, tn), lambda i, j: (i, j)))
    return pl.pallas_call(
        body, name=name, grid=(m // tm, n // tn), in_specs=in_specs, out_specs=out_spec,
        out_shape=jax.ShapeDtypeStruct((n, m) if transpose_out else (m, n), BF16),
        compiler_params=_params(("parallel", "parallel")),
    )(*args)


def _rmsnorm_bwd(dn, x, gain):
    r = lax.rsqrt(jnp.mean(x * x, axis=-1, keepdims=True) + RMS_EPS)
    xh = x * r
    dgain = jnp.sum(dn * xh, axis=0, keepdims=True)
    u = dn * gain
    dx = r * (u - xh * jnp.mean(u * xh, axis=-1, keepdims=True))
    return dx, dgain


def _mm_nt_normbwd(g, w, h, gain, dh_in, *, g_transposed, name):
    m, k = g.shape[::-1] if g_transposed else g.shape
    by_quarter = w.ndim == 3
    d = w.shape[-2]
    tm, tk = min(ROW_TILE, m), (w.shape[-1] if by_quarter else _tile(k, 1024))
    sub = min(256, tm)
    nk = k // tk
    w_spec = (pl.BlockSpec((None, d, tk), lambda i, j: (j, 0, 0)) if by_quarter
              else pl.BlockSpec((d, tk), lambda i, j: (0, j)))

    def body(g_ref, w_ref, h_ref, gain_ref, dh_ref, o_ref, dg_ref, acc_ref):
        i, j = pl.program_id(0), pl.program_id(1)

        @pl.when((i == 0) & (j == 0))
        def _():
            dg_ref[...] = jnp.zeros_like(dg_ref)

        @pl.when(j == 0)
        def _():
            acc_ref[...] = jnp.zeros_like(acc_ref)

        acc_ref[...] += lax.dot_general(g_ref[...], w_ref[...], (((0 if g_transposed else 1,), (1,)), ((), ())),
                                        preferred_element_type=F32)

        @pl.when(j == nk - 1)
        def _():
            for r in range(0, tm, sub):
                rows = slice(r, r + sub)
                dx, dgain = _rmsnorm_bwd(acc_ref[rows, :], h_ref[rows, :], gain_ref[...])
                dg_ref[...] += dgain
                o_ref[rows, :] = dh_ref[rows, :] + dx

    return pl.pallas_call(
        body, name=name, grid=(m // tm, nk),
        in_specs=[pl.BlockSpec((tk, tm), lambda i, j: (j, i)) if g_transposed
                  else pl.BlockSpec((tm, tk), lambda i, j: (i, j)), w_spec,
                  pl.BlockSpec((tm, d), lambda i, j: (i, 0)), pl.BlockSpec((1, d), lambda i, j: (0, 0)),
                  pl.BlockSpec((tm, d), lambda i, j: (i, 0))],
        out_specs=[pl.BlockSpec((tm, d), lambda i, j: (i, 0)), pl.BlockSpec((1, d), lambda i, j: (0, 0))],
        out_shape=[jax.ShapeDtypeStruct((m, d), F32), jax.ShapeDtypeStruct((1, d), F32)],
        scratch_shapes=[pltpu.VMEM((tm, d), F32)],
        compiler_params=_params(("arbitrary", "arbitrary")),
    )(g, w, h, gain, dh_in)


def _mm_tn(x, g, *, x_transposed, g_transposed, column_quarters, name):
    k, m = x.shape if x_transposed else x.shape[::-1]
    n = g.shape[0] if g_transposed else g.shape[1]
    tm, tk, tn = min(2 * ROW_TILE, m), _tile(k, 1152), (n // 4 if column_quarters else _tile(n, 1024))
    nm = m // tm
    lhs_contracts, rhs_contracts = (1 if x_transposed else 0), (1 if g_transposed else 0)
    g_spec = (pl.BlockSpec((tn, tm), lambda a, b, s: (b, s)) if g_transposed
              else pl.BlockSpec((tm, tn), lambda a, b, s: (s, b)))

    def body(x_ref, g_ref, o_ref, acc_ref):
        s = pl.program_id(2)

        @pl.when(s == 0)
        def _():
            acc_ref[...] = jnp.zeros_like(acc_ref)

        acc_ref[...] += lax.dot_general(x_ref[...], g_ref[...].astype(BF16),
                                        (((lhs_contracts,), (rhs_contracts,)), ((), ())), preferred_element_type=F32)

        @pl.when(s == nm - 1)
        def _():
            o_ref[...] = acc_ref[...].astype(BF16)

    x_spec = (pl.BlockSpec((tk, tm), lambda a, b, s: (a, s)) if x_transposed
              else pl.BlockSpec((tm, tk), lambda a, b, s: (s, a)))
    out_spec = (pl.BlockSpec((None, tk, tn), lambda a, b, s: (b, a, 0)) if column_quarters
                else pl.BlockSpec((tk, tn), lambda a, b, s: (a, b)))
    return pl.pallas_call(
        body, name=name, grid=(k // tk, n // tn, nm),
        in_specs=[x_spec, g_spec], out_specs=out_spec,
        out_shape=jax.ShapeDtypeStruct((4, k, n // 4) if column_quarters else (k, n), BF16),
        scratch_shapes=[pltpu.VMEM((tk, tn), F32)],
        compiler_params=_params(("parallel", "parallel", "arbitrary")),
    )(x, g)


def _loss_head(h, gain, target):
    m, d = h.shape
    tm = 512

    def body(h_ref, g_ref, t_ref, dh_ref, loss_ref, dg_ref):
        @pl.when(pl.program_id(0) == 0)
        def _():
            loss_ref[...] = jnp.zeros_like(loss_ref)
            dg_ref[...] = jnp.zeros_like(dg_ref)

        x = h_ref[...]
        gain_v = g_ref[...]
        r = lax.rsqrt(jnp.mean(x * x, axis=-1, keepdims=True) + RMS_EPS)
        err = x * r * gain_v - t_ref[...]
        loss_ref[...] += 0.5 * jnp.sum(jnp.mean(err * err, axis=-1, keepdims=True), axis=0, keepdims=True)
        dx, dgain = _rmsnorm_bwd(err * (1.0 / d), x, gain_v)
        dg_ref[...] += dgain
        dh_ref[...] = dx

    return pl.pallas_call(
        body, name="loss_head", grid=(m // tm,),
        in_specs=[pl.BlockSpec((tm, d), lambda i: (i, 0)), pl.BlockSpec((1, d), lambda i: (0, 0)),
                  pl.BlockSpec((tm, d), lambda i: (i, 0))],
        out_specs=[pl.BlockSpec((tm, d), lambda i: (i, 0)), pl.BlockSpec((1, LANES), lambda i: (0, 0)),
                   pl.BlockSpec((1, d), lambda i: (0, 0))],
        out_shape=[jax.ShapeDtypeStruct((m, d), F32), jax.ShapeDtypeStruct((1, LANES), F32),
                   jax.ShapeDtypeStruct((1, d), F32)],
        compiler_params=_params(("arbitrary",)),
    )(h, gain, target)


def _rope_tables(s):
    t = jnp.arange(s)
    row = (t // GRID_W).astype(F32)
    col = (t % GRID_W).astype(F32)
    axis_dim = HEAD_DIM // 2
    inv_freq = ROPE_THETA ** (-jnp.arange(0, axis_dim, 2, dtype=F32) / axis_dim)
    ar, ac = row[:, None] * inv_freq, col[:, None] * inv_freq
    cos = jnp.concatenate([jnp.cos(ar), jnp.cos(ar), jnp.cos(ac), jnp.cos(ac)], axis=-1)
    sin = jnp.concatenate([-jnp.sin(ar), jnp.sin(ar), -jnp.sin(ac), jnp.sin(ac)], axis=-1)
    return jnp.tile(cos, (1, 2)), jnp.tile(sin, (1, 2))


def _swap16(x):
    lane = lax.broadcasted_iota(jnp.int32, x.shape, 1)
    return jnp.where((lane % 32) < 16, pltpu.roll(x, LANES - 16, 1), pltpu.roll(x, 16, 1))


def _head_mean(v):
    lane = lax.broadcasted_iota(jnp.int32, v.shape, 1)
    lo = lane < HEAD_DIM
    s_all = jnp.sum(v, axis=-1, keepdims=True)
    s_lo = jnp.sum(jnp.where(lo, v, 0.0), axis=-1, keepdims=True)
    return jnp.where(lo, s_lo, s_all - s_lo) * (1.0 / HEAD_DIM)


def _norm_rope(x, gain2, cos, sin):
    r = lax.rsqrt(_head_mean(x * x) + RMS_EPS)
    nrm = x * r * gain2
    return nrm * cos + _swap16(nrm) * sin


def _norm_rope_bwd(dy, x, gain2, cos, sin):
    dn = dy * cos + _swap16(dy * sin)
    r = lax.rsqrt(_head_mean(x * x) + RMS_EPS)
    xh = x * r
    dgain = jnp.sum(dn * xh, axis=0, keepdims=True)
    u = dn * gain2
    return r * (u - xh * _head_mean(u * xh)), dgain


def _a_prep(qkv, cos, sin, gq2, gk2):
    s = qkv.shape[0]
    tr = 256
    nq, nk = A_HEADS * HEAD_DIM, A_KV * HEAD_DIM

    def body(qkv_ref, cos_ref, sin_ref, gq_ref, gk_ref, qt_ref, k_ref, v_ref):
        cos_v, sin_v = cos_ref[...], sin_ref[...]
        for c in range(nq // LANES):
            y = _norm_rope(qkv_ref[:, c * LANES:(c + 1) * LANES], gq_ref[...], cos_v, sin_v) * (QK_SCALE * LOG2E)
            yt = y.T
            qt_ref[2 * c] = yt[:HEAD_DIM].astype(BF16)
            qt_ref[2 * c + 1] = yt[HEAD_DIM:].astype(BF16)
        for c in range(nk // LANES):
            y = _norm_rope(qkv_ref[:, nq + c * LANES:nq + (c + 1) * LANES], gk_ref[...], cos_v, sin_v)
            k_ref[2 * c] = y[:, :HEAD_DIM].astype(BF16)
            k_ref[2 * c + 1] = y[:, HEAD_DIM:].astype(BF16)
            x = qkv_ref[:, nq + nk + c * LANES:nq + nk + (c + 1) * LANES]
            v_ref[2 * c] = x[:, :HEAD_DIM].astype(BF16)
            v_ref[2 * c + 1] = x[:, HEAD_DIM:].astype(BF16)

    return pl.pallas_call(
        body, name="a_prep", grid=(s // tr,),
        in_specs=[pl.BlockSpec((tr, nq + 2 * nk), lambda i: (i, 0)), pl.BlockSpec((tr, LANES), lambda i: (i, 0)),
                  pl.BlockSpec((tr, LANES), lambda i: (i, 0)), pl.BlockSpec((1, LANES), lambda i: (0, 0)),
                  pl.BlockSpec((1, LANES), lambda i: (0, 0))],
        out_specs=[pl.BlockSpec((A_HEADS, HEAD_DIM, tr), lambda i: (0, 0, i)),
                   pl.BlockSpec((A_KV, tr, HEAD_DIM), lambda i: (0, i, 0)),
                   pl.BlockSpec((A_KV, tr, HEAD_DIM), lambda i: (0, i, 0))],
        out_shape=[jax.ShapeDtypeStruct((A_HEADS, HEAD_DIM, s), BF16), jax.ShapeDtypeStruct((A_KV, s, HEAD_DIM), BF16),
                   jax.ShapeDtypeStruct((A_KV, s, HEAD_DIM), BF16)],
        compiler_params=_params(("parallel",)),
    )(qkv, cos, sin, gq2, gk2)


def _a_prep_bwd(dqt, dkt, dvt, qkv, cos, sin, gq2, gk2):
    s = qkv.shape[0]
    tr = 256
    nq, nk = A_HEADS * HEAD_DIM, A_KV * HEAD_DIM

    def body(dqt_ref, dkt_ref, dvt_ref, qkv_ref, cos_ref, sin_ref, gq_ref, gk_ref, o_ref, dgq_ref, dgk_ref):
        @pl.when(pl.program_id(0) == 0)
        def _():
            dgq_ref[...] = jnp.zeros_like(dgq_ref)
            dgk_ref[...] = jnp.zeros_like(dgk_ref)

        cos_v, sin_v = cos_ref[...], sin_ref[...]

        def pair(ref, c):
            return jnp.concatenate([ref[2 * c], ref[2 * c + 1]], axis=0).T

        for c in range(nq // LANES):
            dx, dg = _norm_rope_bwd(pair(dqt_ref, c) * QK_SCALE, qkv_ref[:, c * LANES:(c + 1) * LANES],
                                    gq_ref[...], cos_v, sin_v)
            o_ref[:, c * LANES:(c + 1) * LANES] = dx.astype(BF16)
            dgq_ref[...] += dg
        for c in range(nk // LANES):
            lo = nq + c * LANES
            dx, dg = _norm_rope_bwd(pair(dkt_ref, c) * LN2, qkv_ref[:, lo:lo + LANES], gk_ref[...], cos_v, sin_v)
            o_ref[:, lo:lo + LANES] = dx.astype(BF16)
            dgk_ref[...] += dg
            o_ref[:, lo + nk:lo + nk + LANES] = pair(dvt_ref, c).astype(BF16)

    return pl.pallas_call(
        body, name="a_prep_bwd", grid=(s // tr,),
        in_specs=[pl.BlockSpec((A_HEADS, HEAD_DIM, tr), lambda i: (0, 0, i)),
                  pl.BlockSpec((A_KV, HEAD_DIM, tr), lambda i: (0, 0, i)),
                  pl.BlockSpec((A_KV, HEAD_DIM, tr), lambda i: (0, 0, i)),
                  pl.BlockSpec((tr, nq + 2 * nk), lambda i: (i, 0)), pl.BlockSpec((tr, LANES), lambda i: (i, 0)),
                  pl.BlockSpec((tr, LANES), lambda i: (i, 0)), pl.BlockSpec((1, LANES), lambda i: (0, 0)),
                  pl.BlockSpec((1, LANES), lambda i: (0, 0))],
        out_specs=[pl.BlockSpec((tr, nq + 2 * nk), lambda i: (i, 0)), pl.BlockSpec((1, LANES), lambda i: (0, 0)),
                   pl.BlockSpec((1, LANES), lambda i: (0, 0))],
        out_shape=[jax.ShapeDtypeStruct((s, nq + 2 * nk), BF16), jax.ShapeDtypeStruct((1, LANES), F32),
                   jax.ShapeDtypeStruct((1, LANES), F32)],
        compiler_params=_params(("arbitrary",)),
    )(dqt, dkt, dvt, qkv, cos, sin, gq2, gk2)


A_TQ = 1024
A_TQ_SUB = 256
A_TQ_BWD = 1024
A_KEY_CHUNK = 512


def _a_attn_fwd(qt, k, v, cargo, *, name):
    nh, _, s = qt.shape
    rep = nh // k.shape[0]
    tq = min(A_TQ, s)
    sub = min(A_TQ_SUB, tq)
    grid = (nh, s // tq)

    def body(qt_ref, k_ref, v_ref, o_ref, lse_ref):
        scores = [jnp.dot(k_ref[0], qt_ref[0, :, a:a + sub], preferred_element_type=F32)
                  for a in range(0, tq, sub)]
        for a, st in zip(range(0, tq, sub), scores):
            mx = jnp.max(st, axis=0, keepdims=True)
            p = jnp.exp2(st - mx)
            den = jnp.sum(p, axis=0, keepdims=True)
            ot = lax.dot_general(v_ref[0], p.astype(BF16), (((0,), (0,)), ((), ())), preferred_element_type=F32)
            o_ref[0, :, a:a + sub] = (ot / den).astype(BF16)
            lse_ref[0, :, a:a + sub] = mx + jnp.log(den) * LOG2E

    carried = _carry(cargo, grid, 3, 2, body)
    res = pl.pallas_call(
        carried.body, name=name, grid=grid,
        in_specs=[pl.BlockSpec((1, HEAD_DIM, tq), lambda h, i: (h, 0, i)),
                  pl.BlockSpec((1, s, HEAD_DIM), lambda h, i: (h // rep, 0, 0)),
                  pl.BlockSpec((1, s, HEAD_DIM), lambda h, i: (h // rep, 0, 0))] + carried.in_specs,
        out_specs=[pl.BlockSpec((1, HEAD_DIM, tq), lambda h, i: (h, 0, i)),
                   pl.BlockSpec((1, 1, tq), lambda h, i: (h, 0, i))] + carried.out_specs,
        out_shape=[jax.ShapeDtypeStruct((nh, HEAD_DIM, s), BF16), jax.ShapeDtypeStruct((nh, 1, s), F32)]
        + carried.out_shape,
        scratch_shapes=carried.scratch,
        compiler_params=_params(("arbitrary", "arbitrary")),
    )(qt, k, v, *carried.args)
    return res[0], res[1], res[2:]


def _a_attn_bwd(qt, k, v, dot, ot, lse, cargo, *, name):
    nh, _, s = qt.shape
    nkv = k.shape[0]
    rep = nh // nkv
    tq, ck = min(A_TQ_BWD, s), min(A_KEY_CHUNK, s)
    grid = (nh, s // tq)

    def body(qt_ref, k_ref, v_ref, dot_ref, ot_ref, lse_ref, dq_ref, dk_ref, dv_ref):
        h, i = pl.program_id(0), pl.program_id(1)

        @pl.when((h % rep == 0) & (i == 0))
        def _():
            dk_ref[...] = jnp.zeros_like(dk_ref)
            dv_ref[...] = jnp.zeros_like(dv_ref)

        q_t, do_t, lse_v = qt_ref[0], dot_ref[0], lse_ref[0]
        delta = jnp.sum(do_t.astype(F32) * ot_ref[0].astype(F32), axis=0, keepdims=True)
        nt = (((1,), (1,)), ((), ()))
        dq = jnp.zeros((HEAD_DIM, tq), F32)
        for c in range(s // ck):
            keys = slice(c * ck, (c + 1) * ck)
            kc = k_ref[0, keys, :]
            p = jnp.exp2(jnp.dot(kc, q_t, preferred_element_type=F32) - lse_v)
            dp = jnp.dot(v_ref[0, keys, :], do_t, preferred_element_type=F32)
            ds = (p * (dp - delta)).astype(BF16)
            dv_ref[0, :, keys] += lax.dot_general(do_t, p.astype(BF16), nt, preferred_element_type=F32)
            dk_ref[0, :, keys] += lax.dot_general(q_t, ds, nt, preferred_element_type=F32)
            dq = dq + lax.dot_general(kc, ds, (((0,), (0,)), ((), ())), preferred_element_type=F32)
        dq_ref[0] = dq

    blk_q = pl.BlockSpec((1, HEAD_DIM, tq), lambda h, i: (h, 0, i))
    blk_row = pl.BlockSpec((1, 1, tq), lambda h, i: (h, 0, i))
    blk_kv = pl.BlockSpec((1, s, HEAD_DIM), lambda h, i: (h // rep, 0, 0))
    blk_acc = pl.BlockSpec((1, HEAD_DIM, s), lambda h, i: (h // rep, 0, 0))
    carried = _carry(cargo, grid, 6, 3, body)
    res = pl.pallas_call(
        carried.body, name=name, grid=grid,
        in_specs=[blk_q, blk_kv, blk_kv, blk_q, blk_q, blk_row] + carried.in_specs,
        out_specs=[blk_q, blk_acc, blk_acc] + carried.out_specs,
        out_shape=[jax.ShapeDtypeStruct((nh, HEAD_DIM, s), F32), jax.ShapeDtypeStruct((nkv, HEAD_DIM, s), F32),
                   jax.ShapeDtypeStruct((nkv, HEAD_DIM, s), F32)] + carried.out_shape,
        scratch_shapes=carried.scratch,
        compiler_params=_params(("arbitrary", "arbitrary")),
    )(qt, k, v, dot, ot, lse, *carried.args)
    return res[0], res[1], res[2], res[3:]


WIN_FAR = 1e30


class _Band(NamedTuple):
    window: int
    dil: int
    seg: int
    stride: int

    @property
    def reach(self):
        return -(-self.window // WIN_REACH) * WIN_REACH


def _win_start(i, tq, tk, s, reach):
    return pl.multiple_of(jnp.clip(i * tq - reach, 0, s - tk), LANES)


def _win_penalty(i, start, tq, tk, band):
    qpos = i * tq + lax.broadcasted_iota(jnp.int32, (tk, tq), 1)
    kpos = start + lax.broadcasted_iota(jnp.int32, (tk, tq), 0)
    dist = jnp.abs(kpos - qpos)
    seg_lo = qpos - (qpos & (band.seg - 1))
    valid = (dist <= band.window) & (kpos >= seg_lo) & (kpos < seg_lo + band.seg)
    if band.stride > 1:
        valid &= (dist & (band.stride - 1)) == 0
    return jnp.where(valid, (dist * band.dil).astype(F32), WIN_FAR)


def _win_scores(kw_t, q_t, slope, pen):
    st = lax.dot_general(kw_t, q_t, (((0,), (0,)), ((), ())), preferred_element_type=F32)
    return st * (QK_SCALE * LOG2E) - (slope * LOG2E) * pen


def _win_tq(s):
    return min(512, s)


def _win_fwd(qt, ktp, vtp, slopes, sinks, *, band, out_dtype, name):
    nh, _, s = qt.shape
    nkv = ktp.shape[0]
    rep = nh // nkv
    tq = _win_tq(s)
    tk = min(tq + 2 * band.reach, s)

    def body(*refs):
        qt_ref, kt_ref, vt_ref, sl_ref = refs[:4]
        o_ref, lse_ref, pen_ref = refs[-3:]
        i, kv = pl.program_id(0), pl.program_id(1)
        start = _win_start(i, tq, tk, s, band.reach)

        @pl.when(kv == 0)
        def _():
            pen_ref[...] = _win_penalty(i, start, tq, tk, band)

        win = pl.ds(start, tk)
        kw_t, vw_t, pen = kt_ref[0, :, win], vt_ref[0, :, win], pen_ref[...]
        for g in range(rep):
            st = _win_scores(kw_t, qt_ref[g], sl_ref[g][:, :1], pen)
            mx = jnp.max(st, axis=0, keepdims=True)
            if sinks is not None:
                sink = refs[4][g][:, :1] * LOG2E
                mx = jnp.maximum(mx, sink)
            p = jnp.exp2(st - mx)
            den = jnp.sum(p, axis=0, keepdims=True)
            if sinks is not None:
                den = den + jnp.exp2(sink - mx)
            ot = jnp.dot(vw_t, p.astype(BF16), preferred_element_type=F32)
            o_ref[g] = (ot / den).astype(o_ref.dtype)
            lse_ref[g] = mx * LN2 + jnp.log(den)

    blk_q = pl.BlockSpec((rep, HEAD_DIM, tq), lambda i, kv: (kv, 0, i))
    blk_kv = pl.BlockSpec((1, HEAD_DIM, s), lambda i, kv: (kv, 0, 0))
    blk_h = pl.BlockSpec((rep, 1, LANES), lambda i, kv: (kv, 0, 0))
    in_specs, args = [blk_q, blk_kv, blk_kv, blk_h], [qt, ktp, vtp, slopes]
    if sinks is not None:
        in_specs.append(blk_h)
        args.append(sinks)
    return pl.pallas_call(
        body, name=name, grid=(s // tq, nkv), in_specs=in_specs,
        out_specs=[blk_q, pl.BlockSpec((rep, 1, tq), lambda i, kv: (kv, 0, i))],
        out_shape=[jax.ShapeDtypeStruct((nh, HEAD_DIM, s), out_dtype), jax.ShapeDtypeStruct((nh, 1, s), F32)],
        scratch_shapes=[pltpu.VMEM((tk, tq), F32)],
        compiler_params=_params(("arbitrary", "arbitrary")),
    )(*args)


def _win_bwd(qt, ktp, vtp, slopes, sinks, dot, ot, delta, *, band, name):
    nh, _, s = qt.shape
    nkv = ktp.shape[0]
    rep = nh // nkv
    tq = _win_tq(s)
    tk = min(tq + 2 * band.reach, s)
    n_in = 6 + (sinks is not None)

    def body(*refs):
        qt_ref, kt_ref, vt_ref, sl_ref, dot_ref, aux_ref = refs[:6]
        outs, pen_ref = refs[n_in:-1], refs[-1]
        dq_ref, dk_ref, dv_ref = outs[:3]
        i, kv = pl.program_id(0), pl.program_id(1)

        @pl.when((i == 0) & (kv == 0))
        def _():
            dk_ref[...] = jnp.zeros_like(dk_ref)
            dv_ref[...] = jnp.zeros_like(dv_ref)
            if sinks is not None:
                outs[3][...] = jnp.zeros_like(outs[3])

        start = _win_start(i, tq, tk, s, band.reach)

        @pl.when(kv == 0)
        def _():
            pen_ref[...] = _win_penalty(i, start, tq, tk, band)

        win = pl.ds(start, tk)
        kw_t, vw_t, pen = kt_ref[0, :, win], vt_ref[0, :, win], pen_ref[...]
        nt = (((1,), (1,)), ((), ()))
        dk_acc = jnp.zeros((HEAD_DIM, tk), F32)
        dv_acc = jnp.zeros((HEAD_DIM, tk), F32)
        for g in range(rep):
            q_t, do_t = qt_ref[g], dot_ref[g]
            st = _win_scores(kw_t, q_t, sl_ref[g][:, :1], pen)
            mx = jnp.max(st, axis=0, keepdims=True)
            if sinks is not None:
                sink = refs[6][g][:, :1] * LOG2E
                mx = jnp.maximum(mx, sink)
            p = jnp.exp2(st - mx)
            den = jnp.sum(p, axis=0, keepdims=True)
            if sinks is not None:
                p_sink = jnp.exp2(sink - mx)
                den = den + p_sink
            p = p / den
            dp = lax.dot_general(vw_t, do_t, (((0,), (0,)), ((), ())), preferred_element_type=F32)
            if delta is None:
                row = jnp.sum(do_t.astype(F32) * aux_ref[g].astype(F32), axis=0, keepdims=True)
            else:
                row = aux_ref[g]
            ds = (p * (dp - row) * QK_SCALE).astype(BF16)
            dv_acc = dv_acc + lax.dot_general(do_t, p.astype(BF16), nt, preferred_element_type=F32)
            dk_acc = dk_acc + lax.dot_general(q_t, ds, nt, preferred_element_type=F32)
            dq_ref[g] = jnp.dot(kw_t, ds, preferred_element_type=F32)
            if sinks is not None:
                outs[3][kv * rep + g] += (jnp.zeros((1, LANES), F32)
                                          - jnp.sum(p_sink / den * row, axis=1, keepdims=True))
        dv_ref[kv, :, win] += dv_acc
        dk_ref[kv, :, win] += dk_acc

    blk_q = pl.BlockSpec((rep, HEAD_DIM, tq), lambda i, kv: (kv, 0, i))
    blk_row = pl.BlockSpec((rep, 1, tq), lambda i, kv: (kv, 0, i))
    blk_kv = pl.BlockSpec((1, HEAD_DIM, s), lambda i, kv: (kv, 0, 0))
    blk_acc = pl.BlockSpec((nkv, HEAD_DIM, s), lambda i, kv: (0, 0, 0))
    blk_h = pl.BlockSpec((rep, 1, LANES), lambda i, kv: (kv, 0, 0))
    in_specs = [blk_q, blk_kv, blk_kv, blk_h, blk_q, blk_q if delta is None else blk_row]
    args = [qt, ktp, vtp, slopes, dot, ot if delta is None else delta]
    out_specs = [blk_q, blk_acc, blk_acc]
    out_shape = [jax.ShapeDtypeStruct((nh, HEAD_DIM, s), F32), jax.ShapeDtypeStruct((nkv, HEAD_DIM, s), F32),
                 jax.ShapeDtypeStruct((nkv, HEAD_DIM, s), F32)]
    if sinks is not None:
        in_specs.append(blk_h)
        args.append(sinks)
        out_specs.append(pl.BlockSpec((nh, 1, LANES), lambda i, h: (0, 0, 0)))
        out_shape.append(jax.ShapeDtypeStruct((nh, 1, LANES), F32))
    res = pl.pallas_call(
        body, name=name, grid=(s // tq, nkv), in_specs=in_specs, out_specs=out_specs, out_shape=out_shape,
        scratch_shapes=[pltpu.VMEM((tk, tq), F32)],
        compiler_params=_params(("arbitrary", "arbitrary")),
    )(*args)
    return res if sinks is not None else (*res, None)


def _group_weights(lse):
    e = jnp.exp(lse - jnp.max(lse, axis=0, keepdims=True))
    return e / jnp.sum(e, axis=0, keepdims=True)


def _b_combine_fwd(ot, lse):
    nh, _, s = ot.shape
    ng, hg, _ = lse.shape
    ts = min(512, s)

    def body(ot_ref, lse_ref, o_ref):
        alpha = _group_weights(lse_ref[...])
        for g in range(ng):
            for j in range(hg):
                o_ref[g * hg + j] = (ot_ref[g * hg + j] * alpha[g, j:j + 1, :]).astype(BF16)

    return pl.pallas_call(
        body, name="b_combine_fwd", grid=(s // ts,),
        in_specs=[pl.BlockSpec((nh, HEAD_DIM, ts), lambda i: (0, 0, i)), pl.BlockSpec((ng, hg, ts), lambda i: (0, 0, i))],
        out_specs=pl.BlockSpec((nh, HEAD_DIM, ts), lambda i: (0, 0, i)),
        out_shape=jax.ShapeDtypeStruct((nh, HEAD_DIM, s), BF16),
        compiler_params=_params(("parallel",)),
    )(ot, lse)


def _b_combine_bwd(dout, ot, lse):
    nh, _, s = ot.shape
    ng, hg, _ = lse.shape
    ts = min(512, s)

    def body(dout_ref, ot_ref, lse_ref, do_ref, delta_ref):
        alpha = _group_weights(lse_ref[...])
        for j in range(hg):
            e = [jnp.sum(dout_ref[g * hg + j].astype(F32) * ot_ref[g * hg + j], axis=0, keepdims=True)
                 for g in range(ng)]
            a = [alpha[g, j:j + 1, :] for g in range(ng)]
            mix = a[0] * e[0]
            for g in range(1, ng):
                mix = mix + a[g] * e[g]
            for g in range(ng):
                do_ref[g * hg + j] = (dout_ref[g * hg + j].astype(F32) * a[g]).astype(BF16)
                delta_ref[g * hg + j] = a[g] * mix

    blk = pl.BlockSpec((nh, HEAD_DIM, ts), lambda i: (0, 0, i))
    return pl.pallas_call(
        body, name="b_combine_bwd", grid=(s // ts,),
        in_specs=[blk, blk, pl.BlockSpec((ng, hg, ts), lambda i: (0, 0, i))],
        out_specs=[blk, pl.BlockSpec((nh, 1, ts), lambda i: (0, 0, i))],
        out_shape=[jax.ShapeDtypeStruct((nh, HEAD_DIM, s), BF16), jax.ShapeDtypeStruct((nh, 1, s), F32)],
        compiler_params=_params(("parallel",)),
    )(dout, ot, lse)


def _alibi_slopes(n):
    return 2.0 ** (-8.0 * jnp.arange(1, n + 1, dtype=F32) / n)


def _per_head(v):
    return jnp.broadcast_to(v.astype(F32)[:, None, None], (v.shape[0], 1, LANES))


def _dilate(x, dil):
    if dil == 1:
        return x
    s = x.shape[-1]
    return jnp.swapaxes(x.reshape(x.shape[:-1] + (s // dil, dil)), -1, -2).reshape(x.shape)


def _undilate(x, dil):
    if dil == 1:
        return x
    s = x.shape[-1]
    return jnp.swapaxes(x.reshape(x.shape[:-1] + (dil, s // dil)), -1, -2).reshape(x.shape)


def _heads(x_t):
    return x_t.reshape(-1, HEAD_DIM, x_t.shape[-1])


B_MAX_STRIDE = 4


def _b_band(window, dilation, s):
    if dilation <= B_MAX_STRIDE:
        return _Band(window // 2, 1, s, dilation), 1
    return _Band(window // 2 // dilation, dilation, s // dilation, 1), dilation


def _mixer_fwd(kind, qkv, p, tabs, cargo, layer):
    s = qkv.shape[0 if kind == 0 else 1]
    if kind == 0:
        qt, k, v = _a_prep(qkv, tabs[0], tabs[1], p["gq2"], p["gk2"])
        ot, lse, brought = _a_attn_fwd(qt, k, v, cargo, name=f"a_attn_fwd_l{layer}")
        return ot.reshape(-1, s), dict(qt=qt, k=k, v=v, ot=ot, lse=lse), brought
    assert cargo is None
    if kind == 2:
        nq, nk = C_HEADS * HEAD_DIM, C_KV * HEAD_DIM
        qt = _heads(qkv[:nq])
        kp, vp = _heads(qkv[nq:nq + nk]), _heads(qkv[nq + nk:])
        ot, _ = _win_fwd(qt, kp, vp, p["slopes"], p["sinks"], band=_Band(C_WINDOW, 1, s, 1), out_dtype=BF16,
                         name="c_attn_fwd")
        return ot.reshape(-1, s), dict(qt=qt, kp=kp, vp=vp, ot=ot), ()
    ng, hg, kg = len(B_GROUPS), B_HEADS_PER_GROUP, B_KV_PER_GROUP
    nq, nk = ng * hg * HEAD_DIM, ng * kg * HEAD_DIM
    qt_all, kt_all, vt_all = _heads(qkv[:nq]), _heads(qkv[nq:nq + nk]), _heads(qkv[nq + nk:])
    saved, outs, lses = [], [], []
    for g, (window, dilation) in enumerate(B_GROUPS):
        band, dil = _b_band(window, dilation, s)
        qt = _dilate(qt_all[g * hg:(g + 1) * hg], dil)
        kp = _dilate(kt_all[g * kg:(g + 1) * kg], dil)
        vp = _dilate(vt_all[g * kg:(g + 1) * kg], dil)
        sl = p["slopes"][g * hg:(g + 1) * hg]
        ot, lse = _win_fwd(qt, kp, vp, sl, None, band=band, out_dtype=F32, name=f"b_attn_fwd_g{g}")
        saved.append(dict(qt=qt, kp=kp, vp=vp))
        outs.append(_undilate(ot, dil))
        lses.append(_undilate(lse[:, 0, :], dil))
    ot_all, lse_all = jnp.concatenate(outs, axis=0), jnp.stack(lses, axis=0)
    mixed = _b_combine_fwd(ot_all, lse_all)
    return mixed.reshape(-1, s), dict(groups=saved, ot=ot_all, lse=lse_all), ()


def _mixer_bwd(kind, do_t, qkv, sv, p, tabs, cargo, layer):
    s = do_t.shape[1]
    do_heads = _heads(do_t)
    small = {}
    if kind == 0:
        dqt, dkt, dvt, brought = _a_attn_bwd(sv["qt"], sv["k"], sv["v"], do_heads, sv["ot"], sv["lse"],
                                             cargo, name=f"a_attn_bwd_l{layer}")
        dqkv, dgq, dgk = _a_prep_bwd(dqt, dkt, dvt, qkv, tabs[0], tabs[1], p["gq2"], p["gk2"])
        small["q_gain"] = dgq[0, :HEAD_DIM] + dgq[0, HEAD_DIM:]
        small["k_gain"] = dgk[0, :HEAD_DIM] + dgk[0, HEAD_DIM:]
        return dqkv, small, brought
    assert cargo is None
    if kind == 2:
        dqt, dkt, dvt, dsink = _win_bwd(sv["qt"], sv["kp"], sv["vp"], p["slopes"], p["sinks"], do_heads,
                                        sv["ot"], None, band=_Band(C_WINDOW, 1, s, 1), name="c_attn_bwd")
        small["sinks"] = dsink[:, 0, 0]
        parts = [dqt.reshape(-1, s), dkt.reshape(-1, s), dvt.reshape(-1, s)]
        return jnp.concatenate(parts, axis=0).astype(BF16), small, ()
    ng, hg, kg = len(B_GROUPS), B_HEADS_PER_GROUP, B_KV_PER_GROUP
    do_own, delta = _b_combine_bwd(do_heads, sv["ot"], sv["lse"])
    dqs, dks, dvs = [], [], []
    for g, (window, dilation) in enumerate(B_GROUPS):
        band, dil = _b_band(window, dilation, s)
        gs = sv["groups"][g]
        dqt, dkt, dvt, _ = _win_bwd(gs["qt"], gs["kp"], gs["vp"], p["slopes"][g * hg:(g + 1) * hg], None,
                                    _dilate(do_own[g * hg:(g + 1) * hg], dil), None,
                                    _dilate(delta[g * hg:(g + 1) * hg], dil), band=band, name=f"b_attn_bwd_g{g}")
        dqs.append(_undilate(dqt, dil))
        dks.append(_undilate(dkt, dil))
        dvs.append(_undilate(dvt, dil))
    parts = [x.reshape(-1, s) for x in dqs + dks + dvs]
    return jnp.concatenate(parts, axis=0).astype(BF16), small, ()


LAYER_MATS = ("w_qkv", "w_o", "w1", "w2")
COLUMN_QUARTERS = ("w_qkv", "w1")


def _whole(key, gathered):
    q, r, c = gathered.shape
    if key == "w1":
        return gathered
    if key in COLUMN_QUARTERS:
        return jnp.transpose(gathered, (1, 0, 2)).reshape(r, q * c)
    return gathered.reshape(q * r, c)


def _quarters(key, g):
    r, c = g.shape
    if key in COLUMN_QUARTERS:
        return jnp.transpose(g.reshape(r, 4, c // 4), (1, 0, 2))
    return g.reshape(4, r // 4, c)


def _local_step(x, target, norms, mixer_params, shards, whole=None):
    s = x.shape[0]
    tabs = _rope_tables(s)
    if whole is None:
        assert MIXER_OF_LAYER[0][0] == 0
        first = _run_cargo(_gather_cargo([shards[0][key] for key in LAYER_MATS]), name="gather_l0")
        mats = {0: {key: _whole(key, g) for key, g in zip(LAYER_MATS, first)}}
        later = _gather_cargo([shards[layer][key] for layer in range(1, DEPTH) for key in LAYER_MATS])
    else:
        mats, later = dict(enumerate(whole)), None
    h = x
    saved = []
    for layer in range(DEPTH):
        kind = layer % N_MIXERS
        w, p = mats[layer], mixer_params[layer]
        hn, qkv = _norm_mm(h, norms["attn"][layer][None], w["w_qkv"], out_dtype=F32 if kind == 0 else BF16,
                           relu2=False, transpose_out=kind != 0, name=f"qkv_proj_l{layer}")
        o_t, sv, brought = _mixer_fwd(kind, qkv, p, tabs, later if layer == 0 else None, layer)
        for n, g in enumerate(brought):
            mats.setdefault(1 + n // len(LAYER_MATS), {})[LAYER_MATS[n % len(LAYER_MATS)]] = _whole(
                LAYER_MATS[n % len(LAYER_MATS)], g)
        h_mid = _mm_res(o_t, w["w_o"], h, a_transposed=True, name=f"o_proj_l{layer}")
        hn2, act = _norm_mm(h_mid, norms["mlp"][layer][None], w["w1"], out_dtype=BF16, relu2=True,
                            transpose_out=False, name=f"mlp_up_l{layer}")
        h_out = _mm_res(act, w["w2"], h_mid, a_transposed=False, name=f"mlp_down_l{layer}")
        saved.append(dict(h=h, hn=hn, qkv=qkv, o_t=o_t, mix=sv, h_mid=h_mid, hn2=hn2, act=act))
        h = h_out

    dh, loss, d_final = _loss_head(h, norms["final"][None], target)

    own, received, pending = {}, {}, []
    d_attn, d_mlp, small = [None] * DEPTH, [None] * DEPTH, [None] * DEPTH
    for layer in reversed(range(DEPTH)):
        kind = layer % N_MIXERS
        w, p, sv = mats[layer], mixer_params[layer], saved[layer]
        du = _mm_nt(dh, w["w2"], sv["act"], transpose_out=False, name=f"mlp_down_bwd_l{layer}")
        own[layer, "w2"] = _quarters("w2", _mm_tn(sv["act"], dh, x_transposed=False, g_transposed=False,
                                                  column_quarters=False, name=f"mlp_w2_grad_l{layer}"))
        own[layer, "w1"] = _mm_tn(sv["hn2"], du, x_transposed=False, g_transposed=False, column_quarters=True,
                                  name=f"mlp_w1_grad_l{layer}")
        dh_mid, d_mlp[layer] = _mm_nt_normbwd(du, w["w1"], sv["h_mid"], norms["mlp"][layer][None], dh,
                                              g_transposed=False, name=f"mlp_up_bwd_l{layer}")
        do_t = _mm_nt(dh_mid, w["w_o"], None, transpose_out=True, name=f"o_proj_bwd_l{layer}")
        own[layer, "w_o"] = _quarters("w_o", _mm_tn(sv["o_t"], dh_mid, x_transposed=True, g_transposed=False,
                                                    column_quarters=False, name=f"w_o_grad_l{layer}"))
        pending += [(layer, "w2"), (layer, "w1"), (layer, "w_o")]
        cargo = None
        if kind == 0 and whole is None:
            cargo, sent, pending = _scatter_cargo([own[item] for item in pending], None), pending, []
        dqkv, small[layer], brought = _mixer_bwd(kind, do_t, sv["qkv"], sv["mix"], p, tabs, cargo, layer)
        if cargo is not None:
            received.update(zip(sent, brought))
        own[layer, "w_qkv"] = _quarters("w_qkv", _mm_tn(sv["hn"], dqkv, x_transposed=False, g_transposed=kind != 0,
                                                        column_quarters=False, name=f"w_qkv_grad_l{layer}"))
        pending.append((layer, "w_qkv"))
        dh, d_attn[layer] = _mm_nt_normbwd(dqkv, w["w_qkv"], sv["h"], norms["attn"][layer][None], dh_mid,
                                           g_transposed=kind != 0, name=f"qkv_proj_bwd_l{layer}")
    return loss, dh, own, received, pending, dict(attn=d_attn, mlp=d_mlp, final=d_final, mixer=small)


CHIP_FLIPS = ((1, 0), (0, 1), (1, 1))


class _Cargo(NamedTuple):
    ins: tuple
    out_shape: tuple
    sem_shapes: tuple
    start: Callable
    wait: Callable


class _Carried(NamedTuple):
    body: Callable
    in_specs: list
    out_specs: list
    out_shape: list
    scratch: list
    args: tuple


def _carry(cargo, grid, n_in, n_out, body):
    if cargo is None:
        return _Carried(body, [], [], [], [], ())
    ci, co = len(cargo.ins), len(cargo.out_shape)

    def wrapped(*refs):
        ins, c_ins = refs[:n_in], refs[n_in:n_in + ci]
        outs, c_outs = refs[n_in + ci:n_in + ci + n_out], refs[n_in + ci + n_out:n_in + ci + n_out + co]
        sems = refs[n_in + ci + n_out + co:]
        first = last = None
        for axis, extent in enumerate(grid):
            at = pl.program_id(axis)
            first = (at == 0) if first is None else first & (at == 0)
            last = (at == extent - 1) if last is None else last & (at == extent - 1)

        @pl.when(first)
        def _():
            cargo.start(c_ins, c_outs, sems)

        body(*ins, *outs)

        @pl.when(last)
        def _():
            cargo.wait(c_ins, c_outs, sems)

    return _Carried(wrapped, [ANY] * ci, [ANY] * co, list(cargo.out_shape), list(cargo.sem_shapes), tuple(cargo.ins))


def _run_cargo(cargo, *, name):
    ci, co = len(cargo.ins), len(cargo.out_shape)

    def body(*refs):
        cargo.start(refs[:ci], refs[ci:ci + co], refs[ci + co:])
        cargo.wait(refs[:ci], refs[ci:ci + co], refs[ci + co:])

    return pl.pallas_call(body, name=name, in_specs=[ANY] * ci, out_specs=[ANY] * co, out_shape=list(cargo.out_shape),
                          scratch_shapes=list(cargo.sem_shapes))(*cargo.ins)


def _other_chip(x, y, j):
    fx, fy = CHIP_FLIPS[j]
    return (1 - x if fx else x), (1 - y if fy else y)


def _gather_cargo(shards):
    n = len(shards)
    halves = [a.shape[0] // 2 for a in shards]

    def copies(ins, outs, sems):
        ici_send, ici_recv, d2d_send, d2d_recv, local_sems = sems
        x, y, c = lax.axis_index("x"), lax.axis_index("y"), lax.axis_index("c")
        me = 2 * x + y

        def half(t, which):
            return pl.ds(pl.multiple_of(which * halves[t], 16), halves[t])

        def over_ici(t, j, arriving):
            px, py = _other_chip(x, y, j)
            return pltpu.make_async_remote_copy(
                src_ref=ins[t].at[half(t, c)], dst_ref=outs[t].at[2 * px + py if arriving else me, half(t, c)],
                send_sem=ici_send.at[t, j], recv_sem=ici_recv.at[t, j], device_id=(px, py, c), device_id_type=MESH)

        def over_d2d(t, j, arriving):
            px, py = _other_chip(x, y, j)
            mine = outs[t].at[2 * px + py, half(t, c)]
            return pltpu.make_async_remote_copy(
                src_ref=mine, dst_ref=outs[t].at[2 * px + py, half(t, 1 - c)] if arriving else mine,
                send_sem=d2d_send.at[t, j], recv_sem=d2d_recv.at[t, j], device_id=(x, y, 1 - c), device_id_type=MESH)

        return over_ici, over_d2d, lambda t: pltpu.make_async_copy(ins[t], outs[t].at[me], local_sems.at[t])

    def start(ins, outs, sems):
        over_ici, _, own = copies(ins, outs, sems)
        for t in range(n):
            own(t).start()
            for j in range(len(CHIP_FLIPS)):
                over_ici(t, j, False).start()

    def wait(ins, outs, sems):
        over_ici, over_d2d, own = copies(ins, outs, sems)
        for t in range(n):
            for j in range(len(CHIP_FLIPS)):
                over_ici(t, j, True).wait_recv()
                over_d2d(t, j, False).start()
        for t in range(n):
            for j in range(len(CHIP_FLIPS)):
                over_d2d(t, j, True).wait_recv()
                over_d2d(t, j, False).wait_send()
                over_ici(t, j, False).wait_send()
            own(t).wait()

    dma = pltpu.SemaphoreType.DMA
    return _Cargo(tuple(shards), tuple(jax.ShapeDtypeStruct((4,) + a.shape, a.dtype) for a in shards),
                  (dma((n, 3)), dma((n, 3)), dma((n, 3)), dma((n, 3)), dma((n,))), start, wait)


def _scatter_cargo(grads, small):
    n = len(grads)

    def copies(ins, outs, sems):
        x, y, c = lax.axis_index("x"), lax.axis_index("y"), lax.axis_index("c")
        me = 4 * x + 2 * y + c

        def remote(t, j):
            px, py = _other_chip(x, y, j)
            return pltpu.make_async_remote_copy(
                src_ref=ins[t].at[2 * px + py], dst_ref=outs[t].at[j], send_sem=sems[0].at[t, j],
                recv_sem=sems[1].at[t, j], device_id=(px, py, c), device_id_type=MESH)

        def small_remote(r, arriving):
            fx, fy, fc = (r + 1) // 4, ((r + 1) // 2) % 2, (r + 1) % 2
            px, py, pc = (1 - x if fx else x), (1 - y if fy else y), (1 - c if fc else c)
            return pltpu.make_async_remote_copy(
                src_ref=ins[n], dst_ref=outs[n].at[4 * px + 2 * py + pc if arriving else me],
                send_sem=sems[2].at[r], recv_sem=sems[3].at[r], device_id=(px, py, pc), device_id_type=MESH)

        return remote, small_remote, lambda: pltpu.make_async_copy(ins[n], outs[n].at[me], sems[4])

    def start(ins, outs, sems):
        remote, small_remote, small_own = copies(ins, outs, sems)
        if small is not None:
            small_own().start()
            for r in range(7):
                small_remote(r, False).start()
        for t in range(n):
            for j in range(len(CHIP_FLIPS)):
                remote(t, j).start()

    def wait(ins, outs, sems):
        remote, small_remote, small_own = copies(ins, outs, sems)
        if small is not None:
            for r in range(7):
                small_remote(r, True).wait_recv()
                small_remote(r, False).wait_send()
            small_own().wait()
        for t in range(n):
            for j in range(len(CHIP_FLIPS)):
                remote(t, j).wait()

    dma = pltpu.SemaphoreType.DMA
    ins = tuple(grads) + (() if small is None else (small,))
    out_shape = tuple(jax.ShapeDtypeStruct((3,) + g.shape[1:], g.dtype) for g in grads)
    sem_shapes = (dma((n, 3)), dma((n, 3)))
    if small is not None:
        out_shape += (jax.ShapeDtypeStruct((8,) + small.shape, small.dtype),)
        sem_shapes += (dma((7,)), dma((7,)), dma(()))
    return _Cargo(ins, out_shape, sem_shapes, start, wait)


def _swap_cores(parts):
    n = len(parts)

    def body(*refs):
        ins, outs = refs[:n], refs[n:2 * n]
        send_sems, recv_sems = refs[2 * n:]
        peer = (lax.axis_index("x"), lax.axis_index("y"), 1 - lax.axis_index("c"))
        copies = [pltpu.make_async_remote_copy(src_ref=ins[t], dst_ref=outs[t], send_sem=send_sems.at[t],
                                               recv_sem=recv_sems.at[t], device_id=peer, device_id_type=MESH)
                  for t in range(n)]
        for cp in copies:
            cp.start()
        for cp in copies:
            cp.wait()

    return pl.pallas_call(
        body, name="swap_cores", in_specs=[ANY] * n, out_specs=[ANY] * n,
        out_shape=[jax.ShapeDtypeStruct(a.shape, a.dtype) for a in parts],
        scratch_shapes=[pltpu.SemaphoreType.DMA((n,)), pltpu.SemaphoreType.DMA((n,))],
    )(*parts)


def _rows_tile(r):
    return 256 if r % 256 == 0 else r


def _sum_quarters(own, recv, *, name):
    r, c = own.shape
    tr = _rows_tile(r)

    def body(own_ref, recv_ref, o_ref):
        acc = own_ref[...].astype(F32)
        for j in range(3):
            acc = acc + recv_ref[j].astype(F32)
        o_ref[...] = acc.astype(BF16)

    return pl.pallas_call(
        body, name=name, grid=(r // tr,),
        in_specs=[pl.BlockSpec((tr, c), lambda i: (i, 0)), pl.BlockSpec((3, tr, c), lambda i: (0, i, 0))],
        out_specs=pl.BlockSpec((tr, c), lambda i: (i, 0)),
        out_shape=jax.ShapeDtypeStruct((r, c), BF16),
        compiler_params=_params(("parallel",)),
    )(own, recv)


def _adamw(w, m, v, parts, *, name):
    r, c = w.shape
    tr = _rows_tile(r)
    c1, c2 = 1.0 - ADAM_B1 ** ADAM_STEP, 1.0 - ADAM_B2 ** ADAM_STEP
    n_parts = len(parts)

    def body(*refs):
        w_ref, m_ref, v_ref = refs[:3]
        g_ref, d_ref, nm_ref, nv_ref = refs[3 + n_parts:]
        terms = []
        for p_ref in refs[3:3 + n_parts]:
            terms += [p_ref[...]] if len(p_ref.shape) == 2 else [p_ref[j] for j in range(p_ref.shape[0])]
        g = terms[0].astype(F32)
        for term in terms[1:]:
            g = g + term.astype(F32)
        m_new = ADAM_B1 * m_ref[...] + (1.0 - ADAM_B1) * g
        v_new = ADAM_B2 * v_ref[...] + (1.0 - ADAM_B2) * (g * g)
        step = (m_new / c1) / (jnp.sqrt(v_new / c2) + ADAM_EPS)
        g_ref[...] = g
        d_ref[...] = -ADAM_LR * (step + ADAM_WD * w_ref[...])
        nm_ref[...] = m_new
        nv_ref[...] = v_new

    blk = pl.BlockSpec((tr, c), lambda i: (i, 0))
    part_specs = [blk if p.ndim == 2 else pl.BlockSpec((p.shape[0], tr, c), lambda i: (0, i, 0)) for p in parts]
    return pl.pallas_call(
        body, name=name, grid=(r // tr,), in_specs=[blk, blk, blk] + part_specs,
        out_specs=[blk] * 4, out_shape=[jax.ShapeDtypeStruct((r, c), F32)] * 4,
        compiler_params=_params(("parallel",)),
    )(w, m, v, *parts)


MATS = ("a_w_qkv", "a_w_o", "b_w_qkv", "b_w_o", "c_w_qkv", "c_w_o", "mlp_w1", "mlp_w2")
SMALLS = ("attn_norm", "mlp_norm", "a_q_gain", "a_k_gain", "c_sinks", "final_norm")
WEIGHTS = ("attn_norm", "mlp_norm", "a_w_qkv", "a_q_gain", "a_k_gain", "a_w_o", "b_w_qkv", "b_w_o", "c_w_qkv",
           "c_sinks", "c_w_o", "mlp_w1", "mlp_w2", "final_norm")
MIXER_OF_LAYER = tuple((layer % N_MIXERS, sum(1 for q in range(layer) if q % N_MIXERS == layer % N_MIXERS))
                       for layer in range(DEPTH))
SMALL_ROWS = 8


def _pack_small(values):
    rows, spans, at = [], [], 0
    for v in values:
        flat = v.reshape(-1)
        n = -(-flat.shape[0] // (SMALL_ROWS * LANES)) * SMALL_ROWS
        rows.append(jnp.pad(flat, (0, n * LANES - flat.shape[0])).reshape(n, LANES))
        spans.append((at, n))
        at += n
    return jnp.concatenate(rows, axis=0), spans


def kernel(x, attn_norm, mlp_norm, a_w_qkv, a_q_gain, a_k_gain, a_w_o, b_w_qkv, b_w_o, c_w_qkv, c_sinks, c_w_o, mlp_w1, mlp_w2, final_norm, loss_target, m_attn_norm, m_mlp_norm, m_a_w_qkv, m_a_q_gain, m_a_k_gain, m_a_w_o, m_b_w_qkv, m_b_w_o, m_c_w_qkv, m_c_sinks, m_c_w_o, m_mlp_w1, m_mlp_w2, m_final_norm, v_attn_norm, v_mlp_norm, v_a_w_qkv, v_a_q_gain, v_a_k_gain, v_a_w_o, v_b_w_qkv, v_b_w_o, v_c_w_qkv, v_c_sinks, v_c_w_o, v_mlp_w1, v_mlp_w2, v_final_norm):
    env = dict(locals())
    w = {name: env[name] for name in WEIGHTS}
    mom = {name: (env["m_" + name], env["v_" + name]) for name in WEIGHTS}

    prefix = ("a", "b", "c")
    shards, mixer_params = [], []
    for layer, (kind, j) in enumerate(MIXER_OF_LAYER):
        shards.append(dict(w_qkv=w[prefix[kind] + "_w_qkv"][j].astype(BF16), w_o=w[prefix[kind] + "_w_o"][j].astype(BF16),
                           w1=mlp_w1[layer].astype(BF16), w2=mlp_w2[layer].astype(BF16)))
        if kind == 0:
            mixer_params.append(dict(gq2=jnp.tile(a_q_gain[j], 2)[None], gk2=jnp.tile(a_k_gain[j], 2)[None]))
        elif kind == 1:
            mixer_params.append(dict(slopes=_per_head(_alibi_slopes(len(B_GROUPS) * B_HEADS_PER_GROUP))))
        else:
            mixer_params.append(dict(slopes=_per_head(_alibi_slopes(C_HEADS)), sinks=_per_head(c_sinks[j])))

    norms = dict(attn=attn_norm, mlp=mlp_norm, final=final_norm)
    loss_part, grad_x, own, received, pending, g_small = _local_step(x[0], loss_target[0], norms, mixer_params, shards)
    loss = lax.psum(loss_part[0, 0], ("x", "y", "c"))

    of_kind = lambda kind, key: jnp.stack([g_small["mixer"][layer][key] for layer, (k, _) in enumerate(MIXER_OF_LAYER)
                                           if k == kind])
    small_grads = dict(
        attn_norm=jnp.concatenate(g_small["attn"], axis=0), mlp_norm=jnp.concatenate(g_small["mlp"], axis=0),
        a_q_gain=of_kind(0, "q_gain"), a_k_gain=of_kind(0, "k_gain"), c_sinks=of_kind(2, "sinks"),
        final_norm=g_small["final"][0])
    packed, spans = _pack_small([small_grads[name] for name in SMALLS])
    *last, all_small = _run_cargo(_scatter_cargo([own[item] for item in pending], packed), name="scatter_last")
    received.update(zip(pending, last))

    me_chip = 2 * lax.axis_index("x") + lax.axis_index("y")
    partial = []
    for name in MATS:
        key = name[2:] if name[0] in "abc" else name[4:]
        layers = [layer for layer, (kind, _) in enumerate(MIXER_OF_LAYER)
                  if name.startswith("mlp") or prefix[kind] == name[0]]
        sums = [_sum_quarters(lax.dynamic_index_in_dim(own[layer, key], me_chip, axis=0, keepdims=False),
                              received[layer, key], name=f"sum_{name}_l{layer}") for layer in layers]
        partial.append(jnp.concatenate(sums, axis=0))
    other = _swap_cores(partial)

    out = {}
    for name, mine, theirs in zip(MATS, partial, other):
        shape = w[name].shape
        res = _adamw(*[a.reshape(-1, shape[-1]) for a in (w[name], *mom[name])], [mine, theirs], name=f"adamw_{name}")
        out[name] = [a.reshape(shape) for a in res]
    for name, (at, n) in zip(SMALLS, spans):
        shape = w[name].shape
        packed_in = [_pack_small([a])[0] for a in (w[name], *mom[name])]
        res = _adamw(*packed_in, [all_small[:, at:at + n]], name=f"adamw_{name}")
        out[name] = [a.reshape(-1)[:w[name].size].reshape(shape) for a in res]

    return (loss, grad_x[None], *[out[name][0] for name in WEIGHTS], *[out[name][1] for name in WEIGHTS],
            *[out[name][2] for name in WEIGHTS], *[out[name][3] for name in WEIGHTS])
```

```python
from typing import Callable, NamedTuple

import jax
import jax.numpy as jnp
from jax import lax
from jax.experimental import pallas as pl
from jax.experimental.pallas import tpu as pltpu

F32 = jnp.float32
BF16 = jnp.bfloat16
MESH = pl.DeviceIdType.MESH
ANY = pl.BlockSpec(memory_space=pl.ANY)

D_MODEL = 1024
HEAD_DIM = 64
GRID_W = 64
ROPE_THETA = 10000.0
RMS_EPS = 1e-6
QK_SCALE = HEAD_DIM ** -0.5
LOG2E = 1.4426950408889634
LN2 = 0.6931471805599453
A_HEADS, A_KV = 16, 4
B_GROUPS = ((128, 1), (512, 4), (2048, 16))
B_HEADS_PER_GROUP, B_KV_PER_GROUP = 6, 2
C_HEADS, C_KV, C_WINDOW = 16, 4, 128
DEPTH, N_MIXERS = 4, 3
ADAM_LR, ADAM_B1, ADAM_B2, ADAM_EPS, ADAM_WD, ADAM_STEP = 0.001, 0.9, 0.999, 1e-08, 0.01, 10

WIN_REACH = 128
V7X_VMEM_BUDGET = 48 * 1024 * 1024
LANES = 128
ROW_TILE = 1024


def _params(semantics):
    return pltpu.CompilerParams(dimension_semantics=semantics, vmem_limit_bytes=V7X_VMEM_BUDGET)


def _tile(n, cap):
    if n <= cap:
        return n
    t = (cap // LANES) * LANES
    while n % t:
        t -= LANES
    return t


def _norm_mm(h, gain, w, *, out_dtype, relu2, transpose_out, name):
    m, d = h.shape
    by_quarter = w.ndim == 3
    assert not (by_quarter and transpose_out)
    n = w.shape[-1] * (4 if by_quarter else 1)
    per_step = 2 if by_quarter and 2 * w.shape[-1] <= 2048 else 1
    tm, tn = min(ROW_TILE, m), (per_step * w.shape[-1] if by_quarter else _tile(n, 2048))
    w_spec = (pl.BlockSpec((per_step, d, tn // per_step), lambda i, j: (j, 0, 0)) if by_quarter
              else pl.BlockSpec((d, tn), lambda i, j: (0, j)))
    y_spec = (pl.BlockSpec((tn, tm), lambda i, j: (j, i)) if transpose_out
              else pl.BlockSpec((tm, tn), lambda i, j: (i, j)))

    def body(h_ref, g_ref, w_ref, hn_ref, y_ref):
        @pl.when(pl.program_id(1) == 0)
        def _():
            x = h_ref[...]
            r = lax.rsqrt(jnp.mean(x * x, axis=-1, keepdims=True) + RMS_EPS)
            hn_ref[...] = (x * r * g_ref[...]).astype(BF16)

        def finish(y):
            if relu2:
                y = jnp.maximum(y, 0.0)
                y = y * y
            return y.astype(y_ref.dtype)

        if transpose_out:
            y_ref[...] = finish(lax.dot_general(w_ref[...], hn_ref[...], (((0,), (1,)), ((), ())),
                                                preferred_element_type=F32))
        elif by_quarter:
            cols = tn // per_step
            for q in range(per_step):
                y_ref[:, q * cols:(q + 1) * cols] = finish(jnp.dot(hn_ref[...], w_ref[q], preferred_element_type=F32))
        else:
            y_ref[...] = finish(jnp.dot(hn_ref[...], w_ref[...], preferred_element_type=F32))

    return pl.pallas_call(
        body, name=name, grid=(m // tm, n // tn),
        in_specs=[pl.BlockSpec((tm, d), lambda i, j: (i, 0)), pl.BlockSpec((1, d), lambda i, j: (0, 0)), w_spec],
        out_specs=[pl.BlockSpec((tm, d), lambda i, j: (i, 0)), y_spec],
        out_shape=[jax.ShapeDtypeStruct((m, d), BF16), jax.ShapeDtypeStruct((n, m) if transpose_out else (m, n), out_dtype)],
        compiler_params=_params(("parallel", "arbitrary")),
    )(h, gain, w)


def _mm_res(a, w, h_in, *, a_transposed, name):
    k, d = w.shape
    m = h_in.shape[0]
    tm, tk = min(ROW_TILE, m), _tile(k, 2048)
    lhs_contracts = 0 if a_transposed else 1

    def body(a_ref, w_ref, h_ref, o_ref):
        @pl.when(pl.program_id(1) == 0)
        def _():
            o_ref[...] = h_ref[...]

        o_ref[...] += lax.dot_general(a_ref[...], w_ref[...], (((lhs_contracts,), (0,)), ((), ())),
                                      preferred_element_type=F32)

    a_spec = (pl.BlockSpec((tk, tm), lambda i, j: (j, i)) if a_transposed
              else pl.BlockSpec((tm, tk), lambda i, j: (i, j)))
    return pl.pallas_call(
        body, name=name, grid=(m // tm, k // tk),
        in_specs=[a_spec, pl.BlockSpec((tk, d), lambda i, j: (j, 0)), pl.BlockSpec((tm, d), lambda i, j: (i, 0))],
        out_specs=pl.BlockSpec((tm, d), lambda i, j: (i, 0)),
        out_shape=jax.ShapeDtypeStruct((m, d), F32),
        compiler_params=_params(("parallel", "arbitrary")),
    )(a, w, h_in)


def _mm_nt(a, w, act, *, transpose_out, name):
    m, d = a.shape
    n = w.shape[0]
    tm, tn = min(ROW_TILE, m), _tile(n, 1152)
    assert act is None or not transpose_out
    nt = (((1,), (1,)), ((), ()))

    def body(*refs):
        a_ref, w_ref = refs[0], refs[1]
        o_ref = refs[-1]
        if transpose_out:
            acc = lax.dot_general(w_ref[...], a_ref[...].astype(BF16), nt, preferred_element_type=F32)
        else:
            acc = lax.dot_general(a_ref[...].astype(BF16), w_ref[...], nt, preferred_element_type=F32)
        if act is not None:
            acc = acc * (2.0 * jnp.sqrt(refs[2][...].astype(F32)))
        o_ref[...] = acc.astype(BF16)

    in_specs = [pl.BlockSpec((tm, d), lambda i, j: (i, 0)), pl.BlockSpec((tn, d), lambda i, j: (j, 0))]
    args = [a, w]
    if act is not None:
        in_specs.append(pl.BlockSpec((tm, tn), lambda i, j: (i, j)))
        args.append(act)
    out_spec = (pl.BlockSpec((tn, tm), lambda i, j: (j, i)) if transpose_out
                else pl.BlockSpec((tm, tn), lambda i, j: (i, j)))
    return pl.pallas_call(
        body, name=name, grid=(m // tm, n // tn), in_specs=in_specs, out_specs=out_spec,
        out_shape=jax.ShapeDtypeStruct((n, m) if transpose_out else (m, n), BF16),
        compiler_params=_params(("parallel", "parallel")),
    )(*args)


def _rmsnorm_bwd(dn, x, gain):
    r = lax.rsqrt(jnp.mean(x * x, axis=-1, keepdims=True) + RMS_EPS)
    xh = x * r
    dgain = jnp.sum(dn * xh, axis=0, keepdims=True)
    u = dn * gain
    dx = r * (u - xh * jnp.mean(u * xh, axis=-1, keepdims=True))
    return dx, dgain


def _mm_nt_normbwd(g, w, h, gain, dh_in, *, g_transposed, name):
    m, k = g.shape[::-1] if g_transposed else g.shape
    by_quarter = w.ndim == 3
    d = w.shape[-2]
    tm, tk = min(ROW_TILE, m), (w.shape[-1] if by_quarter else _tile(k, 1024))
    sub = min(256, tm)
    nk = k // tk
    w_spec = (pl.BlockSpec((None, d, tk), lambda i, j: (j, 0, 0)) if by_quarter
              else pl.BlockSpec((d, tk), lambda i, j: (0, j)))

    def body(g_ref, w_ref, h_ref, gain_ref, dh_ref, o_ref, dg_ref, acc_ref):
        i, j = pl.program_id(0), pl.program_id(1)

        @pl.when((i == 0) & (j == 0))
        def _():
            dg_ref[...] = jnp.zeros_like(dg_ref)

        @pl.when(j == 0)
        def _():
            acc_ref[...] = jnp.zeros_like(acc_ref)

        acc_ref[...] += lax.dot_general(g_ref[...], w_ref[...], (((0 if g_transposed else 1,), (1,)), ((), ())),
                                        preferred_element_type=F32)

        @pl.when(j == nk - 1)
        def _():
            for r in range(0, tm, sub):
                rows = slice(r, r + sub)
                dx, dgain = _rmsnorm_bwd(acc_ref[rows, :], h_ref[rows, :], gain_ref[...])
                dg_ref[...] += dgain
                o_ref[rows, :] = dh_ref[rows, :] + dx

    return pl.pallas_call(
        body, name=name, grid=(m // tm, nk),
        in_specs=[pl.BlockSpec((tk, tm), lambda i, j: (j, i)) if g_transposed
                  else pl.BlockSpec((tm, tk), lambda i, j: (i, j)), w_spec,
                  pl.BlockSpec((tm, d), lambda i, j: (i, 0)), pl.BlockSpec((1, d), lambda i, j: (0, 0)),
                  pl.BlockSpec((tm, d), lambda i, j: (i, 0))],
        out_specs=[pl.BlockSpec((tm, d), lambda i, j: (i, 0)), pl.BlockSpec((1, d), lambda i, j: (0, 0))],
        out_shape=[jax.ShapeDtypeStruct((m, d), F32), jax.ShapeDtypeStruct((1, d), F32)],
        scratch_shapes=[pltpu.VMEM((tm, d), F32)],
        compiler_params=_params(("arbitrary", "arbitrary")),
    )(g, w, h, gain, dh_in)


def _mm_tn(x, g, *, x_transposed, g_transposed, column_quarters, name):
    k, m = x.shape if x_transposed else x.shape[::-1]
    n = g.shape[0] if g_transposed else g.shape[1]
    tm, tk, tn = min(2 * ROW_TILE, m), _tile(k, 1152), (n // 4 if column_quarters else _tile(n, 1024))
    nm = m // tm
    lhs_contracts, rhs_contracts = (1 if x_transposed else 0), (1 if g_transposed else 0)
    g_spec = (pl.BlockSpec((tn, tm), lambda a, b, s: (b, s)) if g_transposed
              else pl.BlockSpec((tm, tn), lambda a, b, s: (s, b)))

    def body(x_ref, g_ref, o_ref, acc_ref):
        s = pl.program_id(2)

        @pl.when(s == 0)
        def _():
            acc_ref[...] = jnp.zeros_like(acc_ref)

        acc_ref[...] += lax.dot_general(x_ref[...], g_ref[...].astype(BF16),
                                        (((lhs_contracts,), (rhs_contracts,)), ((), ())), preferred_element_type=F32)

        @pl.when(s == nm - 1)
        def _():
            o_ref[...] = acc_ref[...].astype(BF16)

    x_spec = (pl.BlockSpec((tk, tm), lambda a, b, s: (a, s)) if x_transposed
              else pl.BlockSpec((tm, tk), lambda a, b, s: (s, a)))
    out_spec = (pl.BlockSpec((None, tk, tn), lambda a, b, s: (b, a, 0)) if column_quarters
                else pl.BlockSpec((tk, tn), lambda a, b, s: (a, b)))
    return pl.pallas_call(
        body, name=name, grid=(k // tk, n // tn, nm),
        in_specs=[x_spec, g_spec], out_specs=out_spec,
        out_shape=jax.ShapeDtypeStruct((4, k, n // 4) if column_quarters else (k, n), BF16),
        scratch_shapes=[pltpu.VMEM((tk, tn), F32)],
        compiler_params=_params(("parallel", "parallel", "arbitrary")),
    )(x, g)


def _loss_head(h, gain, target):
    m, d = h.shape
    tm = 512

    def body(h_ref, g_ref, t_ref, dh_ref, loss_ref, dg_ref):
        @pl.when(pl.program_id(0) == 0)
        def _():
            loss_ref[...] = jnp.zeros_like(loss_ref)
            dg_ref[...] = jnp.zeros_like(dg_ref)

        x = h_ref[...]
        gain_v = g_ref[...]
        r = lax.rsqrt(jnp.mean(x * x, axis=-1, keepdims=True) + RMS_EPS)
        err = x * r * gain_v - t_ref[...]
        loss_ref[...] += 0.5 * jnp.sum(jnp.mean(err * err, axis=-1, keepdims=True), axis=0, keepdims=True)
        dx, dgain = _rmsnorm_bwd(err * (1.0 / d), x, gain_v)
        dg_ref[...] += dgain
        dh_ref[...] = dx

    return pl.pallas_call(
        body, name="loss_head", grid=(m // tm,),
        in_specs=[pl.BlockSpec((tm, d), lambda i: (i, 0)), pl.BlockSpec((1, d), lambda i: (0, 0)),
                  pl.BlockSpec((tm, d), lambda i: (i, 0))],
        out_specs=[pl.BlockSpec((tm, d), lambda i: (i, 0)), pl.BlockSpec((1, LANES), lambda i: (0, 0)),
                   pl.BlockSpec((1, d), lambda i: (0, 0))],
        out_shape=[jax.ShapeDtypeStruct((m, d), F32), jax.ShapeDtypeStruct((1, LANES), F32),
                   jax.ShapeDtypeStruct((1, d), F32)],
        compiler_params=_params(("arbitrary",)),
    )(h, gain, target)


def _rope_tables(s):
    t = jnp.arange(s)
    row = (t // GRID_W).astype(F32)
    col = (t % GRID_W).astype(F32)
    axis_dim = HEAD_DIM // 2
    inv_freq = ROPE_THETA ** (-jnp.arange(0, axis_dim, 2, dtype=F32) / axis_dim)
    ar, ac = row[:, None] * inv_freq, col[:, None] * inv_freq
    cos = jnp.concatenate([jnp.cos(ar), jnp.cos(ar), jnp.cos(ac), jnp.cos(ac)], axis=-1)
    sin = jnp.concatenate([-jnp.sin(ar), jnp.sin(ar), -jnp.sin(ac), jnp.sin(ac)], axis=-1)
    return jnp.tile(cos, (1, 2)), jnp.tile(sin, (1, 2))


def _swap16(x):
    lane = lax.broadcasted_iota(jnp.int32, x.shape, 1)
    return jnp.where((lane % 32) < 16, pltpu.roll(x, LANES - 16, 1), pltpu.roll(x, 16, 1))


def _head_mean(v):
    lane = lax.broadcasted_iota(jnp.int32, v.shape, 1)
    lo = lane < HEAD_DIM
    s_all = jnp.sum(v, axis=-1, keepdims=True)
    s_lo = jnp.sum(jnp.where(lo, v, 0.0), axis=-1, keepdims=True)
    return jnp.where(lo, s_lo, s_all - s_lo) * (1.0 / HEAD_DIM)


def _head_rstd(x):
    return lax.rsqrt(_head_mean(x * x) + RMS_EPS)


def _norm_rope(x, r, gain2, cos, sin):
    nrm = x * r * gain2
    return nrm * cos + _swap16(nrm) * sin


def _norm_rope_bwd(dy, x, r, gain2, cos, sin):
    dn = dy * cos + _swap16(dy * sin)
    xh = x * r
    dgain = jnp.sum(dn * xh, axis=0, keepdims=True)
    u = dn * gain2
    return r * (u - xh * _head_mean(u * xh)), dgain


def _a_prep(qkv, cos, sin, gq2, gk2):
    s = qkv.shape[0]
    tr = 256
    nq, nk = A_HEADS * HEAD_DIM, A_KV * HEAD_DIM

    def body(qkv_ref, cos_ref, sin_ref, gq_ref, gk_ref, qt_ref, k_ref, v_ref):
        cos_v, sin_v = cos_ref[...], sin_ref[...]
        rstd = [_head_rstd(qkv_ref[:, c * LANES:(c + 1) * LANES]) for c in range((nq + nk) // LANES)]
        for c in range(nq // LANES):
            y = _norm_rope(qkv_ref[:, c * LANES:(c + 1) * LANES], rstd[c], gq_ref[...], cos_v, sin_v)
            yt = (y * (QK_SCALE * LOG2E)).T
            qt_ref[2 * c] = yt[:HEAD_DIM].astype(BF16)
            qt_ref[2 * c + 1] = yt[HEAD_DIM:].astype(BF16)
        for c in range(nk // LANES):
            y = _norm_rope(qkv_ref[:, nq + c * LANES:nq + (c + 1) * LANES], rstd[nq // LANES + c], gk_ref[...],
                           cos_v, sin_v)
            k_ref[2 * c] = y[:, :HEAD_DIM].astype(BF16)
            k_ref[2 * c + 1] = y[:, HEAD_DIM:].astype(BF16)
            x = qkv_ref[:, nq + nk + c * LANES:nq + nk + (c + 1) * LANES]
            v_ref[2 * c] = x[:, :HEAD_DIM].astype(BF16)
            v_ref[2 * c + 1] = x[:, HEAD_DIM:].astype(BF16)

    return pl.pallas_call(
        body, name="a_prep", grid=(s // tr,),
        in_specs=[pl.BlockSpec((tr, nq + 2 * nk), lambda i: (i, 0)), pl.BlockSpec((tr, LANES), lambda i: (i, 0)),
                  pl.BlockSpec((tr, LANES), lambda i: (i, 0)), pl.BlockSpec((1, LANES), lambda i: (0, 0)),
                  pl.BlockSpec((1, LANES), lambda i: (0, 0))],
        out_specs=[pl.BlockSpec((A_HEADS, HEAD_DIM, tr), lambda i: (0, 0, i)),
                   pl.BlockSpec((A_KV, tr, HEAD_DIM), lambda i: (0, i, 0)),
                   pl.BlockSpec((A_KV, tr, HEAD_DIM), lambda i: (0, i, 0))],
        out_shape=[jax.ShapeDtypeStruct((A_HEADS, HEAD_DIM, s), BF16), jax.ShapeDtypeStruct((A_KV, s, HEAD_DIM), BF16),
                   jax.ShapeDtypeStruct((A_KV, s, HEAD_DIM), BF16)],
        compiler_params=_params(("parallel",)),
    )(qkv, cos, sin, gq2, gk2)


def _a_prep_bwd(dqt, dkt, dvt, qkv, cos, sin, gq2, gk2):
    s = qkv.shape[0]
    tr = 256
    nq, nk = A_HEADS * HEAD_DIM, A_KV * HEAD_DIM

    def body(dqt_ref, dkt_ref, dvt_ref, qkv_ref, cos_ref, sin_ref, gq_ref, gk_ref, o_ref, dgq_ref, dgk_ref):
        @pl.when(pl.program_id(0) == 0)
        def _():
            dgq_ref[...] = jnp.zeros_like(dgq_ref)
            dgk_ref[...] = jnp.zeros_like(dgk_ref)

        cos_v, sin_v = cos_ref[...], sin_ref[...]

        def pair(ref, c):
            return jnp.concatenate([ref[2 * c], ref[2 * c + 1]], axis=0).T

        rstd = [_head_rstd(qkv_ref[:, c * LANES:(c + 1) * LANES]) for c in range((nq + nk) // LANES)]
        dgq = jnp.zeros((1, LANES), F32)
        for c in range(nq // LANES):
            dx, dg = _norm_rope_bwd(pair(dqt_ref, c) * QK_SCALE, qkv_ref[:, c * LANES:(c + 1) * LANES], rstd[c],
                                    gq_ref[...], cos_v, sin_v)
            o_ref[:, c * LANES:(c + 1) * LANES] = dx.astype(BF16)
            dgq = dgq + dg
        dgq_ref[...] += dgq
        dgk = jnp.zeros((1, LANES), F32)
        for c in range(nk // LANES):
            lo = nq + c * LANES
            dx, dg = _norm_rope_bwd(pair(dkt_ref, c) * LN2, qkv_ref[:, lo:lo + LANES], rstd[nq // LANES + c],
                                    gk_ref[...], cos_v, sin_v)
            o_ref[:, lo:lo + LANES] = dx.astype(BF16)
            dgk = dgk + dg
            o_ref[:, lo + nk:lo + nk + LANES] = pair(dvt_ref, c).astype(BF16)
        dgk_ref[...] += dgk

    return pl.pallas_call(
        body, name="a_prep_bwd", grid=(s // tr,),
        in_specs=[pl.BlockSpec((A_HEADS, HEAD_DIM, tr), lambda i: (0, 0, i)),
                  pl.BlockSpec((A_KV, HEAD_DIM, tr), lambda i: (0, 0, i)),
                  pl.BlockSpec((A_KV, HEAD_DIM, tr), lambda i: (0, 0, i)),
                  pl.BlockSpec((tr, nq + 2 * nk), lambda i: (i, 0)), pl.BlockSpec((tr, LANES), lambda i: (i, 0)),
                  pl.BlockSpec((tr, LANES), lambda i: (i, 0)), pl.BlockSpec((1, LANES), lambda i: (0, 0)),
                  pl.BlockSpec((1, LANES), lambda i: (0, 0))],
        out_specs=[pl.BlockSpec((tr, nq + 2 * nk), lambda i: (i, 0)), pl.BlockSpec((1, LANES), lambda i: (0, 0)),
                   pl.BlockSpec((1, LANES), lambda i: (0, 0))],
        out_shape=[jax.ShapeDtypeStruct((s, nq + 2 * nk), BF16), jax.ShapeDtypeStruct((1, LANES), F32),
                   jax.ShapeDtypeStruct((1, LANES), F32)],
        compiler_params=_params(("arbitrary",)),
    )(dqt, dkt, dvt, qkv, cos, sin, gq2, gk2)


A_TQ = 1024
A_TQ_SUB = 256
A_TQ_BWD = 1024
A_KEY_CHUNK = 512


def _a_attn_fwd(qt, k, v, cargo, *, name):
    nh, _, s = qt.shape
    rep = nh // k.shape[0]
    tq = min(A_TQ, s)
    sub = min(A_TQ_SUB, tq)
    grid = (nh, s // tq)

    def body(qt_ref, k_ref, v_ref, o_ref, lse_ref):
        scores = [jnp.dot(k_ref[0], qt_ref[0, :, a:a + sub], preferred_element_type=F32)
                  for a in range(0, tq, sub)]
        for a, st in zip(range(0, tq, sub), scores):
            mx = jnp.max(st, axis=0, keepdims=True)
            p = jnp.exp2(st - mx)
            den = jnp.sum(p, axis=0, keepdims=True)
            ot = lax.dot_general(v_ref[0], p.astype(BF16), (((0,), (0,)), ((), ())), preferred_element_type=F32)
            o_ref[0, :, a:a + sub] = (ot / den).astype(BF16)
            lse_ref[0, :, a:a + sub] = mx + jnp.log(den) * LOG2E

    carried = _carry(cargo, grid, 3, 2, body)
    res = pl.pallas_call(
        carried.body, name=name, grid=grid,
        in_specs=[pl.BlockSpec((1, HEAD_DIM, tq), lambda h, i: (h, 0, i)),
                  pl.BlockSpec((1, s, HEAD_DIM), lambda h, i: (h // rep, 0, 0)),
                  pl.BlockSpec((1, s, HEAD_DIM), lambda h, i: (h // rep, 0, 0))] + carried.in_specs,
        out_specs=[pl.BlockSpec((1, HEAD_DIM, tq), lambda h, i: (h, 0, i)),
                   pl.BlockSpec((1, 1, tq), lambda h, i: (h, 0, i))] + carried.out_specs,
        out_shape=[jax.ShapeDtypeStruct((nh, HEAD_DIM, s), BF16), jax.ShapeDtypeStruct((nh, 1, s), F32)]
        + carried.out_shape,
        scratch_shapes=carried.scratch,
        compiler_params=_params(("arbitrary", "arbitrary")),
    )(qt, k, v, *carried.args)
    return res[0], res[1], res[2:]


def _a_attn_bwd(qt, k, v, dot, ot, lse, cargo, *, name):
    nh, _, s = qt.shape
    nkv = k.shape[0]
    rep = nh // nkv
    tq, ck = min(A_TQ_BWD, s), min(A_KEY_CHUNK, s)
    grid = (nh, s // tq)

    def body(qt_ref, k_ref, v_ref, dot_ref, ot_ref, lse_ref, dq_ref, dk_ref, dv_ref):
        h, i = pl.program_id(0), pl.program_id(1)

        @pl.when((h % rep == 0) & (i == 0))
        def _():
            dk_ref[...] = jnp.zeros_like(dk_ref)
            dv_ref[...] = jnp.zeros_like(dv_ref)

        q_t, do_t, lse_v = qt_ref[0], dot_ref[0], lse_ref[0]
        delta = jnp.sum(do_t.astype(F32) * ot_ref[0].astype(F32), axis=0, keepdims=True)
        nt = (((1,), (1,)), ((), ()))
        dq = jnp.zeros((HEAD_DIM, tq), F32)
        for c in range(s // ck):
            keys = slice(c * ck, (c + 1) * ck)
            kc = k_ref[0, keys, :]
            p = jnp.exp2(jnp.dot(kc, q_t, preferred_element_type=F32) - lse_v)
            dp = jnp.dot(v_ref[0, keys, :], do_t, preferred_element_type=F32)
            ds = (p * (dp - delta)).astype(BF16)
            dv_ref[0, :, keys] += lax.dot_general(do_t, p.astype(BF16), nt, preferred_element_type=F32)
            dk_ref[0, :, keys] += lax.dot_general(q_t, ds, nt, preferred_element_type=F32)
            dq = dq + lax.dot_general(kc, ds, (((0,), (0,)), ((), ())), preferred_element_type=F32)
        dq_ref[0] = dq

    blk_q = pl.BlockSpec((1, HEAD_DIM, tq), lambda h, i: (h, 0, i))
    blk_row = pl.BlockSpec((1, 1, tq), lambda h, i: (h, 0, i))
    blk_kv = pl.BlockSpec((1, s, HEAD_DIM), lambda h, i: (h // rep, 0, 0))
    blk_acc = pl.BlockSpec((1, HEAD_DIM, s), lambda h, i: (h // rep, 0, 0))
    carried = _carry(cargo, grid, 6, 3, body)
    res = pl.pallas_call(
        carried.body, name=name, grid=grid,
        in_specs=[blk_q, blk_kv, blk_kv, blk_q, blk_q, blk_row] + carried.in_specs,
        out_specs=[blk_q, blk_acc, blk_acc] + carried.out_specs,
        out_shape=[jax.ShapeDtypeStruct((nh, HEAD_DIM, s), F32), jax.ShapeDtypeStruct((nkv, HEAD_DIM, s), F32),
                   jax.ShapeDtypeStruct((nkv, HEAD_DIM, s), F32)] + carried.out_shape,
        scratch_shapes=carried.scratch,
        compiler_params=_params(("arbitrary", "arbitrary")),
    )(qt, k, v, dot, ot, lse, *carried.args)
    return res[0], res[1], res[2], res[3:]


WIN_FAR = 1e30


class _Band(NamedTuple):
    window: int
    dil: int
    seg: int
    stride: int

    @property
    def reach(self):
        return -(-self.window // WIN_REACH) * WIN_REACH


def _win_start(i, tq, tk, s, reach):
    return pl.multiple_of(jnp.clip(i * tq - reach, 0, s - tk), LANES)


def _win_penalty(i, start, tq, tk, band):
    qpos = i * tq + lax.broadcasted_iota(jnp.int32, (tk, tq), 1)
    kpos = start + lax.broadcasted_iota(jnp.int32, (tk, tq), 0)
    dist = jnp.abs(kpos - qpos)
    seg_lo = qpos - (qpos & (band.seg - 1))
    valid = (dist <= band.window) & (kpos >= seg_lo) & (kpos < seg_lo + band.seg)
    if band.stride > 1:
        valid &= (dist & (band.stride - 1)) == 0
    return jnp.where(valid, (dist * band.dil).astype(F32), WIN_FAR)


def _win_scores(kw_t, q_t, slope, pen):
    st = lax.dot_general(kw_t, q_t, (((0,), (0,)), ((), ())), preferred_element_type=F32)
    return st * (QK_SCALE * LOG2E) - (slope * LOG2E) * pen


def _win_tq(s):
    return min(512, s)


def _win_fwd(qt, ktp, vtp, slopes, sinks, *, band, out_dtype, name):
    nh, _, s = qt.shape
    nkv = ktp.shape[0]
    rep = nh // nkv
    tq = _win_tq(s)
    tk = min(tq + 2 * band.reach, s)

    def body(*refs):
        qt_ref, kt_ref, vt_ref, sl_ref = refs[:4]
        o_ref, lse_ref, pen_ref = refs[-3:]
        i, kv = pl.program_id(0), pl.program_id(1)
        start = _win_start(i, tq, tk, s, band.reach)

        @pl.when(kv == 0)
        def _():
            pen_ref[...] = _win_penalty(i, start, tq, tk, band)

        win = pl.ds(start, tk)
        kw_t, vw_t, pen = kt_ref[0, :, win], vt_ref[0, :, win], pen_ref[...]
        for g in range(rep):
            st = _win_scores(kw_t, qt_ref[g], sl_ref[g][:, :1], pen)
            mx = jnp.max(st, axis=0, keepdims=True)
            if sinks is not None:
                sink = refs[4][g][:, :1] * LOG2E
                mx = jnp.maximum(mx, sink)
            p = jnp.exp2(st - mx)
            den = jnp.sum(p, axis=0, keepdims=True)
            if sinks is not None:
                den = den + jnp.exp2(sink - mx)
            ot = jnp.dot(vw_t, p.astype(BF16), preferred_element_type=F32)
            o_ref[g] = (ot / den).astype(o_ref.dtype)
            lse_ref[g] = mx * LN2 + jnp.log(den)

    blk_q = pl.BlockSpec((rep, HEAD_DIM, tq), lambda i, kv: (kv, 0, i))
    blk_kv = pl.BlockSpec((1, HEAD_DIM, s), lambda i, kv: (kv, 0, 0))
    blk_h = pl.BlockSpec((rep, 1, LANES), lambda i, kv: (kv, 0, 0))
    in_specs, args = [blk_q, blk_kv, blk_kv, blk_h], [qt, ktp, vtp, slopes]
    if sinks is not None:
        in_specs.append(blk_h)
        args.append(sinks)
    return pl.pallas_call(
        body, name=name, grid=(s // tq, nkv), in_specs=in_specs,
        out_specs=[blk_q, pl.BlockSpec((rep, 1, tq), lambda i, kv: (kv, 0, i))],
        out_shape=[jax.ShapeDtypeStruct((nh, HEAD_DIM, s), out_dtype), jax.ShapeDtypeStruct((nh, 1, s), F32)],
        scratch_shapes=[pltpu.VMEM((tk, tq), F32)],
        compiler_params=_params(("arbitrary", "arbitrary")),
    )(*args)


def _win_bwd(qt, ktp, vtp, slopes, sinks, dot, ot, delta, *, band, name):
    nh, _, s = qt.shape
    nkv = ktp.shape[0]
    rep = nh // nkv
    tq = _win_tq(s)
    tk = min(tq + 2 * band.reach, s)
    n_in = 6 + (sinks is not None)

    def body(*refs):
        qt_ref, kt_ref, vt_ref, sl_ref, dot_ref, aux_ref = refs[:6]
        outs, pen_ref = refs[n_in:-1], refs[-1]
        dq_ref, dk_ref, dv_ref = outs[:3]
        i, kv = pl.program_id(0), pl.program_id(1)

        @pl.when((i == 0) & (kv == 0))
        def _():
            dk_ref[...] = jnp.zeros_like(dk_ref)
            dv_ref[...] = jnp.zeros_like(dv_ref)
            if sinks is not None:
                outs[3][...] = jnp.zeros_like(outs[3])

        start = _win_start(i, tq, tk, s, band.reach)

        @pl.when(kv == 0)
        def _():
            pen_ref[...] = _win_penalty(i, start, tq, tk, band)

        win = pl.ds(start, tk)
        kw_t, vw_t, pen = kt_ref[0, :, win], vt_ref[0, :, win], pen_ref[...]
        nt = (((1,), (1,)), ((), ()))
        dk_acc = jnp.zeros((HEAD_DIM, tk), F32)
        dv_acc = jnp.zeros((HEAD_DIM, tk), F32)
        for g in range(rep):
            q_t, do_t = qt_ref[g], dot_ref[g]
            st = _win_scores(kw_t, q_t, sl_ref[g][:, :1], pen)
            mx = jnp.max(st, axis=0, keepdims=True)
            if sinks is not None:
                sink = refs[6][g][:, :1] * LOG2E
                mx = jnp.maximum(mx, sink)
            p = jnp.exp2(st - mx)
            den = jnp.sum(p, axis=0, keepdims=True)
            if sinks is not None:
                p_sink = jnp.exp2(sink - mx)
                den = den + p_sink
            p = p / den
            dp = lax.dot_general(vw_t, do_t, (((0,), (0,)), ((), ())), preferred_element_type=F32)
            if delta is None:
                row = jnp.sum(do_t.astype(F32) * aux_ref[g].astype(F32), axis=0, keepdims=True)
            else:
                row = aux_ref[g]
            ds = (p * (dp - row) * QK_SCALE).astype(BF16)
            dv_acc = dv_acc + lax.dot_general(do_t, p.astype(BF16), nt, preferred_element_type=F32)
            dk_acc = dk_acc + lax.dot_general(q_t, ds, nt, preferred_element_type=F32)
            dq_ref[g] = jnp.dot(kw_t, ds, preferred_element_type=F32)
            if sinks is not None:
                outs[3][kv * rep + g] += (jnp.zeros((1, LANES), F32)
                                          - jnp.sum(p_sink / den * row, axis=1, keepdims=True))
        dv_ref[kv, :, win] += dv_acc
        dk_ref[kv, :, win] += dk_acc

    blk_q = pl.BlockSpec((rep, HEAD_DIM, tq), lambda i, kv: (kv, 0, i))
    blk_row = pl.BlockSpec((rep, 1, tq), lambda i, kv: (kv, 0, i))
    blk_kv = pl.BlockSpec((1, HEAD_DIM, s), lambda i, kv: (kv, 0, 0))
    blk_acc = pl.BlockSpec((nkv, HEAD_DIM, s), lambda i, kv: (0, 0, 0))
    blk_h = pl.BlockSpec((rep, 1, LANES), lambda i, kv: (kv, 0, 0))
    in_specs = [blk_q, blk_kv, blk_kv, blk_h, blk_q, blk_q if delta is None else blk_row]
    args = [qt, ktp, vtp, slopes, dot, ot if delta is None else delta]
    out_specs = [blk_q, blk_acc, blk_acc]
    out_shape = [jax.ShapeDtypeStruct((nh, HEAD_DIM, s), F32), jax.ShapeDtypeStruct((nkv, HEAD_DIM, s), F32),
                 jax.ShapeDtypeStruct((nkv, HEAD_DIM, s), F32)]
    if sinks is not None:
        in_specs.append(blk_h)
        args.append(sinks)
        out_specs.append(pl.BlockSpec((nh, 1, LANES), lambda i, h: (0, 0, 0)))
        out_shape.append(jax.ShapeDtypeStruct((nh, 1, LANES), F32))
    res = pl.pallas_call(
        body, name=name, grid=(s // tq, nkv), in_specs=in_specs, out_specs=out_specs, out_shape=out_shape,
        scratch_shapes=[pltpu.VMEM((tk, tq), F32)],
        compiler_params=_params(("arbitrary", "arbitrary")),
    )(*args)
    return res if sinks is not None else (*res, None)


def _group_weights(lse):
    e = jnp.exp(lse - jnp.max(lse, axis=0, keepdims=True))
    return e / jnp.sum(e, axis=0, keepdims=True)


def _b_combine_fwd(ot, lse):
    nh, _, s = ot.shape
    ng, hg, _ = lse.shape
    ts = min(512, s)

    def body(ot_ref, lse_ref, o_ref):
        alpha = _group_weights(lse_ref[...])
        for g in range(ng):
            for j in range(hg):
                o_ref[g * hg + j] = (ot_ref[g * hg + j] * alpha[g, j:j + 1, :]).astype(BF16)

    return pl.pallas_call(
        body, name="b_combine_fwd", grid=(s // ts,),
        in_specs=[pl.BlockSpec((nh, HEAD_DIM, ts), lambda i: (0, 0, i)), pl.BlockSpec((ng, hg, ts), lambda i: (0, 0, i))],
        out_specs=pl.BlockSpec((nh, HEAD_DIM, ts), lambda i: (0, 0, i)),
        out_shape=jax.ShapeDtypeStruct((nh, HEAD_DIM, s), BF16),
        compiler_params=_params(("parallel",)),
    )(ot, lse)


def _b_combine_bwd(dout, ot, lse):
    nh, _, s = ot.shape
    ng, hg, _ = lse.shape
    ts = min(512, s)

    def body(dout_ref, ot_ref, lse_ref, do_ref, delta_ref):
        alpha = _group_weights(lse_ref[...])
        for j in range(hg):
            e = [jnp.sum(dout_ref[g * hg + j].astype(F32) * ot_ref[g * hg + j], axis=0, keepdims=True)
                 for g in range(ng)]
            a = [alpha[g, j:j + 1, :] for g in range(ng)]
            mix = a[0] * e[0]
            for g in range(1, ng):
                mix = mix + a[g] * e[g]
            for g in range(ng):
                do_ref[g * hg + j] = (dout_ref[g * hg + j].astype(F32) * a[g]).astype(BF16)
                delta_ref[g * hg + j] = a[g] * mix

    blk = pl.BlockSpec((nh, HEAD_DIM, ts), lambda i: (0, 0, i))
    return pl.pallas_call(
        body, name="b_combine_bwd", grid=(s // ts,),
        in_specs=[blk, blk, pl.BlockSpec((ng, hg, ts), lambda i: (0, 0, i))],
        out_specs=[blk, pl.BlockSpec((nh, 1, ts), lambda i: (0, 0, i))],
        out_shape=[jax.ShapeDtypeStruct((nh, HEAD_DIM, s), BF16), jax.ShapeDtypeStruct((nh, 1, s), F32)],
        compiler_params=_params(("parallel",)),
    )(dout, ot, lse)


def _alibi_slopes(n):
    return 2.0 ** (-8.0 * jnp.arange(1, n + 1, dtype=F32) / n)


def _per_head(v):
    return jnp.broadcast_to(v.astype(F32)[:, None, None], (v.shape[0], 1, LANES))


def _dilate(x, dil):
    if dil == 1:
        return x
    s = x.shape[-1]
    return jnp.swapaxes(x.reshape(x.shape[:-1] + (s // dil, dil)), -1, -2).reshape(x.shape)


def _undilate(x, dil):
    if dil == 1:
        return x
    s = x.shape[-1]
    return jnp.swapaxes(x.reshape(x.shape[:-1] + (dil, s // dil)), -1, -2).reshape(x.shape)


def _heads(x_t):
    return x_t.reshape(-1, HEAD_DIM, x_t.shape[-1])


B_MAX_STRIDE = 4


def _b_band(window, dilation, s):
    if dilation <= B_MAX_STRIDE:
        return _Band(window // 2, 1, s, dilation), 1
    return _Band(window // 2 // dilation, dilation, s // dilation, 1), dilation


def _mixer_fwd(kind, qkv, p, tabs, cargo, layer):
    s = qkv.shape[0 if kind == 0 else 1]
    if kind == 0:
        qt, k, v = _a_prep(qkv, tabs[0], tabs[1], p["gq2"], p["gk2"])
        ot, lse, brought = _a_attn_fwd(qt, k, v, cargo, name=f"a_attn_fwd_l{layer}")
        return ot.reshape(-1, s), dict(qt=qt, k=k, v=v, ot=ot, lse=lse), brought
    assert cargo is None
    if kind == 2:
        nq, nk = C_HEADS * HEAD_DIM, C_KV * HEAD_DIM
        qt = _heads(qkv[:nq])
        kp, vp = _heads(qkv[nq:nq + nk]), _heads(qkv[nq + nk:])
        ot, _ = _win_fwd(qt, kp, vp, p["slopes"], p["sinks"], band=_Band(C_WINDOW, 1, s, 1), out_dtype=BF16,
                         name="c_attn_fwd")
        return ot.reshape(-1, s), dict(qt=qt, kp=kp, vp=vp, ot=ot), ()
    ng, hg, kg = len(B_GROUPS), B_HEADS_PER_GROUP, B_KV_PER_GROUP
    nq, nk = ng * hg * HEAD_DIM, ng * kg * HEAD_DIM
    qt_all, kt_all, vt_all = _heads(qkv[:nq]), _heads(qkv[nq:nq + nk]), _heads(qkv[nq + nk:])
    saved, outs, lses = [], [], []
    for g, (window, dilation) in enumerate(B_GROUPS):
        band, dil = _b_band(window, dilation, s)
        qt = _dilate(qt_all[g * hg:(g + 1) * hg], dil)
        kp = _dilate(kt_all[g * kg:(g + 1) * kg], dil)
        vp = _dilate(vt_all[g * kg:(g + 1) * kg], dil)
        sl = p["slopes"][g * hg:(g + 1) * hg]
        ot, lse = _win_fwd(qt, kp, vp, sl, None, band=band, out_dtype=F32, name=f"b_attn_fwd_g{g}")
        saved.append(dict(qt=qt, kp=kp, vp=vp))
        outs.append(_undilate(ot, dil))
        lses.append(_undilate(lse[:, 0, :], dil))
    ot_all, lse_all = jnp.concatenate(outs, axis=0), jnp.stack(lses, axis=0)
    mixed = _b_combine_fwd(ot_all, lse_all)
    return mixed.reshape(-1, s), dict(groups=saved, ot=ot_all, lse=lse_all), ()


def _mixer_bwd(kind, do_t, qkv, sv, p, tabs, cargo, layer):
    s = do_t.shape[1]
    do_heads = _heads(do_t)
    small = {}
    if kind == 0:
        dqt, dkt, dvt, brought = _a_attn_bwd(sv["qt"], sv["k"], sv["v"], do_heads, sv["ot"], sv["lse"],
                                             cargo, name=f"a_attn_bwd_l{layer}")
        dqkv, dgq, dgk = _a_prep_bwd(dqt, dkt, dvt, qkv, tabs[0], tabs[1], p["gq2"], p["gk2"])
        small["q_gain"] = dgq[0, :HEAD_DIM] + dgq[0, HEAD_DIM:]
        small["k_gain"] = dgk[0, :HEAD_DIM] + dgk[0, HEAD_DIM:]
        return dqkv, small, brought
    assert cargo is None
    if kind == 2:
        dqt, dkt, dvt, dsink = _win_bwd(sv["qt"], sv["kp"], sv["vp"], p["slopes"], p["sinks"], do_heads,
                                        sv["ot"], None, band=_Band(C_WINDOW, 1, s, 1), name="c_attn_bwd")
        small["sinks"] = dsink[:, 0, 0]
        parts = [dqt.reshape(-1, s), dkt.reshape(-1, s), dvt.reshape(-1, s)]
        return jnp.concatenate(parts, axis=0).astype(BF16), small, ()
    ng, hg, kg = len(B_GROUPS), B_HEADS_PER_GROUP, B_KV_PER_GROUP
    do_own, delta = _b_combine_bwd(do_heads, sv["ot"], sv["lse"])
    dqs, dks, dvs = [], [], []
    for g, (window, dilation) in enumerate(B_GROUPS):
        band, dil = _b_band(window, dilation, s)
        gs = sv["groups"][g]
        dqt, dkt, dvt, _ = _win_bwd(gs["qt"], gs["kp"], gs["vp"], p["slopes"][g * hg:(g + 1) * hg], None,
                                    _dilate(do_own[g * hg:(g + 1) * hg], dil), None,
                                    _dilate(delta[g * hg:(g + 1) * hg], dil), band=band, name=f"b_attn_bwd_g{g}")
        dqs.append(_undilate(dqt, dil))
        dks.append(_undilate(dkt, dil))
        dvs.append(_undilate(dvt, dil))
    parts = [x.reshape(-1, s) for x in dqs + dks + dvs]
    return jnp.concatenate(parts, axis=0).astype(BF16), small, ()


LAYER_MATS = ("w_qkv", "w_o", "w1", "w2")
COLUMN_QUARTERS = ("w_qkv", "w1")


def _whole(key, gathered):
    q, r, c = gathered.shape
    if key == "w1":
        return gathered
    if key in COLUMN_QUARTERS:
        return jnp.transpose(gathered, (1, 0, 2)).reshape(r, q * c)
    return gathered.reshape(q * r, c)


def _quarters(key, g):
    r, c = g.shape
    if key in COLUMN_QUARTERS:
        return jnp.transpose(g.reshape(r, 4, c // 4), (1, 0, 2))
    return g.reshape(4, r // 4, c)


def _local_step(x, target, norms, mixer_params, shards, whole=None):
    s = x.shape[0]
    tabs = _rope_tables(s)
    if whole is None:
        assert MIXER_OF_LAYER[0][0] == 0
        first = _run_cargo(_gather_cargo([shards[0][key] for key in LAYER_MATS]), name="gather_l0")
        mats = {0: {key: _whole(key, g) for key, g in zip(LAYER_MATS, first)}}
        later = _gather_cargo([shards[layer][key] for layer in range(1, DEPTH) for key in LAYER_MATS])
    else:
        mats, later = dict(enumerate(whole)), None
    h = x
    saved = []
    for layer in range(DEPTH):
        kind = layer % N_MIXERS
        w, p = mats[layer], mixer_params[layer]
        hn, qkv = _norm_mm(h, norms["attn"][layer][None], w["w_qkv"], out_dtype=F32 if kind == 0 else BF16,
                           relu2=False, transpose_out=kind != 0, name=f"qkv_proj_l{layer}")
        o_t, sv, brought = _mixer_fwd(kind, qkv, p, tabs, later if layer == 0 else None, layer)
        for n, g in enumerate(brought):
            mats.setdefault(1 + n // len(LAYER_MATS), {})[LAYER_MATS[n % len(LAYER_MATS)]] = _whole(
                LAYER_MATS[n % len(LAYER_MATS)], g)
        h_mid = _mm_res(o_t, w["w_o"], h, a_transposed=True, name=f"o_proj_l{layer}")
        hn2, act = _norm_mm(h_mid, norms["mlp"][layer][None], w["w1"], out_dtype=BF16, relu2=True,
                            transpose_out=False, name=f"mlp_up_l{layer}")
        h_out = _mm_res(act, w["w2"], h_mid, a_transposed=False, name=f"mlp_down_l{layer}")
        saved.append(dict(h=h, hn=hn, qkv=qkv, o_t=o_t, mix=sv, h_mid=h_mid, hn2=hn2, act=act))
        h = h_out

    dh, loss, d_final = _loss_head(h, norms["final"][None], target)

    own, received, pending = {}, {}, []
    d_attn, d_mlp, small = [None] * DEPTH, [None] * DEPTH, [None] * DEPTH
    for layer in reversed(range(DEPTH)):
        kind = layer % N_MIXERS
        w, p, sv = mats[layer], mixer_params[layer], saved[layer]
        du = _mm_nt(dh, w["w2"], sv["act"], transpose_out=False, name=f"mlp_down_bwd_l{layer}")
        own[layer, "w2"] = _quarters("w2", _mm_tn(sv["act"], dh, x_transposed=False, g_transposed=False,
                                                  column_quarters=False, name=f"mlp_w2_grad_l{layer}"))
        own[layer, "w1"] = _mm_tn(sv["hn2"], du, x_transposed=False, g_transposed=False, column_quarters=True,
                                  name=f"mlp_w1_grad_l{layer}")
        dh_mid, d_mlp[layer] = _mm_nt_normbwd(du, w["w1"], sv["h_mid"], norms["mlp"][layer][None], dh,
                                              g_transposed=False, name=f"mlp_up_bwd_l{layer}")
        do_t = _mm_nt(dh_mid, w["w_o"], None, transpose_out=True, name=f"o_proj_bwd_l{layer}")
        own[layer, "w_o"] = _quarters("w_o", _mm_tn(sv["o_t"], dh_mid, x_transposed=True, g_transposed=False,
                                                    column_quarters=False, name=f"w_o_grad_l{layer}"))
        pending += [(layer, "w2"), (layer, "w1"), (layer, "w_o")]
        cargo = None
        if kind == 0 and whole is None:
            cargo, sent, pending = _scatter_cargo([own[item] for item in pending], None), pending, []
        dqkv, small[layer], brought = _mixer_bwd(kind, do_t, sv["qkv"], sv["mix"], p, tabs, cargo, layer)
        if cargo is not None:
            received.update(zip(sent, brought))
        own[layer, "w_qkv"] = _quarters("w_qkv", _mm_tn(sv["hn"], dqkv, x_transposed=False, g_transposed=kind != 0,
                                                        column_quarters=False, name=f"w_qkv_grad_l{layer}"))
        pending.append((layer, "w_qkv"))
        dh, d_attn[layer] = _mm_nt_normbwd(dqkv, w["w_qkv"], sv["h"], norms["attn"][layer][None], dh_mid,
                                           g_transposed=kind != 0, name=f"qkv_proj_bwd_l{layer}")
    return loss, dh, own, received, pending, dict(attn=d_attn, mlp=d_mlp, final=d_final, mixer=small)


CHIP_FLIPS = ((1, 0), (0, 1), (1, 1))


class _Cargo(NamedTuple):
    ins: tuple
    out_shape: tuple
    sem_shapes: tuple
    start: Callable
    wait: Callable


class _Carried(NamedTuple):
    body: Callable
    in_specs: list
    out_specs: list
    out_shape: list
    scratch: list
    args: tuple


def _carry(cargo, grid, n_in, n_out, body):
    if cargo is None:
        return _Carried(body, [], [], [], [], ())
    ci, co = len(cargo.ins), len(cargo.out_shape)

    def wrapped(*refs):
        ins, c_ins = refs[:n_in], refs[n_in:n_in + ci]
        outs, c_outs = refs[n_in + ci:n_in + ci + n_out], refs[n_in + ci + n_out:n_in + ci + n_out + co]
        sems = refs[n_in + ci + n_out + co:]
        first = last = None
        for axis, extent in enumerate(grid):
            at = pl.program_id(axis)
            first = (at == 0) if first is None else first & (at == 0)
            last = (at == extent - 1) if last is None else last & (at == extent - 1)

        @pl.when(first)
        def _():
            cargo.start(c_ins, c_outs, sems)

        body(*ins, *outs)

        @pl.when(last)
        def _():
            cargo.wait(c_ins, c_outs, sems)

    return _Carried(wrapped, [ANY] * ci, [ANY] * co, list(cargo.out_shape), list(cargo.sem_shapes), tuple(cargo.ins))


def _run_cargo(cargo, *, name):
    ci, co = len(cargo.ins), len(cargo.out_shape)

    def body(*refs):
        cargo.start(refs[:ci], refs[ci:ci + co], refs[ci + co:])
        cargo.wait(refs[:ci], refs[ci:ci + co], refs[ci + co:])

    return pl.pallas_call(body, name=name, in_specs=[ANY] * ci, out_specs=[ANY] * co, out_shape=list(cargo.out_shape),
                          scratch_shapes=list(cargo.sem_shapes))(*cargo.ins)


def _other_chip(x, y, j):
    fx, fy = CHIP_FLIPS[j]
    return (1 - x if fx else x), (1 - y if fy else y)


def _gather_cargo(shards):
    n = len(shards)
    halves = [a.shape[0] // 2 for a in shards]

    def copies(ins, outs, sems):
        ici_send, ici_recv, d2d_send, d2d_recv, local_sems = sems
        x, y, c = lax.axis_index("x"), lax.axis_index("y"), lax.axis_index("c")
        me = 2 * x + y

        def half(t, which):
            return pl.ds(pl.multiple_of(which * halves[t], 16), halves[t])

        def over_ici(t, j, arriving):
            px, py = _other_chip(x, y, j)
            return pltpu.make_async_remote_copy(
                src_ref=ins[t].at[half(t, c)], dst_ref=outs[t].at[2 * px + py if arriving else me, half(t, c)],
                send_sem=ici_send.at[t, j], recv_sem=ici_recv.at[t, j], device_id=(px, py, c), device_id_type=MESH)

        def over_d2d(t, j, arriving):
            px, py = _other_chip(x, y, j)
            mine = outs[t].at[2 * px + py, half(t, c)]
            return pltpu.make_async_remote_copy(
                src_ref=mine, dst_ref=outs[t].at[2 * px + py, half(t, 1 - c)] if arriving else mine,
                send_sem=d2d_send.at[t, j], recv_sem=d2d_recv.at[t, j], device_id=(x, y, 1 - c), device_id_type=MESH)

        return over_ici, over_d2d, lambda t: pltpu.make_async_copy(ins[t], outs[t].at[me], local_sems.at[t])

    def start(ins, outs, sems):
        over_ici, _, own = copies(ins, outs, sems)
        for t in range(n):
            own(t).start()
            for j in range(len(CHIP_FLIPS)):
                over_ici(t, j, False).start()

    def wait(ins, outs, sems):
        over_ici, over_d2d, own = copies(ins, outs, sems)
        for t in range(n):
            for j in range(len(CHIP_FLIPS)):
                over_ici(t, j, True).wait_recv()
                over_d2d(t, j, False).start()
        for t in range(n):
            for j in range(len(CHIP_FLIPS)):
                over_d2d(t, j, True).wait_recv()
                over_d2d(t, j, False).wait_send()
                over_ici(t, j, False).wait_send()
            own(t).wait()

    dma = pltpu.SemaphoreType.DMA
    return _Cargo(tuple(shards), tuple(jax.ShapeDtypeStruct((4,) + a.shape, a.dtype) for a in shards),
                  (dma((n, 3)), dma((n, 3)), dma((n, 3)), dma((n, 3)), dma((n,))), start, wait)


def _scatter_cargo(grads, small):
    n = len(grads)

    def copies(ins, outs, sems):
        x, y, c = lax.axis_index("x"), lax.axis_index("y"), lax.axis_index("c")
        me = 4 * x + 2 * y + c

        def remote(t, j):
            px, py = _other_chip(x, y, j)
            return pltpu.make_async_remote_copy(
                src_ref=ins[t].at[2 * px + py], dst_ref=outs[t].at[j], send_sem=sems[0].at[t, j],
                recv_sem=sems[1].at[t, j], device_id=(px, py, c), device_id_type=MESH)

        def small_remote(r, arriving):
            fx, fy, fc = (r + 1) // 4, ((r + 1) // 2) % 2, (r + 1) % 2
            px, py, pc = (1 - x if fx else x), (1 - y if fy else y), (1 - c if fc else c)
            return pltpu.make_async_remote_copy(
                src_ref=ins[n], dst_ref=outs[n].at[4 * px + 2 * py + pc if arriving else me],
                send_sem=sems[2].at[r], recv_sem=sems[3].at[r], device_id=(px, py, pc), device_id_type=MESH)

        return remote, small_remote, lambda: pltpu.make_async_copy(ins[n], outs[n].at[me], sems[4])

    def start(ins, outs, sems):
        remote, small_remote, small_own = copies(ins, outs, sems)
        if small is not None:
            small_own().start()
            for r in range(7):
                small_remote(r, False).start()
        for t in range(n):
            for j in range(len(CHIP_FLIPS)):
                remote(t, j).start()

    def wait(ins, outs, sems):
        remote, small_remote, small_own = copies(ins, outs, sems)
        if small is not None:
            for r in range(7):
                small_remote(r, True).wait_recv()
                small_remote(r, False).wait_send()
            small_own().wait()
        for t in range(n):
            for j in range(len(CHIP_FLIPS)):
                remote(t, j).wait()

    dma = pltpu.SemaphoreType.DMA
    ins = tuple(grads) + (() if small is None else (small,))
    out_shape = tuple(jax.ShapeDtypeStruct((3,) + g.shape[1:], g.dtype) for g in grads)
    sem_shapes = (dma((n, 3)), dma((n, 3)))
    if small is not None:
        out_shape += (jax.ShapeDtypeStruct((8,) + small.shape, small.dtype),)
        sem_shapes += (dma((7,)), dma((7,)), dma(()))
    return _Cargo(ins, out_shape, sem_shapes, start, wait)


def _swap_cores(parts):
    n = len(parts)

    def body(*refs):
        ins, outs = refs[:n], refs[n:2 * n]
        send_sems, recv_sems = refs[2 * n:]
        peer = (lax.axis_index("x"), lax.axis_index("y"), 1 - lax.axis_index("c"))
        copies = [pltpu.make_async_remote_copy(src_ref=ins[t], dst_ref=outs[t], send_sem=send_sems.at[t],
                                               recv_sem=recv_sems.at[t], device_id=peer, device_id_type=MESH)
                  for t in range(n)]
        for cp in copies:
            cp.start()
        for cp in copies:
            cp.wait()

    return pl.pallas_call(
        body, name="swap_cores", in_specs=[ANY] * n, out_specs=[ANY] * n,
        out_shape=[jax.ShapeDtypeStruct(a.shape, a.dtype) for a in parts],
        scratch_shapes=[pltpu.SemaphoreType.DMA((n,)), pltpu.SemaphoreType.DMA((n,))],
    )(*parts)


def _rows_tile(r):
    return 256 if r % 256 == 0 else r


def _sum_quarters(own, recv, *, name):
    r, c = own.shape
    tr = _rows_tile(r)

    def body(own_ref, recv_ref, o_ref):
        acc = own_ref[...].astype(F32)
        for j in range(3):
            acc = acc + recv_ref[j].astype(F32)
        o_ref[...] = acc.astype(BF16)

    return pl.pallas_call(
        body, name=name, grid=(r // tr,),
        in_specs=[pl.BlockSpec((tr, c), lambda i: (i, 0)), pl.BlockSpec((3, tr, c), lambda i: (0, i, 0))],
        out_specs=pl.BlockSpec((tr, c), lambda i: (i, 0)),
        out_shape=jax.ShapeDtypeStruct((r, c), BF16),
        compiler_params=_params(("parallel",)),
    )(own, recv)


def _adamw(w, m, v, parts, *, name):
    r, c = w.shape
    tr = _rows_tile(r)
    c1, c2 = 1.0 - ADAM_B1 ** ADAM_STEP, 1.0 - ADAM_B2 ** ADAM_STEP
    n_parts = len(parts)

    def body(*refs):
        w_ref, m_ref, v_ref = refs[:3]
        g_ref, d_ref, nm_ref, nv_ref = refs[3 + n_parts:]
        terms = []
        for p_ref in refs[3:3 + n_parts]:
            terms += [p_ref[...]] if len(p_ref.shape) == 2 else [p_ref[j] for j in range(p_ref.shape[0])]
        g = terms[0].astype(F32)
        for term in terms[1:]:
            g = g + term.astype(F32)
        m_new = ADAM_B1 * m_ref[...] + (1.0 - ADAM_B1) * g
        v_new = ADAM_B2 * v_ref[...] + (1.0 - ADAM_B2) * (g * g)
        step = (m_new / c1) / (jnp.sqrt(v_new / c2) + ADAM_EPS)
        g_ref[...] = g
        d_ref[...] = -ADAM_LR * (step + ADAM_WD * w_ref[...])
        nm_ref[...] = m_new
        nv_ref[...] = v_new

    blk = pl.BlockSpec((tr, c), lambda i: (i, 0))
    part_specs = [blk if p.ndim == 2 else pl.BlockSpec((p.shape[0], tr, c), lambda i: (0, i, 0)) for p in parts]
    return pl.pallas_call(
        body, name=name, grid=(r // tr,), in_specs=[blk, blk, blk] + part_specs,
        out_specs=[blk] * 4, out_shape=[jax.ShapeDtypeStruct((r, c), F32)] * 4,
        compiler_params=_params(("parallel",)),
    )(w, m, v, *parts)


MATS = ("a_w_qkv", "a_w_o", "b_w_qkv", "b_w_o", "c_w_qkv", "c_w_o", "mlp_w1", "mlp_w2")
SMALLS = ("attn_norm", "mlp_norm", "a_q_gain", "a_k_gain", "c_sinks", "final_norm")
WEIGHTS = ("attn_norm", "mlp_norm", "a_w_qkv", "a_q_gain", "a_k_gain", "a_w_o", "b_w_qkv", "b_w_o", "c_w_qkv",
           "c_sinks", "c_w_o", "mlp_w1", "mlp_w2", "final_norm")
MIXER_OF_LAYER = tuple((layer % N_MIXERS, sum(1 for q in range(layer) if q % N_MIXERS == layer % N_MIXERS))
                       for layer in range(DEPTH))
SMALL_ROWS = 8


def _pack_small(values):
    rows, spans, at = [], [], 0
    for v in values:
        flat = v.reshape(-1)
        n = -(-flat.shape[0] // (SMALL_ROWS * LANES)) * SMALL_ROWS
        rows.append(jnp.pad(flat, (0, n * LANES - flat.shape[0])).reshape(n, LANES))
        spans.append((at, n))
        at += n
    return jnp.concatenate(rows, axis=0), spans


def kernel(x, attn_norm, mlp_norm, a_w_qkv, a_q_gain, a_k_gain, a_w_o, b_w_qkv, b_w_o, c_w_qkv, c_sinks, c_w_o, mlp_w1, mlp_w2, final_norm, loss_target, m_attn_norm, m_mlp_norm, m_a_w_qkv, m_a_q_gain, m_a_k_gain, m_a_w_o, m_b_w_qkv, m_b_w_o, m_c_w_qkv, m_c_sinks, m_c_w_o, m_mlp_w1, m_mlp_w2, m_final_norm, v_attn_norm, v_mlp_norm, v_a_w_qkv, v_a_q_gain, v_a_k_gain, v_a_w_o, v_b_w_qkv, v_b_w_o, v_c_w_qkv, v_c_sinks, v_c_w_o, v_mlp_w1, v_mlp_w2, v_final_norm):
    env = dict(locals())
    w = {name: env[name] for name in WEIGHTS}
    mom = {name: (env["m_" + name], env["v_" + name]) for name in WEIGHTS}

    prefix = ("a", "b", "c")
    shards, mixer_params = [], []
    for layer, (kind, j) in enumerate(MIXER_OF_LAYER):
        shards.append(dict(w_qkv=w[prefix[kind] + "_w_qkv"][j].astype(BF16), w_o=w[prefix[kind] + "_w_o"][j].astype(BF16),
                           w1=mlp_w1[layer].astype(BF16), w2=mlp_w2[layer].astype(BF16)))
        if kind == 0:
            mixer_params.append(dict(gq2=jnp.tile(a_q_gain[j], 2)[None], gk2=jnp.tile(a_k_gain[j], 2)[None]))
        elif kind == 1:
            mixer_params.append(dict(slopes=_per_head(_alibi_slopes(len(B_GROUPS) * B_HEADS_PER_GROUP))))
        else:
            mixer_params.append(dict(slopes=_per_head(_alibi_slopes(C_HEADS)), sinks=_per_head(c_sinks[j])))

    norms = dict(attn=attn_norm, mlp=mlp_norm, final=final_norm)
    loss_part, grad_x, own, received, pending, g_small = _local_step(x[0], loss_target[0], norms, mixer_params, shards)
    loss = lax.psum(loss_part[0, 0], ("x", "y", "c"))

    of_kind = lambda kind, key: jnp.stack([g_small["mixer"][layer][key] for layer, (k, _) in enumerate(MIXER_OF_LAYER)
                                           if k == kind])
    small_grads = dict(
        attn_norm=jnp.concatenate(g_small["attn"], axis=0), mlp_norm=jnp.concatenate(g_small["mlp"], axis=0),
        a_q_gain=of_kind(0, "q_gain"), a_k_gain=of_kind(0, "k_gain"), c_sinks=of_kind(2, "sinks"),
        final_norm=g_small["final"][0])
    packed, spans = _pack_small([small_grads[name] for name in SMALLS])
    *last, all_small = _run_cargo(_scatter_cargo([own[item] for item in pending], packed), name="scatter_last")
    received.update(zip(pending, last))

    me_chip = 2 * lax.axis_index("x") + lax.axis_index("y")
    partial = []
    for name in MATS:
        key = name[2:] if name[0] in "abc" else name[4:]
        layers = [layer for layer, (kind, _) in enumerate(MIXER_OF_LAYER)
                  if name.startswith("mlp") or prefix[kind] == name[0]]
        sums = [_sum_quarters(lax.dynamic_index_in_dim(own[layer, key], me_chip, axis=0, keepdims=False),
                              received[layer, key], name=f"sum_{name}_l{layer}") for layer in layers]
        partial.append(jnp.concatenate(sums, axis=0))
    other = _swap_cores(partial)

    out = {}
    for name, mine, theirs in zip(MATS, partial, other):
        shape = w[name].shape
        res = _adamw(*[a.reshape(-1, shape[-1]) for a in (w[name], *mom[name])], [mine, theirs], name=f"adamw_{name}")
        out[name] = [a.reshape(shape) for a in res]
    for name, (at, n) in zip(SMALLS, spans):
        shape = w[name].shape
        packed_in = [_pack_small([a])[0] for a in (w[name], *mom[name])]
        res = _adamw(*packed_in, [all_small[:, at:at + n]], name=f"adamw_{name}")
        out[name] = [a.reshape(-1)[:w[name].size].reshape(shape) for a in res]

    return (loss, grad_x[None], *[out[name][0] for name in WEIGHTS], *[out[name][1] for name in WEIGHTS],
            *[out[name][2] for name in WEIGHTS], *[out[name][3] for name in WEIGHTS])
```

```python
from typing import Callable, NamedTuple

import jax
import jax.numpy as jnp
from jax import lax
from jax.experimental import pallas as pl
from jax.experimental.pallas import tpu as pltpu

F32 = jnp.float32
BF16 = jnp.bfloat16
MESH = pl.DeviceIdType.MESH
ANY = pl.BlockSpec(memory_space=pl.ANY)

D_MODEL = 1024
HEAD_DIM = 64
GRID_W = 64
ROPE_THETA = 10000.0
RMS_EPS = 1e-6
QK_SCALE = HEAD_DIM ** -0.5
LOG2E = 1.4426950408889634
LN2 = 0.6931471805599453
A_HEADS, A_KV = 16, 4
B_GROUPS = ((128, 1), (512, 4), (2048, 16))
B_HEADS_PER_GROUP, B_KV_PER_GROUP = 6, 2
C_HEADS, C_KV, C_WINDOW = 16, 4, 128
DEPTH, N_MIXERS = 4, 3
ADAM_LR, ADAM_B1, ADAM_B2, ADAM_EPS, ADAM_WD, ADAM_STEP = 0.001, 0.9, 0.999, 1e-08, 0.01, 10

WIN_REACH = 128
V7X_VMEM_BUDGET = 48 * 1024 * 1024
LANES = 128
ROW_TILE = 1024


def _params(semantics):
    return pltpu.CompilerParams(dimension_semantics=semantics, vmem_limit_bytes=V7X_VMEM_BUDGET)


def _tile(n, cap):
    if n <= cap:
        return n
    t = (cap // LANES) * LANES
    while n % t:
        t -= LANES
    return t


def _norm_mm(h, gain, w, *, out_dtype, relu2, transpose_out, name):
    m, d = h.shape
    by_quarter = w.ndim == 3
    assert not (by_quarter and transpose_out)
    n = w.shape[-1] * (4 if by_quarter else 1)
    per_step = 2 if by_quarter and 2 * w.shape[-1] <= 2048 else 1
    tm, tn = min(ROW_TILE, m), (per_step * w.shape[-1] if by_quarter else _tile(n, 2048))
    w_spec = (pl.BlockSpec((per_step, d, tn // per_step), lambda i, j: (j, 0, 0)) if by_quarter
              else pl.BlockSpec((d, tn), lambda i, j: (0, j)))
    y_spec = (pl.BlockSpec((tn, tm), lambda i, j: (j, i)) if transpose_out
              else pl.BlockSpec((tm, tn), lambda i, j: (i, j)))

    def body(h_ref, g_ref, w_ref, hn_ref, y_ref):
        @pl.when(pl.program_id(1) == 0)
        def _():
            x = h_ref[...]
            r = lax.rsqrt(jnp.mean(x * x, axis=-1, keepdims=True) + RMS_EPS)
            hn_ref[...] = (x * r * g_ref[...]).astype(BF16)

        def finish(y):
            if relu2:
                y = jnp.maximum(y, 0.0)
                y = y * y
            return y.astype(y_ref.dtype)

        if transpose_out:
            y_ref[...] = finish(lax.dot_general(w_ref[...], hn_ref[...], (((0,), (1,)), ((), ())),
                                                preferred_element_type=F32))
        elif by_quarter:
            cols = tn // per_step
            for q in range(per_step):
                y_ref[:, q * cols:(q + 1) * cols] = finish(jnp.dot(hn_ref[...], w_ref[q], preferred_element_type=F32))
        else:
            y_ref[...] = finish(jnp.dot(hn_ref[...], w_ref[...], preferred_element_type=F32))

    return pl.pallas_call(
        body, name=name, grid=(m // tm, n // tn),
        in_specs=[pl.BlockSpec((tm, d), lambda i, j: (i, 0)), pl.BlockSpec((1, d), lambda i, j: (0, 0)), w_spec],
        out_specs=[pl.BlockSpec((tm, d), lambda i, j: (i, 0)), y_spec],
        out_shape=[jax.ShapeDtypeStruct((m, d), BF16), jax.ShapeDtypeStruct((n, m) if transpose_out else (m, n), out_dtype)],
        compiler_params=_params(("parallel", "arbitrary")),
    )(h, gain, w)


def _mm_res(a, w, h_in, *, a_transposed, name):
    k, d = w.shape
    m = h_in.shape[0]
    tm, tk = min(ROW_TILE, m), _tile(k, 2048)
    lhs_contracts = 0 if a_transposed else 1

    def body(a_ref, w_ref, h_ref, o_ref):
        @pl.when(pl.program_id(1) == 0)
        def _():
            o_ref[...] = h_ref[...]

        o_ref[...] += lax.dot_general(a_ref[...], w_ref[...], (((lhs_contracts,), (0,)), ((), ())),
                                      preferred_element_type=F32)

    a_spec = (pl.BlockSpec((tk, tm), lambda i, j: (j, i)) if a_transposed
              else pl.BlockSpec((tm, tk), lambda i, j: (i, j)))
    return pl.pallas_call(
        body, name=name, grid=(m // tm, k // tk),
        in_specs=[a_spec, pl.BlockSpec((tk, d), lambda i, j: (j, 0)), pl.BlockSpec((tm, d), lambda i, j: (i, 0))],
        out_specs=pl.BlockSpec((tm, d), lambda i, j: (i, 0)),
        out_shape=jax.ShapeDtypeStruct((m, d), F32),
        compiler_params=_params(("parallel", "arbitrary")),
    )(a, w, h_in)


def _mm_nt(a, w, act, *, transpose_out, name):
    m, d = a.shape
    n = w.shape[0]
    tm, tn = min(ROW_TILE, m), _tile(n, 1152)
    assert act is None or not transpose_out
    nt = (((1,), (1,)), ((), ()))

    def body(*refs):
        a_ref, w_ref = refs[0], refs[1]
        o_ref = refs[-1]
        if transpose_out:
            acc = lax.dot_general(w_ref[...], a_ref[...].astype(BF16), nt, preferred_element_type=F32)
        else:
            acc = lax.dot_general(a_ref[...].astype(BF16), w_ref[...], nt, preferred_element_type=F32)
        if act is not None:
            acc = acc * (2.0 * jnp.sqrt(refs[2][...].astype(F32)))
        o_ref[...] = acc.astype(BF16)

    in_specs = [pl.BlockSpec((tm, d), lambda i, j: (i, 0)), pl.BlockSpec((tn, d), lambda i, j: (j, 0))]
    args = [a, w]
    if act is not None:
        in_specs.append(pl.BlockSpec((tm, tn), lambda i, j: (i, j)))
        args.append(act)
    out_spec = (pl.BlockSpec((tn, tm), lambda i, j: (j, i)) if transpose_out
                else pl.BlockSpec((tm, tn), lambda i, j: (i, j)))
    return pl.pallas_call(
        body, name=name, grid=(m // tm, n // tn), in_specs=in_specs, out_specs=out_spec,
        out_shape=jax.ShapeDtypeStruct((n, m) if transpose_out else (m, n), BF16),
        compiler_params=_params(("parallel", "parallel")),
    )(*args)


def _rmsnorm_bwd(dn, x, gain):
    r = lax.rsqrt(jnp.mean(x * x, axis=-1, keepdims=True) + RMS_EPS)
    xh = x * r
    dgain = jnp.sum(dn * xh, axis=0, keepdims=True)
    u = dn * gain
    dx = r * (u - xh * jnp.mean(u * xh, axis=-1, keepdims=True))
    return dx, dgain


def _mm_nt_normbwd(g, w, h, gain, dh_in, *, g_transposed, name):
    m, k = g.shape[::-1] if g_transposed else g.shape
    by_quarter = w.ndim == 3
    d = w.shape[-2]
    tm, tk = min(ROW_TILE, m), (w.shape[-1] if by_quarter else _tile(k, 1024))
    sub = min(256, tm)
    nk = k // tk
    w_spec = (pl.BlockSpec((None, d, tk), lambda i, j: (j, 0, 0)) if by_quarter
              else pl.BlockSpec((d, tk), lambda i, j: (0, j)))

    def body(g_ref, w_ref, h_ref, gain_ref, dh_ref, o_ref, dg_ref, acc_ref):
        i, j = pl.program_id(0), pl.program_id(1)

        @pl.when((i == 0) & (j == 0))
        def _():
            dg_ref[...] = jnp.zeros_like(dg_ref)

        @pl.when(j == 0)
        def _():
            acc_ref[...] = jnp.zeros_like(acc_ref)

        acc_ref[...] += lax.dot_general(g_ref[...], w_ref[...], (((0 if g_transposed else 1,), (1,)), ((), ())),
                                        preferred_element_type=F32)

        @pl.when(j == nk - 1)
        def _():
            for r in range(0, tm, sub):
                rows = slice(r, r + sub)
                dx, dgain = _rmsnorm_bwd(acc_ref[rows, :], h_ref[rows, :], gain_ref[...])
                dg_ref[...] += dgain
                o_ref[rows, :] = dh_ref[rows, :] + dx

    return pl.pallas_call(
        body, name=name, grid=(m // tm, nk),
        in_specs=[pl.BlockSpec((tk, tm), lambda i, j: (j, i)) if g_transposed
                  else pl.BlockSpec((tm, tk), lambda i, j: (i, j)), w_spec,
                  pl.BlockSpec((tm, d), lambda i, j: (i, 0)), pl.BlockSpec((1, d), lambda i, j: (0, 0)),
                  pl.BlockSpec((tm, d), lambda i, j: (i, 0))],
        out_specs=[pl.BlockSpec((tm, d), lambda i, j: (i, 0)), pl.BlockSpec((1, d), lambda i, j: (0, 0))],
        out_shape=[jax.ShapeDtypeStruct((m, d), F32), jax.ShapeDtypeStruct((1, d), F32)],
        scratch_shapes=[pltpu.VMEM((tm, d), F32)],
        compiler_params=_params(("arbitrary", "arbitrary")),
    )(g, w, h, gain, dh_in)


def _mm_tn(x, g, *, x_transposed, g_transposed, column_quarters, name):
    k, m = x.shape if x_transposed else x.shape[::-1]
    n = g.shape[0] if g_transposed else g.shape[1]
    tm, tk, tn = min(2 * ROW_TILE, m), _tile(k, 1152), (n // 4 if column_quarters else _tile(n, 1024))
    nm = m // tm
    lhs_contracts, rhs_contracts = (1 if x_transposed else 0), (1 if g_transposed else 0)
    g_spec = (pl.BlockSpec((tn, tm), lambda a, b, s: (b, s)) if g_transposed
              else pl.BlockSpec((tm, tn), lambda a, b, s: (s, b)))

    def body(x_ref, g_ref, o_ref, acc_ref):
        s = pl.program_id(2)

        @pl.when(s == 0)
        def _():
            acc_ref[...] = jnp.zeros_like(acc_ref)

        acc_ref[...] += lax.dot_general(x_ref[...], g_ref[...].astype(BF16),
                                        (((lhs_contracts,), (rhs_contracts,)), ((), ())), preferred_element_type=F32)

        @pl.when(s == nm - 1)
        def _():
            o_ref[...] = acc_ref[...].astype(BF16)

    x_spec = (pl.BlockSpec((tk, tm), lambda a, b, s: (a, s)) if x_transposed
              else pl.BlockSpec((tm, tk), lambda a, b, s: (s, a)))
    out_spec = (pl.BlockSpec((None, tk, tn), lambda a, b, s: (b, a, 0)) if column_quarters
                else pl.BlockSpec((tk, tn), lambda a, b, s: (a, b)))
    return pl.pallas_call(
        body, name=name, grid=(k // tk, n // tn, nm),
        in_specs=[x_spec, g_spec], out_specs=out_spec,
        out_shape=jax.ShapeDtypeStruct((4, k, n // 4) if column_quarters else (k, n), BF16),
        scratch_shapes=[pltpu.VMEM((tk, tn), F32)],
        compiler_params=_params(("parallel", "parallel", "arbitrary")),
    )(x, g)


def _loss_head(h, gain, target):
    m, d = h.shape
    tm = 512

    def body(h_ref, g_ref, t_ref, dh_ref, loss_ref, dg_ref):
        @pl.when(pl.program_id(0) == 0)
        def _():
            loss_ref[...] = jnp.zeros_like(loss_ref)
            dg_ref[...] = jnp.zeros_like(dg_ref)

        x = h_ref[...]
        gain_v = g_ref[...]
        r = lax.rsqrt(jnp.mean(x * x, axis=-1, keepdims=True) + RMS_EPS)
        err = x * r * gain_v - t_ref[...]
        loss_ref[...] += 0.5 * jnp.sum(jnp.mean(err * err, axis=-1, keepdims=True), axis=0, keepdims=True)
        dx, dgain = _rmsnorm_bwd(err * (1.0 / d), x, gain_v)
        dg_ref[...] += dgain
        dh_ref[...] = dx

    return pl.pallas_call(
        body, name="loss_head", grid=(m // tm,),
        in_specs=[pl.BlockSpec((tm, d), lambda i: (i, 0)), pl.BlockSpec((1, d), lambda i: (0, 0)),
                  pl.BlockSpec((tm, d), lambda i: (i, 0))],
        out_specs=[pl.BlockSpec((tm, d), lambda i: (i, 0)), pl.BlockSpec((1, LANES), lambda i: (0, 0)),
                   pl.BlockSpec((1, d), lambda i: (0, 0))],
        out_shape=[jax.ShapeDtypeStruct((m, d), F32), jax.ShapeDtypeStruct((1, LANES), F32),
                   jax.ShapeDtypeStruct((1, d), F32)],
        compiler_params=_params(("arbitrary",)),
    )(h, gain, target)


def _rope_tables(s):
    t = jnp.arange(s)
    row = (t // GRID_W).astype(F32)
    col = (t % GRID_W).astype(F32)
    axis_dim = HEAD_DIM // 2
    inv_freq = ROPE_THETA ** (-jnp.arange(0, axis_dim, 2, dtype=F32) / axis_dim)
    ar, ac = row[:, None] * inv_freq, col[:, None] * inv_freq
    cos = jnp.concatenate([jnp.cos(ar), jnp.cos(ar), jnp.cos(ac), jnp.cos(ac)], axis=-1)
    sin = jnp.concatenate([-jnp.sin(ar), jnp.sin(ar), -jnp.sin(ac), jnp.sin(ac)], axis=-1)
    return jnp.tile(cos, (1, 2)), jnp.tile(sin, (1, 2))


def _swap16(x):
    lane = lax.broadcasted_iota(jnp.int32, x.shape, 1)
    return jnp.where((lane % 32) < 16, pltpu.roll(x, LANES - 16, 1), pltpu.roll(x, 16, 1))


def _head_mean(v):
    lane = lax.broadcasted_iota(jnp.int32, v.shape, 1)
    lo = lane < HEAD_DIM
    s_all = jnp.sum(v, axis=-1, keepdims=True)
    s_lo = jnp.sum(jnp.where(lo, v, 0.0), axis=-1, keepdims=True)
    return jnp.where(lo, s_lo, s_all - s_lo) * (1.0 / HEAD_DIM)


def _head_rstd(x):
    return lax.rsqrt(_head_mean(x * x) + RMS_EPS)


def _norm_rope(x, r, gain2, cos, sin):
    nrm = x * r * gain2
    return nrm * cos + _swap16(nrm) * sin


def _norm_rope_bwd(dy, x, r, gain2, cos, sin):
    dn = dy * cos + _swap16(dy * sin)
    xh = x * r
    dgain = jnp.sum(dn * xh, axis=0, keepdims=True)
    u = dn * gain2
    return r * (u - xh * _head_mean(u * xh)), dgain


def _a_prep(qkv, cos, sin, gq2, gk2):
    s = qkv.shape[0]
    tr = 256
    nq, nk = A_HEADS * HEAD_DIM, A_KV * HEAD_DIM

    def body(qkv_ref, cos_ref, sin_ref, gq_ref, gk_ref, qt_ref, k_ref, v_ref):
        cos_v, sin_v = cos_ref[...], sin_ref[...]
        rstd = [_head_rstd(qkv_ref[:, c * LANES:(c + 1) * LANES]) for c in range((nq + nk) // LANES)]
        for c in range(nq // LANES):
            y = _norm_rope(qkv_ref[:, c * LANES:(c + 1) * LANES], rstd[c], gq_ref[...], cos_v, sin_v)
            yt = (y * (QK_SCALE * LOG2E)).T
            qt_ref[2 * c] = yt[:HEAD_DIM].astype(BF16)
            qt_ref[2 * c + 1] = yt[HEAD_DIM:].astype(BF16)
        for c in range(nk // LANES):
            y = _norm_rope(qkv_ref[:, nq + c * LANES:nq + (c + 1) * LANES], rstd[nq // LANES + c], gk_ref[...],
                           cos_v, sin_v)
            k_ref[2 * c] = y[:, :HEAD_DIM].astype(BF16)
            k_ref[2 * c + 1] = y[:, HEAD_DIM:].astype(BF16)
            x = qkv_ref[:, nq + nk + c * LANES:nq + nk + (c + 1) * LANES]
            v_ref[2 * c] = x[:, :HEAD_DIM].astype(BF16)
            v_ref[2 * c + 1] = x[:, HEAD_DIM:].astype(BF16)

    return pl.pallas_call(
        body, name="a_prep", grid=(s // tr,),
        in_specs=[pl.BlockSpec((tr, nq + 2 * nk), lambda i: (i, 0)), pl.BlockSpec((tr, LANES), lambda i: (i, 0)),
                  pl.BlockSpec((tr, LANES), lambda i: (i, 0)), pl.BlockSpec((1, LANES), lambda i: (0, 0)),
                  pl.BlockSpec((1, LANES), lambda i: (0, 0))],
        out_specs=[pl.BlockSpec((A_HEADS, HEAD_DIM, tr), lambda i: (0, 0, i)),
                   pl.BlockSpec((A_KV, tr, HEAD_DIM), lambda i: (0, i, 0)),
                   pl.BlockSpec((A_KV, tr, HEAD_DIM), lambda i: (0, i, 0))],
        out_shape=[jax.ShapeDtypeStruct((A_HEADS, HEAD_DIM, s), BF16), jax.ShapeDtypeStruct((A_KV, s, HEAD_DIM), BF16),
                   jax.ShapeDtypeStruct((A_KV, s, HEAD_DIM), BF16)],
        compiler_params=_params(("parallel",)),
    )(qkv, cos, sin, gq2, gk2)


def _a_prep_bwd(dqt, dkt, dvt, qkv, cos, sin, gq2, gk2):
    s = qkv.shape[0]
    tr = 256
    nq, nk = A_HEADS * HEAD_DIM, A_KV * HEAD_DIM

    def body(dqt_ref, dkt_ref, dvt_ref, qkv_ref, cos_ref, sin_ref, gq_ref, gk_ref, o_ref, dgq_ref, dgk_ref):
        @pl.when(pl.program_id(0) == 0)
        def _():
            dgq_ref[...] = jnp.zeros_like(dgq_ref)
            dgk_ref[...] = jnp.zeros_like(dgk_ref)

        cos_v, sin_v = cos_ref[...], sin_ref[...]

        def pair(ref, c):
            return jnp.concatenate([ref[2 * c], ref[2 * c + 1]], axis=0).T

        rstd = [_head_rstd(qkv_ref[:, c * LANES:(c + 1) * LANES]) for c in range((nq + nk) // LANES)]
        dgq = jnp.zeros((1, LANES), F32)
        for c in range(nq // LANES):
            dx, dg = _norm_rope_bwd(pair(dqt_ref, c) * QK_SCALE, qkv_ref[:, c * LANES:(c + 1) * LANES], rstd[c],
                                    gq_ref[...], cos_v, sin_v)
            o_ref[:, c * LANES:(c + 1) * LANES] = dx.astype(BF16)
            dgq = dgq + dg
        dgq_ref[...] += dgq
        dgk = jnp.zeros((1, LANES), F32)
        for c in range(nk // LANES):
            lo = nq + c * LANES
            dx, dg = _norm_rope_bwd(pair(dkt_ref, c) * LN2, qkv_ref[:, lo:lo + LANES], rstd[nq // LANES + c],
                                    gk_ref[...], cos_v, sin_v)
            o_ref[:, lo:lo + LANES] = dx.astype(BF16)
            dgk = dgk + dg
            o_ref[:, lo + nk:lo + nk + LANES] = pair(dvt_ref, c).astype(BF16)
        dgk_ref[...] += dgk

    return pl.pallas_call(
        body, name="a_prep_bwd", grid=(s // tr,),
        in_specs=[pl.BlockSpec((A_HEADS, HEAD_DIM, tr), lambda i: (0, 0, i)),
                  pl.BlockSpec((A_KV, HEAD_DIM, tr), lambda i: (0, 0, i)),
                  pl.BlockSpec((A_KV, HEAD_DIM, tr), lambda i: (0, 0, i)),
                  pl.BlockSpec((tr, nq + 2 * nk), lambda i: (i, 0)), pl.BlockSpec((tr, LANES), lambda i: (i, 0)),
                  pl.BlockSpec((tr, LANES), lambda i: (i, 0)), pl.BlockSpec((1, LANES), lambda i: (0, 0)),
                  pl.BlockSpec((1, LANES), lambda i: (0, 0))],
        out_specs=[pl.BlockSpec((tr, nq + 2 * nk), lambda i: (i, 0)), pl.BlockSpec((1, LANES), lambda i: (0, 0)),
                   pl.BlockSpec((1, LANES), lambda i: (0, 0))],
        out_shape=[jax.ShapeDtypeStruct((s, nq + 2 * nk), BF16), jax.ShapeDtypeStruct((1, LANES), F32),
                   jax.ShapeDtypeStruct((1, LANES), F32)],
        compiler_params=_params(("arbitrary",)),
    )(dqt, dkt, dvt, qkv, cos, sin, gq2, gk2)


A_TQ = 1024
A_TQ_SUB = 256
A_TQ_BWD = 1024
A_KEY_CHUNK = 512


def _a_attn_fwd(qt, k, v, cargo, *, name):
    nh, _, s = qt.shape
    rep = nh // k.shape[0]
    tq = min(A_TQ, s)
    sub = min(A_TQ_SUB, tq)
    grid = (nh, s // tq)

    def body(qt_ref, k_ref, v_ref, o_ref, lse_ref):
        scores = [jnp.dot(k_ref[0], qt_ref[0, :, a:a + sub], preferred_element_type=F32)
                  for a in range(0, tq, sub)]
        for a, st in zip(range(0, tq, sub), scores):
            mx = jnp.max(st, axis=0, keepdims=True)
            p = jnp.exp2(st - mx)
            den = jnp.sum(p, axis=0, keepdims=True)
            ot = lax.dot_general(v_ref[0], p.astype(BF16), (((0,), (0,)), ((), ())), preferred_element_type=F32)
            o_ref[0, :, a:a + sub] = (ot / den).astype(BF16)
            lse_ref[0, :, a:a + sub] = mx + jnp.log(den) * LOG2E

    carried = _carry(cargo, grid, 3, 2, body)
    res = pl.pallas_call(
        carried.body, name=name, grid=grid,
        in_specs=[pl.BlockSpec((1, HEAD_DIM, tq), lambda h, i: (h, 0, i)),
                  pl.BlockSpec((1, s, HEAD_DIM), lambda h, i: (h // rep, 0, 0)),
                  pl.BlockSpec((1, s, HEAD_DIM), lambda h, i: (h // rep, 0, 0))] + carried.in_specs,
        out_specs=[pl.BlockSpec((1, HEAD_DIM, tq), lambda h, i: (h, 0, i)),
                   pl.BlockSpec((1, 1, tq), lambda h, i: (h, 0, i))] + carried.out_specs,
        out_shape=[jax.ShapeDtypeStruct((nh, HEAD_DIM, s), BF16), jax.ShapeDtypeStruct((nh, 1, s), F32)]
        + carried.out_shape,
        scratch_shapes=carried.scratch,
        compiler_params=_params(("arbitrary", "arbitrary")),
    )(qt, k, v, *carried.args)
    return res[0], res[1], res[2:]


def _a_attn_bwd(qt, k, v, dot, ot, lse, cargo, *, name):
    nh, _, s = qt.shape
    nkv = k.shape[0]
    rep = nh // nkv
    tq, ck = min(A_TQ_BWD, s), min(A_KEY_CHUNK, s)
    grid = (nh, s // tq)

    def body(qt_ref, k_ref, v_ref, dot_ref, ot_ref, lse_ref, dq_ref, dk_ref, dv_ref):
        h, i = pl.program_id(0), pl.program_id(1)

        @pl.when((h % rep == 0) & (i == 0))
        def _():
            dk_ref[...] = jnp.zeros_like(dk_ref)
            dv_ref[...] = jnp.zeros_like(dv_ref)

        q_t, do_t, lse_v = qt_ref[0], dot_ref[0], lse_ref[0]
        delta = jnp.sum(do_t.astype(F32) * ot_ref[0].astype(F32), axis=0, keepdims=True)
        nt = (((1,), (1,)), ((), ()))
        dq = jnp.zeros((HEAD_DIM, tq), F32)
        for c in range(s // ck):
            keys = slice(c * ck, (c + 1) * ck)
            kc = k_ref[0, keys, :]
            p = jnp.exp2(jnp.dot(kc, q_t, preferred_element_type=F32) - lse_v)
            dp = jnp.dot(v_ref[0, keys, :], do_t, preferred_element_type=F32)
            ds = (p * (dp - delta)).astype(BF16)
            dv_ref[0, :, keys] += lax.dot_general(do_t, p.astype(BF16), nt, preferred_element_type=F32)
            dk_ref[0, :, keys] += lax.dot_general(q_t, ds, nt, preferred_element_type=F32)
            dq = dq + lax.dot_general(kc, ds, (((0,), (0,)), ((), ())), preferred_element_type=F32)
        dq_ref[0] = dq

    blk_q = pl.BlockSpec((1, HEAD_DIM, tq), lambda h, i: (h, 0, i))
    blk_row = pl.BlockSpec((1, 1, tq), lambda h, i: (h, 0, i))
    blk_kv = pl.BlockSpec((1, s, HEAD_DIM), lambda h, i: (h // rep, 0, 0))
    blk_acc = pl.BlockSpec((1, HEAD_DIM, s), lambda h, i: (h // rep, 0, 0))
    carried = _carry(cargo, grid, 6, 3, body)
    res = pl.pallas_call(
        carried.body, name=name, grid=grid,
        in_specs=[blk_q, blk_kv, blk_kv, blk_q, blk_q, blk_row] + carried.in_specs,
        out_specs=[blk_q, blk_acc, blk_acc] + carried.out_specs,
        out_shape=[jax.ShapeDtypeStruct((nh, HEAD_DIM, s), F32), jax.ShapeDtypeStruct((nkv, HEAD_DIM, s), F32),
                   jax.ShapeDtypeStruct((nkv, HEAD_DIM, s), F32)] + carried.out_shape,
        scratch_shapes=carried.scratch,
        compiler_params=_params(("arbitrary", "arbitrary")),
    )(qt, k, v, dot, ot, lse, *carried.args)
    return res[0], res[1], res[2], res[3:]


WIN_FAR = 1e30


class _Band(NamedTuple):
    window: int
    dil: int
    seg: int
    stride: int

    @property
    def reach(self):
        return -(-self.window // WIN_REACH) * WIN_REACH


def _win_start(i, tq, tk, s, reach):
    return pl.multiple_of(jnp.clip(i * tq - reach, 0, s - tk), LANES)


def _win_penalty(i, start, tq, tk, band):
    qpos = i * tq + lax.broadcasted_iota(jnp.int32, (tk, tq), 1)
    kpos = start + lax.broadcasted_iota(jnp.int32, (tk, tq), 0)
    dist = jnp.abs(kpos - qpos)
    seg_lo = qpos - (qpos & (band.seg - 1))
    valid = (dist <= band.window) & (kpos >= seg_lo) & (kpos < seg_lo + band.seg)
    if band.stride > 1:
        valid &= (dist & (band.stride - 1)) == 0
    return jnp.where(valid, (dist * band.dil).astype(F32), WIN_FAR)


def _win_scores(kw_t, q_t, slope, pen):
    st = lax.dot_general(kw_t, q_t, (((0,), (0,)), ((), ())), preferred_element_type=F32)
    return st * (QK_SCALE * LOG2E) - (slope * LOG2E) * pen


def _win_tq(s):
    return min(512, s)


def _win_fwd(qt, ktp, vtp, slopes, sinks, *, band, out_dtype, name):
    nh, _, s = qt.shape
    nkv = ktp.shape[0]
    rep = nh // nkv
    tq = _win_tq(s)
    tk = min(tq + 2 * band.reach, s)

    def body(*refs):
        qt_ref, kt_ref, vt_ref, sl_ref = refs[:4]
        o_ref, lse_ref, pen_ref = refs[-3:]
        i, kv = pl.program_id(0), pl.program_id(1)
        start = _win_start(i, tq, tk, s, band.reach)

        @pl.when(kv == 0)
        def _():
            pen_ref[...] = _win_penalty(i, start, tq, tk, band)

        win = pl.ds(start, tk)
        kw_t, vw_t, pen = kt_ref[0, :, win], vt_ref[0, :, win], pen_ref[...]
        scores = [_win_scores(kw_t, qt_ref[g], sl_ref[g][:, :1], pen) for g in range(rep)]
        for g, st in enumerate(scores):
            mx = jnp.max(st, axis=0, keepdims=True)
            if sinks is not None:
                sink = refs[4][g][:, :1] * LOG2E
                mx = jnp.maximum(mx, sink)
            p = jnp.exp2(st - mx)
            den = jnp.sum(p, axis=0, keepdims=True)
            if sinks is not None:
                den = den + jnp.exp2(sink - mx)
            ot = jnp.dot(vw_t, p.astype(BF16), preferred_element_type=F32)
            o_ref[g] = (ot / den).astype(o_ref.dtype)
            lse_ref[g] = mx * LN2 + jnp.log(den)

    blk_q = pl.BlockSpec((rep, HEAD_DIM, tq), lambda i, kv: (kv, 0, i))
    blk_kv = pl.BlockSpec((1, HEAD_DIM, s), lambda i, kv: (kv, 0, 0))
    blk_h = pl.BlockSpec((rep, 1, LANES), lambda i, kv: (kv, 0, 0))
    in_specs, args = [blk_q, blk_kv, blk_kv, blk_h], [qt, ktp, vtp, slopes]
    if sinks is not None:
        in_specs.append(blk_h)
        args.append(sinks)
    return pl.pallas_call(
        body, name=name, grid=(s // tq, nkv), in_specs=in_specs,
        out_specs=[blk_q, pl.BlockSpec((rep, 1, tq), lambda i, kv: (kv, 0, i))],
        out_shape=[jax.ShapeDtypeStruct((nh, HEAD_DIM, s), out_dtype), jax.ShapeDtypeStruct((nh, 1, s), F32)],
        scratch_shapes=[pltpu.VMEM((tk, tq), F32)],
        compiler_params=_params(("arbitrary", "arbitrary")),
    )(*args)


def _win_bwd(qt, ktp, vtp, slopes, sinks, dot, ot, delta, *, band, name):
    nh, _, s = qt.shape
    nkv = ktp.shape[0]
    rep = nh // nkv
    tq = _win_tq(s)
    tk = min(tq + 2 * band.reach, s)
    n_in = 6 + (sinks is not None)

    def body(*refs):
        qt_ref, kt_ref, vt_ref, sl_ref, dot_ref, aux_ref = refs[:6]
        outs, pen_ref = refs[n_in:-1], refs[-1]
        dq_ref, dk_ref, dv_ref = outs[:3]
        i, kv = pl.program_id(0), pl.program_id(1)

        @pl.when((i == 0) & (kv == 0))
        def _():
            dk_ref[...] = jnp.zeros_like(dk_ref)
            dv_ref[...] = jnp.zeros_like(dv_ref)
            if sinks is not None:
                outs[3][...] = jnp.zeros_like(outs[3])

        start = _win_start(i, tq, tk, s, band.reach)

        @pl.when(kv == 0)
        def _():
            pen_ref[...] = _win_penalty(i, start, tq, tk, band)

        win = pl.ds(start, tk)
        kw_t, vw_t, pen = kt_ref[0, :, win], vt_ref[0, :, win], pen_ref[...]
        nt = (((1,), (1,)), ((), ()))
        dk_acc = jnp.zeros((HEAD_DIM, tk), F32)
        dv_acc = jnp.zeros((HEAD_DIM, tk), F32)
        products = [(_win_scores(kw_t, qt_ref[g], sl_ref[g][:, :1], pen),
                     lax.dot_general(vw_t, dot_ref[g], (((0,), (0,)), ((), ())), preferred_element_type=F32))
                    for g in range(rep)]
        for g, (st, dp) in enumerate(products):
            q_t, do_t = qt_ref[g], dot_ref[g]
            mx = jnp.max(st, axis=0, keepdims=True)
            if sinks is not None:
                sink = refs[6][g][:, :1] * LOG2E
                mx = jnp.maximum(mx, sink)
            p = jnp.exp2(st - mx)
            den = jnp.sum(p, axis=0, keepdims=True)
            if sinks is not None:
                p_sink = jnp.exp2(sink - mx)
                den = den + p_sink
            p = p / den
            if delta is None:
                row = jnp.sum(do_t.astype(F32) * aux_ref[g].astype(F32), axis=0, keepdims=True)
            else:
                row = aux_ref[g]
            ds = (p * (dp - row) * QK_SCALE).astype(BF16)
            dv_acc = dv_acc + lax.dot_general(do_t, p.astype(BF16), nt, preferred_element_type=F32)
            dk_acc = dk_acc + lax.dot_general(q_t, ds, nt, preferred_element_type=F32)
            dq_ref[g] = jnp.dot(kw_t, ds, preferred_element_type=F32)
            if sinks is not None:
                outs[3][kv * rep + g] += (jnp.zeros((1, LANES), F32)
                                          - jnp.sum(p_sink / den * row, axis=1, keepdims=True))
        dv_ref[kv, :, win] += dv_acc
        dk_ref[kv, :, win] += dk_acc

    blk_q = pl.BlockSpec((rep, HEAD_DIM, tq), lambda i, kv: (kv, 0, i))
    blk_row = pl.BlockSpec((rep, 1, tq), lambda i, kv: (kv, 0, i))
    blk_kv = pl.BlockSpec((1, HEAD_DIM, s), lambda i, kv: (kv, 0, 0))
    blk_acc = pl.BlockSpec((nkv, HEAD_DIM, s), lambda i, kv: (0, 0, 0))
    blk_h = pl.BlockSpec((rep, 1, LANES), lambda i, kv: (kv, 0, 0))
    in_specs = [blk_q, blk_kv, blk_kv, blk_h, blk_q, blk_q if delta is None else blk_row]
    args = [qt, ktp, vtp, slopes, dot, ot if delta is None else delta]
    out_specs = [blk_q, blk_acc, blk_acc]
    out_shape = [jax.ShapeDtypeStruct((nh, HEAD_DIM, s), F32), jax.ShapeDtypeStruct((nkv, HEAD_DIM, s), F32),
                 jax.ShapeDtypeStruct((nkv, HEAD_DIM, s), F32)]
    if sinks is not None:
        in_specs.append(blk_h)
        args.append(sinks)
        out_specs.append(pl.BlockSpec((nh, 1, LANES), lambda i, h: (0, 0, 0)))
        out_shape.append(jax.ShapeDtypeStruct((nh, 1, LANES), F32))
    res = pl.pallas_call(
        body, name=name, grid=(s // tq, nkv), in_specs=in_specs, out_specs=out_specs, out_shape=out_shape,
        scratch_shapes=[pltpu.VMEM((tk, tq), F32)],
        compiler_params=_params(("arbitrary", "arbitrary")),
    )(*args)
    return res if sinks is not None else (*res, None)


def _group_weights(lse):
    e = jnp.exp(lse - jnp.max(lse, axis=0, keepdims=True))
    return e / jnp.sum(e, axis=0, keepdims=True)


def _b_combine_fwd(ot, lse):
    nh, _, s = ot.shape
    ng, hg, _ = lse.shape
    ts = min(512, s)

    def body(ot_ref, lse_ref, o_ref):
        alpha = _group_weights(lse_ref[...])
        for g in range(ng):
            for j in range(hg):
                o_ref[g * hg + j] = (ot_ref[g * hg + j] * alpha[g, j:j + 1, :]).astype(BF16)

    return pl.pallas_call(
        body, name="b_combine_fwd", grid=(s // ts,),
        in_specs=[pl.BlockSpec((nh, HEAD_DIM, ts), lambda i: (0, 0, i)), pl.BlockSpec((ng, hg, ts), lambda i: (0, 0, i))],
        out_specs=pl.BlockSpec((nh, HEAD_DIM, ts), lambda i: (0, 0, i)),
        out_shape=jax.ShapeDtypeStruct((nh, HEAD_DIM, s), BF16),
        compiler_params=_params(("parallel",)),
    )(ot, lse)


def _b_combine_bwd(dout, ot, lse):
    nh, _, s = ot.shape
    ng, hg, _ = lse.shape
    ts = min(512, s)

    def body(dout_ref, ot_ref, lse_ref, do_ref, delta_ref):
        alpha = _group_weights(lse_ref[...])
        for j in range(hg):
            e = [jnp.sum(dout_ref[g * hg + j].astype(F32) * ot_ref[g * hg + j], axis=0, keepdims=True)
                 for g in range(ng)]
            a = [alpha[g, j:j + 1, :] for g in range(ng)]
            mix = a[0] * e[0]
            for g in range(1, ng):
                mix = mix + a[g] * e[g]
            for g in range(ng):
                do_ref[g * hg + j] = (dout_ref[g * hg + j].astype(F32) * a[g]).astype(BF16)
                delta_ref[g * hg + j] = a[g] * mix

    blk = pl.BlockSpec((nh, HEAD_DIM, ts), lambda i: (0, 0, i))
    return pl.pallas_call(
        body, name="b_combine_bwd", grid=(s // ts,),
        in_specs=[blk, blk, pl.BlockSpec((ng, hg, ts), lambda i: (0, 0, i))],
        out_specs=[blk, pl.BlockSpec((nh, 1, ts), lambda i: (0, 0, i))],
        out_shape=[jax.ShapeDtypeStruct((nh, HEAD_DIM, s), BF16), jax.ShapeDtypeStruct((nh, 1, s), F32)],
        compiler_params=_params(("parallel",)),
    )(dout, ot, lse)


def _alibi_slopes(n):
    return 2.0 ** (-8.0 * jnp.arange(1, n + 1, dtype=F32) / n)


def _per_head(v):
    return jnp.broadcast_to(v.astype(F32)[:, None, None], (v.shape[0], 1, LANES))


def _dilate(x, dil):
    if dil == 1:
        return x
    s = x.shape[-1]
    return jnp.swapaxes(x.reshape(x.shape[:-1] + (s // dil, dil)), -1, -2).reshape(x.shape)


def _undilate(x, dil):
    if dil == 1:
        return x
    s = x.shape[-1]
    return jnp.swapaxes(x.reshape(x.shape[:-1] + (dil, s // dil)), -1, -2).reshape(x.shape)


def _heads(x_t):
    return x_t.reshape(-1, HEAD_DIM, x_t.shape[-1])


B_MAX_STRIDE = 4


def _b_band(window, dilation, s):
    if dilation <= B_MAX_STRIDE:
        return _Band(window // 2, 1, s, dilation), 1
    return _Band(window // 2 // dilation, dilation, s // dilation, 1), dilation


def _mixer_fwd(kind, qkv, p, tabs, cargo, layer):
    s = qkv.shape[0 if kind == 0 else 1]
    if kind == 0:
        qt, k, v = _a_prep(qkv, tabs[0], tabs[1], p["gq2"], p["gk2"])
        ot, lse, brought = _a_attn_fwd(qt, k, v, cargo, name=f"a_attn_fwd_l{layer}")
        return ot.reshape(-1, s), dict(qt=qt, k=k, v=v, ot=ot, lse=lse), brought
    assert cargo is None
    if kind == 2:
        nq, nk = C_HEADS * HEAD_DIM, C_KV * HEAD_DIM
        qt = _heads(qkv[:nq])
        kp, vp = _heads(qkv[nq:nq + nk]), _heads(qkv[nq + nk:])
        ot, _ = _win_fwd(qt, kp, vp, p["slopes"], p["sinks"], band=_Band(C_WINDOW, 1, s, 1), out_dtype=BF16,
                         name="c_attn_fwd")
        return ot.reshape(-1, s), dict(qt=qt, kp=kp, vp=vp, ot=ot), ()
    ng, hg, kg = len(B_GROUPS), B_HEADS_PER_GROUP, B_KV_PER_GROUP
    nq, nk = ng * hg * HEAD_DIM, ng * kg * HEAD_DIM
    qt_all, kt_all, vt_all = _heads(qkv[:nq]), _heads(qkv[nq:nq + nk]), _heads(qkv[nq + nk:])
    saved, outs, lses = [], [], []
    for g, (window, dilation) in enumerate(B_GROUPS):
        band, dil = _b_band(window, dilation, s)
        qt = _dilate(qt_all[g * hg:(g + 1) * hg], dil)
        kp = _dilate(kt_all[g * kg:(g + 1) * kg], dil)
        vp = _dilate(vt_all[g * kg:(g + 1) * kg], dil)
        sl = p["slopes"][g * hg:(g + 1) * hg]
        ot, lse = _win_fwd(qt, kp, vp, sl, None, band=band, out_dtype=F32, name=f"b_attn_fwd_g{g}")
        saved.append(dict(qt=qt, kp=kp, vp=vp))
        outs.append(_undilate(ot, dil))
        lses.append(_undilate(lse[:, 0, :], dil))
    ot_all, lse_all = jnp.concatenate(outs, axis=0), jnp.stack(lses, axis=0)
    mixed = _b_combine_fwd(ot_all, lse_all)
    return mixed.reshape(-1, s), dict(groups=saved, ot=ot_all, lse=lse_all), ()


def _mixer_bwd(kind, do_t, qkv, sv, p, tabs, cargo, layer):
    s = do_t.shape[1]
    do_heads = _heads(do_t)
    small = {}
    if kind == 0:
        dqt, dkt, dvt, brought = _a_attn_bwd(sv["qt"], sv["k"], sv["v"], do_heads, sv["ot"], sv["lse"],
                                             cargo, name=f"a_attn_bwd_l{layer}")
        dqkv, dgq, dgk = _a_prep_bwd(dqt, dkt, dvt, qkv, tabs[0], tabs[1], p["gq2"], p["gk2"])
        small["q_gain"] = dgq[0, :HEAD_DIM] + dgq[0, HEAD_DIM:]
        small["k_gain"] = dgk[0, :HEAD_DIM] + dgk[0, HEAD_DIM:]
        return dqkv, small, brought
    assert cargo is None
    if kind == 2:
        dqt, dkt, dvt, dsink = _win_bwd(sv["qt"], sv["kp"], sv["vp"], p["slopes"], p["sinks"], do_heads,
                                        sv["ot"], None, band=_Band(C_WINDOW, 1, s, 1), name="c_attn_bwd")
        small["sinks"] = dsink[:, 0, 0]
        parts = [dqt.reshape(-1, s), dkt.reshape(-1, s), dvt.reshape(-1, s)]
        return jnp.concatenate(parts, axis=0).astype(BF16), small, ()
    ng, hg, kg = len(B_GROUPS), B_HEADS_PER_GROUP, B_KV_PER_GROUP
    do_own, delta = _b_combine_bwd(do_heads, sv["ot"], sv["lse"])
    dqs, dks, dvs = [], [], []
    for g, (window, dilation) in enumerate(B_GROUPS):
        band, dil = _b_band(window, dilation, s)
        gs = sv["groups"][g]
        dqt, dkt, dvt, _ = _win_bwd(gs["qt"], gs["kp"], gs["vp"], p["slopes"][g * hg:(g + 1) * hg], None,
                                    _dilate(do_own[g * hg:(g + 1) * hg], dil), None,
                                    _dilate(delta[g * hg:(g + 1) * hg], dil), band=band, name=f"b_attn_bwd_g{g}")
        dqs.append(_undilate(dqt, dil))
        dks.append(_undilate(dkt, dil))
        dvs.append(_undilate(dvt, dil))
    parts = [x.reshape(-1, s) for x in dqs + dks + dvs]
    return jnp.concatenate(parts, axis=0).astype(BF16), small, ()


LAYER_MATS = ("w_qkv", "w_o", "w1", "w2")
COLUMN_QUARTERS = ("w_qkv", "w1")


def _whole(key, gathered):
    q, r, c = gathered.shape
    if key == "w1":
        return gathered
    if key in COLUMN_QUARTERS:
        return jnp.transpose(gathered, (1, 0, 2)).reshape(r, q * c)
    return gathered.reshape(q * r, c)


def _quarters(key, g):
    r, c = g.shape
    if key in COLUMN_QUARTERS:
        return jnp.transpose(g.reshape(r, 4, c // 4), (1, 0, 2))
    return g.reshape(4, r // 4, c)


def _local_step(x, target, norms, mixer_params, shards, whole=None):
    s = x.shape[0]
    tabs = _rope_tables(s)
    if whole is None:
        assert MIXER_OF_LAYER[0][0] == 0
        first = _run_cargo(_gather_cargo([shards[0][key] for key in LAYER_MATS]), name="gather_l0")
        mats = {0: {key: _whole(key, g) for key, g in zip(LAYER_MATS, first)}}
        later = _gather_cargo([shards[layer][key] for layer in range(1, DEPTH) for key in LAYER_MATS])
    else:
        mats, later = dict(enumerate(whole)), None
    h = x
    saved = []
    for layer in range(DEPTH):
        kind = layer % N_MIXERS
        w, p = mats[layer], mixer_params[layer]
        hn, qkv = _norm_mm(h, norms["attn"][layer][None], w["w_qkv"], out_dtype=F32 if kind == 0 else BF16,
                           relu2=False, transpose_out=kind != 0, name=f"qkv_proj_l{layer}")
        o_t, sv, brought = _mixer_fwd(kind, qkv, p, tabs, later if layer == 0 else None, layer)
        for n, g in enumerate(brought):
            mats.setdefault(1 + n // len(LAYER_MATS), {})[LAYER_MATS[n % len(LAYER_MATS)]] = _whole(
                LAYER_MATS[n % len(LAYER_MATS)], g)
        h_mid = _mm_res(o_t, w["w_o"], h, a_transposed=True, name=f"o_proj_l{layer}")
        hn2, act = _norm_mm(h_mid, norms["mlp"][layer][None], w["w1"], out_dtype=BF16, relu2=True,
                            transpose_out=False, name=f"mlp_up_l{layer}")
        h_out = _mm_res(act, w["w2"], h_mid, a_transposed=False, name=f"mlp_down_l{layer}")
        saved.append(dict(h=h, hn=hn, qkv=qkv, o_t=o_t, mix=sv, h_mid=h_mid, hn2=hn2, act=act))
        h = h_out

    dh, loss, d_final = _loss_head(h, norms["final"][None], target)

    own, received, pending = {}, {}, []
    d_attn, d_mlp, small = [None] * DEPTH, [None] * DEPTH, [None] * DEPTH
    for layer in reversed(range(DEPTH)):
        kind = layer % N_MIXERS
        w, p, sv = mats[layer], mixer_params[layer], saved[layer]
        du = _mm_nt(dh, w["w2"], sv["act"], transpose_out=False, name=f"mlp_down_bwd_l{layer}")
        own[layer, "w2"] = _quarters("w2", _mm_tn(sv["act"], dh, x_transposed=False, g_transposed=False,
                                                  column_quarters=False, name=f"mlp_w2_grad_l{layer}"))
        own[layer, "w1"] = _mm_tn(sv["hn2"], du, x_transposed=False, g_transposed=False, column_quarters=True,
                                  name=f"mlp_w1_grad_l{layer}")
        dh_mid, d_mlp[layer] = _mm_nt_normbwd(du, w["w1"], sv["h_mid"], norms["mlp"][layer][None], dh,
                                              g_transposed=False, name=f"mlp_up_bwd_l{layer}")
        do_t = _mm_nt(dh_mid, w["w_o"], None, transpose_out=True, name=f"o_proj_bwd_l{layer}")
        own[layer, "w_o"] = _quarters("w_o", _mm_tn(sv["o_t"], dh_mid, x_transposed=True, g_transposed=False,
                                                    column_quarters=False, name=f"w_o_grad_l{layer}"))
        pending += [(layer, "w2"), (layer, "w1"), (layer, "w_o")]
        cargo = None
        if kind == 0 and whole is None:
            cargo, sent, pending = _scatter_cargo([own[item] for item in pending], None), pending, []
        dqkv, small[layer], brought = _mixer_bwd(kind, do_t, sv["qkv"], sv["mix"], p, tabs, cargo, layer)
        if cargo is not None:
            received.update(zip(sent, brought))
        own[layer, "w_qkv"] = _quarters("w_qkv", _mm_tn(sv["hn"], dqkv, x_transposed=False, g_transposed=kind != 0,
                                                        column_quarters=False, name=f"w_qkv_grad_l{layer}"))
        pending.append((layer, "w_qkv"))
        dh, d_attn[layer] = _mm_nt_normbwd(dqkv, w["w_qkv"], sv["h"], norms["attn"][layer][None], dh_mid,
                                           g_transposed=kind != 0, name=f"qkv_proj_bwd_l{layer}")
    return loss, dh, own, received, pending, dict(attn=d_attn, mlp=d_mlp, final=d_final, mixer=small)


CHIP_FLIPS = ((1, 0), (0, 1), (1, 1))


class _Cargo(NamedTuple):
    ins: tuple
    out_shape: tuple
    sem_shapes: tuple
    start: Callable
    wait: Callable


class _Carried(NamedTuple):
    body: Callable
    in_specs: list
    out_specs: list
    out_shape: list
    scratch: list
    args: tuple


def _carry(cargo, grid, n_in, n_out, body):
    if cargo is None:
        return _Carried(body, [], [], [], [], ())
    ci, co = len(cargo.ins), len(cargo.out_shape)

    def wrapped(*refs):
        ins, c_ins = refs[:n_in], refs[n_in:n_in + ci]
        outs, c_outs = refs[n_in + ci:n_in + ci + n_out], refs[n_in + ci + n_out:n_in + ci + n_out + co]
        sems = refs[n_in + ci + n_out + co:]
        first = last = None
        for axis, extent in enumerate(grid):
            at = pl.program_id(axis)
            first = (at == 0) if first is None else first & (at == 0)
            last = (at == extent - 1) if last is None else last & (at == extent - 1)

        @pl.when(first)
        def _():
            cargo.start(c_ins, c_outs, sems)

        body(*ins, *outs)

        @pl.when(last)
        def _():
            cargo.wait(c_ins, c_outs, sems)

    return _Carried(wrapped, [ANY] * ci, [ANY] * co, list(cargo.out_shape), list(cargo.sem_shapes), tuple(cargo.ins))


def _run_cargo(cargo, *, name):
    ci, co = len(cargo.ins), len(cargo.out_shape)

    def body(*refs):
        cargo.start(refs[:ci], refs[ci:ci + co], refs[ci + co:])
        cargo.wait(refs[:ci], refs[ci:ci + co], refs[ci + co:])

    return pl.pallas_call(body, name=name, in_specs=[ANY] * ci, out_specs=[ANY] * co, out_shape=list(cargo.out_shape),
                          scratch_shapes=list(cargo.sem_shapes))(*cargo.ins)


def _other_chip(x, y, j):
    fx, fy = CHIP_FLIPS[j]
    return (1 - x if fx else x), (1 - y if fy else y)


def _gather_cargo(shards):
    n = len(shards)
    halves = [a.shape[0] // 2 for a in shards]

    def copies(ins, outs, sems):
        ici_send, ici_recv, d2d_send, d2d_recv, local_sems = sems
        x, y, c = lax.axis_index("x"), lax.axis_index("y"), lax.axis_index("c")
        me = 2 * x + y

        def half(t, which):
            return pl.ds(pl.multiple_of(which * halves[t], 16), halves[t])

        def over_ici(t, j, arriving):
            px, py = _other_chip(x, y, j)
            return pltpu.make_async_remote_copy(
                src_ref=ins[t].at[half(t, c)], dst_ref=outs[t].at[2 * px + py if arriving else me, half(t, c)],
                send_sem=ici_send.at[t, j], recv_sem=ici_recv.at[t, j], device_id=(px, py, c), device_id_type=MESH)

        def over_d2d(t, j, arriving):
            px, py = _other_chip(x, y, j)
            mine = outs[t].at[2 * px + py, half(t, c)]
            return pltpu.make_async_remote_copy(
                src_ref=mine, dst_ref=outs[t].at[2 * px + py, half(t, 1 - c)] if arriving else mine,
                send_sem=d2d_send.at[t, j], recv_sem=d2d_recv.at[t, j], device_id=(x, y, 1 - c), device_id_type=MESH)

        return over_ici, over_d2d, lambda t: pltpu.make_async_copy(ins[t], outs[t].at[me], local_sems.at[t])

    def start(ins, outs, sems):
        over_ici, _, own = copies(ins, outs, sems)
        for t in range(n):
            own(t).start()
            for j in range(len(CHIP_FLIPS)):
                over_ici(t, j, False).start()

    def wait(ins, outs, sems):
        over_ici, over_d2d, own = copies(ins, outs, sems)
        for t in range(n):
            for j in range(len(CHIP_FLIPS)):
                over_ici(t, j, True).wait_recv()
                over_d2d(t, j, False).start()
        for t in range(n):
            for j in range(len(CHIP_FLIPS)):
                over_d2d(t, j, True).wait_recv()
                over_d2d(t, j, False).wait_send()
                over_ici(t, j, False).wait_send()
            own(t).wait()

    dma = pltpu.SemaphoreType.DMA
    return _Cargo(tuple(shards), tuple(jax.ShapeDtypeStruct((4,) + a.shape, a.dtype) for a in shards),
                  (dma((n, 3)), dma((n, 3)), dma((n, 3)), dma((n, 3)), dma((n,))), start, wait)


def _scatter_cargo(grads, small):
    n = len(grads)

    def copies(ins, outs, sems):
        x, y, c = lax.axis_index("x"), lax.axis_index("y"), lax.axis_index("c")
        me = 4 * x + 2 * y + c

        def remote(t, j):
            px, py = _other_chip(x, y, j)
            return pltpu.make_async_remote_copy(
                src_ref=ins[t].at[2 * px + py], dst_ref=outs[t].at[j], send_sem=sems[0].at[t, j],
                recv_sem=sems[1].at[t, j], device_id=(px, py, c), device_id_type=MESH)

        def small_remote(r, arriving):
            fx, fy, fc = (r + 1) // 4, ((r + 1) // 2) % 2, (r + 1) % 2
            px, py, pc = (1 - x if fx else x), (1 - y if fy else y), (1 - c if fc else c)
            return pltpu.make_async_remote_copy(
                src_ref=ins[n], dst_ref=outs[n].at[4 * px + 2 * py + pc if arriving else me],
                send_sem=sems[2].at[r], recv_sem=sems[3].at[r], device_id=(px, py, pc), device_id_type=MESH)

        return remote, small_remote, lambda: pltpu.make_async_copy(ins[n], outs[n].at[me], sems[4])

    def start(ins, outs, sems):
        remote, small_remote, small_own = copies(ins, outs, sems)
        if small is not None:
            small_own().start()
            for r in range(7):
                small_remote(r, False).start()
        for t in range(n):
            for j in range(len(CHIP_FLIPS)):
                remote(t, j).start()

    def wait(ins, outs, sems):
        remote, small_remote, small_own = copies(ins, outs, sems)
        if small is not None:
            for r in range(7):
                small_remote(r, True).wait_recv()
                small_remote(r, False).wait_send()
            small_own().wait()
        for t in range(n):
            for j in range(len(CHIP_FLIPS)):
                remote(t, j).wait()

    dma = pltpu.SemaphoreType.DMA
    ins = tuple(grads) + (() if small is None else (small,))
    out_shape = tuple(jax.ShapeDtypeStruct((3,) + g.shape[1:], g.dtype) for g in grads)
    sem_shapes = (dma((n, 3)), dma((n, 3)))
    if small is not None:
        out_shape += (jax.ShapeDtypeStruct((8,) + small.shape, small.dtype),)
        sem_shapes += (dma((7,)), dma((7,)), dma(()))
    return _Cargo(ins, out_shape, sem_shapes, start, wait)


def _swap_cores(parts):
    n = len(parts)

    def body(*refs):
        ins, outs = refs[:n], refs[n:2 * n]
        send_sems, recv_sems = refs[2 * n:]
        peer = (lax.axis_index("x"), lax.axis_index("y"), 1 - lax.axis_index("c"))
        copies = [pltpu.make_async_remote_copy(src_ref=ins[t], dst_ref=outs[t], send_sem=send_sems.at[t],
                                               recv_sem=recv_sems.at[t], device_id=peer, device_id_type=MESH)
                  for t in range(n)]
        for cp in copies:
            cp.start()
        for cp in copies:
            cp.wait()

    return pl.pallas_call(
        body, name="swap_cores", in_specs=[ANY] * n, out_specs=[ANY] * n,
        out_shape=[jax.ShapeDtypeStruct(a.shape, a.dtype) for a in parts],
        scratch_shapes=[pltpu.SemaphoreType.DMA((n,)), pltpu.SemaphoreType.DMA((n,))],
    )(*parts)


def _rows_tile(r):
    return 256 if r % 256 == 0 else r


def _sum_quarters(own, recv, *, name):
    r, c = own.shape
    tr = _rows_tile(r)

    def body(own_ref, recv_ref, o_ref):
        acc = own_ref[...].astype(F32)
        for j in range(3):
            acc = acc + recv_ref[j].astype(F32)
        o_ref[...] = acc.astype(BF16)

    return pl.pallas_call(
        body, name=name, grid=(r // tr,),
        in_specs=[pl.BlockSpec((tr, c), lambda i: (i, 0)), pl.BlockSpec((3, tr, c), lambda i: (0, i, 0))],
        out_specs=pl.BlockSpec((tr, c), lambda i: (i, 0)),
        out_shape=jax.ShapeDtypeStruct((r, c), BF16),
        compiler_params=_params(("parallel",)),
    )(own, recv)


def _adamw(w, m, v, parts, *, name):
    r, c = w.shape
    tr = _rows_tile(r)
    c1, c2 = 1.0 - ADAM_B1 ** ADAM_STEP, 1.0 - ADAM_B2 ** ADAM_STEP
    n_parts = len(parts)

    def body(*refs):
        w_ref, m_ref, v_ref = refs[:3]
        g_ref, d_ref, nm_ref, nv_ref = refs[3 + n_parts:]
        terms = []
        for p_ref in refs[3:3 + n_parts]:
            terms += [p_ref[...]] if len(p_ref.shape) == 2 else [p_ref[j] for j in range(p_ref.shape[0])]
        g = terms[0].astype(F32)
        for term in terms[1:]:
            g = g + term.astype(F32)
        m_new = ADAM_B1 * m_ref[...] + (1.0 - ADAM_B1) * g
        v_new = ADAM_B2 * v_ref[...] + (1.0 - ADAM_B2) * (g * g)
        step = (m_new / c1) / (jnp.sqrt(v_new / c2) + ADAM_EPS)
        g_ref[...] = g
        d_ref[...] = -ADAM_LR * (step + ADAM_WD * w_ref[...])
        nm_ref[...] = m_new
        nv_ref[...] = v_new

    blk = pl.BlockSpec((tr, c), lambda i: (i, 0))
    part_specs = [blk if p.ndim == 2 else pl.BlockSpec((p.shape[0], tr, c), lambda i: (0, i, 0)) for p in parts]
    return pl.pallas_call(
        body, name=name, grid=(r // tr,), in_specs=[blk, blk, blk] + part_specs,
        out_specs=[blk] * 4, out_shape=[jax.ShapeDtypeStruct((r, c), F32)] * 4,
        compiler_params=_params(("parallel",)),
    )(w, m, v, *parts)


MATS = ("a_w_qkv", "a_w_o", "b_w_qkv", "b_w_o", "c_w_qkv", "c_w_o", "mlp_w1", "mlp_w2")
SMALLS = ("attn_norm", "mlp_norm", "a_q_gain", "a_k_gain", "c_sinks", "final_norm")
WEIGHTS = ("attn_norm", "mlp_norm", "a_w_qkv", "a_q_gain", "a_k_gain", "a_w_o", "b_w_qkv", "b_w_o", "c_w_qkv",
           "c_sinks", "c_w_o", "mlp_w1", "mlp_w2", "final_norm")
MIXER_OF_LAYER = tuple((layer % N_MIXERS, sum(1 for q in range(layer) if q % N_MIXERS == layer % N_MIXERS))
                       for layer in range(DEPTH))
SMALL_ROWS = 8


def _pack_small(values):
    rows, spans, at = [], [], 0
    for v in values:
        flat = v.reshape(-1)
        n = -(-flat.shape[0] // (SMALL_ROWS * LANES)) * SMALL_ROWS
        rows.append(jnp.pad(flat, (0, n * LANES - flat.shape[0])).reshape(n, LANES))
        spans.append((at, n))
        at += n
    return jnp.concatenate(rows, axis=0), spans


def kernel(x, attn_norm, mlp_norm, a_w_qkv, a_q_gain, a_k_gain, a_w_o, b_w_qkv, b_w_o, c_w_qkv, c_sinks, c_w_o, mlp_w1, mlp_w2, final_norm, loss_target, m_attn_norm, m_mlp_norm, m_a_w_qkv, m_a_q_gain, m_a_k_gain, m_a_w_o, m_b_w_qkv, m_b_w_o, m_c_w_qkv, m_c_sinks, m_c_w_o, m_mlp_w1, m_mlp_w2, m_final_norm, v_attn_norm, v_mlp_norm, v_a_w_qkv, v_a_q_gain, v_a_k_gain, v_a_w_o, v_b_w_qkv, v_b_w_o, v_c_w_qkv, v_c_sinks, v_c_w_o, v_mlp_w1, v_mlp_w2, v_final_norm):
    env = dict(locals())
    w = {name: env[name] for name in WEIGHTS}
    mom = {name: (env["m_" + name], env["v_" + name]) for name in WEIGHTS}

    prefix = ("a", "b", "c")
    shards, mixer_params = [], []
    for layer, (kind, j) in enumerate(MIXER_OF_LAYER):
        shards.append(dict(w_qkv=w[prefix[kind] + "_w_qkv"][j].astype(BF16), w_o=w[prefix[kind] + "_w_o"][j].astype(BF16),
                           w1=mlp_w1[layer].astype(BF16), w2=mlp_w2[layer].astype(BF16)))
        if kind == 0:
            mixer_params.append(dict(gq2=jnp.tile(a_q_gain[j], 2)[None], gk2=jnp.tile(a_k_gain[j], 2)[None]))
        elif kind == 1:
            mixer_params.append(dict(slopes=_per_head(_alibi_slopes(len(B_GROUPS) * B_HEADS_PER_GROUP))))
        else:
            mixer_params.append(dict(slopes=_per_head(_alibi_slopes(C_HEADS)), sinks=_per_head(c_sinks[j])))

    norms = dict(attn=attn_norm, mlp=mlp_norm, final=final_norm)
    loss_part, grad_x, own, received, pending, g_small = _local_step(x[0], loss_target[0], norms, mixer_params, shards)
    loss = lax.psum(loss_part[0, 0], ("x", "y", "c"))

    of_kind = lambda kind, key: jnp.stack([g_small["mixer"][layer][key] for layer, (k, _) in enumerate(MIXER_OF_LAYER)
                                           if k == kind])
    small_grads = dict(
        attn_norm=jnp.concatenate(g_small["attn"], axis=0), mlp_norm=jnp.concatenate(g_small["mlp"], axis=0),
        a_q_gain=of_kind(0, "q_gain"), a_k_gain=of_kind(0, "k_gain"), c_sinks=of_kind(2, "sinks"),
        final_norm=g_small["final"][0])
    packed, spans = _pack_small([small_grads[name] for name in SMALLS])
    *last, all_small = _run_cargo(_scatter_cargo([own[item] for item in pending], packed), name="scatter_last")
    received.update(zip(pending, last))

    me_chip = 2 * lax.axis_index("x") + lax.axis_index("y")
    partial = []
    for name in MATS:
        key = name[2:] if name[0] in "abc" else name[4:]
        layers = [layer for layer, (kind, _) in enumerate(MIXER_OF_LAYER)
                  if name.startswith("mlp") or prefix[kind] == name[0]]
        sums = [_sum_quarters(lax.dynamic_index_in_dim(own[layer, key], me_chip, axis=0, keepdims=False),
                              received[layer, key], name=f"sum_{name}_l{layer}") for layer in layers]
        partial.append(jnp.concatenate(sums, axis=0))
    other = _swap_cores(partial)

    out = {}
    for name, mine, theirs in zip(MATS, partial, other):
        shape = w[name].shape
        res = _adamw(*[a.reshape(-1, shape[-1]) for a in (w[name], *mom[name])], [mine, theirs], name=f"adamw_{name}")
        out[name] = [a.reshape(shape) for a in res]
    for name, (at, n) in zip(SMALLS, spans):
        shape = w[name].shape
        packed_in = [_pack_small([a])[0] for a in (w[name], *mom[name])]
        res = _adamw(*packed_in, [all_small[:, at:at + n]], name=f"adamw_{name}")
        out[name] = [a.reshape(-1)[:w[name].size].reshape(shape) for a in res]

    return (loss, grad_x[None], *[out[name][0] for name in WEIGHTS], *[out[name][1] for name in WEIGHTS],
            *[out[name][2] for name in WEIGHTS], *[out[name][3] for name in WEIGHTS])
```

```python
from typing import Callable, NamedTuple

import jax
import jax.numpy as jnp
from jax import lax
from jax.experimental import pallas as pl
from jax.experimental.pallas import tpu as pltpu

F32 = jnp.float32
BF16 = jnp.bfloat16
MESH = pl.DeviceIdType.MESH
ANY = pl.BlockSpec(memory_space=pl.ANY)

D_MODEL = 1024
HEAD_DIM = 64
GRID_W = 64
ROPE_THETA = 10000.0
RMS_EPS = 1e-6
QK_SCALE = HEAD_DIM ** -0.5
LOG2E = 1.4426950408889634
LN2 = 0.6931471805599453
A_HEADS, A_KV = 16, 4
B_GROUPS = ((128, 1), (512, 4), (2048, 16))
B_HEADS_PER_GROUP, B_KV_PER_GROUP = 6, 2
C_HEADS, C_KV, C_WINDOW = 16, 4, 128
DEPTH, N_MIXERS = 4, 3
ADAM_LR, ADAM_B1, ADAM_B2, ADAM_EPS, ADAM_WD, ADAM_STEP = 0.001, 0.9, 0.999, 1e-08, 0.01, 10

WIN_REACH = 128
V7X_VMEM_BUDGET = 48 * 1024 * 1024
LANES = 128
ROW_TILE = 1024


def _params(semantics):
    return pltpu.CompilerParams(dimension_semantics=semantics, vmem_limit_bytes=V7X_VMEM_BUDGET)


def _tile(n, cap):
    if n <= cap:
        return n
    t = (cap // LANES) * LANES
    while n % t:
        t -= LANES
    return t


def _norm_mm(h, gain, w, *, out_dtype, relu2, transpose_out, name):
    m, d = h.shape
    by_quarter = w.ndim == 3
    assert not (by_quarter and transpose_out)
    n = w.shape[-1] * (4 if by_quarter else 1)
    per_step = 2 if by_quarter and 2 * w.shape[-1] <= 2048 else 1
    tm, tn = min(ROW_TILE, m), (per_step * w.shape[-1] if by_quarter else _tile(n, 2048))
    w_spec = (pl.BlockSpec((per_step, d, tn // per_step), lambda i, j: (j, 0, 0)) if by_quarter
              else pl.BlockSpec((d, tn), lambda i, j: (0, j)))
    y_spec = (pl.BlockSpec((tn, tm), lambda i, j: (j, i)) if transpose_out
              else pl.BlockSpec((tm, tn), lambda i, j: (i, j)))

    def body(h_ref, g_ref, w_ref, hn_ref, y_ref):
        @pl.when(pl.program_id(1) == 0)
        def _():
            x = h_ref[...]
            r = lax.rsqrt(jnp.mean(x * x, axis=-1, keepdims=True) + RMS_EPS)
            hn_ref[...] = (x * r * g_ref[...]).astype(BF16)

        def finish(y):
            if relu2:
                y = jnp.maximum(y, 0.0)
                y = y * y
            return y.astype(y_ref.dtype)

        if transpose_out:
            y_ref[...] = finish(lax.dot_general(w_ref[...], hn_ref[...], (((0,), (1,)), ((), ())),
                                                preferred_element_type=F32))
        elif by_quarter:
            cols = tn // per_step
            for q in range(per_step):
                y_ref[:, q * cols:(q + 1) * cols] = finish(jnp.dot(hn_ref[...], w_ref[q], preferred_element_type=F32))
        else:
            y_ref[...] = finish(jnp.dot(hn_ref[...], w_ref[...], preferred_element_type=F32))

    return pl.pallas_call(
        body, name=name, grid=(m // tm, n // tn),
        in_specs=[pl.BlockSpec((tm, d), lambda i, j: (i, 0)), pl.BlockSpec((1, d), lambda i, j: (0, 0)), w_spec],
        out_specs=[pl.BlockSpec((tm, d), lambda i, j: (i, 0)), y_spec],
        out_shape=[jax.ShapeDtypeStruct((m, d), BF16), jax.ShapeDtypeStruct((n, m) if transpose_out else (m, n), out_dtype)],
        compiler_params=_params(("parallel", "arbitrary")),
    )(h, gain, w)


def _mm_res(a, w, h_in, *, a_transposed, name):
    k, d = w.shape
    m = h_in.shape[0]
    tm, tk = min(ROW_TILE, m), _tile(k, 2048)
    lhs_contracts = 0 if a_transposed else 1

    def body(a_ref, w_ref, h_ref, o_ref):
        @pl.when(pl.program_id(1) == 0)
        def _():
            o_ref[...] = h_ref[...]

        o_ref[...] += lax.dot_general(a_ref[...], w_ref[...], (((lhs_contracts,), (0,)), ((), ())),
                                      preferred_element_type=F32)

    a_spec = (pl.BlockSpec((tk, tm), lambda i, j: (j, i)) if a_transposed
              else pl.BlockSpec((tm, tk), lambda i, j: (i, j)))
    return pl.pallas_call(
        body, name=name, grid=(m // tm, k // tk),
        in_specs=[a_spec, pl.BlockSpec((tk, d), lambda i, j: (j, 0)), pl.BlockSpec((tm, d), lambda i, j: (i, 0))],
        out_specs=pl.BlockSpec((tm, d), lambda i, j: (i, 0)),
        out_shape=jax.ShapeDtypeStruct((m, d), F32),
        compiler_params=_params(("parallel", "arbitrary")),
    )(a, w, h_in)


def _mm_nt(a, w, act, *, transpose_out, name):
    m, d = a.shape
    n = w.shape[0]
    tm, tn = min(ROW_TILE, m), _tile(n, 1152)
    assert act is None or not transpose_out
    nt = (((1,), (1,)), ((), ()))

    def body(*refs):
        a_ref, w_ref = refs[0], refs[1]
        o_ref = refs[-1]
        if transpose_out:
            acc = lax.dot_general(w_ref[...], a_ref[...].astype(BF16), nt, preferred_element_type=F32)
        else:
            acc = lax.dot_general(a_ref[...].astype(BF16), w_ref[...], nt, preferred_element_type=F32)
        if act is not None:
            acc = acc * (2.0 * jnp.sqrt(refs[2][...].astype(F32)))
        o_ref[...] = acc.astype(BF16)

    in_specs = [pl.BlockSpec((tm, d), lambda i, j: (i, 0)), pl.BlockSpec((tn, d), lambda i, j: (j, 0))]
    args = [a, w]
    if act is not None:
        in_specs.append(pl.BlockSpec((tm, tn), lambda i, j: (i, j)))
        args.append(act)
    out_spec = (pl.BlockSpec((tn, tm), lambda i, j: (j, i)) if transpose_out
                else pl.BlockSpec((tm, tn), lambda i, j: (i, j)))
    return pl.pallas_call(
        body, name=name, grid=(m // tm, n // tn), in_specs=in_specs, out_specs=out_spec,
        out_shape=jax.ShapeDtypeStruct((n, m) if transpose_out else (m, n), BF16),
        compiler_params=_params(("parallel", "parallel")),
    )(*args)


def _rmsnorm_bwd(dn, x, gain):
    r = lax.rsqrt(jnp.mean(x * x, axis=-1, keepdims=True) + RMS_EPS)
    xh = x * r
    dgain = jnp.sum(dn * xh, axis=0, keepdims=True)
    u = dn * gain
    dx = r * (u - xh * jnp.mean(u * xh, axis=-1, keepdims=True))
    return dx, dgain


def _mm_nt_normbwd(g, w, h, gain, dh_in, *, g_transposed, name):
    m, k = g.shape[::-1] if g_transposed else g.shape
    by_quarter = w.ndim == 3
    d = w.shape[-2]
    tm, tk = min(ROW_TILE, m), (w.shape[-1] if by_quarter else _tile(k, 1024))
    sub = min(256, tm)
    nk = k // tk
    w_spec = (pl.BlockSpec((None, d, tk), lambda i, j: (j, 0, 0)) if by_quarter
              else pl.BlockSpec((d, tk), lambda i, j: (0, j)))

    def body(g_ref, w_ref, h_ref, gain_ref, dh_ref, o_ref, dg_ref, acc_ref):
        i, j = pl.program_id(0), pl.program_id(1)

        @pl.when((i == 0) & (j == 0))
        def _():
            dg_ref[...] = jnp.zeros_like(dg_ref)

        @pl.when(j == 0)
        def _():
            acc_ref[...] = jnp.zeros_like(acc_ref)

        acc_ref[...] += lax.dot_general(g_ref[...], w_ref[...], (((0 if g_transposed else 1,), (1,)), ((), ())),
                                        preferred_element_type=F32)

        @pl.when(j == nk - 1)
        def _():
            for r in range(0, tm, sub):
                rows = slice(r, r + sub)
                dx, dgain = _rmsnorm_bwd(acc_ref[rows, :], h_ref[rows, :], gain_ref[...])
                dg_ref[...] += dgain
                o_ref[rows, :] = dh_ref[rows, :] + dx

    return pl.pallas_call(
        body, name=name, grid=(m // tm, nk),
        in_specs=[pl.BlockSpec((tk, tm), lambda i, j: (j, i)) if g_transposed
                  else pl.BlockSpec((tm, tk), lambda i, j: (i, j)), w_spec,
                  pl.BlockSpec((tm, d), lambda i, j: (i, 0)), pl.BlockSpec((1, d), lambda i, j: (0, 0)),
                  pl.BlockSpec((tm, d), lambda i, j: (i, 0))],
        out_specs=[pl.BlockSpec((tm, d), lambda i, j: (i, 0)), pl.BlockSpec((1, d), lambda i, j: (0, 0))],
        out_shape=[jax.ShapeDtypeStruct((m, d), F32), jax.ShapeDtypeStruct((1, d), F32)],
        scratch_shapes=[pltpu.VMEM((tm, d), F32)],
        compiler_params=_params(("arbitrary", "arbitrary")),
    )(g, w, h, gain, dh_in)


def _mm_tn(x, g, *, x_transposed, g_transposed, column_quarters, name):
    k, m = x.shape if x_transposed else x.shape[::-1]
    n = g.shape[0] if g_transposed else g.shape[1]
    tm, tk, tn = min(2 * ROW_TILE, m), _tile(k, 1152), (n // 4 if column_quarters else _tile(n, 1024))
    nm = m // tm
    lhs_contracts, rhs_contracts = (1 if x_transposed else 0), (1 if g_transposed else 0)
    g_spec = (pl.BlockSpec((tn, tm), lambda a, b, s: (b, s)) if g_transposed
              else pl.BlockSpec((tm, tn), lambda a, b, s: (s, b)))

    def body(x_ref, g_ref, o_ref, acc_ref):
        s = pl.program_id(2)

        @pl.when(s == 0)
        def _():
            acc_ref[...] = jnp.zeros_like(acc_ref)

        acc_ref[...] += lax.dot_general(x_ref[...], g_ref[...].astype(BF16),
                                        (((lhs_contracts,), (rhs_contracts,)), ((), ())), preferred_element_type=F32)

        @pl.when(s == nm - 1)
        def _():
            o_ref[...] = acc_ref[...].astype(BF16)

    x_spec = (pl.BlockSpec((tk, tm), lambda a, b, s: (a, s)) if x_transposed
              else pl.BlockSpec((tm, tk), lambda a, b, s: (s, a)))
    out_spec = (pl.BlockSpec((None, tk, tn), lambda a, b, s: (b, a, 0)) if column_quarters
                else pl.BlockSpec((tk, tn), lambda a, b, s: (a, b)))
    return pl.pallas_call(
        body, name=name, grid=(k // tk, n // tn, nm),
        in_specs=[x_spec, g_spec], out_specs=out_spec,
        out_shape=jax.ShapeDtypeStruct((4, k, n // 4) if column_quarters else (k, n), BF16),
        scratch_shapes=[pltpu.VMEM((tk, tn), F32)],
        compiler_params=_params(("parallel", "parallel", "arbitrary")),
    )(x, g)


def _loss_head(h, gain, target):
    m, d = h.shape
    tm = 512

    def body(h_ref, g_ref, t_ref, dh_ref, loss_ref, dg_ref):
        @pl.when(pl.program_id(0) == 0)
        def _():
            loss_ref[...] = jnp.zeros_like(loss_ref)
            dg_ref[...] = jnp.zeros_like(dg_ref)

        x = h_ref[...]
        gain_v = g_ref[...]
        r = lax.rsqrt(jnp.mean(x * x, axis=-1, keepdims=True) + RMS_EPS)
        err = x * r * gain_v - t_ref[...]
        loss_ref[...] += 0.5 * jnp.sum(jnp.mean(err * err, axis=-1, keepdims=True), axis=0, keepdims=True)
        dx, dgain = _rmsnorm_bwd(err * (1.0 / d), x, gain_v)
        dg_ref[...] += dgain
        dh_ref[...] = dx

    return pl.pallas_call(
        body, name="loss_head", grid=(m // tm,),
        in_specs=[pl.BlockSpec((tm, d), lambda i: (i, 0)), pl.BlockSpec((1, d), lambda i: (0, 0)),
                  pl.BlockSpec((tm, d), lambda i: (i, 0))],
        out_specs=[pl.BlockSpec((tm, d), lambda i: (i, 0)), pl.BlockSpec((1, LANES), lambda i: (0, 0)),
                   pl.BlockSpec((1, d), lambda i: (0, 0))],
        out_shape=[jax.ShapeDtypeStruct((m, d), F32), jax.ShapeDtypeStruct((1, LANES), F32),
                   jax.ShapeDtypeStruct((1, d), F32)],
        compiler_params=_params(("arbitrary",)),
    )(h, gain, target)


def _rope_tables(s):
    t = jnp.arange(s)
    row = (t // GRID_W).astype(F32)
    col = (t % GRID_W).astype(F32)
    axis_dim = HEAD_DIM // 2
    inv_freq = ROPE_THETA ** (-jnp.arange(0, axis_dim, 2, dtype=F32) / axis_dim)
    ar, ac = row[:, None] * inv_freq, col[:, None] * inv_freq
    cos = jnp.concatenate([jnp.cos(ar), jnp.cos(ar), jnp.cos(ac), jnp.cos(ac)], axis=-1)
    sin = jnp.concatenate([-jnp.sin(ar), jnp.sin(ar), -jnp.sin(ac), jnp.sin(ac)], axis=-1)
    return jnp.tile(cos, (1, 2)), jnp.tile(sin, (1, 2))


def _swap16(x):
    lane = lax.broadcasted_iota(jnp.int32, x.shape, 1)
    return jnp.where((lane % 32) < 16, pltpu.roll(x, LANES - 16, 1), pltpu.roll(x, 16, 1))


def _head_mean(v):
    lane = lax.broadcasted_iota(jnp.int32, v.shape, 1)
    lo = lane < HEAD_DIM
    s_all = jnp.sum(v, axis=-1, keepdims=True)
    s_lo = jnp.sum(jnp.where(lo, v, 0.0), axis=-1, keepdims=True)
    return jnp.where(lo, s_lo, s_all - s_lo) * (1.0 / HEAD_DIM)


def _head_rstd(x):
    return lax.rsqrt(_head_mean(x * x) + RMS_EPS)


def _norm_rope(x, r, gain2, cos, sin):
    nrm = x * r * gain2
    return nrm * cos + _swap16(nrm) * sin


def _norm_rope_bwd(dy, x, r, gain2, cos, sin):
    dn = dy * cos + _swap16(dy * sin)
    xh = x * r
    dgain = jnp.sum(dn * xh, axis=0, keepdims=True)
    u = dn * gain2
    return r * (u - xh * _head_mean(u * xh)), dgain


def _a_prep(qkv, cos, sin, gq2, gk2):
    s = qkv.shape[0]
    tr = 256
    nq, nk = A_HEADS * HEAD_DIM, A_KV * HEAD_DIM

    def body(qkv_ref, cos_ref, sin_ref, gq_ref, gk_ref, qt_ref, k_ref, v_ref):
        cos_v, sin_v = cos_ref[...], sin_ref[...]
        rstd = [_head_rstd(qkv_ref[:, c * LANES:(c + 1) * LANES]) for c in range((nq + nk) // LANES)]
        for c in range(nq // LANES):
            y = _norm_rope(qkv_ref[:, c * LANES:(c + 1) * LANES], rstd[c], gq_ref[...], cos_v, sin_v)
            yt = (y * (QK_SCALE * LOG2E)).T
            qt_ref[2 * c] = yt[:HEAD_DIM].astype(BF16)
            qt_ref[2 * c + 1] = yt[HEAD_DIM:].astype(BF16)
        for c in range(nk // LANES):
            y = _norm_rope(qkv_ref[:, nq + c * LANES:nq + (c + 1) * LANES], rstd[nq // LANES + c], gk_ref[...],
                           cos_v, sin_v)
            k_ref[2 * c] = y[:, :HEAD_DIM].astype(BF16)
            k_ref[2 * c + 1] = y[:, HEAD_DIM:].astype(BF16)
            x = qkv_ref[:, nq + nk + c * LANES:nq + nk + (c + 1) * LANES]
            v_ref[2 * c] = x[:, :HEAD_DIM].astype(BF16)
            v_ref[2 * c + 1] = x[:, HEAD_DIM:].astype(BF16)

    return pl.pallas_call(
        body, name="a_prep", grid=(s // tr,),
        in_specs=[pl.BlockSpec((tr, nq + 2 * nk), lambda i: (i, 0)), pl.BlockSpec((tr, LANES), lambda i: (i, 0)),
                  pl.BlockSpec((tr, LANES), lambda i: (i, 0)), pl.BlockSpec((1, LANES), lambda i: (0, 0)),
                  pl.BlockSpec((1, LANES), lambda i: (0, 0))],
        out_specs=[pl.BlockSpec((A_HEADS, HEAD_DIM, tr), lambda i: (0, 0, i)),
                   pl.BlockSpec((A_KV, tr, HEAD_DIM), lambda i: (0, i, 0)),
                   pl.BlockSpec((A_KV, tr, HEAD_DIM), lambda i: (0, i, 0))],
        out_shape=[jax.ShapeDtypeStruct((A_HEADS, HEAD_DIM, s), BF16), jax.ShapeDtypeStruct((A_KV, s, HEAD_DIM), BF16),
                   jax.ShapeDtypeStruct((A_KV, s, HEAD_DIM), BF16)],
        compiler_params=_params(("parallel",)),
    )(qkv, cos, sin, gq2, gk2)


def _a_prep_bwd(dqt, dkt, dvt, qkv, cos, sin, gq2, gk2):
    s = qkv.shape[0]
    tr = 256
    nq, nk = A_HEADS * HEAD_DIM, A_KV * HEAD_DIM

    def body(dqt_ref, dkt_ref, dvt_ref, qkv_ref, cos_ref, sin_ref, gq_ref, gk_ref, o_ref, dgq_ref, dgk_ref):
        @pl.when(pl.program_id(0) == 0)
        def _():
            dgq_ref[...] = jnp.zeros_like(dgq_ref)
            dgk_ref[...] = jnp.zeros_like(dgk_ref)

        cos_v, sin_v = cos_ref[...], sin_ref[...]

        def pair(ref, c):
            return jnp.concatenate([ref[2 * c], ref[2 * c + 1]], axis=0).T

        rstd = [_head_rstd(qkv_ref[:, c * LANES:(c + 1) * LANES]) for c in range((nq + nk) // LANES)]
        dgq = jnp.zeros((1, LANES), F32)
        for c in range(nq // LANES):
            dx, dg = _norm_rope_bwd(pair(dqt_ref, c) * QK_SCALE, qkv_ref[:, c * LANES:(c + 1) * LANES], rstd[c],
                                    gq_ref[...], cos_v, sin_v)
            o_ref[:, c * LANES:(c + 1) * LANES] = dx.astype(BF16)
            dgq = dgq + dg
        dgq_ref[...] += dgq
        dgk = jnp.zeros((1, LANES), F32)
        for c in range(nk // LANES):
            lo = nq + c * LANES
            dx, dg = _norm_rope_bwd(pair(dkt_ref, c) * LN2, qkv_ref[:, lo:lo + LANES], rstd[nq // LANES + c],
                                    gk_ref[...], cos_v, sin_v)
            o_ref[:, lo:lo + LANES] = dx.astype(BF16)
            dgk = dgk + dg
            o_ref[:, lo + nk:lo + nk + LANES] = pair(dvt_ref, c).astype(BF16)
        dgk_ref[...] += dgk

    return pl.pallas_call(
        body, name="a_prep_bwd", grid=(s // tr,),
        in_specs=[pl.BlockSpec((A_HEADS, HEAD_DIM, tr), lambda i: (0, 0, i)),
                  pl.BlockSpec((A_KV, HEAD_DIM, tr), lambda i: (0, 0, i)),
                  pl.BlockSpec((A_KV, HEAD_DIM, tr), lambda i: (0, 0, i)),
                  pl.BlockSpec((tr, nq + 2 * nk), lambda i: (i, 0)), pl.BlockSpec((tr, LANES), lambda i: (i, 0)),
                  pl.BlockSpec((tr, LANES), lambda i: (i, 0)), pl.BlockSpec((1, LANES), lambda i: (0, 0)),
                  pl.BlockSpec((1, LANES), lambda i: (0, 0))],
        out_specs=[pl.BlockSpec((tr, nq + 2 * nk), lambda i: (i, 0)), pl.BlockSpec((1, LANES), lambda i: (0, 0)),
                   pl.BlockSpec((1, LANES), lambda i: (0, 0))],
        out_shape=[jax.ShapeDtypeStruct((s, nq + 2 * nk), BF16), jax.ShapeDtypeStruct((1, LANES), F32),
                   jax.ShapeDtypeStruct((1, LANES), F32)],
        compiler_params=_params(("arbitrary",)),
    )(dqt, dkt, dvt, qkv, cos, sin, gq2, gk2)


A_TQ = 2048
A_TQ_SUB = 256
A_TQ_BWD = 1024
A_KEY_CHUNK = 512


def _a_attn_fwd(qt, k, v, cargo, *, name):
    nh, _, s = qt.shape
    rep = nh // k.shape[0]
    tq = min(A_TQ, s)
    sub = min(A_TQ_SUB, tq)
    grid = (nh, s // tq)

    def body(qt_ref, k_ref, v_ref, o_ref, lse_ref):
        scores = [jnp.dot(k_ref[0], qt_ref[0, :, a:a + sub], preferred_element_type=F32)
                  for a in range(0, tq, sub)]
        for a, st in zip(range(0, tq, sub), scores):
            mx = jnp.max(st, axis=0, keepdims=True)
            p = jnp.exp2(st - mx)
            den = jnp.sum(p, axis=0, keepdims=True)
            ot = lax.dot_general(v_ref[0], p.astype(BF16), (((0,), (0,)), ((), ())), preferred_element_type=F32)
            o_ref[0, :, a:a + sub] = (ot / den).astype(BF16)
            lse_ref[0, :, a:a + sub] = mx + jnp.log(den) * LOG2E

    carried = _carry(cargo, grid, 3, 2, body)
    res = pl.pallas_call(
        carried.body, name=name, grid=grid,
        in_specs=[pl.BlockSpec((1, HEAD_DIM, tq), lambda h, i: (h, 0, i)),
                  pl.BlockSpec((1, s, HEAD_DIM), lambda h, i: (h // rep, 0, 0)),
                  pl.BlockSpec((1, s, HEAD_DIM), lambda h, i: (h // rep, 0, 0))] + carried.in_specs,
        out_specs=[pl.BlockSpec((1, HEAD_DIM, tq), lambda h, i: (h, 0, i)),
                   pl.BlockSpec((1, 1, tq), lambda h, i: (h, 0, i))] + carried.out_specs,
        out_shape=[jax.ShapeDtypeStruct((nh, HEAD_DIM, s), BF16), jax.ShapeDtypeStruct((nh, 1, s), F32)]
        + carried.out_shape,
        scratch_shapes=carried.scratch,
        compiler_params=_params(("arbitrary", "arbitrary")),
    )(qt, k, v, *carried.args)
    return res[0], res[1], res[2:]


def _a_attn_bwd(qt, k, v, dot, ot, lse, cargo, *, name):
    nh, _, s = qt.shape
    nkv = k.shape[0]
    rep = nh // nkv
    tq, ck = min(A_TQ_BWD, s), min(A_KEY_CHUNK, s)
    grid = (nh, s // tq)

    def body(qt_ref, k_ref, v_ref, dot_ref, ot_ref, lse_ref, dq_ref, dk_ref, dv_ref):
        h, i = pl.program_id(0), pl.program_id(1)

        @pl.when((h % rep == 0) & (i == 0))
        def _():
            dk_ref[...] = jnp.zeros_like(dk_ref)
            dv_ref[...] = jnp.zeros_like(dv_ref)

        q_t, do_t, lse_v = qt_ref[0], dot_ref[0], lse_ref[0]
        delta = jnp.sum(do_t.astype(F32) * ot_ref[0].astype(F32), axis=0, keepdims=True)
        nt = (((1,), (1,)), ((), ()))
        dq = jnp.zeros((HEAD_DIM, tq), F32)
        for c in range(s // ck):
            keys = slice(c * ck, (c + 1) * ck)
            kc = k_ref[0, keys, :]
            p = jnp.exp2(jnp.dot(kc, q_t, preferred_element_type=F32) - lse_v)
            dp = jnp.dot(v_ref[0, keys, :], do_t, preferred_element_type=F32)
            ds = (p * (dp - delta)).astype(BF16)
            dv_ref[0, :, keys] += lax.dot_general(do_t, p.astype(BF16), nt, preferred_element_type=F32)
            dk_ref[0, :, keys] += lax.dot_general(q_t, ds, nt, preferred_element_type=F32)
            dq = dq + lax.dot_general(kc, ds, (((0,), (0,)), ((), ())), preferred_element_type=F32)
        dq_ref[0] = dq

    blk_q = pl.BlockSpec((1, HEAD_DIM, tq), lambda h, i: (h, 0, i))
    blk_row = pl.BlockSpec((1, 1, tq), lambda h, i: (h, 0, i))
    blk_kv = pl.BlockSpec((1, s, HEAD_DIM), lambda h, i: (h // rep, 0, 0))
    blk_acc = pl.BlockSpec((1, HEAD_DIM, s), lambda h, i: (h // rep, 0, 0))
    carried = _carry(cargo, grid, 6, 3, body)
    res = pl.pallas_call(
        carried.body, name=name, grid=grid,
        in_specs=[blk_q, blk_kv, blk_kv, blk_q, blk_q, blk_row] + carried.in_specs,
        out_specs=[blk_q, blk_acc, blk_acc] + carried.out_specs,
        out_shape=[jax.ShapeDtypeStruct((nh, HEAD_DIM, s), F32), jax.ShapeDtypeStruct((nkv, HEAD_DIM, s), F32),
                   jax.ShapeDtypeStruct((nkv, HEAD_DIM, s), F32)] + carried.out_shape,
        scratch_shapes=carried.scratch,
        compiler_params=_params(("arbitrary", "arbitrary")),
    )(qt, k, v, dot, ot, lse, *carried.args)
    return res[0], res[1], res[2], res[3:]


WIN_FAR = 1e30


class _Band(NamedTuple):
    window: int
    dil: int
    seg: int
    stride: int

    @property
    def reach(self):
        return -(-self.window // WIN_REACH) * WIN_REACH


def _win_start(i, tq, tk, s, reach):
    return pl.multiple_of(jnp.clip(i * tq - reach, 0, s - tk), LANES)


def _win_penalty(i, start, tq, tk, band):
    qpos = i * tq + lax.broadcasted_iota(jnp.int32, (tk, tq), 1)
    kpos = start + lax.broadcasted_iota(jnp.int32, (tk, tq), 0)
    dist = jnp.abs(kpos - qpos)
    seg_lo = qpos - (qpos & (band.seg - 1))
    valid = (dist <= band.window) & (kpos >= seg_lo) & (kpos < seg_lo + band.seg)
    if band.stride > 1:
        valid &= (dist & (band.stride - 1)) == 0
    return jnp.where(valid, (dist * band.dil).astype(F32), WIN_FAR)


def _win_scores(kw_t, q_t, slope, pen):
    st = lax.dot_general(kw_t, q_t, (((0,), (0,)), ((), ())), preferred_element_type=F32)
    return st * (QK_SCALE * LOG2E) - (slope * LOG2E) * pen


def _win_tq(s):
    return min(512, s)


def _win_fwd(qt, ktp, vtp, slopes, sinks, *, band, out_dtype, name):
    nh, _, s = qt.shape
    nkv = ktp.shape[0]
    rep = nh // nkv
    tq = _win_tq(s)
    tk = min(tq + 2 * band.reach, s)

    def body(*refs):
        qt_ref, kt_ref, vt_ref, sl_ref = refs[:4]
        o_ref, lse_ref, pen_ref = refs[-3:]
        i, kv = pl.program_id(0), pl.program_id(1)
        start = _win_start(i, tq, tk, s, band.reach)

        @pl.when(kv == 0)
        def _():
            pen_ref[...] = _win_penalty(i, start, tq, tk, band)

        win = pl.ds(start, tk)
        kw_t, vw_t, pen = kt_ref[0, :, win], vt_ref[0, :, win], pen_ref[...]
        scores = [_win_scores(kw_t, qt_ref[g], sl_ref[g][:, :1], pen) for g in range(rep)]
        for g, st in enumerate(scores):
            mx = jnp.max(st, axis=0, keepdims=True)
            if sinks is not None:
                sink = refs[4][g][:, :1] * LOG2E
                mx = jnp.maximum(mx, sink)
            p = jnp.exp2(st - mx)
            den = jnp.sum(p, axis=0, keepdims=True)
            if sinks is not None:
                den = den + jnp.exp2(sink - mx)
            ot = jnp.dot(vw_t, p.astype(BF16), preferred_element_type=F32)
            o_ref[g] = (ot / den).astype(o_ref.dtype)
            lse_ref[g] = mx * LN2 + jnp.log(den)

    blk_q = pl.BlockSpec((rep, HEAD_DIM, tq), lambda i, kv: (kv, 0, i))
    blk_kv = pl.BlockSpec((1, HEAD_DIM, s), lambda i, kv: (kv, 0, 0))
    blk_h = pl.BlockSpec((rep, 1, LANES), lambda i, kv: (kv, 0, 0))
    in_specs, args = [blk_q, blk_kv, blk_kv, blk_h], [qt, ktp, vtp, slopes]
    if sinks is not None:
        in_specs.append(blk_h)
        args.append(sinks)
    return pl.pallas_call(
        body, name=name, grid=(s // tq, nkv), in_specs=in_specs,
        out_specs=[blk_q, pl.BlockSpec((rep, 1, tq), lambda i, kv: (kv, 0, i))],
        out_shape=[jax.ShapeDtypeStruct((nh, HEAD_DIM, s), out_dtype), jax.ShapeDtypeStruct((nh, 1, s), F32)],
        scratch_shapes=[pltpu.VMEM((tk, tq), F32)],
        compiler_params=_params(("arbitrary", "arbitrary")),
    )(*args)


def _win_bwd(qt, ktp, vtp, slopes, sinks, dot, ot, delta, *, band, name):
    nh, _, s = qt.shape
    nkv = ktp.shape[0]
    rep = nh // nkv
    tq = _win_tq(s)
    tk = min(tq + 2 * band.reach, s)
    n_in = 6 + (sinks is not None)

    def body(*refs):
        qt_ref, kt_ref, vt_ref, sl_ref, dot_ref, aux_ref = refs[:6]
        outs, pen_ref = refs[n_in:-1], refs[-1]
        dq_ref, dk_ref, dv_ref = outs[:3]
        i, kv = pl.program_id(0), pl.program_id(1)

        @pl.when((i == 0) & (kv == 0))
        def _():
            dk_ref[...] = jnp.zeros_like(dk_ref)
            dv_ref[...] = jnp.zeros_like(dv_ref)
            if sinks is not None:
                outs[3][...] = jnp.zeros_like(outs[3])

        start = _win_start(i, tq, tk, s, band.reach)

        @pl.when(kv == 0)
        def _():
            pen_ref[...] = _win_penalty(i, start, tq, tk, band)

        win = pl.ds(start, tk)
        kw_t, vw_t, pen = kt_ref[0, :, win], vt_ref[0, :, win], pen_ref[...]
        nt = (((1,), (1,)), ((), ()))
        dk_acc = jnp.zeros((HEAD_DIM, tk), F32)
        dv_acc = jnp.zeros((HEAD_DIM, tk), F32)
        products = [(_win_scores(kw_t, qt_ref[g], sl_ref[g][:, :1], pen),
                     lax.dot_general(vw_t, dot_ref[g], (((0,), (0,)), ((), ())), preferred_element_type=F32))
                    for g in range(rep)]
        for g, (st, dp) in enumerate(products):
            q_t, do_t = qt_ref[g], dot_ref[g]
            mx = jnp.max(st, axis=0, keepdims=True)
            if sinks is not None:
                sink = refs[6][g][:, :1] * LOG2E
                mx = jnp.maximum(mx, sink)
            p = jnp.exp2(st - mx)
            den = jnp.sum(p, axis=0, keepdims=True)
            if sinks is not None:
                p_sink = jnp.exp2(sink - mx)
                den = den + p_sink
            p = p / den
            if delta is None:
                row = jnp.sum(do_t.astype(F32) * aux_ref[g].astype(F32), axis=0, keepdims=True)
            else:
                row = aux_ref[g]
            ds = (p * (dp - row) * QK_SCALE).astype(BF16)
            dv_acc = dv_acc + lax.dot_general(do_t, p.astype(BF16), nt, preferred_element_type=F32)
            dk_acc = dk_acc + lax.dot_general(q_t, ds, nt, preferred_element_type=F32)
            dq_ref[g] = jnp.dot(kw_t, ds, preferred_element_type=F32)
            if sinks is not None:
                outs[3][kv * rep + g] += (jnp.zeros((1, LANES), F32)
                                          - jnp.sum(p_sink / den * row, axis=1, keepdims=True))
        dv_ref[kv, :, win] += dv_acc
        dk_ref[kv, :, win] += dk_acc

    blk_q = pl.BlockSpec((rep, HEAD_DIM, tq), lambda i, kv: (kv, 0, i))
    blk_row = pl.BlockSpec((rep, 1, tq), lambda i, kv: (kv, 0, i))
    blk_kv = pl.BlockSpec((1, HEAD_DIM, s), lambda i, kv: (kv, 0, 0))
    blk_acc = pl.BlockSpec((nkv, HEAD_DIM, s), lambda i, kv: (0, 0, 0))
    blk_h = pl.BlockSpec((rep, 1, LANES), lambda i, kv: (kv, 0, 0))
    in_specs = [blk_q, blk_kv, blk_kv, blk_h, blk_q, blk_q if delta is None else blk_row]
    args = [qt, ktp, vtp, slopes, dot, ot if delta is None else delta]
    out_specs = [blk_q, blk_acc, blk_acc]
    out_shape = [jax.ShapeDtypeStruct((nh, HEAD_DIM, s), F32), jax.ShapeDtypeStruct((nkv, HEAD_DIM, s), F32),
                 jax.ShapeDtypeStruct((nkv, HEAD_DIM, s), F32)]
    if sinks is not None:
        in_specs.append(blk_h)
        args.append(sinks)
        out_specs.append(pl.BlockSpec((nh, 1, LANES), lambda i, h: (0, 0, 0)))
        out_shape.append(jax.ShapeDtypeStruct((nh, 1, LANES), F32))
    res = pl.pallas_call(
        body, name=name, grid=(s // tq, nkv), in_specs=in_specs, out_specs=out_specs, out_shape=out_shape,
        scratch_shapes=[pltpu.VMEM((tk, tq), F32)],
        compiler_params=_params(("arbitrary", "arbitrary")),
    )(*args)
    return res if sinks is not None else (*res, None)


def _group_weights(lse):
    e = jnp.exp(lse - jnp.max(lse, axis=0, keepdims=True))
    return e / jnp.sum(e, axis=0, keepdims=True)


def _b_combine_fwd(ot, lse):
    nh, _, s = ot.shape
    ng, hg, _ = lse.shape
    ts = min(512, s)

    def body(ot_ref, lse_ref, o_ref):
        alpha = _group_weights(lse_ref[...])
        for g in range(ng):
            for j in range(hg):
                o_ref[g * hg + j] = (ot_ref[g * hg + j] * alpha[g, j:j + 1, :]).astype(BF16)

    return pl.pallas_call(
        body, name="b_combine_fwd", grid=(s // ts,),
        in_specs=[pl.BlockSpec((nh, HEAD_DIM, ts), lambda i: (0, 0, i)), pl.BlockSpec((ng, hg, ts), lambda i: (0, 0, i))],
        out_specs=pl.BlockSpec((nh, HEAD_DIM, ts), lambda i: (0, 0, i)),
        out_shape=jax.ShapeDtypeStruct((nh, HEAD_DIM, s), BF16),
        compiler_params=_params(("parallel",)),
    )(ot, lse)


def _b_combine_bwd(dout, ot, lse):
    nh, _, s = ot.shape
    ng, hg, _ = lse.shape
    ts = min(512, s)

    def body(dout_ref, ot_ref, lse_ref, do_ref, delta_ref):
        alpha = _group_weights(lse_ref[...])
        for j in range(hg):
            e = [jnp.sum(dout_ref[g * hg + j].astype(F32) * ot_ref[g * hg + j], axis=0, keepdims=True)
                 for g in range(ng)]
            a = [alpha[g, j:j + 1, :] for g in range(ng)]
            mix = a[0] * e[0]
            for g in range(1, ng):
                mix = mix + a[g] * e[g]
            for g in range(ng):
                do_ref[g * hg + j] = (dout_ref[g * hg + j].astype(F32) * a[g]).astype(BF16)
                delta_ref[g * hg + j] = a[g] * mix

    blk = pl.BlockSpec((nh, HEAD_DIM, ts), lambda i: (0, 0, i))
    return pl.pallas_call(
        body, name="b_combine_bwd", grid=(s // ts,),
        in_specs=[blk, blk, pl.BlockSpec((ng, hg, ts), lambda i: (0, 0, i))],
        out_specs=[blk, pl.BlockSpec((nh, 1, ts), lambda i: (0, 0, i))],
        out_shape=[jax.ShapeDtypeStruct((nh, HEAD_DIM, s), BF16), jax.ShapeDtypeStruct((nh, 1, s), F32)],
        compiler_params=_params(("parallel",)),
    )(dout, ot, lse)


def _alibi_slopes(n):
    return 2.0 ** (-8.0 * jnp.arange(1, n + 1, dtype=F32) / n)


def _per_head(v):
    return jnp.broadcast_to(v.astype(F32)[:, None, None], (v.shape[0], 1, LANES))


def _dilate(x, dil):
    if dil == 1:
        return x
    s = x.shape[-1]
    return jnp.swapaxes(x.reshape(x.shape[:-1] + (s // dil, dil)), -1, -2).reshape(x.shape)


def _undilate(x, dil):
    if dil == 1:
        return x
    s = x.shape[-1]
    return jnp.swapaxes(x.reshape(x.shape[:-1] + (dil, s // dil)), -1, -2).reshape(x.shape)


def _heads(x_t):
    return x_t.reshape(-1, HEAD_DIM, x_t.shape[-1])


B_MAX_STRIDE = 4


def _b_band(window, dilation, s):
    if dilation <= B_MAX_STRIDE:
        return _Band(window // 2, 1, s, dilation), 1
    return _Band(window // 2 // dilation, dilation, s // dilation, 1), dilation


def _mixer_fwd(kind, qkv, p, tabs, cargo, layer):
    s = qkv.shape[0 if kind == 0 else 1]
    if kind == 0:
        qt, k, v = _a_prep(qkv, tabs[0], tabs[1], p["gq2"], p["gk2"])
        ot, lse, brought = _a_attn_fwd(qt, k, v, cargo, name=f"a_attn_fwd_l{layer}")
        return ot.reshape(-1, s), dict(qt=qt, k=k, v=v, ot=ot, lse=lse), brought
    assert cargo is None
    if kind == 2:
        nq, nk = C_HEADS * HEAD_DIM, C_KV * HEAD_DIM
        qt = _heads(qkv[:nq])
        kp, vp = _heads(qkv[nq:nq + nk]), _heads(qkv[nq + nk:])
        ot, _ = _win_fwd(qt, kp, vp, p["slopes"], p["sinks"], band=_Band(C_WINDOW, 1, s, 1), out_dtype=BF16,
                         name="c_attn_fwd")
        return ot.reshape(-1, s), dict(qt=qt, kp=kp, vp=vp, ot=ot), ()
    ng, hg, kg = len(B_GROUPS), B_HEADS_PER_GROUP, B_KV_PER_GROUP
    nq, nk = ng * hg * HEAD_DIM, ng * kg * HEAD_DIM
    qt_all, kt_all, vt_all = _heads(qkv[:nq]), _heads(qkv[nq:nq + nk]), _heads(qkv[nq + nk:])
    saved, outs, lses = [], [], []
    for g, (window, dilation) in enumerate(B_GROUPS):
        band, dil = _b_band(window, dilation, s)
        qt = _dilate(qt_all[g * hg:(g + 1) * hg], dil)
        kp = _dilate(kt_all[g * kg:(g + 1) * kg], dil)
        vp = _dilate(vt_all[g * kg:(g + 1) * kg], dil)
        sl = p["slopes"][g * hg:(g + 1) * hg]
        ot, lse = _win_fwd(qt, kp, vp, sl, None, band=band, out_dtype=F32, name=f"b_attn_fwd_g{g}")
        saved.append(dict(qt=qt, kp=kp, vp=vp))
        outs.append(_undilate(ot, dil))
        lses.append(_undilate(lse[:, 0, :], dil))
    ot_all, lse_all = jnp.concatenate(outs, axis=0), jnp.stack(lses, axis=0)
    mixed = _b_combine_fwd(ot_all, lse_all)
    return mixed.reshape(-1, s), dict(groups=saved, ot=ot_all, lse=lse_all), ()


def _mixer_bwd(kind, do_t, qkv, sv, p, tabs, cargo, layer):
    s = do_t.shape[1]
    do_heads = _heads(do_t)
    small = {}
    if kind == 0:
        dqt, dkt, dvt, brought = _a_attn_bwd(sv["qt"], sv["k"], sv["v"], do_heads, sv["ot"], sv["lse"],
                                             cargo, name=f"a_attn_bwd_l{layer}")
        dqkv, dgq, dgk = _a_prep_bwd(dqt, dkt, dvt, qkv, tabs[0], tabs[1], p["gq2"], p["gk2"])
        small["q_gain"] = dgq[0, :HEAD_DIM] + dgq[0, HEAD_DIM:]
        small["k_gain"] = dgk[0, :HEAD_DIM] + dgk[0, HEAD_DIM:]
        return dqkv, small, brought
    assert cargo is None
    if kind == 2:
        dqt, dkt, dvt, dsink = _win_bwd(sv["qt"], sv["kp"], sv["vp"], p["slopes"], p["sinks"], do_heads,
                                        sv["ot"], None, band=_Band(C_WINDOW, 1, s, 1), name="c_attn_bwd")
        small["sinks"] = dsink[:, 0, 0]
        parts = [dqt.reshape(-1, s), dkt.reshape(-1, s), dvt.reshape(-1, s)]
        return jnp.concatenate(parts, axis=0).astype(BF16), small, ()
    ng, hg, kg = len(B_GROUPS), B_HEADS_PER_GROUP, B_KV_PER_GROUP
    do_own, delta = _b_combine_bwd(do_heads, sv["ot"], sv["lse"])
    dqs, dks, dvs = [], [], []
    for g, (window, dilation) in enumerate(B_GROUPS):
        band, dil = _b_band(window, dilation, s)
        gs = sv["groups"][g]
        dqt, dkt, dvt, _ = _win_bwd(gs["qt"], gs["kp"], gs["vp"], p["slopes"][g * hg:(g + 1) * hg], None,
                                    _dilate(do_own[g * hg:(g + 1) * hg], dil), None,
                                    _dilate(delta[g * hg:(g + 1) * hg], dil), band=band, name=f"b_attn_bwd_g{g}")
        dqs.append(_undilate(dqt, dil))
        dks.append(_undilate(dkt, dil))
        dvs.append(_undilate(dvt, dil))
    parts = [x.reshape(-1, s) for x in dqs + dks + dvs]
    return jnp.concatenate(parts, axis=0).astype(BF16), small, ()


LAYER_MATS = ("w_qkv", "w_o", "w1", "w2")
COLUMN_QUARTERS = ("w_qkv", "w1")


def _whole(key, gathered, layer):
    q, r, c = gathered.shape
    if key == "w1" or (key == "w_qkv" and layer % N_MIXERS == 0):
        return gathered
    if key in COLUMN_QUARTERS:
        return jnp.transpose(gathered, (1, 0, 2)).reshape(r, q * c)
    return gathered.reshape(q * r, c)


def _quarters(key, g):
    r, c = g.shape
    if key in COLUMN_QUARTERS:
        return jnp.transpose(g.reshape(r, 4, c // 4), (1, 0, 2))
    return g.reshape(4, r // 4, c)


def _local_step(x, target, norms, mixer_params, shards, whole=None):
    s = x.shape[0]
    tabs = _rope_tables(s)
    if whole is None:
        assert MIXER_OF_LAYER[0][0] == 0
        first = _run_cargo(_gather_cargo([shards[0][key] for key in LAYER_MATS]), name="gather_l0")
        mats = {0: {key: _whole(key, g, 0) for key, g in zip(LAYER_MATS, first)}}
        later = _gather_cargo([shards[layer][key] for layer in range(1, DEPTH) for key in LAYER_MATS])
    else:
        mats, later = dict(enumerate(whole)), None
    h = x
    saved = []
    for layer in range(DEPTH):
        kind = layer % N_MIXERS
        w, p = mats[layer], mixer_params[layer]
        hn, qkv = _norm_mm(h, norms["attn"][layer][None], w["w_qkv"], out_dtype=F32 if kind == 0 else BF16,
                           relu2=False, transpose_out=kind != 0, name=f"qkv_proj_l{layer}")
        o_t, sv, brought = _mixer_fwd(kind, qkv, p, tabs, later if layer == 0 else None, layer)
        for n, g in enumerate(brought):
            mats.setdefault(1 + n // len(LAYER_MATS), {})[LAYER_MATS[n % len(LAYER_MATS)]] = _whole(
                LAYER_MATS[n % len(LAYER_MATS)], g, 1 + n // len(LAYER_MATS))
        h_mid = _mm_res(o_t, w["w_o"], h, a_transposed=True, name=f"o_proj_l{layer}")
        hn2, act = _norm_mm(h_mid, norms["mlp"][layer][None], w["w1"], out_dtype=BF16, relu2=True,
                            transpose_out=False, name=f"mlp_up_l{layer}")
        h_out = _mm_res(act, w["w2"], h_mid, a_transposed=False, name=f"mlp_down_l{layer}")
        saved.append(dict(h=h, hn=hn, qkv=qkv, o_t=o_t, mix=sv, h_mid=h_mid, hn2=hn2, act=act))
        h = h_out

    dh, loss, d_final = _loss_head(h, norms["final"][None], target)

    own, received, pending = {}, {}, []
    d_attn, d_mlp, small = [None] * DEPTH, [None] * DEPTH, [None] * DEPTH
    for layer in reversed(range(DEPTH)):
        kind = layer % N_MIXERS
        w, p, sv = mats[layer], mixer_params[layer], saved[layer]
        du = _mm_nt(dh, w["w2"], sv["act"], transpose_out=False, name=f"mlp_down_bwd_l{layer}")
        own[layer, "w2"] = _quarters("w2", _mm_tn(sv["act"], dh, x_transposed=False, g_transposed=False,
                                                  column_quarters=False, name=f"mlp_w2_grad_l{layer}"))
        own[layer, "w1"] = _mm_tn(sv["hn2"], du, x_transposed=False, g_transposed=False, column_quarters=True,
                                  name=f"mlp_w1_grad_l{layer}")
        dh_mid, d_mlp[layer] = _mm_nt_normbwd(du, w["w1"], sv["h_mid"], norms["mlp"][layer][None], dh,
                                              g_transposed=False, name=f"mlp_up_bwd_l{layer}")
        do_t = _mm_nt(dh_mid, w["w_o"], None, transpose_out=True, name=f"o_proj_bwd_l{layer}")
        own[layer, "w_o"] = _quarters("w_o", _mm_tn(sv["o_t"], dh_mid, x_transposed=True, g_transposed=False,
                                                    column_quarters=False, name=f"w_o_grad_l{layer}"))
        pending += [(layer, "w2"), (layer, "w1"), (layer, "w_o")]
        cargo = None
        if kind == 0 and whole is None:
            cargo, sent, pending = _scatter_cargo([own[item] for item in pending], None), pending, []
        dqkv, small[layer], brought = _mixer_bwd(kind, do_t, sv["qkv"], sv["mix"], p, tabs, cargo, layer)
        if cargo is not None:
            received.update(zip(sent, brought))
        g_qkv = _mm_tn(sv["hn"], dqkv, x_transposed=False, g_transposed=kind != 0,
                       column_quarters=w["w_qkv"].ndim == 3, name=f"w_qkv_grad_l{layer}")
        own[layer, "w_qkv"] = g_qkv if g_qkv.ndim == 3 else _quarters("w_qkv", g_qkv)
        pending.append((layer, "w_qkv"))
        dh, d_attn[layer] = _mm_nt_normbwd(dqkv, w["w_qkv"], sv["h"], norms["attn"][layer][None], dh_mid,
                                           g_transposed=kind != 0, name=f"qkv_proj_bwd_l{layer}")
    return loss, dh, own, received, pending, dict(attn=d_attn, mlp=d_mlp, final=d_final, mixer=small)


CHIP_FLIPS = ((1, 0), (0, 1), (1, 1))


class _Cargo(NamedTuple):
    ins: tuple
    out_shape: tuple
    sem_shapes: tuple
    start: Callable
    wait: Callable


class _Carried(NamedTuple):
    body: Callable
    in_specs: list
    out_specs: list
    out_shape: list
    scratch: list
    args: tuple


def _carry(cargo, grid, n_in, n_out, body):
    if cargo is None:
        return _Carried(body, [], [], [], [], ())
    ci, co = len(cargo.ins), len(cargo.out_shape)

    def wrapped(*refs):
        ins, c_ins = refs[:n_in], refs[n_in:n_in + ci]
        outs, c_outs = refs[n_in + ci:n_in + ci + n_out], refs[n_in + ci + n_out:n_in + ci + n_out + co]
        sems = refs[n_in + ci + n_out + co:]
        first = last = None
        for axis, extent in enumerate(grid):
            at = pl.program_id(axis)
            first = (at == 0) if first is None else first & (at == 0)
            last = (at == extent - 1) if last is None else last & (at == extent - 1)

        @pl.when(first)
        def _():
            cargo.start(c_ins, c_outs, sems)

        body(*ins, *outs)

        @pl.when(last)
        def _():
            cargo.wait(c_ins, c_outs, sems)

    return _Carried(wrapped, [ANY] * ci, [ANY] * co, list(cargo.out_shape), list(cargo.sem_shapes), tuple(cargo.ins))


def _run_cargo(cargo, *, name):
    ci, co = len(cargo.ins), len(cargo.out_shape)

    def body(*refs):
        cargo.start(refs[:ci], refs[ci:ci + co], refs[ci + co:])
        cargo.wait(refs[:ci], refs[ci:ci + co], refs[ci + co:])

    return pl.pallas_call(body, name=name, in_specs=[ANY] * ci, out_specs=[ANY] * co, out_shape=list(cargo.out_shape),
                          scratch_shapes=list(cargo.sem_shapes))(*cargo.ins)


def _other_chip(x, y, j):
    fx, fy = CHIP_FLIPS[j]
    return (1 - x if fx else x), (1 - y if fy else y)


def _gather_cargo(shards):
    n = len(shards)
    halves = [a.shape[0] // 2 for a in shards]

    def copies(ins, outs, sems):
        ici_send, ici_recv, d2d_send, d2d_recv, local_sems = sems
        x, y, c = lax.axis_index("x"), lax.axis_index("y"), lax.axis_index("c")
        me = 2 * x + y

        def half(t, which):
            return pl.ds(pl.multiple_of(which * halves[t], 16), halves[t])

        def over_ici(t, j, arriving):
            px, py = _other_chip(x, y, j)
            return pltpu.make_async_remote_copy(
                src_ref=ins[t].at[half(t, c)], dst_ref=outs[t].at[2 * px + py if arriving else me, half(t, c)],
                send_sem=ici_send.at[t, j], recv_sem=ici_recv.at[t, j], device_id=(px, py, c), device_id_type=MESH)

        def over_d2d(t, j, arriving):
            px, py = _other_chip(x, y, j)
            mine = outs[t].at[2 * px + py, half(t, c)]
            return pltpu.make_async_remote_copy(
                src_ref=mine, dst_ref=outs[t].at[2 * px + py, half(t, 1 - c)] if arriving else mine,
                send_sem=d2d_send.at[t, j], recv_sem=d2d_recv.at[t, j], device_id=(x, y, 1 - c), device_id_type=MESH)

        return over_ici, over_d2d, lambda t: pltpu.make_async_copy(ins[t], outs[t].at[me], local_sems.at[t])

    def start(ins, outs, sems):
        over_ici, _, own = copies(ins, outs, sems)
        for t in range(n):
            own(t).start()
            for j in range(len(CHIP_FLIPS)):
                over_ici(t, j, False).start()

    def wait(ins, outs, sems):
        over_ici, over_d2d, own = copies(ins, outs, sems)
        for t in range(n):
            for j in range(len(CHIP_FLIPS)):
                over_ici(t, j, True).wait_recv()
                over_d2d(t, j, False).start()
        for t in range(n):
            for j in range(len(CHIP_FLIPS)):
                over_d2d(t, j, True).wait_recv()
                over_d2d(t, j, False).wait_send()
                over_ici(t, j, False).wait_send()
            own(t).wait()

    dma = pltpu.SemaphoreType.DMA
    return _Cargo(tuple(shards), tuple(jax.ShapeDtypeStruct((4,) + a.shape, a.dtype) for a in shards),
                  (dma((n, 3)), dma((n, 3)), dma((n, 3)), dma((n, 3)), dma((n,))), start, wait)


def _scatter_cargo(grads, small):
    n = len(grads)

    def copies(ins, outs, sems):
        x, y, c = lax.axis_index("x"), lax.axis_index("y"), lax.axis_index("c")
        me = 4 * x + 2 * y + c

        def remote(t, j):
            px, py = _other_chip(x, y, j)
            return pltpu.make_async_remote_copy(
                src_ref=ins[t].at[2 * px + py], dst_ref=outs[t].at[j], send_sem=sems[0].at[t, j],
                recv_sem=sems[1].at[t, j], device_id=(px, py, c), device_id_type=MESH)

        def small_remote(r, arriving):
            fx, fy, fc = (r + 1) // 4, ((r + 1) // 2) % 2, (r + 1) % 2
            px, py, pc = (1 - x if fx else x), (1 - y if fy else y), (1 - c if fc else c)
            return pltpu.make_async_remote_copy(
                src_ref=ins[n], dst_ref=outs[n].at[4 * px + 2 * py + pc if arriving else me],
                send_sem=sems[2].at[r], recv_sem=sems[3].at[r], device_id=(px, py, pc), device_id_type=MESH)

        return remote, small_remote, lambda: pltpu.make_async_copy(ins[n], outs[n].at[me], sems[4])

    def start(ins, outs, sems):
        remote, small_remote, small_own = copies(ins, outs, sems)
        if small is not None:
            small_own().start()
            for r in range(7):
                small_remote(r, False).start()
        for t in range(n):
            for j in range(len(CHIP_FLIPS)):
                remote(t, j).start()

    def wait(ins, outs, sems):
        remote, small_remote, small_own = copies(ins, outs, sems)
        if small is not None:
            for r in range(7):
                small_remote(r, True).wait_recv()
                small_remote(r, False).wait_send()
            small_own().wait()
        for t in range(n):
            for j in range(len(CHIP_FLIPS)):
                remote(t, j).wait()

    dma = pltpu.SemaphoreType.DMA
    ins = tuple(grads) + (() if small is None else (small,))
    out_shape = tuple(jax.ShapeDtypeStruct((3,) + g.shape[1:], g.dtype) for g in grads)
    sem_shapes = (dma((n, 3)), dma((n, 3)))
    if small is not None:
        out_shape += (jax.ShapeDtypeStruct((8,) + small.shape, small.dtype),)
        sem_shapes += (dma((7,)), dma((7,)), dma(()))
    return _Cargo(ins, out_shape, sem_shapes, start, wait)


def _swap_cores(parts):
    n = len(parts)

    def body(*refs):
        ins, outs = refs[:n], refs[n:2 * n]
        send_sems, recv_sems = refs[2 * n:]
        peer = (lax.axis_index("x"), lax.axis_index("y"), 1 - lax.axis_index("c"))
        copies = [pltpu.make_async_remote_copy(src_ref=ins[t], dst_ref=outs[t], send_sem=send_sems.at[t],
                                               recv_sem=recv_sems.at[t], device_id=peer, device_id_type=MESH)
                  for t in range(n)]
        for cp in copies:
            cp.start()
        for cp in copies:
            cp.wait()

    return pl.pallas_call(
        body, name="swap_cores", in_specs=[ANY] * n, out_specs=[ANY] * n,
        out_shape=[jax.ShapeDtypeStruct(a.shape, a.dtype) for a in parts],
        scratch_shapes=[pltpu.SemaphoreType.DMA((n,)), pltpu.SemaphoreType.DMA((n,))],
    )(*parts)


def _rows_tile(r):
    return 256 if r % 256 == 0 else r


def _sum_quarters(own, recv, *, name):
    r, c = own.shape
    tr = _rows_tile(r)

    def body(own_ref, recv_ref, o_ref):
        acc = own_ref[...].astype(F32)
        for j in range(3):
            acc = acc + recv_ref[j].astype(F32)
        o_ref[...] = acc.astype(BF16)

    return pl.pallas_call(
        body, name=name, grid=(r // tr,),
        in_specs=[pl.BlockSpec((tr, c), lambda i: (i, 0)), pl.BlockSpec((3, tr, c), lambda i: (0, i, 0))],
        out_specs=pl.BlockSpec((tr, c), lambda i: (i, 0)),
        out_shape=jax.ShapeDtypeStruct((r, c), BF16),
        compiler_params=_params(("parallel",)),
    )(own, recv)


def _adamw(w, m, v, parts, *, name):
    r, c = w.shape
    tr = _rows_tile(r)
    c1, c2 = 1.0 - ADAM_B1 ** ADAM_STEP, 1.0 - ADAM_B2 ** ADAM_STEP
    n_parts = len(parts)

    def body(*refs):
        w_ref, m_ref, v_ref = refs[:3]
        g_ref, d_ref, nm_ref, nv_ref = refs[3 + n_parts:]
        terms = []
        for p_ref in refs[3:3 + n_parts]:
            terms += [p_ref[...]] if len(p_ref.shape) == 2 else [p_ref[j] for j in range(p_ref.shape[0])]
        g = terms[0].astype(F32)
        for term in terms[1:]:
            g = g + term.astype(F32)
        m_new = ADAM_B1 * m_ref[...] + (1.0 - ADAM_B1) * g
        v_new = ADAM_B2 * v_ref[...] + (1.0 - ADAM_B2) * (g * g)
        step = (m_new / c1) / (jnp.sqrt(v_new / c2) + ADAM_EPS)
        g_ref[...] = g
        d_ref[...] = -ADAM_LR * (step + ADAM_WD * w_ref[...])
        nm_ref[...] = m_new
        nv_ref[...] = v_new

    blk = pl.BlockSpec((tr, c), lambda i: (i, 0))
    part_specs = [blk if p.ndim == 2 else pl.BlockSpec((p.shape[0], tr, c), lambda i: (0, i, 0)) for p in parts]
    return pl.pallas_call(
        body, name=name, grid=(r // tr,), in_specs=[blk, blk, blk] + part_specs,
        out_specs=[blk] * 4, out_shape=[jax.ShapeDtypeStruct((r, c), F32)] * 4,
        compiler_params=_params(("parallel",)),
    )(w, m, v, *parts)


MATS = ("a_w_qkv", "a_w_o", "b_w_qkv", "b_w_o", "c_w_qkv", "c_w_o", "mlp_w1", "mlp_w2")
SMALLS = ("attn_norm", "mlp_norm", "a_q_gain", "a_k_gain", "c_sinks", "final_norm")
WEIGHTS = ("attn_norm", "mlp_norm", "a_w_qkv", "a_q_gain", "a_k_gain", "a_w_o", "b_w_qkv", "b_w_o", "c_w_qkv",
           "c_sinks", "c_w_o", "mlp_w1", "mlp_w2", "final_norm")
MIXER_OF_LAYER = tuple((layer % N_MIXERS, sum(1 for q in range(layer) if q % N_MIXERS == layer % N_MIXERS))
                       for layer in range(DEPTH))
SMALL_ROWS = 8


def _pack_small(values):
    rows, spans, at = [], [], 0
    for v in values:
        flat = v.reshape(-1)
        n = -(-flat.shape[0] // (SMALL_ROWS * LANES)) * SMALL_ROWS
        rows.append(jnp.pad(flat, (0, n * LANES - flat.shape[0])).reshape(n, LANES))
        spans.append((at, n))
        at += n
    return jnp.concatenate(rows, axis=0), spans


def kernel(x, attn_norm, mlp_norm, a_w_qkv, a_q_gain, a_k_gain, a_w_o, b_w_qkv, b_w_o, c_w_qkv, c_sinks, c_w_o, mlp_w1, mlp_w2, final_norm, loss_target, m_attn_norm, m_mlp_norm, m_a_w_qkv, m_a_q_gain, m_a_k_gain, m_a_w_o, m_b_w_qkv, m_b_w_o, m_c_w_qkv, m_c_sinks, m_c_w_o, m_mlp_w1, m_mlp_w2, m_final_norm, v_attn_norm, v_mlp_norm, v_a_w_qkv, v_a_q_gain, v_a_k_gain, v_a_w_o, v_b_w_qkv, v_b_w_o, v_c_w_qkv, v_c_sinks, v_c_w_o, v_mlp_w1, v_mlp_w2, v_final_norm):
    env = dict(locals())
    w = {name: env[name] for name in WEIGHTS}
    mom = {name: (env["m_" + name], env["v_" + name]) for name in WEIGHTS}

    prefix = ("a", "b", "c")
    shards, mixer_params = [], []
    for layer, (kind, j) in enumerate(MIXER_OF_LAYER):
        shards.append(dict(w_qkv=w[prefix[kind] + "_w_qkv"][j].astype(BF16), w_o=w[prefix[kind] + "_w_o"][j].astype(BF16),
                           w1=mlp_w1[layer].astype(BF16), w2=mlp_w2[layer].astype(BF16)))
        if kind == 0:
            mixer_params.append(dict(gq2=jnp.tile(a_q_gain[j], 2)[None], gk2=jnp.tile(a_k_gain[j], 2)[None]))
        elif kind == 1:
            mixer_params.append(dict(slopes=_per_head(_alibi_slopes(len(B_GROUPS) * B_HEADS_PER_GROUP))))
        else:
            mixer_params.append(dict(slopes=_per_head(_alibi_slopes(C_HEADS)), sinks=_per_head(c_sinks[j])))

    norms = dict(attn=attn_norm, mlp=mlp_norm, final=final_norm)
    loss_part, grad_x, own, received, pending, g_small = _local_step(x[0], loss_target[0], norms, mixer_params, shards)
    loss = lax.psum(loss_part[0, 0], ("x", "y", "c"))

    of_kind = lambda kind, key: jnp.stack([g_small["mixer"][layer][key] for layer, (k, _) in enumerate(MIXER_OF_LAYER)
                                           if k == kind])
    small_grads = dict(
        attn_norm=jnp.concatenate(g_small["attn"], axis=0), mlp_norm=jnp.concatenate(g_small["mlp"], axis=0),
        a_q_gain=of_kind(0, "q_gain"), a_k_gain=of_kind(0, "k_gain"), c_sinks=of_kind(2, "sinks"),
        final_norm=g_small["final"][0])
    packed, spans = _pack_small([small_grads[name] for name in SMALLS])
    *last, all_small = _run_cargo(_scatter_cargo([own[item] for item in pending], packed), name="scatter_last")
    received.update(zip(pending, last))

    me_chip = 2 * lax.axis_index("x") + lax.axis_index("y")
    partial = []
    for name in MATS:
        key = name[2:] if name[0] in "abc" else name[4:]
        layers = [layer for layer, (kind, _) in enumerate(MIXER_OF_LAYER)
                  if name.startswith("mlp") or prefix[kind] == name[0]]
        sums = [_sum_quarters(lax.dynamic_index_in_dim(own[layer, key], me_chip, axis=0, keepdims=False),
                              received[layer, key], name=f"sum_{name}_l{layer}") for layer in layers]
        partial.append(jnp.concatenate(sums, axis=0))
    other = _swap_cores(partial)

    out = {}
    for name, mine, theirs in zip(MATS, partial, other):
        shape = w[name].shape
        res = _adamw(*[a.reshape(-1, shape[-1]) for a in (w[name], *mom[name])], [mine, theirs], name=f"adamw_{name}")
        out[name] = [a.reshape(shape) for a in res]
    for name, (at, n) in zip(SMALLS, spans):
        shape = w[name].shape
        packed_in = [_pack_small([a])[0] for a in (w[name], *mom[name])]
        res = _adamw(*packed_in, [all_small[:, at:at + n]], name=f"adamw_{name}")
        out[name] = [a.reshape(-1)[:w[name].size].reshape(shape) for a in res]

    return (loss, grad_x[None], *[out[name][0] for name in WEIGHTS], *[out[name][1] for name in WEIGHTS],
            *[out[name][2] for name in WEIGHTS], *[out[name][3] for name in WEIGHTS])
```

```python
from typing import Callable, NamedTuple

import jax
import jax.numpy as jnp
from jax import lax
from jax.experimental import pallas as pl
from jax.experimental.pallas import tpu as pltpu

F32 = jnp.float32
BF16 = jnp.bfloat16
MESH = pl.DeviceIdType.MESH
ANY = pl.BlockSpec(memory_space=pl.ANY)

D_MODEL = 1024
HEAD_DIM = 64
GRID_W = 64
ROPE_THETA = 10000.0
RMS_EPS = 1e-6
QK_SCALE = HEAD_DIM ** -0.5
LOG2E = 1.4426950408889634
LN2 = 0.6931471805599453
A_HEADS, A_KV = 16, 4
B_GROUPS = ((128, 1), (512, 4), (2048, 16))
B_HEADS_PER_GROUP, B_KV_PER_GROUP = 6, 2
C_HEADS, C_KV, C_WINDOW = 16, 4, 128
DEPTH, N_MIXERS = 4, 3
ADAM_LR, ADAM_B1, ADAM_B2, ADAM_EPS, ADAM_WD, ADAM_STEP = 0.001, 0.9, 0.999, 1e-08, 0.01, 10

WIN_REACH = 128
V7X_VMEM_BUDGET = 48 * 1024 * 1024
LANES = 128
ROW_TILE = 1024


def _params(semantics):
    return pltpu.CompilerParams(dimension_semantics=semantics, vmem_limit_bytes=V7X_VMEM_BUDGET)


def _tile(n, cap):
    if n <= cap:
        return n
    t = (cap // LANES) * LANES
    while n % t:
        t -= LANES
    return t


def _norm_mm(h, gain, w, *, out_dtype, relu2, transpose_out, name):
    m, d = h.shape
    by_quarter = w.ndim == 3
    assert not (by_quarter and transpose_out)
    n = w.shape[-1] * (4 if by_quarter else 1)
    per_step = 2 if by_quarter and 2 * w.shape[-1] <= 2048 else 1
    tm, tn = min(ROW_TILE, m), (per_step * w.shape[-1] if by_quarter else _tile(n, 2048))
    w_spec = (pl.BlockSpec((per_step, d, tn // per_step), lambda i, j: (j, 0, 0)) if by_quarter
              else pl.BlockSpec((d, tn), lambda i, j: (0, j)))
    y_spec = (pl.BlockSpec((tn, tm), lambda i, j: (j, i)) if transpose_out
              else pl.BlockSpec((tm, tn), lambda i, j: (i, j)))

    def body(h_ref, g_ref, w_ref, hn_ref, y_ref):
        @pl.when(pl.program_id(1) == 0)
        def _():
            x = h_ref[...]
            r = lax.rsqrt(jnp.mean(x * x, axis=-1, keepdims=True) + RMS_EPS)
            hn_ref[...] = (x * r * g_ref[...]).astype(BF16)

        def finish(y):
            if relu2:
                y = jnp.maximum(y, 0.0)
                y = y * y
            return y.astype(y_ref.dtype)

        if transpose_out:
            y_ref[...] = finish(lax.dot_general(w_ref[...], hn_ref[...], (((0,), (1,)), ((), ())),
                                                preferred_element_type=F32))
        elif by_quarter:
            cols = tn // per_step
            for q in range(per_step):
                y_ref[:, q * cols:(q + 1) * cols] = finish(jnp.dot(hn_ref[...], w_ref[q], preferred_element_type=F32))
        else:
            y_ref[...] = finish(jnp.dot(hn_ref[...], w_ref[...], preferred_element_type=F32))

    return pl.pallas_call(
        body, name=name, grid=(m // tm, n // tn),
        in_specs=[pl.BlockSpec((tm, d), lambda i, j: (i, 0)), pl.BlockSpec((1, d), lambda i, j: (0, 0)), w_spec],
        out_specs=[pl.BlockSpec((tm, d), lambda i, j: (i, 0)), y_spec],
        out_shape=[jax.ShapeDtypeStruct((m, d), BF16), jax.ShapeDtypeStruct((n, m) if transpose_out else (m, n), out_dtype)],
        compiler_params=_params(("parallel", "arbitrary")),
    )(h, gain, w)


def _mm_res(a, w, h_in, *, a_transposed, name):
    k, d = w.shape
    m = h_in.shape[0]
    tm, tk = min(ROW_TILE, m), _tile(k, 2048)
    lhs_contracts = 0 if a_transposed else 1

    def body(a_ref, w_ref, h_ref, o_ref):
        @pl.when(pl.program_id(1) == 0)
        def _():
            o_ref[...] = h_ref[...]

        o_ref[...] += lax.dot_general(a_ref[...], w_ref[...], (((lhs_contracts,), (0,)), ((), ())),
                                      preferred_element_type=F32)

    a_spec = (pl.BlockSpec((tk, tm), lambda i, j: (j, i)) if a_transposed
              else pl.BlockSpec((tm, tk), lambda i, j: (i, j)))
    return pl.pallas_call(
        body, name=name, grid=(m // tm, k // tk),
        in_specs=[a_spec, pl.BlockSpec((tk, d), lambda i, j: (j, 0)), pl.BlockSpec((tm, d), lambda i, j: (i, 0))],
        out_specs=pl.BlockSpec((tm, d), lambda i, j: (i, 0)),
        out_shape=jax.ShapeDtypeStruct((m, d), F32),
        compiler_params=_params(("parallel", "arbitrary")),
    )(a, w, h_in)


def _mm_nt(a, w, act, *, transpose_out, name):
    m, d = a.shape
    n = w.shape[0]
    tm, tn = min(ROW_TILE, m), _tile(n, 1152)
    assert act is None or not transpose_out
    nt = (((1,), (1,)), ((), ()))

    def body(*refs):
        a_ref, w_ref = refs[0], refs[1]
        o_ref = refs[-1]
        if transpose_out:
            acc = lax.dot_general(w_ref[...], a_ref[...].astype(BF16), nt, preferred_element_type=F32)
        else:
            acc = lax.dot_general(a_ref[...].astype(BF16), w_ref[...], nt, preferred_element_type=F32)
        if act is not None:
            acc = acc * (2.0 * jnp.sqrt(refs[2][...].astype(F32)))
        o_ref[...] = acc.astype(BF16)

    in_specs = [pl.BlockSpec((tm, d), lambda i, j: (i, 0)), pl.BlockSpec((tn, d), lambda i, j: (j, 0))]
    args = [a, w]
    if act is not None:
        in_specs.append(pl.BlockSpec((tm, tn), lambda i, j: (i, j)))
        args.append(act)
    out_spec = (pl.BlockSpec((tn, tm), lambda i, j: (j, i)) if transpose_out
                else pl.BlockSpec((tm, tn), lambda i, j: (i, j)))
    return pl.pallas_call(
        body, name=name, grid=(m // tm, n // tn), in_specs=in_specs, out_specs=out_spec,
        out_shape=jax.ShapeDtypeStruct((n, m) if transpose_out else (m, n), BF16),
        compiler_params=_params(("parallel", "parallel")),
    )(*args)


def _rmsnorm_bwd(dn, x, gain):
    r = lax.rsqrt(jnp.mean(x * x, axis=-1, keepdims=True) + RMS_EPS)
    xh = x * r
    dgain = jnp.sum(dn * xh, axis=0, keepdims=True)
    u = dn * gain
    dx = r * (u - xh * jnp.mean(u * xh, axis=-1, keepdims=True))
    return dx, dgain


def _mm_nt_normbwd(g, w, h, gain, dh_in, *, g_transposed, name):
    m, k = g.shape[::-1] if g_transposed else g.shape
    by_quarter = w.ndim == 3
    d = w.shape[-2]
    tm, tk = min(ROW_TILE, m), (w.shape[-1] if by_quarter else _tile(k, 1024))
    sub = min(256, tm)
    nk = k // tk
    w_spec = (pl.BlockSpec((None, d, tk), lambda i, j: (j, 0, 0)) if by_quarter
              else pl.BlockSpec((d, tk), lambda i, j: (0, j)))

    def body(g_ref, w_ref, h_ref, gain_ref, dh_ref, o_ref, dg_ref, acc_ref):
        i, j = pl.program_id(0), pl.program_id(1)

        @pl.when((i == 0) & (j == 0))
        def _():
            dg_ref[...] = jnp.zeros_like(dg_ref)

        @pl.when(j == 0)
        def _():
            acc_ref[...] = jnp.zeros_like(acc_ref)

        acc_ref[...] += lax.dot_general(g_ref[...], w_ref[...], (((0 if g_transposed else 1,), (1,)), ((), ())),
                                        preferred_element_type=F32)

        @pl.when(j == nk - 1)
        def _():
            for r in range(0, tm, sub):
                rows = slice(r, r + sub)
                dx, dgain = _rmsnorm_bwd(acc_ref[rows, :], h_ref[rows, :], gain_ref[...])
                dg_ref[...] += dgain
                o_ref[rows, :] = dh_ref[rows, :] + dx

    return pl.pallas_call(
        body, name=name, grid=(m // tm, nk),
        in_specs=[pl.BlockSpec((tk, tm), lambda i, j: (j, i)) if g_transposed
                  else pl.BlockSpec((tm, tk), lambda i, j: (i, j)), w_spec,
                  pl.BlockSpec((tm, d), lambda i, j: (i, 0)), pl.BlockSpec((1, d), lambda i, j: (0, 0)),
                  pl.BlockSpec((tm, d), lambda i, j: (i, 0))],
        out_specs=[pl.BlockSpec((tm, d), lambda i, j: (i, 0)), pl.BlockSpec((1, d), lambda i, j: (0, 0))],
        out_shape=[jax.ShapeDtypeStruct((m, d), F32), jax.ShapeDtypeStruct((1, d), F32)],
        scratch_shapes=[pltpu.VMEM((tm, d), F32)],
        compiler_params=_params(("arbitrary", "arbitrary")),
    )(g, w, h, gain, dh_in)


def _mm_tn(x, g, *, x_transposed, g_transposed, column_quarters, name):
    k, m = x.shape if x_transposed else x.shape[::-1]
    n = g.shape[0] if g_transposed else g.shape[1]
    tm, tk, tn = min(2 * ROW_TILE, m), _tile(k, 1152), (n // 4 if column_quarters else _tile(n, 1024))
    nm = m // tm
    lhs_contracts, rhs_contracts = (1 if x_transposed else 0), (1 if g_transposed else 0)
    g_spec = (pl.BlockSpec((tn, tm), lambda a, b, s: (b, s)) if g_transposed
              else pl.BlockSpec((tm, tn), lambda a, b, s: (s, b)))

    def body(x_ref, g_ref, o_ref, acc_ref):
        s = pl.program_id(2)

        @pl.when(s == 0)
        def _():
            acc_ref[...] = jnp.zeros_like(acc_ref)

        acc_ref[...] += lax.dot_general(x_ref[...], g_ref[...].astype(BF16),
                                        (((lhs_contracts,), (rhs_contracts,)), ((), ())), preferred_element_type=F32)

        @pl.when(s == nm - 1)
        def _():
            o_ref[...] = acc_ref[...].astype(BF16)

    x_spec = (pl.BlockSpec((tk, tm), lambda a, b, s: (a, s)) if x_transposed
              else pl.BlockSpec((tm, tk), lambda a, b, s: (s, a)))
    out_spec = (pl.BlockSpec((None, tk, tn), lambda a, b, s: (b, a, 0)) if column_quarters
                else pl.BlockSpec((tk, tn), lambda a, b, s: (a, b)))
    return pl.pallas_call(
        body, name=name, grid=(k // tk, n // tn, nm),
        in_specs=[x_spec, g_spec], out_specs=out_spec,
        out_shape=jax.ShapeDtypeStruct((4, k, n // 4) if column_quarters else (k, n), BF16),
        scratch_shapes=[pltpu.VMEM((tk, tn), F32)],
        compiler_params=_params(("parallel", "parallel", "arbitrary")),
    )(x, g)


def _loss_head(h, gain, target):
    m, d = h.shape
    tm = 512

    def body(h_ref, g_ref, t_ref, dh_ref, loss_ref, dg_ref):
        @pl.when(pl.program_id(0) == 0)
        def _():
            loss_ref[...] = jnp.zeros_like(loss_ref)
            dg_ref[...] = jnp.zeros_like(dg_ref)

        x = h_ref[...]
        gain_v = g_ref[...]
        r = lax.rsqrt(jnp.mean(x * x, axis=-1, keepdims=True) + RMS_EPS)
        err = x * r * gain_v - t_ref[...]
        loss_ref[...] += 0.5 * jnp.sum(jnp.mean(err * err, axis=-1, keepdims=True), axis=0, keepdims=True)
        dx, dgain = _rmsnorm_bwd(err * (1.0 / d), x, gain_v)
        dg_ref[...] += dgain
        dh_ref[...] = dx

    return pl.pallas_call(
        body, name="loss_head", grid=(m // tm,),
        in_specs=[pl.BlockSpec((tm, d), lambda i: (i, 0)), pl.BlockSpec((1, d), lambda i: (0, 0)),
                  pl.BlockSpec((tm, d), lambda i: (i, 0))],
        out_specs=[pl.BlockSpec((tm, d), lambda i: (i, 0)), pl.BlockSpec((1, LANES), lambda i: (0, 0)),
                   pl.BlockSpec((1, d), lambda i: (0, 0))],
        out_shape=[jax.ShapeDtypeStruct((m, d), F32), jax.ShapeDtypeStruct((1, LANES), F32),
                   jax.ShapeDtypeStruct((1, d), F32)],
        compiler_params=_params(("arbitrary",)),
    )(h, gain, target)


def _rope_tables(s):
    t = jnp.arange(s)
    row = (t // GRID_W).astype(F32)
    col = (t % GRID_W).astype(F32)
    axis_dim = HEAD_DIM // 2
    inv_freq = ROPE_THETA ** (-jnp.arange(0, axis_dim, 2, dtype=F32) / axis_dim)
    ar, ac = row[:, None] * inv_freq, col[:, None] * inv_freq
    cos = jnp.concatenate([jnp.cos(ar), jnp.cos(ar), jnp.cos(ac), jnp.cos(ac)], axis=-1)
    sin = jnp.concatenate([-jnp.sin(ar), jnp.sin(ar), -jnp.sin(ac), jnp.sin(ac)], axis=-1)
    return jnp.tile(cos, (1, 2)), jnp.tile(sin, (1, 2))


def _swap16(x):
    lane = lax.broadcasted_iota(jnp.int32, x.shape, 1)
    return jnp.where((lane % 32) < 16, pltpu.roll(x, LANES - 16, 1), pltpu.roll(x, 16, 1))


def _head_mean(v):
    lane = lax.broadcasted_iota(jnp.int32, v.shape, 1)
    lo = lane < HEAD_DIM
    s_all = jnp.sum(v, axis=-1, keepdims=True)
    s_lo = jnp.sum(jnp.where(lo, v, 0.0), axis=-1, keepdims=True)
    return jnp.where(lo, s_lo, s_all - s_lo) * (1.0 / HEAD_DIM)


def _head_rstd(x):
    return lax.rsqrt(_head_mean(x * x) + RMS_EPS)


def _norm_rope(x, r, gain2, cos, sin):
    nrm = x * r * gain2
    return nrm * cos + _swap16(nrm) * sin


def _norm_rope_bwd(dy, x, r, gain2, cos, sin):
    dn = dy * cos + _swap16(dy * sin)
    xh = x * r
    dgain = jnp.sum(dn * xh, axis=0, keepdims=True)
    u = dn * gain2
    return r * (u - xh * _head_mean(u * xh)), dgain


def _a_prep(qkv, cos, sin, gq2, gk2):
    s = qkv.shape[0]
    tr = 256
    nq, nk = A_HEADS * HEAD_DIM, A_KV * HEAD_DIM

    def body(qkv_ref, cos_ref, sin_ref, gq_ref, gk_ref, qt_ref, k_ref, v_ref):
        cos_v, sin_v = cos_ref[...], sin_ref[...]
        rstd = [_head_rstd(qkv_ref[:, c * LANES:(c + 1) * LANES]) for c in range((nq + nk) // LANES)]
        for c in range(nq // LANES):
            y = _norm_rope(qkv_ref[:, c * LANES:(c + 1) * LANES], rstd[c], gq_ref[...], cos_v, sin_v)
            yt = (y * (QK_SCALE * LOG2E)).T
            qt_ref[2 * c] = yt[:HEAD_DIM].astype(BF16)
            qt_ref[2 * c + 1] = yt[HEAD_DIM:].astype(BF16)
        for c in range(nk // LANES):
            y = _norm_rope(qkv_ref[:, nq + c * LANES:nq + (c + 1) * LANES], rstd[nq // LANES + c], gk_ref[...],
                           cos_v, sin_v)
            k_ref[2 * c] = y[:, :HEAD_DIM].astype(BF16)
            k_ref[2 * c + 1] = y[:, HEAD_DIM:].astype(BF16)
            x = qkv_ref[:, nq + nk + c * LANES:nq + nk + (c + 1) * LANES]
            v_ref[2 * c] = x[:, :HEAD_DIM].astype(BF16)
            v_ref[2 * c + 1] = x[:, HEAD_DIM:].astype(BF16)

    return pl.pallas_call(
        body, name="a_prep", grid=(s // tr,),
        in_specs=[pl.BlockSpec((tr, nq + 2 * nk), lambda i: (i, 0)), pl.BlockSpec((tr, LANES), lambda i: (i, 0)),
                  pl.BlockSpec((tr, LANES), lambda i: (i, 0)), pl.BlockSpec((1, LANES), lambda i: (0, 0)),
                  pl.BlockSpec((1, LANES), lambda i: (0, 0))],
        out_specs=[pl.BlockSpec((A_HEADS, HEAD_DIM, tr), lambda i: (0, 0, i)),
                   pl.BlockSpec((A_KV, tr, HEAD_DIM), lambda i: (0, i, 0)),
                   pl.BlockSpec((A_KV, tr, HEAD_DIM), lambda i: (0, i, 0))],
        out_shape=[jax.ShapeDtypeStruct((A_HEADS, HEAD_DIM, s), BF16), jax.ShapeDtypeStruct((A_KV, s, HEAD_DIM), BF16),
                   jax.ShapeDtypeStruct((A_KV, s, HEAD_DIM), BF16)],
        compiler_params=_params(("parallel",)),
    )(qkv, cos, sin, gq2, gk2)


def _a_prep_bwd(dqt, dkt, dvt, qkv, cos, sin, gq2, gk2):
    s = qkv.shape[0]
    tr = 256
    nq, nk = A_HEADS * HEAD_DIM, A_KV * HEAD_DIM

    def body(dqt_ref, dkt_ref, dvt_ref, qkv_ref, cos_ref, sin_ref, gq_ref, gk_ref, o_ref, dgq_ref, dgk_ref):
        @pl.when(pl.program_id(0) == 0)
        def _():
            dgq_ref[...] = jnp.zeros_like(dgq_ref)
            dgk_ref[...] = jnp.zeros_like(dgk_ref)

        cos_v, sin_v = cos_ref[...], sin_ref[...]

        def pair(ref, c):
            return jnp.concatenate([ref[2 * c], ref[2 * c + 1]], axis=0).T

        rstd = [_head_rstd(qkv_ref[:, c * LANES:(c + 1) * LANES]) for c in range((nq + nk) // LANES)]
        dgq = jnp.zeros((1, LANES), F32)
        for c in range(nq // LANES):
            dx, dg = _norm_rope_bwd(pair(dqt_ref, c) * QK_SCALE, qkv_ref[:, c * LANES:(c + 1) * LANES], rstd[c],
                                    gq_ref[...], cos_v, sin_v)
            o_ref[:, c * LANES:(c + 1) * LANES] = dx.astype(BF16)
            dgq = dgq + dg
        dgq_ref[...] += dgq
        dgk = jnp.zeros((1, LANES), F32)
        for c in range(nk // LANES):
            lo = nq + c * LANES
            dx, dg = _norm_rope_bwd(pair(dkt_ref, c) * LN2, qkv_ref[:, lo:lo + LANES], rstd[nq // LANES + c],
                                    gk_ref[...], cos_v, sin_v)
            o_ref[:, lo:lo + LANES] = dx.astype(BF16)
            dgk = dgk + dg
            o_ref[:, lo + nk:lo + nk + LANES] = pair(dvt_ref, c).astype(BF16)
        dgk_ref[...] += dgk

    return pl.pallas_call(
        body, name="a_prep_bwd", grid=(s // tr,),
        in_specs=[pl.BlockSpec((A_HEADS, HEAD_DIM, tr), lambda i: (0, 0, i)),
                  pl.BlockSpec((A_KV, HEAD_DIM, tr), lambda i: (0, 0, i)),
                  pl.BlockSpec((A_KV, HEAD_DIM, tr), lambda i: (0, 0, i)),
                  pl.BlockSpec((tr, nq + 2 * nk), lambda i: (i, 0)), pl.BlockSpec((tr, LANES), lambda i: (i, 0)),
                  pl.BlockSpec((tr, LANES), lambda i: (i, 0)), pl.BlockSpec((1, LANES), lambda i: (0, 0)),
                  pl.BlockSpec((1, LANES), lambda i: (0, 0))],
        out_specs=[pl.BlockSpec((tr, nq + 2 * nk), lambda i: (i, 0)), pl.BlockSpec((1, LANES), lambda i: (0, 0)),
                   pl.BlockSpec((1, LANES), lambda i: (0, 0))],
        out_shape=[jax.ShapeDtypeStruct((s, nq + 2 * nk), BF16), jax.ShapeDtypeStruct((1, LANES), F32),
                   jax.ShapeDtypeStruct((1, LANES), F32)],
        compiler_params=_params(("arbitrary",)),
    )(dqt, dkt, dvt, qkv, cos, sin, gq2, gk2)


A_TQ = 2048
A_TQ_SUB = 256
A_TQ_BWD = 1024
A_KEY_CHUNK = 512


def _a_attn_fwd(qt, k, v, cargo, *, name):
    nh, _, s = qt.shape
    rep = nh // k.shape[0]
    tq = min(A_TQ, s)
    sub = min(A_TQ_SUB, tq)
    grid = (nh, s // tq)

    def body(qt_ref, k_ref, v_ref, o_ref, lse_ref):
        scores = [jnp.dot(k_ref[0], qt_ref[0, :, a:a + sub], preferred_element_type=F32)
                  for a in range(0, tq, sub)]
        for a, st in zip(range(0, tq, sub), scores):
            mx = jnp.max(st, axis=0, keepdims=True)
            p = jnp.exp2(st - mx)
            den = jnp.sum(p, axis=0, keepdims=True)
            ot = lax.dot_general(v_ref[0], p.astype(BF16), (((0,), (0,)), ((), ())), preferred_element_type=F32)
            o_ref[0, :, a:a + sub] = (ot / den).astype(BF16)
            lse_ref[0, :, a:a + sub] = mx + jnp.log(den) * LOG2E

    carried = _carry(cargo, grid, 3, 2, body)
    res = pl.pallas_call(
        carried.body, name=name, grid=grid,
        in_specs=[pl.BlockSpec((1, HEAD_DIM, tq), lambda h, i: (h, 0, i)),
                  pl.BlockSpec((1, s, HEAD_DIM), lambda h, i: (h // rep, 0, 0)),
                  pl.BlockSpec((1, s, HEAD_DIM), lambda h, i: (h // rep, 0, 0))] + carried.in_specs,
        out_specs=[pl.BlockSpec((1, HEAD_DIM, tq), lambda h, i: (h, 0, i)),
                   pl.BlockSpec((1, 1, tq), lambda h, i: (h, 0, i))] + carried.out_specs,
        out_shape=[jax.ShapeDtypeStruct((nh, HEAD_DIM, s), BF16), jax.ShapeDtypeStruct((nh, 1, s), F32)]
        + carried.out_shape,
        scratch_shapes=carried.scratch,
        compiler_params=_params(("arbitrary", "arbitrary")),
    )(qt, k, v, *carried.args)
    return res[0], res[1], res[2:]


def _a_attn_bwd(qt, k, v, dot, ot, lse, cargo, *, name):
    nh, _, s = qt.shape
    nkv = k.shape[0]
    rep = nh // nkv
    tq, ck = min(A_TQ_BWD, s), min(A_KEY_CHUNK, s)
    grid = (nh, s // tq)

    def body(qt_ref, k_ref, v_ref, dot_ref, ot_ref, lse_ref, dq_ref, dk_ref, dv_ref):
        h, i = pl.program_id(0), pl.program_id(1)

        @pl.when((h % rep == 0) & (i == 0))
        def _():
            dk_ref[...] = jnp.zeros_like(dk_ref)
            dv_ref[...] = jnp.zeros_like(dv_ref)

        q_t, do_t, lse_v = qt_ref[0], dot_ref[0], lse_ref[0]
        delta = jnp.sum(do_t.astype(F32) * ot_ref[0].astype(F32), axis=0, keepdims=True)
        nt = (((1,), (1,)), ((), ()))
        dq = jnp.zeros((HEAD_DIM, tq), F32)
        for c in range(s // ck):
            keys = slice(c * ck, (c + 1) * ck)
            kc = k_ref[0, keys, :]
            p = jnp.exp2(jnp.dot(kc, q_t, preferred_element_type=F32) - lse_v)
            dp = jnp.dot(v_ref[0, keys, :], do_t, preferred_element_type=F32)
            ds = (p * (dp - delta)).astype(BF16)
            dv_ref[0, :, keys] += lax.dot_general(do_t, p.astype(BF16), nt, preferred_element_type=F32)
            dk_ref[0, :, keys] += lax.dot_general(q_t, ds, nt, preferred_element_type=F32)
            dq = dq + lax.dot_general(kc, ds, (((0,), (0,)), ((), ())), preferred_element_type=F32)
        dq_ref[0] = dq

    blk_q = pl.BlockSpec((1, HEAD_DIM, tq), lambda h, i: (h, 0, i))
    blk_row = pl.BlockSpec((1, 1, tq), lambda h, i: (h, 0, i))
    blk_kv = pl.BlockSpec((1, s, HEAD_DIM), lambda h, i: (h // rep, 0, 0))
    blk_acc = pl.BlockSpec((1, HEAD_DIM, s), lambda h, i: (h // rep, 0, 0))
    carried = _carry(cargo, grid, 6, 3, body)
    res = pl.pallas_call(
        carried.body, name=name, grid=grid,
        in_specs=[blk_q, blk_kv, blk_kv, blk_q, blk_q, blk_row] + carried.in_specs,
        out_specs=[blk_q, blk_acc, blk_acc] + carried.out_specs,
        out_shape=[jax.ShapeDtypeStruct((nh, HEAD_DIM, s), F32), jax.ShapeDtypeStruct((nkv, HEAD_DIM, s), F32),
                   jax.ShapeDtypeStruct((nkv, HEAD_DIM, s), F32)] + carried.out_shape,
        scratch_shapes=carried.scratch,
        compiler_params=_params(("arbitrary", "arbitrary")),
    )(qt, k, v, dot, ot, lse, *carried.args)
    return res[0], res[1], res[2], res[3:]


WIN_FAR = 1e30


class _Band(NamedTuple):
    window: int
    dil: int
    seg: int
    stride: int

    @property
    def reach(self):
        return -(-self.window // WIN_REACH) * WIN_REACH


def _win_start(i, tq, tk, s, reach):
    return pl.multiple_of(jnp.clip(i * tq - reach, 0, s - tk), LANES)


def _win_penalty(i, start, tq, tk, band):
    qpos = i * tq + lax.broadcasted_iota(jnp.int32, (tk, tq), 1)
    kpos = start + lax.broadcasted_iota(jnp.int32, (tk, tq), 0)
    dist = jnp.abs(kpos - qpos)
    seg_lo = qpos - (qpos & (band.seg - 1))
    valid = (dist <= band.window) & (kpos >= seg_lo) & (kpos < seg_lo + band.seg)
    if band.stride > 1:
        valid &= (dist & (band.stride - 1)) == 0
    return jnp.where(valid, (dist * band.dil).astype(F32), WIN_FAR)


def _win_scores(kw_t, q_t, slope, pen):
    st = lax.dot_general(kw_t, q_t, (((0,), (0,)), ((), ())), preferred_element_type=F32)
    return st * (QK_SCALE * LOG2E) - (slope * LOG2E) * pen


def _win_tq(s):
    return min(512, s)


def _win_fwd(qt, ktp, vtp, slopes, sinks, *, band, out_dtype, name):
    nh, _, s = qt.shape
    nkv = ktp.shape[0]
    rep = nh // nkv
    tq = _win_tq(s)
    tk = min(tq + 2 * band.reach, s)

    def body(*refs):
        qt_ref, kt_ref, vt_ref, sl_ref = refs[:4]
        o_ref, lse_ref, pen_ref = refs[-3:]
        i, kv = pl.program_id(0), pl.program_id(1)
        start = _win_start(i, tq, tk, s, band.reach)

        @pl.when(kv == 0)
        def _():
            pen_ref[...] = _win_penalty(i, start, tq, tk, band)

        win = pl.ds(start, tk)
        kw_t, vw_t, pen = kt_ref[0, :, win], vt_ref[0, :, win], pen_ref[...]
        scores = [_win_scores(kw_t, qt_ref[g], sl_ref[g][:, :1], pen) for g in range(rep)]
        for g, st in enumerate(scores):
            mx = jnp.max(st, axis=0, keepdims=True)
            if sinks is not None:
                sink = refs[4][g][:, :1] * LOG2E
                mx = jnp.maximum(mx, sink)
            p = jnp.exp2(st - mx)
            den = jnp.sum(p, axis=0, keepdims=True)
            if sinks is not None:
                den = den + jnp.exp2(sink - mx)
            ot = jnp.dot(vw_t, p.astype(BF16), preferred_element_type=F32)
            o_ref[g] = (ot / den).astype(o_ref.dtype)
            lse_ref[g] = mx * LN2 + jnp.log(den)

    blk_q = pl.BlockSpec((rep, HEAD_DIM, tq), lambda i, kv: (kv, 0, i))
    blk_kv = pl.BlockSpec((1, HEAD_DIM, s), lambda i, kv: (kv, 0, 0))
    blk_h = pl.BlockSpec((rep, 1, LANES), lambda i, kv: (kv, 0, 0))
    in_specs, args = [blk_q, blk_kv, blk_kv, blk_h], [qt, ktp, vtp, slopes]
    if sinks is not None:
        in_specs.append(blk_h)
        args.append(sinks)
    return pl.pallas_call(
        body, name=name, grid=(s // tq, nkv), in_specs=in_specs,
        out_specs=[blk_q, pl.BlockSpec((rep, 1, tq), lambda i, kv: (kv, 0, i))],
        out_shape=[jax.ShapeDtypeStruct((nh, HEAD_DIM, s), out_dtype), jax.ShapeDtypeStruct((nh, 1, s), F32)],
        scratch_shapes=[pltpu.VMEM((tk, tq), F32)],
        compiler_params=_params(("arbitrary", "arbitrary")),
    )(*args)


def _win_bwd(qt, ktp, vtp, slopes, sinks, dot, ot, delta, *, band, name):
    nh, _, s = qt.shape
    nkv = ktp.shape[0]
    rep = nh // nkv
    tq = _win_tq(s)
    tk = min(tq + 2 * band.reach, s)
    n_in = 6 + (sinks is not None)

    def body(*refs):
        qt_ref, kt_ref, vt_ref, sl_ref, dot_ref, aux_ref = refs[:6]
        outs, pen_ref = refs[n_in:-1], refs[-1]
        dq_ref, dk_ref, dv_ref = outs[:3]
        i, kv = pl.program_id(0), pl.program_id(1)

        @pl.when((i == 0) & (kv == 0))
        def _():
            dk_ref[...] = jnp.zeros_like(dk_ref)
            dv_ref[...] = jnp.zeros_like(dv_ref)
            if sinks is not None:
                outs[3][...] = jnp.zeros_like(outs[3])

        start = _win_start(i, tq, tk, s, band.reach)

        @pl.when(kv == 0)
        def _():
            pen_ref[...] = _win_penalty(i, start, tq, tk, band)

        win = pl.ds(start, tk)
        kw_t, vw_t, pen = kt_ref[0, :, win], vt_ref[0, :, win], pen_ref[...]
        nt = (((1,), (1,)), ((), ()))
        dk_acc = jnp.zeros((HEAD_DIM, tk), F32)
        dv_acc = jnp.zeros((HEAD_DIM, tk), F32)
        products = [(_win_scores(kw_t, qt_ref[g], sl_ref[g][:, :1], pen),
                     lax.dot_general(vw_t, dot_ref[g], (((0,), (0,)), ((), ())), preferred_element_type=F32))
                    for g in range(rep)]
        for g, (st, dp) in enumerate(products):
            q_t, do_t = qt_ref[g], dot_ref[g]
            mx = jnp.max(st, axis=0, keepdims=True)
            if sinks is not None:
                sink = refs[6][g][:, :1] * LOG2E
                mx = jnp.maximum(mx, sink)
            p = jnp.exp2(st - mx)
            den = jnp.sum(p, axis=0, keepdims=True)
            if sinks is not None:
                p_sink = jnp.exp2(sink - mx)
                den = den + p_sink
            p = p / den
            if delta is None:
                row = jnp.sum(do_t.astype(F32) * aux_ref[g].astype(F32), axis=0, keepdims=True)
            else:
                row = aux_ref[g]
            ds = (p * (dp - row) * QK_SCALE).astype(BF16)
            dv_acc = dv_acc + lax.dot_general(do_t, p.astype(BF16), nt, preferred_element_type=F32)
            dk_acc = dk_acc + lax.dot_general(q_t, ds, nt, preferred_element_type=F32)
            dq_ref[g] = jnp.dot(kw_t, ds, preferred_element_type=F32)
            if sinks is not None:
                outs[3][kv * rep + g] += (jnp.zeros((1, LANES), F32)
                                          - jnp.sum(p_sink / den * row, axis=1, keepdims=True))
        dv_ref[kv, :, win] += dv_acc
        dk_ref[kv, :, win] += dk_acc

    blk_q = pl.BlockSpec((rep, HEAD_DIM, tq), lambda i, kv: (kv, 0, i))
    blk_row = pl.BlockSpec((rep, 1, tq), lambda i, kv: (kv, 0, i))
    blk_kv = pl.BlockSpec((1, HEAD_DIM, s), lambda i, kv: (kv, 0, 0))
    blk_acc = pl.BlockSpec((nkv, HEAD_DIM, s), lambda i, kv: (0, 0, 0))
    blk_h = pl.BlockSpec((rep, 1, LANES), lambda i, kv: (kv, 0, 0))
    in_specs = [blk_q, blk_kv, blk_kv, blk_h, blk_q, blk_q if delta is None else blk_row]
    args = [qt, ktp, vtp, slopes, dot, ot if delta is None else delta]
    out_specs = [blk_q, blk_acc, blk_acc]
    out_shape = [jax.ShapeDtypeStruct((nh, HEAD_DIM, s), F32), jax.ShapeDtypeStruct((nkv, HEAD_DIM, s), F32),
                 jax.ShapeDtypeStruct((nkv, HEAD_DIM, s), F32)]
    if sinks is not None:
        in_specs.append(blk_h)
        args.append(sinks)
        out_specs.append(pl.BlockSpec((nh, 1, LANES), lambda i, h: (0, 0, 0)))
        out_shape.append(jax.ShapeDtypeStruct((nh, 1, LANES), F32))
    res = pl.pallas_call(
        body, name=name, grid=(s // tq, nkv), in_specs=in_specs, out_specs=out_specs, out_shape=out_shape,
        scratch_shapes=[pltpu.VMEM((tk, tq), F32)],
        compiler_params=_params(("arbitrary", "arbitrary")),
    )(*args)
    return res if sinks is not None else (*res, None)


def _group_weights(lse):
    e = jnp.exp(lse - jnp.max(lse, axis=0, keepdims=True))
    return e / jnp.sum(e, axis=0, keepdims=True)


def _b_combine_fwd(ot, lse):
    nh, _, s = ot.shape
    ng, hg, _ = lse.shape
    ts = min(512, s)

    def body(ot_ref, lse_ref, o_ref):
        alpha = _group_weights(lse_ref[...])
        for g in range(ng):
            for j in range(hg):
                o_ref[g * hg + j] = (ot_ref[g * hg + j] * alpha[g, j:j + 1, :]).astype(BF16)

    return pl.pallas_call(
        body, name="b_combine_fwd", grid=(s // ts,),
        in_specs=[pl.BlockSpec((nh, HEAD_DIM, ts), lambda i: (0, 0, i)), pl.BlockSpec((ng, hg, ts), lambda i: (0, 0, i))],
        out_specs=pl.BlockSpec((nh, HEAD_DIM, ts), lambda i: (0, 0, i)),
        out_shape=jax.ShapeDtypeStruct((nh, HEAD_DIM, s), BF16),
        compiler_params=_params(("parallel",)),
    )(ot, lse)


def _b_combine_bwd(dout, ot, lse):
    nh, _, s = ot.shape
    ng, hg, _ = lse.shape
    ts = min(512, s)

    def body(dout_ref, ot_ref, lse_ref, do_ref, delta_ref):
        alpha = _group_weights(lse_ref[...])
        for j in range(hg):
            e = [jnp.sum(dout_ref[g * hg + j].astype(F32) * ot_ref[g * hg + j], axis=0, keepdims=True)
                 for g in range(ng)]
            a = [alpha[g, j:j + 1, :] for g in range(ng)]
            mix = a[0] * e[0]
            for g in range(1, ng):
                mix = mix + a[g] * e[g]
            for g in range(ng):
                do_ref[g * hg + j] = (dout_ref[g * hg + j].astype(F32) * a[g]).astype(BF16)
                delta_ref[g * hg + j] = a[g] * mix

    blk = pl.BlockSpec((nh, HEAD_DIM, ts), lambda i: (0, 0, i))
    return pl.pallas_call(
        body, name="b_combine_bwd", grid=(s // ts,),
        in_specs=[blk, blk, pl.BlockSpec((ng, hg, ts), lambda i: (0, 0, i))],
        out_specs=[blk, pl.BlockSpec((nh, 1, ts), lambda i: (0, 0, i))],
        out_shape=[jax.ShapeDtypeStruct((nh, HEAD_DIM, s), BF16), jax.ShapeDtypeStruct((nh, 1, s), F32)],
        compiler_params=_params(("parallel",)),
    )(dout, ot, lse)


def _alibi_slopes(n):
    return 2.0 ** (-8.0 * jnp.arange(1, n + 1, dtype=F32) / n)


def _per_head(v):
    return jnp.broadcast_to(v.astype(F32)[:, None, None], (v.shape[0], 1, LANES))


def _dilate(x, dil):
    if dil == 1:
        return x
    s = x.shape[-1]
    return jnp.swapaxes(x.reshape(x.shape[:-1] + (s // dil, dil)), -1, -2).reshape(x.shape)


def _undilate(x, dil):
    if dil == 1:
        return x
    s = x.shape[-1]
    return jnp.swapaxes(x.reshape(x.shape[:-1] + (dil, s // dil)), -1, -2).reshape(x.shape)


def _heads(x_t):
    return x_t.reshape(-1, HEAD_DIM, x_t.shape[-1])


B_MAX_STRIDE = 4


def _b_band(window, dilation, s):
    if dilation <= B_MAX_STRIDE:
        return _Band(window // 2, 1, s, dilation), 1
    return _Band(window // 2 // dilation, dilation, s // dilation, 1), dilation


def _mixer_fwd(kind, qkv, p, tabs, cargo, layer):
    s = qkv.shape[0 if kind == 0 else 1]
    if kind == 0:
        qt, k, v = _a_prep(qkv, tabs[0], tabs[1], p["gq2"], p["gk2"])
        ot, lse, brought = _a_attn_fwd(qt, k, v, cargo, name=f"a_attn_fwd_l{layer}")
        return ot.reshape(-1, s), dict(qt=qt, k=k, v=v, ot=ot, lse=lse), brought
    assert cargo is None
    if kind == 2:
        nq, nk = C_HEADS * HEAD_DIM, C_KV * HEAD_DIM
        qt = _heads(qkv[:nq])
        kp, vp = _heads(qkv[nq:nq + nk]), _heads(qkv[nq + nk:])
        ot, _ = _win_fwd(qt, kp, vp, p["slopes"], p["sinks"], band=_Band(C_WINDOW, 1, s, 1), out_dtype=BF16,
                         name="c_attn_fwd")
        return ot.reshape(-1, s), dict(qt=qt, kp=kp, vp=vp, ot=ot), ()
    ng, hg, kg = len(B_GROUPS), B_HEADS_PER_GROUP, B_KV_PER_GROUP
    nq, nk = ng * hg * HEAD_DIM, ng * kg * HEAD_DIM
    qt_all, kt_all, vt_all = _heads(qkv[:nq]), _heads(qkv[nq:nq + nk]), _heads(qkv[nq + nk:])
    saved, outs, lses = [], [], []
    for g, (window, dilation) in enumerate(B_GROUPS):
        band, dil = _b_band(window, dilation, s)
        qt = _dilate(qt_all[g * hg:(g + 1) * hg], dil)
        kp = _dilate(kt_all[g * kg:(g + 1) * kg], dil)
        vp = _dilate(vt_all[g * kg:(g + 1) * kg], dil)
        sl = p["slopes"][g * hg:(g + 1) * hg]
        ot, lse = _win_fwd(qt, kp, vp, sl, None, band=band, out_dtype=F32, name=f"b_attn_fwd_g{g}")
        saved.append(dict(qt=qt, kp=kp, vp=vp))
        outs.append(_undilate(ot, dil))
        lses.append(_undilate(lse[:, 0, :], dil))
    ot_all, lse_all = jnp.concatenate(outs, axis=0), jnp.stack(lses, axis=0)
    mixed = _b_combine_fwd(ot_all, lse_all)
    return mixed.reshape(-1, s), dict(groups=saved, ot=ot_all, lse=lse_all), ()


def _mixer_bwd(kind, do_t, qkv, sv, p, tabs, cargo, layer):
    s = do_t.shape[1]
    do_heads = _heads(do_t)
    small = {}
    if kind == 0:
        dqt, dkt, dvt, brought = _a_attn_bwd(sv["qt"], sv["k"], sv["v"], do_heads, sv["ot"], sv["lse"],
                                             cargo, name=f"a_attn_bwd_l{layer}")
        dqkv, dgq, dgk = _a_prep_bwd(dqt, dkt, dvt, qkv, tabs[0], tabs[1], p["gq2"], p["gk2"])
        small["q_gain"] = dgq[0, :HEAD_DIM] + dgq[0, HEAD_DIM:]
        small["k_gain"] = dgk[0, :HEAD_DIM] + dgk[0, HEAD_DIM:]
        return dqkv, small, brought
    assert cargo is None
    if kind == 2:
        dqt, dkt, dvt, dsink = _win_bwd(sv["qt"], sv["kp"], sv["vp"], p["slopes"], p["sinks"], do_heads,
                                        sv["ot"], None, band=_Band(C_WINDOW, 1, s, 1), name="c_attn_bwd")
        small["sinks"] = dsink[:, 0, 0]
        parts = [dqt.reshape(-1, s), dkt.reshape(-1, s), dvt.reshape(-1, s)]
        return jnp.concatenate(parts, axis=0).astype(BF16), small, ()
    ng, hg, kg = len(B_GROUPS), B_HEADS_PER_GROUP, B_KV_PER_GROUP
    do_own, delta = _b_combine_bwd(do_heads, sv["ot"], sv["lse"])
    dqs, dks, dvs = [], [], []
    for g, (window, dilation) in enumerate(B_GROUPS):
        band, dil = _b_band(window, dilation, s)
        gs = sv["groups"][g]
        dqt, dkt, dvt, _ = _win_bwd(gs["qt"], gs["kp"], gs["vp"], p["slopes"][g * hg:(g + 1) * hg], None,
                                    _dilate(do_own[g * hg:(g + 1) * hg], dil), None,
                                    _dilate(delta[g * hg:(g + 1) * hg], dil), band=band, name=f"b_attn_bwd_g{g}")
        dqs.append(_undilate(dqt, dil))
        dks.append(_undilate(dkt, dil))
        dvs.append(_undilate(dvt, dil))
    parts = [x.reshape(-1, s) for x in dqs + dks + dvs]
    return jnp.concatenate(parts, axis=0).astype(BF16), small, ()


LAYER_MATS = ("w_qkv", "w_o", "w1", "w2")
COLUMN_QUARTERS = ("w_qkv", "w1")


def _whole(key, gathered):
    q, r, c = gathered.shape
    if key == "w1":
        return gathered
    if key in COLUMN_QUARTERS:
        return jnp.transpose(gathered, (1, 0, 2)).reshape(r, q * c)
    return gathered.reshape(q * r, c)


def _quarters(key, g):
    r, c = g.shape
    if key in COLUMN_QUARTERS:
        return jnp.transpose(g.reshape(r, 4, c // 4), (1, 0, 2))
    return g.reshape(4, r // 4, c)


def _local_step(x, target, norms, mixer_params, shards, whole=None):
    s = x.shape[0]
    tabs = _rope_tables(s)
    if whole is None:
        assert MIXER_OF_LAYER[0][0] == 0
        first = _run_cargo(_gather_cargo([shards[0][key] for key in LAYER_MATS]), name="gather_l0")
        mats = {0: {key: _whole(key, g) for key, g in zip(LAYER_MATS, first)}}
        later = _gather_cargo([shards[layer][key] for layer in range(1, DEPTH) for key in LAYER_MATS])
    else:
        mats, later = dict(enumerate(whole)), None
    h = x
    saved = []
    for layer in range(DEPTH):
        kind = layer % N_MIXERS
        w, p = mats[layer], mixer_params[layer]
        hn, qkv = _norm_mm(h, norms["attn"][layer][None], w["w_qkv"], out_dtype=F32 if kind == 0 else BF16,
                           relu2=False, transpose_out=kind != 0, name=f"qkv_proj_l{layer}")
        o_t, sv, brought = _mixer_fwd(kind, qkv, p, tabs, later if layer == 0 else None, layer)
        for n, g in enumerate(brought):
            mats.setdefault(1 + n // len(LAYER_MATS), {})[LAYER_MATS[n % len(LAYER_MATS)]] = _whole(
                LAYER_MATS[n % len(LAYER_MATS)], g)
        h_mid = _mm_res(o_t, w["w_o"], h, a_transposed=True, name=f"o_proj_l{layer}")
        hn2, act = _norm_mm(h_mid, norms["mlp"][layer][None], w["w1"], out_dtype=BF16, relu2=True,
                            transpose_out=False, name=f"mlp_up_l{layer}")
        h_out = _mm_res(act, w["w2"], h_mid, a_transposed=False, name=f"mlp_down_l{layer}")
        saved.append(dict(h=h, hn=hn, qkv=qkv, o_t=o_t, mix=sv, h_mid=h_mid, hn2=hn2, act=act))
        h = h_out

    dh, loss, d_final = _loss_head(h, norms["final"][None], target)

    own, received, pending = {}, {}, []
    d_attn, d_mlp, small = [None] * DEPTH, [None] * DEPTH, [None] * DEPTH
    for layer in reversed(range(DEPTH)):
        kind = layer % N_MIXERS
        w, p, sv = mats[layer], mixer_params[layer], saved[layer]
        du = _mm_nt(dh, w["w2"], sv["act"], transpose_out=False, name=f"mlp_down_bwd_l{layer}")
        own[layer, "w2"] = _quarters("w2", _mm_tn(sv["act"], dh, x_transposed=False, g_transposed=False,
                                                  column_quarters=False, name=f"mlp_w2_grad_l{layer}"))
        own[layer, "w1"] = _mm_tn(sv["hn2"], du, x_transposed=False, g_transposed=False, column_quarters=True,
                                  name=f"mlp_w1_grad_l{layer}")
        dh_mid, d_mlp[layer] = _mm_nt_normbwd(du, w["w1"], sv["h_mid"], norms["mlp"][layer][None], dh,
                                              g_transposed=False, name=f"mlp_up_bwd_l{layer}")
        do_t = _mm_nt(dh_mid, w["w_o"], None, transpose_out=True, name=f"o_proj_bwd_l{layer}")
        own[layer, "w_o"] = _quarters("w_o", _mm_tn(sv["o_t"], dh_mid, x_transposed=True, g_transposed=False,
                                                    column_quarters=False, name=f"w_o_grad_l{layer}"))
        pending += [(layer, "w2"), (layer, "w1"), (layer, "w_o")]
        cargo = None
        if kind == 0 and whole is None:
            cargo, sent, pending = _scatter_cargo([own[item] for item in pending], None), pending, []
        dqkv, small[layer], brought = _mixer_bwd(kind, do_t, sv["qkv"], sv["mix"], p, tabs, cargo, layer)
        if cargo is not None:
            received.update(zip(sent, brought))
        own[layer, "w_qkv"] = _quarters("w_qkv", _mm_tn(sv["hn"], dqkv, x_transposed=False, g_transposed=kind != 0,
                                                        column_quarters=False, name=f"w_qkv_grad_l{layer}"))
        pending.append((layer, "w_qkv"))
        dh, d_attn[layer] = _mm_nt_normbwd(dqkv, w["w_qkv"], sv["h"], norms["attn"][layer][None], dh_mid,
                                           g_transposed=kind != 0, name=f"qkv_proj_bwd_l{layer}")
    return loss, dh, own, received, pending, dict(attn=d_attn, mlp=d_mlp, final=d_final, mixer=small)


CHIP_FLIPS = ((1, 0), (0, 1), (1, 1))


class _Cargo(NamedTuple):
    ins: tuple
    out_shape: tuple
    sem_shapes: tuple
    start: Callable
    wait: Callable


class _Carried(NamedTuple):
    body: Callable
    in_specs: list
    out_specs: list
    out_shape: list
    scratch: list
    args: tuple


def _carry(cargo, grid, n_in, n_out, body):
    if cargo is None:
        return _Carried(body, [], [], [], [], ())
    ci, co = len(cargo.ins), len(cargo.out_shape)

    def wrapped(*refs):
        ins, c_ins = refs[:n_in], refs[n_in:n_in + ci]
        outs, c_outs = refs[n_in + ci:n_in + ci + n_out], refs[n_in + ci + n_out:n_in + ci + n_out + co]
        sems = refs[n_in + ci + n_out + co:]
        first = last = None
        for axis, extent in enumerate(grid):
            at = pl.program_id(axis)
            first = (at == 0) if first is None else first & (at == 0)
            last = (at == extent - 1) if last is None else last & (at == extent - 1)

        @pl.when(first)
        def _():
            cargo.start(c_ins, c_outs, sems)

        body(*ins, *outs)

        @pl.when(last)
        def _():
            cargo.wait(c_ins, c_outs, sems)

    return _Carried(wrapped, [ANY] * ci, [ANY] * co, list(cargo.out_shape), list(cargo.sem_shapes), tuple(cargo.ins))


def _run_cargo(cargo, *, name):
    ci, co = len(cargo.ins), len(cargo.out_shape)

    def body(*refs):
        cargo.start(refs[:ci], refs[ci:ci + co], refs[ci + co:])
        cargo.wait(refs[:ci], refs[ci:ci + co], refs[ci + co:])

    return pl.pallas_call(body, name=name, in_specs=[ANY] * ci, out_specs=[ANY] * co, out_shape=list(cargo.out_shape),
                          scratch_shapes=list(cargo.sem_shapes))(*cargo.ins)


def _other_chip(x, y, j):
    fx, fy = CHIP_FLIPS[j]
    return (1 - x if fx else x), (1 - y if fy else y)


def _gather_cargo(shards):
    n = len(shards)
    halves = [a.shape[0] // 2 for a in shards]

    def copies(ins, outs, sems):
        ici_send, ici_recv, d2d_send, d2d_recv, local_sems = sems
        x, y, c = lax.axis_index("x"), lax.axis_index("y"), lax.axis_index("c")
        me = 2 * x + y

        def half(t, which):
            return pl.ds(pl.multiple_of(which * halves[t], 16), halves[t])

        def over_ici(t, j, arriving):
            px, py = _other_chip(x, y, j)
            return pltpu.make_async_remote_copy(
                src_ref=ins[t].at[half(t, c)], dst_ref=outs[t].at[2 * px + py if arriving else me, half(t, c)],
                send_sem=ici_send.at[t, j], recv_sem=ici_recv.at[t, j], device_id=(px, py, c), device_id_type=MESH)

        def over_d2d(t, j, arriving):
            px, py = _other_chip(x, y, j)
            mine = outs[t].at[2 * px + py, half(t, c)]
            return pltpu.make_async_remote_copy(
                src_ref=mine, dst_ref=outs[t].at[2 * px + py, half(t, 1 - c)] if arriving else mine,
                send_sem=d2d_send.at[t, j], recv_sem=d2d_recv.at[t, j], device_id=(x, y, 1 - c), device_id_type=MESH)

        return over_ici, over_d2d, lambda t: pltpu.make_async_copy(ins[t], outs[t].at[me], local_sems.at[t])

    def start(ins, outs, sems):
        over_ici, _, own = copies(ins, outs, sems)
        for t in range(n):
            own(t).start()
            for j in range(len(CHIP_FLIPS)):
                over_ici(t, j, False).start()

    def wait(ins, outs, sems):
        over_ici, over_d2d, own = copies(ins, outs, sems)
        for t in range(n):
            for j in range(len(CHIP_FLIPS)):
                over_ici(t, j, True).wait_recv()
                over_d2d(t, j, False).start()
        for t in range(n):
            for j in range(len(CHIP_FLIPS)):
                over_d2d(t, j, True).wait_recv()
                over_d2d(t, j, False).wait_send()
                over_ici(t, j, False).wait_send()
            own(t).wait()

    dma = pltpu.SemaphoreType.DMA
    return _Cargo(tuple(shards), tuple(jax.ShapeDtypeStruct((4,) + a.shape, a.dtype) for a in shards),
                  (dma((n, 3)), dma((n, 3)), dma((n, 3)), dma((n, 3)), dma((n,))), start, wait)


def _scatter_cargo(grads, small):
    n = len(grads)

    def copies(ins, outs, sems):
        x, y, c = lax.axis_index("x"), lax.axis_index("y"), lax.axis_index("c")
        me = 4 * x + 2 * y + c

        def remote(t, j):
            px, py = _other_chip(x, y, j)
            return pltpu.make_async_remote_copy(
                src_ref=ins[t].at[2 * px + py], dst_ref=outs[t].at[j], send_sem=sems[0].at[t, j],
                recv_sem=sems[1].at[t, j], device_id=(px, py, c), device_id_type=MESH)

        def small_remote(r, arriving):
            fx, fy, fc = (r + 1) // 4, ((r + 1) // 2) % 2, (r + 1) % 2
            px, py, pc = (1 - x if fx else x), (1 - y if fy else y), (1 - c if fc else c)
            return pltpu.make_async_remote_copy(
                src_ref=ins[n], dst_ref=outs[n].at[4 * px + 2 * py + pc if arriving else me],
                send_sem=sems[2].at[r], recv_sem=sems[3].at[r], device_id=(px, py, pc), device_id_type=MESH)

        return remote, small_remote, lambda: pltpu.make_async_copy(ins[n], outs[n].at[me], sems[4])

    def start(ins, outs, sems):
        remote, small_remote, small_own = copies(ins, outs, sems)
        if small is not None:
            small_own().start()
            for r in range(7):
                small_remote(r, False).start()
        for t in range(n):
            for j in range(len(CHIP_FLIPS)):
                remote(t, j).start()

    def wait(ins, outs, sems):
        remote, small_remote, small_own = copies(ins, outs, sems)
        if small is not None:
            for r in range(7):
                small_remote(r, True).wait_recv()
                small_remote(r, False).wait_send()
            small_own().wait()
        for t in range(n):
            for j in range(len(CHIP_FLIPS)):
                remote(t, j).wait()

    dma = pltpu.SemaphoreType.DMA
    ins = tuple(grads) + (() if small is None else (small,))
    out_shape = tuple(jax.ShapeDtypeStruct((3,) + g.shape[1:], g.dtype) for g in grads)
    sem_shapes = (dma((n, 3)), dma((n, 3)))
    if small is not None:
        out_shape += (jax.ShapeDtypeStruct((8,) + small.shape, small.dtype),)
        sem_shapes += (dma((7,)), dma((7,)), dma(()))
    return _Cargo(ins, out_shape, sem_shapes, start, wait)


def _swap_cores(parts):
    n = len(parts)

    def body(*refs):
        ins, outs = refs[:n], refs[n:2 * n]
        send_sems, recv_sems = refs[2 * n:]
        peer = (lax.axis_index("x"), lax.axis_index("y"), 1 - lax.axis_index("c"))
        copies = [pltpu.make_async_remote_copy(src_ref=ins[t], dst_ref=outs[t], send_sem=send_sems.at[t],
                                               recv_sem=recv_sems.at[t], device_id=peer, device_id_type=MESH)
                  for t in range(n)]
        for cp in copies:
            cp.start()
        for cp in copies:
            cp.wait()

    return pl.pallas_call(
        body, name="swap_cores", in_specs=[ANY] * n, out_specs=[ANY] * n,
        out_shape=[jax.ShapeDtypeStruct(a.shape, a.dtype) for a in parts],
        scratch_shapes=[pltpu.SemaphoreType.DMA((n,)), pltpu.SemaphoreType.DMA((n,))],
    )(*parts)


def _rows_tile(r):
    return 256 if r % 256 == 0 else r


def _sum_quarters(own, recv, *, name):
    r, c = own.shape
    tr = _rows_tile(r)

    def body(own_ref, recv_ref, o_ref):
        acc = own_ref[...].astype(F32)
        for j in range(3):
            acc = acc + recv_ref[j].astype(F32)
        o_ref[...] = acc.astype(BF16)

    return pl.pallas_call(
        body, name=name, grid=(r // tr,),
        in_specs=[pl.BlockSpec((tr, c), lambda i: (i, 0)), pl.BlockSpec((3, tr, c), lambda i: (0, i, 0))],
        out_specs=pl.BlockSpec((tr, c), lambda i: (i, 0)),
        out_shape=jax.ShapeDtypeStruct((r, c), BF16),
        compiler_params=_params(("parallel",)),
    )(own, recv)


def _adamw(w, m, v, parts, *, name):
    r, c = w.shape
    tr = _rows_tile(r)
    c1, c2 = 1.0 - ADAM_B1 ** ADAM_STEP, 1.0 - ADAM_B2 ** ADAM_STEP
    n_parts = len(parts)

    def body(*refs):
        w_ref, m_ref, v_ref = refs[:3]
        g_ref, d_ref, nm_ref, nv_ref = refs[3 + n_parts:]
        terms = []
        for p_ref in refs[3:3 + n_parts]:
            terms += [p_ref[...]] if len(p_ref.shape) == 2 else [p_ref[j] for j in range(p_ref.shape[0])]
        g = terms[0].astype(F32)
        for term in terms[1:]:
            g = g + term.astype(F32)
        m_new = ADAM_B1 * m_ref[...] + (1.0 - ADAM_B1) * g
        v_new = ADAM_B2 * v_ref[...] + (1.0 - ADAM_B2) * (g * g)
        step = (m_new / c1) / (jnp.sqrt(v_new / c2) + ADAM_EPS)
        g_ref[...] = g
        d_ref[...] = -ADAM_LR * (step + ADAM_WD * w_ref[...])
        nm_ref[...] = m_new
        nv_ref[...] = v_new

    blk = pl.BlockSpec((tr, c), lambda i: (i, 0))
    part_specs = [blk if p.ndim == 2 else pl.BlockSpec((p.shape[0], tr, c), lambda i: (0, i, 0)) for p in parts]
    return pl.pallas_call(
        body, name=name, grid=(r // tr,), in_specs=[blk, blk, blk] + part_specs,
        out_specs=[blk] * 4, out_shape=[jax.ShapeDtypeStruct((r, c), F32)] * 4,
        compiler_params=_params(("parallel",)),
    )(w, m, v, *parts)


MATS = ("a_w_qkv", "a_w_o", "b_w_qkv", "b_w_o", "c_w_qkv", "c_w_o", "mlp_w1", "mlp_w2")
SMALLS = ("attn_norm", "mlp_norm", "a_q_gain", "a_k_gain", "c_sinks", "final_norm")
WEIGHTS = ("attn_norm", "mlp_norm", "a_w_qkv", "a_q_gain", "a_k_gain", "a_w_o", "b_w_qkv", "b_w_o", "c_w_qkv",
           "c_sinks", "c_w_o", "mlp_w1", "mlp_w2", "final_norm")
MIXER_OF_LAYER = tuple((layer % N_MIXERS, sum(1 for q in range(layer) if q % N_MIXERS == layer % N_MIXERS))
                       for layer in range(DEPTH))
SMALL_ROWS = 8


def _pack_small(values):
    rows, spans, at = [], [], 0
    for v in values:
        flat = v.reshape(-1)
        n = -(-flat.shape[0] // (SMALL_ROWS * LANES)) * SMALL_ROWS
        rows.append(jnp.pad(flat, (0, n * LANES - flat.shape[0])).reshape(n, LANES))
        spans.append((at, n))
        at += n
    return jnp.concatenate(rows, axis=0), spans


def kernel(x, attn_norm, mlp_norm, a_w_qkv, a_q_gain, a_k_gain, a_w_o, b_w_qkv, b_w_o, c_w_qkv, c_sinks, c_w_o, mlp_w1, mlp_w2, final_norm, loss_target, m_attn_norm, m_mlp_norm, m_a_w_qkv, m_a_q_gain, m_a_k_gain, m_a_w_o, m_b_w_qkv, m_b_w_o, m_c_w_qkv, m_c_sinks, m_c_w_o, m_mlp_w1, m_mlp_w2, m_final_norm, v_attn_norm, v_mlp_norm, v_a_w_qkv, v_a_q_gain, v_a_k_gain, v_a_w_o, v_b_w_qkv, v_b_w_o, v_c_w_qkv, v_c_sinks, v_c_w_o, v_mlp_w1, v_mlp_w2, v_final_norm):
    env = dict(locals())
    w = {name: env[name] for name in WEIGHTS}
    mom = {name: (env["m_" + name], env["v_" + name]) for name in WEIGHTS}

    prefix = ("a", "b", "c")
    shards, mixer_params = [], []
    for layer, (kind, j) in enumerate(MIXER_OF_LAYER):
        shards.append(dict(w_qkv=w[prefix[kind] + "_w_qkv"][j].astype(BF16), w_o=w[prefix[kind] + "_w_o"][j].astype(BF16),
                           w1=mlp_w1[layer].astype(BF16), w2=mlp_w2[layer].astype(BF16)))
        if kind == 0:
            mixer_params.append(dict(gq2=jnp.tile(a_q_gain[j], 2)[None], gk2=jnp.tile(a_k_gain[j], 2)[None]))
        elif kind == 1:
            mixer_params.append(dict(slopes=_per_head(_alibi_slopes(len(B_GROUPS) * B_HEADS_PER_GROUP))))
        else:
            mixer_params.append(dict(slopes=_per_head(_alibi_slopes(C_HEADS)), sinks=_per_head(c_sinks[j])))

    norms = dict(attn=attn_norm, mlp=mlp_norm, final=final_norm)
    loss_part, grad_x, own, received, pending, g_small = _local_step(x[0], loss_target[0], norms, mixer_params, shards)
    loss = lax.psum(loss_part[0, 0], ("x", "y", "c"))

    of_kind = lambda kind, key: jnp.stack([g_small["mixer"][layer][key] for layer, (k, _) in enumerate(MIXER_OF_LAYER)
                                           if k == kind])
    small_grads = dict(
        attn_norm=jnp.concatenate(g_small["attn"], axis=0), mlp_norm=jnp.concatenate(g_small["mlp"], axis=0),
        a_q_gain=of_kind(0, "q_gain"), a_k_gain=of_kind(0, "k_gain"), c_sinks=of_kind(2, "sinks"),
        final_norm=g_small["final"][0])
    packed, spans = _pack_small([small_grads[name] for name in SMALLS])
    *last, all_small = _run_cargo(_scatter_cargo([own[item] for item in pending], packed), name="scatter_last")
    received.update(zip(pending, last))

    me_chip = 2 * lax.axis_index("x") + lax.axis_index("y")
    partial = []
    for name in MATS:
        key = name[2:] if name[0] in "abc" else name[4:]
        layers = [layer for layer, (kind, _) in enumerate(MIXER_OF_LAYER)
                  if name.startswith("mlp") or prefix[kind] == name[0]]
        sums = [_sum_quarters(lax.dynamic_index_in_dim(own[layer, key], me_chip, axis=0, keepdims=False),
                              received[layer, key], name=f"sum_{name}_l{layer}") for layer in layers]
        partial.append(jnp.concatenate(sums, axis=0))
    other = _swap_cores(partial)

    out = {}
    for name, mine, theirs in zip(MATS, partial, other):
        shape = w[name].shape
        res = _adamw(*[a.reshape(-1, shape[-1]) for a in (w[name], *mom[name])], [mine, theirs], name=f"adamw_{name}")
        out[name] = [a.reshape(shape) for a in res]
    for name, (at, n) in zip(SMALLS, spans):
        shape = w[name].shape
        packed_in = [_pack_small([a])[0] for a in (w[name], *mom[name])]
        res = _adamw(*packed_in, [all_small[:, at:at + n]], name=f"adamw_{name}")
        out[name] = [a.reshape(-1)[:w[name].size].reshape(shape) for a in res]

    return (loss, grad_x[None], *[out[name][0] for name in WEIGHTS], *[out[name][1] for name in WEIGHTS],
            *[out[name][2] for name in WEIGHTS], *[out[name][3] for name in WEIGHTS])
```
